```python
import math
import jax, jax.numpy as jnp
from jax import lax
import numpy as np

D_MODEL = 1024
BATCH = 16
SEQ = 2048
DEPTH = 2

GRID_W = 64
CTX_LEN = 256
N_MIXERS = 2
N_MOD = 6
NORM_EPS = 1e-6
MLA_HEADS = 8
Q_LORA = 384
KV_LORA = 256
NOPE_DIM = 128
ROPE_DIM = 64
V_DIM = 128
ROPE_BASE = 10000.0
Q_BLOCK = 128
S5_GROUP = 16
S5_GROUPS = D_MODEL // S5_GROUP
S5_STATE = 64
S5_DT_MIN = 1e-3
S5_DT_MAX = 1e-1
N_EXPERTS = 64
TOP_K = 8
N_EXPERT_GROUPS = 8
TOPK_GROUPS = 4
D_EXPERT = 256
D_SHARED = 256
ROUTED_SCALE = 2.5
MOE_BLOCK = 128

kernel_name = "hybrid_mla_s5_moe_diffusion_trunk"


def rms_norm(x, g):
    xf = x.astype(jnp.float32)
    y = xf * lax.rsqrt(jnp.mean(xf * xf, axis=-1, keepdims=True) + NORM_EPS)
    return (y * g.astype(jnp.float32)).astype(x.dtype)


def modulate(h, shift, scale):
    return h * (1 + scale) + shift


def axial_rope_tables(n_tokens):
    rows = n_tokens // GRID_W
    row = jnp.repeat(jnp.arange(rows), GRID_W).astype(jnp.float32)
    col = jnp.tile(jnp.arange(GRID_W), rows).astype(jnp.float32)
    n_freq = ROPE_DIM // 4
    inv_freq = ROPE_BASE ** (-jnp.arange(n_freq, dtype=jnp.float32) / n_freq)
    ang = jnp.concatenate([row[:, None] * inv_freq, col[:, None] * inv_freq], axis=-1)
    return jnp.cos(ang), jnp.sin(ang)


def apply_rope(x, cos, sin):
    xf = x.astype(jnp.float32).reshape(*x.shape[:-1], ROPE_DIM // 2, 2)
    x0, x1 = xf[..., 0], xf[..., 1]
    out = jnp.stack([x0 * cos - x1 * sin, x0 * sin + x1 * cos], axis=-1)
    return out.reshape(x.shape).astype(x.dtype)


def mla_attend(q_nope, q_pe, k_nope, k_pe, v):
    scale = (NOPE_DIM + ROPE_DIM) ** -0.5
    s = (jnp.einsum('bqhd,bkhd->bhqk', q_nope, k_nope, preferred_element_type=jnp.float32)
         + jnp.einsum('bqhr,bkr->bhqk', q_pe, k_pe, preferred_element_type=jnp.float32)) * scale
    p = jax.nn.softmax(s, axis=-1).astype(v.dtype)
    return jnp.einsum('bhqk,bkhv->bqhv', p, v)


def blocked_attend(q_nope, q_pe, k_nope, k_pe, v):
    B, L, H, _ = q_nope.shape
    nb = L // Q_BLOCK

    def to_blocks(t):
        return jnp.moveaxis(t.reshape(B, nb, Q_BLOCK, *t.shape[2:]), 1, 0)

    out = lax.map(lambda qs: mla_attend(qs[0], qs[1], k_nope, k_pe, v),
                  (to_blocks(q_nope), to_blocks(q_pe)))
    return jnp.moveaxis(out, 0, 1).reshape(B, L, H, V_DIM)


def mla_mixer(h_ctx, h_lat, cos, sin, w_dqkv, g_q, g_kv, w_uq, w_ukv, w_o, need_ctx):
    B, L, _ = h_lat.shape
    n_ctx = h_ctx.shape[1]
    n_all = n_ctx + L
    h = jnp.concatenate([h_ctx, h_lat], axis=1)
    a = h @ w_dqkv
    c_q, c_kv, k_pe = jnp.split(a, [Q_LORA, Q_LORA + KV_LORA], axis=-1)
    q = (rms_norm(c_q, g_q) @ w_uq).reshape(B, n_all, MLA_HEADS, NOPE_DIM + ROPE_DIM)
    kv = (rms_norm(c_kv, g_kv) @ w_ukv).reshape(B, n_all, MLA_HEADS, NOPE_DIM + V_DIM)
    q_nope, q_pe = jnp.split(q, [NOPE_DIM], axis=-1)
    k_nope, v = jnp.split(kv, [NOPE_DIM], axis=-1)
    q_pe = apply_rope(q_pe, cos[:, None, :], sin[:, None, :])
    k_pe = apply_rope(k_pe, cos, sin)
    o_lat = blocked_attend(q_nope[:, n_ctx:], q_pe[:, n_ctx:], k_nope, k_pe, v)
    o_lat = o_lat.reshape(B, L, MLA_HEADS * V_DIM) @ w_o
    if not need_ctx:
        return o_lat, None
    o_ctx = mla_attend(q_nope[:, :n_ctx], q_pe[:, :n_ctx], k_nope[:, :n_ctx], k_pe[:, :n_ctx], v[:, :n_ctx])
    o_ctx = o_ctx.reshape(B, n_ctx, MLA_HEADS * V_DIM) @ w_o
    return o_lat, o_ctx


def _ssm_combine(e1, e2):
    a1, b1 = e1
    a2, b2 = e2
    return a1 * a2, a2 * b1 + b2


def diag_scan(bu, lam_bar, reverse):
    a = jnp.broadcast_to(lam_bar, (1, bu.shape[1]) + lam_bar.shape)
    return lax.associative_scan(_ssm_combine, (a, bu), axis=1, reverse=reverse)[1]


def s5_glu(y, w_glu, b_glu):
    z = jax.nn.gelu(y).astype(w_glu.dtype) @ w_glu + b_glu
    z1, z2 = jnp.split(z, 2, axis=-1)
    return z1 * jax.nn.sigmoid(z2)


def s5_mixer(h_ctx, h_lat, lam_re, lam_im, log_step, b_re, b_im, c_re, c_im, d_skip, w_glu, b_glu, need_ctx):
    B, L, D = h_lat.shape
    n_ctx = h_ctx.shape[1]
    f32 = jnp.float32
    u_ctx = h_ctx.astype(f32).reshape(B, n_ctx, S5_GROUPS, S5_GROUP).astype(jnp.complex64)
    u_lat = h_lat.astype(f32).reshape(B, L, S5_GROUPS, S5_GROUP).astype(jnp.complex64)
    dsk = d_skip.astype(f32)
    y_lat = h_lat.astype(f32) * dsk
    y_ctx = h_ctx.astype(f32) * dsk if need_ctx else None
    for direction in range(2):
        reverse = direction == 1
        lam = lax.complex(lam_re[direction].astype(f32), lam_im[direction].astype(f32))
        step = jnp.exp(log_step[direction].astype(f32))[:, None]
        lam_bar = jnp.exp(lam * step)
        b_bar = ((lam_bar - 1.0) / lam)[:, :, None] * lax.complex(
            b_re[direction].astype(f32), b_im[direction].astype(f32))
        c_mat = lax.complex(c_re[direction].astype(f32), c_im[direction].astype(f32))
        s_ctx = diag_scan(jnp.einsum('blgi,gpi->blgp', u_ctx, b_bar), lam_bar, reverse)
        h0 = s_ctx[:, 0] if reverse else s_ctx[:, -1]
        bu_lat = jnp.einsum('blgi,gpi->blgp', u_lat, b_bar)
        bu_lat = bu_lat.at[:, -1 if reverse else 0].add(lam_bar * h0)
        s_lat = diag_scan(bu_lat, lam_bar, reverse)
        y_lat = y_lat + jnp.einsum('blgp,gip->blgi', s_lat, c_mat).real.reshape(B, L, D)
        if need_ctx:
            y_ctx = y_ctx + jnp.einsum('blgp,gip->blgi', s_ctx, c_mat).real.reshape(B, n_ctx, D)
    o_lat = s5_glu(y_lat, w_glu, b_glu).astype(h_lat.dtype)
    o_ctx = s5_glu(y_ctx, w_glu, b_glu).astype(h_ctx.dtype) if need_ctx else None
    return o_lat, o_ctx


def swiglu(x, w_gate, w_up, w_down):
    return (jax.nn.silu(x @ w_gate) * (x @ w_up)) @ w_down


def moe_ffn(tok, w_router, router_bias, w_gate, w_up, w_down, sh_gate, sh_up, sh_down):
    T, D = tok.shape
    scores = jax.nn.sigmoid(tok.astype(jnp.float32) @ w_router.astype(jnp.float32))
    biased = scores + router_bias.astype(jnp.float32)
    grp = biased.reshape(T, N_EXPERT_GROUPS, N_EXPERTS // N_EXPERT_GROUPS)
    grp_score = lax.top_k(grp, 2)[0].sum(-1)
    _, gidx = lax.top_k(grp_score, TOPK_GROUPS)
    gmask = jax.nn.one_hot(gidx, N_EXPERT_GROUPS, dtype=jnp.bool_).any(axis=1)
    emask = jnp.repeat(gmask, N_EXPERTS // N_EXPERT_GROUPS, axis=1)
    _, eidx = lax.top_k(jnp.where(emask, biased, -jnp.inf), TOP_K)
    gates = jnp.take_along_axis(scores, eidx, axis=1)
    gates = gates / jnp.sum(gates, axis=-1, keepdims=True) * ROUTED_SCALE

    A = T * TOP_K
    flat_e = eidx.reshape(-1)
    flat_tok = jnp.repeat(jnp.arange(T), TOP_K)
    flat_w = gates.reshape(-1)
    order = jnp.argsort(flat_e)
    se, stok, sw = flat_e[order], flat_tok[order], flat_w[order]
    counts = jnp.bincount(flat_e, length=N_EXPERTS)
    padded = (counts + MOE_BLOCK - 1) // MOE_BLOCK * MOE_BLOCK
    pad_end = jnp.cumsum(padded)
    pad_start = pad_end - padded
    start = jnp.cumsum(counts) - counts
    dest = pad_start[se] + (jnp.arange(A) - start[se])
    n_blocks = -(-A // MOE_BLOCK) + N_EXPERTS
    buf = jnp.zeros((n_blocks * MOE_BLOCK, D), tok.dtype).at[dest].set(tok[stok])
    blk_e = jnp.minimum(jnp.searchsorted(pad_end, jnp.arange(n_blocks) * MOE_BLOCK, side='right'),
                        N_EXPERTS - 1)

    def expert_block(args):
        xb, e = args
        return swiglu(xb, w_gate[e], w_up[e], w_down[e])

    out = lax.map(expert_block, (buf.reshape(n_blocks, MOE_BLOCK, D), blk_e)).reshape(-1, D)
    y = out[dest] * sw[:, None].astype(out.dtype)
    routed = jax.ops.segment_sum(y, stok, num_segments=T)
    return routed + swiglu(tok, sh_gate, sh_up, sh_down)


def setup_inputs(seed: int = 0) -> dict:
    key = jax.random.key(seed)
    ks = iter(jax.random.split(key, 48))
    f32 = jnp.float32

    def nrm(shape, scale):
        return jax.random.normal(next(ks), shape, f32) * scale

    n_a = (DEPTH + N_MIXERS - 1) // N_MIXERS
    n_b = DEPTH // N_MIXERS
    D = D_MODEL
    G, P, I = S5_GROUPS, S5_STATE, S5_GROUP
    lam_im0 = jnp.broadcast_to(jnp.pi * jnp.arange(P, dtype=f32), (n_b, 2, G, P))
    return {
        "x": nrm((BATCH, SEQ, D), 1.0),
        "c": nrm((BATCH, D), 1.0),
        "ctx": nrm((BATCH, CTX_LEN, D), 1.0),
        "c_ctx": nrm((D,), 1.0),
        "ada_w": nrm((DEPTH, D, N_MOD * D), 0.5 * D ** -0.5),
        "ada_b": nrm((DEPTH, N_MOD * D), 0.02),
        "norm_g": 1.0 + nrm((DEPTH, 4, D), 0.02),
        "mla_w_dqkv": nrm((n_a, D, Q_LORA + KV_LORA + ROPE_DIM), D ** -0.5),
        "mla_g_q": 1.0 + nrm((n_a, Q_LORA), 0.02),
        "mla_g_kv": 1.0 + nrm((n_a, KV_LORA), 0.02),
        "mla_w_uq": nrm((n_a, Q_LORA, MLA_HEADS * (NOPE_DIM + ROPE_DIM)), Q_LORA ** -0.5),
        "mla_w_ukv": nrm((n_a, KV_LORA, MLA_HEADS * (NOPE_DIM + V_DIM)), KV_LORA ** -0.5),
        "mla_w_o": nrm((n_a, MLA_HEADS * V_DIM, D), (MLA_HEADS * V_DIM) ** -0.5),
        "s5_lam_re": -0.5 + nrm((n_b, 2, G, P), 0.01),
        "s5_lam_im": lam_im0 + nrm((n_b, 2, G, P), 0.01),
        "s5_log_step": jax.random.uniform(next(ks), (n_b, 2, G), f32,
                                          minval=math.log(S5_DT_MIN), maxval=math.log(S5_DT_MAX)),
        "s5_b_re": nrm((n_b, 2, G, P, I), (2 * I) ** -0.5),
        "s5_b_im": nrm((n_b, 2, G, P, I), (2 * I) ** -0.5),
        "s5_c_re": nrm((n_b, 2, G, I, P), (2 * P) ** -0.5),
        "s5_c_im": nrm((n_b, 2, G, I, P), (2 * P) ** -0.5),
        "s5_d": nrm((n_b, D), 1.0),
        "s5_w_glu": nrm((n_b, D, 2 * D), D ** -0.5),
        "s5_b_glu": nrm((n_b, 2 * D), 0.02),
        "moe_w_router": nrm((DEPTH, D, N_EXPERTS), D ** -0.5),
        "moe_bias": nrm((DEPTH, N_EXPERTS), 0.01),
        "moe_w_gate": nrm((DEPTH, N_EXPERTS, D, D_EXPERT), D ** -0.5),
        "moe_w_up": nrm((DEPTH, N_EXPERTS, D, D_EXPERT), D ** -0.5),
        "moe_w_down": nrm((DEPTH, N_EXPERTS, D_EXPERT, D), D_EXPERT ** -0.5),
        "sh_w_gate": nrm((DEPTH, D, D_SHARED), D ** -0.5),
        "sh_w_up": nrm((DEPTH, D, D_SHARED), D ** -0.5),
        "sh_w_down": nrm((DEPTH, D_SHARED, D), D_SHARED ** -0.5),
    }


def reference(x, c, ctx, c_ctx, ada_w, ada_b, norm_g,
              mla_w_dqkv, mla_g_q, mla_g_kv, mla_w_uq, mla_w_ukv, mla_w_o,
              s5_lam_re, s5_lam_im, s5_log_step, s5_b_re, s5_b_im, s5_c_re, s5_c_im,
              s5_d, s5_w_glu, s5_b_glu,
              moe_w_router, moe_bias, moe_w_gate, moe_w_up, moe_w_down,
              sh_w_gate, sh_w_up, sh_w_down):
    B, L, D = x.shape
    n_ctx = ctx.shape[1]
    cos_l, sin_l = axial_rope_tables(L)
    cos = jnp.concatenate([jnp.ones((n_ctx, ROPE_DIM // 2), jnp.float32), cos_l], axis=0)
    sin = jnp.concatenate([jnp.zeros((n_ctx, ROPE_DIM // 2), jnp.float32), sin_l], axis=0)
    silu_c = jax.nn.silu(c)
    silu_cc = jax.nn.silu(c_ctx)
    x_lat, x_ctx = x, ctx
    for i in range(DEPTH):
        need_ctx = i < DEPTH - 1
        j = i // N_MIXERS
        mod_lat = jnp.split((silu_c @ ada_w[i] + ada_b[i])[:, None, :], N_MOD, axis=-1)
        mod_ctx = jnp.split((silu_cc @ ada_w[i] + ada_b[i])[None, None, :], N_MOD, axis=-1)
        h_lat = modulate(rms_norm(x_lat, norm_g[i, 0]), mod_lat[0], mod_lat[1])
        h_ctx = modulate(rms_norm(x_ctx, norm_g[i, 0]), mod_ctx[0], mod_ctx[1])
        if i % N_MIXERS == 0:
            o_lat, o_ctx = mla_mixer(h_ctx, h_lat, cos, sin, mla_w_dqkv[j], mla_g_q[j], mla_g_kv[j],
                                     mla_w_uq[j], mla_w_ukv[j], mla_w_o[j], need_ctx)
        else:
            o_lat, o_ctx = s5_mixer(h_ctx, h_lat, s5_lam_re[j], s5_lam_im[j], s5_log_step[j],
                                    s5_b_re[j], s5_b_im[j], s5_c_re[j], s5_c_im[j], s5_d[j],
                                    s5_w_glu[j], s5_b_glu[j], need_ctx)
        x_lat = x_lat + mod_lat[2] * rms_norm(o_lat, norm_g[i, 1])
        f_lat_in = modulate(rms_norm(x_lat, norm_g[i, 2]), mod_lat[3], mod_lat[4])
        moe_args = (moe_w_router[i], moe_bias[i], moe_w_gate[i], moe_w_up[i], moe_w_down[i],
                    sh_w_gate[i], sh_w_up[i], sh_w_down[i])
        if need_ctx:
            x_ctx = x_ctx + mod_ctx[2] * rms_norm(o_ctx, norm_g[i, 1])
            f_ctx_in = modulate(rms_norm(x_ctx, norm_g[i, 2]), mod_ctx[3], mod_ctx[4])
            tok = jnp.concatenate([f_ctx_in.reshape(-1, D), f_lat_in.reshape(-1, D)], axis=0)
            f = moe_ffn(tok, *moe_args)
            f_ctx = f[:B * n_ctx].reshape(B, n_ctx, D)
            f_lat = f[B * n_ctx:].reshape(B, L, D)
            x_ctx = x_ctx + mod_ctx[5] * rms_norm(f_ctx, norm_g[i, 3])
        else:
            f_lat = moe_ffn(f_lat_in.reshape(-1, D), *moe_args).reshape(B, L, D)
        x_lat = x_lat + mod_lat[5] * rms_norm(f_lat, norm_g[i, 3])
    return x_lat
```

```python
import functools

import jax
import jax.numpy as jnp
from jax import lax
from jax.experimental import pallas as pl
from jax.experimental.pallas import tpu as pltpu

F32 = jnp.float32
BF16 = jnp.bfloat16

N_MOD = 6
NORM_EPS = 1e-6
GRID_W = 64
MLA_HEADS = 8
Q_LORA = 384
KV_LORA = 256
NOPE_DIM = 128
ROPE_DIM = 64
V_DIM = 128
ROPE_BASE = 10000.0
QK_PAD = 256
S5_GROUP = 16
S5_STATE = 64
S5_GROUPS_PER_BLOCK = 8
N_EXPERTS = 64
TOP_K = 8
N_EXPERT_GROUPS = 8
TOPK_GROUPS = 4
D_EXPERT = 256
ROUTED_SCALE = 2.5

VMEM_LIMIT = 56 * 1024 * 1024


def _cparams(sem):
    return pltpu.CompilerParams(dimension_semantics=sem, vmem_limit_bytes=VMEM_LIMIT)


def _rms(x, g):
    return x * lax.rsqrt(jnp.mean(x * x, axis=-1, keepdims=True) + NORM_EPS) * g


def _rows(v, like):
    r = v.shape[0]
    if r == 1:
        return v
    tm, d = like.shape
    return jnp.broadcast_to(v[None], (tm // r, r, d)).reshape(tm, d)


def _mod_chunk(mod_ref, j, d):
    return mod_ref[:, j * d:(j + 1) * d]


def _dot(a, b):
    return jnp.dot(a, b, preferred_element_type=F32)


def _ada_kernel(c_ref, w_ref, b_ref, o_ref):
    c = c_ref[...]
    s = c * jax.nn.sigmoid(c)
    o_ref[...] = jnp.dot(s, w_ref[...], preferred_element_type=F32,
                         precision=lax.Precision.HIGHEST) + b_ref[...]


def _ada_mods(cvec, ada_w, ada_b):
    depth, d, n = ada_w.shape
    rows = cvec.shape[0]
    tn = 1536
    return pl.pallas_call(
        _ada_kernel,
        out_shape=jax.ShapeDtypeStruct((depth, rows, n), F32),
        grid=(depth, n // tn),
        in_specs=[pl.BlockSpec((rows, d), lambda l, j: (0, 0)),
                  pl.BlockSpec((None, d, tn), lambda l, j: (l, 0, j)),
                  pl.BlockSpec((None, 1, tn), lambda l, j: (l, 0, j))],
        out_specs=pl.BlockSpec((None, rows, tn), lambda l, j: (l, 0, j)),
        compiler_params=_cparams(("arbitrary", "arbitrary")),
        name="ada_mods",
    )(cvec, ada_w, ada_b.reshape(depth, 1, n))


def _pre_mla_kernel(x_ref, mod_ref, g0_ref, wd_ref, gq_ref, gkv_ref, wq_ref, wkv_ref, cos_ref, sin_ref,
                    q_ref, k_ref, v_ref):
    d = x_ref.shape[-1]
    x = x_ref[...]
    h = _rms(x, g0_ref[...]) * (1.0 + _mod_chunk(mod_ref, 1, d)) + _mod_chunk(mod_ref, 0, d)
    a = _dot(h.astype(BF16), wd_ref[...])
    cq = _rms(a[:, :Q_LORA], gq_ref[...])
    ckv = _rms(a[:, Q_LORA:Q_LORA + KV_LORA], gkv_ref[...])
    cos = cos_ref[...]
    sin = sin_ref[...]
    o = Q_LORA + KV_LORA
    k_rot = (a[:, o:o + 128] * cos + a[:, o + 128:o + 256] * sin).astype(BF16)
    qa = _dot(cq.astype(BF16), wq_ref[...])
    kva = _dot(ckv.astype(BF16), wkv_ref[...])
    hw = MLA_HEADS * 128
    scale = (NOPE_DIM + ROPE_DIM) ** -0.5
    for hd in range(MLA_HEADS):
        lo = hd * 128
        q_rot = qa[:, hw + lo:hw + lo + 128] * cos + qa[:, 2 * hw + lo:2 * hw + lo + 128] * sin
        q_ref[:, hd * QK_PAD:hd * QK_PAD + 128] = (qa[:, lo:lo + 128] * scale).astype(BF16)
        q_ref[:, hd * QK_PAD + 128:(hd + 1) * QK_PAD] = (q_rot * scale).astype(BF16)
        k_ref[:, hd * QK_PAD:hd * QK_PAD + 128] = kva[:, lo:lo + 128].astype(BF16)
        k_ref[:, hd * QK_PAD + 128:(hd + 1) * QK_PAD] = k_rot
    v_ref[...] = kva[:, hw:].astype(BF16)


def _pre_mla(x, mods, g0, wd, gq, gkv, wq, wkv, cos_t, sin_t, tm):
    b, n, d = x.shape
    nb_mod = mods.shape[0]
    full = lambda a: pl.BlockSpec(a.shape, lambda i, j: (0,) * a.ndim)
    mod_map = (lambda i, j: (i, 0, 0)) if nb_mod > 1 else (lambda i, j: (0, 0, 0))
    qk_w = MLA_HEADS * QK_PAD
    v_w = MLA_HEADS * V_DIM
    return pl.pallas_call(
        _pre_mla_kernel,
        out_shape=(jax.ShapeDtypeStruct((b, n, qk_w), BF16),
                   jax.ShapeDtypeStruct((b, n, qk_w), BF16),
                   jax.ShapeDtypeStruct((b, n, v_w), BF16)),
        grid=(b, n // tm),
        in_specs=[pl.BlockSpec((None, tm, d), lambda i, j: (i, j, 0)),
                  pl.BlockSpec((None, 1, mods.shape[-1]), mod_map),
                  full(g0), full(wd), full(gq), full(gkv), full(wq), full(wkv),
                  pl.BlockSpec((tm, 128), lambda i, j: (j, 0)),
                  pl.BlockSpec((tm, 128), lambda i, j: (j, 0))],
        out_specs=(pl.BlockSpec((None, tm, qk_w), lambda i, j: (i, j, 0)),
                   pl.BlockSpec((None, tm, qk_w), lambda i, j: (i, j, 0)),
                   pl.BlockSpec((None, tm, v_w), lambda i, j: (i, j, 0))),
        compiler_params=_cparams(("arbitrary", "arbitrary")),
        name="pre_mla",
    )(x, mods, g0, wd, gq, gkv, wq, wkv, cos_t, sin_t)


def _attn_kernel(*refs, n_seg):
    q_ref = refs[0]
    k_refs = refs[1:1 + n_seg]
    v_refs = refs[1 + n_seg:1 + 2 * n_seg]
    o_ref = refs[1 + 2 * n_seg]
    nt = (((1,), (1,)), ((), ()))
    for hd in range(MLA_HEADS):
        q = q_ref[:, hd * QK_PAD:(hd + 1) * QK_PAD]
        ss = [lax.dot_general(q, k[:, hd * QK_PAD:(hd + 1) * QK_PAD], nt, preferred_element_type=F32)
              for k in k_refs]
        m = ss[0].max(axis=-1, keepdims=True)
        for s in ss[1:]:
            m = jnp.maximum(m, s.max(axis=-1, keepdims=True))
        ps = [jnp.exp(s - m) for s in ss]
        l = ps[0].sum(axis=-1, keepdims=True)
        for p in ps[1:]:
            l = l + p.sum(axis=-1, keepdims=True)
        acc = None
        for p, v in zip(ps, v_refs):
            pv = _dot(p.astype(BF16), v[:, hd * V_DIM:(hd + 1) * V_DIM])
            acc = pv if acc is None else acc + pv
        o_ref[:, hd * V_DIM:(hd + 1) * V_DIM] = (acc / l).astype(BF16)


def _attention(q, ks, vs, tq):
    b, nq, qk_w = q.shape
    v_w = vs[0].shape[-1]
    kv_spec = lambda a: pl.BlockSpec((None,) + a.shape[1:], lambda i, j: (i, 0, 0))
    return pl.pallas_call(
        functools.partial(_attn_kernel, n_seg=len(ks)),
        out_shape=jax.ShapeDtypeStruct((b, nq, v_w), BF16),
        grid=(b, nq // tq),
        in_specs=[pl.BlockSpec((None, tq, qk_w), lambda i, j: (i, j, 0))]
                 + [kv_spec(a) for a in ks] + [kv_spec(a) for a in vs],
        out_specs=pl.BlockSpec((None, tq, v_w), lambda i, j: (i, j, 0)),
        compiler_params=_cparams(("arbitrary", "arbitrary")),
        name="mla_attention",
    )(q, *ks, *vs)


def _post_core(o, x, mod_ref, g1_ref, g2_ref, wr_ref, x1_ref, fin_ref, lg_ref):
    d = x.shape[-1]
    gate = _rows(_mod_chunk(mod_ref, 2, d), x)
    shift = _rows(_mod_chunk(mod_ref, 3, d), x)
    scale = _rows(_mod_chunk(mod_ref, 4, d), x)
    x1 = x + gate * _rms(o, g1_ref[...])
    fin = _rms(x1, g2_ref[...]) * (1.0 + scale) + shift
    x1_ref[...] = x1
    fin_ref[...] = fin.astype(BF16)
    lg_ref[...] = lax.dot_general(wr_ref[...], fin, (((1,), (1,)), ((), ())),
                                  preferred_element_type=F32, precision=lax.Precision.HIGHEST)


def _post_proj_kernel(o_ref, wo_ref, x_ref, mod_ref, g1_ref, g2_ref, wr_ref, x1_ref, fin_ref, lg_ref):
    o = _dot(o_ref[...], wo_ref[...])
    _post_core(o, x_ref[...], mod_ref, g1_ref, g2_ref, wr_ref, x1_ref, fin_ref, lg_ref)


def _post_glu_kernel(h_ref, yf_ref, yb_ref, dsk_ref, wg_ref, bg_ref, x_ref, mod_ref, g1_ref, g2_ref, wr_ref,
                     x1_ref, fin_ref, lg_ref):
    d = x_ref.shape[-1]
    y = h_ref[...] * dsk_ref[...] + yf_ref[...] + yb_ref[...]
    z = _dot(jax.nn.gelu(y, approximate=True).astype(BF16), wg_ref[...]) + bg_ref[...]
    o = z[:, :d] * jax.nn.sigmoid(z[:, d:])
    _post_core(o, x_ref[...], mod_ref, g1_ref, g2_ref, wr_ref, x1_ref, fin_ref, lg_ref)


def _post_mixer(kernel, tok_inputs, consts, x, mods, g1, g2, wr_t, tm, rows_per_mod, name):
    t, d = x.shape
    ne = wr_t.shape[0]
    tiles_per_mod = rows_per_mod // tm
    full = lambda a: pl.BlockSpec(a.shape, lambda i: (0,) * a.ndim)
    tile = lambda a: pl.BlockSpec((tm, a.shape[-1]), lambda i: (i, 0))
    mod_spec = pl.BlockSpec((None,) + mods.shape[1:], lambda i: (i // tiles_per_mod, 0, 0))
    return pl.pallas_call(
        kernel,
        out_shape=(jax.ShapeDtypeStruct((t, d), F32),
                   jax.ShapeDtypeStruct((t, d), BF16),
                   jax.ShapeDtypeStruct((ne, t), F32)),
        grid=(t // tm,),
        in_specs=[tile(a) for a in tok_inputs] + [full(a) for a in consts]
                 + [tile(x), mod_spec, full(g1), full(g2), full(wr_t)],
        out_specs=(tile(x), tile(x), pl.BlockSpec((ne, tm), lambda i: (0, i))),
        compiler_params=_cparams(("arbitrary",)),
        name=name,
    )(*tok_inputs, *consts, x, mods, g1, g2, wr_t)


def _route_kernel(lg_ref, bias_ref, eidx_ref, gate_ref, rank_ref, cnt_ref, tri_ref, base_ref):
    i = pl.program_id(0)
    ne, tt = lg_ref.shape
    gsz = ne // N_EXPERT_GROUPS
    shp = (N_EXPERT_GROUPS, gsz, tt)
    neg = -jnp.inf

    @pl.when(i == 0)
    def _():
        base_ref[...] = jnp.zeros_like(base_ref)
        r = lax.broadcasted_iota(jnp.int32, (tt, tt), 0)
        c = lax.broadcasted_iota(jnp.int32, (tt, tt), 1)
        tri_ref[...] = (r < c).astype(BF16)

    scores = jax.nn.sigmoid(lg_ref[...])
    s3 = scores.reshape(shp)
    b3 = (scores + bias_ref[...]).reshape(shp)
    io_e = lax.broadcasted_iota(jnp.int32, shp, 1)
    io_g = lax.broadcasted_iota(jnp.int32, shp, 0)
    io_flat = io_g * gsz + io_e
    m1 = b3.max(axis=1, keepdims=True)
    i1 = jnp.where(b3 == m1, io_e, gsz).min(axis=1, keepdims=True)
    m2 = jnp.where(io_e == i1, neg, b3).max(axis=1, keepdims=True)
    cur = jnp.broadcast_to(m1 + m2, shp)
    gsel = jnp.zeros(shp, jnp.bool_)
    for _ in range(TOPK_GROUPS):
        m = cur.max(axis=0, keepdims=True)
        gi = jnp.where(cur == m, io_g, N_EXPERT_GROUPS).min(axis=0, keepdims=True)
        hit = io_g == gi
        gsel = jnp.logical_or(gsel, hit)
        cur = jnp.where(hit, neg, cur)
    cand = jnp.where(gsel, b3, neg)
    sel = jnp.zeros(shp, jnp.bool_)
    eids, gts = [], []
    for _ in range(TOP_K):
        m = cand.max(axis=0, keepdims=True).max(axis=1, keepdims=True)
        ei = jnp.where(cand == m, io_flat, ne).min(axis=0, keepdims=True).min(axis=1, keepdims=True)
        hit = io_flat == ei
        gts.append(jnp.where(hit, s3, 0.0).sum(axis=0, keepdims=True).sum(axis=1, keepdims=True))
        eids.append(ei)
        sel = jnp.logical_or(sel, hit)
        cand = jnp.where(hit, neg, cand)
    gsum = gts[0]
    for g in gts[1:]:
        gsum = gsum + g
    self32 = sel.astype(F32).reshape(ne, tt)
    cnt = _dot(self32.astype(BF16), tri_ref[...]) + base_ref[...]
    cnt3 = cnt.reshape(shp)
    for k in range(TOP_K):
        hit = io_flat == eids[k]
        rk = jnp.where(hit, cnt3, 0.0).sum(axis=0, keepdims=True).sum(axis=1, keepdims=True)
        rank_ref[k:k + 1, :] = rk.reshape(1, tt).astype(jnp.int32)
        eidx_ref[k:k + 1, :] = eids[k].reshape(1, tt)
        gate_ref[k:k + 1, :] = (gts[k] / gsum * ROUTED_SCALE).reshape(1, tt)
    base_new = base_ref[...] + self32.sum(axis=1, keepdims=True)
    base_ref[...] = base_new
    cnt_ref[...] = jnp.broadcast_to(base_new, cnt_ref.shape)


def _route(logits_t, bias, tt):
    ne, t = logits_t.shape
    out_i = jax.ShapeDtypeStruct((TOP_K, t), jnp.int32)
    row = pl.BlockSpec((TOP_K, tt), lambda i: (0, i))
    return pl.pallas_call(
        _route_kernel,
        out_shape=(out_i, jax.ShapeDtypeStruct((TOP_K, t), F32), out_i,
                   jax.ShapeDtypeStruct((ne, 128), F32)),
        grid=(t // tt,),
        in_specs=[pl.BlockSpec((ne, tt), lambda i: (0, i)),
                  pl.BlockSpec((ne, 1), lambda i: (0, 0))],
        out_specs=(row, row, row, pl.BlockSpec((ne, 128), lambda i: (0, 0))),
        scratch_shapes=[pltpu.VMEM((tt, tt), BF16), pltpu.VMEM((ne, 1), F32)],
        compiler_params=_cparams(("arbitrary",)),
        name="moe_route",
    )(logits_t, bias.reshape(ne, 1))


def _expert_kernel(be_ref, nu_ref, x_ref, wgu_ref, wd_ref, o_ref):
    @pl.when(pl.program_id(0) < nu_ref[0])
    def _():
        gu = _dot(x_ref[...], wgu_ref[...])
        g = gu[:, :D_EXPERT]
        h = g * jax.nn.sigmoid(g) * gu[:, D_EXPERT:]
        o_ref[...] = _dot(h.astype(BF16), wd_ref[...]).astype(BF16)


def _experts(xs, blk_e, n_used, wgu, wd, tb):
    rows, d = xs.shape
    nb = rows // tb
    row_map = lambda i, be, nu: (jnp.minimum(i, nu[0] - 1), 0)
    grid_spec = pltpu.PrefetchScalarGridSpec(
        num_scalar_prefetch=2,
        grid=(nb,),
        in_specs=[pl.BlockSpec((tb, d), row_map),
                  pl.BlockSpec((None, d, 2 * D_EXPERT), lambda i, be, nu: (be[i], 0, 0)),
                  pl.BlockSpec((None, D_EXPERT, d), lambda i, be, nu: (be[i], 0, 0))],
        out_specs=pl.BlockSpec((tb, d), row_map),
    )
    return pl.pallas_call(
        _expert_kernel,
        out_shape=jax.ShapeDtypeStruct((rows, d), BF16),
        grid_spec=grid_spec,
        compiler_params=_cparams(("arbitrary",)),
        name="moe_experts",
    )(blk_e, n_used, xs, wgu, wd)


def _combine_kernel(yk_ref, gate_ref, fin_ref, shgu_ref, shd_ref, x1_ref, mod_ref, g3_ref, o_ref):
    d = x1_ref.shape[-1]
    gates = gate_ref[...]
    f = None
    for k in range(TOP_K):
        term = gates[:, k:k + 1] * yk_ref[k].astype(F32)
        f = term if f is None else f + term
    gu = _dot(fin_ref[...], shgu_ref[...])
    g = gu[:, :D_EXPERT]
    hsh = g * jax.nn.sigmoid(g) * gu[:, D_EXPERT:]
    f = f + _dot(hsh.astype(BF16), shd_ref[...])
    x1 = x1_ref[...]
    o_ref[...] = x1 + _rows(_mod_chunk(mod_ref, 5, d), x1) * _rms(f, g3_ref[...])


def _combine(yk, gates, fin, shgu, shd, x1, mods, g3, tm, rows_per_mod, tok_off):
    t, d = x1.shape
    off = tok_off // tm
    tiles_per_mod = rows_per_mod // tm
    full = lambda a: pl.BlockSpec(a.shape, lambda i: (0,) * a.ndim)
    return pl.pallas_call(
        _combine_kernel,
        out_shape=jax.ShapeDtypeStruct((t, d), F32),
        grid=(t // tm,),
        in_specs=[pl.BlockSpec((TOP_K, tm, d), lambda i: (0, i + off, 0)),
                  pl.BlockSpec((tm, TOP_K), lambda i: (i + off, 0)),
                  pl.BlockSpec((tm, d), lambda i: (i + off, 0)),
                  full(shgu), full(shd),
                  pl.BlockSpec((tm, d), lambda i: (i, 0)),
                  pl.BlockSpec((None,) + mods.shape[1:], lambda i: (i // tiles_per_mod, 0, 0)),
                  full(g3)],
        out_specs=pl.BlockSpec((tm, d), lambda i: (i, 0)),
        compiler_params=_cparams(("arbitrary",)),
        name="moe_combine",
    )(yk, gates, fin, shgu, shd, x1, mods, g3)


def _moe(fin, logits_t, bias, wgu, wd, tb):
    t, d = fin.shape
    ne = wgu.shape[0]
    eidx_t, gates_t, rank_t, cnt = _route(logits_t, bias, 512)
    counts = cnt[:, 0].astype(jnp.int32)
    padded = (counts + tb - 1) // tb * tb
    pad_end = jnp.cumsum(padded)
    pad_start = pad_end - padded
    dest = pad_start[eidx_t] + rank_t
    nb = (t * TOP_K) // tb + ne
    n_used = pad_end[-1] // tb
    blk = jnp.minimum(jnp.searchsorted(pad_end, jnp.arange(nb, dtype=jnp.int32) * tb, side="right"), ne - 1)
    blk_e = jnp.where(jnp.arange(nb) < n_used, blk, blk[jnp.maximum(n_used - 1, 0)]).astype(jnp.int32)
    src = jnp.zeros((nb * tb,), jnp.int32).at[dest.reshape(-1)].set(
        jnp.tile(jnp.arange(t, dtype=jnp.int32), TOP_K))
    xs = fin[src]
    ys = _experts(xs, blk_e, n_used.reshape(1).astype(jnp.int32), wgu, wd, tb)
    return ys[dest], gates_t.T


def _pre_s5_kernel(x_ref, mod_ref, g0_ref, h_ref):
    d = x_ref.shape[-1]
    x = x_ref[...]
    h_ref[...] = (_rms(x, g0_ref[...]) * (1.0 + _rows(_mod_chunk(mod_ref, 1, d), x))
                  + _rows(_mod_chunk(mod_ref, 0, d), x))


def _pre_s5(x, mods, g0, tm):
    t, d = x.shape
    return pl.pallas_call(
        _pre_s5_kernel,
        out_shape=jax.ShapeDtypeStruct((t, d), F32),
        grid=(t // tm,),
        in_specs=[pl.BlockSpec((tm, d), lambda i: (i, 0)),
                  pl.BlockSpec(mods.shape, lambda i: (0, 0)),
                  pl.BlockSpec(g0.shape, lambda i: (0, 0))],
        out_specs=pl.BlockSpec((tm, d), lambda i: (i, 0)),
        compiler_params=_cparams(("arbitrary",)),
        name="pre_s5",
    )(x, mods, g0)


def _s5_scan_kernel(hc_ref, hl_ref, bm_ref, cm_ref, lam_ref, y_ref, bu_ref, st_ref, *, n_ctx_chunks):
    dr = pl.program_id(1)
    j = pl.program_id(2)
    tc, nb, cw = hl_ref.shape
    half = bu_ref.shape[1] // 2

    @pl.when(j == 0)
    def _():
        st_ref[...] = jnp.zeros_like(st_ref)

    def run(u_ref, emit):
        u = u_ref[...].reshape(tc * nb, cw).astype(BF16)
        bu_ref[...] = _dot(u, bm_ref[...])
        lr = jnp.broadcast_to(lam_ref[0:1, :], (nb, half))
        li = jnp.broadcast_to(lam_ref[1:2, :], (nb, half))

        def step(i, carry):
            xr, xi = carry
            t = jnp.where(dr == 0, i, tc - 1 - i)
            r0 = pl.multiple_of(t * nb, nb)
            nr = lr * xr - li * xi + bu_ref[pl.ds(r0, nb), 0:half]
            ni = lr * xi + li * xr + bu_ref[pl.ds(r0, nb), half:2 * half]
            bu_ref[pl.ds(r0, nb), 0:half] = nr
            bu_ref[pl.ds(r0, nb), half:2 * half] = ni
            return nr, ni

        xr, xi = lax.fori_loop(0, tc, step, (st_ref[:, 0:half], st_ref[:, half:2 * half]), unroll=4)
        st_ref[:, 0:half] = xr
        st_ref[:, half:2 * half] = xi
        if emit:
            y_ref[...] = _dot(bu_ref[...].astype(BF16), cm_ref[...]).reshape(tc, nb, cw)

    @pl.when(j < n_ctx_chunks)
    def _():
        run(hc_ref, False)

    @pl.when(j >= n_ctx_chunks)
    def _():
        run(hl_ref, True)


def _s5_scan(h_ctx, h_lat, bm, cm, lam, tc):
    nc, nb, d = h_ctx.shape
    nl = h_lat.shape[0]
    cw = S5_GROUPS_PER_BLOCK * S5_GROUP
    sw = 2 * S5_GROUPS_PER_BLOCK * S5_STATE
    ncc, nlc = nc // tc, nl // tc

    def ctx_map(g, dr, j):
        return (jnp.where(dr == 0, jnp.minimum(j, ncc - 1), jnp.maximum(ncc - 1 - j, 0)), 0, g)

    def lat_idx(dr, j):
        jj = jnp.maximum(j - ncc, 0)
        return jnp.where(dr == 0, jj, nlc - 1 - jj)

    return pl.pallas_call(
        functools.partial(_s5_scan_kernel, n_ctx_chunks=ncc),
        out_shape=jax.ShapeDtypeStruct((2, nl, nb, d), F32),
        grid=(d // cw, 2, ncc + nlc),
        in_specs=[pl.BlockSpec((tc, nb, cw), ctx_map),
                  pl.BlockSpec((tc, nb, cw), lambda g, dr, j: (lat_idx(dr, j), 0, g)),
                  pl.BlockSpec((None, None, cw, sw), lambda g, dr, j: (dr, g, 0, 0)),
                  pl.BlockSpec((None, None, sw, cw), lambda g, dr, j: (dr, g, 0, 0)),
                  pl.BlockSpec((None, None, 2, sw // 2), lambda g, dr, j: (dr, g, 0, 0))],
        out_specs=pl.BlockSpec((None, tc, nb, cw), lambda g, dr, j: (dr, lat_idx(dr, j), 0, g)),
        scratch_shapes=[pltpu.VMEM((tc * nb, sw), F32), pltpu.VMEM((nb, sw), F32)],
        compiler_params=_cparams(("arbitrary", "arbitrary", "arbitrary")),
        name="s5_scan",
    )(h_ctx, h_lat, bm, cm, lam)


def _s5_params(lam_re, lam_im, log_step, b_re, b_im, c_re, c_im):
    g, p = lam_re.shape[1:]
    gb = S5_GROUPS_PER_BLOCK
    nblk = g // gb
    step = jnp.exp(log_step)[..., None]
    mag = jnp.exp(lam_re * step)
    lb_re = mag * jnp.cos(lam_im * step)
    lb_im = mag * jnp.sin(lam_im * step)
    den = lam_re * lam_re + lam_im * lam_im
    f_re = ((lb_re - 1.0) * lam_re + lb_im * lam_im) / den
    f_im = (lb_im * lam_re - (lb_re - 1.0) * lam_im) / den
    bb_re = f_re[..., None] * b_re - f_im[..., None] * b_im
    bb_im = f_re[..., None] * b_im + f_im[..., None] * b_re
    eye = jnp.eye(gb, dtype=F32)

    def in_map(w):
        w = w.reshape(2, nblk, gb, p, S5_GROUP)
        return jnp.einsum("dnapi,ab->dnaibp", w, eye).reshape(2, nblk, gb * S5_GROUP, gb * p)

    def out_map(w):
        w = w.reshape(2, nblk, gb, S5_GROUP, p)
        return jnp.einsum("dnaip,ab->dnapbi", w, eye).reshape(2, nblk, gb * p, gb * S5_GROUP)

    bm = jnp.concatenate([in_map(bb_re), in_map(bb_im)], axis=-1).astype(BF16)
    cm = jnp.concatenate([out_map(c_re), out_map(-c_im)], axis=-2).astype(BF16)
    lam = jnp.stack([lb_re.reshape(2, nblk, gb * p), lb_im.reshape(2, nblk, gb * p)], axis=2)
    return bm, cm, lam


def _rope_tables(n_tokens):
    rows = n_tokens // GRID_W
    row = jnp.repeat(jnp.arange(rows), GRID_W).astype(F32)
    col = jnp.tile(jnp.arange(GRID_W), rows).astype(F32)
    n_freq = ROPE_DIM // 4
    inv_freq = ROPE_BASE ** (-jnp.arange(n_freq, dtype=F32) / n_freq)
    ang = jnp.concatenate([row[:, None] * inv_freq, col[:, None] * inv_freq], axis=-1)
    cos, sin = jnp.cos(ang), jnp.sin(ang)
    z = jnp.zeros((n_tokens, 128 - ROPE_DIM), F32)
    return (jnp.concatenate([cos, cos, z], axis=-1), jnp.concatenate([-sin, sin, z], axis=-1))


def _split_pairs(w):
    ev, od = w[..., 0::2], w[..., 1::2]
    z = jnp.zeros(w.shape[:-1] + (128 - ROPE_DIM,), w.dtype)
    return jnp.concatenate([ev, od, z], axis=-1), jnp.concatenate([od, ev, z], axis=-1)


def _mla_weights(w_dqkv, w_uq, w_ukv):
    kp, kps = _split_pairs(w_dqkv[:, Q_LORA + KV_LORA:])
    wd = jnp.concatenate([w_dqkv[:, :Q_LORA + KV_LORA], kp, kps], axis=-1).astype(BF16)
    wq3 = w_uq.reshape(Q_LORA, MLA_HEADS, NOPE_DIM + ROPE_DIM)
    qp, qps = _split_pairs(wq3[:, :, NOPE_DIM:])
    wq = jnp.concatenate([wq3[:, :, :NOPE_DIM].reshape(Q_LORA, -1), qp.reshape(Q_LORA, -1),
                          qps.reshape(Q_LORA, -1)], axis=-1).astype(BF16)
    wkv3 = w_ukv.reshape(KV_LORA, MLA_HEADS, NOPE_DIM + V_DIM)
    wkv = jnp.concatenate([wkv3[:, :, :NOPE_DIM].reshape(KV_LORA, -1),
                           wkv3[:, :, NOPE_DIM:].reshape(KV_LORA, -1)], axis=-1).astype(BF16)
    return wd, wq, wkv


@jax.jit
def kernel(x, c, ctx, c_ctx, ada_w, ada_b, norm_g, mla_w_dqkv, mla_g_q, mla_g_kv, mla_w_uq, mla_w_ukv, mla_w_o, s5_lam_re, s5_lam_im, s5_log_step, s5_b_re, s5_b_im, s5_c_re, s5_c_im, s5_d, s5_w_glu, s5_b_glu, moe_w_router, moe_bias, moe_w_gate, moe_w_up, moe_w_down, sh_w_gate, sh_w_up, sh_w_down):
    b, l, d = x.shape
    n_ctx = ctx.shape[1]
    assert ada_w.shape[0] == 2 and b % 8 == 0
    tm = 256
    tb = 256
    row = lambda v: v.reshape(1, -1)

    n_rows = (b + 1 + 7) // 8 * 8
    cvec = jnp.zeros((n_rows, d), F32).at[:b].set(c).at[b].set(c_ctx)
    mods = _ada_mods(cvec, ada_w, ada_b)

    def moe_weights(i):
        wgu = jnp.concatenate([moe_w_gate[i], moe_w_up[i]], axis=-1).astype(BF16)
        shgu = jnp.concatenate([sh_w_gate[i], sh_w_up[i]], axis=-1).astype(BF16)
        return wgu, moe_w_down[i].astype(BF16), shgu, sh_w_down[i].astype(BF16)

    mod_lat = mods[0, :b].reshape(b, 1, N_MOD * d)
    mod_ctx = mods[0, b].reshape(1, 1, N_MOD * d)
    wd, wq, wkv = _mla_weights(mla_w_dqkv[0], mla_w_uq[0], mla_w_ukv[0])
    cos_l, sin_l = _rope_tables(l)
    cos_c = jnp.concatenate([jnp.ones((n_ctx, ROPE_DIM), F32), jnp.zeros((n_ctx, 128 - ROPE_DIM), F32)], -1)
    sin_c = jnp.zeros((n_ctx, 128), F32)
    pre = functools.partial(_pre_mla, g0=row(norm_g[0, 0]), wd=wd, gq=row(mla_g_q[0]), gkv=row(mla_g_kv[0]),
                            wq=wq, wkv=wkv, tm=tm)
    q_c, k_c, v_c = pre(ctx, mod_ctx, cos_t=cos_c, sin_t=sin_c)
    q_l, k_l, v_l = pre(x, mod_lat, cos_t=cos_l, sin_t=sin_l)
    o_l = _attention(q_l, [k_c, k_l], [v_c, v_l], tm)
    o_c = _attention(q_c, [k_c], [v_c], n_ctx)

    wo = mla_w_o[0].astype(BF16)
    wr_t = moe_w_router[0].T
    g1, g2, g3 = row(norm_g[0, 1]), row(norm_g[0, 2]), row(norm_g[0, 3])
    post = functools.partial(_post_mixer, _post_proj_kernel, consts=[wo], g1=g1, g2=g2, wr_t=wr_t, tm=tm,
                             name="post_mla")
    x1_c, fin_c, lg_c = post([o_c.reshape(b * n_ctx, -1)], x=ctx.reshape(b * n_ctx, d), mods=mod_ctx,
                             rows_per_mod=b * n_ctx)
    x1_l, fin_l, lg_l = post([o_l.reshape(b * l, -1)], x=x.reshape(b * l, d), mods=mod_lat, rows_per_mod=l)
    fin = jnp.concatenate([fin_c, fin_l], axis=0)
    lg = jnp.concatenate([lg_c, lg_l], axis=1)
    wgu, wdn, shgu, shd = moe_weights(0)
    yk, gates = _moe(fin, lg, moe_bias[0], wgu, wdn, tb)
    comb = functools.partial(_combine, yk, gates, fin, shgu, shd, g3=g3, tm=tm)
    x2_c = comb(x1=x1_c, mods=mod_ctx, rows_per_mod=b * n_ctx, tok_off=0)
    x2_l = comb(x1=x1_l, mods=mod_lat, rows_per_mod=l, tok_off=b * n_ctx)

    xt_l = jnp.swapaxes(x2_l.reshape(b, l, d), 0, 1).reshape(l * b, d)
    xt_c = jnp.swapaxes(x2_c.reshape(b, n_ctx, d), 0, 1).reshape(n_ctx * b, d)
    mod_lat = mods[1, :b]
    mod_ctx = jnp.broadcast_to(mods[1, b][None], (b, N_MOD * d))
    g0 = row(norm_g[1, 0])
    h_l = _pre_s5(xt_l, mod_lat, g0, 512)
    h_c = _pre_s5(xt_c, mod_ctx, g0, 512)
    bm, cm, lam = _s5_params(s5_lam_re[0], s5_lam_im[0], s5_log_step[0], s5_b_re[0], s5_b_im[0],
                             s5_c_re[0], s5_c_im[0])
    y = _s5_scan(h_c.reshape(n_ctx, b, d), h_l.reshape(l, b, d), bm, cm, lam, 64)
    g1, g2, g3 = row(norm_g[1, 1]), row(norm_g[1, 2]), row(norm_g[1, 3])
    x1, fin, lg = _post_mixer(
        _post_glu_kernel, [h_l, y[0].reshape(l * b, d), y[1].reshape(l * b, d)],
        [row(s5_d[0]), s5_w_glu[0].astype(BF16), row(s5_b_glu[0])],
        x=xt_l, mods=mod_lat[None], g1=g1, g2=g2, wr_t=moe_w_router[1].T, tm=tm, rows_per_mod=l * b,
        name="post_s5")
    wgu, wdn, shgu, shd = moe_weights(1)
    yk, gates = _moe(fin, lg, moe_bias[1], wgu, wdn, tb)
    x2 = _combine(yk, gates, fin, shgu, shd, x1, mod_lat[None], g3, tm, rows_per_mod=l * b, tok_off=0)
    return jnp.swapaxes(x2.reshape(l, b, d), 0, 1)
```

```python
import functools

import jax
import jax.numpy as jnp
from jax import lax
from jax.experimental import pallas as pl
from jax.experimental.pallas import tpu as pltpu
from jax.experimental.pallas import tpu_sc as plsc

F32 = jnp.float32
BF16 = jnp.bfloat16
U32 = jnp.uint32

N_MOD = 6
NORM_EPS = 1e-6
GRID_W = 64
MLA_HEADS = 8
Q_LORA = 384
KV_LORA = 256
NOPE_DIM = 128
ROPE_DIM = 64
V_DIM = 128
ROPE_BASE = 10000.0
QK_PAD = 256
S5_GROUP = 16
S5_STATE = 64
S5_GROUPS_PER_BLOCK = 8
N_EXPERTS = 64
TOP_K = 8
N_EXPERT_GROUPS = 8
TOPK_GROUPS = 4
D_EXPERT = 256
ROUTED_SCALE = 2.5

VMEM_LIMIT = 56 * 1024 * 1024


def _cparams(sem):
    return pltpu.CompilerParams(dimension_semantics=sem, vmem_limit_bytes=VMEM_LIMIT)


def _rms(x, g):
    return x * lax.rsqrt(jnp.mean(x * x, axis=-1, keepdims=True) + NORM_EPS) * g


def _rows(v, like):
    r = v.shape[0]
    if r == 1:
        return v
    tm, d = like.shape
    return jnp.broadcast_to(v[None], (tm // r, r, d)).reshape(tm, d)


def _mod_chunk(mod_ref, j, d):
    return mod_ref[:, j * d:(j + 1) * d]


def _dot(a, b):
    return jnp.dot(a, b, preferred_element_type=F32)


PACK_ROWS = 4
LANES = 128


def _pack_store(ref, val, lead=()):
    tm = val.shape[0]
    bits = lax.bitcast_convert_type(val.astype(BF16).astype(F32), U32)
    for s in range(PACK_ROWS):
        lo = bits[:, s * LANES:(s + 1) * LANES] >> 16
        hi = bits[:, (s + PACK_ROWS) * LANES:(s + PACK_ROWS + 1) * LANES] & jnp.uint32(0xFFFF0000)
        ref[lead + (pl.ds(s, tm, stride=PACK_ROWS), slice(None))] = lo | hi


def _unpack_load(ref, tm, lead=()):
    los, his = [], []
    for s in range(PACK_ROWS):
        w = ref[lead + (pl.ds(s, tm, stride=PACK_ROWS), slice(None))]
        los.append(lax.bitcast_convert_type(w << 16, F32))
        his.append(lax.bitcast_convert_type(w & jnp.uint32(0xFFFF0000), F32))
    return los + his


def _ada_kernel(c_ref, w_ref, b_ref, o_ref):
    c = c_ref[...]
    s = c * jax.nn.sigmoid(c)
    o_ref[...] = jnp.dot(s, w_ref[...], preferred_element_type=F32,
                         precision=lax.Precision.HIGHEST) + b_ref[...]


def _ada_mods(cvec, ada_w, ada_b):
    depth, d, n = ada_w.shape
    rows = cvec.shape[0]
    tn = 1536
    return pl.pallas_call(
        _ada_kernel,
        out_shape=jax.ShapeDtypeStruct((depth, rows, n), F32),
        grid=(depth, n // tn),
        in_specs=[pl.BlockSpec((rows, d), lambda l, j: (0, 0)),
                  pl.BlockSpec((None, d, tn), lambda l, j: (l, 0, j)),
                  pl.BlockSpec((None, 1, tn), lambda l, j: (l, 0, j))],
        out_specs=pl.BlockSpec((None, rows, tn), lambda l, j: (l, 0, j)),
        compiler_params=_cparams(("arbitrary", "arbitrary")),
        name="ada_mods",
    )(cvec, ada_w, ada_b.reshape(depth, 1, n))


def _pre_mla_kernel(x_ref, mod_ref, g0_ref, wd_ref, gq_ref, gkv_ref, wq_ref, wkv_ref, cos_ref, sin_ref,
                    q_ref, k_ref, v_ref):
    d = x_ref.shape[-1]
    x = x_ref[...]
    h = _rms(x, g0_ref[...]) * (1.0 + _mod_chunk(mod_ref, 1, d)) + _mod_chunk(mod_ref, 0, d)
    a = _dot(h.astype(BF16), wd_ref[...])
    cq = _rms(a[:, :Q_LORA], gq_ref[...])
    ckv = _rms(a[:, Q_LORA:Q_LORA + KV_LORA], gkv_ref[...])
    cos = cos_ref[...]
    sin = sin_ref[...]
    o = Q_LORA + KV_LORA
    k_rot = (a[:, o:o + 128] * cos + a[:, o + 128:o + 256] * sin).astype(BF16)
    qa = _dot(cq.astype(BF16), wq_ref[...])
    kva = _dot(ckv.astype(BF16), wkv_ref[...])
    hw = MLA_HEADS * 128
    scale = (NOPE_DIM + ROPE_DIM) ** -0.5
    for hd in range(MLA_HEADS):
        lo = hd * 128
        q_rot = qa[:, hw + lo:hw + lo + 128] * cos + qa[:, 2 * hw + lo:2 * hw + lo + 128] * sin
        q_ref[:, hd * QK_PAD:hd * QK_PAD + 128] = (qa[:, lo:lo + 128] * scale).astype(BF16)
        q_ref[:, hd * QK_PAD + 128:(hd + 1) * QK_PAD] = (q_rot * scale).astype(BF16)
        k_ref[:, hd * QK_PAD:hd * QK_PAD + 128] = kva[:, lo:lo + 128].astype(BF16)
        k_ref[:, hd * QK_PAD + 128:(hd + 1) * QK_PAD] = k_rot
    v_ref[...] = kva[:, hw:].astype(BF16)


def _pre_mla(x, mods, g0, wd, gq, gkv, wq, wkv, cos_t, sin_t, tm):
    b, n, d = x.shape
    nb_mod = mods.shape[0]
    full = lambda a: pl.BlockSpec(a.shape, lambda i, j: (0,) * a.ndim)
    mod_map = (lambda i, j: (i, 0, 0)) if nb_mod > 1 else (lambda i, j: (0, 0, 0))
    qk_w = MLA_HEADS * QK_PAD
    v_w = MLA_HEADS * V_DIM
    return pl.pallas_call(
        _pre_mla_kernel,
        out_shape=(jax.ShapeDtypeStruct((b, n, qk_w), BF16),
                   jax.ShapeDtypeStruct((b, n, qk_w), BF16),
                   jax.ShapeDtypeStruct((b, n, v_w), BF16)),
        grid=(b, n // tm),
        in_specs=[pl.BlockSpec((None, tm, d), lambda i, j: (i, j, 0)),
                  pl.BlockSpec((None, 1, mods.shape[-1]), mod_map),
                  full(g0), full(wd), full(gq), full(gkv), full(wq), full(wkv),
                  pl.BlockSpec((tm, 128), lambda i, j: (j, 0)),
                  pl.BlockSpec((tm, 128), lambda i, j: (j, 0))],
        out_specs=(pl.BlockSpec((None, tm, qk_w), lambda i, j: (i, j, 0)),
                   pl.BlockSpec((None, tm, qk_w), lambda i, j: (i, j, 0)),
                   pl.BlockSpec((None, tm, v_w), lambda i, j: (i, j, 0))),
        compiler_params=_cparams(("arbitrary", "arbitrary")),
        name="pre_mla",
    )(x, mods, g0, wd, gq, gkv, wq, wkv, cos_t, sin_t)


def _attn_kernel(*refs, n_seg):
    q_ref = refs[0]
    k_refs = refs[1:1 + n_seg]
    v_refs = refs[1 + n_seg:1 + 2 * n_seg]
    o_ref = refs[1 + 2 * n_seg]
    nt = (((1,), (1,)), ((), ()))
    for hd in range(MLA_HEADS):
        q = q_ref[:, hd * QK_PAD:(hd + 1) * QK_PAD]
        ss = [lax.dot_general(q, k[:, hd * QK_PAD:(hd + 1) * QK_PAD], nt, preferred_element_type=F32)
              for k in k_refs]
        m = ss[0].max(axis=-1, keepdims=True)
        for s in ss[1:]:
            m = jnp.maximum(m, s.max(axis=-1, keepdims=True))
        ps = [jnp.exp(s - m) for s in ss]
        l = ps[0].sum(axis=-1, keepdims=True)
        for p in ps[1:]:
            l = l + p.sum(axis=-1, keepdims=True)
        acc = None
        for p, v in zip(ps, v_refs):
            pv = _dot(p.astype(BF16), v[:, hd * V_DIM:(hd + 1) * V_DIM])
            acc = pv if acc is None else acc + pv
        o_ref[:, hd * V_DIM:(hd + 1) * V_DIM] = (acc / l).astype(BF16)


def _attention(q, ks, vs, tq):
    b, nq, qk_w = q.shape
    v_w = vs[0].shape[-1]
    kv_spec = lambda a: pl.BlockSpec((None,) + a.shape[1:], lambda i, j: (i, 0, 0))
    return pl.pallas_call(
        functools.partial(_attn_kernel, n_seg=len(ks)),
        out_shape=jax.ShapeDtypeStruct((b, nq, v_w), BF16),
        grid=(b, nq // tq),
        in_specs=[pl.BlockSpec((None, tq, qk_w), lambda i, j: (i, j, 0))]
                 + [kv_spec(a) for a in ks] + [kv_spec(a) for a in vs],
        out_specs=pl.BlockSpec((None, tq, v_w), lambda i, j: (i, j, 0)),
        compiler_params=_cparams(("arbitrary", "arbitrary")),
        name="mla_attention",
    )(q, *ks, *vs)


def _post_core(o, x, mod_ref, g1_ref, g2_ref, wr_ref, x1_ref, fin_ref, lg_ref):
    d = x.shape[-1]
    gate = _rows(_mod_chunk(mod_ref, 2, d), x)
    shift = _rows(_mod_chunk(mod_ref, 3, d), x)
    scale = _rows(_mod_chunk(mod_ref, 4, d), x)
    x1 = x + gate * _rms(o, g1_ref[...])
    fin = _rms(x1, g2_ref[...]) * (1.0 + scale) + shift
    x1_ref[...] = x1
    _pack_store(fin_ref, fin)
    lg_ref[...] = lax.dot_general(wr_ref[...], fin, (((1,), (1,)), ((), ())),
                                  preferred_element_type=F32, precision=lax.Precision.HIGHEST)


def _post_proj_kernel(o_ref, wo_ref, x_ref, mod_ref, g1_ref, g2_ref, wr_ref, x1_ref, fin_ref, lg_ref):
    o = _dot(o_ref[...], wo_ref[...])
    _post_core(o, x_ref[...], mod_ref, g1_ref, g2_ref, wr_ref, x1_ref, fin_ref, lg_ref)


def _post_glu_kernel(h_ref, yf_ref, yb_ref, dsk_ref, wg_ref, bg_ref, x_ref, mod_ref, g1_ref, g2_ref, wr_ref,
                     x1_ref, fin_ref, lg_ref):
    d = x_ref.shape[-1]
    y = h_ref[...] * dsk_ref[...] + yf_ref[...] + yb_ref[...]
    z = _dot(jax.nn.gelu(y, approximate=True).astype(BF16), wg_ref[...]) + bg_ref[...]
    o = z[:, :d] * jax.nn.sigmoid(z[:, d:])
    _post_core(o, x_ref[...], mod_ref, g1_ref, g2_ref, wr_ref, x1_ref, fin_ref, lg_ref)


def _post_mixer(kernel, tok_inputs, consts, x, mods, g1, g2, wr_t, tm, rows_per_mod, name):
    t, d = x.shape
    ne = wr_t.shape[0]
    tiles_per_mod = rows_per_mod // tm
    full = lambda a: pl.BlockSpec(a.shape, lambda i: (0,) * a.ndim)
    tile = lambda a: pl.BlockSpec((tm, a.shape[-1]), lambda i: (i, 0))
    mod_spec = pl.BlockSpec((None,) + mods.shape[1:], lambda i: (i // tiles_per_mod, 0, 0))
    return pl.pallas_call(
        kernel,
        out_shape=(jax.ShapeDtypeStruct((t, d), F32),
                   jax.ShapeDtypeStruct((t * PACK_ROWS, LANES), U32),
                   jax.ShapeDtypeStruct((ne, t), F32)),
        grid=(t // tm,),
        in_specs=[tile(a) for a in tok_inputs] + [full(a) for a in consts]
                 + [tile(x), mod_spec, full(g1), full(g2), full(wr_t)],
        out_specs=(tile(x), pl.BlockSpec((tm * PACK_ROWS, LANES), lambda i: (i, 0)),
                   pl.BlockSpec((ne, tm), lambda i: (0, i))),
        compiler_params=_cparams(("arbitrary",)),
        name=name,
    )(*tok_inputs, *consts, x, mods, g1, g2, wr_t)


def _route_kernel(lg_ref, bias_ref, eidx_ref, gate_ref, rank_ref, cnt_ref, tri_ref, base_ref):
    i = pl.program_id(0)
    ne, tt = lg_ref.shape
    gsz = ne // N_EXPERT_GROUPS
    shp = (N_EXPERT_GROUPS, gsz, tt)
    neg = -jnp.inf

    @pl.when(i == 0)
    def _():
        base_ref[...] = jnp.zeros_like(base_ref)
        r = lax.broadcasted_iota(jnp.int32, (tt, tt), 0)
        c = lax.broadcasted_iota(jnp.int32, (tt, tt), 1)
        tri_ref[...] = (r < c).astype(BF16)

    scores = jax.nn.sigmoid(lg_ref[...])
    s3 = scores.reshape(shp)
    b3 = (scores + bias_ref[...]).reshape(shp)
    io_e = lax.broadcasted_iota(jnp.int32, shp, 1)
    io_g = lax.broadcasted_iota(jnp.int32, shp, 0)
    io_flat = io_g * gsz + io_e
    m1 = b3.max(axis=1, keepdims=True)
    i1 = jnp.where(b3 == m1, io_e, gsz).min(axis=1, keepdims=True)
    m2 = jnp.where(io_e == i1, neg, b3).max(axis=1, keepdims=True)
    cur = jnp.broadcast_to(m1 + m2, shp)
    gsel = jnp.zeros(shp, jnp.bool_)
    for _ in range(TOPK_GROUPS):
        m = cur.max(axis=0, keepdims=True)
        gi = jnp.where(cur == m, io_g, N_EXPERT_GROUPS).min(axis=0, keepdims=True)
        hit = io_g == gi
        gsel = jnp.logical_or(gsel, hit)
        cur = jnp.where(hit, neg, cur)
    cand = jnp.where(gsel, b3, neg)
    sel = jnp.zeros(shp, jnp.bool_)
    eids, gts = [], []
    for _ in range(TOP_K):
        m = cand.max(axis=0, keepdims=True).max(axis=1, keepdims=True)
        ei = jnp.where(cand == m, io_flat, ne).min(axis=0, keepdims=True).min(axis=1, keepdims=True)
        hit = io_flat == ei
        gts.append(jnp.where(hit, s3, 0.0).sum(axis=0, keepdims=True).sum(axis=1, keepdims=True))
        eids.append(ei)
        sel = jnp.logical_or(sel, hit)
        cand = jnp.where(hit, neg, cand)
    gsum = gts[0]
    for g in gts[1:]:
        gsum = gsum + g
    self32 = sel.astype(F32).reshape(ne, tt)
    cnt = _dot(self32.astype(BF16), tri_ref[...]) + base_ref[...]
    cnt3 = cnt.reshape(shp)
    for k in range(TOP_K):
        hit = io_flat == eids[k]
        rk = jnp.where(hit, cnt3, 0.0).sum(axis=0, keepdims=True).sum(axis=1, keepdims=True)
        rank_ref[k:k + 1, :] = rk.reshape(1, tt).astype(jnp.int32)
        eidx_ref[k:k + 1, :] = eids[k].reshape(1, tt)
        gate_ref[k:k + 1, :] = (gts[k] / gsum * ROUTED_SCALE).reshape(1, tt)
    base_new = base_ref[...] + self32.sum(axis=1, keepdims=True)
    base_ref[...] = base_new
    cnt_ref[...] = jnp.broadcast_to(base_new, cnt_ref.shape)


def _route(logits_t, bias, tt):
    ne, t = logits_t.shape
    out_i = jax.ShapeDtypeStruct((TOP_K, t), jnp.int32)
    row = pl.BlockSpec((TOP_K, tt), lambda i: (0, i))
    return pl.pallas_call(
        _route_kernel,
        out_shape=(out_i, jax.ShapeDtypeStruct((TOP_K, t), F32), out_i,
                   jax.ShapeDtypeStruct((ne, 128), F32)),
        grid=(t // tt,),
        in_specs=[pl.BlockSpec((ne, tt), lambda i: (0, i)),
                  pl.BlockSpec((ne, 1), lambda i: (0, 0))],
        out_specs=(row, row, row, pl.BlockSpec((ne, 128), lambda i: (0, 0))),
        scratch_shapes=[pltpu.VMEM((tt, tt), BF16), pltpu.VMEM((ne, 1), F32)],
        compiler_params=_cparams(("arbitrary",)),
        name="moe_route",
    )(logits_t, bias.reshape(ne, 1))


def _dest_kernel(eidx_ref, rank_ref, start_ref, dest_ref):
    kk, tt = eidx_ref.shape
    ne = start_ref.shape[0]
    n_chunk, _, r = dest_ref.shape
    io_e = lax.broadcasted_iota(jnp.int32, (ne, tt), 0)
    start = start_ref[...]
    for k in range(kk):
        hit = io_e == eidx_ref[k:k + 1, :]
        dk = jnp.where(hit, start, 0).sum(axis=0, keepdims=True) + rank_ref[k:k + 1, :]
        for c in range(n_chunk):
            dest_ref[c, k:k + 1, :] = dk[:, c * r:(c + 1) * r]


def _dest_rows(eidx_t, rank_t, start, tt, r):
    kk, t = eidx_t.shape
    ne = start.shape[0]
    return pl.pallas_call(
        _dest_kernel,
        out_shape=jax.ShapeDtypeStruct((t // r, kk, r), jnp.int32),
        grid=(t // tt,),
        in_specs=[pl.BlockSpec((kk, tt), lambda i: (0, i)),
                  pl.BlockSpec((kk, tt), lambda i: (0, i)),
                  pl.BlockSpec((ne, 1), lambda i: (0, 0))],
        out_specs=pl.BlockSpec((tt // r, kk, r), lambda i: (i, 0, 0)),
        compiler_params=_cparams(("arbitrary",)),
        name="moe_dest",
    )(eidx_t, rank_t, start.reshape(ne, 1))


SC_CHUNK = 64


def _sc_mesh():
    return plsc.VectorSubcoreMesh(core_axis_name="c", subcore_axis_name="s")


def _sc_workers():
    info = plsc.get_sparse_core_info()
    return info.num_cores, info.num_cores * info.num_subcores


def _sc_scatter_rows(rows, dest, n_out):
    t = rows.shape[0]
    n_chunk, kk, r = dest.shape
    nc, nw = _sc_workers()
    cpw = n_chunk // nw
    assert cpw * nw == n_chunk and cpw % 2 == 0 and n_chunk * r == t

    @functools.partial(
        pl.kernel, mesh=_sc_mesh(),
        out_type=jax.ShapeDtypeStruct((n_out,) + rows.shape[1:], rows.dtype),
        scratch_types=[pltpu.VMEM((2, kk, r), jnp.int32), pltpu.VMEM((2, r) + rows.shape[1:], rows.dtype),
                       pltpu.SemaphoreType.DMA((2,)), pltpu.SemaphoreType.DMA((2,))])
    def scatter(rows_hbm, dest_hbm, out_hbm, idx_v, rows_v, load_sem, scat_sem):
        c0 = (lax.axis_index("s") * nc + lax.axis_index("c")) * cpw

        def loads(c, b):
            return (pltpu.make_async_copy(dest_hbm.at[c], idx_v.at[b], load_sem.at[b]),
                    pltpu.make_async_copy(rows_hbm.at[pl.ds(c * r, r)], rows_v.at[b], load_sem.at[b]))

        def scat(b, k):
            return pltpu.make_async_copy(rows_v.at[b], out_hbm.at[idx_v.at[b, k]], scat_sem.at[b])

        for cp in loads(c0, 0):
            cp.start()

        @pl.loop(0, cpw, step=2)
        def _(ci):
            for b in range(2):
                c = c0 + ci + b
                for cp in loads(c, b):
                    cp.wait()
                for k in range(kk):
                    scat(b, k).start()

                @pl.when(ci + b >= 1)
                def _():
                    for k in range(kk):
                        scat(1 - b, k).wait()

                @pl.when(ci + b + 1 < cpw)
                def _():
                    for cp in loads(c + 1, 1 - b):
                        cp.start()

        for k in range(kk):
            scat((cpw - 1) % 2, k).wait()

    return scatter(rows, dest)


def _sc_gather_rows(src, dest):
    n_chunk, kk, r = dest.shape
    t = n_chunk * r
    nc, nw = _sc_workers()
    cpw = n_chunk // nw
    nbuf = 3
    assert cpw * nw == n_chunk and kk > nbuf

    @functools.partial(
        pl.kernel, mesh=_sc_mesh(),
        out_type=jax.ShapeDtypeStruct((kk, t) + src.shape[1:], src.dtype),
        scratch_types=[pltpu.VMEM((kk, r), jnp.int32), pltpu.VMEM((nbuf, r) + src.shape[1:], src.dtype),
                       pltpu.SemaphoreType.DMA((nbuf,)), pltpu.SemaphoreType.DMA((nbuf,))])
    def gather(src_hbm, dest_hbm, out_hbm, idx_v, rows_v, get_sem, put_sem):
        c0 = (lax.axis_index("s") * nc + lax.axis_index("c")) * cpw

        @pl.loop(0, cpw)
        def _(ci):
            c = c0 + ci
            pltpu.sync_copy(dest_hbm.at[c], idx_v)

            def get(k):
                return pltpu.make_async_copy(src_hbm.at[idx_v.at[k]], rows_v.at[k % nbuf], get_sem.at[k % nbuf])

            def put(k):
                return pltpu.make_async_copy(rows_v.at[k % nbuf], out_hbm.at[k, pl.ds(c * r, r)],
                                             put_sem.at[k % nbuf])

            for k in range(nbuf - 1):
                get(k).start()
            for k in range(kk):
                get(k).wait()
                put(k).start()
                if k + nbuf - 1 < kk:
                    if k >= 1:
                        put(k - 1).wait()
                    get(k + nbuf - 1).start()
            for k in range(kk - nbuf, kk):
                put(k).wait()

    return gather(src, dest)


def _expert_kernel(be_ref, nu_ref, x_ref, wg_ref, wu_ref, wd_ref, o_ref, wgu_s, wd_s):
    i = pl.program_id(0)
    tb = o_ref.shape[0] // PACK_ROWS

    @pl.when(i < nu_ref[0])
    def _():
        @pl.when(jnp.logical_or(i == 0, be_ref[i] != be_ref[jnp.maximum(i - 1, 0)]))
        def _():
            wgu_s[:, :D_EXPERT] = wg_ref[...].astype(BF16)
            wgu_s[:, D_EXPERT:] = wu_ref[...].astype(BF16)
            wd_s[...] = wd_ref[...].astype(BF16)

        x = jnp.concatenate([v.astype(BF16) for v in _unpack_load(x_ref, tb)], axis=-1)
        gu = _dot(x, wgu_s[...])
        g = gu[:, :D_EXPERT]
        h = g * jax.nn.sigmoid(g) * gu[:, D_EXPERT:]
        _pack_store(o_ref, _dot(h.astype(BF16), wd_s[...]))


def _experts(xs, blk_e, n_used, w_gate, w_up, w_down, layer, tb):
    rows = xs.shape[0] // PACK_ROWS
    _, ne, d, de = w_gate.shape
    nb = rows // tb
    row_map = lambda i, be, nu: (jnp.minimum(i, nu[0] - 1), 0)
    w_map = lambda i, be, nu: (layer, be[i], 0, 0)
    grid_spec = pltpu.PrefetchScalarGridSpec(
        num_scalar_prefetch=2,
        grid=(nb,),
        in_specs=[pl.BlockSpec((tb * PACK_ROWS, LANES), row_map),
                  pl.BlockSpec((None, None, d, de), w_map),
                  pl.BlockSpec((None, None, d, de), w_map),
                  pl.BlockSpec((None, None, de, d), w_map)],
        out_specs=pl.BlockSpec((tb * PACK_ROWS, LANES), row_map),
        scratch_shapes=[pltpu.VMEM((d, 2 * de), BF16), pltpu.VMEM((de, d), BF16)],
    )
    return pl.pallas_call(
        _expert_kernel,
        out_shape=jax.ShapeDtypeStruct(xs.shape, U32),
        grid_spec=grid_spec,
        compiler_params=_cparams(("arbitrary",)),
        name="moe_experts",
    )(blk_e, n_used, xs, w_gate, w_up, w_down)


def _combine_kernel(yk_ref, gate_ref, fin_ref, shgu_ref, shd_ref, x1_ref, mod_ref, g3_ref, o_ref):
    tm, d = x1_ref.shape
    gates = gate_ref[...]
    blocks = None
    for k in range(TOP_K):
        gk = gates[:, k:k + 1]
        terms = [gk * v for v in _unpack_load(yk_ref, tm, lead=(k,))]
        blocks = terms if blocks is None else [a + b for a, b in zip(blocks, terms)]
    fin = jnp.concatenate([v.astype(BF16) for v in _unpack_load(fin_ref, tm)], axis=-1)
    gu = _dot(fin, shgu_ref[...])
    g = gu[:, :D_EXPERT]
    hsh = g * jax.nn.sigmoid(g) * gu[:, D_EXPERT:]
    f = jnp.concatenate(blocks, axis=-1) + _dot(hsh.astype(BF16), shd_ref[...])
    x1 = x1_ref[...]
    o_ref[...] = x1 + _rows(_mod_chunk(mod_ref, 5, d), x1) * _rms(f, g3_ref[...])


def _combine(yk, gates, fin, shgu, shd, x1, mods, g3, tm, rows_per_mod, tok_off):
    t, d = x1.shape
    off = tok_off // tm
    tiles_per_mod = rows_per_mod // tm
    full = lambda a: pl.BlockSpec(a.shape, lambda i: (0,) * a.ndim)
    return pl.pallas_call(
        _combine_kernel,
        out_shape=jax.ShapeDtypeStruct((t, d), F32),
        grid=(t // tm,),
        in_specs=[pl.BlockSpec((TOP_K, tm * PACK_ROWS, LANES), lambda i: (0, i + off, 0)),
                  pl.BlockSpec((tm, TOP_K), lambda i: (i + off, 0)),
                  pl.BlockSpec((tm * PACK_ROWS, LANES), lambda i: (i + off, 0)),
                  full(shgu), full(shd),
                  pl.BlockSpec((tm, d), lambda i: (i, 0)),
                  pl.BlockSpec((None,) + mods.shape[1:], lambda i: (i // tiles_per_mod, 0, 0)),
                  full(g3)],
        out_specs=pl.BlockSpec((tm, d), lambda i: (i, 0)),
        compiler_params=_cparams(("arbitrary",)),
        name="moe_combine",
    )(yk, gates, fin, shgu, shd, x1, mods, g3)


def _moe(fin, logits_t, bias, w_gate, w_up, w_down, layer, tb):
    t = fin.shape[0] // PACK_ROWS
    ne = w_gate.shape[1]
    tt = 512
    eidx_t, gates_t, rank_t, cnt = _route(logits_t, bias, tt)
    counts = cnt[:, 0].astype(jnp.int32)
    padded = (counts + tb - 1) // tb * tb
    pad_end = jnp.cumsum(padded)
    pad_start = pad_end - padded
    nb = (t * TOP_K) // tb + ne
    n_used = pad_end[-1] // tb
    blk_start = jnp.arange(nb, dtype=jnp.int32) * tb
    blk = jnp.sum(pad_end[None, :] <= jnp.minimum(blk_start, pad_end[-1] - 1)[:, None], axis=1)
    blk_e = jnp.minimum(blk, ne - 1).astype(jnp.int32)
    dest = _dest_rows(eidx_t, rank_t, pad_start, tt, SC_CHUNK)
    xs = _sc_scatter_rows(fin.reshape(t, PACK_ROWS, LANES), dest, nb * tb)
    ys = _experts(xs.reshape(nb * tb * PACK_ROWS, LANES), blk_e, n_used.reshape(1).astype(jnp.int32),
                  w_gate, w_up, w_down, layer, tb)
    yk = _sc_gather_rows(ys.reshape(nb * tb, PACK_ROWS, LANES), dest)
    return yk.reshape(TOP_K, t * PACK_ROWS, LANES), gates_t.T


def _pre_s5_kernel(x_ref, mod_ref, g0_ref, h_ref):
    d = x_ref.shape[-1]
    x = x_ref[...]
    h_ref[...] = (_rms(x, g0_ref[...]) * (1.0 + _rows(_mod_chunk(mod_ref, 1, d), x))
                  + _rows(_mod_chunk(mod_ref, 0, d), x))


def _pre_s5(x, mods, g0, tm):
    t, d = x.shape
    return pl.pallas_call(
        _pre_s5_kernel,
        out_shape=jax.ShapeDtypeStruct((t, d), F32),
        grid=(t // tm,),
        in_specs=[pl.BlockSpec((tm, d), lambda i: (i, 0)),
                  pl.BlockSpec(mods.shape, lambda i: (0, 0)),
                  pl.BlockSpec(g0.shape, lambda i: (0, 0))],
        out_specs=pl.BlockSpec((tm, d), lambda i: (i, 0)),
        compiler_params=_cparams(("arbitrary",)),
        name="pre_s5",
    )(x, mods, g0)


def _s5_scan_kernel(hc_ref, hl_ref, bm_ref, cm_ref, lam_ref, y_ref, bu_ref, st_ref, *, n_ctx_chunks):
    dr = pl.program_id(1)
    j = pl.program_id(2)
    tc, nb, cw = hl_ref.shape
    half = bu_ref.shape[1] // 2

    @pl.when(j == 0)
    def _():
        st_ref[...] = jnp.zeros_like(st_ref)

    def run(u_ref, emit):
        u = u_ref[...].reshape(tc * nb, cw).astype(BF16)
        bu_ref[...] = _dot(u, bm_ref[...])
        lr = jnp.broadcast_to(lam_ref[0:1, :], (nb, half))
        li = jnp.broadcast_to(lam_ref[1:2, :], (nb, half))

        def step(i, carry):
            xr, xi = carry
            t = jnp.where(dr == 0, i, tc - 1 - i)
            r0 = pl.multiple_of(t * nb, nb)
            nr = lr * xr - li * xi + bu_ref[pl.ds(r0, nb), 0:half]
            ni = lr * xi + li * xr + bu_ref[pl.ds(r0, nb), half:2 * half]
            bu_ref[pl.ds(r0, nb), 0:half] = nr
            bu_ref[pl.ds(r0, nb), half:2 * half] = ni
            return nr, ni

        xr, xi = lax.fori_loop(0, tc, step, (st_ref[:, 0:half], st_ref[:, half:2 * half]), unroll=4)
        st_ref[:, 0:half] = xr
        st_ref[:, half:2 * half] = xi
        if emit:
            y_ref[...] = _dot(bu_ref[...].astype(BF16), cm_ref[...]).reshape(tc, nb, cw)

    @pl.when(j < n_ctx_chunks)
    def _():
        run(hc_ref, False)

    @pl.when(j >= n_ctx_chunks)
    def _():
        run(hl_ref, True)


def _s5_scan(h_ctx, h_lat, bm, cm, lam, tc):
    nc, nb, d = h_ctx.shape
    nl = h_lat.shape[0]
    cw = S5_GROUPS_PER_BLOCK * S5_GROUP
    sw = 2 * S5_GROUPS_PER_BLOCK * S5_STATE
    ncc, nlc = nc // tc, nl // tc

    def ctx_map(g, dr, j):
        return (jnp.where(dr == 0, jnp.minimum(j, ncc - 1), jnp.maximum(ncc - 1 - j, 0)), 0, g)

    def lat_idx(dr, j):
        jj = jnp.maximum(j - ncc, 0)
        return jnp.where(dr == 0, jj, nlc - 1 - jj)

    return pl.pallas_call(
        functools.partial(_s5_scan_kernel, n_ctx_chunks=ncc),
        out_shape=jax.ShapeDtypeStruct((2, nl, nb, d), F32),
        grid=(d // cw, 2, ncc + nlc),
        in_specs=[pl.BlockSpec((tc, nb, cw), ctx_map),
                  pl.BlockSpec((tc, nb, cw), lambda g, dr, j: (lat_idx(dr, j), 0, g)),
                  pl.BlockSpec((None, None, cw, sw), lambda g, dr, j: (dr, g, 0, 0)),
                  pl.BlockSpec((None, None, sw, cw), lambda g, dr, j: (dr, g, 0, 0)),
                  pl.BlockSpec((None, None, 2, sw // 2), lambda g, dr, j: (dr, g, 0, 0))],
        out_specs=pl.BlockSpec((None, tc, nb, cw), lambda g, dr, j: (dr, lat_idx(dr, j), 0, g)),
        scratch_shapes=[pltpu.VMEM((tc * nb, sw), F32), pltpu.VMEM((nb, sw), F32)],
        compiler_params=_cparams(("arbitrary", "arbitrary", "arbitrary")),
        name="s5_scan",
    )(h_ctx, h_lat, bm, cm, lam)


def _s5_params(lam_re, lam_im, log_step, b_re, b_im, c_re, c_im):
    g, p = lam_re.shape[1:]
    gb = S5_GROUPS_PER_BLOCK
    nblk = g // gb
    step = jnp.exp(log_step)[..., None]
    mag = jnp.exp(lam_re * step)
    lb_re = mag * jnp.cos(lam_im * step)
    lb_im = mag * jnp.sin(lam_im * step)
    den = lam_re * lam_re + lam_im * lam_im
    f_re = ((lb_re - 1.0) * lam_re + lb_im * lam_im) / den
    f_im = (lb_im * lam_re - (lb_re - 1.0) * lam_im) / den
    bb_re = f_re[..., None] * b_re - f_im[..., None] * b_im
    bb_im = f_re[..., None] * b_im + f_im[..., None] * b_re
    eye = jnp.eye(gb, dtype=F32)

    def in_map(w):
        w = w.reshape(2, nblk, gb, p, S5_GROUP)
        return jnp.einsum("dnapi,ab->dnaibp", w, eye).reshape(2, nblk, gb * S5_GROUP, gb * p)

    def out_map(w):
        w = w.reshape(2, nblk, gb, S5_GROUP, p)
        return jnp.einsum("dnaip,ab->dnapbi", w, eye).reshape(2, nblk, gb * p, gb * S5_GROUP)

    bm = jnp.concatenate([in_map(bb_re), in_map(bb_im)], axis=-1).astype(BF16)
    cm = jnp.concatenate([out_map(c_re), out_map(-c_im)], axis=-2).astype(BF16)
    lam = jnp.stack([lb_re.reshape(2, nblk, gb * p), lb_im.reshape(2, nblk, gb * p)], axis=2)
    return bm, cm, lam


def _rope_tables(n_tokens):
    rows = n_tokens // GRID_W
    row = jnp.repeat(jnp.arange(rows), GRID_W).astype(F32)
    col = jnp.tile(jnp.arange(GRID_W), rows).astype(F32)
    n_freq = ROPE_DIM // 4
    inv_freq = ROPE_BASE ** (-jnp.arange(n_freq, dtype=F32) / n_freq)
    ang = jnp.concatenate([row[:, None] * inv_freq, col[:, None] * inv_freq], axis=-1)
    cos, sin = jnp.cos(ang), jnp.sin(ang)
    z = jnp.zeros((n_tokens, 128 - ROPE_DIM), F32)
    return (jnp.concatenate([cos, cos, z], axis=-1), jnp.concatenate([-sin, sin, z], axis=-1))


def _split_pairs(w):
    ev, od = w[..., 0::2], w[..., 1::2]
    z = jnp.zeros(w.shape[:-1] + (128 - ROPE_DIM,), w.dtype)
    return jnp.concatenate([ev, od, z], axis=-1), jnp.concatenate([od, ev, z], axis=-1)


def _mla_weights(w_dqkv, w_uq, w_ukv):
    kp, kps = _split_pairs(w_dqkv[:, Q_LORA + KV_LORA:])
    wd = jnp.concatenate([w_dqkv[:, :Q_LORA + KV_LORA], kp, kps], axis=-1).astype(BF16)
    wq3 = w_uq.reshape(Q_LORA, MLA_HEADS, NOPE_DIM + ROPE_DIM)
    qp, qps = _split_pairs(wq3[:, :, NOPE_DIM:])
    wq = jnp.concatenate([wq3[:, :, :NOPE_DIM].reshape(Q_LORA, -1), qp.reshape(Q_LORA, -1),
                          qps.reshape(Q_LORA, -1)], axis=-1).astype(BF16)
    wkv3 = w_ukv.reshape(KV_LORA, MLA_HEADS, NOPE_DIM + V_DIM)
    wkv = jnp.concatenate([wkv3[:, :, :NOPE_DIM].reshape(KV_LORA, -1),
                           wkv3[:, :, NOPE_DIM:].reshape(KV_LORA, -1)], axis=-1).astype(BF16)
    return wd, wq, wkv


@jax.jit
def kernel(x, c, ctx, c_ctx, ada_w, ada_b, norm_g, mla_w_dqkv, mla_g_q, mla_g_kv, mla_w_uq, mla_w_ukv, mla_w_o, s5_lam_re, s5_lam_im, s5_log_step, s5_b_re, s5_b_im, s5_c_re, s5_c_im, s5_d, s5_w_glu, s5_b_glu, moe_w_router, moe_bias, moe_w_gate, moe_w_up, moe_w_down, sh_w_gate, sh_w_up, sh_w_down):
    b, l, d = x.shape
    n_ctx = ctx.shape[1]
    assert ada_w.shape[0] == 2 and b % 8 == 0
    tm = 256
    tb = 256
    row = lambda v: v.reshape(1, -1)

    n_rows = (b + 1 + 7) // 8 * 8
    cvec = jnp.zeros((n_rows, d), F32).at[:b].set(c).at[b].set(c_ctx)
    mods = _ada_mods(cvec, ada_w, ada_b)

    def shared_weights(i):
        shgu = jnp.concatenate([sh_w_gate[i], sh_w_up[i]], axis=-1).astype(BF16)
        return shgu, sh_w_down[i].astype(BF16)

    mod_lat = mods[0, :b].reshape(b, 1, N_MOD * d)
    mod_ctx = mods[0, b].reshape(1, 1, N_MOD * d)
    wd, wq, wkv = _mla_weights(mla_w_dqkv[0], mla_w_uq[0], mla_w_ukv[0])
    cos_l, sin_l = _rope_tables(l)
    cos_c = jnp.concatenate([jnp.ones((n_ctx, ROPE_DIM), F32), jnp.zeros((n_ctx, 128 - ROPE_DIM), F32)], -1)
    sin_c = jnp.zeros((n_ctx, 128), F32)
    pre = functools.partial(_pre_mla, g0=row(norm_g[0, 0]), wd=wd, gq=row(mla_g_q[0]), gkv=row(mla_g_kv[0]),
                            wq=wq, wkv=wkv, tm=tm)
    q_c, k_c, v_c = pre(ctx, mod_ctx, cos_t=cos_c, sin_t=sin_c)
    q_l, k_l, v_l = pre(x, mod_lat, cos_t=cos_l, sin_t=sin_l)
    o_l = _attention(q_l, [k_c, k_l], [v_c, v_l], tm)
    o_c = _attention(q_c, [k_c], [v_c], n_ctx)

    wo = mla_w_o[0].astype(BF16)
    wr_t = moe_w_router[0].T
    g1, g2, g3 = row(norm_g[0, 1]), row(norm_g[0, 2]), row(norm_g[0, 3])
    post = functools.partial(_post_mixer, _post_proj_kernel, consts=[wo], g1=g1, g2=g2, wr_t=wr_t, tm=tm,
                             name="post_mla")
    x1_c, fin_c, lg_c = post([o_c.reshape(b * n_ctx, -1)], x=ctx.reshape(b * n_ctx, d), mods=mod_ctx,
                             rows_per_mod=b * n_ctx)
    x1_l, fin_l, lg_l = post([o_l.reshape(b * l, -1)], x=x.reshape(b * l, d), mods=mod_lat, rows_per_mod=l)
    fin = jnp.concatenate([fin_c, fin_l], axis=0)
    lg = jnp.concatenate([lg_c, lg_l], axis=1)
    shgu, shd = shared_weights(0)
    yk, gates = _moe(fin, lg, moe_bias[0], moe_w_gate, moe_w_up, moe_w_down, 0, tb)
    comb = functools.partial(_combine, yk, gates, fin, shgu, shd, g3=g3, tm=tm)
    x2_c = comb(x1=x1_c, mods=mod_ctx, rows_per_mod=b * n_ctx, tok_off=0)
    x2_l = comb(x1=x1_l, mods=mod_lat, rows_per_mod=l, tok_off=b * n_ctx)

    xt_l = jnp.swapaxes(x2_l.reshape(b, l, d), 0, 1).reshape(l * b, d)
    xt_c = jnp.swapaxes(x2_c.reshape(b, n_ctx, d), 0, 1).reshape(n_ctx * b, d)
    mod_lat = mods[1, :b]
    mod_ctx = jnp.broadcast_to(mods[1, b][None], (b, N_MOD * d))
    g0 = row(norm_g[1, 0])
    h_l = _pre_s5(xt_l, mod_lat, g0, 512)
    h_c = _pre_s5(xt_c, mod_ctx, g0, 512)
    bm, cm, lam = _s5_params(s5_lam_re[0], s5_lam_im[0], s5_log_step[0], s5_b_re[0], s5_b_im[0],
                             s5_c_re[0], s5_c_im[0])
    y = _s5_scan(h_c.reshape(n_ctx, b, d), h_l.reshape(l, b, d), bm, cm, lam, 64)
    g1, g2, g3 = row(norm_g[1, 1]), row(norm_g[1, 2]), row(norm_g[1, 3])
    x1, fin, lg = _post_mixer(
        _post_glu_kernel, [h_l, y[0].reshape(l * b, d), y[1].reshape(l * b, d)],
        [row(s5_d[0]), s5_w_glu[0].astype(BF16), row(s5_b_glu[0])],
        x=xt_l, mods=mod_lat[None], g1=g1, g2=g2, wr_t=moe_w_router[1].T, tm=tm, rows_per_mod=l * b,
        name="post_s5")
    shgu, shd = shared_weights(1)
    yk, gates = _moe(fin, lg, moe_bias[1], moe_w_gate, moe_w_up, moe_w_down, 1, tb)
    x2 = _combine(yk, gates, fin, shgu, shd, x1, mod_lat[None], g3, tm, rows_per_mod=l * b, tok_off=0)
    return jnp.swapaxes(x2.reshape(l, b, d), 0, 1)
```

```python
import functools

import jax
import jax.numpy as jnp
from jax import lax
from jax.experimental import pallas as pl
from jax.experimental.pallas import tpu as pltpu
from jax.experimental.pallas import tpu_sc as plsc

F32 = jnp.float32
BF16 = jnp.bfloat16
U32 = jnp.uint32

N_MOD = 6
NORM_EPS = 1e-6
LOG2_E = 1.4426950408889634
GRID_W = 64
MLA_HEADS = 8
Q_LORA = 384
KV_LORA = 256
NOPE_DIM = 128
ROPE_DIM = 64
V_DIM = 128
V_PAD = 256
ROPE_BASE = 10000.0
QK_PAD = 256
S5_GROUP = 16
S5_STATE = 64
S5_GROUPS_PER_BLOCK = 8
N_EXPERTS = 64
TOP_K = 8
N_EXPERT_GROUPS = 8
TOPK_GROUPS = 4
D_EXPERT = 256
ROUTED_SCALE = 2.5

VMEM_LIMIT = 56 * 1024 * 1024


def _cparams(sem):
    return pltpu.CompilerParams(dimension_semantics=sem, vmem_limit_bytes=VMEM_LIMIT)


def _rms(x, g):
    return x * lax.rsqrt(jnp.mean(x * x, axis=-1, keepdims=True) + NORM_EPS) * g


def _rows(v, like):
    r = v.shape[0]
    if r == 1:
        return v
    tm, d = like.shape
    return jnp.broadcast_to(v[None], (tm // r, r, d)).reshape(tm, d)


def _mod_chunk(mod_ref, j, d):
    return mod_ref[:, j * d:(j + 1) * d]


def _dot(a, b):
    return jnp.dot(a, b, preferred_element_type=F32)


PACK_ROWS = 4
LANES = 128


def _pack_store(ref, val, lead=()):
    tm = val.shape[0]
    bits = lax.bitcast_convert_type(val.astype(BF16).astype(F32), U32)
    for s in range(PACK_ROWS):
        lo = bits[:, s * LANES:(s + 1) * LANES] >> 16
        hi = bits[:, (s + PACK_ROWS) * LANES:(s + PACK_ROWS + 1) * LANES] & jnp.uint32(0xFFFF0000)
        ref[lead + (pl.ds(s, tm, stride=PACK_ROWS), slice(None))] = lo | hi


def _unpack_load(ref, tm, lead=()):
    los, his = [], []
    for s in range(PACK_ROWS):
        w = ref[lead + (pl.ds(s, tm, stride=PACK_ROWS), slice(None))]
        los.append(lax.bitcast_convert_type(w << 16, F32))
        his.append(lax.bitcast_convert_type(w & jnp.uint32(0xFFFF0000), F32))
    return los + his


def _ada_kernel(c_ref, w_ref, b_ref, o_ref):
    c = c_ref[...]
    s = c * jax.nn.sigmoid(c)
    o_ref[...] = jnp.dot(s, w_ref[...], preferred_element_type=F32,
                         precision=lax.Precision.HIGHEST) + b_ref[...]


def _ada_mods(cvec, ada_w, ada_b):
    depth, d, n = ada_w.shape
    rows = cvec.shape[0]
    tn = 1536
    return pl.pallas_call(
        _ada_kernel,
        out_shape=jax.ShapeDtypeStruct((depth, rows, n), F32),
        grid=(depth, n // tn),
        in_specs=[pl.BlockSpec((rows, d), lambda l, j: (0, 0)),
                  pl.BlockSpec((None, d, tn), lambda l, j: (l, 0, j)),
                  pl.BlockSpec((None, 1, tn), lambda l, j: (l, 0, j))],
        out_specs=pl.BlockSpec((None, rows, tn), lambda l, j: (l, 0, j)),
        compiler_params=_cparams(("arbitrary", "arbitrary")),
        name="ada_mods",
    )(cvec, ada_w, ada_b.reshape(depth, 1, n))


def _pre_mla_kernel(x_ref, mod_ref, g0_ref, wd_ref, gq_ref, gkv_ref, wq_ref, wkv_ref, cos_ref, sin_ref,
                    q_ref, k_ref, v_ref):
    d = x_ref.shape[-1]
    x = x_ref[...]
    h = _rms(x, g0_ref[...]) * (1.0 + _mod_chunk(mod_ref, 1, d)) + _mod_chunk(mod_ref, 0, d)
    a = _dot(h.astype(BF16), wd_ref[...])
    cq = _rms(a[:, :Q_LORA], gq_ref[...])
    ckv = _rms(a[:, Q_LORA:Q_LORA + KV_LORA], gkv_ref[...])
    cos = cos_ref[...]
    sin = sin_ref[...]
    o = Q_LORA + KV_LORA
    k_rot = (a[:, o:o + 128] * cos + a[:, o + 128:o + 256] * sin).astype(BF16)
    qa = _dot(cq.astype(BF16), wq_ref[...])
    kva = _dot(ckv.astype(BF16), wkv_ref[...])
    hw = MLA_HEADS * 128
    scale = (NOPE_DIM + ROPE_DIM) ** -0.5 * LOG2_E
    for hd in range(MLA_HEADS):
        lo = hd * 128
        q_rot = qa[:, hw + lo:hw + lo + 128] * cos + qa[:, 2 * hw + lo:2 * hw + lo + 128] * sin
        q_ref[:, hd * QK_PAD:hd * QK_PAD + 128] = (qa[:, lo:lo + 128] * scale).astype(BF16)
        q_ref[:, hd * QK_PAD + 128:(hd + 1) * QK_PAD] = (q_rot * scale).astype(BF16)
        k_ref[:, hd * QK_PAD:hd * QK_PAD + 128] = kva[:, lo:lo + 128].astype(BF16)
        k_ref[:, hd * QK_PAD + 128:(hd + 1) * QK_PAD] = k_rot
        v_ref[:, hd * V_PAD:hd * V_PAD + V_DIM] = kva[:, hw + lo:hw + lo + 128].astype(BF16)
        v_ref[:, hd * V_PAD + V_DIM:(hd + 1) * V_PAD] = jnp.ones((x.shape[0], V_PAD - V_DIM), BF16)


def _pre_mla(x, mods, g0, wd, gq, gkv, wq, wkv, cos_t, sin_t, tm):
    b, n, d = x.shape
    nb_mod = mods.shape[0]
    full = lambda a: pl.BlockSpec(a.shape, lambda i, j: (0,) * a.ndim)
    mod_map = (lambda i, j: (i, 0, 0)) if nb_mod > 1 else (lambda i, j: (0, 0, 0))
    qk_w = MLA_HEADS * QK_PAD
    v_w = MLA_HEADS * V_PAD
    return pl.pallas_call(
        _pre_mla_kernel,
        out_shape=(jax.ShapeDtypeStruct((b, n, qk_w), BF16),
                   jax.ShapeDtypeStruct((b, n, qk_w), BF16),
                   jax.ShapeDtypeStruct((b, n, v_w), BF16)),
        grid=(b, n // tm),
        in_specs=[pl.BlockSpec((None, tm, d), lambda i, j: (i, j, 0)),
                  pl.BlockSpec((None, 1, mods.shape[-1]), mod_map),
                  full(g0), full(wd), full(gq), full(gkv), full(wq), full(wkv),
                  pl.BlockSpec((tm, 128), lambda i, j: (j, 0)),
                  pl.BlockSpec((tm, 128), lambda i, j: (j, 0))],
        out_specs=(pl.BlockSpec((None, tm, qk_w), lambda i, j: (i, j, 0)),
                   pl.BlockSpec((None, tm, qk_w), lambda i, j: (i, j, 0)),
                   pl.BlockSpec((None, tm, v_w), lambda i, j: (i, j, 0))),
        compiler_params=_cparams(("arbitrary", "arbitrary")),
        name="pre_mla",
    )(x, mods, g0, wd, gq, gkv, wq, wkv, cos_t, sin_t)


def _attn_kernel(*refs, n_seg):
    q_ref = refs[0]
    k_refs = refs[1:1 + n_seg]
    v_refs = refs[1 + n_seg:1 + 2 * n_seg]
    o_ref = refs[1 + 2 * n_seg]
    nt = (((1,), (1,)), ((), ()))
    for hd in range(MLA_HEADS):
        q = q_ref[:, hd * QK_PAD:(hd + 1) * QK_PAD]
        ss = [lax.dot_general(q, k[:, hd * QK_PAD:(hd + 1) * QK_PAD], nt, preferred_element_type=F32)
              for k in k_refs]
        m = ss[0].max(axis=-1, keepdims=True)
        for s in ss[1:]:
            m = jnp.maximum(m, s.max(axis=-1, keepdims=True))
        acc = None
        for s, v in zip(ss, v_refs):
            pv = _dot(jnp.exp2((s - m).astype(BF16)), v[:, hd * V_PAD:(hd + 1) * V_PAD])
            acc = pv if acc is None else acc + pv
        o_ref[:, hd * V_DIM:(hd + 1) * V_DIM] = (acc[:, :V_DIM] / acc[:, V_DIM:V_DIM + 1]).astype(BF16)


def _attention(q, ks, vs, tq):
    b, nq, qk_w = q.shape
    v_w = MLA_HEADS * V_DIM
    kv_spec = lambda a: pl.BlockSpec((None,) + a.shape[1:], lambda i, j: (i, 0, 0))
    return pl.pallas_call(
        functools.partial(_attn_kernel, n_seg=len(ks)),
        out_shape=jax.ShapeDtypeStruct((b, nq, v_w), BF16),
        grid=(b, nq // tq),
        in_specs=[pl.BlockSpec((None, tq, qk_w), lambda i, j: (i, j, 0))]
                 + [kv_spec(a) for a in ks] + [kv_spec(a) for a in vs],
        out_specs=pl.BlockSpec((None, tq, v_w), lambda i, j: (i, j, 0)),
        compiler_params=_cparams(("arbitrary", "arbitrary")),
        name="mla_attention",
    )(q, *ks, *vs)


def _post_core(o, x, mod_ref, g1_ref, g2_ref, wr_ref, x1_ref, fin_ref, lg_ref):
    d = x.shape[-1]
    gate = _rows(_mod_chunk(mod_ref, 2, d), x)
    shift = _rows(_mod_chunk(mod_ref, 3, d), x)
    scale = _rows(_mod_chunk(mod_ref, 4, d), x)
    x1 = x + gate * _rms(o, g1_ref[...])
    fin = _rms(x1, g2_ref[...]) * (1.0 + scale) + shift
    x1_ref[...] = x1
    _pack_store(fin_ref, fin)
    lg_ref[...] = lax.dot_general(wr_ref[...], fin, (((1,), (1,)), ((), ())),
                                  preferred_element_type=F32, precision=lax.Precision.HIGHEST)


def _post_proj_kernel(o_ref, wo_ref, x_ref, mod_ref, g1_ref, g2_ref, wr_ref, x1_ref, fin_ref, lg_ref):
    o = _dot(o_ref[...], wo_ref[...])
    _post_core(o, x_ref[...], mod_ref, g1_ref, g2_ref, wr_ref, x1_ref, fin_ref, lg_ref)


def _post_glu_kernel(h_ref, yf_ref, yb_ref, dsk_ref, wg_ref, bg_ref, x_ref, mod_ref, g1_ref, g2_ref, wr_ref,
                     x1_ref, fin_ref, lg_ref):
    d = x_ref.shape[-1]
    y = h_ref[...] * dsk_ref[...] + yf_ref[...] + yb_ref[...]
    z = _dot(jax.nn.gelu(y, approximate=True).astype(BF16), wg_ref[...]) + bg_ref[...]
    o = z[:, :d] * jax.nn.sigmoid(z[:, d:])
    _post_core(o, x_ref[...], mod_ref, g1_ref, g2_ref, wr_ref, x1_ref, fin_ref, lg_ref)


def _post_mixer(kernel, tok_inputs, consts, x, n_tok, x_off, mods, g1, g2, wr_t, tm, rows_per_mod, name):
    d = x.shape[-1]
    ne = wr_t.shape[0]
    tiles_per_mod = rows_per_mod // tm
    xo = x_off // tm
    full = lambda a: pl.BlockSpec(a.shape, lambda i: (0,) * a.ndim)
    tile = pl.BlockSpec((tm, d), lambda i: (i, 0))
    mod_spec = pl.BlockSpec((None,) + mods.shape[1:], lambda i: (i // tiles_per_mod, 0, 0))
    return pl.pallas_call(
        kernel,
        out_shape=(jax.ShapeDtypeStruct((n_tok, d), F32),
                   jax.ShapeDtypeStruct((n_tok * PACK_ROWS, LANES), U32),
                   jax.ShapeDtypeStruct((ne, n_tok), F32)),
        grid=(n_tok // tm,),
        in_specs=[spec for _, spec in tok_inputs] + [full(a) for a in consts]
                 + [pl.BlockSpec((tm, d), lambda i: (i + xo, 0)), mod_spec, full(g1), full(g2), full(wr_t)],
        out_specs=(tile, pl.BlockSpec((tm * PACK_ROWS, LANES), lambda i: (i, 0)),
                   pl.BlockSpec((ne, tm), lambda i: (0, i))),
        compiler_params=_cparams(("arbitrary",)),
        name=name,
    )(*[a for a, _ in tok_inputs], *consts, x, mods, g1, g2, wr_t)


def _route_kernel(lg_ref, bias_ref, eidx_ref, gate_ref, rank_ref, cnt_ref, tri_ref, base_ref):
    i = pl.program_id(0)
    ne, tt = lg_ref.shape
    gsz = ne // N_EXPERT_GROUPS
    shp = (N_EXPERT_GROUPS, gsz, tt)
    neg = -jnp.inf

    @pl.when(i == 0)
    def _():
        base_ref[...] = jnp.zeros_like(base_ref)
        r = lax.broadcasted_iota(jnp.int32, (tt, tt), 0)
        c = lax.broadcasted_iota(jnp.int32, (tt, tt), 1)
        tri_ref[...] = (r < c).astype(BF16)

    scores = jax.nn.sigmoid(lg_ref[...])
    s3 = scores.reshape(shp)
    b3 = (scores + bias_ref[...]).reshape(shp)
    io_e = lax.broadcasted_iota(jnp.int32, shp, 1)
    io_g = lax.broadcasted_iota(jnp.int32, shp, 0)
    io_flat = io_g * gsz + io_e
    m1 = b3.max(axis=1, keepdims=True)
    i1 = jnp.where(b3 == m1, io_e, gsz).min(axis=1, keepdims=True)
    m2 = jnp.where(io_e == i1, neg, b3).max(axis=1, keepdims=True)
    cur = jnp.broadcast_to(m1 + m2, shp)
    gsel = jnp.zeros(shp, jnp.bool_)
    for _ in range(TOPK_GROUPS):
        m = cur.max(axis=0, keepdims=True)
        gi = jnp.where(cur == m, io_g, N_EXPERT_GROUPS).min(axis=0, keepdims=True)
        hit = io_g == gi
        gsel = jnp.logical_or(gsel, hit)
        cur = jnp.where(hit, neg, cur)
    cand = jnp.where(gsel, b3, neg)
    sel = jnp.zeros(shp, jnp.bool_)
    eids, gts = [], []
    for _ in range(TOP_K):
        m = cand.max(axis=0, keepdims=True).max(axis=1, keepdims=True)
        ei = jnp.where(cand == m, io_flat, ne).min(axis=0, keepdims=True).min(axis=1, keepdims=True)
        hit = io_flat == ei
        gts.append(jnp.where(hit, s3, 0.0).sum(axis=0, keepdims=True).sum(axis=1, keepdims=True))
        eids.append(ei)
        sel = jnp.logical_or(sel, hit)
        cand = jnp.where(hit, neg, cand)
    gsum = gts[0]
    for g in gts[1:]:
        gsum = gsum + g
    self32 = sel.astype(F32).reshape(ne, tt)
    cnt = _dot(self32.astype(BF16), tri_ref[...]) + base_ref[...]
    cnt3 = cnt.reshape(shp)
    for k in range(TOP_K):
        hit = io_flat == eids[k]
        rk = jnp.where(hit, cnt3, 0.0).sum(axis=0, keepdims=True).sum(axis=1, keepdims=True)
        rank_ref[k:k + 1, :] = rk.reshape(1, tt).astype(jnp.int32)
        eidx_ref[k:k + 1, :] = eids[k].reshape(1, tt)
        gate_ref[k:k + 1, :] = (gts[k] / gsum * ROUTED_SCALE).reshape(1, tt)
    base_new = base_ref[...] + self32.sum(axis=1, keepdims=True)
    base_ref[...] = base_new
    cnt_ref[...] = jnp.broadcast_to(base_new, cnt_ref.shape)


def _route(logits_t, bias, tt):
    ne, t = logits_t.shape
    out_i = jax.ShapeDtypeStruct((TOP_K, t), jnp.int32)
    row = pl.BlockSpec((TOP_K, tt), lambda i: (0, i))
    return pl.pallas_call(
        _route_kernel,
        out_shape=(out_i, jax.ShapeDtypeStruct((TOP_K, t), F32), out_i,
                   jax.ShapeDtypeStruct((ne, 128), F32)),
        grid=(t // tt,),
        in_specs=[pl.BlockSpec((ne, tt), lambda i: (0, i)),
                  pl.BlockSpec((ne, 1), lambda i: (0, 0))],
        out_specs=(row, row, row, pl.BlockSpec((ne, 128), lambda i: (0, 0))),
        scratch_shapes=[pltpu.VMEM((tt, tt), BF16), pltpu.VMEM((ne, 1), F32)],
        compiler_params=_cparams(("arbitrary",)),
        name="moe_route",
    )(logits_t, bias.reshape(ne, 1))


def _dest_kernel(eidx_ref, rank_ref, start_ref, dest_ref):
    kk, tt = eidx_ref.shape
    ne = start_ref.shape[0]
    n_chunk, _, r = dest_ref.shape
    io_e = lax.broadcasted_iota(jnp.int32, (ne, tt), 0)
    start = start_ref[...]
    for k in range(kk):
        hit = io_e == eidx_ref[k:k + 1, :]
        dk = jnp.where(hit, start, 0).sum(axis=0, keepdims=True) + rank_ref[k:k + 1, :]
        for c in range(n_chunk):
            dest_ref[c, k:k + 1, :] = dk[:, c * r:(c + 1) * r]


def _dest_rows(eidx_t, rank_t, start, tt, r):
    kk, t = eidx_t.shape
    ne = start.shape[0]
    return pl.pallas_call(
        _dest_kernel,
        out_shape=jax.ShapeDtypeStruct((t // r, kk, r), jnp.int32),
        grid=(t // tt,),
        in_specs=[pl.BlockSpec((kk, tt), lambda i: (0, i)),
                  pl.BlockSpec((kk, tt), lambda i: (0, i)),
                  pl.BlockSpec((ne, 1), lambda i: (0, 0))],
        out_specs=pl.BlockSpec((tt // r, kk, r), lambda i: (i, 0, 0)),
        compiler_params=_cparams(("arbitrary",)),
        name="moe_dest",
    )(eidx_t, rank_t, start.reshape(ne, 1))


SC_CHUNK = 64


def _sc_mesh():
    return plsc.VectorSubcoreMesh(core_axis_name="c", subcore_axis_name="s")


def _sc_workers():
    info = plsc.get_sparse_core_info()
    return info.num_cores, info.num_cores * info.num_subcores


def _sc_scatter_rows(rows, dest, n_out):
    t = rows.shape[0]
    n_chunk, kk, r = dest.shape
    nc, nw = _sc_workers()
    cpw = n_chunk // nw
    assert cpw * nw == n_chunk and cpw % 2 == 0 and n_chunk * r == t

    @functools.partial(
        pl.kernel, mesh=_sc_mesh(),
        out_type=jax.ShapeDtypeStruct((n_out,) + rows.shape[1:], rows.dtype),
        scratch_types=[pltpu.VMEM((2, kk, r), jnp.int32), pltpu.VMEM((2, r) + rows.shape[1:], rows.dtype),
                       pltpu.SemaphoreType.DMA((2,)), pltpu.SemaphoreType.DMA((2,))])
    def scatter(rows_hbm, dest_hbm, out_hbm, idx_v, rows_v, load_sem, scat_sem):
        c0 = (lax.axis_index("s") * nc + lax.axis_index("c")) * cpw

        def loads(c, b):
            return (pltpu.make_async_copy(dest_hbm.at[c], idx_v.at[b], load_sem.at[b]),
                    pltpu.make_async_copy(rows_hbm.at[pl.ds(c * r, r)], rows_v.at[b], load_sem.at[b]))

        def scat(b, k):
            return pltpu.make_async_copy(rows_v.at[b], out_hbm.at[idx_v.at[b, k]], scat_sem.at[b])

        for cp in loads(c0, 0):
            cp.start()

        @pl.loop(0, cpw, step=2)
        def _(ci):
            for b in range(2):
                c = c0 + ci + b
                for cp in loads(c, b):
                    cp.wait()
                for k in range(kk):
                    scat(b, k).start()

                @pl.when(ci + b >= 1)
                def _():
                    for k in range(kk):
                        scat(1 - b, k).wait()

                @pl.when(ci + b + 1 < cpw)
                def _():
                    for cp in loads(c + 1, 1 - b):
                        cp.start()

        for k in range(kk):
            scat((cpw - 1) % 2, k).wait()

    return scatter(rows, dest)


def _sc_gather_rows(src, dest):
    n_chunk, kk, r = dest.shape
    t = n_chunk * r
    nc, nw = _sc_workers()
    cpw = n_chunk // nw
    nbuf = 3
    assert cpw * nw == n_chunk and kk > nbuf

    @functools.partial(
        pl.kernel, mesh=_sc_mesh(),
        out_type=jax.ShapeDtypeStruct((kk, t) + src.shape[1:], src.dtype),
        scratch_types=[pltpu.VMEM((kk, r), jnp.int32), pltpu.VMEM((nbuf, r) + src.shape[1:], src.dtype),
                       pltpu.SemaphoreType.DMA((nbuf,)), pltpu.SemaphoreType.DMA((nbuf,))])
    def gather(src_hbm, dest_hbm, out_hbm, idx_v, rows_v, get_sem, put_sem):
        c0 = (lax.axis_index("s") * nc + lax.axis_index("c")) * cpw

        @pl.loop(0, cpw)
        def _(ci):
            c = c0 + ci
            pltpu.sync_copy(dest_hbm.at[c], idx_v)

            def get(k):
                return pltpu.make_async_copy(src_hbm.at[idx_v.at[k]], rows_v.at[k % nbuf], get_sem.at[k % nbuf])

            def put(k):
                return pltpu.make_async_copy(rows_v.at[k % nbuf], out_hbm.at[k, pl.ds(c * r, r)],
                                             put_sem.at[k % nbuf])

            for k in range(nbuf - 1):
                get(k).start()
            for k in range(kk):
                get(k).wait()
                put(k).start()
                if k + nbuf - 1 < kk:
                    if k >= 1:
                        put(k - 1).wait()
                    get(k + nbuf - 1).start()
            for k in range(kk - nbuf, kk):
                put(k).wait()

    return gather(src, dest)


def _expert_kernel(be_ref, nu_ref, x_ref, wg_ref, wu_ref, wd_ref, o_ref, wgu_s, wd_s):
    i = pl.program_id(0)
    tb = o_ref.shape[0] // PACK_ROWS

    @pl.when(i < nu_ref[0])
    def _():
        @pl.when(jnp.logical_or(i == 0, be_ref[i] != be_ref[jnp.maximum(i - 1, 0)]))
        def _():
            wgu_s[:, :D_EXPERT] = wg_ref[...].astype(BF16)
            wgu_s[:, D_EXPERT:] = wu_ref[...].astype(BF16)
            wd_s[...] = wd_ref[...].astype(BF16)

        x = jnp.concatenate([v.astype(BF16) for v in _unpack_load(x_ref, tb)], axis=-1)
        gu = _dot(x, wgu_s[...])
        g = gu[:, :D_EXPERT]
        h = g * jax.nn.sigmoid(g) * gu[:, D_EXPERT:]
        _pack_store(o_ref, _dot(h.astype(BF16), wd_s[...]))


def _experts(xs, blk_e, n_used, w_gate, w_up, w_down, layer, tb):
    rows = xs.shape[0] // PACK_ROWS
    _, ne, d, de = w_gate.shape
    nb = rows // tb
    row_map = lambda i, be, nu: (jnp.minimum(i, nu[0] - 1), 0)
    w_map = lambda i, be, nu: (layer, be[i], 0, 0)
    grid_spec = pltpu.PrefetchScalarGridSpec(
        num_scalar_prefetch=2,
        grid=(nb,),
        in_specs=[pl.BlockSpec((tb * PACK_ROWS, LANES), row_map),
                  pl.BlockSpec((None, None, d, de), w_map),
                  pl.BlockSpec((None, None, d, de), w_map),
                  pl.BlockSpec((None, None, de, d), w_map)],
        out_specs=pl.BlockSpec((tb * PACK_ROWS, LANES), row_map),
        scratch_shapes=[pltpu.VMEM((d, 2 * de), BF16), pltpu.VMEM((de, d), BF16)],
    )
    return pl.pallas_call(
        _expert_kernel,
        out_shape=jax.ShapeDtypeStruct(xs.shape, U32),
        grid_spec=grid_spec,
        compiler_params=_cparams(("arbitrary",)),
        name="moe_experts",
    )(blk_e, n_used, xs, w_gate, w_up, w_down)


def _combine_kernel(yk_ref, gate_ref, fin_ref, shgu_ref, shd_ref, x1_ref, mod_ref, g3_ref, o_ref):
    tm, d = x1_ref.shape
    gates = gate_ref[...]
    blocks = None
    for k in range(TOP_K):
        gk = gates[:, k:k + 1]
        terms = [gk * v for v in _unpack_load(yk_ref, tm, lead=(k,))]
        blocks = terms if blocks is None else [a + b for a, b in zip(blocks, terms)]
    fin = jnp.concatenate([v.astype(BF16) for v in _unpack_load(fin_ref, tm)], axis=-1)
    gu = _dot(fin, shgu_ref[...])
    g = gu[:, :D_EXPERT]
    hsh = g * jax.nn.sigmoid(g) * gu[:, D_EXPERT:]
    f = jnp.concatenate(blocks, axis=-1) + _dot(hsh.astype(BF16), shd_ref[...])
    x1 = x1_ref[...]
    o_ref[...] = x1 + _rows(_mod_chunk(mod_ref, 5, d), x1) * _rms(f, g3_ref[...])


def _combine(yk, gates, fin, shgu, shd, x1, mods, g3, tm, rows_per_mod, tok_off):
    t, d = x1.shape
    off = tok_off // tm
    tiles_per_mod = rows_per_mod // tm
    full = lambda a: pl.BlockSpec(a.shape, lambda i: (0,) * a.ndim)
    return pl.pallas_call(
        _combine_kernel,
        out_shape=jax.ShapeDtypeStruct((t, d), F32),
        grid=(t // tm,),
        in_specs=[pl.BlockSpec((TOP_K, tm * PACK_ROWS, LANES), lambda i: (0, i + off, 0)),
                  pl.BlockSpec((tm, TOP_K), lambda i: (i + off, 0)),
                  pl.BlockSpec((tm * PACK_ROWS, LANES), lambda i: (i + off, 0)),
                  full(shgu), full(shd),
                  pl.BlockSpec((tm, d), lambda i: (i, 0)),
                  pl.BlockSpec((None,) + mods.shape[1:], lambda i: (i // tiles_per_mod, 0, 0)),
                  full(g3)],
        out_specs=pl.BlockSpec((tm, d), lambda i: (i, 0)),
        compiler_params=_cparams(("arbitrary",)),
        name="moe_combine",
    )(yk, gates, fin, shgu, shd, x1, mods, g3)


def _moe(fin, logits_t, bias, w_gate, w_up, w_down, layer, tb):
    t = fin.shape[0] // PACK_ROWS
    ne = w_gate.shape[1]
    tt = 512
    eidx_t, gates_t, rank_t, cnt = _route(logits_t, bias, tt)
    counts = cnt[:, 0].astype(jnp.int32)
    padded = (counts + tb - 1) // tb * tb
    pad_end = jnp.cumsum(padded)
    pad_start = pad_end - padded
    nb = (t * TOP_K) // tb + ne
    n_used = pad_end[-1] // tb
    blk_start = jnp.arange(nb, dtype=jnp.int32) * tb
    blk = jnp.sum(pad_end[None, :] <= jnp.minimum(blk_start, pad_end[-1] - 1)[:, None], axis=1)
    blk_e = jnp.minimum(blk, ne - 1).astype(jnp.int32)
    dest = _dest_rows(eidx_t, rank_t, pad_start, tt, SC_CHUNK)
    xs = _sc_scatter_rows(fin.reshape(t, PACK_ROWS, LANES), dest, nb * tb)
    ys = _experts(xs.reshape(nb * tb * PACK_ROWS, LANES), blk_e, n_used.reshape(1).astype(jnp.int32),
                  w_gate, w_up, w_down, layer, tb)
    yk = _sc_gather_rows(ys.reshape(nb * tb, PACK_ROWS, LANES), dest)
    return yk.reshape(TOP_K, t * PACK_ROWS, LANES), gates_t.T


def _pre_s5_kernel(x_ref, mod_ref, g0_ref, h_ref):
    d = x_ref.shape[-1]
    x = x_ref[...]
    h_ref[...] = (_rms(x, g0_ref[...]) * (1.0 + _rows(_mod_chunk(mod_ref, 1, d), x))
                  + _rows(_mod_chunk(mod_ref, 0, d), x))


def _pre_s5(x, mods, n_first, g0, tm):
    t, d = x.shape
    first_tiles = n_first // tm
    return pl.pallas_call(
        _pre_s5_kernel,
        out_shape=jax.ShapeDtypeStruct((t, d), F32),
        grid=(t // tm,),
        in_specs=[pl.BlockSpec((tm, d), lambda i: (i, 0)),
                  pl.BlockSpec((None,) + mods.shape[1:], lambda i: (jnp.where(i < first_tiles, 0, 1), 0, 0)),
                  pl.BlockSpec(g0.shape, lambda i: (0, 0))],
        out_specs=pl.BlockSpec((tm, d), lambda i: (i, 0)),
        compiler_params=_cparams(("arbitrary",)),
        name="pre_s5",
    )(x, mods, g0)


def _s5_scan_kernel(h_ref, bm_ref, cm_ref, lam_ref, y_ref, bu0, bu1, xb0, xb1, st_ref):
    first = jnp.logical_and(jnp.logical_and(pl.program_id(0) == 0, pl.program_id(1) == 0), pl.program_id(2) == 0)
    dr = pl.program_id(1)
    s = pl.program_id(2)
    tc, nb, cw = h_ref.shape
    half = st_ref.shape[1] // 2

    @pl.when(first)
    def _():
        for r in (bu0, bu1, xb0, xb1, st_ref):
            r[...] = jnp.zeros_like(r)

    def stages(bu_w, bu_r, xb_w, xb_r):
        y_ref[...] = _dot(xb_r[...], cm_ref[...]).reshape(tc, nb, cw)
        bu_w[...] = _dot(h_ref[...].reshape(tc * nb, cw).astype(BF16), bm_ref[...])
        lr = jnp.broadcast_to(lam_ref[0:1, :], (nb, half))
        li = jnp.broadcast_to(lam_ref[1:2, :], (nb, half))
        fresh = s == 1
        xr = jnp.where(fresh, 0.0, st_ref[:, 0:half])
        xi = jnp.where(fresh, 0.0, st_ref[:, half:2 * half])
        for i in range(tc):
            t = i + dr * (tc - 1 - 2 * i)
            rows = pl.ds(pl.multiple_of(t * nb, nb), nb)
            nr = lr * xr - li * xi + bu_r[rows, 0:half]
            ni = lr * xi + li * xr + bu_r[rows, half:2 * half]
            xb_w[rows, 0:half] = nr.astype(BF16)
            xb_w[rows, half:2 * half] = ni.astype(BF16)
            xr, xi = nr, ni
        st_ref[:, 0:half] = xr
        st_ref[:, half:2 * half] = xi

    @pl.when(s % 2 == 0)
    def _():
        stages(bu0, bu1, xb1, xb0)

    @pl.when(s % 2 == 1)
    def _():
        stages(bu1, bu0, xb0, xb1)


def _s5_scan(h_all, n_ctx, bm, cm, lam, tc):
    nt, nb, d = h_all.shape
    nl = nt - n_ctx
    cw = S5_GROUPS_PER_BLOCK * S5_GROUP
    sw = 2 * S5_GROUPS_PER_BLOCK * S5_STATE
    ncc, n = n_ctx // tc, nt // tc

    def chunk(dr, j):
        j = jnp.clip(j, 0, n - 1)
        rev = jnp.where(j < ncc, ncc - 1 - j, n - 1 - (j - ncc))
        return jnp.where(dr == 0, j, rev)

    def out_map(g, dr, s):
        return (dr, chunk(dr, jnp.clip(s - 2, ncc, n - 1)) - ncc, 0, g)

    return pl.pallas_call(
        _s5_scan_kernel,
        out_shape=jax.ShapeDtypeStruct((2, nl, nb, d), F32),
        grid=(d // cw, 2, n + 2),
        in_specs=[pl.BlockSpec((tc, nb, cw), lambda g, dr, s: (chunk(dr, s), 0, g)),
                  pl.BlockSpec((None, None, cw, sw), lambda g, dr, s: (dr, g, 0, 0)),
                  pl.BlockSpec((None, None, sw, cw), lambda g, dr, s: (dr, g, 0, 0)),
                  pl.BlockSpec((None, None, 2, sw // 2), lambda g, dr, s: (dr, g, 0, 0))],
        out_specs=pl.BlockSpec((None, tc, nb, cw), out_map),
        scratch_shapes=[pltpu.VMEM((tc * nb, sw), F32), pltpu.VMEM((tc * nb, sw), F32),
                        pltpu.VMEM((tc * nb, sw), BF16), pltpu.VMEM((tc * nb, sw), BF16),
                        pltpu.VMEM((nb, sw), F32)],
        compiler_params=_cparams(("arbitrary", "arbitrary", "arbitrary")),
        name="s5_scan",
    )(h_all, bm, cm, lam)


def _s5_params(lam_re, lam_im, log_step, b_re, b_im, c_re, c_im):
    g, p = lam_re.shape[1:]
    gb = S5_GROUPS_PER_BLOCK
    nblk = g // gb
    step = jnp.exp(log_step)[..., None]
    mag = jnp.exp(lam_re * step)
    lb_re = mag * jnp.cos(lam_im * step)
    lb_im = mag * jnp.sin(lam_im * step)
    den = lam_re * lam_re + lam_im * lam_im
    f_re = ((lb_re - 1.0) * lam_re + lb_im * lam_im) / den
    f_im = (lb_im * lam_re - (lb_re - 1.0) * lam_im) / den
    bb_re = f_re[..., None] * b_re - f_im[..., None] * b_im
    bb_im = f_re[..., None] * b_im + f_im[..., None] * b_re
    eye = jnp.eye(gb, dtype=F32)

    def in_map(w):
        w = w.reshape(2, nblk, gb, p, S5_GROUP)
        return jnp.einsum("dnapi,ab->dnaibp", w, eye).reshape(2, nblk, gb * S5_GROUP, gb * p)

    def out_map(w):
        w = w.reshape(2, nblk, gb, S5_GROUP, p)
        return jnp.einsum("dnaip,ab->dnapbi", w, eye).reshape(2, nblk, gb * p, gb * S5_GROUP)

    bm = jnp.concatenate([in_map(bb_re), in_map(bb_im)], axis=-1).astype(BF16)
    cm = jnp.concatenate([out_map(c_re), out_map(-c_im)], axis=-2).astype(BF16)
    lam = jnp.stack([lb_re.reshape(2, nblk, gb * p), lb_im.reshape(2, nblk, gb * p)], axis=2)
    return bm, cm, lam


def _rope_tables(n_tokens):
    rows = n_tokens // GRID_W
    row = jnp.repeat(jnp.arange(rows), GRID_W).astype(F32)
    col = jnp.tile(jnp.arange(GRID_W), rows).astype(F32)
    n_freq = ROPE_DIM // 4
    inv_freq = ROPE_BASE ** (-jnp.arange(n_freq, dtype=F32) / n_freq)
    ang = jnp.concatenate([row[:, None] * inv_freq, col[:, None] * inv_freq], axis=-1)
    cos, sin = jnp.cos(ang), jnp.sin(ang)
    z = jnp.zeros((n_tokens, 128 - ROPE_DIM), F32)
    return (jnp.concatenate([cos, cos, z], axis=-1), jnp.concatenate([-sin, sin, z], axis=-1))


def _split_pairs(w):
    ev, od = w[..., 0::2], w[..., 1::2]
    z = jnp.zeros(w.shape[:-1] + (128 - ROPE_DIM,), w.dtype)
    return jnp.concatenate([ev, od, z], axis=-1), jnp.concatenate([od, ev, z], axis=-1)


def _mla_weights(w_dqkv, w_uq, w_ukv):
    kp, kps = _split_pairs(w_dqkv[:, Q_LORA + KV_LORA:])
    wd = jnp.concatenate([w_dqkv[:, :Q_LORA + KV_LORA], kp, kps], axis=-1).astype(BF16)
    wq3 = w_uq.reshape(Q_LORA, MLA_HEADS, NOPE_DIM + ROPE_DIM)
    qp, qps = _split_pairs(wq3[:, :, NOPE_DIM:])
    wq = jnp.concatenate([wq3[:, :, :NOPE_DIM].reshape(Q_LORA, -1), qp.reshape(Q_LORA, -1),
                          qps.reshape(Q_LORA, -1)], axis=-1).astype(BF16)
    wkv3 = w_ukv.reshape(KV_LORA, MLA_HEADS, NOPE_DIM + V_DIM)
    wkv = jnp.concatenate([wkv3[:, :, :NOPE_DIM].reshape(KV_LORA, -1),
                           wkv3[:, :, NOPE_DIM:].reshape(KV_LORA, -1)], axis=-1).astype(BF16)
    return wd, wq, wkv


@jax.jit
def kernel(x, c, ctx, c_ctx, ada_w, ada_b, norm_g, mla_w_dqkv, mla_g_q, mla_g_kv, mla_w_uq, mla_w_ukv, mla_w_o, s5_lam_re, s5_lam_im, s5_log_step, s5_b_re, s5_b_im, s5_c_re, s5_c_im, s5_d, s5_w_glu, s5_b_glu, moe_w_router, moe_bias, moe_w_gate, moe_w_up, moe_w_down, sh_w_gate, sh_w_up, sh_w_down):
    b, l, d = x.shape
    n_ctx = ctx.shape[1]
    assert ada_w.shape[0] == 2 and b % 8 == 0
    tm = 256
    tb = 512
    row = lambda v: v.reshape(1, -1)

    n_rows = (b + 1 + 7) // 8 * 8
    cvec = jnp.zeros((n_rows, d), F32).at[:b].set(c).at[b].set(c_ctx)
    mods = _ada_mods(cvec, ada_w, ada_b)

    def shared_weights(i):
        shgu = jnp.concatenate([sh_w_gate[i], sh_w_up[i]], axis=-1).astype(BF16)
        return shgu, sh_w_down[i].astype(BF16)

    mod_lat = mods[0, :b].reshape(b, 1, N_MOD * d)
    mod_ctx = mods[0, b].reshape(1, 1, N_MOD * d)
    wd, wq, wkv = _mla_weights(mla_w_dqkv[0], mla_w_uq[0], mla_w_ukv[0])
    cos_l, sin_l = _rope_tables(l)
    cos_c = jnp.concatenate([jnp.ones((n_ctx, ROPE_DIM), F32), jnp.zeros((n_ctx, 128 - ROPE_DIM), F32)], -1)
    sin_c = jnp.zeros((n_ctx, 128), F32)
    pre = functools.partial(_pre_mla, g0=row(norm_g[0, 0]), wd=wd, gq=row(mla_g_q[0]), gkv=row(mla_g_kv[0]),
                            wq=wq, wkv=wkv, tm=tm)
    q_c, k_c, v_c = pre(ctx, mod_ctx, cos_t=cos_c, sin_t=sin_c)
    q_l, k_l, v_l = pre(x, mod_lat, cos_t=cos_l, sin_t=sin_l)
    o_l = _attention(q_l, [k_c, k_l], [v_c, v_l], tm)
    o_c = _attention(q_c, [k_c], [v_c], n_ctx)

    wo = mla_w_o[0].astype(BF16)
    wr_t = moe_w_router[0].T
    g1, g2, g3 = row(norm_g[0, 1]), row(norm_g[0, 2]), row(norm_g[0, 3])
    post = functools.partial(_post_mixer, _post_proj_kernel, consts=[wo], g1=g1, g2=g2, wr_t=wr_t, tm=tm,
                             name="post_mla")
    o_spec = pl.BlockSpec((tm, o_l.shape[-1]), lambda i: (i, 0))
    x1_c, fin_c, lg_c = post([(o_c.reshape(b * n_ctx, -1), o_spec)], x=ctx.reshape(b * n_ctx, d),
                             n_tok=b * n_ctx, x_off=0, mods=mod_ctx, rows_per_mod=b * n_ctx)
    x1_l, fin_l, lg_l = post([(o_l.reshape(b * l, -1), o_spec)], x=x.reshape(b * l, d), n_tok=b * l, x_off=0,
                             mods=mod_lat, rows_per_mod=l)
    fin = jnp.concatenate([fin_c, fin_l], axis=0)
    lg = jnp.concatenate([lg_c, lg_l], axis=1)
    shgu, shd = shared_weights(0)
    yk, gates = _moe(fin, lg, moe_bias[0], moe_w_gate, moe_w_up, moe_w_down, 0, tb)
    comb = functools.partial(_combine, yk, gates, fin, shgu, shd, g3=g3, tm=tm)
    x2_c = comb(x1=x1_c, mods=mod_ctx, rows_per_mod=b * n_ctx, tok_off=0)
    x2_l = comb(x1=x1_l, mods=mod_lat, rows_per_mod=l, tok_off=b * n_ctx)

    n_all = n_ctx + l
    xt = jnp.concatenate([jnp.swapaxes(x2_c.reshape(b, n_ctx, d), 0, 1),
                          jnp.swapaxes(x2_l.reshape(b, l, d), 0, 1)], axis=0).reshape(n_all * b, d)
    mod_lat = mods[1, :b]
    mod_ctx = jnp.broadcast_to(mods[1, b][None], (b, N_MOD * d))
    h = _pre_s5(xt, jnp.stack([mod_ctx, mod_lat]), n_ctx * b, row(norm_g[1, 0]), 512)
    bm, cm, lam = _s5_params(s5_lam_re[0], s5_lam_im[0], s5_log_step[0], s5_b_re[0], s5_b_im[0],
                             s5_c_re[0], s5_c_im[0])
    y = _s5_scan(h.reshape(n_all, b, d), n_ctx, bm, cm, lam, 64).reshape(2, l * b, d)
    g1, g2, g3 = row(norm_g[1, 1]), row(norm_g[1, 2]), row(norm_g[1, 3])
    lat0 = n_ctx * b // tm
    x1, fin, lg = _post_mixer(
        _post_glu_kernel,
        [(h, pl.BlockSpec((tm, d), lambda i: (i + lat0, 0))),
         (y, pl.BlockSpec((None, tm, d), lambda i: (0, i, 0))),
         (y, pl.BlockSpec((None, tm, d), lambda i: (1, i, 0)))],
        [row(s5_d[0]), s5_w_glu[0].astype(BF16), row(s5_b_glu[0])],
        x=xt, n_tok=l * b, x_off=n_ctx * b, mods=mod_lat[None], g1=g1, g2=g2, wr_t=moe_w_router[1].T, tm=tm,
        rows_per_mod=l * b, name="post_s5")
    shgu, shd = shared_weights(1)
    yk, gates = _moe(fin, lg, moe_bias[1], moe_w_gate, moe_w_up, moe_w_down, 1, tb)
    x2 = _combine(yk, gates, fin, shgu, shd, x1, mod_lat[None], g3, tm, rows_per_mod=l * b, tok_off=0)
    return jnp.swapaxes(x2.reshape(l, b, d), 0, 1)
```

```python
import functools

import jax
import jax.numpy as jnp
from jax import lax
from jax.experimental import pallas as pl
from jax.experimental.pallas import tpu as pltpu
from jax.experimental.pallas import tpu_sc as plsc

F32 = jnp.float32
BF16 = jnp.bfloat16
U32 = jnp.uint32

N_MOD = 6
NORM_EPS = 1e-6
LOG2_E = 1.4426950408889634
GRID_W = 64
MLA_HEADS = 8
Q_LORA = 384
KV_LORA = 256
NOPE_DIM = 128
ROPE_DIM = 64
V_DIM = 128
V_PAD = 256
ROPE_BASE = 10000.0
QK_PAD = 256
S5_GROUP = 16
S5_STATE = 64
S5_GROUPS_PER_BLOCK = 8
N_EXPERTS = 64
TOP_K = 8
N_EXPERT_GROUPS = 8
TOPK_GROUPS = 4
D_EXPERT = 256
ROUTED_SCALE = 2.5

VMEM_LIMIT = 56 * 1024 * 1024


def _cparams(sem):
    return pltpu.CompilerParams(dimension_semantics=sem, vmem_limit_bytes=VMEM_LIMIT)


def _rms(x, g):
    return x * lax.rsqrt(jnp.mean(x * x, axis=-1, keepdims=True) + NORM_EPS) * g


def _rows(v, like):
    r = v.shape[0]
    if r == 1:
        return v
    tm, d = like.shape
    return jnp.broadcast_to(v[None], (tm // r, r, d)).reshape(tm, d)


def _mod_chunk(mod_ref, j, d):
    return mod_ref[:, j * d:(j + 1) * d]


def _dot(a, b):
    return jnp.dot(a, b, preferred_element_type=F32)


PACK_ROWS = 4
LANES = 128


def _pack_store(ref, val, lead=(), row0=0):
    n = val.shape[0]
    bits = lax.bitcast_convert_type(val.astype(BF16).astype(F32), U32)
    for s in range(PACK_ROWS):
        lo = bits[:, s * LANES:(s + 1) * LANES] >> 16
        hi = bits[:, (s + PACK_ROWS) * LANES:(s + PACK_ROWS + 1) * LANES] & jnp.uint32(0xFFFF0000)
        ref[lead + (pl.ds(row0 * PACK_ROWS + s, n, stride=PACK_ROWS), slice(None))] = lo | hi


def _unpack_load(ref, n, lead=(), row0=0):
    los, his = [], []
    for s in range(PACK_ROWS):
        w = ref[lead + (pl.ds(row0 * PACK_ROWS + s, n, stride=PACK_ROWS), slice(None))]
        los.append(lax.bitcast_convert_type(w << 16, F32))
        his.append(lax.bitcast_convert_type(w & jnp.uint32(0xFFFF0000), F32))
    return los + his


def _ada_kernel(c_ref, w_ref, b_ref, o_ref):
    c = c_ref[...]
    s = c * jax.nn.sigmoid(c)
    o_ref[...] = jnp.dot(s, w_ref[...], preferred_element_type=F32,
                         precision=lax.Precision.HIGHEST) + b_ref[...]


def _ada_mods(cvec, ada_w, ada_b):
    depth, d, n = ada_w.shape
    rows = cvec.shape[0]
    tn = 1536
    return pl.pallas_call(
        _ada_kernel,
        out_shape=jax.ShapeDtypeStruct((depth, rows, n), F32),
        grid=(depth, n // tn),
        in_specs=[pl.BlockSpec((rows, d), lambda l, j: (0, 0)),
                  pl.BlockSpec((None, d, tn), lambda l, j: (l, 0, j)),
                  pl.BlockSpec((None, 1, tn), lambda l, j: (l, 0, j))],
        out_specs=pl.BlockSpec((None, rows, tn), lambda l, j: (l, 0, j)),
        compiler_params=_cparams(("arbitrary", "arbitrary")),
        name="ada_mods",
    )(cvec, ada_w, ada_b.reshape(depth, 1, n))


def _pre_mla_kernel(x_ref, mod_ref, g0_ref, wd_ref, gq_ref, gkv_ref, wq_ref, wkv_ref, cos_ref, sin_ref,
                    q_ref, k_ref, v_ref):
    d = x_ref.shape[-1]
    x = x_ref[...]
    h = _rms(x, g0_ref[...]) * (1.0 + _mod_chunk(mod_ref, 1, d)) + _mod_chunk(mod_ref, 0, d)
    a = _dot(h.astype(BF16), wd_ref[...])
    cq = _rms(a[:, :Q_LORA], gq_ref[...])
    ckv = _rms(a[:, Q_LORA:Q_LORA + KV_LORA], gkv_ref[...])
    cos = cos_ref[...]
    sin = sin_ref[...]
    o = Q_LORA + KV_LORA
    k_rot = (a[:, o:o + 128] * cos + a[:, o + 128:o + 256] * sin).astype(BF16)
    qa = _dot(cq.astype(BF16), wq_ref[...])
    kva = _dot(ckv.astype(BF16), wkv_ref[...])
    hw = MLA_HEADS * 128
    scale = (NOPE_DIM + ROPE_DIM) ** -0.5 * LOG2_E
    for hd in range(MLA_HEADS):
        lo = hd * 128
        q_rot = qa[:, hw + lo:hw + lo + 128] * cos + qa[:, 2 * hw + lo:2 * hw + lo + 128] * sin
        q_ref[:, hd * QK_PAD:hd * QK_PAD + 128] = (qa[:, lo:lo + 128] * scale).astype(BF16)
        q_ref[:, hd * QK_PAD + 128:(hd + 1) * QK_PAD] = (q_rot * scale).astype(BF16)
        k_ref[:, hd * QK_PAD:hd * QK_PAD + 128] = kva[:, lo:lo + 128].astype(BF16)
        k_ref[:, hd * QK_PAD + 128:(hd + 1) * QK_PAD] = k_rot
        v_ref[:, hd * V_PAD:hd * V_PAD + V_DIM] = kva[:, hw + lo:hw + lo + 128].astype(BF16)
        v_ref[:, hd * V_PAD + V_DIM:(hd + 1) * V_PAD] = jnp.ones((x.shape[0], V_PAD - V_DIM), BF16)


def _pre_mla(x, mods, g0, wd, gq, gkv, wq, wkv, cos_t, sin_t, tm):
    b, n, d = x.shape
    nb_mod = mods.shape[0]
    full = lambda a: pl.BlockSpec(a.shape, lambda i, j: (0,) * a.ndim)
    mod_map = (lambda i, j: (i, 0, 0)) if nb_mod > 1 else (lambda i, j: (0, 0, 0))
    qk_w = MLA_HEADS * QK_PAD
    v_w = MLA_HEADS * V_PAD
    return pl.pallas_call(
        _pre_mla_kernel,
        out_shape=(jax.ShapeDtypeStruct((b, n, qk_w), BF16),
                   jax.ShapeDtypeStruct((b, n, qk_w), BF16),
                   jax.ShapeDtypeStruct((b, n, v_w), BF16)),
        grid=(b, n // tm),
        in_specs=[pl.BlockSpec((None, tm, d), lambda i, j: (i, j, 0)),
                  pl.BlockSpec((None, 1, mods.shape[-1]), mod_map),
                  full(g0), full(wd), full(gq), full(gkv), full(wq), full(wkv),
                  pl.BlockSpec((tm, 128), lambda i, j: (j, 0)),
                  pl.BlockSpec((tm, 128), lambda i, j: (j, 0))],
        out_specs=(pl.BlockSpec((None, tm, qk_w), lambda i, j: (i, j, 0)),
                   pl.BlockSpec((None, tm, qk_w), lambda i, j: (i, j, 0)),
                   pl.BlockSpec((None, tm, v_w), lambda i, j: (i, j, 0))),
        compiler_params=_cparams(("arbitrary", "arbitrary")),
        name="pre_mla",
    )(x, mods, g0, wd, gq, gkv, wq, wkv, cos_t, sin_t)


def _attn_kernel(*refs, n_seg):
    q_ref = refs[0]
    k_refs = refs[1:1 + n_seg]
    v_refs = refs[1 + n_seg:1 + 2 * n_seg]
    o_ref = refs[1 + 2 * n_seg]
    nt = (((1,), (1,)), ((), ()))
    for hd in range(MLA_HEADS):
        q = q_ref[:, hd * QK_PAD:(hd + 1) * QK_PAD]
        ss = [lax.dot_general(q, k[:, hd * QK_PAD:(hd + 1) * QK_PAD], nt, preferred_element_type=F32)
              for k in k_refs]
        m = ss[0].max(axis=-1, keepdims=True)
        for s in ss[1:]:
            m = jnp.maximum(m, s.max(axis=-1, keepdims=True))
        acc = None
        for s, v in zip(ss, v_refs):
            pv = _dot(jnp.exp2((s - m).astype(BF16)), v[:, hd * V_PAD:(hd + 1) * V_PAD])
            acc = pv if acc is None else acc + pv
        o_ref[:, hd * V_DIM:(hd + 1) * V_DIM] = (acc[:, :V_DIM] / acc[:, V_DIM:V_DIM + 1]).astype(BF16)


def _attention(q, ks, vs, tq):
    b, nq, qk_w = q.shape
    v_w = MLA_HEADS * V_DIM
    kv_spec = lambda a: pl.BlockSpec((None,) + a.shape[1:], lambda i, j: (i, 0, 0))
    return pl.pallas_call(
        functools.partial(_attn_kernel, n_seg=len(ks)),
        out_shape=jax.ShapeDtypeStruct((b, nq, v_w), BF16),
        grid=(b, nq // tq),
        in_specs=[pl.BlockSpec((None, tq, qk_w), lambda i, j: (i, j, 0))]
                 + [kv_spec(a) for a in ks] + [kv_spec(a) for a in vs],
        out_specs=pl.BlockSpec((None, tq, v_w), lambda i, j: (i, j, 0)),
        compiler_params=_cparams(("arbitrary", "arbitrary")),
        name="mla_attention",
    )(q, *ks, *vs)


SUB_ROWS = 256


def _sub_tiles(n):
    return [slice(r, r + SUB_ROWS) for r in range(0, n, SUB_ROWS)]


def _post_core(o, x, rows, mod_ref, g1_ref, g2_ref, wr_ref, x1_ref, fin_ref, lg_ref):
    d = x.shape[-1]
    ne = lg_ref.shape[0]
    gate = _rows(_mod_chunk(mod_ref, 2, d), x)
    shift = _rows(_mod_chunk(mod_ref, 3, d), x)
    scale = _rows(_mod_chunk(mod_ref, 4, d), x)
    x1 = x + gate * _rms(o, g1_ref[...])
    fin = _rms(x1, g2_ref[...]) * (1.0 + scale) + shift
    x1_ref[rows, :] = x1
    _pack_store(fin_ref, fin, row0=rows.start)
    nt = (((1,), (1,)), ((), ()))
    f_hi = fin.astype(BF16)
    f_lo = (fin - f_hi.astype(F32)).astype(BF16)
    r_hi = lax.dot_general(wr_ref[...], f_hi, nt, preferred_element_type=F32)
    r_lo = lax.dot_general(wr_ref[0:ne, :], f_lo, nt, preferred_element_type=F32)
    lg_ref[:, rows] = r_hi[:ne] + r_hi[ne:] + r_lo


def _post_proj_kernel(o_ref, wo_ref, x_ref, mod_ref, g1_ref, g2_ref, wr_ref, x1_ref, fin_ref, lg_ref):
    for rows in _sub_tiles(x_ref.shape[0]):
        o = _dot(o_ref[rows, :], wo_ref[...])
        _post_core(o, x_ref[rows, :], rows, mod_ref, g1_ref, g2_ref, wr_ref, x1_ref, fin_ref, lg_ref)


def _post_glu_kernel(h_ref, yf_ref, yb_ref, dsk_ref, wg_ref, bg_ref, x_ref, mod_ref, g1_ref, g2_ref, wr_ref,
                     x1_ref, fin_ref, lg_ref):
    d = x_ref.shape[-1]
    for rows in _sub_tiles(x_ref.shape[0]):
        y = h_ref[rows, :] * dsk_ref[...] + yf_ref[rows, :] + yb_ref[rows, :]
        z = _dot(jax.nn.gelu(y, approximate=True).astype(BF16), wg_ref[...]) + bg_ref[...]
        o = z[:, :d] * jax.nn.sigmoid(z[:, d:])
        _post_core(o, x_ref[rows, :], rows, mod_ref, g1_ref, g2_ref, wr_ref, x1_ref, fin_ref, lg_ref)


def _post_mixer(kernel, tok_inputs, consts, x, n_tok, x_off, mods, g1, g2, wr_t, tm, rows_per_mod, name):
    d = x.shape[-1]
    ne = wr_t.shape[0] // 2
    tiles_per_mod = rows_per_mod // tm
    xo = x_off // tm
    full = lambda a: pl.BlockSpec(a.shape, lambda i: (0,) * a.ndim)
    tile = pl.BlockSpec((tm, d), lambda i: (i, 0))
    mod_spec = pl.BlockSpec((None,) + mods.shape[1:], lambda i: (i // tiles_per_mod, 0, 0))
    return pl.pallas_call(
        kernel,
        out_shape=(jax.ShapeDtypeStruct((n_tok, d), F32),
                   jax.ShapeDtypeStruct((n_tok * PACK_ROWS, LANES), U32),
                   jax.ShapeDtypeStruct((ne, n_tok), F32)),
        grid=(n_tok // tm,),
        in_specs=[spec for _, spec in tok_inputs] + [full(a) for a in consts]
                 + [pl.BlockSpec((tm, d), lambda i: (i + xo, 0)), mod_spec, full(g1), full(g2), full(wr_t)],
        out_specs=(tile, pl.BlockSpec((tm * PACK_ROWS, LANES), lambda i: (i, 0)),
                   pl.BlockSpec((ne, tm), lambda i: (0, i))),
        compiler_params=_cparams(("arbitrary",)),
        name=name,
    )(*[a for a, _ in tok_inputs], *consts, x, mods, g1, g2, wr_t)


def _route_kernel(lg_ref, bias_ref, eidx_ref, gate_ref, rank_ref, cnt_ref, tri_ref, base_ref):
    i = pl.program_id(0)
    ne, tt = lg_ref.shape
    gsz = ne // N_EXPERT_GROUPS
    shp = (N_EXPERT_GROUPS, gsz, tt)
    neg = -jnp.inf

    @pl.when(i == 0)
    def _():
        base_ref[...] = jnp.zeros_like(base_ref)
        r = lax.broadcasted_iota(jnp.int32, (tt, tt), 0)
        c = lax.broadcasted_iota(jnp.int32, (tt, tt), 1)
        tri_ref[...] = (r < c).astype(BF16)

    scores = jax.nn.sigmoid(lg_ref[...])
    s3 = scores.reshape(shp)
    b3 = (scores + bias_ref[...]).reshape(shp)
    io_e = lax.broadcasted_iota(jnp.int32, shp, 1)
    io_g = lax.broadcasted_iota(jnp.int32, shp, 0)
    io_flat = io_g * gsz + io_e
    m1 = b3.max(axis=1, keepdims=True)
    i1 = jnp.where(b3 == m1, io_e, gsz).min(axis=1, keepdims=True)
    m2 = jnp.where(io_e == i1, neg, b3).max(axis=1, keepdims=True)
    cur = jnp.broadcast_to(m1 + m2, shp)
    gsel = jnp.zeros(shp, jnp.bool_)
    for _ in range(TOPK_GROUPS):
        m = cur.max(axis=0, keepdims=True)
        gi = jnp.where(cur == m, io_g, N_EXPERT_GROUPS).min(axis=0, keepdims=True)
        hit = io_g == gi
        gsel = jnp.logical_or(gsel, hit)
        cur = jnp.where(hit, neg, cur)
    cand = jnp.where(gsel, b3, neg)
    sel = jnp.zeros(shp, jnp.bool_)
    eids, gts = [], []
    for _ in range(TOP_K):
        m = cand.max(axis=0, keepdims=True).max(axis=1, keepdims=True)
        ei = jnp.where(cand == m, io_flat, ne).min(axis=0, keepdims=True).min(axis=1, keepdims=True)
        hit = io_flat == ei
        gts.append(jnp.where(hit, s3, 0.0).sum(axis=0, keepdims=True).sum(axis=1, keepdims=True))
        eids.append(ei)
        sel = jnp.logical_or(sel, hit)
        cand = jnp.where(hit, neg, cand)
    gsum = gts[0]
    for g in gts[1:]:
        gsum = gsum + g
    self32 = sel.astype(F32).reshape(ne, tt)
    cnt = _dot(self32.astype(BF16), tri_ref[...]) + base_ref[...]
    cnt3 = cnt.reshape(shp)
    for k in range(TOP_K):
        hit = io_flat == eids[k]
        rk = jnp.where(hit, cnt3, 0.0).sum(axis=0, keepdims=True).sum(axis=1, keepdims=True)
        rank_ref[k:k + 1, :] = rk.reshape(1, tt).astype(jnp.int32)
        eidx_ref[k:k + 1, :] = eids[k].reshape(1, tt)
        gate_ref[k:k + 1, :] = (gts[k] / gsum * ROUTED_SCALE).reshape(1, tt)
    base_new = base_ref[...] + self32.sum(axis=1, keepdims=True)
    base_ref[...] = base_new
    cnt_ref[...] = jnp.broadcast_to(base_new, cnt_ref.shape)


def _route(logits_t, bias, tt):
    ne, t = logits_t.shape
    out_i = jax.ShapeDtypeStruct((TOP_K, t), jnp.int32)
    row = pl.BlockSpec((TOP_K, tt), lambda i: (0, i))
    return pl.pallas_call(
        _route_kernel,
        out_shape=(out_i, jax.ShapeDtypeStruct((TOP_K, t), F32), out_i,
                   jax.ShapeDtypeStruct((ne, 128), F32)),
        grid=(t // tt,),
        in_specs=[pl.BlockSpec((ne, tt), lambda i: (0, i)),
                  pl.BlockSpec((ne, 1), lambda i: (0, 0))],
        out_specs=(row, row, row, pl.BlockSpec((ne, 128), lambda i: (0, 0))),
        scratch_shapes=[pltpu.VMEM((tt, tt), BF16), pltpu.VMEM((ne, 1), F32)],
        compiler_params=_cparams(("arbitrary",)),
        name="moe_route",
    )(logits_t, bias.reshape(ne, 1))


def _dest_kernel(eidx_ref, rank_ref, start_ref, dest_ref):
    kk, tt = eidx_ref.shape
    ne = start_ref.shape[0]
    n_chunk, _, r = dest_ref.shape
    io_e = lax.broadcasted_iota(jnp.int32, (ne, tt), 0)
    start = start_ref[...]
    for k in range(kk):
        hit = io_e == eidx_ref[k:k + 1, :]
        dk = jnp.where(hit, start, 0).sum(axis=0, keepdims=True) + rank_ref[k:k + 1, :]
        for c in range(n_chunk):
            dest_ref[c, k:k + 1, :] = dk[:, c * r:(c + 1) * r]


def _dest_rows(eidx_t, rank_t, start, tt, r):
    kk, t = eidx_t.shape
    ne = start.shape[0]
    return pl.pallas_call(
        _dest_kernel,
        out_shape=jax.ShapeDtypeStruct((t // r, kk, r), jnp.int32),
        grid=(t // tt,),
        in_specs=[pl.BlockSpec((kk, tt), lambda i: (0, i)),
                  pl.BlockSpec((kk, tt), lambda i: (0, i)),
                  pl.BlockSpec((ne, 1), lambda i: (0, 0))],
        out_specs=pl.BlockSpec((tt // r, kk, r), lambda i: (i, 0, 0)),
        compiler_params=_cparams(("arbitrary",)),
        name="moe_dest",
    )(eidx_t, rank_t, start.reshape(ne, 1))


SC_CHUNK = 64


def _sc_mesh():
    return plsc.VectorSubcoreMesh(core_axis_name="c", subcore_axis_name="s")


def _sc_workers():
    info = plsc.get_sparse_core_info()
    return info.num_cores, info.num_cores * info.num_subcores


def _sc_scatter_rows(rows, dest, n_out):
    t = rows.shape[0]
    n_chunk, kk, r = dest.shape
    nc, nw = _sc_workers()
    cpw = n_chunk // nw
    assert cpw * nw == n_chunk and cpw % 2 == 0 and n_chunk * r == t

    @functools.partial(
        pl.kernel, mesh=_sc_mesh(),
        out_type=jax.ShapeDtypeStruct((n_out,) + rows.shape[1:], rows.dtype),
        scratch_types=[pltpu.VMEM((2, kk, r), jnp.int32), pltpu.VMEM((2, r) + rows.shape[1:], rows.dtype),
                       pltpu.SemaphoreType.DMA((2,)), pltpu.SemaphoreType.DMA((2,))])
    def scatter(rows_hbm, dest_hbm, out_hbm, idx_v, rows_v, load_sem, scat_sem):
        c0 = (lax.axis_index("s") * nc + lax.axis_index("c")) * cpw

        def loads(c, b):
            return (pltpu.make_async_copy(dest_hbm.at[c], idx_v.at[b], load_sem.at[b]),
                    pltpu.make_async_copy(rows_hbm.at[pl.ds(c * r, r)], rows_v.at[b], load_sem.at[b]))

        def scat(b, k):
            return pltpu.make_async_copy(rows_v.at[b], out_hbm.at[idx_v.at[b, k]], scat_sem.at[b])

        for cp in loads(c0, 0):
            cp.start()

        @pl.loop(0, cpw, step=2)
        def _(ci):
            for b in range(2):
                c = c0 + ci + b
                for cp in loads(c, b):
                    cp.wait()
                for k in range(kk):
                    scat(b, k).start()

                @pl.when(ci + b >= 1)
                def _():
                    for k in range(kk):
                        scat(1 - b, k).wait()

                @pl.when(ci + b + 1 < cpw)
                def _():
                    for cp in loads(c + 1, 1 - b):
                        cp.start()

        for k in range(kk):
            scat((cpw - 1) % 2, k).wait()

    return scatter(rows, dest)


def _sc_gather_rows(src, dest):
    n_chunk, kk, r = dest.shape
    t = n_chunk * r
    nc, nw = _sc_workers()
    cpw = n_chunk // nw
    nbuf = 3
    assert cpw * nw == n_chunk and kk > nbuf

    @functools.partial(
        pl.kernel, mesh=_sc_mesh(),
        out_type=jax.ShapeDtypeStruct((kk, t) + src.shape[1:], src.dtype),
        scratch_types=[pltpu.VMEM((kk, r), jnp.int32), pltpu.VMEM((nbuf, r) + src.shape[1:], src.dtype),
                       pltpu.SemaphoreType.DMA((nbuf,)), pltpu.SemaphoreType.DMA((nbuf,))])
    def gather(src_hbm, dest_hbm, out_hbm, idx_v, rows_v, get_sem, put_sem):
        c0 = (lax.axis_index("s") * nc + lax.axis_index("c")) * cpw

        @pl.loop(0, cpw)
        def _(ci):
            c = c0 + ci
            pltpu.sync_copy(dest_hbm.at[c], idx_v)

            def get(k):
                return pltpu.make_async_copy(src_hbm.at[idx_v.at[k]], rows_v.at[k % nbuf], get_sem.at[k % nbuf])

            def put(k):
                return pltpu.make_async_copy(rows_v.at[k % nbuf], out_hbm.at[k, pl.ds(c * r, r)],
                                             put_sem.at[k % nbuf])

            for k in range(nbuf - 1):
                get(k).start()
            for k in range(kk):
                get(k).wait()
                put(k).start()
                if k + nbuf - 1 < kk:
                    if k >= 1:
                        put(k - 1).wait()
                    get(k + nbuf - 1).start()
            for k in range(kk - nbuf, kk):
                put(k).wait()

    return gather(src, dest)


def _expert_kernel(be_ref, nu_ref, x_ref, wg_ref, wu_ref, wd_ref, o_ref, wgu_s, wd_s):
    i = pl.program_id(0)
    tb = o_ref.shape[0] // PACK_ROWS

    @pl.when(i < nu_ref[0])
    def _():
        @pl.when(jnp.logical_or(i == 0, be_ref[i] != be_ref[jnp.maximum(i - 1, 0)]))
        def _():
            wgu_s[:, :D_EXPERT] = wg_ref[...].astype(BF16)
            wgu_s[:, D_EXPERT:] = wu_ref[...].astype(BF16)
            wd_s[...] = wd_ref[...].astype(BF16)

        x = jnp.concatenate([v.astype(BF16) for v in _unpack_load(x_ref, tb)], axis=-1)
        gu = _dot(x, wgu_s[...])
        g = gu[:, :D_EXPERT]
        h = g * jax.nn.sigmoid(g) * gu[:, D_EXPERT:]
        _pack_store(o_ref, _dot(h.astype(BF16), wd_s[...]))


def _experts(xs, blk_e, n_used, w_gate, w_up, w_down, layer, tb):
    rows = xs.shape[0] // PACK_ROWS
    _, ne, d, de = w_gate.shape
    nb = rows // tb
    row_map = lambda i, be, nu: (jnp.minimum(i, nu[0] - 1), 0)
    w_map = lambda i, be, nu: (layer, be[i], 0, 0)
    grid_spec = pltpu.PrefetchScalarGridSpec(
        num_scalar_prefetch=2,
        grid=(nb,),
        in_specs=[pl.BlockSpec((tb * PACK_ROWS, LANES), row_map),
                  pl.BlockSpec((None, None, d, de), w_map),
                  pl.BlockSpec((None, None, d, de), w_map),
                  pl.BlockSpec((None, None, de, d), w_map)],
        out_specs=pl.BlockSpec((tb * PACK_ROWS, LANES), row_map),
        scratch_shapes=[pltpu.VMEM((d, 2 * de), BF16), pltpu.VMEM((de, d), BF16)],
    )
    return pl.pallas_call(
        _expert_kernel,
        out_shape=jax.ShapeDtypeStruct(xs.shape, U32),
        grid_spec=grid_spec,
        compiler_params=_cparams(("arbitrary",)),
        name="moe_experts",
    )(blk_e, n_used, xs, w_gate, w_up, w_down)


def _combine_kernel(yk_ref, gate_ref, fin_ref, shgu_ref, shd_ref, x1_ref, mod_ref, g3_ref, o_ref):
    tm, d = x1_ref.shape
    for rows in _sub_tiles(tm):
        n, r0 = SUB_ROWS, rows.start
        gates = gate_ref[rows, :]
        blocks = None
        for k in range(TOP_K):
            gk = gates[:, k:k + 1]
            terms = [gk * v for v in _unpack_load(yk_ref, n, lead=(k,), row0=r0)]
            blocks = terms if blocks is None else [a + b for a, b in zip(blocks, terms)]
        fin = jnp.concatenate([v.astype(BF16) for v in _unpack_load(fin_ref, n, row0=r0)], axis=-1)
        gu = _dot(fin, shgu_ref[...])
        g = gu[:, :D_EXPERT]
        hsh = g * jax.nn.sigmoid(g) * gu[:, D_EXPERT:]
        f = jnp.concatenate(blocks, axis=-1) + _dot(hsh.astype(BF16), shd_ref[...])
        x1 = x1_ref[rows, :]
        o_ref[rows, :] = x1 + _rows(_mod_chunk(mod_ref, 5, d), x1) * _rms(f, g3_ref[...])


def _combine(yk, gates, fin, shgu, shd, x1, mods, g3, tm, rows_per_mod, tok_off):
    t, d = x1.shape
    off = tok_off // tm
    tiles_per_mod = rows_per_mod // tm
    full = lambda a: pl.BlockSpec(a.shape, lambda i: (0,) * a.ndim)
    return pl.pallas_call(
        _combine_kernel,
        out_shape=jax.ShapeDtypeStruct((t, d), F32),
        grid=(t // tm,),
        in_specs=[pl.BlockSpec((TOP_K, tm * PACK_ROWS, LANES), lambda i: (0, i + off, 0)),
                  pl.BlockSpec((tm, TOP_K), lambda i: (i + off, 0)),
                  pl.BlockSpec((tm * PACK_ROWS, LANES), lambda i: (i + off, 0)),
                  full(shgu), full(shd),
                  pl.BlockSpec((tm, d), lambda i: (i, 0)),
                  pl.BlockSpec((None,) + mods.shape[1:], lambda i: (i // tiles_per_mod, 0, 0)),
                  full(g3)],
        out_specs=pl.BlockSpec((tm, d), lambda i: (i, 0)),
        compiler_params=_cparams(("arbitrary",)),
        name="moe_combine",
    )(yk, gates, fin, shgu, shd, x1, mods, g3)


def _moe(fin, logits_t, bias, w_gate, w_up, w_down, layer, tb):
    t = fin.shape[0] // PACK_ROWS
    ne = w_gate.shape[1]
    tt = 512
    eidx_t, gates_t, rank_t, cnt = _route(logits_t, bias, tt)
    counts = cnt[:, 0].astype(jnp.int32)
    padded = (counts + tb - 1) // tb * tb
    pad_end = jnp.cumsum(padded)
    pad_start = pad_end - padded
    nb = (t * TOP_K) // tb + ne
    n_used = pad_end[-1] // tb
    blk_start = jnp.arange(nb, dtype=jnp.int32) * tb
    blk = jnp.sum(pad_end[None, :] <= jnp.minimum(blk_start, pad_end[-1] - 1)[:, None], axis=1)
    blk_e = jnp.minimum(blk, ne - 1).astype(jnp.int32)
    dest = _dest_rows(eidx_t, rank_t, pad_start, tt, SC_CHUNK)
    xs = _sc_scatter_rows(fin.reshape(t, PACK_ROWS, LANES), dest, nb * tb)
    ys = _experts(xs.reshape(nb * tb * PACK_ROWS, LANES), blk_e, n_used.reshape(1).astype(jnp.int32),
                  w_gate, w_up, w_down, layer, tb)
    yk = _sc_gather_rows(ys.reshape(nb * tb, PACK_ROWS, LANES), dest)
    return yk.reshape(TOP_K, t * PACK_ROWS, LANES), gates_t.T


def _pre_s5_kernel(x_ref, mod_ref, g0_ref, h_ref):
    d = x_ref.shape[-1]
    x = x_ref[...]
    h_ref[...] = (_rms(x, g0_ref[...]) * (1.0 + _rows(_mod_chunk(mod_ref, 1, d), x))
                  + _rows(_mod_chunk(mod_ref, 0, d), x))


def _pre_s5(x, mods, n_first, g0, tm):
    t, d = x.shape
    first_tiles = n_first // tm
    return pl.pallas_call(
        _pre_s5_kernel,
        out_shape=jax.ShapeDtypeStruct((t, d), F32),
        grid=(t // tm,),
        in_specs=[pl.BlockSpec((tm, d), lambda i: (i, 0)),
                  pl.BlockSpec((None,) + mods.shape[1:], lambda i: (jnp.where(i < first_tiles, 0, 1), 0, 0)),
                  pl.BlockSpec(g0.shape, lambda i: (0, 0))],
        out_specs=pl.BlockSpec((tm, d), lambda i: (i, 0)),
        compiler_params=_cparams(("arbitrary",)),
        name="pre_s5",
    )(x, mods, g0)


def _s5_scan_kernel(h_ref, bm_ref, cm_ref, lam_ref, y_ref, bu0, bu1, xb0, xb1, st_ref):
    first = jnp.logical_and(jnp.logical_and(pl.program_id(0) == 0, pl.program_id(1) == 0), pl.program_id(2) == 0)
    dr = pl.program_id(1)
    s = pl.program_id(2)
    tc, nb, cw = h_ref.shape
    half = st_ref.shape[1] // 2

    @pl.when(first)
    def _():
        for r in (bu0, bu1, xb0, xb1, st_ref):
            r[...] = jnp.zeros_like(r)

    def stages(bu_w, bu_r, xb_w, xb_r):
        y_ref[...] = _dot(xb_r[...], cm_ref[...]).reshape(tc, nb, cw)
        bu_w[...] = _dot(h_ref[...].reshape(tc * nb, cw).astype(BF16), bm_ref[...])
        lr = jnp.broadcast_to(lam_ref[0:1, :], (nb, half))
        li = jnp.broadcast_to(lam_ref[1:2, :], (nb, half))
        fresh = s == 1
        xr = jnp.where(fresh, 0.0, st_ref[:, 0:half])
        xi = jnp.where(fresh, 0.0, st_ref[:, half:2 * half])
        for i in range(tc):
            t = i + dr * (tc - 1 - 2 * i)
            rows = pl.ds(pl.multiple_of(t * nb, nb), nb)
            nr = lr * xr - li * xi + bu_r[rows, 0:half]
            ni = lr * xi + li * xr + bu_r[rows, half:2 * half]
            xb_w[rows, 0:half] = nr.astype(BF16)
            xb_w[rows, half:2 * half] = ni.astype(BF16)
            xr, xi = nr, ni
        st_ref[:, 0:half] = xr
        st_ref[:, half:2 * half] = xi

    @pl.when(s % 2 == 0)
    def _():
        stages(bu0, bu1, xb1, xb0)

    @pl.when(s % 2 == 1)
    def _():
        stages(bu1, bu0, xb0, xb1)


def _s5_scan(h_all, n_ctx, bm, cm, lam, tc):
    nt, nb, d = h_all.shape
    nl = nt - n_ctx
    cw = S5_GROUPS_PER_BLOCK * S5_GROUP
    sw = 2 * S5_GROUPS_PER_BLOCK * S5_STATE
    ncc, n = n_ctx // tc, nt // tc

    def chunk(dr, j):
        j = jnp.clip(j, 0, n - 1)
        rev = jnp.where(j < ncc, ncc - 1 - j, n - 1 - (j - ncc))
        return jnp.where(dr == 0, j, rev)

    def out_map(g, dr, s):
        return (dr, chunk(dr, jnp.clip(s - 2, ncc, n - 1)) - ncc, 0, g)

    return pl.pallas_call(
        _s5_scan_kernel,
        out_shape=jax.ShapeDtypeStruct((2, nl, nb, d), F32),
        grid=(d // cw, 2, n + 2),
        in_specs=[pl.BlockSpec((tc, nb, cw), lambda g, dr, s: (chunk(dr, s), 0, g)),
                  pl.BlockSpec((None, None, cw, sw), lambda g, dr, s: (dr, g, 0, 0)),
                  pl.BlockSpec((None, None, sw, cw), lambda g, dr, s: (dr, g, 0, 0)),
                  pl.BlockSpec((None, None, 2, sw // 2), lambda g, dr, s: (dr, g, 0, 0))],
        out_specs=pl.BlockSpec((None, tc, nb, cw), out_map),
        scratch_shapes=[pltpu.VMEM((tc * nb, sw), F32), pltpu.VMEM((tc * nb, sw), F32),
                        pltpu.VMEM((tc * nb, sw), BF16), pltpu.VMEM((tc * nb, sw), BF16),
                        pltpu.VMEM((nb, sw), F32)],
        compiler_params=_cparams(("arbitrary", "arbitrary", "arbitrary")),
        name="s5_scan",
    )(h_all, bm, cm, lam)


def _s5_params(lam_re, lam_im, log_step, b_re, b_im, c_re, c_im):
    g, p = lam_re.shape[1:]
    gb = S5_GROUPS_PER_BLOCK
    nblk = g // gb
    step = jnp.exp(log_step)[..., None]
    mag = jnp.exp(lam_re * step)
    lb_re = mag * jnp.cos(lam_im * step)
    lb_im = mag * jnp.sin(lam_im * step)
    den = lam_re * lam_re + lam_im * lam_im
    f_re = ((lb_re - 1.0) * lam_re + lb_im * lam_im) / den
    f_im = (lb_im * lam_re - (lb_re - 1.0) * lam_im) / den
    bb_re = f_re[..., None] * b_re - f_im[..., None] * b_im
    bb_im = f_re[..., None] * b_im + f_im[..., None] * b_re
    eye = jnp.eye(gb, dtype=F32)

    def in_map(w):
        w = w.reshape(2, nblk, gb, p, S5_GROUP)
        return jnp.einsum("dnapi,ab->dnaibp", w, eye).reshape(2, nblk, gb * S5_GROUP, gb * p)

    def out_map(w):
        w = w.reshape(2, nblk, gb, S5_GROUP, p)
        return jnp.einsum("dnaip,ab->dnapbi", w, eye).reshape(2, nblk, gb * p, gb * S5_GROUP)

    bm = jnp.concatenate([in_map(bb_re), in_map(bb_im)], axis=-1).astype(BF16)
    cm = jnp.concatenate([out_map(c_re), out_map(-c_im)], axis=-2).astype(BF16)
    lam = jnp.stack([lb_re.reshape(2, nblk, gb * p), lb_im.reshape(2, nblk, gb * p)], axis=2)
    return bm, cm, lam


def _rope_tables(n_tokens):
    rows = n_tokens // GRID_W
    row = jnp.repeat(jnp.arange(rows), GRID_W).astype(F32)
    col = jnp.tile(jnp.arange(GRID_W), rows).astype(F32)
    n_freq = ROPE_DIM // 4
    inv_freq = ROPE_BASE ** (-jnp.arange(n_freq, dtype=F32) / n_freq)
    ang = jnp.concatenate([row[:, None] * inv_freq, col[:, None] * inv_freq], axis=-1)
    cos, sin = jnp.cos(ang), jnp.sin(ang)
    z = jnp.zeros((n_tokens, 128 - ROPE_DIM), F32)
    return (jnp.concatenate([cos, cos, z], axis=-1), jnp.concatenate([-sin, sin, z], axis=-1))


def _router_halves(w_router):
    wt = w_router.T
    hi = wt.astype(BF16)
    lo = (wt - hi.astype(F32)).astype(BF16)
    return jnp.concatenate([hi, lo], axis=0)


def _split_pairs(w):
    ev, od = w[..., 0::2], w[..., 1::2]
    z = jnp.zeros(w.shape[:-1] + (128 - ROPE_DIM,), w.dtype)
    return jnp.concatenate([ev, od, z], axis=-1), jnp.concatenate([od, ev, z], axis=-1)


def _mla_weights(w_dqkv, w_uq, w_ukv):
    kp, kps = _split_pairs(w_dqkv[:, Q_LORA + KV_LORA:])
    wd = jnp.concatenate([w_dqkv[:, :Q_LORA + KV_LORA], kp, kps], axis=-1).astype(BF16)
    wq3 = w_uq.reshape(Q_LORA, MLA_HEADS, NOPE_DIM + ROPE_DIM)
    qp, qps = _split_pairs(wq3[:, :, NOPE_DIM:])
    wq = jnp.concatenate([wq3[:, :, :NOPE_DIM].reshape(Q_LORA, -1), qp.reshape(Q_LORA, -1),
                          qps.reshape(Q_LORA, -1)], axis=-1).astype(BF16)
    wkv3 = w_ukv.reshape(KV_LORA, MLA_HEADS, NOPE_DIM + V_DIM)
    wkv = jnp.concatenate([wkv3[:, :, :NOPE_DIM].reshape(KV_LORA, -1),
                           wkv3[:, :, NOPE_DIM:].reshape(KV_LORA, -1)], axis=-1).astype(BF16)
    return wd, wq, wkv


@jax.jit
def kernel(x, c, ctx, c_ctx, ada_w, ada_b, norm_g, mla_w_dqkv, mla_g_q, mla_g_kv, mla_w_uq, mla_w_ukv, mla_w_o, s5_lam_re, s5_lam_im, s5_log_step, s5_b_re, s5_b_im, s5_c_re, s5_c_im, s5_d, s5_w_glu, s5_b_glu, moe_w_router, moe_bias, moe_w_gate, moe_w_up, moe_w_down, sh_w_gate, sh_w_up, sh_w_down):
    b, l, d = x.shape
    n_ctx = ctx.shape[1]
    assert ada_w.shape[0] == 2 and b % 8 == 0
    ta = 256
    tm = 512
    tb = 512
    row = lambda v: v.reshape(1, -1)

    n_rows = (b + 1 + 7) // 8 * 8
    cvec = jnp.zeros((n_rows, d), F32).at[:b].set(c).at[b].set(c_ctx)
    mods = _ada_mods(cvec, ada_w, ada_b)

    def shared_weights(i):
        shgu = jnp.concatenate([sh_w_gate[i], sh_w_up[i]], axis=-1).astype(BF16)
        return shgu, sh_w_down[i].astype(BF16)

    mod_lat = mods[0, :b].reshape(b, 1, N_MOD * d)
    mod_ctx = mods[0, b].reshape(1, 1, N_MOD * d)
    wd, wq, wkv = _mla_weights(mla_w_dqkv[0], mla_w_uq[0], mla_w_ukv[0])
    cos_l, sin_l = _rope_tables(l)
    cos_c = jnp.concatenate([jnp.ones((n_ctx, ROPE_DIM), F32), jnp.zeros((n_ctx, 128 - ROPE_DIM), F32)], -1)
    sin_c = jnp.zeros((n_ctx, 128), F32)
    pre = functools.partial(_pre_mla, g0=row(norm_g[0, 0]), wd=wd, gq=row(mla_g_q[0]), gkv=row(mla_g_kv[0]),
                            wq=wq, wkv=wkv, tm=ta)
    q_c, k_c, v_c = pre(ctx, mod_ctx, cos_t=cos_c, sin_t=sin_c)
    q_l, k_l, v_l = pre(x, mod_lat, cos_t=cos_l, sin_t=sin_l)
    o_l = _attention(q_l, [k_c, k_l], [v_c, v_l], ta)
    o_c = _attention(q_c, [k_c], [v_c], n_ctx)

    wo = mla_w_o[0].astype(BF16)
    wr_t = _router_halves(moe_w_router[0])
    g1, g2, g3 = row(norm_g[0, 1]), row(norm_g[0, 2]), row(norm_g[0, 3])
    post = functools.partial(_post_mixer, _post_proj_kernel, consts=[wo], g1=g1, g2=g2, wr_t=wr_t, tm=tm,
                             name="post_mla")
    o_spec = pl.BlockSpec((tm, o_l.shape[-1]), lambda i: (i, 0))
    x1_c, fin_c, lg_c = post([(o_c.reshape(b * n_ctx, -1), o_spec)], x=ctx.reshape(b * n_ctx, d),
                             n_tok=b * n_ctx, x_off=0, mods=mod_ctx, rows_per_mod=b * n_ctx)
    x1_l, fin_l, lg_l = post([(o_l.reshape(b * l, -1), o_spec)], x=x.reshape(b * l, d), n_tok=b * l, x_off=0,
                             mods=mod_lat, rows_per_mod=l)
    fin = jnp.concatenate([fin_c, fin_l], axis=0)
    lg = jnp.concatenate([lg_c, lg_l], axis=1)
    shgu, shd = shared_weights(0)
    yk, gates = _moe(fin, lg, moe_bias[0], moe_w_gate, moe_w_up, moe_w_down, 0, tb)
    comb = functools.partial(_combine, yk, gates, fin, shgu, shd, g3=g3, tm=tm)
    x2_c = comb(x1=x1_c, mods=mod_ctx, rows_per_mod=b * n_ctx, tok_off=0)
    x2_l = comb(x1=x1_l, mods=mod_lat, rows_per_mod=l, tok_off=b * n_ctx)

    n_all = n_ctx + l
    xt = jnp.concatenate([jnp.swapaxes(x2_c.reshape(b, n_ctx, d), 0, 1),
                          jnp.swapaxes(x2_l.reshape(b, l, d), 0, 1)], axis=0).reshape(n_all * b, d)
    mod_lat = mods[1, :b]
    mod_ctx = jnp.broadcast_to(mods[1, b][None], (b, N_MOD * d))
    h = _pre_s5(xt, jnp.stack([mod_ctx, mod_lat]), n_ctx * b, row(norm_g[1, 0]), 512)
    bm, cm, lam = _s5_params(s5_lam_re[0], s5_lam_im[0], s5_log_step[0], s5_b_re[0], s5_b_im[0],
                             s5_c_re[0], s5_c_im[0])
    y = _s5_scan(h.reshape(n_all, b, d), n_ctx, bm, cm, lam, 64).reshape(2, l * b, d)
    g1, g2, g3 = row(norm_g[1, 1]), row(norm_g[1, 2]), row(norm_g[1, 3])
    lat0 = n_ctx * b // tm
    x1, fin, lg = _post_mixer(
        _post_glu_kernel,
        [(h, pl.BlockSpec((tm, d), lambda i: (i + lat0, 0))),
         (y, pl.BlockSpec((None, tm, d), lambda i: (0, i, 0))),
         (y, pl.BlockSpec((None, tm, d), lambda i: (1, i, 0)))],
        [row(s5_d[0]), s5_w_glu[0].astype(BF16), row(s5_b_glu[0])],
        x=xt, n_tok=l * b, x_off=n_ctx * b, mods=mod_lat[None], g1=g1, g2=g2, wr_t=_router_halves(moe_w_router[1]), tm=tm,
        rows_per_mod=l * b, name="post_s5")
    shgu, shd = shared_weights(1)
    yk, gates = _moe(fin, lg, moe_bias[1], moe_w_gate, moe_w_up, moe_w_down, 1, tb)
    x2 = _combine(yk, gates, fin, shgu, shd, x1, mod_lat[None], g3, tm, rows_per_mod=l * b, tok_off=0)
    return jnp.swapaxes(x2.reshape(l, b, d), 0, 1)
```

```python
import functools

import jax
import jax.numpy as jnp
from jax import lax
from jax.experimental import pallas as pl
from jax.experimental.pallas import tpu as pltpu
from jax.experimental.pallas import tpu_sc as plsc

F32 = jnp.float32
BF16 = jnp.bfloat16
U32 = jnp.uint32

N_MOD = 6
NORM_EPS = 1e-6
LOG2_E = 1.4426950408889634
GRID_W = 64
MLA_HEADS = 8
Q_LORA = 384
KV_LORA = 256
NOPE_DIM = 128
ROPE_DIM = 64
V_DIM = 128
V_PAD = 256
ROPE_BASE = 10000.0
QK_PAD = 256
S5_GROUP = 16
S5_STATE = 64
S5_GROUPS_PER_BLOCK = 8
N_EXPERTS = 64
TOP_K = 8
N_EXPERT_GROUPS = 8
TOPK_GROUPS = 4
D_EXPERT = 256
ROUTED_SCALE = 2.5

VMEM_LIMIT = 56 * 1024 * 1024


def _cparams(sem):
    return pltpu.CompilerParams(dimension_semantics=sem, vmem_limit_bytes=VMEM_LIMIT)


def _rms(x, g):
    return x * lax.rsqrt(jnp.mean(x * x, axis=-1, keepdims=True) + NORM_EPS) * g


def _rows(v, like):
    r = v.shape[0]
    if r == 1:
        return v
    tm, d = like.shape
    return jnp.broadcast_to(v[None], (tm // r, r, d)).reshape(tm, d)


def _mod_chunk(mod_ref, j, d):
    return mod_ref[:, j * d:(j + 1) * d]


def _dot(a, b):
    return jnp.dot(a, b, preferred_element_type=F32)


PACK_ROWS = 4
LANES = 128


def _pack_store(ref, val, lead=(), row0=0):
    n = val.shape[0]
    bits = lax.bitcast_convert_type(val.astype(BF16).astype(F32), U32)
    for s in range(PACK_ROWS):
        lo = bits[:, s * LANES:(s + 1) * LANES] >> 16
        hi = bits[:, (s + PACK_ROWS) * LANES:(s + PACK_ROWS + 1) * LANES] & jnp.uint32(0xFFFF0000)
        ref[lead + (pl.ds(row0 * PACK_ROWS + s, n, stride=PACK_ROWS), slice(None))] = lo | hi


def _unpack_load(ref, n, lead=(), row0=0):
    los, his = [], []
    for s in range(PACK_ROWS):
        w = ref[lead + (pl.ds(row0 * PACK_ROWS + s, n, stride=PACK_ROWS), slice(None))]
        los.append(lax.bitcast_convert_type(w << 16, F32))
        his.append(lax.bitcast_convert_type(w & jnp.uint32(0xFFFF0000), F32))
    return los + his


def _ada_kernel(c_ref, w_ref, b_ref, o_ref):
    c = c_ref[...]
    s = c * jax.nn.sigmoid(c)
    o_ref[...] = jnp.dot(s, w_ref[...], preferred_element_type=F32,
                         precision=lax.Precision.HIGHEST) + b_ref[...]


def _ada_mods(cvec, ada_w, ada_b):
    depth, d, n = ada_w.shape
    rows = cvec.shape[0]
    tn = 1536
    return pl.pallas_call(
        _ada_kernel,
        out_shape=jax.ShapeDtypeStruct((depth, rows, n), F32),
        grid=(depth, n // tn),
        in_specs=[pl.BlockSpec((rows, d), lambda l, j: (0, 0)),
                  pl.BlockSpec((None, d, tn), lambda l, j: (l, 0, j)),
                  pl.BlockSpec((None, 1, tn), lambda l, j: (l, 0, j))],
        out_specs=pl.BlockSpec((None, rows, tn), lambda l, j: (l, 0, j)),
        compiler_params=_cparams(("arbitrary", "arbitrary")),
        name="ada_mods",
    )(cvec, ada_w, ada_b.reshape(depth, 1, n))


def _pre_mla_kernel(x_ref, mod_ref, g0_ref, wd_ref, gq_ref, gkv_ref, wq_ref, wkv_ref, cos_ref, sin_ref,
                    q_ref, k_ref, v_ref):
    d = x_ref.shape[-1]
    x = x_ref[...]
    h = _rms(x, g0_ref[...]) * (1.0 + _mod_chunk(mod_ref, 1, d)) + _mod_chunk(mod_ref, 0, d)
    a = _dot(h.astype(BF16), wd_ref[...])
    cq = _rms(a[:, :Q_LORA], gq_ref[...])
    ckv = _rms(a[:, Q_LORA:Q_LORA + KV_LORA], gkv_ref[...])
    cos = cos_ref[...]
    sin = sin_ref[...]
    o = Q_LORA + KV_LORA
    k_rot = (a[:, o:o + 128] * cos + a[:, o + 128:o + 256] * sin).astype(BF16)
    qa = _dot(cq.astype(BF16), wq_ref[...])
    kva = _dot(ckv.astype(BF16), wkv_ref[...])
    hw = MLA_HEADS * 128
    scale = (NOPE_DIM + ROPE_DIM) ** -0.5 * LOG2_E
    for hd in range(MLA_HEADS):
        lo = hd * 128
        q_rot = qa[:, hw + lo:hw + lo + 128] * cos + qa[:, 2 * hw + lo:2 * hw + lo + 128] * sin
        q_ref[:, hd * QK_PAD:hd * QK_PAD + 128] = (qa[:, lo:lo + 128] * scale).astype(BF16)
        q_ref[:, hd * QK_PAD + 128:(hd + 1) * QK_PAD] = (q_rot * scale).astype(BF16)
        k_ref[:, hd * QK_PAD:hd * QK_PAD + 128] = kva[:, lo:lo + 128].astype(BF16)
        k_ref[:, hd * QK_PAD + 128:(hd + 1) * QK_PAD] = k_rot
        v_ref[:, hd * V_PAD:hd * V_PAD + V_DIM] = kva[:, hw + lo:hw + lo + 128].astype(BF16)
        v_ref[:, hd * V_PAD + V_DIM:(hd + 1) * V_PAD] = jnp.ones((x.shape[0], V_PAD - V_DIM), BF16)


def _pre_mla(x, mods, g0, wd, gq, gkv, wq, wkv, cos_t, sin_t, tm):
    b, n, d = x.shape
    nb_mod = mods.shape[0]
    full = lambda a: pl.BlockSpec(a.shape, lambda i, j: (0,) * a.ndim)
    mod_map = (lambda i, j: (i, 0, 0)) if nb_mod > 1 else (lambda i, j: (0, 0, 0))
    qk_w = MLA_HEADS * QK_PAD
    v_w = MLA_HEADS * V_PAD
    return pl.pallas_call(
        _pre_mla_kernel,
        out_shape=(jax.ShapeDtypeStruct((b, n, qk_w), BF16),
                   jax.ShapeDtypeStruct((b, n, qk_w), BF16),
                   jax.ShapeDtypeStruct((b, n, v_w), BF16)),
        grid=(b, n // tm),
        in_specs=[pl.BlockSpec((None, tm, d), lambda i, j: (i, j, 0)),
                  pl.BlockSpec((None, 1, mods.shape[-1]), mod_map),
                  full(g0), full(wd), full(gq), full(gkv), full(wq), full(wkv),
                  pl.BlockSpec((tm, 128), lambda i, j: (j, 0)),
                  pl.BlockSpec((tm, 128), lambda i, j: (j, 0))],
        out_specs=(pl.BlockSpec((None, tm, qk_w), lambda i, j: (i, j, 0)),
                   pl.BlockSpec((None, tm, qk_w), lambda i, j: (i, j, 0)),
                   pl.BlockSpec((None, tm, v_w), lambda i, j: (i, j, 0))),
        compiler_params=_cparams(("arbitrary", "arbitrary")),
        name="pre_mla",
    )(x, mods, g0, wd, gq, gkv, wq, wkv, cos_t, sin_t)


def _attn_kernel(*refs, n_seg):
    q_ref = refs[0]
    k_refs = refs[1:1 + n_seg]
    v_refs = refs[1 + n_seg:1 + 2 * n_seg]
    o_ref = refs[1 + 2 * n_seg]
    nt = (((1,), (1,)), ((), ()))
    for hd in range(MLA_HEADS):
        q = q_ref[:, hd * QK_PAD:(hd + 1) * QK_PAD]
        ss = [lax.dot_general(q, k[:, hd * QK_PAD:(hd + 1) * QK_PAD], nt, preferred_element_type=F32)
              for k in k_refs]
        m = ss[0].max(axis=-1, keepdims=True)
        for s in ss[1:]:
            m = jnp.maximum(m, s.max(axis=-1, keepdims=True))
        acc = None
        for s, v in zip(ss, v_refs):
            pv = _dot(jnp.exp2((s - m).astype(BF16)), v[:, hd * V_PAD:(hd + 1) * V_PAD])
            acc = pv if acc is None else acc + pv
        o_ref[:, hd * V_DIM:(hd + 1) * V_DIM] = (acc[:, :V_DIM] / acc[:, V_DIM:V_DIM + 1]).astype(BF16)


def _attention(q, ks, vs, tq):
    b, nq, qk_w = q.shape
    v_w = MLA_HEADS * V_DIM
    kv_spec = lambda a: pl.BlockSpec((None,) + a.shape[1:], lambda i, j: (i, 0, 0))
    return pl.pallas_call(
        functools.partial(_attn_kernel, n_seg=len(ks)),
        out_shape=jax.ShapeDtypeStruct((b, nq, v_w), BF16),
        grid=(b, nq // tq),
        in_specs=[pl.BlockSpec((None, tq, qk_w), lambda i, j: (i, j, 0))]
                 + [kv_spec(a) for a in ks] + [kv_spec(a) for a in vs],
        out_specs=pl.BlockSpec((None, tq, v_w), lambda i, j: (i, j, 0)),
        compiler_params=_cparams(("arbitrary", "arbitrary")),
        name="mla_attention",
    )(q, *ks, *vs)


SUB_ROWS = 256


def _sub_tiles(n):
    return [slice(r, r + SUB_ROWS) for r in range(0, n, SUB_ROWS)]


def _post_core(o, x, rows, mod_ref, g1_ref, g2_ref, wr_ref, x1_ref, fin_ref, lg_ref):
    d = x.shape[-1]
    ne = lg_ref.shape[0]
    gate = _rows(_mod_chunk(mod_ref, 2, d), x)
    shift = _rows(_mod_chunk(mod_ref, 3, d), x)
    scale = _rows(_mod_chunk(mod_ref, 4, d), x)
    x1 = x + gate * _rms(o, g1_ref[...])
    fin = _rms(x1, g2_ref[...]) * (1.0 + scale) + shift
    x1_ref[rows, :] = x1
    _pack_store(fin_ref, fin, row0=rows.start)
    nt = (((1,), (1,)), ((), ()))
    f_hi = fin.astype(BF16)
    f_lo = (fin - f_hi.astype(F32)).astype(BF16)
    r_hi = lax.dot_general(wr_ref[...], f_hi, nt, preferred_element_type=F32)
    r_lo = lax.dot_general(wr_ref[0:ne, :], f_lo, nt, preferred_element_type=F32)
    lg_ref[:, rows] = r_hi[:ne] + r_hi[ne:] + r_lo


def _post_proj_kernel(o_ref, wo_ref, x_ref, mod_ref, g1_ref, g2_ref, wr_ref, *rest):
    x1_ref, fin_ref, lg_ref = rest[-3:]
    for rows in _sub_tiles(x_ref.shape[0]):
        o = _dot(o_ref[rows, :], wo_ref[...])
        _post_core(o, x_ref[rows, :], rows, mod_ref, g1_ref, g2_ref, wr_ref, x1_ref, fin_ref, lg_ref)


def _post_glu_kernel(h_ref, yf_ref, yb_ref, dsk_ref, wg_ref, bg_ref, x_ref, mod_ref, g1_ref, g2_ref, wr_ref,
                     *rest):
    x1_ref, fin_ref, lg_ref = rest[-3:]
    d = x_ref.shape[-1]
    for rows in _sub_tiles(x_ref.shape[0]):
        y = h_ref[rows, :] * dsk_ref[...] + yf_ref[rows, :] + yb_ref[rows, :]
        z = _dot(jax.nn.gelu(y, approximate=True).astype(BF16), wg_ref[...]) + bg_ref[...]
        o = z[:, :d] * jax.nn.sigmoid(z[:, d:])
        _post_core(o, x_ref[rows, :], rows, mod_ref, g1_ref, g2_ref, wr_ref, x1_ref, fin_ref, lg_ref)


def _post_mixer(kernel, tok_inputs, consts, x, n_tok, x_off, mods, g1, g2, wr_t, tm, rows_per_mod, name,
                moe_total=None, moe_off=0, prev=None):
    d = x.shape[-1]
    ne = wr_t.shape[0] // 2
    moe_total = n_tok if moe_total is None else moe_total
    tiles_per_mod = rows_per_mod // tm
    xo, mo = x_off // tm, moe_off // tm
    full = lambda a: pl.BlockSpec(a.shape, lambda i: (0,) * a.ndim)
    tile = pl.BlockSpec((tm, d), lambda i: (i, 0))
    mod_spec = pl.BlockSpec((None,) + mods.shape[1:], lambda i: (i // tiles_per_mod, 0, 0))
    in_specs = ([spec for _, spec in tok_inputs] + [full(a) for a in consts]
                + [pl.BlockSpec((tm, d), lambda i: (i + xo, 0)), mod_spec, full(g1), full(g2), full(wr_t)])
    args = [a for a, _ in tok_inputs] + list(consts) + [x, mods, g1, g2, wr_t]
    aliases = {}
    if prev is not None:
        aliases = {len(args): 1, len(args) + 1: 2}
        in_specs += [pl.BlockSpec(memory_space=pl.ANY)] * 2
        args += list(prev)
    return pl.pallas_call(
        kernel,
        out_shape=(jax.ShapeDtypeStruct((n_tok, d), F32),
                   jax.ShapeDtypeStruct((moe_total * PACK_ROWS, LANES), U32),
                   jax.ShapeDtypeStruct((ne, moe_total), F32)),
        grid=(n_tok // tm,),
        in_specs=in_specs,
        out_specs=(tile, pl.BlockSpec((tm * PACK_ROWS, LANES), lambda i: (i + mo, 0)),
                   pl.BlockSpec((ne, tm), lambda i: (0, i + mo))),
        input_output_aliases=aliases,
        compiler_params=_cparams(("arbitrary",)),
        name=name,
    )(*args)


def _route_kernel(lg_ref, bias_ref, eidx_ref, gate_ref, rank_ref, cnt_ref, tri_ref, base_ref):
    i = pl.program_id(0)
    ne, tt = lg_ref.shape
    gsz = ne // N_EXPERT_GROUPS
    shp = (N_EXPERT_GROUPS, gsz, tt)
    neg = -jnp.inf

    @pl.when(i == 0)
    def _():
        base_ref[...] = jnp.zeros_like(base_ref)
        r = lax.broadcasted_iota(jnp.int32, (tt, tt), 0)
        c = lax.broadcasted_iota(jnp.int32, (tt, tt), 1)
        tri_ref[...] = (r < c).astype(BF16)

    scores = jax.nn.sigmoid(lg_ref[...])
    s3 = scores.reshape(shp)
    b3 = (scores + bias_ref[...]).reshape(shp)
    io_e = lax.broadcasted_iota(jnp.int32, shp, 1)
    io_g = lax.broadcasted_iota(jnp.int32, shp, 0)
    io_flat = io_g * gsz + io_e
    m1 = b3.max(axis=1, keepdims=True)
    i1 = jnp.where(b3 == m1, io_e, gsz).min(axis=1, keepdims=True)
    m2 = jnp.where(io_e == i1, neg, b3).max(axis=1, keepdims=True)
    cur = jnp.broadcast_to(m1 + m2, shp)
    gsel = jnp.zeros(shp, jnp.bool_)
    for _ in range(TOPK_GROUPS):
        m = cur.max(axis=0, keepdims=True)
        gi = jnp.where(cur == m, io_g, N_EXPERT_GROUPS).min(axis=0, keepdims=True)
        hit = io_g == gi
        gsel = jnp.logical_or(gsel, hit)
        cur = jnp.where(hit, neg, cur)
    cand = jnp.where(gsel, b3, neg)
    sel = jnp.zeros(shp, jnp.bool_)
    eids, gts = [], []
    for _ in range(TOP_K):
        m = cand.max(axis=0, keepdims=True).max(axis=1, keepdims=True)
        ei = jnp.where(cand == m, io_flat, ne).min(axis=0, keepdims=True).min(axis=1, keepdims=True)
        hit = io_flat == ei
        gts.append(jnp.where(hit, s3, 0.0).sum(axis=0, keepdims=True).sum(axis=1, keepdims=True))
        eids.append(ei)
        sel = jnp.logical_or(sel, hit)
        cand = jnp.where(hit, neg, cand)
    gsum = gts[0]
    for g in gts[1:]:
        gsum = gsum + g
    self32 = sel.astype(F32).reshape(ne, tt)
    cnt = _dot(self32.astype(BF16), tri_ref[...]) + base_ref[...]
    cnt3 = cnt.reshape(shp)
    for k in range(TOP_K):
        hit = io_flat == eids[k]
        rk = jnp.where(hit, cnt3, 0.0).sum(axis=0, keepdims=True).sum(axis=1, keepdims=True)
        rank_ref[k:k + 1, :] = rk.reshape(1, tt).astype(jnp.int32)
        eidx_ref[k:k + 1, :] = eids[k].reshape(1, tt)
        gate_ref[k:k + 1, :] = (gts[k] / gsum * ROUTED_SCALE).reshape(1, tt)
    base_new = base_ref[...] + self32.sum(axis=1, keepdims=True)
    base_ref[...] = base_new
    cnt_ref[...] = jnp.broadcast_to(base_new, cnt_ref.shape)


def _route(logits_t, bias, tt):
    ne, t = logits_t.shape
    out_i = jax.ShapeDtypeStruct((TOP_K, t), jnp.int32)
    row = pl.BlockSpec((TOP_K, tt), lambda i: (0, i))
    return pl.pallas_call(
        _route_kernel,
        out_shape=(out_i, jax.ShapeDtypeStruct((TOP_K, t), F32), out_i,
                   jax.ShapeDtypeStruct((ne, 128), F32)),
        grid=(t // tt,),
        in_specs=[pl.BlockSpec((ne, tt), lambda i: (0, i)),
                  pl.BlockSpec((ne, 1), lambda i: (0, 0))],
        out_specs=(row, row, row, pl.BlockSpec((ne, 128), lambda i: (0, 0))),
        scratch_shapes=[pltpu.VMEM((tt, tt), BF16), pltpu.VMEM((ne, 1), F32)],
        compiler_params=_cparams(("arbitrary",)),
        name="moe_route",
    )(logits_t, bias.reshape(ne, 1))


def _dest_kernel(eidx_ref, rank_ref, start_ref, dest_ref):
    kk, tt = eidx_ref.shape
    ne = start_ref.shape[0]
    n_chunk, _, r = dest_ref.shape
    io_e = lax.broadcasted_iota(jnp.int32, (ne, tt), 0)
    start = start_ref[...]
    for k in range(kk):
        hit = io_e == eidx_ref[k:k + 1, :]
        dk = jnp.where(hit, start, 0).sum(axis=0, keepdims=True) + rank_ref[k:k + 1, :]
        for c in range(n_chunk):
            dest_ref[c, k:k + 1, :] = dk[:, c * r:(c + 1) * r]


def _dest_rows(eidx_t, rank_t, start, tt, r):
    kk, t = eidx_t.shape
    ne = start.shape[0]
    return pl.pallas_call(
        _dest_kernel,
        out_shape=jax.ShapeDtypeStruct((t // r, kk, r), jnp.int32),
        grid=(t // tt,),
        in_specs=[pl.BlockSpec((kk, tt), lambda i: (0, i)),
                  pl.BlockSpec((kk, tt), lambda i: (0, i)),
                  pl.BlockSpec((ne, 1), lambda i: (0, 0))],
        out_specs=pl.BlockSpec((tt // r, kk, r), lambda i: (i, 0, 0)),
        compiler_params=_cparams(("arbitrary",)),
        name="moe_dest",
    )(eidx_t, rank_t, start.reshape(ne, 1))


SC_CHUNK = 64


def _sc_mesh():
    return plsc.VectorSubcoreMesh(core_axis_name="c", subcore_axis_name="s")


def _sc_workers():
    info = plsc.get_sparse_core_info()
    return info.num_cores, info.num_cores * info.num_subcores


def _sc_scatter_rows(rows, dest, n_out):
    t = rows.shape[0]
    n_chunk, kk, r = dest.shape
    nc, nw = _sc_workers()
    cpw = n_chunk // nw
    assert cpw * nw == n_chunk and cpw % 2 == 0 and n_chunk * r == t

    @functools.partial(
        pl.kernel, mesh=_sc_mesh(),
        out_type=jax.ShapeDtypeStruct((n_out,) + rows.shape[1:], rows.dtype),
        scratch_types=[pltpu.VMEM((2, kk, r), jnp.int32), pltpu.VMEM((2, r) + rows.shape[1:], rows.dtype),
                       pltpu.SemaphoreType.DMA((2,)), pltpu.SemaphoreType.DMA((2,))])
    def scatter(rows_hbm, dest_hbm, out_hbm, idx_v, rows_v, load_sem, scat_sem):
        c0 = (lax.axis_index("s") * nc + lax.axis_index("c")) * cpw

        def loads(c, b):
            return (pltpu.make_async_copy(dest_hbm.at[c], idx_v.at[b], load_sem.at[b]),
                    pltpu.make_async_copy(rows_hbm.at[pl.ds(c * r, r)], rows_v.at[b], load_sem.at[b]))

        def scat(b, k):
            return pltpu.make_async_copy(rows_v.at[b], out_hbm.at[idx_v.at[b, k]], scat_sem.at[b])

        for cp in loads(c0, 0):
            cp.start()

        @pl.loop(0, cpw, step=2)
        def _(ci):
            for b in range(2):
                c = c0 + ci + b
                for cp in loads(c, b):
                    cp.wait()
                for k in range(kk):
                    scat(b, k).start()

                @pl.when(ci + b >= 1)
                def _():
                    for k in range(kk):
                        scat(1 - b, k).wait()

                @pl.when(ci + b + 1 < cpw)
                def _():
                    for cp in loads(c + 1, 1 - b):
                        cp.start()

        for k in range(kk):
            scat((cpw - 1) % 2, k).wait()

    return scatter(rows, dest)


def _sc_gather_rows(src, dest):
    n_chunk, kk, r = dest.shape
    t = n_chunk * r
    nc, nw = _sc_workers()
    cpw = n_chunk // nw
    nbuf = 3
    assert cpw * nw == n_chunk and kk > nbuf

    @functools.partial(
        pl.kernel, mesh=_sc_mesh(),
        out_type=jax.ShapeDtypeStruct((kk, t) + src.shape[1:], src.dtype),
        scratch_types=[pltpu.VMEM((kk, r), jnp.int32), pltpu.VMEM((nbuf, r) + src.shape[1:], src.dtype),
                       pltpu.SemaphoreType.DMA((nbuf,)), pltpu.SemaphoreType.DMA((nbuf,))])
    def gather(src_hbm, dest_hbm, out_hbm, idx_v, rows_v, get_sem, put_sem):
        c0 = (lax.axis_index("s") * nc + lax.axis_index("c")) * cpw

        @pl.loop(0, cpw)
        def _(ci):
            c = c0 + ci
            pltpu.sync_copy(dest_hbm.at[c], idx_v)

            def get(k):
                return pltpu.make_async_copy(src_hbm.at[idx_v.at[k]], rows_v.at[k % nbuf], get_sem.at[k % nbuf])

            def put(k):
                return pltpu.make_async_copy(rows_v.at[k % nbuf], out_hbm.at[k, pl.ds(c * r, r)],
                                             put_sem.at[k % nbuf])

            for k in range(nbuf - 1):
                get(k).start()
            for k in range(kk):
                get(k).wait()
                put(k).start()
                if k + nbuf - 1 < kk:
                    if k >= 1:
                        put(k - 1).wait()
                    get(k + nbuf - 1).start()
            for k in range(kk - nbuf, kk):
                put(k).wait()

    return gather(src, dest)


def _expert_kernel(be_ref, nu_ref, x_ref, wg_ref, wu_ref, wd_ref, o_ref, wgu_s, wd_s):
    i = pl.program_id(0)
    tb = o_ref.shape[0] // PACK_ROWS

    @pl.when(i < nu_ref[0])
    def _():
        @pl.when(jnp.logical_or(i == 0, be_ref[i] != be_ref[jnp.maximum(i - 1, 0)]))
        def _():
            wgu_s[:, :D_EXPERT] = wg_ref[...].astype(BF16)
            wgu_s[:, D_EXPERT:] = wu_ref[...].astype(BF16)
            wd_s[...] = wd_ref[...].astype(BF16)

        x = jnp.concatenate([v.astype(BF16) for v in _unpack_load(x_ref, tb)], axis=-1)
        gu = _dot(x, wgu_s[...])
        g = gu[:, :D_EXPERT]
        h = g * jax.nn.sigmoid(g) * gu[:, D_EXPERT:]
        _pack_store(o_ref, _dot(h.astype(BF16), wd_s[...]))


def _experts(xs, blk_e, n_used, w_gate, w_up, w_down, layer, tb):
    rows = xs.shape[0] // PACK_ROWS
    _, ne, d, de = w_gate.shape
    nb = rows // tb
    row_map = lambda i, be, nu: (jnp.minimum(i, nu[0] - 1), 0)
    w_map = lambda i, be, nu: (layer, be[i], 0, 0)
    grid_spec = pltpu.PrefetchScalarGridSpec(
        num_scalar_prefetch=2,
        grid=(nb,),
        in_specs=[pl.BlockSpec((tb * PACK_ROWS, LANES), row_map),
                  pl.BlockSpec((None, None, d, de), w_map),
                  pl.BlockSpec((None, None, d, de), w_map),
                  pl.BlockSpec((None, None, de, d), w_map)],
        out_specs=pl.BlockSpec((tb * PACK_ROWS, LANES), row_map),
        scratch_shapes=[pltpu.VMEM((d, 2 * de), BF16), pltpu.VMEM((de, d), BF16)],
    )
    return pl.pallas_call(
        _expert_kernel,
        out_shape=jax.ShapeDtypeStruct(xs.shape, U32),
        grid_spec=grid_spec,
        compiler_params=_cparams(("arbitrary",)),
        name="moe_experts",
    )(blk_e, n_used, xs, w_gate, w_up, w_down)


def _combine_kernel(yk_ref, gate_ref, fin_ref, shgu_ref, shd_ref, x1_ref, mod_ref, g3_ref, o_ref):
    tm, d = x1_ref.shape
    for rows in _sub_tiles(tm):
        n, r0 = SUB_ROWS, rows.start
        gates = gate_ref[rows, :]
        blocks = None
        for k in range(TOP_K):
            gk = gates[:, k:k + 1]
            terms = [gk * v for v in _unpack_load(yk_ref, n, lead=(k,), row0=r0)]
            blocks = terms if blocks is None else [a + b for a, b in zip(blocks, terms)]
        fin = jnp.concatenate([v.astype(BF16) for v in _unpack_load(fin_ref, n, row0=r0)], axis=-1)
        gu = _dot(fin, shgu_ref[...])
        g = gu[:, :D_EXPERT]
        hsh = g * jax.nn.sigmoid(g) * gu[:, D_EXPERT:]
        f = jnp.concatenate(blocks, axis=-1) + _dot(hsh.astype(BF16), shd_ref[...])
        x1 = x1_ref[rows, :]
        x2 = x1 + _rows(_mod_chunk(mod_ref, 5, d), x1) * _rms(f, g3_ref[...])
        if len(o_ref.shape) == 2:
            o_ref[rows, :] = x2
        else:
            nb = o_ref.shape[0]
            ts = SUB_ROWS // nb
            o_ref[:, r0 // nb:r0 // nb + ts, :] = jnp.swapaxes(x2.reshape(ts, nb, d), 0, 1)


def _combine(yk, gates, fin, shgu, shd, x1, mods, g3, tm, rows_per_mod, tok_off, batch_out=0):
    t, d = x1.shape
    off = tok_off // tm
    tiles_per_mod = rows_per_mod // tm
    full = lambda a: pl.BlockSpec(a.shape, lambda i: (0,) * a.ndim)
    if batch_out:
        out_shape = jax.ShapeDtypeStruct((batch_out, t // batch_out, d), F32)
        out_spec = pl.BlockSpec((batch_out, tm // batch_out, d), lambda i: (0, i, 0))
    else:
        out_shape = jax.ShapeDtypeStruct((t, d), F32)
        out_spec = pl.BlockSpec((tm, d), lambda i: (i, 0))
    return pl.pallas_call(
        _combine_kernel,
        out_shape=out_shape,
        grid=(t // tm,),
        in_specs=[pl.BlockSpec((TOP_K, tm * PACK_ROWS, LANES), lambda i: (0, i + off, 0)),
                  pl.BlockSpec((tm, TOP_K), lambda i: (i + off, 0)),
                  pl.BlockSpec((tm * PACK_ROWS, LANES), lambda i: (i + off, 0)),
                  full(shgu), full(shd),
                  pl.BlockSpec((tm, d), lambda i: (i, 0)),
                  pl.BlockSpec((None,) + mods.shape[1:], lambda i: (i // tiles_per_mod, 0, 0)),
                  full(g3)],
        out_specs=out_spec,
        compiler_params=_cparams(("arbitrary",)),
        name="moe_combine",
    )(yk, gates, fin, shgu, shd, x1, mods, g3)


def _moe(fin, logits_t, bias, w_gate, w_up, w_down, layer, tb):
    t = fin.shape[0] // PACK_ROWS
    ne = w_gate.shape[1]
    tt = 512
    eidx_t, gates_t, rank_t, cnt = _route(logits_t, bias, tt)
    counts = cnt[:, 0].astype(jnp.int32)
    padded = (counts + tb - 1) // tb * tb
    pad_end = jnp.cumsum(padded)
    pad_start = pad_end - padded
    nb = (t * TOP_K) // tb + ne
    n_used = pad_end[-1] // tb
    blk_start = jnp.arange(nb, dtype=jnp.int32) * tb
    blk = jnp.sum(pad_end[None, :] <= jnp.minimum(blk_start, pad_end[-1] - 1)[:, None], axis=1)
    blk_e = jnp.minimum(blk, ne - 1).astype(jnp.int32)
    dest = _dest_rows(eidx_t, rank_t, pad_start, tt, SC_CHUNK)
    xs = _sc_scatter_rows(fin.reshape(t, PACK_ROWS, LANES), dest, nb * tb)
    ys = _experts(xs.reshape(nb * tb * PACK_ROWS, LANES), blk_e, n_used.reshape(1).astype(jnp.int32),
                  w_gate, w_up, w_down, layer, tb)
    yk = _sc_gather_rows(ys.reshape(nb * tb, PACK_ROWS, LANES), dest)
    return yk.reshape(TOP_K, t * PACK_ROWS, LANES), gates_t.T


def _pre_s5_kernel(x_ref, mod_ref, g0_ref, *refs):
    h_ref, xt_ref = refs[-2:]
    nb, tt, d = x_ref.shape
    x = jnp.swapaxes(x_ref[...], 0, 1).reshape(tt * nb, d)
    h_ref[...] = (_rms(x, g0_ref[...]) * (1.0 + _rows(_mod_chunk(mod_ref, 1, d), x))
                  + _rows(_mod_chunk(mod_ref, 0, d), x))
    xt_ref[...] = x


def _pre_s5(x, mods, g0, n_total, t_off, prev, tt):
    nb, n, d = x.shape
    off = t_off // tt
    out_shape = (jax.ShapeDtypeStruct((n_total * nb, d), F32),) * 2
    out_spec = pl.BlockSpec((tt * nb, d), lambda i: (i + off, 0))
    in_specs = [pl.BlockSpec((nb, tt, d), lambda i: (0, i, 0)),
                pl.BlockSpec(mods.shape, lambda i: (0, 0)),
                pl.BlockSpec(g0.shape, lambda i: (0, 0))]
    args = (x, mods, g0)
    aliases = {}
    if prev is not None:
        in_specs += [pl.BlockSpec(memory_space=pl.ANY)] * 2
        args += tuple(prev)
        aliases = {3: 0, 4: 1}
    return pl.pallas_call(
        _pre_s5_kernel,
        out_shape=out_shape,
        grid=(n // tt,),
        in_specs=in_specs,
        out_specs=(out_spec, out_spec),
        input_output_aliases=aliases,
        compiler_params=_cparams(("arbitrary",)),
        name="pre_s5",
    )(*args)


def _s5_scan_kernel(h_ref, bm_ref, cm_ref, lam_ref, y_ref, bu0, bu1, xb0, xb1, st_ref):
    first = jnp.logical_and(jnp.logical_and(pl.program_id(0) == 0, pl.program_id(1) == 0), pl.program_id(2) == 0)
    dr = pl.program_id(1)
    s = pl.program_id(2)
    tc, nb, cw = h_ref.shape
    half = st_ref.shape[1] // 2

    @pl.when(first)
    def _():
        for r in (bu0, bu1, xb0, xb1, st_ref):
            r[...] = jnp.zeros_like(r)

    def stages(bu_w, bu_r, xb_w, xb_r):
        y_ref[...] = _dot(xb_r[...], cm_ref[...]).reshape(tc, nb, cw)
        bu_w[...] = _dot(h_ref[...].reshape(tc * nb, cw).astype(BF16), bm_ref[...])
        lr = jnp.broadcast_to(lam_ref[0:1, :], (nb, half))
        li = jnp.broadcast_to(lam_ref[1:2, :], (nb, half))
        fresh = s == 1
        xr = jnp.where(fresh, 0.0, st_ref[:, 0:half])
        xi = jnp.where(fresh, 0.0, st_ref[:, half:2 * half])
        for i in range(tc):
            t = i + dr * (tc - 1 - 2 * i)
            rows = pl.ds(pl.multiple_of(t * nb, nb), nb)
            nr = lr * xr - li * xi + bu_r[rows, 0:half]
            ni = lr * xi + li * xr + bu_r[rows, half:2 * half]
            xb_w[rows, 0:half] = nr.astype(BF16)
            xb_w[rows, half:2 * half] = ni.astype(BF16)
            xr, xi = nr, ni
        st_ref[:, 0:half] = xr
        st_ref[:, half:2 * half] = xi

    @pl.when(s % 2 == 0)
    def _():
        stages(bu0, bu1, xb1, xb0)

    @pl.when(s % 2 == 1)
    def _():
        stages(bu1, bu0, xb0, xb1)


def _s5_scan(h_all, n_ctx, bm, cm, lam, tc):
    nt, nb, d = h_all.shape
    nl = nt - n_ctx
    cw = S5_GROUPS_PER_BLOCK * S5_GROUP
    sw = 2 * S5_GROUPS_PER_BLOCK * S5_STATE
    ncc, n = n_ctx // tc, nt // tc

    def chunk(dr, j):
        j = jnp.clip(j, 0, n - 1)
        rev = jnp.where(j < ncc, ncc - 1 - j, n - 1 - (j - ncc))
        return jnp.where(dr == 0, j, rev)

    def out_map(g, dr, s):
        return (dr, chunk(dr, jnp.clip(s - 2, ncc, n - 1)) - ncc, 0, g)

    return pl.pallas_call(
        _s5_scan_kernel,
        out_shape=jax.ShapeDtypeStruct((2, nl, nb, d), F32),
        grid=(d // cw, 2, n + 2),
        in_specs=[pl.BlockSpec((tc, nb, cw), lambda g, dr, s: (chunk(dr, s), 0, g)),
                  pl.BlockSpec((None, None, cw, sw), lambda g, dr, s: (dr, g, 0, 0)),
                  pl.BlockSpec((None, None, sw, cw), lambda g, dr, s: (dr, g, 0, 0)),
                  pl.BlockSpec((None, None, 2, sw // 2), lambda g, dr, s: (dr, g, 0, 0))],
        out_specs=pl.BlockSpec((None, tc, nb, cw), out_map),
        scratch_shapes=[pltpu.VMEM((tc * nb, sw), F32), pltpu.VMEM((tc * nb, sw), F32),
                        pltpu.VMEM((tc * nb, sw), BF16), pltpu.VMEM((tc * nb, sw), BF16),
                        pltpu.VMEM((nb, sw), F32)],
        compiler_params=_cparams(("arbitrary", "arbitrary", "arbitrary")),
        name="s5_scan",
    )(h_all, bm, cm, lam)


def _s5_params(lam_re, lam_im, log_step, b_re, b_im, c_re, c_im):
    g, p = lam_re.shape[1:]
    gb = S5_GROUPS_PER_BLOCK
    nblk = g // gb
    step = jnp.exp(log_step)[..., None]
    mag = jnp.exp(lam_re * step)
    lb_re = mag * jnp.cos(lam_im * step)
    lb_im = mag * jnp.sin(lam_im * step)
    den = lam_re * lam_re + lam_im * lam_im
    f_re = ((lb_re - 1.0) * lam_re + lb_im * lam_im) / den
    f_im = (lb_im * lam_re - (lb_re - 1.0) * lam_im) / den
    bb_re = f_re[..., None] * b_re - f_im[..., None] * b_im
    bb_im = f_re[..., None] * b_im + f_im[..., None] * b_re
    eye = jnp.eye(gb, dtype=F32)

    def in_map(w):
        w = w.reshape(2, nblk, gb, p, S5_GROUP)
        return jnp.einsum("dnapi,ab->dnaibp", w, eye).reshape(2, nblk, gb * S5_GROUP, gb * p)

    def out_map(w):
        w = w.reshape(2, nblk, gb, S5_GROUP, p)
        return jnp.einsum("dnaip,ab->dnapbi", w, eye).reshape(2, nblk, gb * p, gb * S5_GROUP)

    bm = jnp.concatenate([in_map(bb_re), in_map(bb_im)], axis=-1).astype(BF16)
    cm = jnp.concatenate([out_map(c_re), out_map(-c_im)], axis=-2).astype(BF16)
    lam = jnp.stack([lb_re.reshape(2, nblk, gb * p), lb_im.reshape(2, nblk, gb * p)], axis=2)
    return bm, cm, lam


def _rope_tables(n_tokens):
    rows = n_tokens // GRID_W
    row = jnp.repeat(jnp.arange(rows), GRID_W).astype(F32)
    col = jnp.tile(jnp.arange(GRID_W), rows).astype(F32)
    n_freq = ROPE_DIM // 4
    inv_freq = ROPE_BASE ** (-jnp.arange(n_freq, dtype=F32) / n_freq)
    ang = jnp.concatenate([row[:, None] * inv_freq, col[:, None] * inv_freq], axis=-1)
    cos, sin = jnp.cos(ang), jnp.sin(ang)
    z = jnp.zeros((n_tokens, 128 - ROPE_DIM), F32)
    return (jnp.concatenate([cos, cos, z], axis=-1), jnp.concatenate([-sin, sin, z], axis=-1))


def _router_halves(w_router):
    wt = w_router.T
    hi = wt.astype(BF16)
    lo = (wt - hi.astype(F32)).astype(BF16)
    return jnp.concatenate([hi, lo], axis=0)


def _split_pairs(w):
    ev, od = w[..., 0::2], w[..., 1::2]
    z = jnp.zeros(w.shape[:-1] + (128 - ROPE_DIM,), w.dtype)
    return jnp.concatenate([ev, od, z], axis=-1), jnp.concatenate([od, ev, z], axis=-1)


def _mla_weights(w_dqkv, w_uq, w_ukv):
    kp, kps = _split_pairs(w_dqkv[:, Q_LORA + KV_LORA:])
    wd = jnp.concatenate([w_dqkv[:, :Q_LORA + KV_LORA], kp, kps], axis=-1).astype(BF16)
    wq3 = w_uq.reshape(Q_LORA, MLA_HEADS, NOPE_DIM + ROPE_DIM)
    qp, qps = _split_pairs(wq3[:, :, NOPE_DIM:])
    wq = jnp.concatenate([wq3[:, :, :NOPE_DIM].reshape(Q_LORA, -1), qp.reshape(Q_LORA, -1),
                          qps.reshape(Q_LORA, -1)], axis=-1).astype(BF16)
    wkv3 = w_ukv.reshape(KV_LORA, MLA_HEADS, NOPE_DIM + V_DIM)
    wkv = jnp.concatenate([wkv3[:, :, :NOPE_DIM].reshape(KV_LORA, -1),
                           wkv3[:, :, NOPE_DIM:].reshape(KV_LORA, -1)], axis=-1).astype(BF16)
    return wd, wq, wkv


@jax.jit
def kernel(x, c, ctx, c_ctx, ada_w, ada_b, norm_g, mla_w_dqkv, mla_g_q, mla_g_kv, mla_w_uq, mla_w_ukv, mla_w_o, s5_lam_re, s5_lam_im, s5_log_step, s5_b_re, s5_b_im, s5_c_re, s5_c_im, s5_d, s5_w_glu, s5_b_glu, moe_w_router, moe_bias, moe_w_gate, moe_w_up, moe_w_down, sh_w_gate, sh_w_up, sh_w_down):
    b, l, d = x.shape
    n_ctx = ctx.shape[1]
    assert ada_w.shape[0] == 2 and b % 8 == 0
    ta = 256
    tm = 512
    tb = 512
    row = lambda v: v.reshape(1, -1)

    n_rows = (b + 1 + 7) // 8 * 8
    cvec = jnp.zeros((n_rows, d), F32).at[:b].set(c).at[b].set(c_ctx)
    mods = _ada_mods(cvec, ada_w, ada_b)

    def shared_weights(i):
        shgu = jnp.concatenate([sh_w_gate[i], sh_w_up[i]], axis=-1).astype(BF16)
        return shgu, sh_w_down[i].astype(BF16)

    mod_lat = mods[0, :b].reshape(b, 1, N_MOD * d)
    mod_ctx = mods[0, b].reshape(1, 1, N_MOD * d)
    wd, wq, wkv = _mla_weights(mla_w_dqkv[0], mla_w_uq[0], mla_w_ukv[0])
    cos_l, sin_l = _rope_tables(l)
    cos_c = jnp.concatenate([jnp.ones((n_ctx, ROPE_DIM), F32), jnp.zeros((n_ctx, 128 - ROPE_DIM), F32)], -1)
    sin_c = jnp.zeros((n_ctx, 128), F32)
    pre = functools.partial(_pre_mla, g0=row(norm_g[0, 0]), wd=wd, gq=row(mla_g_q[0]), gkv=row(mla_g_kv[0]),
                            wq=wq, wkv=wkv, tm=ta)
    q_c, k_c, v_c = pre(ctx, mod_ctx, cos_t=cos_c, sin_t=sin_c)
    q_l, k_l, v_l = pre(x, mod_lat, cos_t=cos_l, sin_t=sin_l)
    o_l = _attention(q_l, [k_c, k_l], [v_c, v_l], ta)
    o_c = _attention(q_c, [k_c], [v_c], n_ctx)

    wo = mla_w_o[0].astype(BF16)
    wr_t = _router_halves(moe_w_router[0])
    g1, g2, g3 = row(norm_g[0, 1]), row(norm_g[0, 2]), row(norm_g[0, 3])
    post = functools.partial(_post_mixer, _post_proj_kernel, consts=[wo], g1=g1, g2=g2, wr_t=wr_t, tm=tm,
                             name="post_mla")
    o_spec = pl.BlockSpec((tm, o_l.shape[-1]), lambda i: (i, 0))
    n_moe = b * (n_ctx + l)
    x1_c, fin, lg = post([(o_c.reshape(b * n_ctx, -1), o_spec)], x=ctx.reshape(b * n_ctx, d),
                         n_tok=b * n_ctx, x_off=0, mods=mod_ctx, rows_per_mod=b * n_ctx, moe_total=n_moe)
    x1_l, fin, lg = post([(o_l.reshape(b * l, -1), o_spec)], x=x.reshape(b * l, d), n_tok=b * l, x_off=0,
                         mods=mod_lat, rows_per_mod=l, moe_total=n_moe, moe_off=b * n_ctx, prev=(fin, lg))
    shgu, shd = shared_weights(0)
    yk, gates = _moe(fin, lg, moe_bias[0], moe_w_gate, moe_w_up, moe_w_down, 0, tb)
    comb = functools.partial(_combine, yk, gates, fin, shgu, shd, g3=g3, tm=tm)
    x2_c = comb(x1=x1_c, mods=mod_ctx, rows_per_mod=b * n_ctx, tok_off=0)
    x2_l = comb(x1=x1_l, mods=mod_lat, rows_per_mod=l, tok_off=b * n_ctx)

    n_all = n_ctx + l
    mod_lat = mods[1, :b]
    mod_ctx = jnp.broadcast_to(mods[1, b][None], (b, N_MOD * d))
    g0 = row(norm_g[1, 0])
    tt = tm // b
    h_xt = _pre_s5(x2_c.reshape(b, n_ctx, d), mod_ctx, g0, n_all, 0, None, tt)
    h, xt = _pre_s5(x2_l.reshape(b, l, d), mod_lat, g0, n_all, n_ctx, h_xt, tt)
    bm, cm, lam = _s5_params(s5_lam_re[0], s5_lam_im[0], s5_log_step[0], s5_b_re[0], s5_b_im[0],
                             s5_c_re[0], s5_c_im[0])
    y = _s5_scan(h.reshape(n_all, b, d), n_ctx, bm, cm, lam, 64).reshape(2, l * b, d)
    g1, g2, g3 = row(norm_g[1, 1]), row(norm_g[1, 2]), row(norm_g[1, 3])
    lat0 = n_ctx * b // tm
    x1, fin, lg = _post_mixer(
        _post_glu_kernel,
        [(h, pl.BlockSpec((tm, d), lambda i: (i + lat0, 0))),
         (y, pl.BlockSpec((None, tm, d), lambda i: (0, i, 0))),
         (y, pl.BlockSpec((None, tm, d), lambda i: (1, i, 0)))],
        [row(s5_d[0]), s5_w_glu[0].astype(BF16), row(s5_b_glu[0])],
        x=xt, n_tok=l * b, x_off=n_ctx * b, mods=mod_lat[None], g1=g1, g2=g2, wr_t=_router_halves(moe_w_router[1]), tm=tm,
        rows_per_mod=l * b, name="post_s5")
    shgu, shd = shared_weights(1)
    yk, gates = _moe(fin, lg, moe_bias[1], moe_w_gate, moe_w_up, moe_w_down, 1, tb)
    return _combine(yk, gates, fin, shgu, shd, x1, mod_lat[None], g3, tm, rows_per_mod=l * b, tok_off=0,
                    batch_out=b)
```

```python
import functools

import jax
import jax.numpy as jnp
from jax import lax
from jax.experimental import pallas as pl
from jax.experimental.pallas import tpu as pltpu
from jax.experimental.pallas import tpu_sc as plsc

F32 = jnp.float32
BF16 = jnp.bfloat16
U32 = jnp.uint32

N_MOD = 6
NORM_EPS = 1e-6
LOG2_E = 1.4426950408889634
GRID_W = 64
MLA_HEADS = 8
Q_LORA = 384
KV_LORA = 256
NOPE_DIM = 128
ROPE_DIM = 64
V_DIM = 128
V_PAD = 256
ROPE_BASE = 10000.0
QK_PAD = 256
S5_GROUP = 16
S5_STATE = 64
S5_GROUPS_PER_BLOCK = 8
S5_PIECES = 4
N_EXPERTS = 64
TOP_K = 8
N_EXPERT_GROUPS = 8
TOPK_GROUPS = 4
D_EXPERT = 256
ROUTED_SCALE = 2.5

VMEM_LIMIT = 56 * 1024 * 1024


def _cparams(sem):
    return pltpu.CompilerParams(dimension_semantics=sem, vmem_limit_bytes=VMEM_LIMIT)


def _rms(x, g):
    return x * lax.rsqrt(jnp.mean(x * x, axis=-1, keepdims=True) + NORM_EPS) * g


def _rows(v, like):
    r = v.shape[0]
    if r == 1:
        return v
    tm, d = like.shape
    return jnp.broadcast_to(v[None], (tm // r, r, d)).reshape(tm, d)


def _mod_chunk(mod_ref, j, d):
    return mod_ref[:, j * d:(j + 1) * d]


def _dot(a, b):
    return jnp.dot(a, b, preferred_element_type=F32)


PACK_ROWS = 4
LANES = 128


def _pack_store(ref, val, lead=(), row0=0):
    n = val.shape[0]
    bits = lax.bitcast_convert_type(val.astype(BF16).astype(F32), U32)
    for s in range(PACK_ROWS):
        lo = bits[:, s * LANES:(s + 1) * LANES] >> 16
        hi = bits[:, (s + PACK_ROWS) * LANES:(s + PACK_ROWS + 1) * LANES] & jnp.uint32(0xFFFF0000)
        ref[lead + (pl.ds(row0 * PACK_ROWS + s, n, stride=PACK_ROWS), slice(None))] = lo | hi


def _unpack_load(ref, n, lead=(), row0=0):
    los, his = [], []
    for s in range(PACK_ROWS):
        w = ref[lead + (pl.ds(row0 * PACK_ROWS + s, n, stride=PACK_ROWS), slice(None))]
        los.append(lax.bitcast_convert_type(w << 16, F32))
        his.append(lax.bitcast_convert_type(w & jnp.uint32(0xFFFF0000), F32))
    return los + his


def _ada_kernel(c_ref, w_ref, b_ref, o_ref):
    c = c_ref[...]
    s = c * jax.nn.sigmoid(c)
    o_ref[...] = jnp.dot(s, w_ref[...], preferred_element_type=F32,
                         precision=lax.Precision.HIGHEST) + b_ref[...]


def _ada_mods(cvec, ada_w, ada_b):
    depth, d, n = ada_w.shape
    rows = cvec.shape[0]
    tn = 1536
    return pl.pallas_call(
        _ada_kernel,
        out_shape=jax.ShapeDtypeStruct((depth, rows, n), F32),
        grid=(depth, n // tn),
        in_specs=[pl.BlockSpec((rows, d), lambda l, j: (0, 0)),
                  pl.BlockSpec((None, d, tn), lambda l, j: (l, 0, j)),
                  pl.BlockSpec((None, 1, tn), lambda l, j: (l, 0, j))],
        out_specs=pl.BlockSpec((None, rows, tn), lambda l, j: (l, 0, j)),
        compiler_params=_cparams(("arbitrary", "arbitrary")),
        name="ada_mods",
    )(cvec, ada_w, ada_b.reshape(depth, 1, n))


def _pre_mla_kernel(x_ref, mod_ref, g0_ref, wd_ref, gq_ref, gkv_ref, wq_ref, wkv_ref, cos_ref, sin_ref,
                    q_ref, k_ref, v_ref):
    d = x_ref.shape[-1]
    x = x_ref[...]
    h = _rms(x, g0_ref[...]) * (1.0 + _mod_chunk(mod_ref, 1, d)) + _mod_chunk(mod_ref, 0, d)
    a = _dot(h.astype(BF16), wd_ref[...])
    cq = _rms(a[:, :Q_LORA], gq_ref[...])
    ckv = _rms(a[:, Q_LORA:Q_LORA + KV_LORA], gkv_ref[...])
    cos = cos_ref[...]
    sin = sin_ref[...]
    o = Q_LORA + KV_LORA
    k_rot = (a[:, o:o + 128] * cos + a[:, o + 128:o + 256] * sin).astype(BF16)
    qa = _dot(cq.astype(BF16), wq_ref[...])
    kva = _dot(ckv.astype(BF16), wkv_ref[...])
    hw = MLA_HEADS * 128
    scale = (NOPE_DIM + ROPE_DIM) ** -0.5 * LOG2_E
    for hd in range(MLA_HEADS):
        lo = hd * 128
        q_rot = qa[:, hw + lo:hw + lo + 128] * cos + qa[:, 2 * hw + lo:2 * hw + lo + 128] * sin
        q_ref[:, hd * QK_PAD:hd * QK_PAD + 128] = (qa[:, lo:lo + 128] * scale).astype(BF16)
        q_ref[:, hd * QK_PAD + 128:(hd + 1) * QK_PAD] = (q_rot * scale).astype(BF16)
        k_ref[:, hd * QK_PAD:hd * QK_PAD + 128] = kva[:, lo:lo + 128].astype(BF16)
        k_ref[:, hd * QK_PAD + 128:(hd + 1) * QK_PAD] = k_rot
        v_ref[:, hd * V_PAD:hd * V_PAD + V_DIM] = kva[:, hw + lo:hw + lo + 128].astype(BF16)
        v_ref[:, hd * V_PAD + V_DIM:(hd + 1) * V_PAD] = jnp.ones((x.shape[0], V_PAD - V_DIM), BF16)


def _pre_mla(x, mods, g0, wd, gq, gkv, wq, wkv, cos_t, sin_t, tm):
    b, n, d = x.shape
    nb_mod = mods.shape[0]
    full = lambda a: pl.BlockSpec(a.shape, lambda i, j: (0,) * a.ndim)
    mod_map = (lambda i, j: (i, 0, 0)) if nb_mod > 1 else (lambda i, j: (0, 0, 0))
    qk_w = MLA_HEADS * QK_PAD
    v_w = MLA_HEADS * V_PAD
    return pl.pallas_call(
        _pre_mla_kernel,
        out_shape=(jax.ShapeDtypeStruct((b, n, qk_w), BF16),
                   jax.ShapeDtypeStruct((b, n, qk_w), BF16),
                   jax.ShapeDtypeStruct((b, n, v_w), BF16)),
        grid=(b, n // tm),
        in_specs=[pl.BlockSpec((None, tm, d), lambda i, j: (i, j, 0)),
                  pl.BlockSpec((None, 1, mods.shape[-1]), mod_map),
                  full(g0), full(wd), full(gq), full(gkv), full(wq), full(wkv),
                  pl.BlockSpec((tm, 128), lambda i, j: (j, 0)),
                  pl.BlockSpec((tm, 128), lambda i, j: (j, 0))],
        out_specs=(pl.BlockSpec((None, tm, qk_w), lambda i, j: (i, j, 0)),
                   pl.BlockSpec((None, tm, qk_w), lambda i, j: (i, j, 0)),
                   pl.BlockSpec((None, tm, v_w), lambda i, j: (i, j, 0))),
        compiler_params=_cparams(("arbitrary", "arbitrary")),
        name="pre_mla",
    )(x, mods, g0, wd, gq, gkv, wq, wkv, cos_t, sin_t)


def _attn_kernel(*refs, n_seg):
    q_ref = refs[0]
    k_refs = refs[1:1 + n_seg]
    v_refs = refs[1 + n_seg:1 + 2 * n_seg]
    o_ref = refs[1 + 2 * n_seg]
    nt = (((1,), (1,)), ((), ()))

    def scores(hd):
        q = q_ref[:, hd * QK_PAD:(hd + 1) * QK_PAD]
        return [lax.dot_general(q, k[:, hd * QK_PAD:(hd + 1) * QK_PAD], nt, preferred_element_type=F32)
                for k in k_refs]

    nxt = scores(0)
    for hd in range(MLA_HEADS):
        ss = nxt
        if hd + 1 < MLA_HEADS:
            nxt = scores(hd + 1)
        m = ss[0].max(axis=-1, keepdims=True)
        for s in ss[1:]:
            m = jnp.maximum(m, s.max(axis=-1, keepdims=True))
        acc = None
        for s, v in zip(ss, v_refs):
            pv = _dot(jnp.exp2((s - m).astype(BF16)), v[:, hd * V_PAD:(hd + 1) * V_PAD])
            acc = pv if acc is None else acc + pv
        o_ref[:, hd * V_DIM:(hd + 1) * V_DIM] = (acc[:, :V_DIM] / acc[:, V_DIM:V_DIM + 1]).astype(BF16)


def _attention(q, ks, vs, tq):
    b, nq, qk_w = q.shape
    v_w = MLA_HEADS * V_DIM
    kv_spec = lambda a: pl.BlockSpec((None,) + a.shape[1:], lambda i, j: (i, 0, 0))
    return pl.pallas_call(
        functools.partial(_attn_kernel, n_seg=len(ks)),
        out_shape=jax.ShapeDtypeStruct((b, nq, v_w), BF16),
        grid=(b, nq // tq),
        in_specs=[pl.BlockSpec((None, tq, qk_w), lambda i, j: (i, j, 0))]
                 + [kv_spec(a) for a in ks] + [kv_spec(a) for a in vs],
        out_specs=pl.BlockSpec((None, tq, v_w), lambda i, j: (i, j, 0)),
        compiler_params=_cparams(("arbitrary", "arbitrary")),
        name="mla_attention",
    )(q, *ks, *vs)


SUB_ROWS = 256


def _sub_tiles(n):
    return [slice(r, r + SUB_ROWS) for r in range(0, n, SUB_ROWS)]


def _post_core(o, x, rows, mod_ref, g1_ref, g2_ref, wr_ref, x1_ref, fin_ref, lg_ref):
    d = x.shape[-1]
    ne = lg_ref.shape[0]
    gate = _rows(_mod_chunk(mod_ref, 2, d), x)
    shift = _rows(_mod_chunk(mod_ref, 3, d), x)
    scale = _rows(_mod_chunk(mod_ref, 4, d), x)
    x1 = x + gate * _rms(o, g1_ref[...])
    fin = _rms(x1, g2_ref[...]) * (1.0 + scale) + shift
    x1_ref[rows, :] = x1
    _pack_store(fin_ref, fin, row0=rows.start)
    nt = (((1,), (1,)), ((), ()))
    f_hi = fin.astype(BF16)
    f_lo = (fin - f_hi.astype(F32)).astype(BF16)
    r_hi = lax.dot_general(wr_ref[...], f_hi, nt, preferred_element_type=F32)
    r_lo = lax.dot_general(wr_ref[0:ne, :], f_lo, nt, preferred_element_type=F32)
    lg_ref[:, rows] = r_hi[:ne] + r_hi[ne:] + r_lo


def _post_proj_kernel(o_ref, wo_ref, x_ref, mod_ref, g1_ref, g2_ref, wr_ref, *rest):
    x1_ref, fin_ref, lg_ref = rest[-3:]
    for rows in _sub_tiles(x_ref.shape[0]):
        o = _dot(o_ref[rows, :], wo_ref[...])
        _post_core(o, x_ref[rows, :], rows, mod_ref, g1_ref, g2_ref, wr_ref, x1_ref, fin_ref, lg_ref)


def _post_glu_kernel(h_ref, yf_ref, yb_ref, dsk_ref, wg_ref, bg_ref, x_ref, mod_ref, g1_ref, g2_ref, wr_ref,
                     *rest):
    x1_ref, fin_ref, lg_ref = rest[-3:]
    d = x_ref.shape[-1]
    for rows in _sub_tiles(x_ref.shape[0]):
        y = h_ref[rows, :] * dsk_ref[...] + yf_ref[rows, :] + yb_ref[rows, :]
        z = _dot(jax.nn.gelu(y, approximate=True).astype(BF16), wg_ref[...]) + bg_ref[...]
        o = z[:, :d] * jax.nn.sigmoid(z[:, d:])
        _post_core(o, x_ref[rows, :], rows, mod_ref, g1_ref, g2_ref, wr_ref, x1_ref, fin_ref, lg_ref)


def _post_mixer(kernel, tok_inputs, consts, x, n_tok, x_off, mods, g1, g2, wr_t, tm, rows_per_mod, name,
                moe_total=None, moe_off=0, prev=None):
    d = x.shape[-1]
    ne = wr_t.shape[0] // 2
    moe_total = n_tok if moe_total is None else moe_total
    tiles_per_mod = rows_per_mod // tm
    xo, mo = x_off // tm, moe_off // tm
    full = lambda a: pl.BlockSpec(a.shape, lambda i: (0,) * a.ndim)
    tile = pl.BlockSpec((tm, d), lambda i: (i, 0))
    mod_spec = pl.BlockSpec((None,) + mods.shape[1:], lambda i: (i // tiles_per_mod, 0, 0))
    in_specs = ([spec for _, spec in tok_inputs] + [full(a) for a in consts]
                + [pl.BlockSpec((tm, d), lambda i: (i + xo, 0)), mod_spec, full(g1), full(g2), full(wr_t)])
    args = [a for a, _ in tok_inputs] + list(consts) + [x, mods, g1, g2, wr_t]
    aliases = {}
    if prev is not None:
        aliases = {len(args): 1, len(args) + 1: 2}
        in_specs += [pl.BlockSpec(memory_space=pl.ANY)] * 2
        args += list(prev)
    return pl.pallas_call(
        kernel,
        out_shape=(jax.ShapeDtypeStruct((n_tok, d), F32),
                   jax.ShapeDtypeStruct((moe_total * PACK_ROWS, LANES), U32),
                   jax.ShapeDtypeStruct((ne, moe_total), F32)),
        grid=(n_tok // tm,),
        in_specs=in_specs,
        out_specs=(tile, pl.BlockSpec((tm * PACK_ROWS, LANES), lambda i: (i + mo, 0)),
                   pl.BlockSpec((ne, tm), lambda i: (0, i + mo))),
        input_output_aliases=aliases,
        compiler_params=_cparams(("arbitrary",)),
        name=name,
    )(*args)


def _route_kernel(lg_ref, bias_ref, eidx_ref, gate_ref, rank_ref, cnt_ref, tri_ref, base_ref):
    i = pl.program_id(0)
    ne, tt = lg_ref.shape
    gsz = ne // N_EXPERT_GROUPS
    shp = (N_EXPERT_GROUPS, gsz, tt)
    neg = -jnp.inf

    @pl.when(i == 0)
    def _():
        base_ref[...] = jnp.zeros_like(base_ref)
        r = lax.broadcasted_iota(jnp.int32, (tt, tt), 0)
        c = lax.broadcasted_iota(jnp.int32, (tt, tt), 1)
        tri_ref[...] = (r < c).astype(BF16)

    scores = jax.nn.sigmoid(lg_ref[...])
    s3 = scores.reshape(shp)
    b3 = (scores + bias_ref[...]).reshape(shp)
    io_e = lax.broadcasted_iota(jnp.int32, shp, 1)
    io_g = lax.broadcasted_iota(jnp.int32, shp, 0)
    io_flat = io_g * gsz + io_e
    m1 = b3.max(axis=1, keepdims=True)
    i1 = jnp.where(b3 == m1, io_e, gsz).min(axis=1, keepdims=True)
    m2 = jnp.where(io_e == i1, neg, b3).max(axis=1, keepdims=True)
    cur = jnp.broadcast_to(m1 + m2, shp)
    gsel = jnp.zeros(shp, jnp.bool_)
    for _ in range(TOPK_GROUPS):
        m = cur.max(axis=0, keepdims=True)
        gi = jnp.where(cur == m, io_g, N_EXPERT_GROUPS).min(axis=0, keepdims=True)
        hit = io_g == gi
        gsel = jnp.logical_or(gsel, hit)
        cur = jnp.where(hit, neg, cur)
    cand = jnp.where(gsel, b3, neg)
    sel = jnp.zeros(shp, jnp.bool_)
    eids, gts = [], []
    for _ in range(TOP_K):
        m = cand.max(axis=0, keepdims=True).max(axis=1, keepdims=True)
        ei = jnp.where(cand == m, io_flat, ne).min(axis=0, keepdims=True).min(axis=1, keepdims=True)
        hit = io_flat == ei
        gts.append(jnp.where(hit, s3, 0.0).sum(axis=0, keepdims=True).sum(axis=1, keepdims=True))
        eids.append(ei)
        sel = jnp.logical_or(sel, hit)
        cand = jnp.where(hit, neg, cand)
    gsum = gts[0]
    for g in gts[1:]:
        gsum = gsum + g
    self32 = sel.astype(F32).reshape(ne, tt)
    cnt = _dot(self32.astype(BF16), tri_ref[...]) + base_ref[...]
    cnt3 = cnt.reshape(shp)
    for k in range(TOP_K):
        hit = io_flat == eids[k]
        rk = jnp.where(hit, cnt3, 0.0).sum(axis=0, keepdims=True).sum(axis=1, keepdims=True)
        rank_ref[k:k + 1, :] = rk.reshape(1, tt).astype(jnp.int32)
        eidx_ref[k:k + 1, :] = eids[k].reshape(1, tt)
        gate_ref[k:k + 1, :] = (gts[k] / gsum * ROUTED_SCALE).reshape(1, tt)
    base_new = base_ref[...] + self32.sum(axis=1, keepdims=True)
    base_ref[...] = base_new
    cnt_ref[...] = jnp.broadcast_to(base_new, cnt_ref.shape)


def _route(logits_t, bias, tt):
    ne, t = logits_t.shape
    out_i = jax.ShapeDtypeStruct((TOP_K, t), jnp.int32)
    row = pl.BlockSpec((TOP_K, tt), lambda i: (0, i))
    return pl.pallas_call(
        _route_kernel,
        out_shape=(out_i, jax.ShapeDtypeStruct((TOP_K, t), F32), out_i,
                   jax.ShapeDtypeStruct((ne, 128), F32)),
        grid=(t // tt,),
        in_specs=[pl.BlockSpec((ne, tt), lambda i: (0, i)),
                  pl.BlockSpec((ne, 1), lambda i: (0, 0))],
        out_specs=(row, row, row, pl.BlockSpec((ne, 128), lambda i: (0, 0))),
        scratch_shapes=[pltpu.VMEM((tt, tt), BF16), pltpu.VMEM((ne, 1), F32)],
        compiler_params=_cparams(("arbitrary",)),
        name="moe_route",
    )(logits_t, bias.reshape(ne, 1))


def _dest_kernel(eidx_ref, rank_ref, start_ref, dest_ref):
    kk, tt = eidx_ref.shape
    ne = start_ref.shape[0]
    n_chunk, _, r = dest_ref.shape
    io_e = lax.broadcasted_iota(jnp.int32, (ne, tt), 0)
    start = start_ref[...]
    for k in range(kk):
        hit = io_e == eidx_ref[k:k + 1, :]
        dk = jnp.where(hit, start, 0).sum(axis=0, keepdims=True) + rank_ref[k:k + 1, :]
        for c in range(n_chunk):
            dest_ref[c, k:k + 1, :] = dk[:, c * r:(c + 1) * r]


def _dest_rows(eidx_t, rank_t, start, tt, r):
    kk, t = eidx_t.shape
    ne = start.shape[0]
    return pl.pallas_call(
        _dest_kernel,
        out_shape=jax.ShapeDtypeStruct((t // r, kk, r), jnp.int32),
        grid=(t // tt,),
        in_specs=[pl.BlockSpec((kk, tt), lambda i: (0, i)),
                  pl.BlockSpec((kk, tt), lambda i: (0, i)),
                  pl.BlockSpec((ne, 1), lambda i: (0, 0))],
        out_specs=pl.BlockSpec((tt // r, kk, r), lambda i: (i, 0, 0)),
        compiler_params=_cparams(("arbitrary",)),
        name="moe_dest",
    )(eidx_t, rank_t, start.reshape(ne, 1))


SC_CHUNK = 64


def _sc_mesh():
    return plsc.VectorSubcoreMesh(core_axis_name="c", subcore_axis_name="s")


def _sc_workers():
    info = plsc.get_sparse_core_info()
    return info.num_cores, info.num_cores * info.num_subcores


def _sc_scatter_rows(rows, dest, n_out):
    t = rows.shape[0]
    n_chunk, kk, r = dest.shape
    nc, nw = _sc_workers()
    cpw = n_chunk // nw
    assert cpw * nw == n_chunk and cpw % 2 == 0 and n_chunk * r == t

    @functools.partial(
        pl.kernel, mesh=_sc_mesh(),
        out_type=jax.ShapeDtypeStruct((n_out,) + rows.shape[1:], rows.dtype),
        scratch_types=[pltpu.VMEM((2, kk, r), jnp.int32), pltpu.VMEM((2, r) + rows.shape[1:], rows.dtype),
                       pltpu.SemaphoreType.DMA((2,)), pltpu.SemaphoreType.DMA((2,))])
    def scatter(rows_hbm, dest_hbm, out_hbm, idx_v, rows_v, load_sem, scat_sem):
        c0 = (lax.axis_index("s") * nc + lax.axis_index("c")) * cpw

        def loads(c, b):
            return (pltpu.make_async_copy(dest_hbm.at[c], idx_v.at[b], load_sem.at[b]),
                    pltpu.make_async_copy(rows_hbm.at[pl.ds(c * r, r)], rows_v.at[b], load_sem.at[b]))

        def scat(b, k):
            return pltpu.make_async_copy(rows_v.at[b], out_hbm.at[idx_v.at[b, k]], scat_sem.at[b])

        for cp in loads(c0, 0):
            cp.start()

        @pl.loop(0, cpw, step=2)
        def _(ci):
            for b in range(2):
                c = c0 + ci + b
                for cp in loads(c, b):
                    cp.wait()
                for k in range(kk):
                    scat(b, k).start()

                @pl.when(ci + b >= 1)
                def _():
                    for k in range(kk):
                        scat(1 - b, k).wait()

                @pl.when(ci + b + 1 < cpw)
                def _():
                    for cp in loads(c + 1, 1 - b):
                        cp.start()

        for k in range(kk):
            scat((cpw - 1) % 2, k).wait()

    return scatter(rows, dest)


def _sc_gather_rows(src, dest):
    n_chunk, kk, r = dest.shape
    t = n_chunk * r
    nc, nw = _sc_workers()
    cpw = n_chunk // nw
    nbuf = 3
    assert cpw * nw == n_chunk and kk > nbuf

    @functools.partial(
        pl.kernel, mesh=_sc_mesh(),
        out_type=jax.ShapeDtypeStruct((kk, t) + src.shape[1:], src.dtype),
        scratch_types=[pltpu.VMEM((kk, r), jnp.int32), pltpu.VMEM((nbuf, r) + src.shape[1:], src.dtype),
                       pltpu.SemaphoreType.DMA((nbuf,)), pltpu.SemaphoreType.DMA((nbuf,))])
    def gather(src_hbm, dest_hbm, out_hbm, idx_v, rows_v, get_sem, put_sem):
        c0 = (lax.axis_index("s") * nc + lax.axis_index("c")) * cpw

        @pl.loop(0, cpw)
        def _(ci):
            c = c0 + ci
            pltpu.sync_copy(dest_hbm.at[c], idx_v)

            def get(k):
                return pltpu.make_async_copy(src_hbm.at[idx_v.at[k]], rows_v.at[k % nbuf], get_sem.at[k % nbuf])

            def put(k):
                return pltpu.make_async_copy(rows_v.at[k % nbuf], out_hbm.at[k, pl.ds(c * r, r)],
                                             put_sem.at[k % nbuf])

            for k in range(nbuf - 1):
                get(k).start()
            for k in range(kk):
                get(k).wait()
                put(k).start()
                if k + nbuf - 1 < kk:
                    if k >= 1:
                        put(k - 1).wait()
                    get(k + nbuf - 1).start()
            for k in range(kk - nbuf, kk):
                put(k).wait()

    return gather(src, dest)


def _expert_kernel(be_ref, nu_ref, x_ref, wg_ref, wu_ref, wd_ref, o_ref, wgu_s, wd_s):
    i = pl.program_id(0)
    tb = o_ref.shape[0] // PACK_ROWS

    @pl.when(i < nu_ref[0])
    def _():
        @pl.when(jnp.logical_or(i == 0, be_ref[i] != be_ref[jnp.maximum(i - 1, 0)]))
        def _():
            wgu_s[:, :D_EXPERT] = wg_ref[...].astype(BF16)
            wgu_s[:, D_EXPERT:] = wu_ref[...].astype(BF16)
            wd_s[...] = wd_ref[...].astype(BF16)

        x = jnp.concatenate([v.astype(BF16) for v in _unpack_load(x_ref, tb)], axis=-1)
        gu = _dot(x, wgu_s[...])
        g = gu[:, :D_EXPERT]
        h = g * jax.nn.sigmoid(g) * gu[:, D_EXPERT:]
        _pack_store(o_ref, _dot(h.astype(BF16), wd_s[...]))


def _experts(xs, blk_e, n_used, w_gate, w_up, w_down, layer, tb):
    rows = xs.shape[0] // PACK_ROWS
    _, ne, d, de = w_gate.shape
    nb = rows // tb
    row_map = lambda i, be, nu: (jnp.minimum(i, nu[0] - 1), 0)
    w_map = lambda i, be, nu: (layer, be[i], 0, 0)
    grid_spec = pltpu.PrefetchScalarGridSpec(
        num_scalar_prefetch=2,
        grid=(nb,),
        in_specs=[pl.BlockSpec((tb * PACK_ROWS, LANES), row_map),
                  pl.BlockSpec((None, None, d, de), w_map),
                  pl.BlockSpec((None, None, d, de), w_map),
                  pl.BlockSpec((None, None, de, d), w_map)],
        out_specs=pl.BlockSpec((tb * PACK_ROWS, LANES), row_map),
        scratch_shapes=[pltpu.VMEM((d, 2 * de), BF16), pltpu.VMEM((de, d), BF16)],
    )
    return pl.pallas_call(
        _expert_kernel,
        out_shape=jax.ShapeDtypeStruct(xs.shape, U32),
        grid_spec=grid_spec,
        compiler_params=_cparams(("arbitrary",)),
        name="moe_experts",
    )(blk_e, n_used, xs, w_gate, w_up, w_down)


def _combine_kernel(yk_ref, gate_ref, fin_ref, shgu_ref, shd_ref, x1_ref, mod_ref, g3_ref, o_ref):
    tm, d = x1_ref.shape
    for rows in _sub_tiles(tm):
        n, r0 = SUB_ROWS, rows.start
        gates = gate_ref[rows, :]
        blocks = None
        for k in range(TOP_K):
            gk = gates[:, k:k + 1]
            terms = [gk * v for v in _unpack_load(yk_ref, n, lead=(k,), row0=r0)]
            blocks = terms if blocks is None else [a + b for a, b in zip(blocks, terms)]
        fin = jnp.concatenate([v.astype(BF16) for v in _unpack_load(fin_ref, n, row0=r0)], axis=-1)
        gu = _dot(fin, shgu_ref[...])
        g = gu[:, :D_EXPERT]
        hsh = g * jax.nn.sigmoid(g) * gu[:, D_EXPERT:]
        f = jnp.concatenate(blocks, axis=-1) + _dot(hsh.astype(BF16), shd_ref[...])
        x1 = x1_ref[rows, :]
        x2 = x1 + _rows(_mod_chunk(mod_ref, 5, d), x1) * _rms(f, g3_ref[...])
        if len(o_ref.shape) == 2:
            o_ref[rows, :] = x2
        else:
            nb = o_ref.shape[0]
            ts = SUB_ROWS // nb
            o_ref[:, r0 // nb:r0 // nb + ts, :] = jnp.swapaxes(x2.reshape(ts, nb, d), 0, 1)


def _combine(yk, gates, fin, shgu, shd, x1, mods, g3, tm, rows_per_mod, tok_off, batch_out=0):
    t, d = x1.shape
    off = tok_off // tm
    tiles_per_mod = rows_per_mod // tm
    full = lambda a: pl.BlockSpec(a.shape, lambda i: (0,) * a.ndim)
    if batch_out:
        out_shape = jax.ShapeDtypeStruct((batch_out, t // batch_out, d), F32)
        out_spec = pl.BlockSpec((batch_out, tm // batch_out, d), lambda i: (0, i, 0))
    else:
        out_shape = jax.ShapeDtypeStruct((t, d), F32)
        out_spec = pl.BlockSpec((tm, d), lambda i: (i, 0))
    return pl.pallas_call(
        _combine_kernel,
        out_shape=out_shape,
        grid=(t // tm,),
        in_specs=[pl.BlockSpec((TOP_K, tm * PACK_ROWS, LANES), lambda i: (0, i + off, 0)),
                  pl.BlockSpec((tm, TOP_K), lambda i: (i + off, 0)),
                  pl.BlockSpec((tm * PACK_ROWS, LANES), lambda i: (i + off, 0)),
                  full(shgu), full(shd),
                  pl.BlockSpec((tm, d), lambda i: (i, 0)),
                  pl.BlockSpec((None,) + mods.shape[1:], lambda i: (i // tiles_per_mod, 0, 0)),
                  full(g3)],
        out_specs=out_spec,
        compiler_params=_cparams(("arbitrary",)),
        name="moe_combine",
    )(yk, gates, fin, shgu, shd, x1, mods, g3)


def _moe(fin, logits_t, bias, w_gate, w_up, w_down, layer, tb):
    t = fin.shape[0] // PACK_ROWS
    ne = w_gate.shape[1]
    tt = 512
    eidx_t, gates_t, rank_t, cnt = _route(logits_t, bias, tt)
    counts = cnt[:, 0].astype(jnp.int32)
    padded = (counts + tb - 1) // tb * tb
    pad_end = jnp.cumsum(padded)
    pad_start = pad_end - padded
    nb = (t * TOP_K) // tb + ne
    n_used = pad_end[-1] // tb
    blk_start = jnp.arange(nb, dtype=jnp.int32) * tb
    blk = jnp.sum(pad_end[None, :] <= jnp.minimum(blk_start, pad_end[-1] - 1)[:, None], axis=1)
    blk_e = jnp.minimum(blk, ne - 1).astype(jnp.int32)
    dest = _dest_rows(eidx_t, rank_t, pad_start, tt, SC_CHUNK)
    xs = _sc_scatter_rows(fin.reshape(t, PACK_ROWS, LANES), dest, nb * tb)
    ys = _experts(xs.reshape(nb * tb * PACK_ROWS, LANES), blk_e, n_used.reshape(1).astype(jnp.int32),
                  w_gate, w_up, w_down, layer, tb)
    yk = _sc_gather_rows(ys.reshape(nb * tb, PACK_ROWS, LANES), dest)
    return yk.reshape(TOP_K, t * PACK_ROWS, LANES), gates_t.T


def _pre_s5_kernel(x_ref, mod_ref, g0_ref, *refs):
    h_ref, xt_ref = refs[-2:]
    nb, tt, d = x_ref.shape
    x = jnp.swapaxes(x_ref[...], 0, 1).reshape(tt * nb, d)
    h_ref[...] = (_rms(x, g0_ref[...]) * (1.0 + _rows(_mod_chunk(mod_ref, 1, d), x))
                  + _rows(_mod_chunk(mod_ref, 0, d), x))
    xt_ref[...] = x


def _pre_s5(x, mods, g0, n_total, t_off, prev, tt):
    nb, n, d = x.shape
    off = t_off // tt
    out_shape = (jax.ShapeDtypeStruct((n_total * nb, d), F32),) * 2
    out_spec = pl.BlockSpec((tt * nb, d), lambda i: (i + off, 0))
    in_specs = [pl.BlockSpec((nb, tt, d), lambda i: (0, i, 0)),
                pl.BlockSpec(mods.shape, lambda i: (0, 0)),
                pl.BlockSpec(g0.shape, lambda i: (0, 0))]
    args = (x, mods, g0)
    aliases = {}
    if prev is not None:
        in_specs += [pl.BlockSpec(memory_space=pl.ANY)] * 2
        args += tuple(prev)
        aliases = {3: 0, 4: 1}
    return pl.pallas_call(
        _pre_s5_kernel,
        out_shape=out_shape,
        grid=(n // tt,),
        in_specs=in_specs,
        out_specs=(out_spec, out_spec),
        input_output_aliases=aliases,
        compiler_params=_cparams(("arbitrary",)),
        name="pre_s5",
    )(*args)


def _s5_scan_kernel(h_ref, bm_ref, cm_ref, lam_ref, y_ref, bu0, bu1, xb0, xb1, st_ref):
    first = jnp.logical_and(jnp.logical_and(pl.program_id(0) == 0, pl.program_id(1) == 0), pl.program_id(2) == 0)
    dr = pl.program_id(1)
    s = pl.program_id(2)
    tc, nb, cw = h_ref.shape
    half = st_ref.shape[1] // 2

    @pl.when(first)
    def _():
        for r in (bu0, bu1, xb0, xb1, st_ref):
            r[...] = jnp.zeros_like(r)

    def stages(bu_w, bu_r, xb_w, xb_r):
        lr = jnp.broadcast_to(lam_ref[0:1, :], (nb, half))
        li = jnp.broadcast_to(lam_ref[1:2, :], (nb, half))
        fresh = s == 1
        xr = jnp.where(fresh, 0.0, st_ref[:, 0:half])
        xi = jnp.where(fresh, 0.0, st_ref[:, half:2 * half])
        tp = tc // S5_PIECES
        for p in range(S5_PIECES):
            for i in range(p * tp, (p + 1) * tp):
                t = i + dr * (tc - 1 - 2 * i)
                rows = pl.ds(pl.multiple_of(t * nb, nb), nb)
                nr = lr * xr - li * xi + bu_r[rows, 0:half]
                ni = lr * xi + li * xr + bu_r[rows, half:2 * half]
                xb_w[rows, 0:half] = nr.astype(BF16)
                xb_w[rows, half:2 * half] = ni.astype(BF16)
                xr, xi = nr, ni
            ts = slice(p * tp, (p + 1) * tp)
            mr = slice(p * tp * nb, (p + 1) * tp * nb)
            y_ref[ts] = _dot(xb_r[mr, :], cm_ref[...]).reshape(tp, nb, cw)
            bu_w[mr, :] = _dot(h_ref[ts].reshape(tp * nb, cw).astype(BF16), bm_ref[...])
        st_ref[:, 0:half] = xr
        st_ref[:, half:2 * half] = xi

    @pl.when(s % 2 == 0)
    def _():
        stages(bu0, bu1, xb1, xb0)

    @pl.when(s % 2 == 1)
    def _():
        stages(bu1, bu0, xb0, xb1)


def _s5_scan(h_all, n_ctx, bm, cm, lam, tc):
    nt, nb, d = h_all.shape
    nl = nt - n_ctx
    cw = S5_GROUPS_PER_BLOCK * S5_GROUP
    sw = 2 * S5_GROUPS_PER_BLOCK * S5_STATE
    ncc, n = n_ctx // tc, nt // tc

    def chunk(dr, j):
        j = jnp.clip(j, 0, n - 1)
        rev = jnp.where(j < ncc, ncc - 1 - j, n - 1 - (j - ncc))
        return jnp.where(dr == 0, j, rev)

    def out_map(g, dr, s):
        return (dr, chunk(dr, jnp.clip(s - 2, ncc, n - 1)) - ncc, 0, g)

    return pl.pallas_call(
        _s5_scan_kernel,
        out_shape=jax.ShapeDtypeStruct((2, nl, nb, d), F32),
        grid=(d // cw, 2, n + 2),
        in_specs=[pl.BlockSpec((tc, nb, cw), lambda g, dr, s: (chunk(dr, s), 0, g)),
                  pl.BlockSpec((None, None, cw, sw), lambda g, dr, s: (dr, g, 0, 0)),
                  pl.BlockSpec((None, None, sw, cw), lambda g, dr, s: (dr, g, 0, 0)),
                  pl.BlockSpec((None, None, 2, sw // 2), lambda g, dr, s: (dr, g, 0, 0))],
        out_specs=pl.BlockSpec((None, tc, nb, cw), out_map),
        scratch_shapes=[pltpu.VMEM((tc * nb, sw), F32), pltpu.VMEM((tc * nb, sw), F32),
                        pltpu.VMEM((tc * nb, sw), BF16), pltpu.VMEM((tc * nb, sw), BF16),
                        pltpu.VMEM((nb, sw), F32)],
        compiler_params=_cparams(("arbitrary", "arbitrary", "arbitrary")),
        name="s5_scan",
    )(h_all, bm, cm, lam)


def _s5_params(lam_re, lam_im, log_step, b_re, b_im, c_re, c_im):
    g, p = lam_re.shape[1:]
    gb = S5_GROUPS_PER_BLOCK
    nblk = g // gb
    step = jnp.exp(log_step)[..., None]
    mag = jnp.exp(lam_re * step)
    lb_re = mag * jnp.cos(lam_im * step)
    lb_im = mag * jnp.sin(lam_im * step)
    den = lam_re * lam_re + lam_im * lam_im
    f_re = ((lb_re - 1.0) * lam_re + lb_im * lam_im) / den
    f_im = (lb_im * lam_re - (lb_re - 1.0) * lam_im) / den
    bb_re = f_re[..., None] * b_re - f_im[..., None] * b_im
    bb_im = f_re[..., None] * b_im + f_im[..., None] * b_re
    eye = jnp.eye(gb, dtype=F32)

    def in_map(w):
        w = w.reshape(2, nblk, gb, p, S5_GROUP)
        return jnp.einsum("dnapi,ab->dnaibp", w, eye).reshape(2, nblk, gb * S5_GROUP, gb * p)

    def out_map(w):
        w = w.reshape(2, nblk, gb, S5_GROUP, p)
        return jnp.einsum("dnaip,ab->dnapbi", w, eye).reshape(2, nblk, gb * p, gb * S5_GROUP)

    bm = jnp.concatenate([in_map(bb_re), in_map(bb_im)], axis=-1).astype(BF16)
    cm = jnp.concatenate([out_map(c_re), out_map(-c_im)], axis=-2).astype(BF16)
    lam = jnp.stack([lb_re.reshape(2, nblk, gb * p), lb_im.reshape(2, nblk, gb * p)], axis=2)
    return bm, cm, lam


def _rope_tables(n_tokens):
    rows = n_tokens // GRID_W
    row = jnp.repeat(jnp.arange(rows), GRID_W).astype(F32)
    col = jnp.tile(jnp.arange(GRID_W), rows).astype(F32)
    n_freq = ROPE_DIM // 4
    inv_freq = ROPE_BASE ** (-jnp.arange(n_freq, dtype=F32) / n_freq)
    ang = jnp.concatenate([row[:, None] * inv_freq, col[:, None] * inv_freq], axis=-1)
    cos, sin = jnp.cos(ang), jnp.sin(ang)
    z = jnp.zeros((n_tokens, 128 - ROPE_DIM), F32)
    return (jnp.concatenate([cos, cos, z], axis=-1), jnp.concatenate([-sin, sin, z], axis=-1))


def _router_halves(w_router):
    wt = w_router.T
    hi = wt.astype(BF16)
    lo = (wt - hi.astype(F32)).astype(BF16)
    return jnp.concatenate([hi, lo], axis=0)


def _split_pairs(w):
    ev, od = w[..., 0::2], w[..., 1::2]
    z = jnp.zeros(w.shape[:-1] + (128 - ROPE_DIM,), w.dtype)
    return jnp.concatenate([ev, od, z], axis=-1), jnp.concatenate([od, ev, z], axis=-1)


def _mla_weights(w_dqkv, w_uq, w_ukv):
    kp, kps = _split_pairs(w_dqkv[:, Q_LORA + KV_LORA:])
    wd = jnp.concatenate([w_dqkv[:, :Q_LORA + KV_LORA], kp, kps], axis=-1).astype(BF16)
    wq3 = w_uq.reshape(Q_LORA, MLA_HEADS, NOPE_DIM + ROPE_DIM)
    qp, qps = _split_pairs(wq3[:, :, NOPE_DIM:])
    wq = jnp.concatenate([wq3[:, :, :NOPE_DIM].reshape(Q_LORA, -1), qp.reshape(Q_LORA, -1),
                          qps.reshape(Q_LORA, -1)], axis=-1).astype(BF16)
    wkv3 = w_ukv.reshape(KV_LORA, MLA_HEADS, NOPE_DIM + V_DIM)
    wkv = jnp.concatenate([wkv3[:, :, :NOPE_DIM].reshape(KV_LORA, -1),
                           wkv3[:, :, NOPE_DIM:].reshape(KV_LORA, -1)], axis=-1).astype(BF16)
    return wd, wq, wkv


@jax.jit
def kernel(x, c, ctx, c_ctx, ada_w, ada_b, norm_g, mla_w_dqkv, mla_g_q, mla_g_kv, mla_w_uq, mla_w_ukv, mla_w_o, s5_lam_re, s5_lam_im, s5_log_step, s5_b_re, s5_b_im, s5_c_re, s5_c_im, s5_d, s5_w_glu, s5_b_glu, moe_w_router, moe_bias, moe_w_gate, moe_w_up, moe_w_down, sh_w_gate, sh_w_up, sh_w_down):
    b, l, d = x.shape
    n_ctx = ctx.shape[1]
    assert ada_w.shape[0] == 2 and b % 8 == 0
    ta = 256
    tm = 512
    tb = 512
    row = lambda v: v.reshape(1, -1)

    n_rows = (b + 1 + 7) // 8 * 8
    cvec = jnp.zeros((n_rows, d), F32).at[:b].set(c).at[b].set(c_ctx)
    mods = _ada_mods(cvec, ada_w, ada_b)

    def shared_weights(i):
        shgu = jnp.concatenate([sh_w_gate[i], sh_w_up[i]], axis=-1).astype(BF16)
        return shgu, sh_w_down[i].astype(BF16)

    mod_lat = mods[0, :b].reshape(b, 1, N_MOD * d)
    mod_ctx = mods[0, b].reshape(1, 1, N_MOD * d)
    wd, wq, wkv = _mla_weights(mla_w_dqkv[0], mla_w_uq[0], mla_w_ukv[0])
    cos_l, sin_l = _rope_tables(l)
    cos_c = jnp.concatenate([jnp.ones((n_ctx, ROPE_DIM), F32), jnp.zeros((n_ctx, 128 - ROPE_DIM), F32)], -1)
    sin_c = jnp.zeros((n_ctx, 128), F32)
    pre = functools.partial(_pre_mla, g0=row(norm_g[0, 0]), wd=wd, gq=row(mla_g_q[0]), gkv=row(mla_g_kv[0]),
                            wq=wq, wkv=wkv, tm=ta)
    q_c, k_c, v_c = pre(ctx, mod_ctx, cos_t=cos_c, sin_t=sin_c)
    q_l, k_l, v_l = pre(x, mod_lat, cos_t=cos_l, sin_t=sin_l)
    o_l = _attention(q_l, [k_c, k_l], [v_c, v_l], ta)
    o_c = _attention(q_c, [k_c], [v_c], n_ctx)

    wo = mla_w_o[0].astype(BF16)
    wr_t = _router_halves(moe_w_router[0])
    g1, g2, g3 = row(norm_g[0, 1]), row(norm_g[0, 2]), row(norm_g[0, 3])
    post = functools.partial(_post_mixer, _post_proj_kernel, consts=[wo], g1=g1, g2=g2, wr_t=wr_t, tm=tm,
                             name="post_mla")
    o_spec = pl.BlockSpec((tm, o_l.shape[-1]), lambda i: (i, 0))
    n_moe = b * (n_ctx + l)
    x1_c, fin, lg = post([(o_c.reshape(b * n_ctx, -1), o_spec)], x=ctx.reshape(b * n_ctx, d),
                         n_tok=b * n_ctx, x_off=0, mods=mod_ctx, rows_per_mod=b * n_ctx, moe_total=n_moe)
    x1_l, fin, lg = post([(o_l.reshape(b * l, -1), o_spec)], x=x.reshape(b * l, d), n_tok=b * l, x_off=0,
                         mods=mod_lat, rows_per_mod=l, moe_total=n_moe, moe_off=b * n_ctx, prev=(fin, lg))
    shgu, shd = shared_weights(0)
    yk, gates = _moe(fin, lg, moe_bias[0], moe_w_gate, moe_w_up, moe_w_down, 0, tb)
    comb = functools.partial(_combine, yk, gates, fin, shgu, shd, g3=g3, tm=tm)
    x2_c = comb(x1=x1_c, mods=mod_ctx, rows_per_mod=b * n_ctx, tok_off=0)
    x2_l = comb(x1=x1_l, mods=mod_lat, rows_per_mod=l, tok_off=b * n_ctx)

    n_all = n_ctx + l
    mod_lat = mods[1, :b]
    mod_ctx = jnp.broadcast_to(mods[1, b][None], (b, N_MOD * d))
    g0 = row(norm_g[1, 0])
    tt = tm // b
    h_xt = _pre_s5(x2_c.reshape(b, n_ctx, d), mod_ctx, g0, n_all, 0, None, tt)
    h, xt = _pre_s5(x2_l.reshape(b, l, d), mod_lat, g0, n_all, n_ctx, h_xt, tt)
    bm, cm, lam = _s5_params(s5_lam_re[0], s5_lam_im[0], s5_log_step[0], s5_b_re[0], s5_b_im[0],
                             s5_c_re[0], s5_c_im[0])
    y = _s5_scan(h.reshape(n_all, b, d), n_ctx, bm, cm, lam, 64).reshape(2, l * b, d)
    g1, g2, g3 = row(norm_g[1, 1]), row(norm_g[1, 2]), row(norm_g[1, 3])
    lat0 = n_ctx * b // tm
    x1, fin, lg = _post_mixer(
        _post_glu_kernel,
        [(h, pl.BlockSpec((tm, d), lambda i: (i + lat0, 0))),
         (y, pl.BlockSpec((None, tm, d), lambda i: (0, i, 0))),
         (y, pl.BlockSpec((None, tm, d), lambda i: (1, i, 0)))],
        [row(s5_d[0]), s5_w_glu[0].astype(BF16), row(s5_b_glu[0])],
        x=xt, n_tok=l * b, x_off=n_ctx * b, mods=mod_lat[None], g1=g1, g2=g2, wr_t=_router_halves(moe_w_router[1]), tm=tm,
        rows_per_mod=l * b, name="post_s5")
    shgu, shd = shared_weights(1)
    yk, gates = _moe(fin, lg, moe_bias[1], moe_w_gate, moe_w_up, moe_w_down, 1, tb)
    return _combine(yk, gates, fin, shgu, shd, x1, mod_lat[None], g3, tm, rows_per_mod=l * b, tok_off=0,
                    batch_out=b)
```

```python
import functools

import jax
import jax.numpy as jnp
from jax import lax
from jax.experimental import pallas as pl
from jax.experimental.pallas import tpu as pltpu
from jax.experimental.pallas import tpu_sc as plsc

F32 = jnp.float32
BF16 = jnp.bfloat16
U32 = jnp.uint32

N_MOD = 6
NORM_EPS = 1e-6
LOG2_E = 1.4426950408889634
GRID_W = 64
MLA_HEADS = 8
Q_LORA = 384
KV_LORA = 256
NOPE_DIM = 128
ROPE_DIM = 64
V_DIM = 128
V_PAD = 256
ROPE_BASE = 10000.0
QK_PAD = 256
S5_GROUP = 16
S5_STATE = 64
S5_GROUPS_PER_BLOCK = 8
S5_PIECES = 4
N_EXPERTS = 64
TOP_K = 8
N_EXPERT_GROUPS = 8
TOPK_GROUPS = 4
D_EXPERT = 256
ROUTED_SCALE = 2.5

VMEM_LIMIT = 56 * 1024 * 1024


def _cparams(sem):
    return pltpu.CompilerParams(dimension_semantics=sem, vmem_limit_bytes=VMEM_LIMIT)


def _rms(x, g):
    return x * lax.rsqrt(jnp.mean(x * x, axis=-1, keepdims=True) + NORM_EPS) * g


def _rows(v, like):
    r = v.shape[0]
    if r == 1:
        return v
    tm, d = like.shape
    return jnp.broadcast_to(v[None], (tm // r, r, d)).reshape(tm, d)


def _mod_chunk(mod_ref, j, d):
    return mod_ref[:, j * d:(j + 1) * d]


def _dot(a, b):
    return jnp.dot(a, b, preferred_element_type=F32)


PACK_ROWS = 4
LANES = 128


def _pack_store(ref, val, lead=(), row0=0):
    n = val.shape[0]
    bits = lax.bitcast_convert_type(val.astype(BF16).astype(F32), U32)
    for s in range(PACK_ROWS):
        lo = bits[:, s * LANES:(s + 1) * LANES] >> 16
        hi = bits[:, (s + PACK_ROWS) * LANES:(s + PACK_ROWS + 1) * LANES] & jnp.uint32(0xFFFF0000)
        ref[lead + (pl.ds(row0 * PACK_ROWS + s, n, stride=PACK_ROWS), slice(None))] = lo | hi


def _unpack_load(ref, n, lead=(), row0=0):
    los, his = [], []
    for s in range(PACK_ROWS):
        w = ref[lead + (pl.ds(row0 * PACK_ROWS + s, n, stride=PACK_ROWS), slice(None))]
        los.append(lax.bitcast_convert_type(w << 16, F32))
        his.append(lax.bitcast_convert_type(w & jnp.uint32(0xFFFF0000), F32))
    return los + his


def _ada_kernel(c_ref, w_ref, b_ref, o_ref):
    c = c_ref[...]
    s = c * jax.nn.sigmoid(c)
    o_ref[...] = jnp.dot(s, w_ref[...], preferred_element_type=F32,
                         precision=lax.Precision.HIGHEST) + b_ref[...]


def _ada_mods(cvec, ada_w, ada_b):
    depth, d, n = ada_w.shape
    rows = cvec.shape[0]
    tn = 1536
    return pl.pallas_call(
        _ada_kernel,
        out_shape=jax.ShapeDtypeStruct((depth, rows, n), F32),
        grid=(depth, n // tn),
        in_specs=[pl.BlockSpec((rows, d), lambda l, j: (0, 0)),
                  pl.BlockSpec((None, d, tn), lambda l, j: (l, 0, j)),
                  pl.BlockSpec((None, 1, tn), lambda l, j: (l, 0, j))],
        out_specs=pl.BlockSpec((None, rows, tn), lambda l, j: (l, 0, j)),
        compiler_params=_cparams(("arbitrary", "arbitrary")),
        name="ada_mods",
    )(cvec, ada_w, ada_b.reshape(depth, 1, n))


def _pre_mla_kernel(x_ref, mod_ref, g0_ref, wd_ref, gq_ref, gkv_ref, wq_ref, wkv_ref, cos_ref, sin_ref,
                    q_ref, k_ref, v_ref):
    d = x_ref.shape[-1]
    x = x_ref[...]
    h = _rms(x, g0_ref[...]) * (1.0 + _mod_chunk(mod_ref, 1, d)) + _mod_chunk(mod_ref, 0, d)
    a = _dot(h.astype(BF16), wd_ref[...])
    cq = _rms(a[:, :Q_LORA], gq_ref[...])
    ckv = _rms(a[:, Q_LORA:Q_LORA + KV_LORA], gkv_ref[...])
    cos = cos_ref[...]
    sin = sin_ref[...]
    o = Q_LORA + KV_LORA
    k_rot = (a[:, o:o + 128] * cos + a[:, o + 128:o + 256] * sin).astype(BF16)
    qa = _dot(cq.astype(BF16), wq_ref[...])
    kva = _dot(ckv.astype(BF16), wkv_ref[...])
    hw = MLA_HEADS * 128
    scale = (NOPE_DIM + ROPE_DIM) ** -0.5 * LOG2_E
    for hd in range(MLA_HEADS):
        lo = hd * 128
        q_rot = qa[:, hw + lo:hw + lo + 128] * cos + qa[:, 2 * hw + lo:2 * hw + lo + 128] * sin
        q_ref[:, hd * QK_PAD:hd * QK_PAD + 128] = (qa[:, lo:lo + 128] * scale).astype(BF16)
        q_ref[:, hd * QK_PAD + 128:(hd + 1) * QK_PAD] = (q_rot * scale).astype(BF16)
        k_ref[:, hd * QK_PAD:hd * QK_PAD + 128] = kva[:, lo:lo + 128].astype(BF16)
        k_ref[:, hd * QK_PAD + 128:(hd + 1) * QK_PAD] = k_rot
        v_ref[:, hd * V_PAD:hd * V_PAD + V_DIM] = kva[:, hw + lo:hw + lo + 128].astype(BF16)
        v_ref[:, hd * V_PAD + V_DIM:(hd + 1) * V_PAD] = jnp.ones((x.shape[0], V_PAD - V_DIM), BF16)


def _pre_mla(x, mods, g0, wd, gq, gkv, wq, wkv, cos_t, sin_t, tm):
    b, n, d = x.shape
    nb_mod = mods.shape[0]
    full = lambda a: pl.BlockSpec(a.shape, lambda i, j: (0,) * a.ndim)
    mod_map = (lambda i, j: (i, 0, 0)) if nb_mod > 1 else (lambda i, j: (0, 0, 0))
    qk_w = MLA_HEADS * QK_PAD
    v_w = MLA_HEADS * V_PAD
    return pl.pallas_call(
        _pre_mla_kernel,
        out_shape=(jax.ShapeDtypeStruct((b, n, qk_w), BF16),
                   jax.ShapeDtypeStruct((b, n, qk_w), BF16),
                   jax.ShapeDtypeStruct((b, n, v_w), BF16)),
        grid=(b, n // tm),
        in_specs=[pl.BlockSpec((None, tm, d), lambda i, j: (i, j, 0)),
                  pl.BlockSpec((None, 1, mods.shape[-1]), mod_map),
                  full(g0), full(wd), full(gq), full(gkv), full(wq), full(wkv),
                  pl.BlockSpec((tm, 128), lambda i, j: (j, 0)),
                  pl.BlockSpec((tm, 128), lambda i, j: (j, 0))],
        out_specs=(pl.BlockSpec((None, tm, qk_w), lambda i, j: (i, j, 0)),
                   pl.BlockSpec((None, tm, qk_w), lambda i, j: (i, j, 0)),
                   pl.BlockSpec((None, tm, v_w), lambda i, j: (i, j, 0))),
        compiler_params=_cparams(("arbitrary", "arbitrary")),
        name="pre_mla",
    )(x, mods, g0, wd, gq, gkv, wq, wkv, cos_t, sin_t)


def _attn_kernel(*refs, n_seg):
    q_ref = refs[0]
    k_refs = refs[1:1 + n_seg]
    v_refs = refs[1 + n_seg:1 + 2 * n_seg]
    o_ref = refs[1 + 2 * n_seg]
    nt = (((1,), (1,)), ((), ()))

    def scores(hd):
        q = q_ref[:, hd * QK_PAD:(hd + 1) * QK_PAD]
        return [lax.dot_general(q, k[:, hd * QK_PAD:(hd + 1) * QK_PAD], nt, preferred_element_type=F32)
                for k in k_refs]

    nxt = scores(0)
    for hd in range(MLA_HEADS):
        ss = nxt
        if hd + 1 < MLA_HEADS:
            nxt = scores(hd + 1)
        m = ss[0].max(axis=-1, keepdims=True)
        for s in ss[1:]:
            m = jnp.maximum(m, s.max(axis=-1, keepdims=True))
        acc = None
        for s, v in zip(ss, v_refs):
            pv = _dot(jnp.exp2((s - m).astype(BF16)), v[:, hd * V_PAD:(hd + 1) * V_PAD])
            acc = pv if acc is None else acc + pv
        o_ref[:, hd * V_DIM:(hd + 1) * V_DIM] = (acc[:, :V_DIM] / acc[:, V_DIM:V_DIM + 1]).astype(BF16)


def _attention(q, ks, vs, tq):
    b, nq, qk_w = q.shape
    v_w = MLA_HEADS * V_DIM
    kv_spec = lambda a: pl.BlockSpec((None,) + a.shape[1:], lambda i, j: (i, 0, 0))
    return pl.pallas_call(
        functools.partial(_attn_kernel, n_seg=len(ks)),
        out_shape=jax.ShapeDtypeStruct((b, nq, v_w), BF16),
        grid=(b, nq // tq),
        in_specs=[pl.BlockSpec((None, tq, qk_w), lambda i, j: (i, j, 0))]
                 + [kv_spec(a) for a in ks] + [kv_spec(a) for a in vs],
        out_specs=pl.BlockSpec((None, tq, v_w), lambda i, j: (i, j, 0)),
        compiler_params=_cparams(("arbitrary", "arbitrary")),
        name="mla_attention",
    )(q, *ks, *vs)


SUB_ROWS = 256


def _sub_tiles(n):
    return [slice(r, r + SUB_ROWS) for r in range(0, n, SUB_ROWS)]


def _post_core(o, x, rows, mod_ref, g1_ref, g2_ref, wr_ref, x1_ref, fin_ref, lg_ref):
    d = x.shape[-1]
    ne = lg_ref.shape[0]
    gate = _rows(_mod_chunk(mod_ref, 2, d), x)
    shift = _rows(_mod_chunk(mod_ref, 3, d), x)
    scale = _rows(_mod_chunk(mod_ref, 4, d), x)
    x1 = x + gate * _rms(o, g1_ref[...])
    fin = _rms(x1, g2_ref[...]) * (1.0 + scale) + shift
    x1_ref[rows, :] = x1
    _pack_store(fin_ref, fin, row0=rows.start)
    nt = (((1,), (1,)), ((), ()))
    f_hi = fin.astype(BF16)
    f_lo = (fin - f_hi.astype(F32)).astype(BF16)
    r_hi = lax.dot_general(wr_ref[...], f_hi, nt, preferred_element_type=F32)
    r_lo = lax.dot_general(wr_ref[0:ne, :], f_lo, nt, preferred_element_type=F32)
    lg_ref[:, rows] = r_hi[:ne] + r_hi[ne:] + r_lo


def _post_proj_kernel(o_ref, wo_ref, x_ref, mod_ref, g1_ref, g2_ref, wr_ref, *rest):
    x1_ref, fin_ref, lg_ref = rest[-3:]
    for rows in _sub_tiles(x_ref.shape[0]):
        o = _dot(o_ref[rows, :], wo_ref[...])
        _post_core(o, x_ref[rows, :], rows, mod_ref, g1_ref, g2_ref, wr_ref, x1_ref, fin_ref, lg_ref)


def _post_glu_kernel(h_ref, yf_ref, yb_ref, dsk_ref, wg_ref, bg_ref, x_ref, mod_ref, g1_ref, g2_ref, wr_ref,
                     *rest):
    x1_ref, fin_ref, lg_ref = rest[-3:]
    d = x_ref.shape[-1]
    for rows in _sub_tiles(x_ref.shape[0]):
        wide = lambda r: jnp.concatenate([r[g, rows, :] for g in range(r.shape[0])], axis=-1)
        y = wide(h_ref) * dsk_ref[...] + wide(yf_ref) + wide(yb_ref)
        z = _dot(jax.nn.gelu(y, approximate=True).astype(BF16), wg_ref[...]) + bg_ref[...]
        o = z[:, :d] * jax.nn.sigmoid(z[:, d:])
        _post_core(o, x_ref[rows, :], rows, mod_ref, g1_ref, g2_ref, wr_ref, x1_ref, fin_ref, lg_ref)


def _post_mixer(kernel, tok_inputs, consts, x, n_tok, x_off, mods, g1, g2, wr_t, tm, rows_per_mod, name,
                moe_total=None, moe_off=0, prev=None):
    d = x.shape[-1]
    ne = wr_t.shape[0] // 2
    moe_total = n_tok if moe_total is None else moe_total
    tiles_per_mod = rows_per_mod // tm
    xo, mo = x_off // tm, moe_off // tm
    full = lambda a: pl.BlockSpec(a.shape, lambda i: (0,) * a.ndim)
    tile = pl.BlockSpec((tm, d), lambda i: (i, 0))
    mod_spec = pl.BlockSpec((None,) + mods.shape[1:], lambda i: (i // tiles_per_mod, 0, 0))
    in_specs = ([spec for _, spec in tok_inputs] + [full(a) for a in consts]
                + [pl.BlockSpec((tm, d), lambda i: (i + xo, 0)), mod_spec, full(g1), full(g2), full(wr_t)])
    args = [a for a, _ in tok_inputs] + list(consts) + [x, mods, g1, g2, wr_t]
    aliases = {}
    if prev is not None:
        aliases = {len(args): 1, len(args) + 1: 2}
        in_specs += [pl.BlockSpec(memory_space=pl.ANY)] * 2
        args += list(prev)
    return pl.pallas_call(
        kernel,
        out_shape=(jax.ShapeDtypeStruct((n_tok, d), F32),
                   jax.ShapeDtypeStruct((moe_total * PACK_ROWS, LANES), U32),
                   jax.ShapeDtypeStruct((ne, moe_total), F32)),
        grid=(n_tok // tm,),
        in_specs=in_specs,
        out_specs=(tile, pl.BlockSpec((tm * PACK_ROWS, LANES), lambda i: (i + mo, 0)),
                   pl.BlockSpec((ne, tm), lambda i: (0, i + mo))),
        input_output_aliases=aliases,
        compiler_params=_cparams(("arbitrary",)),
        name=name,
    )(*args)


def _route_kernel(lg_ref, bias_ref, eidx_ref, gate_ref, rank_ref, cnt_ref, tri_ref, base_ref):
    i = pl.program_id(0)
    ne, tt = lg_ref.shape
    gsz = ne // N_EXPERT_GROUPS
    shp = (N_EXPERT_GROUPS, gsz, tt)
    neg = -jnp.inf

    @pl.when(i == 0)
    def _():
        base_ref[...] = jnp.zeros_like(base_ref)
        r = lax.broadcasted_iota(jnp.int32, (tt, tt), 0)
        c = lax.broadcasted_iota(jnp.int32, (tt, tt), 1)
        tri_ref[...] = (r < c).astype(BF16)

    scores = jax.nn.sigmoid(lg_ref[...])
    s3 = scores.reshape(shp)
    b3 = (scores + bias_ref[...]).reshape(shp)
    io_e = lax.broadcasted_iota(jnp.int32, shp, 1)
    io_g = lax.broadcasted_iota(jnp.int32, shp, 0)
    io_flat = io_g * gsz + io_e
    m1 = b3.max(axis=1, keepdims=True)
    i1 = jnp.where(b3 == m1, io_e, gsz).min(axis=1, keepdims=True)
    m2 = jnp.where(io_e == i1, neg, b3).max(axis=1, keepdims=True)
    cur = jnp.broadcast_to(m1 + m2, shp)
    gsel = jnp.zeros(shp, jnp.bool_)
    for _ in range(TOPK_GROUPS):
        m = cur.max(axis=0, keepdims=True)
        gi = jnp.where(cur == m, io_g, N_EXPERT_GROUPS).min(axis=0, keepdims=True)
        hit = io_g == gi
        gsel = jnp.logical_or(gsel, hit)
        cur = jnp.where(hit, neg, cur)
    cand = jnp.where(gsel, b3, neg)
    sel = jnp.zeros(shp, jnp.bool_)
    eids, gts = [], []
    for _ in range(TOP_K):
        m = cand.max(axis=0, keepdims=True).max(axis=1, keepdims=True)
        ei = jnp.where(cand == m, io_flat, ne).min(axis=0, keepdims=True).min(axis=1, keepdims=True)
        hit = io_flat == ei
        gts.append(jnp.where(hit, s3, 0.0).sum(axis=0, keepdims=True).sum(axis=1, keepdims=True))
        eids.append(ei)
        sel = jnp.logical_or(sel, hit)
        cand = jnp.where(hit, neg, cand)
    gsum = gts[0]
    for g in gts[1:]:
        gsum = gsum + g
    self32 = sel.astype(F32).reshape(ne, tt)
    cnt = _dot(self32.astype(BF16), tri_ref[...]) + base_ref[...]
    cnt3 = cnt.reshape(shp)
    for k in range(TOP_K):
        hit = io_flat == eids[k]
        rk = jnp.where(hit, cnt3, 0.0).sum(axis=0, keepdims=True).sum(axis=1, keepdims=True)
        rank_ref[k:k + 1, :] = rk.reshape(1, tt).astype(jnp.int32)
        eidx_ref[k:k + 1, :] = eids[k].reshape(1, tt)
        gate_ref[k:k + 1, :] = (gts[k] / gsum * ROUTED_SCALE).reshape(1, tt)
    base_new = base_ref[...] + self32.sum(axis=1, keepdims=True)
    base_ref[...] = base_new
    cnt_ref[...] = jnp.broadcast_to(base_new, cnt_ref.shape)


def _route(logits_t, bias, tt):
    ne, t = logits_t.shape
    out_i = jax.ShapeDtypeStruct((TOP_K, t), jnp.int32)
    row = pl.BlockSpec((TOP_K, tt), lambda i: (0, i))
    return pl.pallas_call(
        _route_kernel,
        out_shape=(out_i, jax.ShapeDtypeStruct((TOP_K, t), F32), out_i,
                   jax.ShapeDtypeStruct((ne, 128), F32)),
        grid=(t // tt,),
        in_specs=[pl.BlockSpec((ne, tt), lambda i: (0, i)),
                  pl.BlockSpec((ne, 1), lambda i: (0, 0))],
        out_specs=(row, row, row, pl.BlockSpec((ne, 128), lambda i: (0, 0))),
        scratch_shapes=[pltpu.VMEM((tt, tt), BF16), pltpu.VMEM((ne, 1), F32)],
        compiler_params=_cparams(("arbitrary",)),
        name="moe_route",
    )(logits_t, bias.reshape(ne, 1))


def _dest_kernel(eidx_ref, rank_ref, start_ref, dest_ref):
    kk, tt = eidx_ref.shape
    ne = start_ref.shape[0]
    n_chunk, _, r = dest_ref.shape
    io_e = lax.broadcasted_iota(jnp.int32, (ne, tt), 0)
    start = start_ref[...]
    for k in range(kk):
        hit = io_e == eidx_ref[k:k + 1, :]
        dk = jnp.where(hit, start, 0).sum(axis=0, keepdims=True) + rank_ref[k:k + 1, :]
        for c in range(n_chunk):
            dest_ref[c, k:k + 1, :] = dk[:, c * r:(c + 1) * r]


def _dest_rows(eidx_t, rank_t, start, tt, r):
    kk, t = eidx_t.shape
    ne = start.shape[0]
    return pl.pallas_call(
        _dest_kernel,
        out_shape=jax.ShapeDtypeStruct((t // r, kk, r), jnp.int32),
        grid=(t // tt,),
        in_specs=[pl.BlockSpec((kk, tt), lambda i: (0, i)),
                  pl.BlockSpec((kk, tt), lambda i: (0, i)),
                  pl.BlockSpec((ne, 1), lambda i: (0, 0))],
        out_specs=pl.BlockSpec((tt // r, kk, r), lambda i: (i, 0, 0)),
        compiler_params=_cparams(("arbitrary",)),
        name="moe_dest",
    )(eidx_t, rank_t, start.reshape(ne, 1))


SC_CHUNK = 64


def _sc_mesh():
    return plsc.VectorSubcoreMesh(core_axis_name="c", subcore_axis_name="s")


def _sc_workers():
    info = plsc.get_sparse_core_info()
    return info.num_cores, info.num_cores * info.num_subcores


def _sc_scatter_rows(rows, dest, n_out):
    t = rows.shape[0]
    n_chunk, kk, r = dest.shape
    nc, nw = _sc_workers()
    cpw = n_chunk // nw
    assert cpw * nw == n_chunk and cpw % 2 == 0 and n_chunk * r == t

    @functools.partial(
        pl.kernel, mesh=_sc_mesh(),
        out_type=jax.ShapeDtypeStruct((n_out,) + rows.shape[1:], rows.dtype),
        scratch_types=[pltpu.VMEM((2, kk, r), jnp.int32), pltpu.VMEM((2, r) + rows.shape[1:], rows.dtype),
                       pltpu.SemaphoreType.DMA((2,)), pltpu.SemaphoreType.DMA((2,))])
    def scatter(rows_hbm, dest_hbm, out_hbm, idx_v, rows_v, load_sem, scat_sem):
        c0 = (lax.axis_index("s") * nc + lax.axis_index("c")) * cpw

        def loads(c, b):
            return (pltpu.make_async_copy(dest_hbm.at[c], idx_v.at[b], load_sem.at[b]),
                    pltpu.make_async_copy(rows_hbm.at[pl.ds(c * r, r)], rows_v.at[b], load_sem.at[b]))

        def scat(b, k):
            return pltpu.make_async_copy(rows_v.at[b], out_hbm.at[idx_v.at[b, k]], scat_sem.at[b])

        for cp in loads(c0, 0):
            cp.start()

        @pl.loop(0, cpw, step=2)
        def _(ci):
            for b in range(2):
                c = c0 + ci + b
                for cp in loads(c, b):
                    cp.wait()
                for k in range(kk):
                    scat(b, k).start()

                @pl.when(ci + b >= 1)
                def _():
                    for k in range(kk):
                        scat(1 - b, k).wait()

                @pl.when(ci + b + 1 < cpw)
                def _():
                    for cp in loads(c + 1, 1 - b):
                        cp.start()

        for k in range(kk):
            scat((cpw - 1) % 2, k).wait()

    return scatter(rows, dest)


def _sc_gather_rows(src, dest):
    n_chunk, kk, r = dest.shape
    t = n_chunk * r
    nc, nw = _sc_workers()
    cpw = n_chunk // nw
    nbuf = 3
    assert cpw * nw == n_chunk and kk > nbuf

    @functools.partial(
        pl.kernel, mesh=_sc_mesh(),
        out_type=jax.ShapeDtypeStruct((kk, t) + src.shape[1:], src.dtype),
        scratch_types=[pltpu.VMEM((kk, r), jnp.int32), pltpu.VMEM((nbuf, r) + src.shape[1:], src.dtype),
                       pltpu.SemaphoreType.DMA((nbuf,)), pltpu.SemaphoreType.DMA((nbuf,))])
    def gather(src_hbm, dest_hbm, out_hbm, idx_v, rows_v, get_sem, put_sem):
        c0 = (lax.axis_index("s") * nc + lax.axis_index("c")) * cpw

        @pl.loop(0, cpw)
        def _(ci):
            c = c0 + ci
            pltpu.sync_copy(dest_hbm.at[c], idx_v)

            def get(k):
                return pltpu.make_async_copy(src_hbm.at[idx_v.at[k]], rows_v.at[k % nbuf], get_sem.at[k % nbuf])

            def put(k):
                return pltpu.make_async_copy(rows_v.at[k % nbuf], out_hbm.at[k, pl.ds(c * r, r)],
                                             put_sem.at[k % nbuf])

            for k in range(nbuf - 1):
                get(k).start()
            for k in range(kk):
                get(k).wait()
                put(k).start()
                if k + nbuf - 1 < kk:
                    if k >= 1:
                        put(k - 1).wait()
                    get(k + nbuf - 1).start()
            for k in range(kk - nbuf, kk):
                put(k).wait()

    return gather(src, dest)


def _expert_kernel(be_ref, nu_ref, x_ref, wg_ref, wu_ref, wd_ref, o_ref, wgu_s, wd_s):
    i = pl.program_id(0)
    tb = o_ref.shape[0] // PACK_ROWS

    @pl.when(i < nu_ref[0])
    def _():
        @pl.when(jnp.logical_or(i == 0, be_ref[i] != be_ref[jnp.maximum(i - 1, 0)]))
        def _():
            wgu_s[:, :D_EXPERT] = wg_ref[...].astype(BF16)
            wgu_s[:, D_EXPERT:] = wu_ref[...].astype(BF16)
            wd_s[...] = wd_ref[...].astype(BF16)

        x = jnp.concatenate([v.astype(BF16) for v in _unpack_load(x_ref, tb)], axis=-1)
        gu = _dot(x, wgu_s[...])
        g = gu[:, :D_EXPERT]
        h = g * jax.nn.sigmoid(g) * gu[:, D_EXPERT:]
        _pack_store(o_ref, _dot(h.astype(BF16), wd_s[...]))


def _experts(xs, blk_e, n_used, w_gate, w_up, w_down, layer, tb):
    rows = xs.shape[0] // PACK_ROWS
    _, ne, d, de = w_gate.shape
    nb = rows // tb
    row_map = lambda i, be, nu: (jnp.minimum(i, nu[0] - 1), 0)
    w_map = lambda i, be, nu: (layer, be[i], 0, 0)
    grid_spec = pltpu.PrefetchScalarGridSpec(
        num_scalar_prefetch=2,
        grid=(nb,),
        in_specs=[pl.BlockSpec((tb * PACK_ROWS, LANES), row_map),
                  pl.BlockSpec((None, None, d, de), w_map),
                  pl.BlockSpec((None, None, d, de), w_map),
                  pl.BlockSpec((None, None, de, d), w_map)],
        out_specs=pl.BlockSpec((tb * PACK_ROWS, LANES), row_map),
        scratch_shapes=[pltpu.VMEM((d, 2 * de), BF16), pltpu.VMEM((de, d), BF16)],
    )
    return pl.pallas_call(
        _expert_kernel,
        out_shape=jax.ShapeDtypeStruct(xs.shape, U32),
        grid_spec=grid_spec,
        compiler_params=_cparams(("arbitrary",)),
        name="moe_experts",
    )(blk_e, n_used, xs, w_gate, w_up, w_down)


def _combine_kernel(yk_ref, gate_ref, fin_ref, shgu_ref, shd_ref, x1_ref, mod_ref, g3_ref, o_ref):
    tm, d = x1_ref.shape
    for rows in _sub_tiles(tm):
        n, r0 = SUB_ROWS, rows.start
        gates = gate_ref[rows, :]
        blocks = None
        for k in range(TOP_K):
            gk = gates[:, k:k + 1]
            terms = [gk * v for v in _unpack_load(yk_ref, n, lead=(k,), row0=r0)]
            blocks = terms if blocks is None else [a + b for a, b in zip(blocks, terms)]
        fin = jnp.concatenate([v.astype(BF16) for v in _unpack_load(fin_ref, n, row0=r0)], axis=-1)
        gu = _dot(fin, shgu_ref[...])
        g = gu[:, :D_EXPERT]
        hsh = g * jax.nn.sigmoid(g) * gu[:, D_EXPERT:]
        f = jnp.concatenate(blocks, axis=-1) + _dot(hsh.astype(BF16), shd_ref[...])
        x1 = x1_ref[rows, :]
        x2 = x1 + _rows(_mod_chunk(mod_ref, 5, d), x1) * _rms(f, g3_ref[...])
        if len(o_ref.shape) == 2:
            o_ref[rows, :] = x2
        else:
            nb = o_ref.shape[0]
            ts = SUB_ROWS // nb
            o_ref[:, r0 // nb:r0 // nb + ts, :] = jnp.swapaxes(x2.reshape(ts, nb, d), 0, 1)


def _combine(yk, gates, fin, shgu, shd, x1, mods, g3, tm, rows_per_mod, tok_off, batch_out=0):
    t, d = x1.shape
    off = tok_off // tm
    tiles_per_mod = rows_per_mod // tm
    full = lambda a: pl.BlockSpec(a.shape, lambda i: (0,) * a.ndim)
    if batch_out:
        out_shape = jax.ShapeDtypeStruct((batch_out, t // batch_out, d), F32)
        out_spec = pl.BlockSpec((batch_out, tm // batch_out, d), lambda i: (0, i, 0))
    else:
        out_shape = jax.ShapeDtypeStruct((t, d), F32)
        out_spec = pl.BlockSpec((tm, d), lambda i: (i, 0))
    return pl.pallas_call(
        _combine_kernel,
        out_shape=out_shape,
        grid=(t // tm,),
        in_specs=[pl.BlockSpec((TOP_K, tm * PACK_ROWS, LANES), lambda i: (0, i + off, 0)),
                  pl.BlockSpec((tm, TOP_K), lambda i: (i + off, 0)),
                  pl.BlockSpec((tm * PACK_ROWS, LANES), lambda i: (i + off, 0)),
                  full(shgu), full(shd),
                  pl.BlockSpec((tm, d), lambda i: (i, 0)),
                  pl.BlockSpec((None,) + mods.shape[1:], lambda i: (i // tiles_per_mod, 0, 0)),
                  full(g3)],
        out_specs=out_spec,
        compiler_params=_cparams(("arbitrary",)),
        name="moe_combine",
    )(yk, gates, fin, shgu, shd, x1, mods, g3)


def _moe(fin, logits_t, bias, w_gate, w_up, w_down, layer, tb):
    t = fin.shape[0] // PACK_ROWS
    ne = w_gate.shape[1]
    tt = 512
    eidx_t, gates_t, rank_t, cnt = _route(logits_t, bias, tt)
    counts = cnt[:, 0].astype(jnp.int32)
    padded = (counts + tb - 1) // tb * tb
    pad_end = jnp.cumsum(padded)
    pad_start = pad_end - padded
    nb = (t * TOP_K) // tb + ne
    n_used = pad_end[-1] // tb
    blk_start = jnp.arange(nb, dtype=jnp.int32) * tb
    blk = jnp.sum(pad_end[None, :] <= jnp.minimum(blk_start, pad_end[-1] - 1)[:, None], axis=1)
    blk_e = jnp.minimum(blk, ne - 1).astype(jnp.int32)
    dest = _dest_rows(eidx_t, rank_t, pad_start, tt, SC_CHUNK)
    xs = _sc_scatter_rows(fin.reshape(t, PACK_ROWS, LANES), dest, nb * tb)
    ys = _experts(xs.reshape(nb * tb * PACK_ROWS, LANES), blk_e, n_used.reshape(1).astype(jnp.int32),
                  w_gate, w_up, w_down, layer, tb)
    yk = _sc_gather_rows(ys.reshape(nb * tb, PACK_ROWS, LANES), dest)
    return yk.reshape(TOP_K, t * PACK_ROWS, LANES), gates_t.T


def _pre_s5_kernel(x_ref, mod_ref, g0_ref, *refs):
    h_ref, xt_ref = refs[-2:]
    nb, tt, d = x_ref.shape
    x = jnp.swapaxes(x_ref[...], 0, 1).reshape(tt * nb, d)
    h = (_rms(x, g0_ref[...]) * (1.0 + _rows(_mod_chunk(mod_ref, 1, d), x))
         + _rows(_mod_chunk(mod_ref, 0, d), x))
    for g in range(h_ref.shape[0]):
        h_ref[g] = h[:, g * LANES:(g + 1) * LANES]
    xt_ref[...] = x


def _pre_s5(x, mods, g0, n_total, t_off, prev, tt):
    nb, n, d = x.shape
    off = t_off // tt
    out_shape = (jax.ShapeDtypeStruct((d // LANES, n_total * nb, LANES), F32),
                 jax.ShapeDtypeStruct((n_total * nb, d), F32))
    out_specs = (pl.BlockSpec((d // LANES, tt * nb, LANES), lambda i: (0, i + off, 0)),
                 pl.BlockSpec((tt * nb, d), lambda i: (i + off, 0)))
    in_specs = [pl.BlockSpec((nb, tt, d), lambda i: (0, i, 0)),
                pl.BlockSpec(mods.shape, lambda i: (0, 0)),
                pl.BlockSpec(g0.shape, lambda i: (0, 0))]
    args = (x, mods, g0)
    aliases = {}
    if prev is not None:
        in_specs += [pl.BlockSpec(memory_space=pl.ANY)] * 2
        args += tuple(prev)
        aliases = {3: 0, 4: 1}
    return pl.pallas_call(
        _pre_s5_kernel,
        out_shape=out_shape,
        grid=(n // tt,),
        in_specs=in_specs,
        out_specs=out_specs,
        input_output_aliases=aliases,
        compiler_params=_cparams(("arbitrary",)),
        name="pre_s5",
    )(*args)


def _s5_scan_kernel(h_ref, bm_ref, cm_ref, lam_ref, y_ref, bu0, bu1, xb0, xb1, st_ref):
    first = jnp.logical_and(jnp.logical_and(pl.program_id(0) == 0, pl.program_id(1) == 0), pl.program_id(2) == 0)
    dr = pl.program_id(1)
    s = pl.program_id(2)
    tc, nb, cw = h_ref.shape
    half = st_ref.shape[1] // 2

    @pl.when(first)
    def _():
        for r in (bu0, bu1, xb0, xb1, st_ref):
            r[...] = jnp.zeros_like(r)

    def stages(bu_w, bu_r, xb_w, xb_r):
        lr = jnp.broadcast_to(lam_ref[0:1, :], (nb, half))
        li = jnp.broadcast_to(lam_ref[1:2, :], (nb, half))
        fresh = s == 1
        xr = jnp.where(fresh, 0.0, st_ref[:, 0:half])
        xi = jnp.where(fresh, 0.0, st_ref[:, half:2 * half])
        tp = tc // S5_PIECES
        for p in range(S5_PIECES):
            for i in range(p * tp, (p + 1) * tp):
                t = i + dr * (tc - 1 - 2 * i)
                rows = pl.ds(pl.multiple_of(t * nb, nb), nb)
                nr = lr * xr - li * xi + bu_r[rows, 0:half]
                ni = lr * xi + li * xr + bu_r[rows, half:2 * half]
                xb_w[rows, 0:half] = nr.astype(BF16)
                xb_w[rows, half:2 * half] = ni.astype(BF16)
                xr, xi = nr, ni
            ts = slice(p * tp, (p + 1) * tp)
            mr = slice(p * tp * nb, (p + 1) * tp * nb)
            y_ref[ts] = _dot(xb_r[mr, :], cm_ref[...]).reshape(tp, nb, cw)
            bu_w[mr, :] = _dot(h_ref[ts].reshape(tp * nb, cw).astype(BF16), bm_ref[...])
        st_ref[:, 0:half] = xr
        st_ref[:, half:2 * half] = xi

    @pl.when(s % 2 == 0)
    def _():
        stages(bu0, bu1, xb1, xb0)

    @pl.when(s % 2 == 1)
    def _():
        stages(bu1, bu0, xb0, xb1)


def _s5_scan(h_all, n_ctx, bm, cm, lam, tc):
    ng, nt, nb, cw = h_all.shape
    nl = nt - n_ctx
    assert cw == S5_GROUPS_PER_BLOCK * S5_GROUP
    sw = 2 * S5_GROUPS_PER_BLOCK * S5_STATE
    ncc, n = n_ctx // tc, nt // tc

    def chunk(dr, j):
        j = jnp.clip(j, 0, n - 1)
        rev = jnp.where(j < ncc, ncc - 1 - j, n - 1 - (j - ncc))
        return jnp.where(dr == 0, j, rev)

    def out_map(g, dr, s):
        return (dr, g, chunk(dr, jnp.clip(s - 2, ncc, n - 1)) - ncc, 0, 0)

    return pl.pallas_call(
        _s5_scan_kernel,
        out_shape=jax.ShapeDtypeStruct((2, ng, nl, nb, cw), F32),
        grid=(ng, 2, n + 2),
        in_specs=[pl.BlockSpec((None, tc, nb, cw), lambda g, dr, s: (g, chunk(dr, s), 0, 0)),
                  pl.BlockSpec((None, None, cw, sw), lambda g, dr, s: (dr, g, 0, 0)),
                  pl.BlockSpec((None, None, sw, cw), lambda g, dr, s: (dr, g, 0, 0)),
                  pl.BlockSpec((None, None, 2, sw // 2), lambda g, dr, s: (dr, g, 0, 0))],
        out_specs=pl.BlockSpec((None, None, tc, nb, cw), out_map),
        scratch_shapes=[pltpu.VMEM((tc * nb, sw), F32), pltpu.VMEM((tc * nb, sw), F32),
                        pltpu.VMEM((tc * nb, sw), BF16), pltpu.VMEM((tc * nb, sw), BF16),
                        pltpu.VMEM((nb, sw), F32)],
        compiler_params=_cparams(("arbitrary", "arbitrary", "arbitrary")),
        name="s5_scan",
    )(h_all, bm, cm, lam)


def _s5_params(lam_re, lam_im, log_step, b_re, b_im, c_re, c_im):
    g, p = lam_re.shape[1:]
    gb = S5_GROUPS_PER_BLOCK
    nblk = g // gb
    step = jnp.exp(log_step)[..., None]
    mag = jnp.exp(lam_re * step)
    lb_re = mag * jnp.cos(lam_im * step)
    lb_im = mag * jnp.sin(lam_im * step)
    den = lam_re * lam_re + lam_im * lam_im
    f_re = ((lb_re - 1.0) * lam_re + lb_im * lam_im) / den
    f_im = (lb_im * lam_re - (lb_re - 1.0) * lam_im) / den
    bb_re = f_re[..., None] * b_re - f_im[..., None] * b_im
    bb_im = f_re[..., None] * b_im + f_im[..., None] * b_re
    eye = jnp.eye(gb, dtype=F32)

    def in_map(w):
        w = w.reshape(2, nblk, gb, p, S5_GROUP)
        return jnp.einsum("dnapi,ab->dnaibp", w, eye).reshape(2, nblk, gb * S5_GROUP, gb * p)

    def out_map(w):
        w = w.reshape(2, nblk, gb, S5_GROUP, p)
        return jnp.einsum("dnaip,ab->dnapbi", w, eye).reshape(2, nblk, gb * p, gb * S5_GROUP)

    bm = jnp.concatenate([in_map(bb_re), in_map(bb_im)], axis=-1).astype(BF16)
    cm = jnp.concatenate([out_map(c_re), out_map(-c_im)], axis=-2).astype(BF16)
    lam = jnp.stack([lb_re.reshape(2, nblk, gb * p), lb_im.reshape(2, nblk, gb * p)], axis=2)
    return bm, cm, lam


def _rope_tables(n_tokens):
    rows = n_tokens // GRID_W
    row = jnp.repeat(jnp.arange(rows), GRID_W).astype(F32)
    col = jnp.tile(jnp.arange(GRID_W), rows).astype(F32)
    n_freq = ROPE_DIM // 4
    inv_freq = ROPE_BASE ** (-jnp.arange(n_freq, dtype=F32) / n_freq)
    ang = jnp.concatenate([row[:, None] * inv_freq, col[:, None] * inv_freq], axis=-1)
    cos, sin = jnp.cos(ang), jnp.sin(ang)
    z = jnp.zeros((n_tokens, 128 - ROPE_DIM), F32)
    return (jnp.concatenate([cos, cos, z], axis=-1), jnp.concatenate([-sin, sin, z], axis=-1))


def _router_halves(w_router):
    wt = w_router.T
    hi = wt.astype(BF16)
    lo = (wt - hi.astype(F32)).astype(BF16)
    return jnp.concatenate([hi, lo], axis=0)


def _split_pairs(w):
    ev, od = w[..., 0::2], w[..., 1::2]
    z = jnp.zeros(w.shape[:-1] + (128 - ROPE_DIM,), w.dtype)
    return jnp.concatenate([ev, od, z], axis=-1), jnp.concatenate([od, ev, z], axis=-1)


def _mla_weights(w_dqkv, w_uq, w_ukv):
    kp, kps = _split_pairs(w_dqkv[:, Q_LORA + KV_LORA:])
    wd = jnp.concatenate([w_dqkv[:, :Q_LORA + KV_LORA], kp, kps], axis=-1).astype(BF16)
    wq3 = w_uq.reshape(Q_LORA, MLA_HEADS, NOPE_DIM + ROPE_DIM)
    qp, qps = _split_pairs(wq3[:, :, NOPE_DIM:])
    wq = jnp.concatenate([wq3[:, :, :NOPE_DIM].reshape(Q_LORA, -1), qp.reshape(Q_LORA, -1),
                          qps.reshape(Q_LORA, -1)], axis=-1).astype(BF16)
    wkv3 = w_ukv.reshape(KV_LORA, MLA_HEADS, NOPE_DIM + V_DIM)
    wkv = jnp.concatenate([wkv3[:, :, :NOPE_DIM].reshape(KV_LORA, -1),
                           wkv3[:, :, NOPE_DIM:].reshape(KV_LORA, -1)], axis=-1).astype(BF16)
    return wd, wq, wkv


@jax.jit
def kernel(x, c, ctx, c_ctx, ada_w, ada_b, norm_g, mla_w_dqkv, mla_g_q, mla_g_kv, mla_w_uq, mla_w_ukv, mla_w_o, s5_lam_re, s5_lam_im, s5_log_step, s5_b_re, s5_b_im, s5_c_re, s5_c_im, s5_d, s5_w_glu, s5_b_glu, moe_w_router, moe_bias, moe_w_gate, moe_w_up, moe_w_down, sh_w_gate, sh_w_up, sh_w_down):
    b, l, d = x.shape
    n_ctx = ctx.shape[1]
    assert ada_w.shape[0] == 2 and b % 8 == 0
    ta = 256
    tm = 512
    tb = 512
    row = lambda v: v.reshape(1, -1)

    n_rows = (b + 1 + 7) // 8 * 8
    cvec = jnp.zeros((n_rows, d), F32).at[:b].set(c).at[b].set(c_ctx)
    mods = _ada_mods(cvec, ada_w, ada_b)

    def shared_weights(i):
        shgu = jnp.concatenate([sh_w_gate[i], sh_w_up[i]], axis=-1).astype(BF16)
        return shgu, sh_w_down[i].astype(BF16)

    mod_lat = mods[0, :b].reshape(b, 1, N_MOD * d)
    mod_ctx = mods[0, b].reshape(1, 1, N_MOD * d)
    wd, wq, wkv = _mla_weights(mla_w_dqkv[0], mla_w_uq[0], mla_w_ukv[0])
    cos_l, sin_l = _rope_tables(l)
    cos_c = jnp.concatenate([jnp.ones((n_ctx, ROPE_DIM), F32), jnp.zeros((n_ctx, 128 - ROPE_DIM), F32)], -1)
    sin_c = jnp.zeros((n_ctx, 128), F32)
    pre = functools.partial(_pre_mla, g0=row(norm_g[0, 0]), wd=wd, gq=row(mla_g_q[0]), gkv=row(mla_g_kv[0]),
                            wq=wq, wkv=wkv, tm=ta)
    q_c, k_c, v_c = pre(ctx, mod_ctx, cos_t=cos_c, sin_t=sin_c)
    q_l, k_l, v_l = pre(x, mod_lat, cos_t=cos_l, sin_t=sin_l)
    o_l = _attention(q_l, [k_c, k_l], [v_c, v_l], ta)
    o_c = _attention(q_c, [k_c], [v_c], n_ctx)

    wo = mla_w_o[0].astype(BF16)
    wr_t = _router_halves(moe_w_router[0])
    g1, g2, g3 = row(norm_g[0, 1]), row(norm_g[0, 2]), row(norm_g[0, 3])
    post = functools.partial(_post_mixer, _post_proj_kernel, consts=[wo], g1=g1, g2=g2, wr_t=wr_t, tm=tm,
                             name="post_mla")
    o_spec = pl.BlockSpec((tm, o_l.shape[-1]), lambda i: (i, 0))
    n_moe = b * (n_ctx + l)
    x1_c, fin, lg = post([(o_c.reshape(b * n_ctx, -1), o_spec)], x=ctx.reshape(b * n_ctx, d),
                         n_tok=b * n_ctx, x_off=0, mods=mod_ctx, rows_per_mod=b * n_ctx, moe_total=n_moe)
    x1_l, fin, lg = post([(o_l.reshape(b * l, -1), o_spec)], x=x.reshape(b * l, d), n_tok=b * l, x_off=0,
                         mods=mod_lat, rows_per_mod=l, moe_total=n_moe, moe_off=b * n_ctx, prev=(fin, lg))
    shgu, shd = shared_weights(0)
    yk, gates = _moe(fin, lg, moe_bias[0], moe_w_gate, moe_w_up, moe_w_down, 0, tb)
    comb = functools.partial(_combine, yk, gates, fin, shgu, shd, g3=g3, tm=tm)
    x2_c = comb(x1=x1_c, mods=mod_ctx, rows_per_mod=b * n_ctx, tok_off=0)
    x2_l = comb(x1=x1_l, mods=mod_lat, rows_per_mod=l, tok_off=b * n_ctx)

    n_all = n_ctx + l
    mod_lat = mods[1, :b]
    mod_ctx = jnp.broadcast_to(mods[1, b][None], (b, N_MOD * d))
    g0 = row(norm_g[1, 0])
    tt = tm // b
    h_xt = _pre_s5(x2_c.reshape(b, n_ctx, d), mod_ctx, g0, n_all, 0, None, tt)
    h, xt = _pre_s5(x2_l.reshape(b, l, d), mod_lat, g0, n_all, n_ctx, h_xt, tt)
    bm, cm, lam = _s5_params(s5_lam_re[0], s5_lam_im[0], s5_log_step[0], s5_b_re[0], s5_b_im[0],
                             s5_c_re[0], s5_c_im[0])
    ng = d // LANES
    y = _s5_scan(h.reshape(ng, n_all, b, LANES), n_ctx, bm, cm, lam, 64).reshape(2, ng, l * b, LANES)
    g1, g2, g3 = row(norm_g[1, 1]), row(norm_g[1, 2]), row(norm_g[1, 3])
    lat0 = n_ctx * b // tm
    x1, fin, lg = _post_mixer(
        _post_glu_kernel,
        [(h, pl.BlockSpec((ng, tm, LANES), lambda i: (0, i + lat0, 0))),
         (y, pl.BlockSpec((None, ng, tm, LANES), lambda i: (0, 0, i, 0))),
         (y, pl.BlockSpec((None, ng, tm, LANES), lambda i: (1, 0, i, 0)))],
        [row(s5_d[0]), s5_w_glu[0].astype(BF16), row(s5_b_glu[0])],
        x=xt, n_tok=l * b, x_off=n_ctx * b, mods=mod_lat[None], g1=g1, g2=g2, wr_t=_router_halves(moe_w_router[1]), tm=tm,
        rows_per_mod=l * b, name="post_s5")
    shgu, shd = shared_weights(1)
    yk, gates = _moe(fin, lg, moe_bias[1], moe_w_gate, moe_w_up, moe_w_down, 1, tb)
    return _combine(yk, gates, fin, shgu, shd, x1, mod_lat[None], g3, tm, rows_per_mod=l * b, tok_off=0,
                    batch_out=b)
```

```python
import functools

import jax
import jax.numpy as jnp
from jax import lax
from jax.experimental import pallas as pl
from jax.experimental.pallas import tpu as pltpu
from jax.experimental.pallas import tpu_sc as plsc

F32 = jnp.float32
BF16 = jnp.bfloat16
U32 = jnp.uint32

N_MOD = 6
NORM_EPS = 1e-6
LOG2_E = 1.4426950408889634
GRID_W = 64
MLA_HEADS = 8
Q_LORA = 384
KV_LORA = 256
NOPE_DIM = 128
ROPE_DIM = 64
V_DIM = 128
V_PAD = 256
ROPE_BASE = 10000.0
QK_PAD = 256
S5_GROUP = 16
S5_STATE = 64
S5_GROUPS_PER_BLOCK = 8
S5_PIECES = 4
N_EXPERTS = 64
TOP_K = 8
N_EXPERT_GROUPS = 8
TOPK_GROUPS = 4
D_EXPERT = 256
ROUTED_SCALE = 2.5

VMEM_LIMIT = 56 * 1024 * 1024


def _cparams(sem):
    return pltpu.CompilerParams(dimension_semantics=sem, vmem_limit_bytes=VMEM_LIMIT)


def _rms(x, g):
    return x * lax.rsqrt(jnp.mean(x * x, axis=-1, keepdims=True) + NORM_EPS) * g


def _rows(v, like):
    r = v.shape[0]
    if r == 1:
        return v
    tm, d = like.shape
    return jnp.broadcast_to(v[None], (tm // r, r, d)).reshape(tm, d)


def _mod_chunk(mod_ref, j, d):
    return mod_ref[:, j * d:(j + 1) * d]


def _dot(a, b):
    return jnp.dot(a, b, preferred_element_type=F32)


PACK_ROWS = 4
LANES = 128


def _pack_store(ref, val, lead=(), row0=0):
    n = val.shape[0]
    bits = lax.bitcast_convert_type(val.astype(BF16).astype(F32), U32)
    for s in range(PACK_ROWS):
        lo = bits[:, s * LANES:(s + 1) * LANES] >> 16
        hi = bits[:, (s + PACK_ROWS) * LANES:(s + PACK_ROWS + 1) * LANES] & jnp.uint32(0xFFFF0000)
        ref[lead + (pl.ds(row0 * PACK_ROWS + s, n, stride=PACK_ROWS), slice(None))] = lo | hi


def _unpack_load(ref, n, lead=(), row0=0):
    los, his = [], []
    for s in range(PACK_ROWS):
        w = ref[lead + (pl.ds(row0 * PACK_ROWS + s, n, stride=PACK_ROWS), slice(None))]
        los.append(lax.bitcast_convert_type(w << 16, F32))
        his.append(lax.bitcast_convert_type(w & jnp.uint32(0xFFFF0000), F32))
    return los + his


def _ada_kernel(c_ref, w_ref, b_ref, o_ref):
    c = c_ref[...]
    s = c * jax.nn.sigmoid(c)
    o_ref[...] = jnp.dot(s, w_ref[...], preferred_element_type=F32,
                         precision=lax.Precision.HIGHEST) + b_ref[...]


def _ada_mods(cvec, ada_w, ada_b):
    depth, d, n = ada_w.shape
    rows = cvec.shape[0]
    tn = 1536
    return pl.pallas_call(
        _ada_kernel,
        out_shape=jax.ShapeDtypeStruct((depth, rows, n), F32),
        grid=(depth, n // tn),
        in_specs=[pl.BlockSpec((rows, d), lambda l, j: (0, 0)),
                  pl.BlockSpec((None, d, tn), lambda l, j: (l, 0, j)),
                  pl.BlockSpec((None, 1, tn), lambda l, j: (l, 0, j))],
        out_specs=pl.BlockSpec((None, rows, tn), lambda l, j: (l, 0, j)),
        compiler_params=_cparams(("arbitrary", "arbitrary")),
        name="ada_mods",
    )(cvec, ada_w, ada_b.reshape(depth, 1, n))


def _pre_mla_kernel(x_ref, mod_ref, g0_ref, wd_ref, gq_ref, gkv_ref, wq_ref, wkv_ref, cos_ref, sin_ref,
                    q_ref, k_ref, v_ref):
    d = x_ref.shape[-1]
    x = x_ref[...]
    h = _rms(x, g0_ref[...]) * (1.0 + _mod_chunk(mod_ref, 1, d)) + _mod_chunk(mod_ref, 0, d)
    a = _dot(h.astype(BF16), wd_ref[...])
    cq = _rms(a[:, :Q_LORA], gq_ref[...])
    ckv = _rms(a[:, Q_LORA:Q_LORA + KV_LORA], gkv_ref[...])
    cos = cos_ref[...]
    sin = sin_ref[...]
    o = Q_LORA + KV_LORA
    k_rot = (a[:, o:o + 128] * cos + a[:, o + 128:o + 256] * sin).astype(BF16)
    qa = _dot(cq.astype(BF16), wq_ref[...])
    kva = _dot(ckv.astype(BF16), wkv_ref[...])
    hw = MLA_HEADS * 128
    scale = (NOPE_DIM + ROPE_DIM) ** -0.5 * LOG2_E
    for hd in range(MLA_HEADS):
        lo = hd * 128
        q_rot = qa[:, hw + lo:hw + lo + 128] * cos + qa[:, 2 * hw + lo:2 * hw + lo + 128] * sin
        q_ref[:, hd * QK_PAD:hd * QK_PAD + 128] = (qa[:, lo:lo + 128] * scale).astype(BF16)
        q_ref[:, hd * QK_PAD + 128:(hd + 1) * QK_PAD] = (q_rot * scale).astype(BF16)
        k_ref[:, hd * QK_PAD:hd * QK_PAD + 128] = kva[:, lo:lo + 128].astype(BF16)
        k_ref[:, hd * QK_PAD + 128:(hd + 1) * QK_PAD] = k_rot
        v_ref[:, hd * V_PAD:hd * V_PAD + V_DIM] = kva[:, hw + lo:hw + lo + 128].astype(BF16)
        v_ref[:, hd * V_PAD + V_DIM:(hd + 1) * V_PAD] = jnp.ones((x.shape[0], V_PAD - V_DIM), BF16)


def _pre_mla(x, mods, g0, wd, gq, gkv, wq, wkv, cos_t, sin_t, tm):
    b, n, d = x.shape
    nb_mod = mods.shape[0]
    full = lambda a: pl.BlockSpec(a.shape, lambda i, j: (0,) * a.ndim)
    mod_map = (lambda i, j: (i, 0, 0)) if nb_mod > 1 else (lambda i, j: (0, 0, 0))
    qk_w = MLA_HEADS * QK_PAD
    v_w = MLA_HEADS * V_PAD
    return pl.pallas_call(
        _pre_mla_kernel,
        out_shape=(jax.ShapeDtypeStruct((b, n, qk_w), BF16),
                   jax.ShapeDtypeStruct((b, n, qk_w), BF16),
                   jax.ShapeDtypeStruct((b, n, v_w), BF16)),
        grid=(b, n // tm),
        in_specs=[pl.BlockSpec((None, tm, d), lambda i, j: (i, j, 0)),
                  pl.BlockSpec((None, 1, mods.shape[-1]), mod_map),
                  full(g0), full(wd), full(gq), full(gkv), full(wq), full(wkv),
                  pl.BlockSpec((tm, 128), lambda i, j: (j, 0)),
                  pl.BlockSpec((tm, 128), lambda i, j: (j, 0))],
        out_specs=(pl.BlockSpec((None, tm, qk_w), lambda i, j: (i, j, 0)),
                   pl.BlockSpec((None, tm, qk_w), lambda i, j: (i, j, 0)),
                   pl.BlockSpec((None, tm, v_w), lambda i, j: (i, j, 0))),
        compiler_params=_cparams(("arbitrary", "arbitrary")),
        name="pre_mla",
    )(x, mods, g0, wd, gq, gkv, wq, wkv, cos_t, sin_t)


def _attn_kernel(*refs, n_seg):
    q_ref = refs[0]
    k_refs = refs[1:1 + n_seg]
    v_refs = refs[1 + n_seg:1 + 2 * n_seg]
    o_ref = refs[1 + 2 * n_seg]
    nt = (((1,), (1,)), ((), ()))

    def scores(hd):
        q = q_ref[:, hd * QK_PAD:(hd + 1) * QK_PAD]
        return [lax.dot_general(q, k[:, hd * QK_PAD:(hd + 1) * QK_PAD], nt, preferred_element_type=F32)
                for k in k_refs]

    nxt = scores(0)
    for hd in range(MLA_HEADS):
        ss = nxt
        if hd + 1 < MLA_HEADS:
            nxt = scores(hd + 1)
        m = ss[0].max(axis=-1, keepdims=True)
        for s in ss[1:]:
            m = jnp.maximum(m, s.max(axis=-1, keepdims=True))
        acc = None
        for s, v in zip(ss, v_refs):
            pv = _dot(jnp.exp2((s - m).astype(BF16)), v[:, hd * V_PAD:(hd + 1) * V_PAD])
            acc = pv if acc is None else acc + pv
        o_ref[:, hd * V_DIM:(hd + 1) * V_DIM] = (acc[:, :V_DIM] / acc[:, V_DIM:V_DIM + 1]).astype(BF16)


def _attention(q, ks, vs, tq):
    b, nq, qk_w = q.shape
    v_w = MLA_HEADS * V_DIM
    kv_spec = lambda a: pl.BlockSpec((None,) + a.shape[1:], lambda i, j: (i, 0, 0))
    return pl.pallas_call(
        functools.partial(_attn_kernel, n_seg=len(ks)),
        out_shape=jax.ShapeDtypeStruct((b, nq, v_w), BF16),
        grid=(b, nq // tq),
        in_specs=[pl.BlockSpec((None, tq, qk_w), lambda i, j: (i, j, 0))]
                 + [kv_spec(a) for a in ks] + [kv_spec(a) for a in vs],
        out_specs=pl.BlockSpec((None, tq, v_w), lambda i, j: (i, j, 0)),
        compiler_params=_cparams(("arbitrary", "arbitrary")),
        name="mla_attention",
    )(q, *ks, *vs)


SUB_ROWS = 256


def _sub_tiles(n):
    return [slice(r, r + SUB_ROWS) for r in range(0, n, SUB_ROWS)]


def _post_core(o, x, rows, mod_ref, g1_ref, g2_ref, wr_ref, x1_ref, fin_ref, lg_ref):
    d = x.shape[-1]
    ne = lg_ref.shape[0]
    gate = _rows(_mod_chunk(mod_ref, 2, d), x)
    shift = _rows(_mod_chunk(mod_ref, 3, d), x)
    scale = _rows(_mod_chunk(mod_ref, 4, d), x)
    x1 = x + gate * _rms(o, g1_ref[...])
    fin = _rms(x1, g2_ref[...]) * (1.0 + scale) + shift
    x1_ref[rows, :] = x1
    _pack_store(fin_ref, fin, row0=rows.start)
    nt = (((1,), (1,)), ((), ()))
    f_hi = fin.astype(BF16)
    f_lo = (fin - f_hi.astype(F32)).astype(BF16)
    r_hi = lax.dot_general(wr_ref[...], f_hi, nt, preferred_element_type=F32)
    r_lo = lax.dot_general(wr_ref[0:ne, :], f_lo, nt, preferred_element_type=F32)
    lg_ref[:, rows] = r_hi[:ne] + r_hi[ne:] + r_lo


def _post_proj_kernel(o_ref, wo_ref, x_ref, mod_ref, g1_ref, g2_ref, wr_ref, *rest):
    x1_ref, fin_ref, lg_ref = rest[-3:]
    for rows in _sub_tiles(x_ref.shape[0]):
        o = _dot(o_ref[rows, :], wo_ref[...])
        _post_core(o, x_ref[rows, :], rows, mod_ref, g1_ref, g2_ref, wr_ref, x1_ref, fin_ref, lg_ref)


def _post_glu_kernel(h_ref, yf_ref, yb_ref, dsk_ref, wg_ref, bg_ref, x_ref, mod_ref, g1_ref, g2_ref, wr_ref,
                     *rest):
    x1_ref, fin_ref, lg_ref = rest[-3:]
    d = x_ref.shape[-1]
    for rows in _sub_tiles(x_ref.shape[0]):
        wide = lambda r: jnp.concatenate([r[g, rows, :] for g in range(r.shape[0])], axis=-1)
        y = wide(h_ref) * dsk_ref[...] + wide(yf_ref) + wide(yb_ref)
        z = _dot(jax.nn.gelu(y, approximate=True).astype(BF16), wg_ref[...]) + bg_ref[...]
        o = z[:, :d] * jax.nn.sigmoid(z[:, d:])
        _post_core(o, x_ref[rows, :], rows, mod_ref, g1_ref, g2_ref, wr_ref, x1_ref, fin_ref, lg_ref)


def _post_mixer(kernel, tok_inputs, consts, x, n_tok, x_off, mods, g1, g2, wr_t, tm, rows_per_mod, name,
                moe_total=None, moe_off=0, prev=None):
    d = x.shape[-1]
    ne = wr_t.shape[0] // 2
    moe_total = n_tok if moe_total is None else moe_total
    tiles_per_mod = rows_per_mod // tm
    xo, mo = x_off // tm, moe_off // tm
    full = lambda a: pl.BlockSpec(a.shape, lambda i: (0,) * a.ndim)
    tile = pl.BlockSpec((tm, d), lambda i: (i, 0))
    mod_spec = pl.BlockSpec((None,) + mods.shape[1:], lambda i: (i // tiles_per_mod, 0, 0))
    in_specs = ([spec for _, spec in tok_inputs] + [full(a) for a in consts]
                + [pl.BlockSpec((tm, d), lambda i: (i + xo, 0)), mod_spec, full(g1), full(g2), full(wr_t)])
    args = [a for a, _ in tok_inputs] + list(consts) + [x, mods, g1, g2, wr_t]
    aliases = {}
    if prev is not None:
        aliases = {len(args): 1, len(args) + 1: 2}
        in_specs += [pl.BlockSpec(memory_space=pl.ANY)] * 2
        args += list(prev)
    return pl.pallas_call(
        kernel,
        out_shape=(jax.ShapeDtypeStruct((n_tok, d), F32),
                   jax.ShapeDtypeStruct((moe_total * PACK_ROWS, LANES), U32),
                   jax.ShapeDtypeStruct((ne, moe_total), F32)),
        grid=(n_tok // tm,),
        in_specs=in_specs,
        out_specs=(tile, pl.BlockSpec((tm * PACK_ROWS, LANES), lambda i: (i + mo, 0)),
                   pl.BlockSpec((ne, tm), lambda i: (0, i + mo))),
        input_output_aliases=aliases,
        compiler_params=_cparams(("arbitrary",)),
        name=name,
    )(*args)


def _route_kernel(lg_ref, bias_ref, eidx_ref, gate_ref, rank_ref, cnt_ref, tri_ref, base_ref):
    i = pl.program_id(0)
    ne, tt = lg_ref.shape
    gsz = ne // N_EXPERT_GROUPS
    shp = (N_EXPERT_GROUPS, gsz, tt)
    neg = -jnp.inf

    @pl.when(i == 0)
    def _():
        base_ref[...] = jnp.zeros_like(base_ref)
        r = lax.broadcasted_iota(jnp.int32, (tt, tt), 0)
        c = lax.broadcasted_iota(jnp.int32, (tt, tt), 1)
        tri_ref[...] = (r < c).astype(BF16)

    scores = jax.nn.sigmoid(lg_ref[...])
    s3 = scores.reshape(shp)
    b3 = (scores + bias_ref[...]).reshape(shp)
    io_e = lax.broadcasted_iota(jnp.int32, shp, 1)
    io_g = lax.broadcasted_iota(jnp.int32, shp, 0)
    io_flat = io_g * gsz + io_e
    m1 = b3.max(axis=1, keepdims=True)
    i1 = jnp.where(b3 == m1, io_e, gsz).min(axis=1, keepdims=True)
    m2 = jnp.where(io_e == i1, neg, b3).max(axis=1, keepdims=True)
    cur = jnp.broadcast_to(m1 + m2, shp)
    gsel = jnp.zeros(shp, jnp.bool_)
    for _ in range(TOPK_GROUPS):
        m = cur.max(axis=0, keepdims=True)
        gi = jnp.where(cur == m, io_g, N_EXPERT_GROUPS).min(axis=0, keepdims=True)
        hit = io_g == gi
        gsel = jnp.logical_or(gsel, hit)
        cur = jnp.where(hit, neg, cur)
    cand = jnp.where(gsel, b3, neg)
    sel = jnp.zeros(shp, jnp.bool_)
    eids, gts = [], []
    for _ in range(TOP_K):
        m = cand.max(axis=0, keepdims=True).max(axis=1, keepdims=True)
        ei = jnp.where(cand == m, io_flat, ne).min(axis=0, keepdims=True).min(axis=1, keepdims=True)
        hit = io_flat == ei
        gts.append(jnp.where(hit, s3, 0.0).sum(axis=0, keepdims=True).sum(axis=1, keepdims=True))
        eids.append(ei)
        sel = jnp.logical_or(sel, hit)
        cand = jnp.where(hit, neg, cand)
    gsum = gts[0]
    for g in gts[1:]:
        gsum = gsum + g
    self32 = sel.astype(F32).reshape(ne, tt)
    cnt = _dot(self32.astype(BF16), tri_ref[...]) + base_ref[...]
    cnt3 = cnt.reshape(shp)
    for k in range(TOP_K):
        hit = io_flat == eids[k]
        rk = jnp.where(hit, cnt3, 0.0).sum(axis=0, keepdims=True).sum(axis=1, keepdims=True)
        rank_ref[k:k + 1, :] = rk.reshape(1, tt).astype(jnp.int32)
        eidx_ref[k:k + 1, :] = eids[k].reshape(1, tt)
        gate_ref[k:k + 1, :] = (gts[k] / gsum * ROUTED_SCALE).reshape(1, tt)
    base_new = base_ref[...] + self32.sum(axis=1, keepdims=True)
    base_ref[...] = base_new
    cnt_ref[...] = jnp.broadcast_to(base_new, cnt_ref.shape)


def _route(logits_t, bias, tt, tok0, t):
    ne = logits_t.shape[0]
    off = tok0 // tt
    out_i = jax.ShapeDtypeStruct((TOP_K, t), jnp.int32)
    row = pl.BlockSpec((TOP_K, tt), lambda i: (0, i))
    return pl.pallas_call(
        _route_kernel,
        out_shape=(out_i, jax.ShapeDtypeStruct((TOP_K, t), F32), out_i,
                   jax.ShapeDtypeStruct((ne, 128), F32)),
        grid=(t // tt,),
        in_specs=[pl.BlockSpec((ne, tt), lambda i: (0, i + off)),
                  pl.BlockSpec((ne, 1), lambda i: (0, 0))],
        out_specs=(row, row, row, pl.BlockSpec((ne, 128), lambda i: (0, 0))),
        scratch_shapes=[pltpu.VMEM((tt, tt), BF16), pltpu.VMEM((ne, 1), F32)],
        compiler_params=_cparams(("arbitrary",)),
        name="moe_route",
    )(logits_t, bias.reshape(ne, 1))


def _dest_kernel(eidx_ref, rank_ref, start_ref, dest_ref):
    kk, tt = eidx_ref.shape
    ne = start_ref.shape[0]
    n_chunk, _, r = dest_ref.shape
    io_e = lax.broadcasted_iota(jnp.int32, (ne, tt), 0)
    start = start_ref[...]
    for k in range(kk):
        hit = io_e == eidx_ref[k:k + 1, :]
        dk = jnp.where(hit, start, 0).sum(axis=0, keepdims=True) + rank_ref[k:k + 1, :]
        for c in range(n_chunk):
            dest_ref[c, k:k + 1, :] = dk[:, c * r:(c + 1) * r]


def _dest_rows(eidx_t, rank_t, start, tt, r):
    kk, t = eidx_t.shape
    ne = start.shape[0]
    return pl.pallas_call(
        _dest_kernel,
        out_shape=jax.ShapeDtypeStruct((t // r, kk, r), jnp.int32),
        grid=(t // tt,),
        in_specs=[pl.BlockSpec((kk, tt), lambda i: (0, i)),
                  pl.BlockSpec((kk, tt), lambda i: (0, i)),
                  pl.BlockSpec((ne, 1), lambda i: (0, 0))],
        out_specs=pl.BlockSpec((tt // r, kk, r), lambda i: (i, 0, 0)),
        compiler_params=_cparams(("arbitrary",)),
        name="moe_dest",
    )(eidx_t, rank_t, start.reshape(ne, 1))


SC_CHUNK = 64


def _sc_mesh():
    return plsc.VectorSubcoreMesh(core_axis_name="c", subcore_axis_name="s")


def _sc_workers():
    info = plsc.get_sparse_core_info()
    return info.num_cores, info.num_cores * info.num_subcores


def _sc_scatter_rows(rows, dest, n_out, row0=0):
    n_chunk, kk, r = dest.shape
    nc, nw = _sc_workers()
    cpw = n_chunk // nw
    assert cpw * nw == n_chunk and cpw % 2 == 0 and row0 % r == 0 and row0 + n_chunk * r <= rows.shape[0]

    @functools.partial(
        pl.kernel, mesh=_sc_mesh(),
        out_type=jax.ShapeDtypeStruct((n_out,) + rows.shape[1:], rows.dtype),
        scratch_types=[pltpu.VMEM((2, kk, r), jnp.int32), pltpu.VMEM((2, r) + rows.shape[1:], rows.dtype),
                       pltpu.SemaphoreType.DMA((2,)), pltpu.SemaphoreType.DMA((2,))])
    def scatter(rows_hbm, dest_hbm, out_hbm, idx_v, rows_v, load_sem, scat_sem):
        c0 = (lax.axis_index("s") * nc + lax.axis_index("c")) * cpw

        def loads(c, b):
            return (pltpu.make_async_copy(dest_hbm.at[c], idx_v.at[b], load_sem.at[b]),
                    pltpu.make_async_copy(rows_hbm.at[pl.ds(row0 + c * r, r)], rows_v.at[b], load_sem.at[b]))

        def scat(b, k):
            return pltpu.make_async_copy(rows_v.at[b], out_hbm.at[idx_v.at[b, k]], scat_sem.at[b])

        for cp in loads(c0, 0):
            cp.start()

        @pl.loop(0, cpw, step=2)
        def _(ci):
            for b in range(2):
                c = c0 + ci + b
                for cp in loads(c, b):
                    cp.wait()
                for k in range(kk):
                    scat(b, k).start()

                @pl.when(ci + b >= 1)
                def _():
                    for k in range(kk):
                        scat(1 - b, k).wait()

                @pl.when(ci + b + 1 < cpw)
                def _():
                    for cp in loads(c + 1, 1 - b):
                        cp.start()

        for k in range(kk):
            scat((cpw - 1) % 2, k).wait()

    return scatter(rows, dest)


def _sc_gather_rows(src, dest):
    n_chunk, kk, r = dest.shape
    t = n_chunk * r
    nc, nw = _sc_workers()
    cpw = n_chunk // nw
    nbuf = 3
    assert cpw * nw == n_chunk and kk > nbuf

    @functools.partial(
        pl.kernel, mesh=_sc_mesh(),
        out_type=jax.ShapeDtypeStruct((kk, t) + src.shape[1:], src.dtype),
        scratch_types=[pltpu.VMEM((kk, r), jnp.int32), pltpu.VMEM((nbuf, r) + src.shape[1:], src.dtype),
                       pltpu.SemaphoreType.DMA((nbuf,)), pltpu.SemaphoreType.DMA((nbuf,))])
    def gather(src_hbm, dest_hbm, out_hbm, idx_v, rows_v, get_sem, put_sem):
        c0 = (lax.axis_index("s") * nc + lax.axis_index("c")) * cpw

        @pl.loop(0, cpw)
        def _(ci):
            c = c0 + ci
            pltpu.sync_copy(dest_hbm.at[c], idx_v)

            def get(k):
                return pltpu.make_async_copy(src_hbm.at[idx_v.at[k]], rows_v.at[k % nbuf], get_sem.at[k % nbuf])

            def put(k):
                return pltpu.make_async_copy(rows_v.at[k % nbuf], out_hbm.at[k, pl.ds(c * r, r)],
                                             put_sem.at[k % nbuf])

            for k in range(nbuf - 1):
                get(k).start()
            for k in range(kk):
                get(k).wait()
                put(k).start()
                if k + nbuf - 1 < kk:
                    if k >= 1:
                        put(k - 1).wait()
                    get(k + nbuf - 1).start()
            for k in range(kk - nbuf, kk):
                put(k).wait()

    return gather(src, dest)


def _expert_kernel(be_ref, nu_ref, x_ref, wg_ref, wu_ref, wd_ref, o_ref, wgu_s, wd_s):
    i = pl.program_id(0)
    tb = o_ref.shape[0] // PACK_ROWS

    @pl.when(i < nu_ref[0])
    def _():
        @pl.when(jnp.logical_or(i == 0, be_ref[i] != be_ref[jnp.maximum(i - 1, 0)]))
        def _():
            wgu_s[:, :D_EXPERT] = wg_ref[...].astype(BF16)
            wgu_s[:, D_EXPERT:] = wu_ref[...].astype(BF16)
            wd_s[...] = wd_ref[...].astype(BF16)

        x = jnp.concatenate([v.astype(BF16) for v in _unpack_load(x_ref, tb)], axis=-1)
        gu = _dot(x, wgu_s[...])
        g = gu[:, :D_EXPERT]
        h = g * jax.nn.sigmoid(g) * gu[:, D_EXPERT:]
        _pack_store(o_ref, _dot(h.astype(BF16), wd_s[...]))


def _experts(xs, blk_e, n_used, w_gate, w_up, w_down, layer, tb):
    rows = xs.shape[0] // PACK_ROWS
    _, ne, d, de = w_gate.shape
    nb = rows // tb
    row_map = lambda i, be, nu: (jnp.minimum(i, nu[0] - 1), 0)
    w_map = lambda i, be, nu: (layer, be[i], 0, 0)
    grid_spec = pltpu.PrefetchScalarGridSpec(
        num_scalar_prefetch=2,
        grid=(nb,),
        in_specs=[pl.BlockSpec((tb * PACK_ROWS, LANES), row_map),
                  pl.BlockSpec((None, None, d, de), w_map),
                  pl.BlockSpec((None, None, d, de), w_map),
                  pl.BlockSpec((None, None, de, d), w_map)],
        out_specs=pl.BlockSpec((tb * PACK_ROWS, LANES), row_map),
        scratch_shapes=[pltpu.VMEM((d, 2 * de), BF16), pltpu.VMEM((de, d), BF16)],
    )
    return pl.pallas_call(
        _expert_kernel,
        out_shape=jax.ShapeDtypeStruct(xs.shape, U32),
        grid_spec=grid_spec,
        compiler_params=_cparams(("arbitrary",)),
        name="moe_experts",
    )(blk_e, n_used, xs, w_gate, w_up, w_down)


def _combine_kernel(yk_ref, gate_ref, fin_ref, shgu_ref, shd_ref, x1_ref, mod_ref, g3_ref, *rest):
    o_ref = rest[-1]
    tm, d = x1_ref.shape
    for rows in _sub_tiles(tm):
        n, r0 = SUB_ROWS, rows.start
        gates = gate_ref[rows, :]
        blocks = None
        for k in range(TOP_K):
            gk = gates[:, k:k + 1]
            terms = [gk * v for v in _unpack_load(yk_ref, n, lead=(k,), row0=r0)]
            blocks = terms if blocks is None else [a + b for a, b in zip(blocks, terms)]
        fin = jnp.concatenate([v.astype(BF16) for v in _unpack_load(fin_ref, n, row0=r0)], axis=-1)
        gu = _dot(fin, shgu_ref[...])
        g = gu[:, :D_EXPERT]
        hsh = g * jax.nn.sigmoid(g) * gu[:, D_EXPERT:]
        f = jnp.concatenate(blocks, axis=-1) + _dot(hsh.astype(BF16), shd_ref[...])
        x1 = x1_ref[rows, :]
        x2 = x1 + _rows(_mod_chunk(mod_ref, 5, d), x1) * _rms(f, g3_ref[...])
        if len(o_ref.shape) == 2:
            o_ref[rows, :] = x2
        else:
            nb = o_ref.shape[0]
            ts = SUB_ROWS // nb
            o_ref[:, r0 // nb:r0 // nb + ts, :] = jnp.swapaxes(x2.reshape(ts, nb, d), 0, 1)


def _combine(yk, gates, fin, shgu, shd, x1, mods, g3, tm, rows_per_mod, n_tok, x_off, yk_off, fin_off,
             batch_out=0, prev=None):
    t, d = x1.shape
    xo, yo, fo = x_off // tm, yk_off // tm, fin_off // tm
    tiles_per_mod = rows_per_mod // tm
    full = lambda a: pl.BlockSpec(a.shape, lambda i: (0,) * a.ndim)
    if batch_out:
        out_shape = jax.ShapeDtypeStruct((batch_out, t // batch_out, d), F32)
        out_spec = pl.BlockSpec((batch_out, tm // batch_out, d), lambda i: (0, i + xo, 0))
    else:
        out_shape = jax.ShapeDtypeStruct((t, d), F32)
        out_spec = pl.BlockSpec((tm, d), lambda i: (i + xo, 0))
    in_specs = [pl.BlockSpec((TOP_K, tm * PACK_ROWS, LANES), lambda i: (0, i + yo, 0)),
                pl.BlockSpec((tm, TOP_K), lambda i: (i + yo, 0)),
                pl.BlockSpec((tm * PACK_ROWS, LANES), lambda i: (i + fo, 0)),
                full(shgu), full(shd),
                pl.BlockSpec((tm, d), lambda i: (i + xo, 0)),
                pl.BlockSpec((None,) + mods.shape[1:], lambda i: ((i + xo) // tiles_per_mod, 0, 0)),
                full(g3)]
    args = [yk, gates, fin, shgu, shd, x1, mods, g3]
    aliases = {}
    if prev is not None:
        in_specs.append(pl.BlockSpec(memory_space=pl.ANY))
        aliases = {len(args): 0}
        args.append(prev)
    return pl.pallas_call(
        _combine_kernel,
        out_shape=out_shape,
        grid=(n_tok // tm,),
        in_specs=in_specs,
        out_specs=out_spec,
        input_output_aliases=aliases,
        compiler_params=_cparams(("arbitrary",)),
        name="moe_combine",
    )(*args)


def _moe(fin, logits_t, bias, w_gate, w_up, w_down, layer, tb, tok0, t):
    t_all = fin.shape[0] // PACK_ROWS
    ne = w_gate.shape[1]
    tt = 512
    eidx_t, gates_t, rank_t, cnt = _route(logits_t, bias, tt, tok0, t)
    counts = cnt[:, 0].astype(jnp.int32)
    padded = (counts + tb - 1) // tb * tb
    pad_end = jnp.cumsum(padded)
    pad_start = pad_end - padded
    nb = (t * TOP_K) // tb + ne
    n_used = pad_end[-1] // tb
    blk_start = jnp.arange(nb, dtype=jnp.int32) * tb
    blk = jnp.sum(pad_end[None, :] <= jnp.minimum(blk_start, pad_end[-1] - 1)[:, None], axis=1)
    blk_e = jnp.minimum(blk, ne - 1).astype(jnp.int32)
    dest = _dest_rows(eidx_t, rank_t, pad_start, tt, SC_CHUNK)
    xs = _sc_scatter_rows(fin.reshape(t_all, PACK_ROWS, LANES), dest, nb * tb, row0=tok0)
    ys = _experts(xs.reshape(nb * tb * PACK_ROWS, LANES), blk_e, n_used.reshape(1).astype(jnp.int32),
                  w_gate, w_up, w_down, layer, tb)
    yk = _sc_gather_rows(ys.reshape(nb * tb, PACK_ROWS, LANES), dest)
    return yk.reshape(TOP_K, t * PACK_ROWS, LANES), gates_t.T


def _pre_s5_kernel(x_ref, mod_ref, g0_ref, *refs):
    h_ref, xt_ref = refs[-2:]
    nb, tt, d = x_ref.shape
    x = jnp.swapaxes(x_ref[...], 0, 1).reshape(tt * nb, d)
    h = (_rms(x, g0_ref[...]) * (1.0 + _rows(_mod_chunk(mod_ref, 1, d), x))
         + _rows(_mod_chunk(mod_ref, 0, d), x))
    for g in range(h_ref.shape[0]):
        h_ref[g] = h[:, g * LANES:(g + 1) * LANES]
    xt_ref[...] = x


def _pre_s5(x, mods, g0, n_total, t_off, prev, tt):
    nb, n, d = x.shape
    off = t_off // tt
    out_shape = (jax.ShapeDtypeStruct((d // LANES, n_total * nb, LANES), F32),
                 jax.ShapeDtypeStruct((n_total * nb, d), F32))
    out_specs = (pl.BlockSpec((d // LANES, tt * nb, LANES), lambda i: (0, i + off, 0)),
                 pl.BlockSpec((tt * nb, d), lambda i: (i + off, 0)))
    in_specs = [pl.BlockSpec((nb, tt, d), lambda i: (0, i, 0)),
                pl.BlockSpec(mods.shape, lambda i: (0, 0)),
                pl.BlockSpec(g0.shape, lambda i: (0, 0))]
    args = (x, mods, g0)
    aliases = {}
    if prev is not None:
        in_specs += [pl.BlockSpec(memory_space=pl.ANY)] * 2
        args += tuple(prev)
        aliases = {3: 0, 4: 1}
    return pl.pallas_call(
        _pre_s5_kernel,
        out_shape=out_shape,
        grid=(n // tt,),
        in_specs=in_specs,
        out_specs=out_specs,
        input_output_aliases=aliases,
        compiler_params=_cparams(("arbitrary",)),
        name="pre_s5",
    )(*args)


def _s5_scan_kernel(h_ref, bm_ref, cm_ref, lam_ref, y_ref, bu0, bu1, xb0, xb1, st_ref):
    first = jnp.logical_and(jnp.logical_and(pl.program_id(0) == 0, pl.program_id(1) == 0), pl.program_id(2) == 0)
    dr = pl.program_id(1)
    s = pl.program_id(2)
    tc, nb, cw = h_ref.shape
    half = st_ref.shape[1] // 2

    @pl.when(first)
    def _():
        for r in (bu0, bu1, xb0, xb1, st_ref):
            r[...] = jnp.zeros_like(r)

    def stages(bu_w, bu_r, xb_w, xb_r):
        lr = jnp.broadcast_to(lam_ref[0:1, :], (nb, half))
        li = jnp.broadcast_to(lam_ref[1:2, :], (nb, half))
        fresh = s == 1
        xr = jnp.where(fresh, 0.0, st_ref[:, 0:half])
        xi = jnp.where(fresh, 0.0, st_ref[:, half:2 * half])
        tp = tc // S5_PIECES
        for p in range(S5_PIECES):
            for i in range(p * tp, (p + 1) * tp):
                t = i + dr * (tc - 1 - 2 * i)
                rows = pl.ds(pl.multiple_of(t * nb, nb), nb)
                nr = lr * xr - li * xi + bu_r[rows, 0:half]
                ni = lr * xi + li * xr + bu_r[rows, half:2 * half]
                xb_w[rows, 0:half] = nr.astype(BF16)
                xb_w[rows, half:2 * half] = ni.astype(BF16)
                xr, xi = nr, ni
            ts = slice(p * tp, (p + 1) * tp)
            mr = slice(p * tp * nb, (p + 1) * tp * nb)
            y_ref[ts] = _dot(xb_r[mr, :], cm_ref[...]).reshape(tp, nb, cw)
            bu_w[mr, :] = _dot(h_ref[ts].reshape(tp * nb, cw).astype(BF16), bm_ref[...])
        st_ref[:, 0:half] = xr
        st_ref[:, half:2 * half] = xi

    @pl.when(s % 2 == 0)
    def _():
        stages(bu0, bu1, xb1, xb0)

    @pl.when(s % 2 == 1)
    def _():
        stages(bu1, bu0, xb0, xb1)


def _s5_scan(h_all, n_ctx, bm, cm, lam, tc):
    ng, nt, nb, cw = h_all.shape
    nl = nt - n_ctx
    assert cw == S5_GROUPS_PER_BLOCK * S5_GROUP
    sw = 2 * S5_GROUPS_PER_BLOCK * S5_STATE
    ncc, n = n_ctx // tc, nt // tc

    def chunk(dr, j):
        j = jnp.clip(j, 0, n - 1)
        rev = jnp.where(j < ncc, ncc - 1 - j, n - 1 - (j - ncc))
        return jnp.where(dr == 0, j, rev)

    def out_map(g, dr, s):
        return (dr, g, chunk(dr, jnp.clip(s - 2, ncc, n - 1)) - ncc, 0, 0)

    return pl.pallas_call(
        _s5_scan_kernel,
        out_shape=jax.ShapeDtypeStruct((2, ng, nl, nb, cw), F32),
        grid=(ng, 2, n + 2),
        in_specs=[pl.BlockSpec((None, tc, nb, cw), lambda g, dr, s: (g, chunk(dr, s), 0, 0)),
                  pl.BlockSpec((None, None, cw, sw), lambda g, dr, s: (dr, g, 0, 0)),
                  pl.BlockSpec((None, None, sw, cw), lambda g, dr, s: (dr, g, 0, 0)),
                  pl.BlockSpec((None, None, 2, sw // 2), lambda g, dr, s: (dr, g, 0, 0))],
        out_specs=pl.BlockSpec((None, None, tc, nb, cw), out_map),
        scratch_shapes=[pltpu.VMEM((tc * nb, sw), F32), pltpu.VMEM((tc * nb, sw), F32),
                        pltpu.VMEM((tc * nb, sw), BF16), pltpu.VMEM((tc * nb, sw), BF16),
                        pltpu.VMEM((nb, sw), F32)],
        compiler_params=_cparams(("arbitrary", "arbitrary", "arbitrary")),
        name="s5_scan",
    )(h_all, bm, cm, lam)


def _s5_params(lam_re, lam_im, log_step, b_re, b_im, c_re, c_im):
    g, p = lam_re.shape[1:]
    gb = S5_GROUPS_PER_BLOCK
    nblk = g // gb
    step = jnp.exp(log_step)[..., None]
    mag = jnp.exp(lam_re * step)
    lb_re = mag * jnp.cos(lam_im * step)
    lb_im = mag * jnp.sin(lam_im * step)
    den = lam_re * lam_re + lam_im * lam_im
    f_re = ((lb_re - 1.0) * lam_re + lb_im * lam_im) / den
    f_im = (lb_im * lam_re - (lb_re - 1.0) * lam_im) / den
    bb_re = f_re[..., None] * b_re - f_im[..., None] * b_im
    bb_im = f_re[..., None] * b_im + f_im[..., None] * b_re
    eye = jnp.eye(gb, dtype=F32)

    def in_map(w):
        w = w.reshape(2, nblk, gb, p, S5_GROUP)
        return jnp.einsum("dnapi,ab->dnaibp", w, eye).reshape(2, nblk, gb * S5_GROUP, gb * p)

    def out_map(w):
        w = w.reshape(2, nblk, gb, S5_GROUP, p)
        return jnp.einsum("dnaip,ab->dnapbi", w, eye).reshape(2, nblk, gb * p, gb * S5_GROUP)

    bm = jnp.concatenate([in_map(bb_re), in_map(bb_im)], axis=-1).astype(BF16)
    cm = jnp.concatenate([out_map(c_re), out_map(-c_im)], axis=-2).astype(BF16)
    lam = jnp.stack([lb_re.reshape(2, nblk, gb * p), lb_im.reshape(2, nblk, gb * p)], axis=2)
    return bm, cm, lam


def _rope_tables(n_tokens):
    rows = n_tokens // GRID_W
    row = jnp.repeat(jnp.arange(rows), GRID_W).astype(F32)
    col = jnp.tile(jnp.arange(GRID_W), rows).astype(F32)
    n_freq = ROPE_DIM // 4
    inv_freq = ROPE_BASE ** (-jnp.arange(n_freq, dtype=F32) / n_freq)
    ang = jnp.concatenate([row[:, None] * inv_freq, col[:, None] * inv_freq], axis=-1)
    cos, sin = jnp.cos(ang), jnp.sin(ang)
    z = jnp.zeros((n_tokens, 128 - ROPE_DIM), F32)
    return (jnp.concatenate([cos, cos, z], axis=-1), jnp.concatenate([-sin, sin, z], axis=-1))


def _router_halves(w_router):
    wt = w_router.T
    hi = wt.astype(BF16)
    lo = (wt - hi.astype(F32)).astype(BF16)
    return jnp.concatenate([hi, lo], axis=0)


def _split_pairs(w):
    ev, od = w[..., 0::2], w[..., 1::2]
    z = jnp.zeros(w.shape[:-1] + (128 - ROPE_DIM,), w.dtype)
    return jnp.concatenate([ev, od, z], axis=-1), jnp.concatenate([od, ev, z], axis=-1)


def _mla_weights(w_dqkv, w_uq, w_ukv):
    kp, kps = _split_pairs(w_dqkv[:, Q_LORA + KV_LORA:])
    wd = jnp.concatenate([w_dqkv[:, :Q_LORA + KV_LORA], kp, kps], axis=-1).astype(BF16)
    wq3 = w_uq.reshape(Q_LORA, MLA_HEADS, NOPE_DIM + ROPE_DIM)
    qp, qps = _split_pairs(wq3[:, :, NOPE_DIM:])
    wq = jnp.concatenate([wq3[:, :, :NOPE_DIM].reshape(Q_LORA, -1), qp.reshape(Q_LORA, -1),
                          qps.reshape(Q_LORA, -1)], axis=-1).astype(BF16)
    wkv3 = w_ukv.reshape(KV_LORA, MLA_HEADS, NOPE_DIM + V_DIM)
    wkv = jnp.concatenate([wkv3[:, :, :NOPE_DIM].reshape(KV_LORA, -1),
                           wkv3[:, :, NOPE_DIM:].reshape(KV_LORA, -1)], axis=-1).astype(BF16)
    return wd, wq, wkv


@jax.jit
def kernel(x, c, ctx, c_ctx, ada_w, ada_b, norm_g, mla_w_dqkv, mla_g_q, mla_g_kv, mla_w_uq, mla_w_ukv, mla_w_o, s5_lam_re, s5_lam_im, s5_log_step, s5_b_re, s5_b_im, s5_c_re, s5_c_im, s5_d, s5_w_glu, s5_b_glu, moe_w_router, moe_bias, moe_w_gate, moe_w_up, moe_w_down, sh_w_gate, sh_w_up, sh_w_down):
    b, l, d = x.shape
    n_ctx = ctx.shape[1]
    assert ada_w.shape[0] == 2 and b % 8 == 0
    ta = 256
    tm = 512
    tb = 512
    row = lambda v: v.reshape(1, -1)

    n_rows = (b + 1 + 7) // 8 * 8
    cvec = jnp.zeros((n_rows, d), F32).at[:b].set(c).at[b].set(c_ctx)
    mods = _ada_mods(cvec, ada_w, ada_b)

    def shared_weights(i):
        shgu = jnp.concatenate([sh_w_gate[i], sh_w_up[i]], axis=-1).astype(BF16)
        return shgu, sh_w_down[i].astype(BF16)

    mod_lat = mods[0, :b].reshape(b, 1, N_MOD * d)
    mod_ctx = mods[0, b].reshape(1, 1, N_MOD * d)
    wd, wq, wkv = _mla_weights(mla_w_dqkv[0], mla_w_uq[0], mla_w_ukv[0])
    cos_l, sin_l = _rope_tables(l)
    cos_c = jnp.concatenate([jnp.ones((n_ctx, ROPE_DIM), F32), jnp.zeros((n_ctx, 128 - ROPE_DIM), F32)], -1)
    sin_c = jnp.zeros((n_ctx, 128), F32)
    pre = functools.partial(_pre_mla, g0=row(norm_g[0, 0]), wd=wd, gq=row(mla_g_q[0]), gkv=row(mla_g_kv[0]),
                            wq=wq, wkv=wkv, tm=ta)
    q_c, k_c, v_c = pre(ctx, mod_ctx, cos_t=cos_c, sin_t=sin_c)
    q_l, k_l, v_l = pre(x, mod_lat, cos_t=cos_l, sin_t=sin_l)
    o_l = _attention(q_l, [k_c, k_l], [v_c, v_l], ta)
    o_c = _attention(q_c, [k_c], [v_c], n_ctx)

    wo = mla_w_o[0].astype(BF16)
    wr_t = _router_halves(moe_w_router[0])
    g1, g2, g3 = row(norm_g[0, 1]), row(norm_g[0, 2]), row(norm_g[0, 3])
    post = functools.partial(_post_mixer, _post_proj_kernel, consts=[wo], g1=g1, g2=g2, wr_t=wr_t, tm=tm,
                             name="post_mla")
    o_spec = pl.BlockSpec((tm, o_l.shape[-1]), lambda i: (i, 0))
    n_moe = b * (n_ctx + l)
    x1_c, fin, lg = post([(o_c.reshape(b * n_ctx, -1), o_spec)], x=ctx.reshape(b * n_ctx, d),
                         n_tok=b * n_ctx, x_off=0, mods=mod_ctx, rows_per_mod=b * n_ctx, moe_total=n_moe)
    x1_l, fin, lg = post([(o_l.reshape(b * l, -1), o_spec)], x=x.reshape(b * l, d), n_tok=b * l, x_off=0,
                         mods=mod_lat, rows_per_mod=l, moe_total=n_moe, moe_off=b * n_ctx, prev=(fin, lg))
    shgu, shd = shared_weights(0)
    lat_a = (b // 2) * l
    n_a = b * n_ctx + lat_a
    moe = functools.partial(_moe, fin, lg, moe_bias[0], moe_w_gate, moe_w_up, moe_w_down, 0, tb)
    yk_a, gates_a = moe(0, n_a)
    yk_b, gates_b = moe(n_a, n_moe - n_a)
    comb = functools.partial(_combine, fin=fin, shgu=shgu, shd=shd, g3=g3, tm=tm)
    x2_c = comb(yk_a, gates_a, x1=x1_c, mods=mod_ctx, rows_per_mod=b * n_ctx, n_tok=b * n_ctx, x_off=0, yk_off=0,
                fin_off=0)
    x2_l = comb(yk_a, gates_a, x1=x1_l, mods=mod_lat, rows_per_mod=l, n_tok=lat_a, x_off=0, yk_off=b * n_ctx,
                fin_off=b * n_ctx)
    x2_l = comb(yk_b, gates_b, x1=x1_l, mods=mod_lat, rows_per_mod=l, n_tok=b * l - lat_a, x_off=lat_a, yk_off=0,
                fin_off=n_a, prev=x2_l)

    n_all = n_ctx + l
    mod_lat = mods[1, :b]
    mod_ctx = jnp.broadcast_to(mods[1, b][None], (b, N_MOD * d))
    g0 = row(norm_g[1, 0])
    tt = tm // b
    h_xt = _pre_s5(x2_c.reshape(b, n_ctx, d), mod_ctx, g0, n_all, 0, None, tt)
    h, xt = _pre_s5(x2_l.reshape(b, l, d), mod_lat, g0, n_all, n_ctx, h_xt, tt)
    bm, cm, lam = _s5_params(s5_lam_re[0], s5_lam_im[0], s5_log_step[0], s5_b_re[0], s5_b_im[0],
                             s5_c_re[0], s5_c_im[0])
    ng = d // LANES
    y = _s5_scan(h.reshape(ng, n_all, b, LANES), n_ctx, bm, cm, lam, 64).reshape(2, ng, l * b, LANES)
    g1, g2, g3 = row(norm_g[1, 1]), row(norm_g[1, 2]), row(norm_g[1, 3])
    lat0 = n_ctx * b // tm
    x1, fin, lg = _post_mixer(
        _post_glu_kernel,
        [(h, pl.BlockSpec((ng, tm, LANES), lambda i: (0, i + lat0, 0))),
         (y, pl.BlockSpec((None, ng, tm, LANES), lambda i: (0, 0, i, 0))),
         (y, pl.BlockSpec((None, ng, tm, LANES), lambda i: (1, 0, i, 0)))],
        [row(s5_d[0]), s5_w_glu[0].astype(BF16), row(s5_b_glu[0])],
        x=xt, n_tok=l * b, x_off=n_ctx * b, mods=mod_lat[None], g1=g1, g2=g2, wr_t=_router_halves(moe_w_router[1]), tm=tm,
        rows_per_mod=l * b, name="post_s5")
    shgu, shd = shared_weights(1)
    n_h = (l // 2) * b
    moe = functools.partial(_moe, fin, lg, moe_bias[1], moe_w_gate, moe_w_up, moe_w_down, 1, tb)
    yk_a, gates_a = moe(0, n_h)
    yk_b, gates_b = moe(n_h, l * b - n_h)
    comb = functools.partial(_combine, fin=fin, shgu=shgu, shd=shd, x1=x1, mods=mod_lat[None], g3=g3, tm=tm,
                             rows_per_mod=l * b, yk_off=0, batch_out=b)
    out = comb(yk_a, gates_a, n_tok=n_h, x_off=0, fin_off=0)
    return comb(yk_b, gates_b, n_tok=l * b - n_h, x_off=n_h, fin_off=n_h, prev=out)
```

```python
import functools

import jax
import jax.numpy as jnp
from jax import lax
from jax.experimental import pallas as pl
from jax.experimental.pallas import tpu as pltpu
from jax.experimental.pallas import tpu_sc as plsc

F32 = jnp.float32
BF16 = jnp.bfloat16
U32 = jnp.uint32

N_MOD = 6
NORM_EPS = 1e-6
LOG2_E = 1.4426950408889634
GRID_W = 64
MLA_HEADS = 8
Q_LORA = 384
KV_LORA = 256
NOPE_DIM = 128
ROPE_DIM = 64
V_DIM = 128
V_PAD = 256
ROPE_BASE = 10000.0
QK_PAD = 256
S5_GROUP = 16
S5_STATE = 64
S5_GROUPS_PER_BLOCK = 8
S5_PIECES = 4
N_EXPERTS = 64
TOP_K = 8
N_EXPERT_GROUPS = 8
TOPK_GROUPS = 4
D_EXPERT = 256
ROUTED_SCALE = 2.5

VMEM_LIMIT = 56 * 1024 * 1024


def _cparams(sem):
    return pltpu.CompilerParams(dimension_semantics=sem, vmem_limit_bytes=VMEM_LIMIT)


def _rms(x, g):
    return x * lax.rsqrt(jnp.mean(x * x, axis=-1, keepdims=True) + NORM_EPS) * g


def _rows(v, like):
    r = v.shape[0]
    if r == 1:
        return v
    tm, d = like.shape
    return jnp.broadcast_to(v[None], (tm // r, r, d)).reshape(tm, d)


def _mod_chunk(mod_ref, j, d):
    return mod_ref[:, j * d:(j + 1) * d]


def _dot(a, b):
    return jnp.dot(a, b, preferred_element_type=F32)


PACK_ROWS = 4
LANES = 128


def _pack_store(ref, val, lead=(), row0=0):
    n = val.shape[0]
    bits = lax.bitcast_convert_type(val.astype(BF16).astype(F32), U32)
    for s in range(PACK_ROWS):
        lo = bits[:, s * LANES:(s + 1) * LANES] >> 16
        hi = bits[:, (s + PACK_ROWS) * LANES:(s + PACK_ROWS + 1) * LANES] & jnp.uint32(0xFFFF0000)
        ref[lead + (pl.ds(row0 * PACK_ROWS + s, n, stride=PACK_ROWS), slice(None))] = lo | hi


def _unpack_load(ref, n, lead=(), row0=0):
    los, his = [], []
    for s in range(PACK_ROWS):
        w = ref[lead + (pl.ds(row0 * PACK_ROWS + s, n, stride=PACK_ROWS), slice(None))]
        los.append(lax.bitcast_convert_type(w << 16, F32))
        his.append(lax.bitcast_convert_type(w & jnp.uint32(0xFFFF0000), F32))
    return los + his


def _ada_kernel(c_ref, w_ref, b_ref, o_ref):
    c = c_ref[...]
    s = c * jax.nn.sigmoid(c)
    o_ref[...] = jnp.dot(s, w_ref[...], preferred_element_type=F32,
                         precision=lax.Precision.HIGHEST) + b_ref[...]


def _ada_mods(cvec, ada_w, ada_b):
    depth, d, n = ada_w.shape
    rows = cvec.shape[0]
    tn = 1536
    return pl.pallas_call(
        _ada_kernel,
        out_shape=jax.ShapeDtypeStruct((depth, rows, n), F32),
        grid=(depth, n // tn),
        in_specs=[pl.BlockSpec((rows, d), lambda l, j: (0, 0)),
                  pl.BlockSpec((None, d, tn), lambda l, j: (l, 0, j)),
                  pl.BlockSpec((None, 1, tn), lambda l, j: (l, 0, j))],
        out_specs=pl.BlockSpec((None, rows, tn), lambda l, j: (l, 0, j)),
        compiler_params=_cparams(("arbitrary", "arbitrary")),
        name="ada_mods",
    )(cvec, ada_w, ada_b.reshape(depth, 1, n))


def _pre_mla_kernel(x_ref, mod_ref, g0_ref, wd_ref, gq_ref, gkv_ref, wq_ref, wkv_ref, cos_ref, sin_ref,
                    q_ref, k_ref, v_ref):
    d = x_ref.shape[-1]
    x = x_ref[...]
    h = _rms(x, g0_ref[...]) * (1.0 + _mod_chunk(mod_ref, 1, d)) + _mod_chunk(mod_ref, 0, d)
    a = _dot(h.astype(BF16), wd_ref[...])
    cq = _rms(a[:, :Q_LORA], gq_ref[...])
    ckv = _rms(a[:, Q_LORA:Q_LORA + KV_LORA], gkv_ref[...])
    cos = cos_ref[...]
    sin = sin_ref[...]
    o = Q_LORA + KV_LORA
    k_rot = (a[:, o:o + 128] * cos + a[:, o + 128:o + 256] * sin).astype(BF16)
    qa = _dot(cq.astype(BF16), wq_ref[...])
    kva = _dot(ckv.astype(BF16), wkv_ref[...])
    hw = MLA_HEADS * 128
    scale = (NOPE_DIM + ROPE_DIM) ** -0.5 * LOG2_E
    for hd in range(MLA_HEADS):
        lo = hd * 128
        q_rot = qa[:, hw + lo:hw + lo + 128] * cos + qa[:, 2 * hw + lo:2 * hw + lo + 128] * sin
        q_ref[:, hd * QK_PAD:hd * QK_PAD + 128] = (qa[:, lo:lo + 128] * scale).astype(BF16)
        q_ref[:, hd * QK_PAD + 128:(hd + 1) * QK_PAD] = (q_rot * scale).astype(BF16)
        k_ref[:, hd * QK_PAD:hd * QK_PAD + 128] = kva[:, lo:lo + 128].astype(BF16)
        k_ref[:, hd * QK_PAD + 128:(hd + 1) * QK_PAD] = k_rot
        v_ref[:, hd * V_PAD:hd * V_PAD + V_DIM] = kva[:, hw + lo:hw + lo + 128].astype(BF16)
        v_ref[:, hd * V_PAD + V_DIM:(hd + 1) * V_PAD] = jnp.ones((x.shape[0], V_PAD - V_DIM), BF16)


def _pre_mla(x, mods, g0, wd, gq, gkv, wq, wkv, cos_t, sin_t, tm):
    b, n, d = x.shape
    nb_mod = mods.shape[0]
    full = lambda a: pl.BlockSpec(a.shape, lambda i, j: (0,) * a.ndim)
    mod_map = (lambda i, j: (i, 0, 0)) if nb_mod > 1 else (lambda i, j: (0, 0, 0))
    qk_w = MLA_HEADS * QK_PAD
    v_w = MLA_HEADS * V_PAD
    return pl.pallas_call(
        _pre_mla_kernel,
        out_shape=(jax.ShapeDtypeStruct((b, n, qk_w), BF16),
                   jax.ShapeDtypeStruct((b, n, qk_w), BF16),
                   jax.ShapeDtypeStruct((b, n, v_w), BF16)),
        grid=(b, n // tm),
        in_specs=[pl.BlockSpec((None, tm, d), lambda i, j: (i, j, 0)),
                  pl.BlockSpec((None, 1, mods.shape[-1]), mod_map),
                  full(g0), full(wd), full(gq), full(gkv), full(wq), full(wkv),
                  pl.BlockSpec((tm, 128), lambda i, j: (j, 0)),
                  pl.BlockSpec((tm, 128), lambda i, j: (j, 0))],
        out_specs=(pl.BlockSpec((None, tm, qk_w), lambda i, j: (i, j, 0)),
                   pl.BlockSpec((None, tm, qk_w), lambda i, j: (i, j, 0)),
                   pl.BlockSpec((None, tm, v_w), lambda i, j: (i, j, 0))),
        compiler_params=_cparams(("arbitrary", "arbitrary")),
        name="pre_mla",
    )(x, mods, g0, wd, gq, gkv, wq, wkv, cos_t, sin_t)


def _attn_kernel(*refs, n_seg):
    q_ref = refs[0]
    k_refs = refs[1:1 + n_seg]
    v_refs = refs[1 + n_seg:1 + 2 * n_seg]
    o_ref = refs[1 + 2 * n_seg]
    nt = (((1,), (1,)), ((), ()))

    def scores(hd):
        q = q_ref[:, hd * QK_PAD:(hd + 1) * QK_PAD]
        return [lax.dot_general(q, k[:, hd * QK_PAD:(hd + 1) * QK_PAD], nt, preferred_element_type=F32)
                for k in k_refs]

    nxt = scores(0)
    for hd in range(MLA_HEADS):
        ss = nxt
        if hd + 1 < MLA_HEADS:
            nxt = scores(hd + 1)
        m = ss[0].max(axis=-1, keepdims=True)
        for s in ss[1:]:
            m = jnp.maximum(m, s.max(axis=-1, keepdims=True))
        acc = None
        for s, v in zip(ss, v_refs):
            pv = _dot(jnp.exp2((s - m).astype(BF16)), v[:, hd * V_PAD:(hd + 1) * V_PAD])
            acc = pv if acc is None else acc + pv
        o_ref[:, hd * V_DIM:(hd + 1) * V_DIM] = (acc[:, :V_DIM] / acc[:, V_DIM:V_DIM + 1]).astype(BF16)


def _attention(q, ks, vs, tq):
    b, nq, qk_w = q.shape
    v_w = MLA_HEADS * V_DIM
    kv_spec = lambda a: pl.BlockSpec((None,) + a.shape[1:], lambda i, j: (i, 0, 0))
    return pl.pallas_call(
        functools.partial(_attn_kernel, n_seg=len(ks)),
        out_shape=jax.ShapeDtypeStruct((b, nq, v_w), BF16),
        grid=(b, nq // tq),
        in_specs=[pl.BlockSpec((None, tq, qk_w), lambda i, j: (i, j, 0))]
                 + [kv_spec(a) for a in ks] + [kv_spec(a) for a in vs],
        out_specs=pl.BlockSpec((None, tq, v_w), lambda i, j: (i, j, 0)),
        compiler_params=_cparams(("arbitrary", "arbitrary")),
        name="mla_attention",
    )(q, *ks, *vs)


SUB_ROWS = 256


def _sub_tiles(n):
    return [slice(r, r + SUB_ROWS) for r in range(0, n, SUB_ROWS)]


def _post_core(o, x, rows, mod_ref, g1_ref, g2_ref, wr_ref, x1_ref, fin_ref, lg_ref):
    d = x.shape[-1]
    ne = lg_ref.shape[0]
    gate = _rows(_mod_chunk(mod_ref, 2, d), x)
    shift = _rows(_mod_chunk(mod_ref, 3, d), x)
    scale = _rows(_mod_chunk(mod_ref, 4, d), x)
    x1 = x + gate * _rms(o, g1_ref[...])
    fin = _rms(x1, g2_ref[...]) * (1.0 + scale) + shift
    x1_ref[rows, :] = x1
    _pack_store(fin_ref, fin, row0=rows.start)
    nt = (((1,), (1,)), ((), ()))
    f_hi = fin.astype(BF16)
    f_lo = (fin - f_hi.astype(F32)).astype(BF16)
    r_hi = lax.dot_general(wr_ref[...], f_hi, nt, preferred_element_type=F32)
    r_lo = lax.dot_general(wr_ref[0:ne, :], f_lo, nt, preferred_element_type=F32)
    lg_ref[:, rows] = r_hi[:ne] + r_hi[ne:] + r_lo


def _post_proj_kernel(o_ref, wo_ref, x_ref, mod_ref, g1_ref, g2_ref, wr_ref, *rest):
    x1_ref, fin_ref, lg_ref = rest[-3:]
    for rows in _sub_tiles(x_ref.shape[0]):
        o = _dot(o_ref[rows, :], wo_ref[...])
        _post_core(o, x_ref[rows, :], rows, mod_ref, g1_ref, g2_ref, wr_ref, x1_ref, fin_ref, lg_ref)


def _post_proj_tm_kernel(o_ref, wo_ref, x_ref, mod_ref, g1_ref, g2_ref, wr_ref, *rest):
    x1_ref, fin_ref, lg_ref = rest[-3:]
    nb, tt, d = x_ref.shape
    ts = SUB_ROWS // nb
    for t0 in range(0, tt, ts):
        o = _dot(o_ref[:, t0:t0 + ts, :].reshape(nb * ts, o_ref.shape[-1]), wo_ref[...])
        o = jnp.swapaxes(o.reshape(nb, ts, d), 0, 1).reshape(ts * nb, d)
        x = jnp.swapaxes(x_ref[:, t0:t0 + ts, :], 0, 1).reshape(ts * nb, d)
        _post_core(o, x, slice(t0 * nb, (t0 + ts) * nb), mod_ref, g1_ref, g2_ref, wr_ref, x1_ref, fin_ref, lg_ref)


def _post_glu_kernel(h_ref, yf_ref, yb_ref, dsk_ref, wg_ref, bg_ref, x_ref, mod_ref, g1_ref, g2_ref, wr_ref,
                     *rest):
    x1_ref, fin_ref, lg_ref = rest[-3:]
    d = x_ref.shape[-1]
    for rows in _sub_tiles(x_ref.shape[0]):
        wide = lambda r: jnp.concatenate([r[g, rows, :] for g in range(r.shape[0])], axis=-1)
        y = wide(h_ref) * dsk_ref[...] + wide(yf_ref) + wide(yb_ref)
        z = _dot(jax.nn.gelu(y, approximate=True).astype(BF16), wg_ref[...]) + bg_ref[...]
        o = z[:, :d] * jax.nn.sigmoid(z[:, d:])
        _post_core(o, x_ref[rows, :], rows, mod_ref, g1_ref, g2_ref, wr_ref, x1_ref, fin_ref, lg_ref)


def _post_mixer(kernel, tok_inputs, consts, x, n_tok, x_off, mods, g1, g2, wr_t, tm, rows_per_mod, name,
                moe_total=None, moe_off=0, prev=None, x_spec=None):
    d = x.shape[-1]
    ne = wr_t.shape[0] // 2
    moe_total = n_tok if moe_total is None else moe_total
    tiles_per_mod = rows_per_mod // tm
    xo, mo = x_off // tm, moe_off // tm
    full = lambda a: pl.BlockSpec(a.shape, lambda i: (0,) * a.ndim)
    tile = pl.BlockSpec((tm, d), lambda i: (i, 0))
    mod_spec = pl.BlockSpec((None,) + mods.shape[1:], lambda i: (i // tiles_per_mod, 0, 0))
    x_spec = pl.BlockSpec((tm, d), lambda i: (i + xo, 0)) if x_spec is None else x_spec
    in_specs = ([spec for _, spec in tok_inputs] + [full(a) for a in consts]
                + [x_spec, mod_spec, full(g1), full(g2), full(wr_t)])
    args = [a for a, _ in tok_inputs] + list(consts) + [x, mods, g1, g2, wr_t]
    aliases = {}
    if prev is not None:
        aliases = {len(args): 1, len(args) + 1: 2}
        in_specs += [pl.BlockSpec(memory_space=pl.ANY)] * 2
        args += list(prev)
    return pl.pallas_call(
        kernel,
        out_shape=(jax.ShapeDtypeStruct((n_tok, d), F32),
                   jax.ShapeDtypeStruct((moe_total * PACK_ROWS, LANES), U32),
                   jax.ShapeDtypeStruct((ne, moe_total), F32)),
        grid=(n_tok // tm,),
        in_specs=in_specs,
        out_specs=(tile, pl.BlockSpec((tm * PACK_ROWS, LANES), lambda i: (i + mo, 0)),
                   pl.BlockSpec((ne, tm), lambda i: (0, i + mo))),
        input_output_aliases=aliases,
        compiler_params=_cparams(("arbitrary",)),
        name=name,
    )(*args)


def _route_kernel(lg_ref, bias_ref, eidx_ref, gate_ref, rank_ref, cnt_ref, tri_ref, base_ref):
    i = pl.program_id(0)
    ne, tt = lg_ref.shape
    gsz = ne // N_EXPERT_GROUPS
    shp = (N_EXPERT_GROUPS, gsz, tt)
    neg = -jnp.inf

    @pl.when(i == 0)
    def _():
        base_ref[...] = jnp.zeros_like(base_ref)
        r = lax.broadcasted_iota(jnp.int32, (tt, tt), 0)
        c = lax.broadcasted_iota(jnp.int32, (tt, tt), 1)
        tri_ref[...] = (r < c).astype(BF16)

    scores = jax.nn.sigmoid(lg_ref[...])
    s3 = scores.reshape(shp)
    b3 = (scores + bias_ref[...]).reshape(shp)
    io_e = lax.broadcasted_iota(jnp.int32, shp, 1)
    io_g = lax.broadcasted_iota(jnp.int32, shp, 0)
    io_flat = io_g * gsz + io_e
    m1 = b3.max(axis=1, keepdims=True)
    i1 = jnp.where(b3 == m1, io_e, gsz).min(axis=1, keepdims=True)
    m2 = jnp.where(io_e == i1, neg, b3).max(axis=1, keepdims=True)
    cur = jnp.broadcast_to(m1 + m2, shp)
    gsel = jnp.zeros(shp, jnp.bool_)
    for _ in range(TOPK_GROUPS):
        m = cur.max(axis=0, keepdims=True)
        gi = jnp.where(cur == m, io_g, N_EXPERT_GROUPS).min(axis=0, keepdims=True)
        hit = io_g == gi
        gsel = jnp.logical_or(gsel, hit)
        cur = jnp.where(hit, neg, cur)
    cand = jnp.where(gsel, b3, neg)
    sel = jnp.zeros(shp, jnp.bool_)
    eids, gts = [], []
    for _ in range(TOP_K):
        m = cand.max(axis=0, keepdims=True).max(axis=1, keepdims=True)
        ei = jnp.where(cand == m, io_flat, ne).min(axis=0, keepdims=True).min(axis=1, keepdims=True)
        hit = io_flat == ei
        gts.append(jnp.where(hit, s3, 0.0).sum(axis=0, keepdims=True).sum(axis=1, keepdims=True))
        eids.append(ei)
        sel = jnp.logical_or(sel, hit)
        cand = jnp.where(hit, neg, cand)
    gsum = gts[0]
    for g in gts[1:]:
        gsum = gsum + g
    self32 = sel.astype(F32).reshape(ne, tt)
    cnt = _dot(self32.astype(BF16), tri_ref[...]) + base_ref[...]
    cnt3 = cnt.reshape(shp)
    for k in range(TOP_K):
        hit = io_flat == eids[k]
        rk = jnp.where(hit, cnt3, 0.0).sum(axis=0, keepdims=True).sum(axis=1, keepdims=True)
        rank_ref[k:k + 1, :] = rk.reshape(1, tt).astype(jnp.int32)
        eidx_ref[k:k + 1, :] = eids[k].reshape(1, tt)
        gate_ref[k:k + 1, :] = (gts[k] / gsum * ROUTED_SCALE).reshape(1, tt)
    base_new = base_ref[...] + self32.sum(axis=1, keepdims=True)
    base_ref[...] = base_new
    cnt_ref[...] = jnp.broadcast_to(base_new, cnt_ref.shape)


def _route(logits_t, bias, tt, tok0, t):
    ne = logits_t.shape[0]
    off = tok0 // tt
    out_i = jax.ShapeDtypeStruct((TOP_K, t), jnp.int32)
    row = pl.BlockSpec((TOP_K, tt), lambda i: (0, i))
    return pl.pallas_call(
        _route_kernel,
        out_shape=(out_i, jax.ShapeDtypeStruct((TOP_K, t), F32), out_i,
                   jax.ShapeDtypeStruct((ne, 128), F32)),
        grid=(t // tt,),
        in_specs=[pl.BlockSpec((ne, tt), lambda i: (0, i + off)),
                  pl.BlockSpec((ne, 1), lambda i: (0, 0))],
        out_specs=(row, row, row, pl.BlockSpec((ne, 128), lambda i: (0, 0))),
        scratch_shapes=[pltpu.VMEM((tt, tt), BF16), pltpu.VMEM((ne, 1), F32)],
        compiler_params=_cparams(("arbitrary",)),
        name="moe_route",
    )(logits_t, bias.reshape(ne, 1))


def _dest_kernel(eidx_ref, rank_ref, start_ref, dest_ref):
    kk, tt = eidx_ref.shape
    ne = start_ref.shape[0]
    n_chunk, _, r = dest_ref.shape
    io_e = lax.broadcasted_iota(jnp.int32, (ne, tt), 0)
    start = start_ref[...]
    for k in range(kk):
        hit = io_e == eidx_ref[k:k + 1, :]
        dk = jnp.where(hit, start, 0).sum(axis=0, keepdims=True) + rank_ref[k:k + 1, :]
        for c in range(n_chunk):
            dest_ref[c, k:k + 1, :] = dk[:, c * r:(c + 1) * r]


def _dest_rows(eidx_t, rank_t, start, tt, r):
    kk, t = eidx_t.shape
    ne = start.shape[0]
    return pl.pallas_call(
        _dest_kernel,
        out_shape=jax.ShapeDtypeStruct((t // r, kk, r), jnp.int32),
        grid=(t // tt,),
        in_specs=[pl.BlockSpec((kk, tt), lambda i: (0, i)),
                  pl.BlockSpec((kk, tt), lambda i: (0, i)),
                  pl.BlockSpec((ne, 1), lambda i: (0, 0))],
        out_specs=pl.BlockSpec((tt // r, kk, r), lambda i: (i, 0, 0)),
        compiler_params=_cparams(("arbitrary",)),
        name="moe_dest",
    )(eidx_t, rank_t, start.reshape(ne, 1))


SC_CHUNK = 64


def _sc_mesh():
    return plsc.VectorSubcoreMesh(core_axis_name="c", subcore_axis_name="s")


def _sc_workers():
    info = plsc.get_sparse_core_info()
    return info.num_cores, info.num_cores * info.num_subcores


def _sc_scatter_rows(rows, dest, n_out, row0=0):
    n_chunk, kk, r = dest.shape
    nc, nw = _sc_workers()
    cpw = n_chunk // nw
    assert cpw * nw == n_chunk and cpw % 2 == 0 and row0 % r == 0 and row0 + n_chunk * r <= rows.shape[0]

    @functools.partial(
        pl.kernel, mesh=_sc_mesh(),
        out_type=jax.ShapeDtypeStruct((n_out,) + rows.shape[1:], rows.dtype),
        scratch_types=[pltpu.VMEM((2, kk, r), jnp.int32), pltpu.VMEM((2, r) + rows.shape[1:], rows.dtype),
                       pltpu.SemaphoreType.DMA((2,)), pltpu.SemaphoreType.DMA((2,))])
    def scatter(rows_hbm, dest_hbm, out_hbm, idx_v, rows_v, load_sem, scat_sem):
        c0 = (lax.axis_index("s") * nc + lax.axis_index("c")) * cpw

        def loads(c, b):
            return (pltpu.make_async_copy(dest_hbm.at[c], idx_v.at[b], load_sem.at[b]),
                    pltpu.make_async_copy(rows_hbm.at[pl.ds(row0 + c * r, r)], rows_v.at[b], load_sem.at[b]))

        def scat(b, k):
            return pltpu.make_async_copy(rows_v.at[b], out_hbm.at[idx_v.at[b, k]], scat_sem.at[b])

        for cp in loads(c0, 0):
            cp.start()

        @pl.loop(0, cpw, step=2)
        def _(ci):
            for b in range(2):
                c = c0 + ci + b
                for cp in loads(c, b):
                    cp.wait()
                for k in range(kk):
                    scat(b, k).start()

                @pl.when(ci + b >= 1)
                def _():
                    for k in range(kk):
                        scat(1 - b, k).wait()

                @pl.when(ci + b + 1 < cpw)
                def _():
                    for cp in loads(c + 1, 1 - b):
                        cp.start()

        for k in range(kk):
            scat((cpw - 1) % 2, k).wait()

    return scatter(rows, dest)


def _sc_gather_rows(src, dest):
    n_chunk, kk, r = dest.shape
    t = n_chunk * r
    nc, nw = _sc_workers()
    cpw = n_chunk // nw
    nbuf = 3
    assert cpw * nw == n_chunk and kk > nbuf

    @functools.partial(
        pl.kernel, mesh=_sc_mesh(),
        out_type=jax.ShapeDtypeStruct((kk, t) + src.shape[1:], src.dtype),
        scratch_types=[pltpu.VMEM((kk, r), jnp.int32), pltpu.VMEM((nbuf, r) + src.shape[1:], src.dtype),
                       pltpu.SemaphoreType.DMA((nbuf,)), pltpu.SemaphoreType.DMA((nbuf,))])
    def gather(src_hbm, dest_hbm, out_hbm, idx_v, rows_v, get_sem, put_sem):
        c0 = (lax.axis_index("s") * nc + lax.axis_index("c")) * cpw

        @pl.loop(0, cpw)
        def _(ci):
            c = c0 + ci
            pltpu.sync_copy(dest_hbm.at[c], idx_v)

            def get(k):
                return pltpu.make_async_copy(src_hbm.at[idx_v.at[k]], rows_v.at[k % nbuf], get_sem.at[k % nbuf])

            def put(k):
                return pltpu.make_async_copy(rows_v.at[k % nbuf], out_hbm.at[k, pl.ds(c * r, r)],
                                             put_sem.at[k % nbuf])

            for k in range(nbuf - 1):
                get(k).start()
            for k in range(kk):
                get(k).wait()
                put(k).start()
                if k + nbuf - 1 < kk:
                    if k >= 1:
                        put(k - 1).wait()
                    get(k + nbuf - 1).start()
            for k in range(kk - nbuf, kk):
                put(k).wait()

    return gather(src, dest)


def _expert_kernel(be_ref, nu_ref, x_ref, wg_ref, wu_ref, wd_ref, o_ref, wgu_s, wd_s):
    i = pl.program_id(0)
    tb = o_ref.shape[0] // PACK_ROWS

    @pl.when(i < nu_ref[0])
    def _():
        @pl.when(jnp.logical_or(i == 0, be_ref[i] != be_ref[jnp.maximum(i - 1, 0)]))
        def _():
            wgu_s[:, :D_EXPERT] = wg_ref[...].astype(BF16)
            wgu_s[:, D_EXPERT:] = wu_ref[...].astype(BF16)
            wd_s[...] = wd_ref[...].astype(BF16)

        x = jnp.concatenate([v.astype(BF16) for v in _unpack_load(x_ref, tb)], axis=-1)
        gu = _dot(x, wgu_s[...])
        g = gu[:, :D_EXPERT]
        h = g * jax.nn.sigmoid(g) * gu[:, D_EXPERT:]
        _pack_store(o_ref, _dot(h.astype(BF16), wd_s[...]))


def _experts(xs, blk_e, n_used, w_gate, w_up, w_down, layer, tb):
    rows = xs.shape[0] // PACK_ROWS
    _, ne, d, de = w_gate.shape
    nb = rows // tb
    row_map = lambda i, be, nu: (jnp.minimum(i, nu[0] - 1), 0)
    w_map = lambda i, be, nu: (layer, be[i], 0, 0)
    grid_spec = pltpu.PrefetchScalarGridSpec(
        num_scalar_prefetch=2,
        grid=(nb,),
        in_specs=[pl.BlockSpec((tb * PACK_ROWS, LANES), row_map),
                  pl.BlockSpec((None, None, d, de), w_map),
                  pl.BlockSpec((None, None, d, de), w_map),
                  pl.BlockSpec((None, None, de, d), w_map)],
        out_specs=pl.BlockSpec((tb * PACK_ROWS, LANES), row_map),
        scratch_shapes=[pltpu.VMEM((d, 2 * de), BF16), pltpu.VMEM((de, d), BF16)],
    )
    return pl.pallas_call(
        _expert_kernel,
        out_shape=jax.ShapeDtypeStruct(xs.shape, U32),
        grid_spec=grid_spec,
        compiler_params=_cparams(("arbitrary",)),
        name="moe_experts",
    )(blk_e, n_used, xs, w_gate, w_up, w_down)


def _combine_kernel(yk_ref, gate_ref, fin_ref, shgu_ref, shd_ref, x1_ref, mod_ref, g3_ref, *rest, fuse_next):
    if fuse_next:
        nmod_ref, ng0_ref = rest[0], rest[1]
        o_ref, h_ref = rest[-2], rest[-1]
    else:
        o_ref = rest[-1]
    tm, d = x1_ref.shape
    for rows in _sub_tiles(tm):
        n, r0 = SUB_ROWS, rows.start
        gates = gate_ref[rows, :]
        blocks = None
        for k in range(TOP_K):
            gk = gates[:, k:k + 1]
            terms = [gk * v for v in _unpack_load(yk_ref, n, lead=(k,), row0=r0)]
            blocks = terms if blocks is None else [a + b for a, b in zip(blocks, terms)]
        fin = jnp.concatenate([v.astype(BF16) for v in _unpack_load(fin_ref, n, row0=r0)], axis=-1)
        gu = _dot(fin, shgu_ref[...])
        g = gu[:, :D_EXPERT]
        hsh = g * jax.nn.sigmoid(g) * gu[:, D_EXPERT:]
        f = jnp.concatenate(blocks, axis=-1) + _dot(hsh.astype(BF16), shd_ref[...])
        x1 = x1_ref[rows, :]
        x2 = x1 + _rows(_mod_chunk(mod_ref, 5, d), x1) * _rms(f, g3_ref[...])
        if len(o_ref.shape) == 2:
            o_ref[rows, :] = x2
        else:
            nb = o_ref.shape[0]
            ts = SUB_ROWS // nb
            o_ref[:, r0 // nb:r0 // nb + ts, :] = jnp.swapaxes(x2.reshape(ts, nb, d), 0, 1)
        if fuse_next:
            hn = (_rms(x2, ng0_ref[...]) * (1.0 + _rows(_mod_chunk(nmod_ref, 1, d), x2))
                  + _rows(_mod_chunk(nmod_ref, 0, d), x2))
            for gi in range(h_ref.shape[0]):
                h_ref[gi, rows, :] = hn[:, gi * LANES:(gi + 1) * LANES]


def _combine(yk, gates, fin, shgu, shd, x1, mods, g3, tm, rows_per_mod, n_tok, x_off, yk_off, fin_off,
             batch_out=0, prev=None, out_rows=None, out_off=None, nxt=None):
    t, d = x1.shape
    out_rows = t if out_rows is None else out_rows
    out_off = x_off if out_off is None else out_off
    xo, yo, fo, oo = x_off // tm, yk_off // tm, fin_off // tm, out_off // tm
    tiles_per_mod = rows_per_mod // tm
    full = lambda a: pl.BlockSpec(a.shape, lambda i: (0,) * a.ndim)
    if batch_out:
        out_shape = [jax.ShapeDtypeStruct((batch_out, out_rows // batch_out, d), F32)]
        out_specs = [pl.BlockSpec((batch_out, tm // batch_out, d), lambda i: (0, i + oo, 0))]
    else:
        out_shape = [jax.ShapeDtypeStruct((out_rows, d), F32)]
        out_specs = [pl.BlockSpec((tm, d), lambda i: (i + oo, 0))]
    in_specs = [pl.BlockSpec((TOP_K, tm * PACK_ROWS, LANES), lambda i: (0, i + yo, 0)),
                pl.BlockSpec((tm, TOP_K), lambda i: (i + yo, 0)),
                pl.BlockSpec((tm * PACK_ROWS, LANES), lambda i: (i + fo, 0)),
                full(shgu), full(shd),
                pl.BlockSpec((tm, d), lambda i: (i + xo, 0)),
                pl.BlockSpec((None,) + mods.shape[1:], lambda i: ((i + xo) // tiles_per_mod, 0, 0)),
                full(g3)]
    args = [yk, gates, fin, shgu, shd, x1, mods, g3]
    if nxt is not None:
        in_specs += [full(nxt[0]), full(nxt[1])]
        args += list(nxt)
        out_shape.append(jax.ShapeDtypeStruct((d // LANES, out_rows, LANES), F32))
        out_specs.append(pl.BlockSpec((d // LANES, tm, LANES), lambda i: (0, i + oo, 0)))
    aliases = {}
    if prev is not None:
        for j, p in enumerate(prev if isinstance(prev, (tuple, list)) else [prev]):
            in_specs.append(pl.BlockSpec(memory_space=pl.ANY))
            aliases[len(args)] = j
            args.append(p)
    out = pl.pallas_call(
        functools.partial(_combine_kernel, fuse_next=nxt is not None),
        out_shape=tuple(out_shape),
        grid=(n_tok // tm,),
        in_specs=in_specs,
        out_specs=tuple(out_specs),
        input_output_aliases=aliases,
        compiler_params=_cparams(("arbitrary",)),
        name="moe_combine",
    )(*args)
    return out if nxt is not None else out[0]


def _moe(fin, logits_t, bias, w_gate, w_up, w_down, layer, tb, tok0, t):
    t_all = fin.shape[0] // PACK_ROWS
    ne = w_gate.shape[1]
    tt = 512
    eidx_t, gates_t, rank_t, cnt = _route(logits_t, bias, tt, tok0, t)
    counts = cnt[:, 0].astype(jnp.int32)
    padded = (counts + tb - 1) // tb * tb
    pad_end = jnp.cumsum(padded)
    pad_start = pad_end - padded
    nb = (t * TOP_K) // tb + ne
    n_used = pad_end[-1] // tb
    blk_start = jnp.arange(nb, dtype=jnp.int32) * tb
    blk = jnp.sum(pad_end[None, :] <= jnp.minimum(blk_start, pad_end[-1] - 1)[:, None], axis=1)
    blk_e = jnp.minimum(blk, ne - 1).astype(jnp.int32)
    dest = _dest_rows(eidx_t, rank_t, pad_start, tt, SC_CHUNK)
    xs = _sc_scatter_rows(fin.reshape(t_all, PACK_ROWS, LANES), dest, nb * tb, row0=tok0)
    ys = _experts(xs.reshape(nb * tb * PACK_ROWS, LANES), blk_e, n_used.reshape(1).astype(jnp.int32),
                  w_gate, w_up, w_down, layer, tb)
    yk = _sc_gather_rows(ys.reshape(nb * tb, PACK_ROWS, LANES), dest)
    return yk.reshape(TOP_K, t * PACK_ROWS, LANES), gates_t.T


def _pre_s5_kernel(x_ref, mod_ref, g0_ref, *refs):
    h_ref, xt_ref = refs[-2:]
    nb, tt, d = x_ref.shape
    x = jnp.swapaxes(x_ref[...], 0, 1).reshape(tt * nb, d)
    h = (_rms(x, g0_ref[...]) * (1.0 + _rows(_mod_chunk(mod_ref, 1, d), x))
         + _rows(_mod_chunk(mod_ref, 0, d), x))
    for g in range(h_ref.shape[0]):
        h_ref[g] = h[:, g * LANES:(g + 1) * LANES]
    xt_ref[...] = x


def _pre_s5(x, mods, g0, n_total, t_off, prev, tt):
    nb, n, d = x.shape
    off = t_off // tt
    out_shape = (jax.ShapeDtypeStruct((d // LANES, n_total * nb, LANES), F32),
                 jax.ShapeDtypeStruct((n_total * nb, d), F32))
    out_specs = (pl.BlockSpec((d // LANES, tt * nb, LANES), lambda i: (0, i + off, 0)),
                 pl.BlockSpec((tt * nb, d), lambda i: (i + off, 0)))
    in_specs = [pl.BlockSpec((nb, tt, d), lambda i: (0, i, 0)),
                pl.BlockSpec(mods.shape, lambda i: (0, 0)),
                pl.BlockSpec(g0.shape, lambda i: (0, 0))]
    args = (x, mods, g0)
    aliases = {}
    if prev is not None:
        in_specs += [pl.BlockSpec(memory_space=pl.ANY)] * 2
        args += tuple(prev)
        aliases = {3: 0, 4: 1}
    return pl.pallas_call(
        _pre_s5_kernel,
        out_shape=out_shape,
        grid=(n // tt,),
        in_specs=in_specs,
        out_specs=out_specs,
        input_output_aliases=aliases,
        compiler_params=_cparams(("arbitrary",)),
        name="pre_s5",
    )(*args)


def _s5_scan_kernel(h_ref, bm_ref, cm_ref, lam_ref, y_ref, bu0, bu1, xb0, xb1, st_ref):
    first = jnp.logical_and(jnp.logical_and(pl.program_id(0) == 0, pl.program_id(1) == 0), pl.program_id(2) == 0)
    dr = pl.program_id(1)
    s = pl.program_id(2)
    tc, nb, cw = h_ref.shape
    half = st_ref.shape[1] // 2

    @pl.when(first)
    def _():
        for r in (bu0, bu1, xb0, xb1, st_ref):
            r[...] = jnp.zeros_like(r)

    def stages(bu_w, bu_r, xb_w, xb_r):
        lr = jnp.broadcast_to(lam_ref[0:1, :], (nb, half))
        li = jnp.broadcast_to(lam_ref[1:2, :], (nb, half))
        fresh = s == 1
        xr = jnp.where(fresh, 0.0, st_ref[:, 0:half])
        xi = jnp.where(fresh, 0.0, st_ref[:, half:2 * half])
        tp = tc // S5_PIECES
        for p in range(S5_PIECES):
            for i in range(p * tp, (p + 1) * tp):
                t = i + dr * (tc - 1 - 2 * i)
                rows = pl.ds(pl.multiple_of(t * nb, nb), nb)
                nr = lr * xr - li * xi + bu_r[rows, 0:half]
                ni = lr * xi + li * xr + bu_r[rows, half:2 * half]
                xb_w[rows, 0:half] = nr.astype(BF16)
                xb_w[rows, half:2 * half] = ni.astype(BF16)
                xr, xi = nr, ni
            ts = slice(p * tp, (p + 1) * tp)
            mr = slice(p * tp * nb, (p + 1) * tp * nb)
            y_ref[ts] = _dot(xb_r[mr, :], cm_ref[...]).reshape(tp, nb, cw)
            bu_w[mr, :] = _dot(h_ref[ts].reshape(tp * nb, cw).astype(BF16), bm_ref[...])
        st_ref[:, 0:half] = xr
        st_ref[:, half:2 * half] = xi

    @pl.when(s % 2 == 0)
    def _():
        stages(bu0, bu1, xb1, xb0)

    @pl.when(s % 2 == 1)
    def _():
        stages(bu1, bu0, xb0, xb1)


def _s5_scan(h_all, n_ctx, bm, cm, lam, tc):
    ng, nt, nb, cw = h_all.shape
    nl = nt - n_ctx
    assert cw == S5_GROUPS_PER_BLOCK * S5_GROUP
    sw = 2 * S5_GROUPS_PER_BLOCK * S5_STATE
    ncc, n = n_ctx // tc, nt // tc

    def chunk(dr, j):
        j = jnp.clip(j, 0, n - 1)
        rev = jnp.where(j < ncc, ncc - 1 - j, n - 1 - (j - ncc))
        return jnp.where(dr == 0, j, rev)

    def out_map(g, dr, s):
        return (dr, g, chunk(dr, jnp.clip(s - 2, ncc, n - 1)) - ncc, 0, 0)

    return pl.pallas_call(
        _s5_scan_kernel,
        out_shape=jax.ShapeDtypeStruct((2, ng, nl, nb, cw), F32),
        grid=(ng, 2, n + 2),
        in_specs=[pl.BlockSpec((None, tc, nb, cw), lambda g, dr, s: (g, chunk(dr, s), 0, 0)),
                  pl.BlockSpec((None, None, cw, sw), lambda g, dr, s: (dr, g, 0, 0)),
                  pl.BlockSpec((None, None, sw, cw), lambda g, dr, s: (dr, g, 0, 0)),
                  pl.BlockSpec((None, None, 2, sw // 2), lambda g, dr, s: (dr, g, 0, 0))],
        out_specs=pl.BlockSpec((None, None, tc, nb, cw), out_map),
        scratch_shapes=[pltpu.VMEM((tc * nb, sw), F32), pltpu.VMEM((tc * nb, sw), F32),
                        pltpu.VMEM((tc * nb, sw), BF16), pltpu.VMEM((tc * nb, sw), BF16),
                        pltpu.VMEM((nb, sw), F32)],
        compiler_params=_cparams(("arbitrary", "arbitrary", "arbitrary")),
        name="s5_scan",
    )(h_all, bm, cm, lam)


def _s5_params(lam_re, lam_im, log_step, b_re, b_im, c_re, c_im):
    g, p = lam_re.shape[1:]
    gb = S5_GROUPS_PER_BLOCK
    nblk = g // gb
    step = jnp.exp(log_step)[..., None]
    mag = jnp.exp(lam_re * step)
    lb_re = mag * jnp.cos(lam_im * step)
    lb_im = mag * jnp.sin(lam_im * step)
    den = lam_re * lam_re + lam_im * lam_im
    f_re = ((lb_re - 1.0) * lam_re + lb_im * lam_im) / den
    f_im = (lb_im * lam_re - (lb_re - 1.0) * lam_im) / den
    bb_re = f_re[..., None] * b_re - f_im[..., None] * b_im
    bb_im = f_re[..., None] * b_im + f_im[..., None] * b_re
    eye = jnp.eye(gb, dtype=F32)

    def in_map(w):
        w = w.reshape(2, nblk, gb, p, S5_GROUP)
        return jnp.einsum("dnapi,ab->dnaibp", w, eye).reshape(2, nblk, gb * S5_GROUP, gb * p)

    def out_map(w):
        w = w.reshape(2, nblk, gb, S5_GROUP, p)
        return jnp.einsum("dnaip,ab->dnapbi", w, eye).reshape(2, nblk, gb * p, gb * S5_GROUP)

    bm = jnp.concatenate([in_map(bb_re), in_map(bb_im)], axis=-1).astype(BF16)
    cm = jnp.concatenate([out_map(c_re), out_map(-c_im)], axis=-2).astype(BF16)
    lam = jnp.stack([lb_re.reshape(2, nblk, gb * p), lb_im.reshape(2, nblk, gb * p)], axis=2)
    return bm, cm, lam


def _rope_tables(n_tokens):
    rows = n_tokens // GRID_W
    row = jnp.repeat(jnp.arange(rows), GRID_W).astype(F32)
    col = jnp.tile(jnp.arange(GRID_W), rows).astype(F32)
    n_freq = ROPE_DIM // 4
    inv_freq = ROPE_BASE ** (-jnp.arange(n_freq, dtype=F32) / n_freq)
    ang = jnp.concatenate([row[:, None] * inv_freq, col[:, None] * inv_freq], axis=-1)
    cos, sin = jnp.cos(ang), jnp.sin(ang)
    z = jnp.zeros((n_tokens, 128 - ROPE_DIM), F32)
    return (jnp.concatenate([cos, cos, z], axis=-1), jnp.concatenate([-sin, sin, z], axis=-1))


def _router_halves(w_router):
    wt = w_router.T
    hi = wt.astype(BF16)
    lo = (wt - hi.astype(F32)).astype(BF16)
    return jnp.concatenate([hi, lo], axis=0)


def _split_pairs(w):
    ev, od = w[..., 0::2], w[..., 1::2]
    z = jnp.zeros(w.shape[:-1] + (128 - ROPE_DIM,), w.dtype)
    return jnp.concatenate([ev, od, z], axis=-1), jnp.concatenate([od, ev, z], axis=-1)


def _mla_weights(w_dqkv, w_uq, w_ukv):
    kp, kps = _split_pairs(w_dqkv[:, Q_LORA + KV_LORA:])
    wd = jnp.concatenate([w_dqkv[:, :Q_LORA + KV_LORA], kp, kps], axis=-1).astype(BF16)
    wq3 = w_uq.reshape(Q_LORA, MLA_HEADS, NOPE_DIM + ROPE_DIM)
    qp, qps = _split_pairs(wq3[:, :, NOPE_DIM:])
    wq = jnp.concatenate([wq3[:, :, :NOPE_DIM].reshape(Q_LORA, -1), qp.reshape(Q_LORA, -1),
                          qps.reshape(Q_LORA, -1)], axis=-1).astype(BF16)
    wkv3 = w_ukv.reshape(KV_LORA, MLA_HEADS, NOPE_DIM + V_DIM)
    wkv = jnp.concatenate([wkv3[:, :, :NOPE_DIM].reshape(KV_LORA, -1),
                           wkv3[:, :, NOPE_DIM:].reshape(KV_LORA, -1)], axis=-1).astype(BF16)
    return wd, wq, wkv


@jax.jit
def kernel(x, c, ctx, c_ctx, ada_w, ada_b, norm_g, mla_w_dqkv, mla_g_q, mla_g_kv, mla_w_uq, mla_w_ukv, mla_w_o, s5_lam_re, s5_lam_im, s5_log_step, s5_b_re, s5_b_im, s5_c_re, s5_c_im, s5_d, s5_w_glu, s5_b_glu, moe_w_router, moe_bias, moe_w_gate, moe_w_up, moe_w_down, sh_w_gate, sh_w_up, sh_w_down):
    b, l, d = x.shape
    n_ctx = ctx.shape[1]
    assert ada_w.shape[0] == 2 and b % 8 == 0
    ta = 256
    tm = 512
    tb = 512
    row = lambda v: v.reshape(1, -1)

    n_rows = (b + 1 + 7) // 8 * 8
    cvec = jnp.zeros((n_rows, d), F32).at[:b].set(c).at[b].set(c_ctx)
    mods = _ada_mods(cvec, ada_w, ada_b)

    def shared_weights(i):
        shgu = jnp.concatenate([sh_w_gate[i], sh_w_up[i]], axis=-1).astype(BF16)
        return shgu, sh_w_down[i].astype(BF16)

    mod_lat = mods[0, :b].reshape(b, 1, N_MOD * d)
    mod_ctx = mods[0, b].reshape(1, 1, N_MOD * d)
    wd, wq, wkv = _mla_weights(mla_w_dqkv[0], mla_w_uq[0], mla_w_ukv[0])
    cos_l, sin_l = _rope_tables(l)
    cos_c = jnp.concatenate([jnp.ones((n_ctx, ROPE_DIM), F32), jnp.zeros((n_ctx, 128 - ROPE_DIM), F32)], -1)
    sin_c = jnp.zeros((n_ctx, 128), F32)
    pre = functools.partial(_pre_mla, g0=row(norm_g[0, 0]), wd=wd, gq=row(mla_g_q[0]), gkv=row(mla_g_kv[0]),
                            wq=wq, wkv=wkv, tm=ta)
    q_c, k_c, v_c = pre(ctx, mod_ctx, cos_t=cos_c, sin_t=sin_c)
    q_l, k_l, v_l = pre(x, mod_lat, cos_t=cos_l, sin_t=sin_l)
    o_l = _attention(q_l, [k_c, k_l], [v_c, v_l], ta)
    o_c = _attention(q_c, [k_c], [v_c], n_ctx)

    wo = mla_w_o[0].astype(BF16)
    wr_t = _router_halves(moe_w_router[0])
    g1, g2, g3 = row(norm_g[0, 1]), row(norm_g[0, 2]), row(norm_g[0, 3])
    post = functools.partial(_post_mixer, _post_proj_kernel, consts=[wo], g1=g1, g2=g2, wr_t=wr_t, tm=tm,
                             name="post_mla")
    o_spec = pl.BlockSpec((tm, o_l.shape[-1]), lambda i: (i, 0))
    n_moe = b * (n_ctx + l)
    x1_c, fin, lg = post([(o_c.reshape(b * n_ctx, -1), o_spec)], x=ctx.reshape(b * n_ctx, d),
                         n_tok=b * n_ctx, x_off=0, mods=mod_ctx, rows_per_mod=b * n_ctx, moe_total=n_moe)
    tt = tm // b
    mod_lat_tm = mods[0, :b][None]
    x1_l, fin, lg = _post_mixer(
        _post_proj_tm_kernel, [(o_l, pl.BlockSpec((b, tt, o_l.shape[-1]), lambda i: (0, i, 0)))], [wo],
        x=x, n_tok=b * l, x_off=0, mods=mod_lat_tm, g1=g1, g2=g2, wr_t=wr_t, tm=tm, rows_per_mod=b * l,
        name="post_mla", moe_total=n_moe, moe_off=b * n_ctx, prev=(fin, lg),
        x_spec=pl.BlockSpec((b, tt, d), lambda i: (0, i, 0)))
    shgu, shd = shared_weights(0)
    lat_a = (l // 2) * b
    n_a = b * n_ctx + lat_a
    moe = functools.partial(_moe, fin, lg, moe_bias[0], moe_w_gate, moe_w_up, moe_w_down, 0, tb)
    yk_a, gates_a = moe(0, n_a)
    yk_b, gates_b = moe(n_a, n_moe - n_a)
    comb = functools.partial(_combine, fin=fin, shgu=shgu, shd=shd, g3=g3, tm=tm)
    x2_c = comb(yk_a, gates_a, x1=x1_c, mods=mod_ctx, rows_per_mod=b * n_ctx, n_tok=b * n_ctx, x_off=0, yk_off=0,
                fin_off=0)

    n_all = n_ctx + l
    mod_lat = mods[1, :b]
    mod_ctx = jnp.broadcast_to(mods[1, b][None], (b, N_MOD * d))
    g0 = row(norm_g[1, 0])
    h, xt = _pre_s5(x2_c.reshape(b, n_ctx, d), mod_ctx, g0, n_all, 0, None, tt)
    comb_l = functools.partial(comb, x1=x1_l, mods=mod_lat_tm, rows_per_mod=b * l, out_rows=n_all * b,
                               nxt=(mod_lat, g0))
    xt, h = comb_l(yk_a, gates_a, n_tok=lat_a, x_off=0, yk_off=b * n_ctx, fin_off=b * n_ctx,
                   out_off=n_ctx * b, prev=(xt, h))
    xt, h = comb_l(yk_b, gates_b, n_tok=b * l - lat_a, x_off=lat_a, yk_off=0, fin_off=n_a,
                   out_off=n_ctx * b + lat_a, prev=(xt, h))
    bm, cm, lam = _s5_params(s5_lam_re[0], s5_lam_im[0], s5_log_step[0], s5_b_re[0], s5_b_im[0],
                             s5_c_re[0], s5_c_im[0])
    ng = d // LANES
    y = _s5_scan(h.reshape(ng, n_all, b, LANES), n_ctx, bm, cm, lam, 64).reshape(2, ng, l * b, LANES)
    g1, g2, g3 = row(norm_g[1, 1]), row(norm_g[1, 2]), row(norm_g[1, 3])
    lat0 = n_ctx * b // tm
    x1, fin, lg = _post_mixer(
        _post_glu_kernel,
        [(h, pl.BlockSpec((ng, tm, LANES), lambda i: (0, i + lat0, 0))),
         (y, pl.BlockSpec((None, ng, tm, LANES), lambda i: (0, 0, i, 0))),
         (y, pl.BlockSpec((None, ng, tm, LANES), lambda i: (1, 0, i, 0)))],
        [row(s5_d[0]), s5_w_glu[0].astype(BF16), row(s5_b_glu[0])],
        x=xt, n_tok=l * b, x_off=n_ctx * b, mods=mod_lat[None], g1=g1, g2=g2, wr_t=_router_halves(moe_w_router[1]), tm=tm,
        rows_per_mod=l * b, name="post_s5")
    shgu, shd = shared_weights(1)
    n_h = (l // 2) * b
    moe = functools.partial(_moe, fin, lg, moe_bias[1], moe_w_gate, moe_w_up, moe_w_down, 1, tb)
    yk_a, gates_a = moe(0, n_h)
    yk_b, gates_b = moe(n_h, l * b - n_h)
    comb = functools.partial(_combine, fin=fin, shgu=shgu, shd=shd, x1=x1, mods=mod_lat[None], g3=g3, tm=tm,
                             rows_per_mod=l * b, yk_off=0, batch_out=b)
    out = comb(yk_a, gates_a, n_tok=n_h, x_off=0, fin_off=0)
    return comb(yk_b, gates_b, n_tok=l * b - n_h, x_off=n_h, fin_off=n_h, prev=out)
```

```python
import functools

import jax
import jax.numpy as jnp
from jax import lax
from jax.experimental import pallas as pl
from jax.experimental.pallas import tpu as pltpu
from jax.experimental.pallas import tpu_sc as plsc

F32 = jnp.float32
BF16 = jnp.bfloat16
U32 = jnp.uint32

N_MOD = 6
NORM_EPS = 1e-6
LOG2_E = 1.4426950408889634
GRID_W = 64
MLA_HEADS = 8
Q_LORA = 384
KV_LORA = 256
NOPE_DIM = 128
ROPE_DIM = 64
V_DIM = 128
V_PAD = 256
ROPE_BASE = 10000.0
QK_PAD = 256
S5_GROUP = 16
S5_STATE = 64
S5_GROUPS_PER_BLOCK = 8
S5_PIECES = 4
N_EXPERTS = 64
TOP_K = 8
N_EXPERT_GROUPS = 8
TOPK_GROUPS = 4
D_EXPERT = 256
ROUTED_SCALE = 2.5

VMEM_LIMIT = 56 * 1024 * 1024


def _cparams(sem):
    return pltpu.CompilerParams(dimension_semantics=sem, vmem_limit_bytes=VMEM_LIMIT)


def _rms(x, g):
    return x * lax.rsqrt(jnp.mean(x * x, axis=-1, keepdims=True) + NORM_EPS) * g


def _rows(v, like):
    r = v.shape[0]
    if r == 1:
        return v
    tm, d = like.shape
    return jnp.broadcast_to(v[None], (tm // r, r, d)).reshape(tm, d)


def _mod_chunk(mod_ref, j, d):
    return mod_ref[:, j * d:(j + 1) * d]


def _dot(a, b):
    return jnp.dot(a, b, preferred_element_type=F32)


PACK_ROWS = 4
LANES = 128


def _pack_store(ref, val, lead=(), row0=0):
    n = val.shape[0]
    bits = lax.bitcast_convert_type(val.astype(BF16).astype(F32), U32)
    for s in range(PACK_ROWS):
        lo = bits[:, s * LANES:(s + 1) * LANES] >> 16
        hi = bits[:, (s + PACK_ROWS) * LANES:(s + PACK_ROWS + 1) * LANES] & jnp.uint32(0xFFFF0000)
        ref[lead + (pl.ds(row0 * PACK_ROWS + s, n, stride=PACK_ROWS), slice(None))] = lo | hi


def _unpack_load(ref, n, lead=(), row0=0):
    los, his = [], []
    for s in range(PACK_ROWS):
        w = ref[lead + (pl.ds(row0 * PACK_ROWS + s, n, stride=PACK_ROWS), slice(None))]
        los.append(lax.bitcast_convert_type(w << 16, F32))
        his.append(lax.bitcast_convert_type(w & jnp.uint32(0xFFFF0000), F32))
    return los + his


def _ada_kernel(c_ref, w_ref, b_ref, o_ref):
    c = c_ref[...]
    s = c * jax.nn.sigmoid(c)
    o_ref[...] = jnp.dot(s, w_ref[...], preferred_element_type=F32,
                         precision=lax.Precision.HIGHEST) + b_ref[...]


def _ada_mods(cvec, ada_w, ada_b):
    depth, d, n = ada_w.shape
    rows = cvec.shape[0]
    tn = 1536
    return pl.pallas_call(
        _ada_kernel,
        out_shape=jax.ShapeDtypeStruct((depth, rows, n), F32),
        grid=(depth, n // tn),
        in_specs=[pl.BlockSpec((rows, d), lambda l, j: (0, 0)),
                  pl.BlockSpec((None, d, tn), lambda l, j: (l, 0, j)),
                  pl.BlockSpec((None, 1, tn), lambda l, j: (l, 0, j))],
        out_specs=pl.BlockSpec((None, rows, tn), lambda l, j: (l, 0, j)),
        compiler_params=_cparams(("arbitrary", "arbitrary")),
        name="ada_mods",
    )(cvec, ada_w, ada_b.reshape(depth, 1, n))


def _pre_mla_kernel(x_ref, mod_ref, g0_ref, wd_ref, gq_ref, gkv_ref, wq_ref, wkv_ref, cos_ref, sin_ref,
                    q_ref, k_ref, v_ref):
    d = x_ref.shape[-1]
    x = x_ref[...]
    h = _rms(x, g0_ref[...]) * (1.0 + _mod_chunk(mod_ref, 1, d)) + _mod_chunk(mod_ref, 0, d)
    a = _dot(h.astype(BF16), wd_ref[...])
    cq = _rms(a[:, :Q_LORA], gq_ref[...])
    ckv = _rms(a[:, Q_LORA:Q_LORA + KV_LORA], gkv_ref[...])
    cos = cos_ref[...]
    sin = sin_ref[...]
    o = Q_LORA + KV_LORA
    k_rot = (a[:, o:o + 128] * cos + a[:, o + 128:o + 256] * sin).astype(BF16)
    qa = _dot(cq.astype(BF16), wq_ref[...])
    kva = _dot(ckv.astype(BF16), wkv_ref[...])
    hw = MLA_HEADS * 128
    scale = (NOPE_DIM + ROPE_DIM) ** -0.5 * LOG2_E
    for hd in range(MLA_HEADS):
        lo = hd * 128
        q_rot = qa[:, hw + lo:hw + lo + 128] * cos + qa[:, 2 * hw + lo:2 * hw + lo + 128] * sin
        q_ref[:, hd * QK_PAD:hd * QK_PAD + 128] = (qa[:, lo:lo + 128] * scale).astype(BF16)
        q_ref[:, hd * QK_PAD + 128:(hd + 1) * QK_PAD] = (q_rot * scale).astype(BF16)
        k_ref[:, hd * QK_PAD:hd * QK_PAD + 128] = kva[:, lo:lo + 128].astype(BF16)
        k_ref[:, hd * QK_PAD + 128:(hd + 1) * QK_PAD] = k_rot
        v_ref[:, hd * V_PAD:hd * V_PAD + V_DIM] = kva[:, hw + lo:hw + lo + 128].astype(BF16)
        v_ref[:, hd * V_PAD + V_DIM:(hd + 1) * V_PAD] = jnp.ones((x.shape[0], V_PAD - V_DIM), BF16)


def _pre_mla(x, mods, g0, wd, gq, gkv, wq, wkv, cos_t, sin_t, tm):
    b, n, d = x.shape
    nb_mod = mods.shape[0]
    full = lambda a: pl.BlockSpec(a.shape, lambda i, j: (0,) * a.ndim)
    mod_map = (lambda i, j: (i, 0, 0)) if nb_mod > 1 else (lambda i, j: (0, 0, 0))
    qk_w = MLA_HEADS * QK_PAD
    v_w = MLA_HEADS * V_PAD
    return pl.pallas_call(
        _pre_mla_kernel,
        out_shape=(jax.ShapeDtypeStruct((b, n, qk_w), BF16),
                   jax.ShapeDtypeStruct((b, n, qk_w), BF16),
                   jax.ShapeDtypeStruct((b, n, v_w), BF16)),
        grid=(b, n // tm),
        in_specs=[pl.BlockSpec((None, tm, d), lambda i, j: (i, j, 0)),
                  pl.BlockSpec((None, 1, mods.shape[-1]), mod_map),
                  full(g0), full(wd), full(gq), full(gkv), full(wq), full(wkv),
                  pl.BlockSpec((tm, 128), lambda i, j: (j, 0)),
                  pl.BlockSpec((tm, 128), lambda i, j: (j, 0))],
        out_specs=(pl.BlockSpec((None, tm, qk_w), lambda i, j: (i, j, 0)),
                   pl.BlockSpec((None, tm, qk_w), lambda i, j: (i, j, 0)),
                   pl.BlockSpec((None, tm, v_w), lambda i, j: (i, j, 0))),
        compiler_params=_cparams(("arbitrary", "arbitrary")),
        name="pre_mla",
    )(x, mods, g0, wd, gq, gkv, wq, wkv, cos_t, sin_t)


def _attn_kernel(*refs, n_seg):
    q_ref = refs[0]
    k_refs = refs[1:1 + n_seg]
    v_refs = refs[1 + n_seg:1 + 2 * n_seg]
    o_ref = refs[1 + 2 * n_seg]
    nt = (((1,), (1,)), ((), ()))

    def scores(hd):
        q = q_ref[:, hd * QK_PAD:(hd + 1) * QK_PAD]
        return [lax.dot_general(q, k[:, hd * QK_PAD:(hd + 1) * QK_PAD], nt, preferred_element_type=F32)
                for k in k_refs]

    nxt = scores(0)
    for hd in range(MLA_HEADS):
        ss = nxt
        if hd + 1 < MLA_HEADS:
            nxt = scores(hd + 1)
        m = ss[0].max(axis=-1, keepdims=True)
        for s in ss[1:]:
            m = jnp.maximum(m, s.max(axis=-1, keepdims=True))
        acc = None
        for s, v in zip(ss, v_refs):
            pv = _dot(jnp.exp2((s - m).astype(BF16)), v[:, hd * V_PAD:(hd + 1) * V_PAD])
            acc = pv if acc is None else acc + pv
        o_ref[:, hd * V_DIM:(hd + 1) * V_DIM] = (acc[:, :V_DIM] / acc[:, V_DIM:V_DIM + 1]).astype(BF16)


def _attention(q, ks, vs, tq):
    b, nq, qk_w = q.shape
    v_w = MLA_HEADS * V_DIM
    kv_spec = lambda a: pl.BlockSpec((None,) + a.shape[1:], lambda i, j: (i, 0, 0))
    return pl.pallas_call(
        functools.partial(_attn_kernel, n_seg=len(ks)),
        out_shape=jax.ShapeDtypeStruct((b, nq, v_w), BF16),
        grid=(b, nq // tq),
        in_specs=[pl.BlockSpec((None, tq, qk_w), lambda i, j: (i, j, 0))]
                 + [kv_spec(a) for a in ks] + [kv_spec(a) for a in vs],
        out_specs=pl.BlockSpec((None, tq, v_w), lambda i, j: (i, j, 0)),
        compiler_params=_cparams(("arbitrary", "arbitrary")),
        name="mla_attention",
    )(q, *ks, *vs)


SUB_ROWS = 256


def _sub_tiles(n):
    return [slice(r, r + SUB_ROWS) for r in range(0, n, SUB_ROWS)]


def _post_core(o, x, rows, mod_ref, g1_ref, g2_ref, wr_ref, x1_ref, fin_ref, lg_ref):
    d = x.shape[-1]
    ne = lg_ref.shape[0]
    gate = _rows(_mod_chunk(mod_ref, 2, d), x)
    shift = _rows(_mod_chunk(mod_ref, 3, d), x)
    scale = _rows(_mod_chunk(mod_ref, 4, d), x)
    x1 = x + gate * _rms(o, g1_ref[...])
    fin = _rms(x1, g2_ref[...]) * (1.0 + scale) + shift
    x1_ref[rows, :] = x1
    _pack_store(fin_ref, fin, row0=rows.start)
    nt = (((1,), (1,)), ((), ()))
    f_hi = fin.astype(BF16)
    f_lo = (fin - f_hi.astype(F32)).astype(BF16)
    r_hi = lax.dot_general(wr_ref[...], f_hi, nt, preferred_element_type=F32)
    r_lo = lax.dot_general(wr_ref[0:ne, :], f_lo, nt, preferred_element_type=F32)
    lg_ref[:, rows] = r_hi[:ne] + r_hi[ne:] + r_lo


def _post_proj_kernel(o_ref, wo_ref, x_ref, mod_ref, g1_ref, g2_ref, wr_ref, *rest):
    x1_ref, fin_ref, lg_ref = rest[-3:]
    for rows in _sub_tiles(x_ref.shape[0]):
        o = _dot(o_ref[rows, :], wo_ref[...])
        _post_core(o, x_ref[rows, :], rows, mod_ref, g1_ref, g2_ref, wr_ref, x1_ref, fin_ref, lg_ref)


def _post_proj_tm_kernel(o_ref, wo_ref, x_ref, mod_ref, g1_ref, g2_ref, wr_ref, *rest):
    x1_ref, fin_ref, lg_ref = rest[-3:]
    nb, tt, d = x_ref.shape
    ts = SUB_ROWS // nb
    for t0 in range(0, tt, ts):
        o = _dot(o_ref[:, t0:t0 + ts, :].reshape(nb * ts, o_ref.shape[-1]), wo_ref[...])
        o = jnp.swapaxes(o.reshape(nb, ts, d), 0, 1).reshape(ts * nb, d)
        x = jnp.swapaxes(x_ref[:, t0:t0 + ts, :], 0, 1).reshape(ts * nb, d)
        _post_core(o, x, slice(t0 * nb, (t0 + ts) * nb), mod_ref, g1_ref, g2_ref, wr_ref, x1_ref, fin_ref, lg_ref)


def _post_glu_kernel(h_ref, yf_ref, yb_ref, dsk_ref, wg_ref, bg_ref, x_ref, mod_ref, g1_ref, g2_ref, wr_ref,
                     *rest):
    x1_ref, fin_ref, lg_ref = rest[-3:]
    d = x_ref.shape[-1]
    for rows in _sub_tiles(x_ref.shape[0]):
        wide = lambda r: jnp.concatenate([r[g, rows, :] for g in range(r.shape[0])], axis=-1)
        y = wide(h_ref) * dsk_ref[...] + wide(yf_ref) + wide(yb_ref)
        z = _dot(jax.nn.gelu(y, approximate=True).astype(BF16), wg_ref[...]) + bg_ref[...]
        o = z[:, :d] * jax.nn.sigmoid(z[:, d:])
        _post_core(o, x_ref[rows, :], rows, mod_ref, g1_ref, g2_ref, wr_ref, x1_ref, fin_ref, lg_ref)


def _post_mixer(kernel, tok_inputs, consts, x, n_tok, x_off, mods, g1, g2, wr_t, tm, rows_per_mod, name,
                moe_total=None, moe_off=0, prev=None, x_spec=None):
    d = x.shape[-1]
    ne = wr_t.shape[0] // 2
    moe_total = n_tok if moe_total is None else moe_total
    tiles_per_mod = rows_per_mod // tm
    xo, mo = x_off // tm, moe_off // tm
    full = lambda a: pl.BlockSpec(a.shape, lambda i: (0,) * a.ndim)
    tile = pl.BlockSpec((tm, d), lambda i: (i, 0))
    mod_spec = pl.BlockSpec((None,) + mods.shape[1:], lambda i: (i // tiles_per_mod, 0, 0))
    x_spec = pl.BlockSpec((tm, d), lambda i: (i + xo, 0)) if x_spec is None else x_spec
    in_specs = ([spec for _, spec in tok_inputs] + [full(a) for a in consts]
                + [x_spec, mod_spec, full(g1), full(g2), full(wr_t)])
    args = [a for a, _ in tok_inputs] + list(consts) + [x, mods, g1, g2, wr_t]
    aliases = {}
    if prev is not None:
        aliases = {len(args): 1, len(args) + 1: 2}
        in_specs += [pl.BlockSpec(memory_space=pl.ANY)] * 2
        args += list(prev)
    return pl.pallas_call(
        kernel,
        out_shape=(jax.ShapeDtypeStruct((n_tok, d), F32),
                   jax.ShapeDtypeStruct((moe_total * PACK_ROWS, LANES), U32),
                   jax.ShapeDtypeStruct((ne, moe_total), F32)),
        grid=(n_tok // tm,),
        in_specs=in_specs,
        out_specs=(tile, pl.BlockSpec((tm * PACK_ROWS, LANES), lambda i: (i + mo, 0)),
                   pl.BlockSpec((ne, tm), lambda i: (0, i + mo))),
        input_output_aliases=aliases,
        compiler_params=_cparams(("arbitrary",)),
        name=name,
    )(*args)


def _route_kernel(lg_ref, bias_ref, eidx_ref, gate_ref, rank_ref, cnt_ref, tri_ref, base_ref):
    i = pl.program_id(0)
    ne, tt = lg_ref.shape
    gsz = ne // N_EXPERT_GROUPS
    shp = (N_EXPERT_GROUPS, gsz, tt)
    neg = -jnp.inf

    @pl.when(i == 0)
    def _():
        base_ref[...] = jnp.zeros_like(base_ref)
        r = lax.broadcasted_iota(jnp.int32, (tt, tt), 0)
        c = lax.broadcasted_iota(jnp.int32, (tt, tt), 1)
        tri_ref[...] = (r < c).astype(BF16)

    scores = jax.nn.sigmoid(lg_ref[...])
    s3 = scores.reshape(shp)
    b3 = (scores + bias_ref[...]).reshape(shp)
    io_e = lax.broadcasted_iota(jnp.int32, shp, 1)
    io_g = lax.broadcasted_iota(jnp.int32, shp, 0)
    io_flat = io_g * gsz + io_e
    m1 = b3.max(axis=1, keepdims=True)
    i1 = jnp.where(b3 == m1, io_e, gsz).min(axis=1, keepdims=True)
    m2 = jnp.where(io_e == i1, neg, b3).max(axis=1, keepdims=True)
    cur = jnp.broadcast_to(m1 + m2, shp)
    gsel = jnp.zeros(shp, jnp.bool_)
    for _ in range(TOPK_GROUPS):
        m = cur.max(axis=0, keepdims=True)
        gi = jnp.where(cur == m, io_g, N_EXPERT_GROUPS).min(axis=0, keepdims=True)
        hit = io_g == gi
        gsel = jnp.logical_or(gsel, hit)
        cur = jnp.where(hit, neg, cur)
    cand = jnp.where(gsel, b3, neg)
    sel = jnp.zeros(shp, jnp.bool_)
    eids, gts = [], []
    for _ in range(TOP_K):
        m = cand.max(axis=0, keepdims=True).max(axis=1, keepdims=True)
        ei = jnp.where(cand == m, io_flat, ne).min(axis=0, keepdims=True).min(axis=1, keepdims=True)
        hit = io_flat == ei
        gts.append(jnp.where(hit, s3, 0.0).sum(axis=0, keepdims=True).sum(axis=1, keepdims=True))
        eids.append(ei)
        sel = jnp.logical_or(sel, hit)
        cand = jnp.where(hit, neg, cand)
    gsum = gts[0]
    for g in gts[1:]:
        gsum = gsum + g
    self32 = sel.astype(F32).reshape(ne, tt)
    cnt = _dot(self32.astype(BF16), tri_ref[...]) + base_ref[...]
    cnt3 = cnt.reshape(shp)
    for k in range(TOP_K):
        hit = io_flat == eids[k]
        rk = jnp.where(hit, cnt3, 0.0).sum(axis=0, keepdims=True).sum(axis=1, keepdims=True)
        rank_ref[k:k + 1, :] = rk.reshape(1, tt).astype(jnp.int32)
        eidx_ref[k:k + 1, :] = eids[k].reshape(1, tt)
        gate_ref[k:k + 1, :] = (gts[k] / gsum * ROUTED_SCALE).reshape(1, tt)
    base_new = base_ref[...] + self32.sum(axis=1, keepdims=True)
    base_ref[...] = base_new
    cnt_ref[...] = jnp.broadcast_to(base_new, cnt_ref.shape)


def _route(logits_t, bias, tt, tok0, t):
    ne = logits_t.shape[0]
    off = tok0 // tt
    out_i = jax.ShapeDtypeStruct((TOP_K, t), jnp.int32)
    row = pl.BlockSpec((TOP_K, tt), lambda i: (0, i))
    return pl.pallas_call(
        _route_kernel,
        out_shape=(out_i, jax.ShapeDtypeStruct((TOP_K, t), F32), out_i,
                   jax.ShapeDtypeStruct((ne, 128), F32)),
        grid=(t // tt,),
        in_specs=[pl.BlockSpec((ne, tt), lambda i: (0, i + off)),
                  pl.BlockSpec((ne, 1), lambda i: (0, 0))],
        out_specs=(row, row, row, pl.BlockSpec((ne, 128), lambda i: (0, 0))),
        scratch_shapes=[pltpu.VMEM((tt, tt), BF16), pltpu.VMEM((ne, 1), F32)],
        compiler_params=_cparams(("arbitrary",)),
        name="moe_route",
    )(logits_t, bias.reshape(ne, 1))


def _dest_kernel(eidx_ref, rank_ref, start_ref, dest_ref):
    kk, tt = eidx_ref.shape
    ne = start_ref.shape[0]
    n_chunk, _, r = dest_ref.shape
    io_e = lax.broadcasted_iota(jnp.int32, (ne, tt), 0)
    start = start_ref[...]
    for k in range(kk):
        hit = io_e == eidx_ref[k:k + 1, :]
        dk = jnp.where(hit, start, 0).sum(axis=0, keepdims=True) + rank_ref[k:k + 1, :]
        for c in range(n_chunk):
            dest_ref[c, k:k + 1, :] = dk[:, c * r:(c + 1) * r]


def _dest_rows(eidx_t, rank_t, start, tt, r):
    kk, t = eidx_t.shape
    ne = start.shape[0]
    return pl.pallas_call(
        _dest_kernel,
        out_shape=jax.ShapeDtypeStruct((t // r, kk, r), jnp.int32),
        grid=(t // tt,),
        in_specs=[pl.BlockSpec((kk, tt), lambda i: (0, i)),
                  pl.BlockSpec((kk, tt), lambda i: (0, i)),
                  pl.BlockSpec((ne, 1), lambda i: (0, 0))],
        out_specs=pl.BlockSpec((tt // r, kk, r), lambda i: (i, 0, 0)),
        compiler_params=_cparams(("arbitrary",)),
        name="moe_dest",
    )(eidx_t, rank_t, start.reshape(ne, 1))


SC_CHUNK = 64


def _sc_mesh():
    return plsc.VectorSubcoreMesh(core_axis_name="c", subcore_axis_name="s")


def _sc_workers():
    info = plsc.get_sparse_core_info()
    return info.num_cores, info.num_cores * info.num_subcores


def _sc_scatter_rows(rows, dest, n_out, row0=0):
    n_chunk, kk, r = dest.shape
    nc, nw = _sc_workers()
    cpw = n_chunk // nw
    assert cpw * nw == n_chunk and cpw % 2 == 0 and row0 % r == 0 and row0 + n_chunk * r <= rows.shape[0]

    @functools.partial(
        pl.kernel, mesh=_sc_mesh(),
        out_type=jax.ShapeDtypeStruct((n_out,) + rows.shape[1:], rows.dtype),
        scratch_types=[pltpu.VMEM((2, kk, r), jnp.int32), pltpu.VMEM((2, r) + rows.shape[1:], rows.dtype),
                       pltpu.SemaphoreType.DMA((2,)), pltpu.SemaphoreType.DMA((2,))])
    def scatter(rows_hbm, dest_hbm, out_hbm, idx_v, rows_v, load_sem, scat_sem):
        c0 = (lax.axis_index("s") * nc + lax.axis_index("c")) * cpw

        def loads(c, b):
            return (pltpu.make_async_copy(dest_hbm.at[c], idx_v.at[b], load_sem.at[b]),
                    pltpu.make_async_copy(rows_hbm.at[pl.ds(row0 + c * r, r)], rows_v.at[b], load_sem.at[b]))

        def scat(b, k):
            return pltpu.make_async_copy(rows_v.at[b], out_hbm.at[idx_v.at[b, k]], scat_sem.at[b])

        for cp in loads(c0, 0):
            cp.start()

        @pl.loop(0, cpw, step=2)
        def _(ci):
            for b in range(2):
                c = c0 + ci + b
                for cp in loads(c, b):
                    cp.wait()
                for k in range(kk):
                    scat(b, k).start()

                @pl.when(ci + b >= 1)
                def _():
                    for k in range(kk):
                        scat(1 - b, k).wait()

                @pl.when(ci + b + 1 < cpw)
                def _():
                    for cp in loads(c + 1, 1 - b):
                        cp.start()

        for k in range(kk):
            scat((cpw - 1) % 2, k).wait()

    return scatter(rows, dest)


def _sc_gather_rows(src, dest):
    n_chunk, kk, r = dest.shape
    t = n_chunk * r
    nc, nw = _sc_workers()
    cpw = n_chunk // nw
    nbuf = 3
    assert cpw * nw == n_chunk and kk > nbuf

    @functools.partial(
        pl.kernel, mesh=_sc_mesh(),
        out_type=jax.ShapeDtypeStruct((kk, t) + src.shape[1:], src.dtype),
        scratch_types=[pltpu.VMEM((kk, r), jnp.int32), pltpu.VMEM((nbuf, r) + src.shape[1:], src.dtype),
                       pltpu.SemaphoreType.DMA((nbuf,)), pltpu.SemaphoreType.DMA((nbuf,))])
    def gather(src_hbm, dest_hbm, out_hbm, idx_v, rows_v, get_sem, put_sem):
        c0 = (lax.axis_index("s") * nc + lax.axis_index("c")) * cpw

        @pl.loop(0, cpw)
        def _(ci):
            c = c0 + ci
            pltpu.sync_copy(dest_hbm.at[c], idx_v)

            def get(k):
                return pltpu.make_async_copy(src_hbm.at[idx_v.at[k]], rows_v.at[k % nbuf], get_sem.at[k % nbuf])

            def put(k):
                return pltpu.make_async_copy(rows_v.at[k % nbuf], out_hbm.at[k, pl.ds(c * r, r)],
                                             put_sem.at[k % nbuf])

            for k in range(nbuf - 1):
                get(k).start()
            for k in range(kk):
                get(k).wait()
                put(k).start()
                if k + nbuf - 1 < kk:
                    if k >= 1:
                        put(k - 1).wait()
                    get(k + nbuf - 1).start()
            for k in range(kk - nbuf, kk):
                put(k).wait()

    return gather(src, dest)


def _expert_kernel(be_ref, nu_ref, x_ref, wg_ref, wu_ref, wd_ref, o_ref, wgu_s, wd_s):
    i = pl.program_id(0)
    tb = o_ref.shape[0] // PACK_ROWS

    @pl.when(i < nu_ref[0])
    def _():
        @pl.when(jnp.logical_or(i == 0, be_ref[i] != be_ref[jnp.maximum(i - 1, 0)]))
        def _():
            wgu_s[:, :D_EXPERT] = wg_ref[...].astype(BF16)
            wgu_s[:, D_EXPERT:] = wu_ref[...].astype(BF16)
            wd_s[...] = wd_ref[...].astype(BF16)

        x = jnp.concatenate([v.astype(BF16) for v in _unpack_load(x_ref, tb)], axis=-1)
        gu = _dot(x, wgu_s[...])
        g = gu[:, :D_EXPERT]
        h = g * jax.nn.sigmoid(g) * gu[:, D_EXPERT:]
        _pack_store(o_ref, _dot(h.astype(BF16), wd_s[...]))


def _experts(xs, blk_e, n_used, w_gate, w_up, w_down, layer, tb):
    rows = xs.shape[0] // PACK_ROWS
    _, ne, d, de = w_gate.shape
    nb = rows // tb
    row_map = lambda i, be, nu: (jnp.minimum(i, nu[0] - 1), 0)
    w_map = lambda i, be, nu: (layer, be[i], 0, 0)
    grid_spec = pltpu.PrefetchScalarGridSpec(
        num_scalar_prefetch=2,
        grid=(nb,),
        in_specs=[pl.BlockSpec((tb * PACK_ROWS, LANES), row_map),
                  pl.BlockSpec((None, None, d, de), w_map),
                  pl.BlockSpec((None, None, d, de), w_map),
                  pl.BlockSpec((None, None, de, d), w_map)],
        out_specs=pl.BlockSpec((tb * PACK_ROWS, LANES), row_map),
        scratch_shapes=[pltpu.VMEM((d, 2 * de), BF16), pltpu.VMEM((de, d), BF16)],
    )
    return pl.pallas_call(
        _expert_kernel,
        out_shape=jax.ShapeDtypeStruct(xs.shape, U32),
        grid_spec=grid_spec,
        compiler_params=_cparams(("arbitrary",)),
        name="moe_experts",
    )(blk_e, n_used, xs, w_gate, w_up, w_down)


def _combine_kernel(yk_ref, gate_ref, fin_ref, shgu_ref, shd_ref, x1_ref, mod_ref, g3_ref, *rest, fuse_next):
    if fuse_next:
        nmod_ref, ng0_ref = rest[0], rest[1]
        o_ref, h_ref = rest[-2], rest[-1]
    else:
        o_ref = rest[-1]
    tm, d = x1_ref.shape
    for rows in _sub_tiles(tm):
        n, r0 = SUB_ROWS, rows.start
        gates = gate_ref[rows, :]
        blocks = None
        for k in range(TOP_K):
            gk = gates[:, k:k + 1]
            terms = [gk * v for v in _unpack_load(yk_ref, n, lead=(k,), row0=r0)]
            blocks = terms if blocks is None else [a + b for a, b in zip(blocks, terms)]
        fin = jnp.concatenate([v.astype(BF16) for v in _unpack_load(fin_ref, n, row0=r0)], axis=-1)
        gu = _dot(fin, shgu_ref[...])
        g = gu[:, :D_EXPERT]
        hsh = g * jax.nn.sigmoid(g) * gu[:, D_EXPERT:]
        f = jnp.concatenate(blocks, axis=-1) + _dot(hsh.astype(BF16), shd_ref[...])
        x1 = x1_ref[rows, :]
        x2 = x1 + _rows(_mod_chunk(mod_ref, 5, d), x1) * _rms(f, g3_ref[...])
        if len(o_ref.shape) == 2:
            o_ref[rows, :] = x2
        else:
            nb = o_ref.shape[0]
            ts = SUB_ROWS // nb
            o_ref[:, r0 // nb:r0 // nb + ts, :] = jnp.swapaxes(x2.reshape(ts, nb, d), 0, 1)
        if fuse_next:
            hn = (_rms(x2, ng0_ref[...]) * (1.0 + _rows(_mod_chunk(nmod_ref, 1, d), x2))
                  + _rows(_mod_chunk(nmod_ref, 0, d), x2))
            for gi in range(h_ref.shape[0]):
                h_ref[gi, rows, :] = hn[:, gi * LANES:(gi + 1) * LANES]


def _combine(yk, gates, fin, shgu, shd, x1, mods, g3, tm, rows_per_mod, n_tok, x_off, yk_off, fin_off,
             batch_out=0, prev=None, out_rows=None, out_off=None, nxt=None):
    t, d = x1.shape
    out_rows = t if out_rows is None else out_rows
    out_off = x_off if out_off is None else out_off
    xo, yo, fo, oo = x_off // tm, yk_off // tm, fin_off // tm, out_off // tm
    tiles_per_mod = rows_per_mod // tm
    full = lambda a: pl.BlockSpec(a.shape, lambda i: (0,) * a.ndim)
    if batch_out:
        out_shape = [jax.ShapeDtypeStruct((batch_out, out_rows // batch_out, d), F32)]
        out_specs = [pl.BlockSpec((batch_out, tm // batch_out, d), lambda i: (0, i + oo, 0))]
    else:
        out_shape = [jax.ShapeDtypeStruct((out_rows, d), F32)]
        out_specs = [pl.BlockSpec((tm, d), lambda i: (i + oo, 0))]
    in_specs = [pl.BlockSpec((TOP_K, tm * PACK_ROWS, LANES), lambda i: (0, i + yo, 0)),
                pl.BlockSpec((tm, TOP_K), lambda i: (i + yo, 0)),
                pl.BlockSpec((tm * PACK_ROWS, LANES), lambda i: (i + fo, 0)),
                full(shgu), full(shd),
                pl.BlockSpec((tm, d), lambda i: (i + xo, 0)),
                pl.BlockSpec((None,) + mods.shape[1:], lambda i: ((i + xo) // tiles_per_mod, 0, 0)),
                full(g3)]
    args = [yk, gates, fin, shgu, shd, x1, mods, g3]
    if nxt is not None:
        in_specs += [full(nxt[0]), full(nxt[1])]
        args += list(nxt)
        out_shape.append(jax.ShapeDtypeStruct((d // LANES, out_rows, LANES), F32))
        out_specs.append(pl.BlockSpec((d // LANES, tm, LANES), lambda i: (0, i + oo, 0)))
    aliases = {}
    if prev is not None:
        for j, p in enumerate(prev if isinstance(prev, (tuple, list)) else [prev]):
            in_specs.append(pl.BlockSpec(memory_space=pl.ANY))
            aliases[len(args)] = j
            args.append(p)
    out = pl.pallas_call(
        functools.partial(_combine_kernel, fuse_next=nxt is not None),
        out_shape=tuple(out_shape),
        grid=(n_tok // tm,),
        in_specs=in_specs,
        out_specs=tuple(out_specs),
        input_output_aliases=aliases,
        compiler_params=_cparams(("arbitrary",)),
        name="moe_combine",
    )(*args)
    return out if nxt is not None else out[0]


def _moe(fin, logits_t, bias, w_gate, w_up, w_down, layer, tb, tok0, t):
    t_all = fin.shape[0] // PACK_ROWS
    ne = w_gate.shape[1]
    tt = 512
    eidx_t, gates_t, rank_t, cnt = _route(logits_t, bias, tt, tok0, t)
    counts = cnt[:, 0].astype(jnp.int32)
    padded = (counts + tb - 1) // tb * tb
    pad_end = jnp.cumsum(padded)
    pad_start = pad_end - padded
    nb = (t * TOP_K) // tb + ne
    n_used = pad_end[-1] // tb
    blk_start = jnp.arange(nb, dtype=jnp.int32) * tb
    blk = jnp.sum(pad_end[None, :] <= jnp.minimum(blk_start, pad_end[-1] - 1)[:, None], axis=1)
    blk_e = jnp.minimum(blk, ne - 1).astype(jnp.int32)
    dest = _dest_rows(eidx_t, rank_t, pad_start, tt, SC_CHUNK)
    xs = _sc_scatter_rows(fin.reshape(t_all, PACK_ROWS, LANES), dest, nb * tb, row0=tok0)
    ys = _experts(xs.reshape(nb * tb * PACK_ROWS, LANES), blk_e, n_used.reshape(1).astype(jnp.int32),
                  w_gate, w_up, w_down, layer, tb)
    yk = _sc_gather_rows(ys.reshape(nb * tb, PACK_ROWS, LANES), dest)
    return yk.reshape(TOP_K, t * PACK_ROWS, LANES), gates_t.T


def _pre_s5_kernel(x_ref, mod_ref, g0_ref, *refs):
    h_ref, xt_ref = refs[-2:]
    nb, tt, d = x_ref.shape
    x = jnp.swapaxes(x_ref[...], 0, 1).reshape(tt * nb, d)
    h = (_rms(x, g0_ref[...]) * (1.0 + _rows(_mod_chunk(mod_ref, 1, d), x))
         + _rows(_mod_chunk(mod_ref, 0, d), x))
    for g in range(h_ref.shape[0]):
        h_ref[g] = h[:, g * LANES:(g + 1) * LANES]
    xt_ref[...] = x


def _pre_s5(x, mods, g0, n_total, t_off, prev, tt):
    nb, n, d = x.shape
    off = t_off // tt
    out_shape = (jax.ShapeDtypeStruct((d // LANES, n_total * nb, LANES), F32),
                 jax.ShapeDtypeStruct((n_total * nb, d), F32))
    out_specs = (pl.BlockSpec((d // LANES, tt * nb, LANES), lambda i: (0, i + off, 0)),
                 pl.BlockSpec((tt * nb, d), lambda i: (i + off, 0)))
    in_specs = [pl.BlockSpec((nb, tt, d), lambda i: (0, i, 0)),
                pl.BlockSpec(mods.shape, lambda i: (0, 0)),
                pl.BlockSpec(g0.shape, lambda i: (0, 0))]
    args = (x, mods, g0)
    aliases = {}
    if prev is not None:
        in_specs += [pl.BlockSpec(memory_space=pl.ANY)] * 2
        args += tuple(prev)
        aliases = {3: 0, 4: 1}
    return pl.pallas_call(
        _pre_s5_kernel,
        out_shape=out_shape,
        grid=(n // tt,),
        in_specs=in_specs,
        out_specs=out_specs,
        input_output_aliases=aliases,
        compiler_params=_cparams(("arbitrary",)),
        name="pre_s5",
    )(*args)


def _s5_scan_kernel(h_ref, bm_ref, cm_ref, lam_ref, y_ref, bu0, bu1, xb0, xb1, st_ref):
    first = jnp.logical_and(jnp.logical_and(pl.program_id(0) == 0, pl.program_id(1) == 0), pl.program_id(2) == 0)
    dr = pl.program_id(1)
    s = pl.program_id(2)
    tc, nb, cw = h_ref.shape
    half = st_ref.shape[1] // 2

    @pl.when(first)
    def _():
        for r in (bu0, bu1, xb0, xb1, st_ref):
            r[...] = jnp.zeros_like(r)

    def stages(bu_w, bu_r, xb_w, xb_r):
        lr = jnp.broadcast_to(lam_ref[0:1, :], (nb, half))
        li = jnp.broadcast_to(lam_ref[1:2, :], (nb, half))
        fresh = s == 1
        xr = jnp.where(fresh, 0.0, st_ref[:, 0:half])
        xi = jnp.where(fresh, 0.0, st_ref[:, half:2 * half])
        tp = tc // S5_PIECES
        for p in range(S5_PIECES):
            for i in range(p * tp, (p + 1) * tp):
                t = i + dr * (tc - 1 - 2 * i)
                rows = pl.ds(pl.multiple_of(t * nb, nb), nb)
                nr = lr * xr - li * xi + bu_r[rows, 0:half]
                ni = lr * xi + li * xr + bu_r[rows, half:2 * half]
                xb_w[rows, 0:half] = nr.astype(BF16)
                xb_w[rows, half:2 * half] = ni.astype(BF16)
                xr, xi = nr, ni
            ts = slice(p * tp, (p + 1) * tp)
            mr = slice(p * tp * nb, (p + 1) * tp * nb)
            y_ref[ts] = _dot(xb_r[mr, :], cm_ref[...]).reshape(tp, nb, cw)
            bu_w[mr, :] = _dot(h_ref[ts].reshape(tp * nb, cw).astype(BF16), bm_ref[...])
        st_ref[:, 0:half] = xr
        st_ref[:, half:2 * half] = xi

    @pl.when(s % 2 == 0)
    def _():
        stages(bu0, bu1, xb1, xb0)

    @pl.when(s % 2 == 1)
    def _():
        stages(bu1, bu0, xb0, xb1)


def _s5_scan(h_all, n_ctx, bm, cm, lam, tc):
    ng, nt, nb, cw = h_all.shape
    nl = nt - n_ctx
    assert cw == S5_GROUPS_PER_BLOCK * S5_GROUP
    sw = 2 * S5_GROUPS_PER_BLOCK * S5_STATE
    ncc, n = n_ctx // tc, nt // tc

    def chunk(dr, j):
        j = jnp.clip(j, 0, n - 1)
        rev = jnp.where(j < ncc, ncc - 1 - j, n - 1 - (j - ncc))
        return jnp.where(dr == 0, j, rev)

    def out_map(g, dr, s):
        return (dr, g, chunk(dr, jnp.clip(s - 2, ncc, n - 1)) - ncc, 0, 0)

    return pl.pallas_call(
        _s5_scan_kernel,
        out_shape=jax.ShapeDtypeStruct((2, ng, nl, nb, cw), F32),
        grid=(ng, 2, n + 2),
        in_specs=[pl.BlockSpec((None, tc, nb, cw), lambda g, dr, s: (g, chunk(dr, s), 0, 0)),
                  pl.BlockSpec((None, None, cw, sw), lambda g, dr, s: (dr, g, 0, 0)),
                  pl.BlockSpec((None, None, sw, cw), lambda g, dr, s: (dr, g, 0, 0)),
                  pl.BlockSpec((None, None, 2, sw // 2), lambda g, dr, s: (dr, g, 0, 0))],
        out_specs=pl.BlockSpec((None, None, tc, nb, cw), out_map),
        scratch_shapes=[pltpu.VMEM((tc * nb, sw), F32), pltpu.VMEM((tc * nb, sw), F32),
                        pltpu.VMEM((tc * nb, sw), BF16), pltpu.VMEM((tc * nb, sw), BF16),
                        pltpu.VMEM((nb, sw), F32)],
        compiler_params=_cparams(("arbitrary", "arbitrary", "arbitrary")),
        name="s5_scan",
    )(h_all, bm, cm, lam)


def _s5_params(lam_re, lam_im, log_step, b_re, b_im, c_re, c_im):
    g, p = lam_re.shape[1:]
    gb = S5_GROUPS_PER_BLOCK
    nblk = g // gb
    step = jnp.exp(log_step)[..., None]
    mag = jnp.exp(lam_re * step)
    lb_re = mag * jnp.cos(lam_im * step)
    lb_im = mag * jnp.sin(lam_im * step)
    den = lam_re * lam_re + lam_im * lam_im
    f_re = ((lb_re - 1.0) * lam_re + lb_im * lam_im) / den
    f_im = (lb_im * lam_re - (lb_re - 1.0) * lam_im) / den
    bb_re = f_re[..., None] * b_re - f_im[..., None] * b_im
    bb_im = f_re[..., None] * b_im + f_im[..., None] * b_re
    eye = jnp.eye(gb, dtype=F32)

    def in_map(w):
        w = w.reshape(2, nblk, gb, p, S5_GROUP)
        return jnp.einsum("dnapi,ab->dnaibp", w, eye).reshape(2, nblk, gb * S5_GROUP, gb * p)

    def out_map(w):
        w = w.reshape(2, nblk, gb, S5_GROUP, p)
        return jnp.einsum("dnaip,ab->dnapbi", w, eye).reshape(2, nblk, gb * p, gb * S5_GROUP)

    bm = jnp.concatenate([in_map(bb_re), in_map(bb_im)], axis=-1).astype(BF16)
    cm = jnp.concatenate([out_map(c_re), out_map(-c_im)], axis=-2).astype(BF16)
    lam = jnp.stack([lb_re.reshape(2, nblk, gb * p), lb_im.reshape(2, nblk, gb * p)], axis=2)
    return bm, cm, lam


def _rope_tables(n_tokens):
    rows = n_tokens // GRID_W
    row = jnp.repeat(jnp.arange(rows), GRID_W).astype(F32)
    col = jnp.tile(jnp.arange(GRID_W), rows).astype(F32)
    n_freq = ROPE_DIM // 4
    inv_freq = ROPE_BASE ** (-jnp.arange(n_freq, dtype=F32) / n_freq)
    ang = jnp.concatenate([row[:, None] * inv_freq, col[:, None] * inv_freq], axis=-1)
    cos, sin = jnp.cos(ang), jnp.sin(ang)
    z = jnp.zeros((n_tokens, 128 - ROPE_DIM), F32)
    return (jnp.concatenate([cos, cos, z], axis=-1), jnp.concatenate([-sin, sin, z], axis=-1))


def _router_halves(w_router):
    wt = w_router.T
    hi = wt.astype(BF16)
    lo = (wt - hi.astype(F32)).astype(BF16)
    return jnp.concatenate([hi, lo], axis=0)


def _split_pairs(w):
    ev, od = w[..., 0::2], w[..., 1::2]
    z = jnp.zeros(w.shape[:-1] + (128 - ROPE_DIM,), w.dtype)
    return jnp.concatenate([ev, od, z], axis=-1), jnp.concatenate([od, ev, z], axis=-1)


def _mla_weights(w_dqkv, w_uq, w_ukv):
    kp, kps = _split_pairs(w_dqkv[:, Q_LORA + KV_LORA:])
    wd = jnp.concatenate([w_dqkv[:, :Q_LORA + KV_LORA], kp, kps], axis=-1).astype(BF16)
    wq3 = w_uq.reshape(Q_LORA, MLA_HEADS, NOPE_DIM + ROPE_DIM)
    qp, qps = _split_pairs(wq3[:, :, NOPE_DIM:])
    wq = jnp.concatenate([wq3[:, :, :NOPE_DIM].reshape(Q_LORA, -1), qp.reshape(Q_LORA, -1),
                          qps.reshape(Q_LORA, -1)], axis=-1).astype(BF16)
    wkv3 = w_ukv.reshape(KV_LORA, MLA_HEADS, NOPE_DIM + V_DIM)
    wkv = jnp.concatenate([wkv3[:, :, :NOPE_DIM].reshape(KV_LORA, -1),
                           wkv3[:, :, NOPE_DIM:].reshape(KV_LORA, -1)], axis=-1).astype(BF16)
    return wd, wq, wkv


@jax.jit
def kernel(x, c, ctx, c_ctx, ada_w, ada_b, norm_g, mla_w_dqkv, mla_g_q, mla_g_kv, mla_w_uq, mla_w_ukv, mla_w_o, s5_lam_re, s5_lam_im, s5_log_step, s5_b_re, s5_b_im, s5_c_re, s5_c_im, s5_d, s5_w_glu, s5_b_glu, moe_w_router, moe_bias, moe_w_gate, moe_w_up, moe_w_down, sh_w_gate, sh_w_up, sh_w_down):
    b, l, d = x.shape
    n_ctx = ctx.shape[1]
    assert ada_w.shape[0] == 2 and b % 8 == 0
    ta = 256
    tm = 512
    tb = 512
    row = lambda v: v.reshape(1, -1)

    n_rows = (b + 1 + 7) // 8 * 8
    cvec = jnp.zeros((n_rows, d), F32).at[:b].set(c).at[b].set(c_ctx)
    mods = _ada_mods(cvec, ada_w, ada_b)

    def shared_weights(i):
        shgu = jnp.concatenate([sh_w_gate[i], sh_w_up[i]], axis=-1).astype(BF16)
        return shgu, sh_w_down[i].astype(BF16)

    mod_lat = mods[0, :b].reshape(b, 1, N_MOD * d)
    mod_ctx = mods[0, b].reshape(1, 1, N_MOD * d)
    wd, wq, wkv = _mla_weights(mla_w_dqkv[0], mla_w_uq[0], mla_w_ukv[0])
    cos_l, sin_l = _rope_tables(l)
    cos_c = jnp.concatenate([jnp.ones((n_ctx, ROPE_DIM), F32), jnp.zeros((n_ctx, 128 - ROPE_DIM), F32)], -1)
    sin_c = jnp.zeros((n_ctx, 128), F32)
    pre = functools.partial(_pre_mla, g0=row(norm_g[0, 0]), wd=wd, gq=row(mla_g_q[0]), gkv=row(mla_g_kv[0]),
                            wq=wq, wkv=wkv, tm=ta)
    q_c, k_c, v_c = pre(ctx, mod_ctx, cos_t=cos_c, sin_t=sin_c)
    q_l, k_l, v_l = pre(x, mod_lat, cos_t=cos_l, sin_t=sin_l)
    o_l = _attention(q_l, [k_c, k_l], [v_c, v_l], 2 * ta)
    o_c = _attention(q_c, [k_c], [v_c], n_ctx)

    wo = mla_w_o[0].astype(BF16)
    wr_t = _router_halves(moe_w_router[0])
    g1, g2, g3 = row(norm_g[0, 1]), row(norm_g[0, 2]), row(norm_g[0, 3])
    post = functools.partial(_post_mixer, _post_proj_kernel, consts=[wo], g1=g1, g2=g2, wr_t=wr_t, tm=tm,
                             name="post_mla")
    o_spec = pl.BlockSpec((tm, o_l.shape[-1]), lambda i: (i, 0))
    n_moe = b * (n_ctx + l)
    x1_c, fin, lg = post([(o_c.reshape(b * n_ctx, -1), o_spec)], x=ctx.reshape(b * n_ctx, d),
                         n_tok=b * n_ctx, x_off=0, mods=mod_ctx, rows_per_mod=b * n_ctx, moe_total=n_moe)
    tt = tm // b
    mod_lat_tm = mods[0, :b][None]
    x1_l, fin, lg = _post_mixer(
        _post_proj_tm_kernel, [(o_l, pl.BlockSpec((b, tt, o_l.shape[-1]), lambda i: (0, i, 0)))], [wo],
        x=x, n_tok=b * l, x_off=0, mods=mod_lat_tm, g1=g1, g2=g2, wr_t=wr_t, tm=tm, rows_per_mod=b * l,
        name="post_mla", moe_total=n_moe, moe_off=b * n_ctx, prev=(fin, lg),
        x_spec=pl.BlockSpec((b, tt, d), lambda i: (0, i, 0)))
    shgu, shd = shared_weights(0)
    lat_a = (l // 2) * b
    n_a = b * n_ctx + lat_a
    moe = functools.partial(_moe, fin, lg, moe_bias[0], moe_w_gate, moe_w_up, moe_w_down, 0, tb)
    yk_a, gates_a = moe(0, n_a)
    yk_b, gates_b = moe(n_a, n_moe - n_a)
    comb = functools.partial(_combine, fin=fin, shgu=shgu, shd=shd, g3=g3, tm=tm)
    x2_c = comb(yk_a, gates_a, x1=x1_c, mods=mod_ctx, rows_per_mod=b * n_ctx, n_tok=b * n_ctx, x_off=0, yk_off=0,
                fin_off=0)

    n_all = n_ctx + l
    mod_lat = mods[1, :b]
    mod_ctx = jnp.broadcast_to(mods[1, b][None], (b, N_MOD * d))
    g0 = row(norm_g[1, 0])
    h, xt = _pre_s5(x2_c.reshape(b, n_ctx, d), mod_ctx, g0, n_all, 0, None, tt)
    comb_l = functools.partial(comb, x1=x1_l, mods=mod_lat_tm, rows_per_mod=b * l, out_rows=n_all * b,
                               nxt=(mod_lat, g0))
    xt, h = comb_l(yk_a, gates_a, n_tok=lat_a, x_off=0, yk_off=b * n_ctx, fin_off=b * n_ctx,
                   out_off=n_ctx * b, prev=(xt, h))
    xt, h = comb_l(yk_b, gates_b, n_tok=b * l - lat_a, x_off=lat_a, yk_off=0, fin_off=n_a,
                   out_off=n_ctx * b + lat_a, prev=(xt, h))
    bm, cm, lam = _s5_params(s5_lam_re[0], s5_lam_im[0], s5_log_step[0], s5_b_re[0], s5_b_im[0],
                             s5_c_re[0], s5_c_im[0])
    ng = d // LANES
    y = _s5_scan(h.reshape(ng, n_all, b, LANES), n_ctx, bm, cm, lam, 64).reshape(2, ng, l * b, LANES)
    g1, g2, g3 = row(norm_g[1, 1]), row(norm_g[1, 2]), row(norm_g[1, 3])
    lat0 = n_ctx * b // tm
    x1, fin, lg = _post_mixer(
        _post_glu_kernel,
        [(h, pl.BlockSpec((ng, tm, LANES), lambda i: (0, i + lat0, 0))),
         (y, pl.BlockSpec((None, ng, tm, LANES), lambda i: (0, 0, i, 0))),
         (y, pl.BlockSpec((None, ng, tm, LANES), lambda i: (1, 0, i, 0)))],
        [row(s5_d[0]), s5_w_glu[0].astype(BF16), row(s5_b_glu[0])],
        x=xt, n_tok=l * b, x_off=n_ctx * b, mods=mod_lat[None], g1=g1, g2=g2, wr_t=_router_halves(moe_w_router[1]), tm=tm,
        rows_per_mod=l * b, name="post_s5")
    shgu, shd = shared_weights(1)
    n_h = (l // 2) * b
    moe = functools.partial(_moe, fin, lg, moe_bias[1], moe_w_gate, moe_w_up, moe_w_down, 1, tb)
    yk_a, gates_a = moe(0, n_h)
    yk_b, gates_b = moe(n_h, l * b - n_h)
    comb = functools.partial(_combine, fin=fin, shgu=shgu, shd=shd, x1=x1, mods=mod_lat[None], g3=g3, tm=tm,
                             rows_per_mod=l * b, yk_off=0, batch_out=b)
    out = comb(yk_a, gates_a, n_tok=n_h, x_off=0, fin_off=0)
    return comb(yk_b, gates_b, n_tok=l * b - n_h, x_off=n_h, fin_off=n_h, prev=out)
```

```python
import functools

import jax
import jax.numpy as jnp
from jax import lax
from jax.experimental import pallas as pl
from jax.experimental.pallas import tpu as pltpu
from jax.experimental.pallas import tpu_sc as plsc

F32 = jnp.float32
BF16 = jnp.bfloat16
U32 = jnp.uint32

N_MOD = 6
NORM_EPS = 1e-6
LOG2_E = 1.4426950408889634
GRID_W = 64
MLA_HEADS = 8
Q_LORA = 384
KV_LORA = 256
NOPE_DIM = 128
ROPE_DIM = 64
V_DIM = 128
V_PAD = 256
ROPE_BASE = 10000.0
QK_PAD = 256
S5_GROUP = 16
S5_STATE = 64
S5_GROUPS_PER_BLOCK = 8
S5_PIECES = 4
N_EXPERTS = 64
TOP_K = 8
N_EXPERT_GROUPS = 8
TOPK_GROUPS = 4
D_EXPERT = 256
ROUTED_SCALE = 2.5

VMEM_LIMIT = 56 * 1024 * 1024


def _cparams(sem):
    return pltpu.CompilerParams(dimension_semantics=sem, vmem_limit_bytes=VMEM_LIMIT)


def _rms(x, g):
    return x * lax.rsqrt(jnp.mean(x * x, axis=-1, keepdims=True) + NORM_EPS) * g


def _rows(v, like):
    r = v.shape[0]
    if r == 1:
        return v
    tm, d = like.shape
    return jnp.broadcast_to(v[None], (tm // r, r, d)).reshape(tm, d)


def _mod_chunk(mod_ref, j, d):
    return mod_ref[:, j * d:(j + 1) * d]


def _dot(a, b):
    return jnp.dot(a, b, preferred_element_type=F32)


PACK_ROWS = 4
LANES = 128


def _pack_store(ref, val, lead=(), row0=0):
    n = val.shape[0]
    bits = lax.bitcast_convert_type(val.astype(BF16).astype(F32), U32)
    for s in range(PACK_ROWS):
        lo = bits[:, s * LANES:(s + 1) * LANES] >> 16
        hi = bits[:, (s + PACK_ROWS) * LANES:(s + PACK_ROWS + 1) * LANES] & jnp.uint32(0xFFFF0000)
        ref[lead + (pl.ds(row0 * PACK_ROWS + s, n, stride=PACK_ROWS), slice(None))] = lo | hi


def _unpack_load(ref, n, lead=(), row0=0):
    los, his = [], []
    for s in range(PACK_ROWS):
        w = ref[lead + (pl.ds(row0 * PACK_ROWS + s, n, stride=PACK_ROWS), slice(None))]
        los.append(lax.bitcast_convert_type(w << 16, F32))
        his.append(lax.bitcast_convert_type(w & jnp.uint32(0xFFFF0000), F32))
    return los + his


def _ada_kernel(c_ref, w_ref, b_ref, o_ref):
    c = c_ref[...]
    s = c * jax.nn.sigmoid(c)
    o_ref[...] = jnp.dot(s, w_ref[...], preferred_element_type=F32,
                         precision=lax.Precision.HIGHEST) + b_ref[...]


def _ada_mods(cvec, ada_w, ada_b):
    depth, d, n = ada_w.shape
    rows = cvec.shape[0]
    tn = 1536
    return pl.pallas_call(
        _ada_kernel,
        out_shape=jax.ShapeDtypeStruct((depth, rows, n), F32),
        grid=(depth, n // tn),
        in_specs=[pl.BlockSpec((rows, d), lambda l, j: (0, 0)),
                  pl.BlockSpec((None, d, tn), lambda l, j: (l, 0, j)),
                  pl.BlockSpec((None, 1, tn), lambda l, j: (l, 0, j))],
        out_specs=pl.BlockSpec((None, rows, tn), lambda l, j: (l, 0, j)),
        compiler_params=_cparams(("arbitrary", "arbitrary")),
        name="ada_mods",
    )(cvec, ada_w, ada_b.reshape(depth, 1, n))


def _pre_mla_kernel(x_ref, mod_ref, g0_ref, wd_ref, gq_ref, gkv_ref, wq_ref, wkv_ref, cos_ref, sin_ref,
                    q_ref, k_ref, v_ref):
    d = x_ref.shape[-1]
    x = x_ref[...]
    h = _rms(x, g0_ref[...]) * (1.0 + _mod_chunk(mod_ref, 1, d)) + _mod_chunk(mod_ref, 0, d)
    a = _dot(h.astype(BF16), wd_ref[...])
    cq = _rms(a[:, :Q_LORA], gq_ref[...])
    ckv = _rms(a[:, Q_LORA:Q_LORA + KV_LORA], gkv_ref[...])
    cos = cos_ref[...]
    sin = sin_ref[...]
    o = Q_LORA + KV_LORA
    k_rot = (a[:, o:o + 128] * cos + a[:, o + 128:o + 256] * sin).astype(BF16)
    qa = _dot(cq.astype(BF16), wq_ref[...])
    kva = _dot(ckv.astype(BF16), wkv_ref[...])
    hw = MLA_HEADS * 128
    scale = (NOPE_DIM + ROPE_DIM) ** -0.5 * LOG2_E
    for hd in range(MLA_HEADS):
        lo = hd * 128
        q_rot = qa[:, hw + lo:hw + lo + 128] * cos + qa[:, 2 * hw + lo:2 * hw + lo + 128] * sin
        q_ref[:, hd * QK_PAD:hd * QK_PAD + 128] = (qa[:, lo:lo + 128] * scale).astype(BF16)
        q_ref[:, hd * QK_PAD + 128:(hd + 1) * QK_PAD] = (q_rot * scale).astype(BF16)
        k_ref[:, hd * QK_PAD:hd * QK_PAD + 128] = kva[:, lo:lo + 128].astype(BF16)
        k_ref[:, hd * QK_PAD + 128:(hd + 1) * QK_PAD] = k_rot
        v_ref[:, hd * V_PAD:hd * V_PAD + V_DIM] = kva[:, hw + lo:hw + lo + 128].astype(BF16)
        v_ref[:, hd * V_PAD + V_DIM:(hd + 1) * V_PAD] = jnp.ones((x.shape[0], V_PAD - V_DIM), BF16)


def _pre_mla(x, mods, g0, wd, gq, gkv, wq, wkv, cos_t, sin_t, tm):
    b, n, d = x.shape
    nb_mod = mods.shape[0]
    full = lambda a: pl.BlockSpec(a.shape, lambda i, j: (0,) * a.ndim)
    mod_map = (lambda i, j: (i, 0, 0)) if nb_mod > 1 else (lambda i, j: (0, 0, 0))
    qk_w = MLA_HEADS * QK_PAD
    v_w = MLA_HEADS * V_PAD
    return pl.pallas_call(
        _pre_mla_kernel,
        out_shape=(jax.ShapeDtypeStruct((b, n, qk_w), BF16),
                   jax.ShapeDtypeStruct((b, n, qk_w), BF16),
                   jax.ShapeDtypeStruct((b, n, v_w), BF16)),
        grid=(b, n // tm),
        in_specs=[pl.BlockSpec((None, tm, d), lambda i, j: (i, j, 0)),
                  pl.BlockSpec((None, 1, mods.shape[-1]), mod_map),
                  full(g0), full(wd), full(gq), full(gkv), full(wq), full(wkv),
                  pl.BlockSpec((tm, 128), lambda i, j: (j, 0)),
                  pl.BlockSpec((tm, 128), lambda i, j: (j, 0))],
        out_specs=(pl.BlockSpec((None, tm, qk_w), lambda i, j: (i, j, 0)),
                   pl.BlockSpec((None, tm, qk_w), lambda i, j: (i, j, 0)),
                   pl.BlockSpec((None, tm, v_w), lambda i, j: (i, j, 0))),
        compiler_params=_cparams(("arbitrary", "arbitrary")),
        name="pre_mla",
    )(x, mods, g0, wd, gq, gkv, wq, wkv, cos_t, sin_t)


def _attn_kernel(*refs, n_seg):
    q_ref = refs[0]
    k_refs = refs[1:1 + n_seg]
    v_refs = refs[1 + n_seg:1 + 2 * n_seg]
    o_ref = refs[1 + 2 * n_seg]
    nt = (((1,), (1,)), ((), ()))

    def scores(hd):
        q = q_ref[:, hd * QK_PAD:(hd + 1) * QK_PAD]
        return [lax.dot_general(q, k[:, hd * QK_PAD:(hd + 1) * QK_PAD], nt, preferred_element_type=F32)
                for k in k_refs]

    nxt = scores(0)
    for hd in range(MLA_HEADS):
        ss = nxt
        if hd + 1 < MLA_HEADS:
            nxt = scores(hd + 1)
        m = ss[0].max(axis=-1, keepdims=True)
        for s in ss[1:]:
            m = jnp.maximum(m, s.max(axis=-1, keepdims=True))
        acc = None
        for s, v in zip(ss, v_refs):
            pv = _dot(jnp.exp2((s - m).astype(BF16)), v[:, hd * V_PAD:(hd + 1) * V_PAD])
            acc = pv if acc is None else acc + pv
        o_ref[:, hd * V_DIM:(hd + 1) * V_DIM] = (acc[:, :V_DIM] / acc[:, V_DIM:V_DIM + 1]).astype(BF16)


def _attention(q, ks, vs, tq):
    b, nq, qk_w = q.shape
    v_w = MLA_HEADS * V_DIM
    kv_spec = lambda a: pl.BlockSpec((None,) + a.shape[1:], lambda i, j: (i, 0, 0))
    return pl.pallas_call(
        functools.partial(_attn_kernel, n_seg=len(ks)),
        out_shape=jax.ShapeDtypeStruct((b, nq, v_w), BF16),
        grid=(b, nq // tq),
        in_specs=[pl.BlockSpec((None, tq, qk_w), lambda i, j: (i, j, 0))]
                 + [kv_spec(a) for a in ks] + [kv_spec(a) for a in vs],
        out_specs=pl.BlockSpec((None, tq, v_w), lambda i, j: (i, j, 0)),
        compiler_params=_cparams(("arbitrary", "arbitrary")),
        name="mla_attention",
    )(q, *ks, *vs)


SUB_ROWS = 256


def _sub_tiles(n):
    return [slice(r, r + SUB_ROWS) for r in range(0, n, SUB_ROWS)]


def _post_core(o, x, rows, mod_ref, g1_ref, g2_ref, wr_ref, x1_ref, fin_ref, lg_ref):
    d = x.shape[-1]
    ne = lg_ref.shape[0]
    gate = _rows(_mod_chunk(mod_ref, 2, d), x)
    shift = _rows(_mod_chunk(mod_ref, 3, d), x)
    scale = _rows(_mod_chunk(mod_ref, 4, d), x)
    x1 = x + gate * _rms(o, g1_ref[...])
    fin = _rms(x1, g2_ref[...]) * (1.0 + scale) + shift
    x1_ref[rows, :] = x1
    _pack_store(fin_ref, fin, row0=rows.start)
    nt = (((1,), (1,)), ((), ()))
    f_hi = fin.astype(BF16)
    f_lo = (fin - f_hi.astype(F32)).astype(BF16)
    r_hi = lax.dot_general(wr_ref[...], f_hi, nt, preferred_element_type=F32)
    r_lo = lax.dot_general(wr_ref[0:ne, :], f_lo, nt, preferred_element_type=F32)
    lg_ref[:, rows] = r_hi[:ne] + r_hi[ne:] + r_lo


def _post_proj_kernel(o_ref, wo_ref, x_ref, mod_ref, g1_ref, g2_ref, wr_ref, *rest):
    x1_ref, fin_ref, lg_ref = rest[-3:]
    for rows in _sub_tiles(x_ref.shape[0]):
        o = _dot(o_ref[rows, :], wo_ref[...])
        _post_core(o, x_ref[rows, :], rows, mod_ref, g1_ref, g2_ref, wr_ref, x1_ref, fin_ref, lg_ref)


def _post_proj_tm_kernel(o_ref, wo_ref, x_ref, mod_ref, g1_ref, g2_ref, wr_ref, *rest):
    x1_ref, fin_ref, lg_ref = rest[-3:]
    nb, tt, d = x_ref.shape
    ts = SUB_ROWS // nb
    for t0 in range(0, tt, ts):
        o = _dot(o_ref[:, t0:t0 + ts, :].reshape(nb * ts, o_ref.shape[-1]), wo_ref[...])
        o = jnp.swapaxes(o.reshape(nb, ts, d), 0, 1).reshape(ts * nb, d)
        x = jnp.swapaxes(x_ref[:, t0:t0 + ts, :], 0, 1).reshape(ts * nb, d)
        _post_core(o, x, slice(t0 * nb, (t0 + ts) * nb), mod_ref, g1_ref, g2_ref, wr_ref, x1_ref, fin_ref, lg_ref)


def _post_glu_kernel(h_ref, ys_ref, dsk_ref, wg_ref, bg_ref, x_ref, mod_ref, g1_ref, g2_ref, wr_ref,
                     *rest):
    x1_ref, fin_ref, lg_ref = rest[-3:]
    d = x_ref.shape[-1]
    for rows in _sub_tiles(x_ref.shape[0]):
        h = jnp.concatenate([h_ref[g, rows, :] for g in range(h_ref.shape[0])], axis=-1)
        y = h * dsk_ref[...] + ys_ref[rows, :]
        z = _dot(jax.nn.gelu(y, approximate=True).astype(BF16), wg_ref[...]) + bg_ref[...]
        o = z[:, :d] * jax.nn.sigmoid(z[:, d:])
        _post_core(o, x_ref[rows, :], rows, mod_ref, g1_ref, g2_ref, wr_ref, x1_ref, fin_ref, lg_ref)


def _post_mixer(kernel, tok_inputs, consts, x, n_tok, x_off, mods, g1, g2, wr_t, tm, rows_per_mod, name,
                moe_total=None, moe_off=0, prev=None, x_spec=None):
    d = x.shape[-1]
    ne = wr_t.shape[0] // 2
    moe_total = n_tok if moe_total is None else moe_total
    tiles_per_mod = rows_per_mod // tm
    xo, mo = x_off // tm, moe_off // tm
    full = lambda a: pl.BlockSpec(a.shape, lambda i: (0,) * a.ndim)
    tile = pl.BlockSpec((tm, d), lambda i: (i, 0))
    mod_spec = pl.BlockSpec((None,) + mods.shape[1:], lambda i: (i // tiles_per_mod, 0, 0))
    x_spec = pl.BlockSpec((tm, d), lambda i: (i + xo, 0)) if x_spec is None else x_spec
    in_specs = ([spec for _, spec in tok_inputs] + [full(a) for a in consts]
                + [x_spec, mod_spec, full(g1), full(g2), full(wr_t)])
    args = [a for a, _ in tok_inputs] + list(consts) + [x, mods, g1, g2, wr_t]
    aliases = {}
    if prev is not None:
        aliases = {len(args): 1, len(args) + 1: 2}
        in_specs += [pl.BlockSpec(memory_space=pl.ANY)] * 2
        args += list(prev)
    return pl.pallas_call(
        kernel,
        out_shape=(jax.ShapeDtypeStruct((n_tok, d), F32),
                   jax.ShapeDtypeStruct((moe_total * PACK_ROWS, LANES), U32),
                   jax.ShapeDtypeStruct((ne, moe_total), F32)),
        grid=(n_tok // tm,),
        in_specs=in_specs,
        out_specs=(tile, pl.BlockSpec((tm * PACK_ROWS, LANES), lambda i: (i + mo, 0)),
                   pl.BlockSpec((ne, tm), lambda i: (0, i + mo))),
        input_output_aliases=aliases,
        compiler_params=_cparams(("arbitrary",)),
        name=name,
    )(*args)


def _route_kernel(lg_ref, bias_ref, eidx_ref, gate_ref, rank_ref, cnt_ref, tri_ref, base_ref):
    i = pl.program_id(0)
    ne, tt = lg_ref.shape
    gsz = ne // N_EXPERT_GROUPS
    shp = (N_EXPERT_GROUPS, gsz, tt)
    neg = -jnp.inf

    @pl.when(i == 0)
    def _():
        base_ref[...] = jnp.zeros_like(base_ref)
        r = lax.broadcasted_iota(jnp.int32, (tt, tt), 0)
        c = lax.broadcasted_iota(jnp.int32, (tt, tt), 1)
        tri_ref[...] = (r < c).astype(BF16)

    scores = jax.nn.sigmoid(lg_ref[...])
    s3 = scores.reshape(shp)
    b3 = (scores + bias_ref[...]).reshape(shp)
    io_e = lax.broadcasted_iota(jnp.int32, shp, 1)
    io_g = lax.broadcasted_iota(jnp.int32, shp, 0)
    io_flat = io_g * gsz + io_e
    m1 = b3.max(axis=1, keepdims=True)
    i1 = jnp.where(b3 == m1, io_e, gsz).min(axis=1, keepdims=True)
    m2 = jnp.where(io_e == i1, neg, b3).max(axis=1, keepdims=True)
    cur = jnp.broadcast_to(m1 + m2, shp)
    gsel = jnp.zeros(shp, jnp.bool_)
    for _ in range(TOPK_GROUPS):
        m = cur.max(axis=0, keepdims=True)
        gi = jnp.where(cur == m, io_g, N_EXPERT_GROUPS).min(axis=0, keepdims=True)
        hit = io_g == gi
        gsel = jnp.logical_or(gsel, hit)
        cur = jnp.where(hit, neg, cur)
    cand = jnp.where(gsel, b3, neg)
    sel = jnp.zeros(shp, jnp.bool_)
    eids, gts = [], []
    for _ in range(TOP_K):
        m = cand.max(axis=0, keepdims=True).max(axis=1, keepdims=True)
        ei = jnp.where(cand == m, io_flat, ne).min(axis=0, keepdims=True).min(axis=1, keepdims=True)
        hit = io_flat == ei
        gts.append(jnp.where(hit, s3, 0.0).sum(axis=0, keepdims=True).sum(axis=1, keepdims=True))
        eids.append(ei)
        sel = jnp.logical_or(sel, hit)
        cand = jnp.where(hit, neg, cand)
    gsum = gts[0]
    for g in gts[1:]:
        gsum = gsum + g
    self32 = sel.astype(F32).reshape(ne, tt)
    cnt = _dot(self32.astype(BF16), tri_ref[...]) + base_ref[...]
    cnt3 = cnt.reshape(shp)
    for k in range(TOP_K):
        hit = io_flat == eids[k]
        rk = jnp.where(hit, cnt3, 0.0).sum(axis=0, keepdims=True).sum(axis=1, keepdims=True)
        rank_ref[k:k + 1, :] = rk.reshape(1, tt).astype(jnp.int32)
        eidx_ref[k:k + 1, :] = eids[k].reshape(1, tt)
        gate_ref[k:k + 1, :] = (gts[k] / gsum * ROUTED_SCALE).reshape(1, tt)
    base_new = base_ref[...] + self32.sum(axis=1, keepdims=True)
    base_ref[...] = base_new
    cnt_ref[...] = jnp.broadcast_to(base_new, cnt_ref.shape)


def _route(logits_t, bias, tt, tok0, t):
    ne = logits_t.shape[0]
    off = tok0 // tt
    out_i = jax.ShapeDtypeStruct((TOP_K, t), jnp.int32)
    row = pl.BlockSpec((TOP_K, tt), lambda i: (0, i))
    return pl.pallas_call(
        _route_kernel,
        out_shape=(out_i, jax.ShapeDtypeStruct((TOP_K, t), F32), out_i,
                   jax.ShapeDtypeStruct((ne, 128), F32)),
        grid=(t // tt,),
        in_specs=[pl.BlockSpec((ne, tt), lambda i: (0, i + off)),
                  pl.BlockSpec((ne, 1), lambda i: (0, 0))],
        out_specs=(row, row, row, pl.BlockSpec((ne, 128), lambda i: (0, 0))),
        scratch_shapes=[pltpu.VMEM((tt, tt), BF16), pltpu.VMEM((ne, 1), F32)],
        compiler_params=_cparams(("arbitrary",)),
        name="moe_route",
    )(logits_t, bias.reshape(ne, 1))


def _dest_kernel(eidx_ref, rank_ref, start_ref, dest_ref):
    kk, tt = eidx_ref.shape
    ne = start_ref.shape[0]
    n_chunk, _, r = dest_ref.shape
    io_e = lax.broadcasted_iota(jnp.int32, (ne, tt), 0)
    start = start_ref[...]
    for k in range(kk):
        hit = io_e == eidx_ref[k:k + 1, :]
        dk = jnp.where(hit, start, 0).sum(axis=0, keepdims=True) + rank_ref[k:k + 1, :]
        for c in range(n_chunk):
            dest_ref[c, k:k + 1, :] = dk[:, c * r:(c + 1) * r]


def _dest_rows(eidx_t, rank_t, start, tt, r):
    kk, t = eidx_t.shape
    ne = start.shape[0]
    return pl.pallas_call(
        _dest_kernel,
        out_shape=jax.ShapeDtypeStruct((t // r, kk, r), jnp.int32),
        grid=(t // tt,),
        in_specs=[pl.BlockSpec((kk, tt), lambda i: (0, i)),
                  pl.BlockSpec((kk, tt), lambda i: (0, i)),
                  pl.BlockSpec((ne, 1), lambda i: (0, 0))],
        out_specs=pl.BlockSpec((tt // r, kk, r), lambda i: (i, 0, 0)),
        compiler_params=_cparams(("arbitrary",)),
        name="moe_dest",
    )(eidx_t, rank_t, start.reshape(ne, 1))


SC_CHUNK = 64


def _sc_mesh():
    return plsc.VectorSubcoreMesh(core_axis_name="c", subcore_axis_name="s")


def _sc_workers():
    info = plsc.get_sparse_core_info()
    return info.num_cores, info.num_cores * info.num_subcores


def _sc_scatter_rows(rows, dest, n_out, row0=0):
    n_chunk, kk, r = dest.shape
    nc, nw = _sc_workers()
    cpw = n_chunk // nw
    assert cpw * nw == n_chunk and cpw % 2 == 0 and row0 % r == 0 and row0 + n_chunk * r <= rows.shape[0]

    @functools.partial(
        pl.kernel, mesh=_sc_mesh(),
        out_type=jax.ShapeDtypeStruct((n_out,) + rows.shape[1:], rows.dtype),
        scratch_types=[pltpu.VMEM((2, kk, r), jnp.int32), pltpu.VMEM((2, r) + rows.shape[1:], rows.dtype),
                       pltpu.SemaphoreType.DMA((2,)), pltpu.SemaphoreType.DMA((2,))])
    def scatter(rows_hbm, dest_hbm, out_hbm, idx_v, rows_v, load_sem, scat_sem):
        c0 = (lax.axis_index("s") * nc + lax.axis_index("c")) * cpw

        def loads(c, b):
            return (pltpu.make_async_copy(dest_hbm.at[c], idx_v.at[b], load_sem.at[b]),
                    pltpu.make_async_copy(rows_hbm.at[pl.ds(row0 + c * r, r)], rows_v.at[b], load_sem.at[b]))

        def scat(b, k):
            return pltpu.make_async_copy(rows_v.at[b], out_hbm.at[idx_v.at[b, k]], scat_sem.at[b])

        for cp in loads(c0, 0):
            cp.start()

        @pl.loop(0, cpw, step=2)
        def _(ci):
            for b in range(2):
                c = c0 + ci + b
                for cp in loads(c, b):
                    cp.wait()
                for k in range(kk):
                    scat(b, k).start()

                @pl.when(ci + b >= 1)
                def _():
                    for k in range(kk):
                        scat(1 - b, k).wait()

                @pl.when(ci + b + 1 < cpw)
                def _():
                    for cp in loads(c + 1, 1 - b):
                        cp.start()

        for k in range(kk):
            scat((cpw - 1) % 2, k).wait()

    return scatter(rows, dest)


def _sc_gather_rows(src, dest):
    n_chunk, kk, r = dest.shape
    t = n_chunk * r
    nc, nw = _sc_workers()
    cpw = n_chunk // nw
    nbuf = 3
    assert cpw * nw == n_chunk and kk > nbuf

    @functools.partial(
        pl.kernel, mesh=_sc_mesh(),
        out_type=jax.ShapeDtypeStruct((kk, t) + src.shape[1:], src.dtype),
        scratch_types=[pltpu.VMEM((kk, r), jnp.int32), pltpu.VMEM((nbuf, r) + src.shape[1:], src.dtype),
                       pltpu.SemaphoreType.DMA((nbuf,)), pltpu.SemaphoreType.DMA((nbuf,))])
    def gather(src_hbm, dest_hbm, out_hbm, idx_v, rows_v, get_sem, put_sem):
        c0 = (lax.axis_index("s") * nc + lax.axis_index("c")) * cpw

        @pl.loop(0, cpw)
        def _(ci):
            c = c0 + ci
            pltpu.sync_copy(dest_hbm.at[c], idx_v)

            def get(k):
                return pltpu.make_async_copy(src_hbm.at[idx_v.at[k]], rows_v.at[k % nbuf], get_sem.at[k % nbuf])

            def put(k):
                return pltpu.make_async_copy(rows_v.at[k % nbuf], out_hbm.at[k, pl.ds(c * r, r)],
                                             put_sem.at[k % nbuf])

            for k in range(nbuf - 1):
                get(k).start()
            for k in range(kk):
                get(k).wait()
                put(k).start()
                if k + nbuf - 1 < kk:
                    if k >= 1:
                        put(k - 1).wait()
                    get(k + nbuf - 1).start()
            for k in range(kk - nbuf, kk):
                put(k).wait()

    return gather(src, dest)


def _expert_kernel(be_ref, nu_ref, x_ref, wg_ref, wu_ref, wd_ref, o_ref, wgu_s, wd_s):
    i = pl.program_id(0)
    tb = o_ref.shape[0] // PACK_ROWS

    @pl.when(i < nu_ref[0])
    def _():
        @pl.when(jnp.logical_or(i == 0, be_ref[i] != be_ref[jnp.maximum(i - 1, 0)]))
        def _():
            wgu_s[:, :D_EXPERT] = wg_ref[...].astype(BF16)
            wgu_s[:, D_EXPERT:] = wu_ref[...].astype(BF16)
            wd_s[...] = wd_ref[...].astype(BF16)

        x = jnp.concatenate([v.astype(BF16) for v in _unpack_load(x_ref, tb)], axis=-1)
        gu = _dot(x, wgu_s[...])
        g = gu[:, :D_EXPERT]
        h = g * jax.nn.sigmoid(g) * gu[:, D_EXPERT:]
        _pack_store(o_ref, _dot(h.astype(BF16), wd_s[...]))


def _experts(xs, blk_e, n_used, w_gate, w_up, w_down, layer, tb):
    rows = xs.shape[0] // PACK_ROWS
    _, ne, d, de = w_gate.shape
    nb = rows // tb
    row_map = lambda i, be, nu: (jnp.minimum(i, nu[0] - 1), 0)
    w_map = lambda i, be, nu: (layer, be[i], 0, 0)
    grid_spec = pltpu.PrefetchScalarGridSpec(
        num_scalar_prefetch=2,
        grid=(nb,),
        in_specs=[pl.BlockSpec((tb * PACK_ROWS, LANES), row_map),
                  pl.BlockSpec((None, None, d, de), w_map),
                  pl.BlockSpec((None, None, d, de), w_map),
                  pl.BlockSpec((None, None, de, d), w_map)],
        out_specs=pl.BlockSpec((tb * PACK_ROWS, LANES), row_map),
        scratch_shapes=[pltpu.VMEM((d, 2 * de), BF16), pltpu.VMEM((de, d), BF16)],
    )
    return pl.pallas_call(
        _expert_kernel,
        out_shape=jax.ShapeDtypeStruct(xs.shape, U32),
        grid_spec=grid_spec,
        compiler_params=_cparams(("arbitrary",)),
        name="moe_experts",
    )(blk_e, n_used, xs, w_gate, w_up, w_down)


def _combine_kernel(yk_ref, gate_ref, fin_ref, shgu_ref, shd_ref, x1_ref, mod_ref, g3_ref, *rest, fuse_next):
    if fuse_next:
        nmod_ref, ng0_ref = rest[0], rest[1]
        o_ref, h_ref = rest[-2], rest[-1]
    else:
        o_ref = rest[-1]
    tm, d = x1_ref.shape
    for rows in _sub_tiles(tm):
        n, r0 = SUB_ROWS, rows.start
        gates = gate_ref[rows, :]
        blocks = None
        for k in range(TOP_K):
            gk = gates[:, k:k + 1]
            terms = [gk * v for v in _unpack_load(yk_ref, n, lead=(k,), row0=r0)]
            blocks = terms if blocks is None else [a + b for a, b in zip(blocks, terms)]
        fin = jnp.concatenate([v.astype(BF16) for v in _unpack_load(fin_ref, n, row0=r0)], axis=-1)
        gu = _dot(fin, shgu_ref[...])
        g = gu[:, :D_EXPERT]
        hsh = g * jax.nn.sigmoid(g) * gu[:, D_EXPERT:]
        f = jnp.concatenate(blocks, axis=-1) + _dot(hsh.astype(BF16), shd_ref[...])
        x1 = x1_ref[rows, :]
        x2 = x1 + _rows(_mod_chunk(mod_ref, 5, d), x1) * _rms(f, g3_ref[...])
        if len(o_ref.shape) == 2:
            o_ref[rows, :] = x2
        else:
            nb = o_ref.shape[0]
            ts = SUB_ROWS // nb
            o_ref[:, r0 // nb:r0 // nb + ts, :] = jnp.swapaxes(x2.reshape(ts, nb, d), 0, 1)
        if fuse_next:
            hn = (_rms(x2, ng0_ref[...]) * (1.0 + _rows(_mod_chunk(nmod_ref, 1, d), x2))
                  + _rows(_mod_chunk(nmod_ref, 0, d), x2))
            for gi in range(h_ref.shape[0]):
                h_ref[gi, rows, :] = hn[:, gi * LANES:(gi + 1) * LANES]


def _combine(yk, gates, fin, shgu, shd, x1, mods, g3, tm, rows_per_mod, n_tok, x_off, yk_off, fin_off,
             batch_out=0, prev=None, out_rows=None, out_off=None, nxt=None):
    t, d = x1.shape
    out_rows = t if out_rows is None else out_rows
    out_off = x_off if out_off is None else out_off
    xo, yo, fo, oo = x_off // tm, yk_off // tm, fin_off // tm, out_off // tm
    tiles_per_mod = rows_per_mod // tm
    full = lambda a: pl.BlockSpec(a.shape, lambda i: (0,) * a.ndim)
    if batch_out:
        out_shape = [jax.ShapeDtypeStruct((batch_out, out_rows // batch_out, d), F32)]
        out_specs = [pl.BlockSpec((batch_out, tm // batch_out, d), lambda i: (0, i + oo, 0))]
    else:
        out_shape = [jax.ShapeDtypeStruct((out_rows, d), F32)]
        out_specs = [pl.BlockSpec((tm, d), lambda i: (i + oo, 0))]
    in_specs = [pl.BlockSpec((TOP_K, tm * PACK_ROWS, LANES), lambda i: (0, i + yo, 0)),
                pl.BlockSpec((tm, TOP_K), lambda i: (i + yo, 0)),
                pl.BlockSpec((tm * PACK_ROWS, LANES), lambda i: (i + fo, 0)),
                full(shgu), full(shd),
                pl.BlockSpec((tm, d), lambda i: (i + xo, 0)),
                pl.BlockSpec((None,) + mods.shape[1:], lambda i: ((i + xo) // tiles_per_mod, 0, 0)),
                full(g3)]
    args = [yk, gates, fin, shgu, shd, x1, mods, g3]
    if nxt is not None:
        in_specs += [full(nxt[0]), full(nxt[1])]
        args += list(nxt)
        out_shape.append(jax.ShapeDtypeStruct((d // LANES, out_rows, LANES), F32))
        out_specs.append(pl.BlockSpec((d // LANES, tm, LANES), lambda i: (0, i + oo, 0)))
    aliases = {}
    if prev is not None:
        for j, p in enumerate(prev if isinstance(prev, (tuple, list)) else [prev]):
            in_specs.append(pl.BlockSpec(memory_space=pl.ANY))
            aliases[len(args)] = j
            args.append(p)
    out = pl.pallas_call(
        functools.partial(_combine_kernel, fuse_next=nxt is not None),
        out_shape=tuple(out_shape),
        grid=(n_tok // tm,),
        in_specs=in_specs,
        out_specs=tuple(out_specs),
        input_output_aliases=aliases,
        compiler_params=_cparams(("arbitrary",)),
        name="moe_combine",
    )(*args)
    return out if nxt is not None else out[0]


def _moe(fin, logits_t, bias, w_gate, w_up, w_down, layer, tb, tok0, t):
    t_all = fin.shape[0] // PACK_ROWS
    ne = w_gate.shape[1]
    tt = 512
    eidx_t, gates_t, rank_t, cnt = _route(logits_t, bias, tt, tok0, t)
    counts = cnt[:, 0].astype(jnp.int32)
    padded = (counts + tb - 1) // tb * tb
    pad_end = jnp.cumsum(padded)
    pad_start = pad_end - padded
    nb = (t * TOP_K) // tb + ne
    n_used = pad_end[-1] // tb
    blk_start = jnp.arange(nb, dtype=jnp.int32) * tb
    blk = jnp.sum(pad_end[None, :] <= jnp.minimum(blk_start, pad_end[-1] - 1)[:, None], axis=1)
    blk_e = jnp.minimum(blk, ne - 1).astype(jnp.int32)
    dest = _dest_rows(eidx_t, rank_t, pad_start, tt, SC_CHUNK)
    xs = _sc_scatter_rows(fin.reshape(t_all, PACK_ROWS, LANES), dest, nb * tb, row0=tok0)
    ys = _experts(xs.reshape(nb * tb * PACK_ROWS, LANES), blk_e, n_used.reshape(1).astype(jnp.int32),
                  w_gate, w_up, w_down, layer, tb)
    yk = _sc_gather_rows(ys.reshape(nb * tb, PACK_ROWS, LANES), dest)
    return yk.reshape(TOP_K, t * PACK_ROWS, LANES), gates_t.T


def _pre_s5_kernel(x_ref, mod_ref, g0_ref, *refs):
    h_ref, xt_ref = refs[-2:]
    nb, tt, d = x_ref.shape
    x = jnp.swapaxes(x_ref[...], 0, 1).reshape(tt * nb, d)
    h = (_rms(x, g0_ref[...]) * (1.0 + _rows(_mod_chunk(mod_ref, 1, d), x))
         + _rows(_mod_chunk(mod_ref, 0, d), x))
    for g in range(h_ref.shape[0]):
        h_ref[g] = h[:, g * LANES:(g + 1) * LANES]
    xt_ref[...] = x


def _pre_s5(x, mods, g0, n_total, t_off, prev, tt):
    nb, n, d = x.shape
    off = t_off // tt
    out_shape = (jax.ShapeDtypeStruct((d // LANES, n_total * nb, LANES), F32),
                 jax.ShapeDtypeStruct((n_total * nb, d), F32))
    out_specs = (pl.BlockSpec((d // LANES, tt * nb, LANES), lambda i: (0, i + off, 0)),
                 pl.BlockSpec((tt * nb, d), lambda i: (i + off, 0)))
    in_specs = [pl.BlockSpec((nb, tt, d), lambda i: (0, i, 0)),
                pl.BlockSpec(mods.shape, lambda i: (0, 0)),
                pl.BlockSpec(g0.shape, lambda i: (0, 0))]
    args = (x, mods, g0)
    aliases = {}
    if prev is not None:
        in_specs += [pl.BlockSpec(memory_space=pl.ANY)] * 2
        args += tuple(prev)
        aliases = {3: 0, 4: 1}
    return pl.pallas_call(
        _pre_s5_kernel,
        out_shape=out_shape,
        grid=(n // tt,),
        in_specs=in_specs,
        out_specs=out_specs,
        input_output_aliases=aliases,
        compiler_params=_cparams(("arbitrary",)),
        name="pre_s5",
    )(*args)


def _s5_scan_kernel(h_ref, bm_ref, cm_ref, lam_ref, y_ref, bu0, bu1, xb0, xb1, st_ref):
    first = jnp.logical_and(jnp.logical_and(pl.program_id(0) == 0, pl.program_id(1) == 0), pl.program_id(2) == 0)
    dr = pl.program_id(1)
    s = pl.program_id(2)
    tc, nb, cw = h_ref.shape
    half = st_ref.shape[1] // 2

    @pl.when(first)
    def _():
        for r in (bu0, bu1, xb0, xb1, st_ref):
            r[...] = jnp.zeros_like(r)

    def stages(bu_w, bu_r, xb_w, xb_r):
        lr = jnp.broadcast_to(lam_ref[0:1, :], (nb, half))
        li = jnp.broadcast_to(lam_ref[1:2, :], (nb, half))
        fresh = s == 1
        xr = jnp.where(fresh, 0.0, st_ref[:, 0:half])
        xi = jnp.where(fresh, 0.0, st_ref[:, half:2 * half])
        tp = tc // S5_PIECES
        for p in range(S5_PIECES):
            for i in range(p * tp, (p + 1) * tp):
                t = i + dr * (tc - 1 - 2 * i)
                rows = pl.ds(pl.multiple_of(t * nb, nb), nb)
                nr = lr * xr - li * xi + bu_r[rows, 0:half]
                ni = lr * xi + li * xr + bu_r[rows, half:2 * half]
                xb_w[rows, 0:half] = nr.astype(BF16)
                xb_w[rows, half:2 * half] = ni.astype(BF16)
                xr, xi = nr, ni
            ts = slice(p * tp, (p + 1) * tp)
            mr = slice(p * tp * nb, (p + 1) * tp * nb)
            y_ref[ts] = _dot(xb_r[mr, :], cm_ref[...]).reshape(tp, nb, cw)
            bu_w[mr, :] = _dot(h_ref[ts].reshape(tp * nb, cw).astype(BF16), bm_ref[...])
        st_ref[:, 0:half] = xr
        st_ref[:, half:2 * half] = xi

    @pl.when(s % 2 == 0)
    def _():
        stages(bu0, bu1, xb1, xb0)

    @pl.when(s % 2 == 1)
    def _():
        stages(bu1, bu0, xb0, xb1)


def _s5_scan(h_all, n_ctx, bm, cm, lam, tc):
    ng, nt, nb, cw = h_all.shape
    nl = nt - n_ctx
    assert cw == S5_GROUPS_PER_BLOCK * S5_GROUP
    sw = 2 * S5_GROUPS_PER_BLOCK * S5_STATE
    ncc, n = n_ctx // tc, nt // tc

    def chunk(dr, j):
        j = jnp.clip(j, 0, n - 1)
        rev = jnp.where(j < ncc, ncc - 1 - j, n - 1 - (j - ncc))
        return jnp.where(dr == 0, j, rev)

    def out_map(g, dr, s):
        return (dr, g, chunk(dr, jnp.clip(s - 2, ncc, n - 1)) - ncc, 0, 0)

    return pl.pallas_call(
        _s5_scan_kernel,
        out_shape=jax.ShapeDtypeStruct((2, ng, nl, nb, cw), F32),
        grid=(ng, 2, n + 2),
        in_specs=[pl.BlockSpec((None, tc, nb, cw), lambda g, dr, s: (g, chunk(dr, s), 0, 0)),
                  pl.BlockSpec((None, None, cw, sw), lambda g, dr, s: (dr, g, 0, 0)),
                  pl.BlockSpec((None, None, sw, cw), lambda g, dr, s: (dr, g, 0, 0)),
                  pl.BlockSpec((None, None, 2, sw // 2), lambda g, dr, s: (dr, g, 0, 0))],
        out_specs=pl.BlockSpec((None, None, tc, nb, cw), out_map),
        scratch_shapes=[pltpu.VMEM((tc * nb, sw), F32), pltpu.VMEM((tc * nb, sw), F32),
                        pltpu.VMEM((tc * nb, sw), BF16), pltpu.VMEM((tc * nb, sw), BF16),
                        pltpu.VMEM((nb, sw), F32)],
        compiler_params=_cparams(("arbitrary", "arbitrary", "arbitrary")),
        name="s5_scan",
    )(h_all, bm, cm, lam)


def _s5_params(lam_re, lam_im, log_step, b_re, b_im, c_re, c_im):
    g, p = lam_re.shape[1:]
    gb = S5_GROUPS_PER_BLOCK
    nblk = g // gb
    step = jnp.exp(log_step)[..., None]
    mag = jnp.exp(lam_re * step)
    lb_re = mag * jnp.cos(lam_im * step)
    lb_im = mag * jnp.sin(lam_im * step)
    den = lam_re * lam_re + lam_im * lam_im
    f_re = ((lb_re - 1.0) * lam_re + lb_im * lam_im) / den
    f_im = (lb_im * lam_re - (lb_re - 1.0) * lam_im) / den
    bb_re = f_re[..., None] * b_re - f_im[..., None] * b_im
    bb_im = f_re[..., None] * b_im + f_im[..., None] * b_re
    eye = jnp.eye(gb, dtype=F32)

    def in_map(w):
        w = w.reshape(2, nblk, gb, p, S5_GROUP)
        return jnp.einsum("dnapi,ab->dnaibp", w, eye).reshape(2, nblk, gb * S5_GROUP, gb * p)

    def out_map(w):
        w = w.reshape(2, nblk, gb, S5_GROUP, p)
        return jnp.einsum("dnaip,ab->dnapbi", w, eye).reshape(2, nblk, gb * p, gb * S5_GROUP)

    bm = jnp.concatenate([in_map(bb_re), in_map(bb_im)], axis=-1).astype(BF16)
    cm = jnp.concatenate([out_map(c_re), out_map(-c_im)], axis=-2).astype(BF16)
    lam = jnp.stack([lb_re.reshape(2, nblk, gb * p), lb_im.reshape(2, nblk, gb * p)], axis=2)
    return bm, cm, lam


S5_Q = 16
S5_STEP_GROUPS = 4


def _s5c_kernel(u_ref, toep_ref, win_ref, wout_ref, lamq_ref, y_ref, sre_ref, sim_ref, xin_ref, *, nb, n_ctx_chunks):
    npair, n_rows, kw = u_ref.shape
    half = kw // 2
    sw = 2 * S5_STATE
    n = n_rows // nb
    ncc = n_ctx_chunks
    for dr in range(2):
        for p in range(npair):
            s = _dot(u_ref[p], win_ref[dr, p])
            sre_ref[dr, :, p * sw:(p + 1) * sw] = s[:, :sw]
            sim_ref[dr, :, p * sw:(p + 1) * sw] = s[:, sw:]
    w = npair * sw
    lr = [jnp.broadcast_to(lamq_ref[dr, 0:1, :], (nb, w)) for dr in range(2)]
    li = [jnp.broadcast_to(lamq_ref[dr, 1:2, :], (nb, w)) for dr in range(2)]

    def step(j, carry):
        rev = jnp.where(j < ncc, ncc - 1 - j, n - 1 - (j - ncc))
        out = []
        for dr in range(2):
            xr, xi = carry[dr]
            c = j if dr == 0 else rev
            rows = pl.ds(pl.multiple_of(c * nb, nb), nb)
            for p in range(npair):
                xin_ref[dr, rows, 2 * p * sw:(2 * p + 1) * sw] = xr[:, p * sw:(p + 1) * sw].astype(BF16)
                xin_ref[dr, rows, (2 * p + 1) * sw:(2 * p + 2) * sw] = xi[:, p * sw:(p + 1) * sw].astype(BF16)
            nr = lr[dr] * xr - li[dr] * xi + sre_ref[dr, rows, :]
            ni = lr[dr] * xi + li[dr] * xr + sim_ref[dr, rows, :]
            out.append((nr, ni))
        return tuple(out)

    zero = jnp.zeros((nb, w), F32)
    lax.fori_loop(0, n, step, ((zero, zero), (zero, zero)), unroll=2)
    lat = slice(ncc * nb, n_rows)
    for p in range(npair):
        acc = None
        for dr in range(2):
            intra = jnp.concatenate([_dot(u_ref[p, lat, 0:half], toep_ref[dr, 2 * p]),
                                     _dot(u_ref[p, lat, half:kw], toep_ref[dr, 2 * p + 1])], axis=-1)
            term = intra + _dot(xin_ref[dr, lat, 2 * p * sw:(2 * p + 2) * sw], wout_ref[dr, p])
            acc = term if acc is None else acc + term
        y_ref[p] = acc


def _s5_chunked(u, toep, win, wout, lamq, nb, n_ctx):
    n_pairs, n_rows, kw = u.shape
    pp = S5_STEP_GROUPS // 2
    ncc = n_ctx // S5_Q
    lat_rows = n_rows - ncc * nb
    sw = 2 * S5_STATE
    return pl.pallas_call(
        functools.partial(_s5c_kernel, nb=nb, n_ctx_chunks=ncc),
        out_shape=jax.ShapeDtypeStruct((n_pairs, lat_rows, kw), F32),
        grid=(n_pairs // pp,),
        in_specs=[pl.BlockSpec((pp, n_rows, kw), lambda i: (i, 0, 0)),
                  pl.BlockSpec((2, 2 * pp, kw // 2, kw // 2), lambda i: (0, i, 0, 0)),
                  pl.BlockSpec((2, pp, kw, 2 * sw), lambda i: (0, i, 0, 0)),
                  pl.BlockSpec((2, pp, 2 * sw, kw), lambda i: (0, i, 0, 0)),
                  pl.BlockSpec((2, None, 2, pp * sw), lambda i: (0, i, 0, 0))],
        out_specs=pl.BlockSpec((pp, lat_rows, kw), lambda i: (i, 0, 0)),
        scratch_shapes=[pltpu.VMEM((2, n_rows, pp * sw), F32), pltpu.VMEM((2, n_rows, pp * sw), F32),
                        pltpu.VMEM((2, n_rows, 2 * pp * sw), BF16)],
        compiler_params=_cparams(("arbitrary",)),
        name="s5_chunked",
    )(u, toep, win, wout, lamq)


def _s5c_params(lam_re, lam_im, log_step, b_re, b_im, c_re, c_im):
    hp = lax.Precision.HIGHEST
    q = S5_Q
    _, g, p = lam_re.shape
    ni = b_re.shape[-1]
    step = jnp.exp(log_step)[..., None]
    ar, ai = lam_re * step, lam_im * step
    tau = jnp.arange(q + 1, dtype=F32)[:, None, None, None]
    mag = jnp.exp(tau * ar)
    pr, pi = mag * jnp.cos(tau * ai), mag * jnp.sin(tau * ai)
    den = lam_re * lam_re + lam_im * lam_im
    f_re = ((pr[1] - 1.0) * lam_re + pi[1] * lam_im) / den
    f_im = (pi[1] * lam_re - (pr[1] - 1.0) * lam_im) / den
    bb_re = f_re[..., None] * b_re - f_im[..., None] * b_im
    bb_im = f_re[..., None] * b_im + f_im[..., None] * b_re
    cp_re = c_re[None] * pr[:, :, :, None, :] - c_im[None] * pi[:, :, :, None, :]
    cp_im = c_re[None] * pi[:, :, :, None, :] + c_im[None] * pr[:, :, :, None, :]
    taps = (jnp.einsum("tdgop,dgpi->tdgoi", cp_re[:q], bb_re, precision=hp)
            - jnp.einsum("tdgop,dgpi->tdgoi", cp_im[:q], bb_im, precision=hp))
    s_idx = jnp.arange(q)[:, None]
    t_idx = jnp.arange(q)[None, :]

    def toeplitz(dr):
        lag = (t_idx - s_idx) if dr == 0 else (s_idx - t_idx)
        k = taps[:, dr][jnp.clip(lag, 0, q - 1)]
        k = jnp.where((lag >= 0)[:, :, None, None, None], k, 0.0)
        return k.transpose(2, 0, 4, 1, 3).reshape(g, q * ni, q * ni)

    toep = jnp.stack([toeplitz(0), toeplitz(1)]).astype(BF16)

    def state_in(dr):
        e = (q - 1 - jnp.arange(q)) if dr == 0 else jnp.arange(q)
        er, ei = pr[e, dr], pi[e, dr]
        br, bi = bb_re[dr].transpose(0, 2, 1), bb_im[dr].transpose(0, 2, 1)
        w_re = er[:, :, None, :] * br[None] - ei[:, :, None, :] * bi[None]
        w_im = er[:, :, None, :] * bi[None] + ei[:, :, None, :] * br[None]
        fl = lambda a: a.transpose(1, 0, 2, 3).reshape(g, q * ni, p)
        return fl(w_re), fl(w_im)

    def state_out(dr):
        f = (jnp.arange(q) + 1) if dr == 0 else (q - jnp.arange(q))
        fl = lambda a: a.transpose(1, 3, 0, 2).reshape(g, p, q * ni)
        return fl(cp_re[f, dr]), fl(-cp_im[f, dr])

    z = lambda *shape: jnp.zeros(shape, F32)

    def pair_in(dr):
        w_re, w_im = state_in(dr)
        a_re, b_re_, a_im, b_im_ = w_re[0::2], w_re[1::2], w_im[0::2], w_im[1::2]
        zz = z(g // 2, q * ni, p)
        top = jnp.concatenate([a_re, zz, a_im, zz], axis=-1)
        bot = jnp.concatenate([zz, b_re_, zz, b_im_], axis=-1)
        return jnp.concatenate([top, bot], axis=1)

    def pair_out(dr):
        w_re, w_im = state_out(dr)
        zz = z(g // 2, p, q * ni)
        rows = [jnp.concatenate([w_re[0::2], zz], axis=-1), jnp.concatenate([zz, w_re[1::2]], axis=-1),
                jnp.concatenate([w_im[0::2], zz], axis=-1), jnp.concatenate([zz, w_im[1::2]], axis=-1)]
        return jnp.concatenate(rows, axis=1)

    win = jnp.stack([pair_in(0), pair_in(1)]).astype(BF16)
    wout = jnp.stack([pair_out(0), pair_out(1)]).astype(BF16)
    ng = S5_STEP_GROUPS
    lamq = jnp.stack([pr[q].reshape(2, g // ng, ng * p), pi[q].reshape(2, g // ng, ng * p)], axis=2)
    return toep, win, wout, lamq


def _rope_tables(n_tokens):
    rows = n_tokens // GRID_W
    row = jnp.repeat(jnp.arange(rows), GRID_W).astype(F32)
    col = jnp.tile(jnp.arange(GRID_W), rows).astype(F32)
    n_freq = ROPE_DIM // 4
    inv_freq = ROPE_BASE ** (-jnp.arange(n_freq, dtype=F32) / n_freq)
    ang = jnp.concatenate([row[:, None] * inv_freq, col[:, None] * inv_freq], axis=-1)
    cos, sin = jnp.cos(ang), jnp.sin(ang)
    z = jnp.zeros((n_tokens, 128 - ROPE_DIM), F32)
    return (jnp.concatenate([cos, cos, z], axis=-1), jnp.concatenate([-sin, sin, z], axis=-1))


def _router_halves(w_router):
    wt = w_router.T
    hi = wt.astype(BF16)
    lo = (wt - hi.astype(F32)).astype(BF16)
    return jnp.concatenate([hi, lo], axis=0)


def _split_pairs(w):
    ev, od = w[..., 0::2], w[..., 1::2]
    z = jnp.zeros(w.shape[:-1] + (128 - ROPE_DIM,), w.dtype)
    return jnp.concatenate([ev, od, z], axis=-1), jnp.concatenate([od, ev, z], axis=-1)


def _mla_weights(w_dqkv, w_uq, w_ukv):
    kp, kps = _split_pairs(w_dqkv[:, Q_LORA + KV_LORA:])
    wd = jnp.concatenate([w_dqkv[:, :Q_LORA + KV_LORA], kp, kps], axis=-1).astype(BF16)
    wq3 = w_uq.reshape(Q_LORA, MLA_HEADS, NOPE_DIM + ROPE_DIM)
    qp, qps = _split_pairs(wq3[:, :, NOPE_DIM:])
    wq = jnp.concatenate([wq3[:, :, :NOPE_DIM].reshape(Q_LORA, -1), qp.reshape(Q_LORA, -1),
                          qps.reshape(Q_LORA, -1)], axis=-1).astype(BF16)
    wkv3 = w_ukv.reshape(KV_LORA, MLA_HEADS, NOPE_DIM + V_DIM)
    wkv = jnp.concatenate([wkv3[:, :, :NOPE_DIM].reshape(KV_LORA, -1),
                           wkv3[:, :, NOPE_DIM:].reshape(KV_LORA, -1)], axis=-1).astype(BF16)
    return wd, wq, wkv


@jax.jit
def kernel(x, c, ctx, c_ctx, ada_w, ada_b, norm_g, mla_w_dqkv, mla_g_q, mla_g_kv, mla_w_uq, mla_w_ukv, mla_w_o, s5_lam_re, s5_lam_im, s5_log_step, s5_b_re, s5_b_im, s5_c_re, s5_c_im, s5_d, s5_w_glu, s5_b_glu, moe_w_router, moe_bias, moe_w_gate, moe_w_up, moe_w_down, sh_w_gate, sh_w_up, sh_w_down):
    b, l, d = x.shape
    n_ctx = ctx.shape[1]
    assert ada_w.shape[0] == 2 and b % 8 == 0
    ta = 256
    tm = 512
    tb = 512
    row = lambda v: v.reshape(1, -1)

    n_rows = (b + 1 + 7) // 8 * 8
    cvec = jnp.zeros((n_rows, d), F32).at[:b].set(c).at[b].set(c_ctx)
    mods = _ada_mods(cvec, ada_w, ada_b)

    def shared_weights(i):
        shgu = jnp.concatenate([sh_w_gate[i], sh_w_up[i]], axis=-1).astype(BF16)
        return shgu, sh_w_down[i].astype(BF16)

    mod_lat = mods[0, :b].reshape(b, 1, N_MOD * d)
    mod_ctx = mods[0, b].reshape(1, 1, N_MOD * d)
    wd, wq, wkv = _mla_weights(mla_w_dqkv[0], mla_w_uq[0], mla_w_ukv[0])
    cos_l, sin_l = _rope_tables(l)
    cos_c = jnp.concatenate([jnp.ones((n_ctx, ROPE_DIM), F32), jnp.zeros((n_ctx, 128 - ROPE_DIM), F32)], -1)
    sin_c = jnp.zeros((n_ctx, 128), F32)
    pre = functools.partial(_pre_mla, g0=row(norm_g[0, 0]), wd=wd, gq=row(mla_g_q[0]), gkv=row(mla_g_kv[0]),
                            wq=wq, wkv=wkv, tm=ta)
    q_c, k_c, v_c = pre(ctx, mod_ctx, cos_t=cos_c, sin_t=sin_c)
    q_l, k_l, v_l = pre(x, mod_lat, cos_t=cos_l, sin_t=sin_l)
    o_l = _attention(q_l, [k_c, k_l], [v_c, v_l], 2 * ta)
    o_c = _attention(q_c, [k_c], [v_c], n_ctx)

    wo = mla_w_o[0].astype(BF16)
    wr_t = _router_halves(moe_w_router[0])
    g1, g2, g3 = row(norm_g[0, 1]), row(norm_g[0, 2]), row(norm_g[0, 3])
    post = functools.partial(_post_mixer, _post_proj_kernel, consts=[wo], g1=g1, g2=g2, wr_t=wr_t, tm=tm,
                             name="post_mla")
    o_spec = pl.BlockSpec((tm, o_l.shape[-1]), lambda i: (i, 0))
    n_moe = b * (n_ctx + l)
    x1_c, fin, lg = post([(o_c.reshape(b * n_ctx, -1), o_spec)], x=ctx.reshape(b * n_ctx, d),
                         n_tok=b * n_ctx, x_off=0, mods=mod_ctx, rows_per_mod=b * n_ctx, moe_total=n_moe)
    tt = tm // b
    mod_lat_tm = mods[0, :b][None]
    x1_l, fin, lg = _post_mixer(
        _post_proj_tm_kernel, [(o_l, pl.BlockSpec((b, tt, o_l.shape[-1]), lambda i: (0, i, 0)))], [wo],
        x=x, n_tok=b * l, x_off=0, mods=mod_lat_tm, g1=g1, g2=g2, wr_t=wr_t, tm=tm, rows_per_mod=b * l,
        name="post_mla", moe_total=n_moe, moe_off=b * n_ctx, prev=(fin, lg),
        x_spec=pl.BlockSpec((b, tt, d), lambda i: (0, i, 0)))
    shgu, shd = shared_weights(0)
    lat_a = (l // 2) * b
    n_a = b * n_ctx + lat_a
    moe = functools.partial(_moe, fin, lg, moe_bias[0], moe_w_gate, moe_w_up, moe_w_down, 0, tb)
    yk_a, gates_a = moe(0, n_a)
    yk_b, gates_b = moe(n_a, n_moe - n_a)
    comb = functools.partial(_combine, fin=fin, shgu=shgu, shd=shd, g3=g3, tm=tm)
    x2_c = comb(yk_a, gates_a, x1=x1_c, mods=mod_ctx, rows_per_mod=b * n_ctx, n_tok=b * n_ctx, x_off=0, yk_off=0,
                fin_off=0)

    n_all = n_ctx + l
    mod_lat = mods[1, :b]
    mod_ctx = jnp.broadcast_to(mods[1, b][None], (b, N_MOD * d))
    g0 = row(norm_g[1, 0])
    h, xt = _pre_s5(x2_c.reshape(b, n_ctx, d), mod_ctx, g0, n_all, 0, None, tt)
    comb_l = functools.partial(comb, x1=x1_l, mods=mod_lat_tm, rows_per_mod=b * l, out_rows=n_all * b,
                               nxt=(mod_lat, g0))
    xt, h = comb_l(yk_a, gates_a, n_tok=lat_a, x_off=0, yk_off=b * n_ctx, fin_off=b * n_ctx,
                   out_off=n_ctx * b, prev=(xt, h))
    xt, h = comb_l(yk_b, gates_b, n_tok=b * l - lat_a, x_off=lat_a, yk_off=0, fin_off=n_a,
                   out_off=n_ctx * b + lat_a, prev=(xt, h))
    toep, win, wout, lamq = _s5c_params(s5_lam_re[0], s5_lam_im[0], s5_log_step[0], s5_b_re[0], s5_b_im[0],
                                        s5_c_re[0], s5_c_im[0])
    ng = d // LANES
    q = S5_Q
    gpb = LANES // S5_GROUP
    u = h.reshape(ng, n_all // q, q, b, gpb // 2, 2, S5_GROUP).transpose(0, 4, 1, 3, 5, 2, 6)
    u = u.reshape(ng * gpb // 2, (n_all // q) * b, 2 * q * S5_GROUP).astype(BF16)
    yc = _s5_chunked(u, toep, win, wout, lamq, b, n_ctx)
    y = yc.reshape(ng * gpb // 2, l // q, b, 2, q, S5_GROUP).transpose(1, 4, 2, 0, 3, 5).reshape(l * b, d)
    g1, g2, g3 = row(norm_g[1, 1]), row(norm_g[1, 2]), row(norm_g[1, 3])
    lat0 = n_ctx * b // tm
    x1, fin, lg = _post_mixer(
        _post_glu_kernel,
        [(h, pl.BlockSpec((ng, tm, LANES), lambda i: (0, i + lat0, 0))),
         (y, pl.BlockSpec((tm, d), lambda i: (i, 0)))],
        [row(s5_d[0]), s5_w_glu[0].astype(BF16), row(s5_b_glu[0])],
        x=xt, n_tok=l * b, x_off=n_ctx * b, mods=mod_lat[None], g1=g1, g2=g2, wr_t=_router_halves(moe_w_router[1]), tm=tm,
        rows_per_mod=l * b, name="post_s5")
    shgu, shd = shared_weights(1)
    n_h = (l // 2) * b
    moe = functools.partial(_moe, fin, lg, moe_bias[1], moe_w_gate, moe_w_up, moe_w_down, 1, tb)
    yk_a, gates_a = moe(0, n_h)
    yk_b, gates_b = moe(n_h, l * b - n_h)
    comb = functools.partial(_combine, fin=fin, shgu=shgu, shd=shd, x1=x1, mods=mod_lat[None], g3=g3, tm=tm,
                             rows_per_mod=l * b, yk_off=0, batch_out=b)
    out = comb(yk_a, gates_a, n_tok=n_h, x_off=0, fin_off=0)
    return comb(yk_b, gates_b, n_tok=l * b - n_h, x_off=n_h, fin_off=n_h, prev=out)
```

```python
import functools

import jax
import jax.numpy as jnp
from jax import lax
from jax.experimental import pallas as pl
from jax.experimental.pallas import tpu as pltpu
from jax.experimental.pallas import tpu_sc as plsc

F32 = jnp.float32
BF16 = jnp.bfloat16
U32 = jnp.uint32

N_MOD = 6
NORM_EPS = 1e-6
LOG2_E = 1.4426950408889634
GRID_W = 64
MLA_HEADS = 8
Q_LORA = 384
KV_LORA = 256
NOPE_DIM = 128
ROPE_DIM = 64
V_DIM = 128
V_PAD = 256
ROPE_BASE = 10000.0
QK_PAD = 256
S5_GROUP = 16
S5_STATE = 64
S5_GROUPS_PER_BLOCK = 8
S5_PIECES = 4
N_EXPERTS = 64
TOP_K = 8
N_EXPERT_GROUPS = 8
TOPK_GROUPS = 4
D_EXPERT = 256
ROUTED_SCALE = 2.5

VMEM_LIMIT = 56 * 1024 * 1024


def _cparams(sem):
    return pltpu.CompilerParams(dimension_semantics=sem, vmem_limit_bytes=VMEM_LIMIT)


def _rms(x, g):
    return x * lax.rsqrt(jnp.mean(x * x, axis=-1, keepdims=True) + NORM_EPS) * g


def _rows(v, like):
    r = v.shape[0]
    if r == 1:
        return v
    tm, d = like.shape
    return jnp.broadcast_to(v[None], (tm // r, r, d)).reshape(tm, d)


def _mod_chunk(mod_ref, j, d):
    return mod_ref[:, j * d:(j + 1) * d]


def _dot(a, b):
    return jnp.dot(a, b, preferred_element_type=F32)


PACK_ROWS = 4
LANES = 128


def _pack_store(ref, val, lead=(), row0=0):
    n = val.shape[0]
    bits = lax.bitcast_convert_type(val.astype(BF16).astype(F32), U32)
    for s in range(PACK_ROWS):
        lo = bits[:, s * LANES:(s + 1) * LANES] >> 16
        hi = bits[:, (s + PACK_ROWS) * LANES:(s + PACK_ROWS + 1) * LANES] & jnp.uint32(0xFFFF0000)
        ref[lead + (pl.ds(row0 * PACK_ROWS + s, n, stride=PACK_ROWS), slice(None))] = lo | hi


def _unpack_load(ref, n, lead=(), row0=0):
    los, his = [], []
    for s in range(PACK_ROWS):
        w = ref[lead + (pl.ds(row0 * PACK_ROWS + s, n, stride=PACK_ROWS), slice(None))]
        los.append(lax.bitcast_convert_type(w << 16, F32))
        his.append(lax.bitcast_convert_type(w & jnp.uint32(0xFFFF0000), F32))
    return los + his


def _ada_kernel(c_ref, w_ref, b_ref, o_ref):
    c = c_ref[...]
    s = c * jax.nn.sigmoid(c)
    o_ref[...] = jnp.dot(s, w_ref[...], preferred_element_type=F32,
                         precision=lax.Precision.HIGHEST) + b_ref[...]


def _ada_mods(cvec, ada_w, ada_b):
    depth, d, n = ada_w.shape
    rows = cvec.shape[0]
    tn = 1536
    return pl.pallas_call(
        _ada_kernel,
        out_shape=jax.ShapeDtypeStruct((depth, rows, n), F32),
        grid=(depth, n // tn),
        in_specs=[pl.BlockSpec((rows, d), lambda l, j: (0, 0)),
                  pl.BlockSpec((None, d, tn), lambda l, j: (l, 0, j)),
                  pl.BlockSpec((None, 1, tn), lambda l, j: (l, 0, j))],
        out_specs=pl.BlockSpec((None, rows, tn), lambda l, j: (l, 0, j)),
        compiler_params=_cparams(("arbitrary", "arbitrary")),
        name="ada_mods",
    )(cvec, ada_w, ada_b.reshape(depth, 1, n))


def _pre_mla_kernel(x_ref, mod_ref, g0_ref, wd_ref, gq_ref, gkv_ref, wq_ref, wkv_ref, cos_ref, sin_ref,
                    q_ref, k_ref, v_ref):
    d = x_ref.shape[-1]
    x = x_ref[...]
    h = _rms(x, g0_ref[...]) * (1.0 + _mod_chunk(mod_ref, 1, d)) + _mod_chunk(mod_ref, 0, d)
    a = _dot(h.astype(BF16), wd_ref[...])
    cq = _rms(a[:, :Q_LORA], gq_ref[...])
    ckv = _rms(a[:, Q_LORA:Q_LORA + KV_LORA], gkv_ref[...])
    rd = ROPE_DIM
    cos = cos_ref[:, 0:rd]
    sin = sin_ref[:, 0:rd]
    o = Q_LORA + KV_LORA
    k_rot = (a[:, o:o + rd] * cos + a[:, o + rd:o + 2 * rd] * sin).astype(BF16)
    qa = _dot(cq.astype(BF16), wq_ref[...])
    kva = _dot(ckv.astype(BF16), wkv_ref[...])
    hw = MLA_HEADS * 128
    hr = MLA_HEADS * rd
    zpad = jnp.zeros((x.shape[0], QK_PAD - NOPE_DIM - rd), BF16)
    scale = (NOPE_DIM + ROPE_DIM) ** -0.5 * LOG2_E
    for hd in range(MLA_HEADS):
        lo = hd * 128
        q_rot = qa[:, hw + hd * rd:hw + (hd + 1) * rd] * cos + qa[:, hw + hr + hd * rd:hw + hr + (hd + 1) * rd] * sin
        q_ref[:, hd * QK_PAD:hd * QK_PAD + 128] = (qa[:, lo:lo + 128] * scale).astype(BF16)
        q_ref[:, hd * QK_PAD + 128:hd * QK_PAD + 128 + rd] = (q_rot * scale).astype(BF16)
        q_ref[:, hd * QK_PAD + 128 + rd:(hd + 1) * QK_PAD] = zpad
        k_ref[:, hd * QK_PAD:hd * QK_PAD + 128] = kva[:, lo:lo + 128].astype(BF16)
        k_ref[:, hd * QK_PAD + 128:hd * QK_PAD + 128 + rd] = k_rot
        k_ref[:, hd * QK_PAD + 128 + rd:(hd + 1) * QK_PAD] = zpad
        v_ref[:, hd * V_PAD:hd * V_PAD + V_DIM] = kva[:, hw + lo:hw + lo + 128].astype(BF16)
        v_ref[:, hd * V_PAD + V_DIM:(hd + 1) * V_PAD] = jnp.ones((x.shape[0], V_PAD - V_DIM), BF16)


def _pre_mla(x, mods, g0, wd, gq, gkv, wq, wkv, cos_t, sin_t, tm):
    b, n, d = x.shape
    nb_mod = mods.shape[0]
    full = lambda a: pl.BlockSpec(a.shape, lambda i, j: (0,) * a.ndim)
    mod_map = (lambda i, j: (i, 0, 0)) if nb_mod > 1 else (lambda i, j: (0, 0, 0))
    qk_w = MLA_HEADS * QK_PAD
    v_w = MLA_HEADS * V_PAD
    return pl.pallas_call(
        _pre_mla_kernel,
        out_shape=(jax.ShapeDtypeStruct((b, n, qk_w), BF16),
                   jax.ShapeDtypeStruct((b, n, qk_w), BF16),
                   jax.ShapeDtypeStruct((b, n, v_w), BF16)),
        grid=(b, n // tm),
        in_specs=[pl.BlockSpec((None, tm, d), lambda i, j: (i, j, 0)),
                  pl.BlockSpec((None, 1, mods.shape[-1]), mod_map),
                  full(g0), full(wd), full(gq), full(gkv), full(wq), full(wkv),
                  pl.BlockSpec((tm, 128), lambda i, j: (j, 0)),
                  pl.BlockSpec((tm, 128), lambda i, j: (j, 0))],
        out_specs=(pl.BlockSpec((None, tm, qk_w), lambda i, j: (i, j, 0)),
                   pl.BlockSpec((None, tm, qk_w), lambda i, j: (i, j, 0)),
                   pl.BlockSpec((None, tm, v_w), lambda i, j: (i, j, 0))),
        compiler_params=_cparams(("arbitrary", "arbitrary")),
        name="pre_mla",
    )(x, mods, g0, wd, gq, gkv, wq, wkv, cos_t, sin_t)


def _attn_kernel(*refs, n_seg):
    q_ref = refs[0]
    k_refs = refs[1:1 + n_seg]
    v_refs = refs[1 + n_seg:1 + 2 * n_seg]
    o_ref = refs[1 + 2 * n_seg]
    nt = (((1,), (1,)), ((), ()))

    def scores(hd):
        q = q_ref[:, hd * QK_PAD:(hd + 1) * QK_PAD]
        return [lax.dot_general(q, k[:, hd * QK_PAD:(hd + 1) * QK_PAD], nt, preferred_element_type=F32)
                for k in k_refs]

    nxt = scores(0)
    for hd in range(MLA_HEADS):
        ss = nxt
        if hd + 1 < MLA_HEADS:
            nxt = scores(hd + 1)
        m = ss[0].max(axis=-1, keepdims=True)
        for s in ss[1:]:
            m = jnp.maximum(m, s.max(axis=-1, keepdims=True))
        acc = None
        for s, v in zip(ss, v_refs):
            pv = _dot(jnp.exp2((s - m).astype(BF16)), v[:, hd * V_PAD:(hd + 1) * V_PAD])
            acc = pv if acc is None else acc + pv
        o_ref[:, hd * V_DIM:(hd + 1) * V_DIM] = (acc[:, :V_DIM] / acc[:, V_DIM:V_DIM + 1]).astype(BF16)


def _attention(q, ks, vs, tq):
    b, nq, qk_w = q.shape
    v_w = MLA_HEADS * V_DIM
    kv_spec = lambda a: pl.BlockSpec((None,) + a.shape[1:], lambda i, j: (i, 0, 0))
    return pl.pallas_call(
        functools.partial(_attn_kernel, n_seg=len(ks)),
        out_shape=jax.ShapeDtypeStruct((b, nq, v_w), BF16),
        grid=(b, nq // tq),
        in_specs=[pl.BlockSpec((None, tq, qk_w), lambda i, j: (i, j, 0))]
                 + [kv_spec(a) for a in ks] + [kv_spec(a) for a in vs],
        out_specs=pl.BlockSpec((None, tq, v_w), lambda i, j: (i, j, 0)),
        compiler_params=_cparams(("arbitrary", "arbitrary")),
        name="mla_attention",
    )(q, *ks, *vs)


SUB_ROWS = 256


def _sub_tiles(n):
    return [slice(r, r + SUB_ROWS) for r in range(0, n, SUB_ROWS)]


def _post_core(o, x, rows, mod_ref, g1_ref, g2_ref, wr_ref, x1_ref, fin_ref, lg_ref):
    d = x.shape[-1]
    ne = lg_ref.shape[0]
    gate = _rows(_mod_chunk(mod_ref, 2, d), x)
    shift = _rows(_mod_chunk(mod_ref, 3, d), x)
    scale = _rows(_mod_chunk(mod_ref, 4, d), x)
    x1 = x + gate * _rms(o, g1_ref[...])
    fin = _rms(x1, g2_ref[...]) * (1.0 + scale) + shift
    x1_ref[rows, :] = x1
    _pack_store(fin_ref, fin, row0=rows.start)
    nt = (((1,), (1,)), ((), ()))
    f_hi = fin.astype(BF16)
    f_lo = (fin - f_hi.astype(F32)).astype(BF16)
    r_hi = lax.dot_general(wr_ref[...], f_hi, nt, preferred_element_type=F32)
    r_lo = lax.dot_general(wr_ref[0:ne, :], f_lo, nt, preferred_element_type=F32)
    lg_ref[:, rows] = r_hi[:ne] + r_hi[ne:] + r_lo


def _post_proj_kernel(o_ref, wo_ref, x_ref, mod_ref, g1_ref, g2_ref, wr_ref, *rest):
    x1_ref, fin_ref, lg_ref = rest[-3:]
    for rows in _sub_tiles(x_ref.shape[0]):
        o = _dot(o_ref[rows, :], wo_ref[...])
        _post_core(o, x_ref[rows, :], rows, mod_ref, g1_ref, g2_ref, wr_ref, x1_ref, fin_ref, lg_ref)


def _post_proj_tm_kernel(o_ref, wo_ref, x_ref, mod_ref, g1_ref, g2_ref, wr_ref, *rest):
    x1_ref, fin_ref, lg_ref = rest[-3:]
    nb, tt, d = x_ref.shape
    ts = SUB_ROWS // nb
    for t0 in range(0, tt, ts):
        o = _dot(o_ref[:, t0:t0 + ts, :].reshape(nb * ts, o_ref.shape[-1]), wo_ref[...])
        o = jnp.swapaxes(o.reshape(nb, ts, d), 0, 1).reshape(ts * nb, d)
        x = jnp.swapaxes(x_ref[:, t0:t0 + ts, :], 0, 1).reshape(ts * nb, d)
        _post_core(o, x, slice(t0 * nb, (t0 + ts) * nb), mod_ref, g1_ref, g2_ref, wr_ref, x1_ref, fin_ref, lg_ref)


def _post_glu_kernel(h_ref, yf_ref, yb_ref, dsk_ref, wg_ref, bg_ref, x_ref, mod_ref, g1_ref, g2_ref, wr_ref,
                     *rest):
    x1_ref, fin_ref, lg_ref = rest[-3:]
    d = x_ref.shape[-1]
    for rows in _sub_tiles(x_ref.shape[0]):
        wide = lambda r: jnp.concatenate([r[g, rows, :] for g in range(r.shape[0])], axis=-1)
        y = wide(h_ref) * dsk_ref[...] + wide(yf_ref) + wide(yb_ref)
        z = _dot(jax.nn.gelu(y, approximate=True).astype(BF16), wg_ref[...]) + bg_ref[...]
        o = z[:, :d] * jax.nn.sigmoid(z[:, d:])
        _post_core(o, x_ref[rows, :], rows, mod_ref, g1_ref, g2_ref, wr_ref, x1_ref, fin_ref, lg_ref)


def _post_mixer(kernel, tok_inputs, consts, x, n_tok, x_off, mods, g1, g2, wr_t, tm, rows_per_mod, name,
                moe_total=None, moe_off=0, prev=None, x_spec=None):
    d = x.shape[-1]
    ne = wr_t.shape[0] // 2
    moe_total = n_tok if moe_total is None else moe_total
    tiles_per_mod = rows_per_mod // tm
    xo, mo = x_off // tm, moe_off // tm
    full = lambda a: pl.BlockSpec(a.shape, lambda i: (0,) * a.ndim)
    tile = pl.BlockSpec((tm, d), lambda i: (i, 0))
    mod_spec = pl.BlockSpec((None,) + mods.shape[1:], lambda i: (i // tiles_per_mod, 0, 0))
    x_spec = pl.BlockSpec((tm, d), lambda i: (i + xo, 0)) if x_spec is None else x_spec
    in_specs = ([spec for _, spec in tok_inputs] + [full(a) for a in consts]
                + [x_spec, mod_spec, full(g1), full(g2), full(wr_t)])
    args = [a for a, _ in tok_inputs] + list(consts) + [x, mods, g1, g2, wr_t]
    aliases = {}
    if prev is not None:
        aliases = {len(args): 1, len(args) + 1: 2}
        in_specs += [pl.BlockSpec(memory_space=pl.ANY)] * 2
        args += list(prev)
    return pl.pallas_call(
        kernel,
        out_shape=(jax.ShapeDtypeStruct((n_tok, d), F32),
                   jax.ShapeDtypeStruct((moe_total * PACK_ROWS, LANES), U32),
                   jax.ShapeDtypeStruct((ne, moe_total), F32)),
        grid=(n_tok // tm,),
        in_specs=in_specs,
        out_specs=(tile, pl.BlockSpec((tm * PACK_ROWS, LANES), lambda i: (i + mo, 0)),
                   pl.BlockSpec((ne, tm), lambda i: (0, i + mo))),
        input_output_aliases=aliases,
        compiler_params=_cparams(("arbitrary",)),
        name=name,
    )(*args)


def _route_kernel(lg_ref, bias_ref, eidx_ref, gate_ref, rank_ref, cnt_ref, tri_ref, base_ref):
    i = pl.program_id(0)
    ne, tt = lg_ref.shape
    gsz = ne // N_EXPERT_GROUPS
    shp = (N_EXPERT_GROUPS, gsz, tt)
    neg = -jnp.inf

    @pl.when(i == 0)
    def _():
        base_ref[...] = jnp.zeros_like(base_ref)
        r = lax.broadcasted_iota(jnp.int32, (tt, tt), 0)
        c = lax.broadcasted_iota(jnp.int32, (tt, tt), 1)
        tri_ref[...] = (r < c).astype(BF16)

    scores = jax.nn.sigmoid(lg_ref[...])
    s3 = scores.reshape(shp)
    b3 = (scores + bias_ref[...]).reshape(shp)
    io_e = lax.broadcasted_iota(jnp.int32, shp, 1)
    io_g = lax.broadcasted_iota(jnp.int32, shp, 0)
    io_flat = io_g * gsz + io_e
    m1 = b3.max(axis=1, keepdims=True)
    i1 = jnp.where(b3 == m1, io_e, gsz).min(axis=1, keepdims=True)
    m2 = jnp.where(io_e == i1, neg, b3).max(axis=1, keepdims=True)
    cur = jnp.broadcast_to(m1 + m2, shp)
    gsel = jnp.zeros(shp, jnp.bool_)
    for _ in range(TOPK_GROUPS):
        m = cur.max(axis=0, keepdims=True)
        gi = jnp.where(cur == m, io_g, N_EXPERT_GROUPS).min(axis=0, keepdims=True)
        hit = io_g == gi
        gsel = jnp.logical_or(gsel, hit)
        cur = jnp.where(hit, neg, cur)
    cand = jnp.where(gsel, b3, neg)
    sel = jnp.zeros(shp, jnp.bool_)
    eids, gts = [], []
    for _ in range(TOP_K):
        m = cand.max(axis=0, keepdims=True).max(axis=1, keepdims=True)
        ei = jnp.where(cand == m, io_flat, ne).min(axis=0, keepdims=True).min(axis=1, keepdims=True)
        hit = io_flat == ei
        gts.append(jnp.where(hit, s3, 0.0).sum(axis=0, keepdims=True).sum(axis=1, keepdims=True))
        eids.append(ei)
        sel = jnp.logical_or(sel, hit)
        cand = jnp.where(hit, neg, cand)
    gsum = gts[0]
    for g in gts[1:]:
        gsum = gsum + g
    self32 = sel.astype(F32).reshape(ne, tt)
    cnt = _dot(self32.astype(BF16), tri_ref[...]) + base_ref[...]
    cnt3 = cnt.reshape(shp)
    for k in range(TOP_K):
        hit = io_flat == eids[k]
        rk = jnp.where(hit, cnt3, 0.0).sum(axis=0, keepdims=True).sum(axis=1, keepdims=True)
        rank_ref[k:k + 1, :] = rk.reshape(1, tt).astype(jnp.int32)
        eidx_ref[k:k + 1, :] = eids[k].reshape(1, tt)
        gate_ref[k:k + 1, :] = (gts[k] / gsum * ROUTED_SCALE).reshape(1, tt)
    base_new = base_ref[...] + self32.sum(axis=1, keepdims=True)
    base_ref[...] = base_new
    cnt_ref[...] = jnp.broadcast_to(base_new, cnt_ref.shape)


def _route(logits_t, bias, tt, tok0, t):
    ne = logits_t.shape[0]
    off = tok0 // tt
    out_i = jax.ShapeDtypeStruct((TOP_K, t), jnp.int32)
    row = pl.BlockSpec((TOP_K, tt), lambda i: (0, i))
    return pl.pallas_call(
        _route_kernel,
        out_shape=(out_i, jax.ShapeDtypeStruct((TOP_K, t), F32), out_i,
                   jax.ShapeDtypeStruct((ne, 128), F32)),
        grid=(t // tt,),
        in_specs=[pl.BlockSpec((ne, tt), lambda i: (0, i + off)),
                  pl.BlockSpec((ne, 1), lambda i: (0, 0))],
        out_specs=(row, row, row, pl.BlockSpec((ne, 128), lambda i: (0, 0))),
        scratch_shapes=[pltpu.VMEM((tt, tt), BF16), pltpu.VMEM((ne, 1), F32)],
        compiler_params=_cparams(("arbitrary",)),
        name="moe_route",
    )(logits_t, bias.reshape(ne, 1))


def _dest_kernel(eidx_ref, rank_ref, start_ref, dest_ref):
    kk, tt = eidx_ref.shape
    ne = start_ref.shape[0]
    n_chunk, _, r = dest_ref.shape
    io_e = lax.broadcasted_iota(jnp.int32, (ne, tt), 0)
    start = start_ref[...]
    for k in range(kk):
        hit = io_e == eidx_ref[k:k + 1, :]
        dk = jnp.where(hit, start, 0).sum(axis=0, keepdims=True) + rank_ref[k:k + 1, :]
        for c in range(n_chunk):
            dest_ref[c, k:k + 1, :] = dk[:, c * r:(c + 1) * r]


def _dest_rows(eidx_t, rank_t, start, tt, r):
    kk, t = eidx_t.shape
    ne = start.shape[0]
    return pl.pallas_call(
        _dest_kernel,
        out_shape=jax.ShapeDtypeStruct((t // r, kk, r), jnp.int32),
        grid=(t // tt,),
        in_specs=[pl.BlockSpec((kk, tt), lambda i: (0, i)),
                  pl.BlockSpec((kk, tt), lambda i: (0, i)),
                  pl.BlockSpec((ne, 1), lambda i: (0, 0))],
        out_specs=pl.BlockSpec((tt // r, kk, r), lambda i: (i, 0, 0)),
        compiler_params=_cparams(("arbitrary",)),
        name="moe_dest",
    )(eidx_t, rank_t, start.reshape(ne, 1))


SC_CHUNK = 64


def _sc_mesh():
    return plsc.VectorSubcoreMesh(core_axis_name="c", subcore_axis_name="s")


def _sc_workers():
    info = plsc.get_sparse_core_info()
    return info.num_cores, info.num_cores * info.num_subcores


def _sc_scatter_rows(rows, dest, n_out, row0=0):
    n_chunk, kk, r = dest.shape
    nc, nw = _sc_workers()
    cpw = n_chunk // nw
    assert cpw * nw == n_chunk and cpw % 2 == 0 and row0 % r == 0 and row0 + n_chunk * r <= rows.shape[0]

    @functools.partial(
        pl.kernel, mesh=_sc_mesh(),
        out_type=jax.ShapeDtypeStruct((n_out,) + rows.shape[1:], rows.dtype),
        scratch_types=[pltpu.VMEM((2, kk, r), jnp.int32), pltpu.VMEM((2, r) + rows.shape[1:], rows.dtype),
                       pltpu.SemaphoreType.DMA((2,)), pltpu.SemaphoreType.DMA((2,))])
    def scatter(rows_hbm, dest_hbm, out_hbm, idx_v, rows_v, load_sem, scat_sem):
        c0 = (lax.axis_index("s") * nc + lax.axis_index("c")) * cpw

        def loads(c, b):
            return (pltpu.make_async_copy(dest_hbm.at[c], idx_v.at[b], load_sem.at[b]),
                    pltpu.make_async_copy(rows_hbm.at[pl.ds(row0 + c * r, r)], rows_v.at[b], load_sem.at[b]))

        def scat(b, k):
            return pltpu.make_async_copy(rows_v.at[b], out_hbm.at[idx_v.at[b, k]], scat_sem.at[b])

        for cp in loads(c0, 0):
            cp.start()

        @pl.loop(0, cpw, step=2)
        def _(ci):
            for b in range(2):
                c = c0 + ci + b
                for cp in loads(c, b):
                    cp.wait()
                for k in range(kk):
                    scat(b, k).start()

                @pl.when(ci + b >= 1)
                def _():
                    for k in range(kk):
                        scat(1 - b, k).wait()

                @pl.when(ci + b + 1 < cpw)
                def _():
                    for cp in loads(c + 1, 1 - b):
                        cp.start()

        for k in range(kk):
            scat((cpw - 1) % 2, k).wait()

    return scatter(rows, dest)


def _sc_gather_rows(src, dest):
    n_chunk, kk, r = dest.shape
    t = n_chunk * r
    nc, nw = _sc_workers()
    cpw = n_chunk // nw
    nbuf = 3
    assert cpw * nw == n_chunk and kk > nbuf

    @functools.partial(
        pl.kernel, mesh=_sc_mesh(),
        out_type=jax.ShapeDtypeStruct((kk, t) + src.shape[1:], src.dtype),
        scratch_types=[pltpu.VMEM((kk, r), jnp.int32), pltpu.VMEM((nbuf, r) + src.shape[1:], src.dtype),
                       pltpu.SemaphoreType.DMA((nbuf,)), pltpu.SemaphoreType.DMA((nbuf,))])
    def gather(src_hbm, dest_hbm, out_hbm, idx_v, rows_v, get_sem, put_sem):
        c0 = (lax.axis_index("s") * nc + lax.axis_index("c")) * cpw

        @pl.loop(0, cpw)
        def _(ci):
            c = c0 + ci
            pltpu.sync_copy(dest_hbm.at[c], idx_v)

            def get(k):
                return pltpu.make_async_copy(src_hbm.at[idx_v.at[k]], rows_v.at[k % nbuf], get_sem.at[k % nbuf])

            def put(k):
                return pltpu.make_async_copy(rows_v.at[k % nbuf], out_hbm.at[k, pl.ds(c * r, r)],
                                             put_sem.at[k % nbuf])

            for k in range(nbuf - 1):
                get(k).start()
            for k in range(kk):
                get(k).wait()
                put(k).start()
                if k + nbuf - 1 < kk:
                    if k >= 1:
                        put(k - 1).wait()
                    get(k + nbuf - 1).start()
            for k in range(kk - nbuf, kk):
                put(k).wait()

    return gather(src, dest)


def _expert_kernel(be_ref, nu_ref, x_ref, wg_ref, wu_ref, wd_ref, o_ref, wgu_s, wd_s):
    i = pl.program_id(0)
    tb = o_ref.shape[0] // PACK_ROWS

    @pl.when(i < nu_ref[0])
    def _():
        @pl.when(jnp.logical_or(i == 0, be_ref[i] != be_ref[jnp.maximum(i - 1, 0)]))
        def _():
            wgu_s[:, :D_EXPERT] = wg_ref[...].astype(BF16)
            wgu_s[:, D_EXPERT:] = wu_ref[...].astype(BF16)
            wd_s[...] = wd_ref[...].astype(BF16)

        x = jnp.concatenate([v.astype(BF16) for v in _unpack_load(x_ref, tb)], axis=-1)
        gu = _dot(x, wgu_s[...])
        g = gu[:, :D_EXPERT]
        h = g * jax.nn.sigmoid(g) * gu[:, D_EXPERT:]
        _pack_store(o_ref, _dot(h.astype(BF16), wd_s[...]))


def _experts(xs, blk_e, n_used, w_gate, w_up, w_down, layer, tb):
    rows = xs.shape[0] // PACK_ROWS
    _, ne, d, de = w_gate.shape
    nb = rows // tb
    row_map = lambda i, be, nu: (jnp.minimum(i, nu[0] - 1), 0)
    w_map = lambda i, be, nu: (layer, be[i], 0, 0)
    grid_spec = pltpu.PrefetchScalarGridSpec(
        num_scalar_prefetch=2,
        grid=(nb,),
        in_specs=[pl.BlockSpec((tb * PACK_ROWS, LANES), row_map),
                  pl.BlockSpec((None, None, d, de), w_map),
                  pl.BlockSpec((None, None, d, de), w_map),
                  pl.BlockSpec((None, None, de, d), w_map)],
        out_specs=pl.BlockSpec((tb * PACK_ROWS, LANES), row_map),
        scratch_shapes=[pltpu.VMEM((d, 2 * de), BF16), pltpu.VMEM((de, d), BF16)],
    )
    return pl.pallas_call(
        _expert_kernel,
        out_shape=jax.ShapeDtypeStruct(xs.shape, U32),
        grid_spec=grid_spec,
        compiler_params=_cparams(("arbitrary",)),
        name="moe_experts",
    )(blk_e, n_used, xs, w_gate, w_up, w_down)


def _combine_kernel(yk_ref, gate_ref, fin_ref, shgu_ref, shd_ref, x1_ref, mod_ref, g3_ref, *rest, fuse_next):
    if fuse_next:
        nmod_ref, ng0_ref = rest[0], rest[1]
        o_ref, h_ref = rest[-2], rest[-1]
    else:
        o_ref = rest[-1]
    tm, d = x1_ref.shape
    for rows in _sub_tiles(tm):
        n, r0 = SUB_ROWS, rows.start
        gates = gate_ref[rows, :]
        blocks = None
        for k in range(TOP_K):
            gk = gates[:, k:k + 1]
            terms = [gk * v for v in _unpack_load(yk_ref, n, lead=(k,), row0=r0)]
            blocks = terms if blocks is None else [a + b for a, b in zip(blocks, terms)]
        fin = jnp.concatenate([v.astype(BF16) for v in _unpack_load(fin_ref, n, row0=r0)], axis=-1)
        gu = _dot(fin, shgu_ref[...])
        g = gu[:, :D_EXPERT]
        hsh = g * jax.nn.sigmoid(g) * gu[:, D_EXPERT:]
        f = jnp.concatenate(blocks, axis=-1) + _dot(hsh.astype(BF16), shd_ref[...])
        x1 = x1_ref[rows, :]
        x2 = x1 + _rows(_mod_chunk(mod_ref, 5, d), x1) * _rms(f, g3_ref[...])
        if len(o_ref.shape) == 2:
            o_ref[rows, :] = x2
        else:
            nb = o_ref.shape[0]
            ts = SUB_ROWS // nb
            o_ref[:, r0 // nb:r0 // nb + ts, :] = jnp.swapaxes(x2.reshape(ts, nb, d), 0, 1)
        if fuse_next:
            hn = (_rms(x2, ng0_ref[...]) * (1.0 + _rows(_mod_chunk(nmod_ref, 1, d), x2))
                  + _rows(_mod_chunk(nmod_ref, 0, d), x2))
            for gi in range(h_ref.shape[0]):
                h_ref[gi, rows, :] = hn[:, gi * LANES:(gi + 1) * LANES]


def _combine(yk, gates, fin, shgu, shd, x1, mods, g3, tm, rows_per_mod, n_tok, x_off, yk_off, fin_off,
             batch_out=0, prev=None, out_rows=None, out_off=None, nxt=None):
    t, d = x1.shape
    out_rows = t if out_rows is None else out_rows
    out_off = x_off if out_off is None else out_off
    xo, yo, fo, oo = x_off // tm, yk_off // tm, fin_off // tm, out_off // tm
    tiles_per_mod = rows_per_mod // tm
    full = lambda a: pl.BlockSpec(a.shape, lambda i: (0,) * a.ndim)
    if batch_out:
        out_shape = [jax.ShapeDtypeStruct((batch_out, out_rows // batch_out, d), F32)]
        out_specs = [pl.BlockSpec((batch_out, tm // batch_out, d), lambda i: (0, i + oo, 0))]
    else:
        out_shape = [jax.ShapeDtypeStruct((out_rows, d), F32)]
        out_specs = [pl.BlockSpec((tm, d), lambda i: (i + oo, 0))]
    in_specs = [pl.BlockSpec((TOP_K, tm * PACK_ROWS, LANES), lambda i: (0, i + yo, 0)),
                pl.BlockSpec((tm, TOP_K), lambda i: (i + yo, 0)),
                pl.BlockSpec((tm * PACK_ROWS, LANES), lambda i: (i + fo, 0)),
                full(shgu), full(shd),
                pl.BlockSpec((tm, d), lambda i: (i + xo, 0)),
                pl.BlockSpec((None,) + mods.shape[1:], lambda i: ((i + xo) // tiles_per_mod, 0, 0)),
                full(g3)]
    args = [yk, gates, fin, shgu, shd, x1, mods, g3]
    if nxt is not None:
        in_specs += [full(nxt[0]), full(nxt[1])]
        args += list(nxt)
        out_shape.append(jax.ShapeDtypeStruct((d // LANES, out_rows, LANES), F32))
        out_specs.append(pl.BlockSpec((d // LANES, tm, LANES), lambda i: (0, i + oo, 0)))
    aliases = {}
    if prev is not None:
        for j, p in enumerate(prev if isinstance(prev, (tuple, list)) else [prev]):
            in_specs.append(pl.BlockSpec(memory_space=pl.ANY))
            aliases[len(args)] = j
            args.append(p)
    out = pl.pallas_call(
        functools.partial(_combine_kernel, fuse_next=nxt is not None),
        out_shape=tuple(out_shape),
        grid=(n_tok // tm,),
        in_specs=in_specs,
        out_specs=tuple(out_specs),
        input_output_aliases=aliases,
        compiler_params=_cparams(("arbitrary",)),
        name="moe_combine",
    )(*args)
    return out if nxt is not None else out[0]


def _moe(fin, logits_t, bias, w_gate, w_up, w_down, layer, tb, tok0, t):
    t_all = fin.shape[0] // PACK_ROWS
    ne = w_gate.shape[1]
    tt = 512
    eidx_t, gates_t, rank_t, cnt = _route(logits_t, bias, tt, tok0, t)
    counts = cnt[:, 0].astype(jnp.int32)
    padded = (counts + tb - 1) // tb * tb
    pad_end = jnp.cumsum(padded)
    pad_start = pad_end - padded
    nb = (t * TOP_K) // tb + ne
    n_used = pad_end[-1] // tb
    blk_start = jnp.arange(nb, dtype=jnp.int32) * tb
    blk = jnp.sum(pad_end[None, :] <= jnp.minimum(blk_start, pad_end[-1] - 1)[:, None], axis=1)
    blk_e = jnp.minimum(blk, ne - 1).astype(jnp.int32)
    dest = _dest_rows(eidx_t, rank_t, pad_start, tt, SC_CHUNK)
    xs = _sc_scatter_rows(fin.reshape(t_all, PACK_ROWS, LANES), dest, nb * tb, row0=tok0)
    ys = _experts(xs.reshape(nb * tb * PACK_ROWS, LANES), blk_e, n_used.reshape(1).astype(jnp.int32),
                  w_gate, w_up, w_down, layer, tb)
    yk = _sc_gather_rows(ys.reshape(nb * tb, PACK_ROWS, LANES), dest)
    return yk.reshape(TOP_K, t * PACK_ROWS, LANES), gates_t.T


def _pre_s5_kernel(x_ref, mod_ref, g0_ref, *refs):
    h_ref, xt_ref = refs[-2:]
    nb, tt, d = x_ref.shape
    x = jnp.swapaxes(x_ref[...], 0, 1).reshape(tt * nb, d)
    h = (_rms(x, g0_ref[...]) * (1.0 + _rows(_mod_chunk(mod_ref, 1, d), x))
         + _rows(_mod_chunk(mod_ref, 0, d), x))
    for g in range(h_ref.shape[0]):
        h_ref[g] = h[:, g * LANES:(g + 1) * LANES]
    xt_ref[...] = x


def _pre_s5(x, mods, g0, n_total, t_off, prev, tt):
    nb, n, d = x.shape
    off = t_off // tt
    out_shape = (jax.ShapeDtypeStruct((d // LANES, n_total * nb, LANES), F32),
                 jax.ShapeDtypeStruct((n_total * nb, d), F32))
    out_specs = (pl.BlockSpec((d // LANES, tt * nb, LANES), lambda i: (0, i + off, 0)),
                 pl.BlockSpec((tt * nb, d), lambda i: (i + off, 0)))
    in_specs = [pl.BlockSpec((nb, tt, d), lambda i: (0, i, 0)),
                pl.BlockSpec(mods.shape, lambda i: (0, 0)),
                pl.BlockSpec(g0.shape, lambda i: (0, 0))]
    args = (x, mods, g0)
    aliases = {}
    if prev is not None:
        in_specs += [pl.BlockSpec(memory_space=pl.ANY)] * 2
        args += tuple(prev)
        aliases = {3: 0, 4: 1}
    return pl.pallas_call(
        _pre_s5_kernel,
        out_shape=out_shape,
        grid=(n // tt,),
        in_specs=in_specs,
        out_specs=out_specs,
        input_output_aliases=aliases,
        compiler_params=_cparams(("arbitrary",)),
        name="pre_s5",
    )(*args)


def _s5_scan_kernel(h_ref, bm_ref, cm_ref, lam_ref, y_ref, bu0, bu1, xb0, xb1, st_ref):
    first = jnp.logical_and(jnp.logical_and(pl.program_id(0) == 0, pl.program_id(1) == 0), pl.program_id(2) == 0)
    dr = pl.program_id(1)
    s = pl.program_id(2)
    tc, nb, cw = h_ref.shape
    half = st_ref.shape[1] // 2

    @pl.when(first)
    def _():
        for r in (bu0, bu1, xb0, xb1, st_ref):
            r[...] = jnp.zeros_like(r)

    def stages(bu_w, bu_r, xb_w, xb_r):
        lr = jnp.broadcast_to(lam_ref[0:1, :], (nb, half))
        li = jnp.broadcast_to(lam_ref[1:2, :], (nb, half))
        fresh = s == 1
        xr = jnp.where(fresh, 0.0, st_ref[:, 0:half])
        xi = jnp.where(fresh, 0.0, st_ref[:, half:2 * half])
        tp = tc // S5_PIECES
        for p in range(S5_PIECES):
            for i in range(p * tp, (p + 1) * tp):
                t = i + dr * (tc - 1 - 2 * i)
                rows = pl.ds(pl.multiple_of(t * nb, nb), nb)
                nr = lr * xr - li * xi + bu_r[rows, 0:half]
                ni = lr * xi + li * xr + bu_r[rows, half:2 * half]
                xb_w[rows, 0:half] = nr.astype(BF16)
                xb_w[rows, half:2 * half] = ni.astype(BF16)
                xr, xi = nr, ni
            ts = slice(p * tp, (p + 1) * tp)
            mr = slice(p * tp * nb, (p + 1) * tp * nb)
            y_ref[ts] = _dot(xb_r[mr, :], cm_ref[...]).reshape(tp, nb, cw)
            bu_w[mr, :] = _dot(h_ref[ts].reshape(tp * nb, cw).astype(BF16), bm_ref[...])
        st_ref[:, 0:half] = xr
        st_ref[:, half:2 * half] = xi

    @pl.when(s % 2 == 0)
    def _():
        stages(bu0, bu1, xb1, xb0)

    @pl.when(s % 2 == 1)
    def _():
        stages(bu1, bu0, xb0, xb1)


def _s5_scan(h_all, n_ctx, bm, cm, lam, tc):
    ng, nt, nb, cw = h_all.shape
    nl = nt - n_ctx
    assert cw == S5_GROUPS_PER_BLOCK * S5_GROUP
    sw = 2 * S5_GROUPS_PER_BLOCK * S5_STATE
    ncc, n = n_ctx // tc, nt // tc

    def chunk(dr, j):
        j = jnp.clip(j, 0, n - 1)
        rev = jnp.where(j < ncc, ncc - 1 - j, n - 1 - (j - ncc))
        return jnp.where(dr == 0, j, rev)

    def out_map(g, dr, s):
        return (dr, g, chunk(dr, jnp.clip(s - 2, ncc, n - 1)) - ncc, 0, 0)

    return pl.pallas_call(
        _s5_scan_kernel,
        out_shape=jax.ShapeDtypeStruct((2, ng, nl, nb, cw), F32),
        grid=(ng, 2, n + 2),
        in_specs=[pl.BlockSpec((None, tc, nb, cw), lambda g, dr, s: (g, chunk(dr, s), 0, 0)),
                  pl.BlockSpec((None, None, cw, sw), lambda g, dr, s: (dr, g, 0, 0)),
                  pl.BlockSpec((None, None, sw, cw), lambda g, dr, s: (dr, g, 0, 0)),
                  pl.BlockSpec((None, None, 2, sw // 2), lambda g, dr, s: (dr, g, 0, 0))],
        out_specs=pl.BlockSpec((None, None, tc, nb, cw), out_map),
        scratch_shapes=[pltpu.VMEM((tc * nb, sw), F32), pltpu.VMEM((tc * nb, sw), F32),
                        pltpu.VMEM((tc * nb, sw), BF16), pltpu.VMEM((tc * nb, sw), BF16),
                        pltpu.VMEM((nb, sw), F32)],
        compiler_params=_cparams(("arbitrary", "arbitrary", "arbitrary")),
        name="s5_scan",
    )(h_all, bm, cm, lam)


def _s5_params(lam_re, lam_im, log_step, b_re, b_im, c_re, c_im):
    g, p = lam_re.shape[1:]
    gb = S5_GROUPS_PER_BLOCK
    nblk = g // gb
    step = jnp.exp(log_step)[..., None]
    mag = jnp.exp(lam_re * step)
    lb_re = mag * jnp.cos(lam_im * step)
    lb_im = mag * jnp.sin(lam_im * step)
    den = lam_re * lam_re + lam_im * lam_im
    f_re = ((lb_re - 1.0) * lam_re + lb_im * lam_im) / den
    f_im = (lb_im * lam_re - (lb_re - 1.0) * lam_im) / den
    bb_re = f_re[..., None] * b_re - f_im[..., None] * b_im
    bb_im = f_re[..., None] * b_im + f_im[..., None] * b_re
    eye = jnp.eye(gb, dtype=F32)

    def in_map(w):
        w = w.reshape(2, nblk, gb, p, S5_GROUP)
        return jnp.einsum("dnapi,ab->dnaibp", w, eye).reshape(2, nblk, gb * S5_GROUP, gb * p)

    def out_map(w):
        w = w.reshape(2, nblk, gb, S5_GROUP, p)
        return jnp.einsum("dnaip,ab->dnapbi", w, eye).reshape(2, nblk, gb * p, gb * S5_GROUP)

    bm = jnp.concatenate([in_map(bb_re), in_map(bb_im)], axis=-1).astype(BF16)
    cm = jnp.concatenate([out_map(c_re), out_map(-c_im)], axis=-2).astype(BF16)
    lam = jnp.stack([lb_re.reshape(2, nblk, gb * p), lb_im.reshape(2, nblk, gb * p)], axis=2)
    return bm, cm, lam


def _rope_tables(n_tokens):
    rows = n_tokens // GRID_W
    row = jnp.repeat(jnp.arange(rows), GRID_W).astype(F32)
    col = jnp.tile(jnp.arange(GRID_W), rows).astype(F32)
    n_freq = ROPE_DIM // 4
    inv_freq = ROPE_BASE ** (-jnp.arange(n_freq, dtype=F32) / n_freq)
    ang = jnp.concatenate([row[:, None] * inv_freq, col[:, None] * inv_freq], axis=-1)
    cos, sin = jnp.cos(ang), jnp.sin(ang)
    z = jnp.zeros((n_tokens, 128 - ROPE_DIM), F32)
    return (jnp.concatenate([cos, cos, z], axis=-1), jnp.concatenate([-sin, sin, z], axis=-1))


def _router_halves(w_router):
    wt = w_router.T
    hi = wt.astype(BF16)
    lo = (wt - hi.astype(F32)).astype(BF16)
    return jnp.concatenate([hi, lo], axis=0)


def _split_pairs(w):
    ev, od = w[..., 0::2], w[..., 1::2]
    return jnp.concatenate([ev, od], axis=-1), jnp.concatenate([od, ev], axis=-1)


def _mla_weights(w_dqkv, w_uq, w_ukv):
    kp, kps = _split_pairs(w_dqkv[:, Q_LORA + KV_LORA:])
    wd = jnp.concatenate([w_dqkv[:, :Q_LORA + KV_LORA], kp, kps], axis=-1).astype(BF16)
    wq3 = w_uq.reshape(Q_LORA, MLA_HEADS, NOPE_DIM + ROPE_DIM)
    qp, qps = _split_pairs(wq3[:, :, NOPE_DIM:])
    wq = jnp.concatenate([wq3[:, :, :NOPE_DIM].reshape(Q_LORA, -1), qp.reshape(Q_LORA, -1),
                          qps.reshape(Q_LORA, -1)], axis=-1).astype(BF16)
    wkv3 = w_ukv.reshape(KV_LORA, MLA_HEADS, NOPE_DIM + V_DIM)
    wkv = jnp.concatenate([wkv3[:, :, :NOPE_DIM].reshape(KV_LORA, -1),
                           wkv3[:, :, NOPE_DIM:].reshape(KV_LORA, -1)], axis=-1).astype(BF16)
    return wd, wq, wkv


@jax.jit
def kernel(x, c, ctx, c_ctx, ada_w, ada_b, norm_g, mla_w_dqkv, mla_g_q, mla_g_kv, mla_w_uq, mla_w_ukv, mla_w_o, s5_lam_re, s5_lam_im, s5_log_step, s5_b_re, s5_b_im, s5_c_re, s5_c_im, s5_d, s5_w_glu, s5_b_glu, moe_w_router, moe_bias, moe_w_gate, moe_w_up, moe_w_down, sh_w_gate, sh_w_up, sh_w_down):
    b, l, d = x.shape
    n_ctx = ctx.shape[1]
    assert ada_w.shape[0] == 2 and b % 8 == 0
    ta = 256
    tm = 512
    tb = 512
    row = lambda v: v.reshape(1, -1)

    n_rows = (b + 1 + 7) // 8 * 8
    cvec = jnp.zeros((n_rows, d), F32).at[:b].set(c).at[b].set(c_ctx)
    mods = _ada_mods(cvec, ada_w, ada_b)

    def shared_weights(i):
        shgu = jnp.concatenate([sh_w_gate[i], sh_w_up[i]], axis=-1).astype(BF16)
        return shgu, sh_w_down[i].astype(BF16)

    mod_lat = mods[0, :b].reshape(b, 1, N_MOD * d)
    mod_ctx = mods[0, b].reshape(1, 1, N_MOD * d)
    wd, wq, wkv = _mla_weights(mla_w_dqkv[0], mla_w_uq[0], mla_w_ukv[0])
    cos_l, sin_l = _rope_tables(l)
    cos_c = jnp.concatenate([jnp.ones((n_ctx, ROPE_DIM), F32), jnp.zeros((n_ctx, 128 - ROPE_DIM), F32)], -1)
    sin_c = jnp.zeros((n_ctx, 128), F32)
    pre = functools.partial(_pre_mla, g0=row(norm_g[0, 0]), wd=wd, gq=row(mla_g_q[0]), gkv=row(mla_g_kv[0]),
                            wq=wq, wkv=wkv, tm=ta)
    q_c, k_c, v_c = pre(ctx, mod_ctx, cos_t=cos_c, sin_t=sin_c)
    q_l, k_l, v_l = pre(x, mod_lat, cos_t=cos_l, sin_t=sin_l)
    o_l = _attention(q_l, [k_c, k_l], [v_c, v_l], 2 * ta)
    o_c = _attention(q_c, [k_c], [v_c], n_ctx)

    wo = mla_w_o[0].astype(BF16)
    wr_t = _router_halves(moe_w_router[0])
    g1, g2, g3 = row(norm_g[0, 1]), row(norm_g[0, 2]), row(norm_g[0, 3])
    post = functools.partial(_post_mixer, _post_proj_kernel, consts=[wo], g1=g1, g2=g2, wr_t=wr_t, tm=tm,
                             name="post_mla")
    o_spec = pl.BlockSpec((tm, o_l.shape[-1]), lambda i: (i, 0))
    n_moe = b * (n_ctx + l)
    x1_c, fin, lg = post([(o_c.reshape(b * n_ctx, -1), o_spec)], x=ctx.reshape(b * n_ctx, d),
                         n_tok=b * n_ctx, x_off=0, mods=mod_ctx, rows_per_mod=b * n_ctx, moe_total=n_moe)
    tt = tm // b
    mod_lat_tm = mods[0, :b][None]
    x1_l, fin, lg = _post_mixer(
        _post_proj_tm_kernel, [(o_l, pl.BlockSpec((b, tt, o_l.shape[-1]), lambda i: (0, i, 0)))], [wo],
        x=x, n_tok=b * l, x_off=0, mods=mod_lat_tm, g1=g1, g2=g2, wr_t=wr_t, tm=tm, rows_per_mod=b * l,
        name="post_mla", moe_total=n_moe, moe_off=b * n_ctx, prev=(fin, lg),
        x_spec=pl.BlockSpec((b, tt, d), lambda i: (0, i, 0)))
    shgu, shd = shared_weights(0)
    lat_a = (l // 2) * b
    n_a = b * n_ctx + lat_a
    moe = functools.partial(_moe, fin, lg, moe_bias[0], moe_w_gate, moe_w_up, moe_w_down, 0, tb)
    yk_a, gates_a = moe(0, n_a)
    yk_b, gates_b = moe(n_a, n_moe - n_a)
    comb = functools.partial(_combine, fin=fin, shgu=shgu, shd=shd, g3=g3, tm=tm)
    x2_c = comb(yk_a, gates_a, x1=x1_c, mods=mod_ctx, rows_per_mod=b * n_ctx, n_tok=b * n_ctx, x_off=0, yk_off=0,
                fin_off=0)

    n_all = n_ctx + l
    mod_lat = mods[1, :b]
    mod_ctx = jnp.broadcast_to(mods[1, b][None], (b, N_MOD * d))
    g0 = row(norm_g[1, 0])
    h, xt = _pre_s5(x2_c.reshape(b, n_ctx, d), mod_ctx, g0, n_all, 0, None, tt)
    comb_l = functools.partial(comb, x1=x1_l, mods=mod_lat_tm, rows_per_mod=b * l, out_rows=n_all * b,
                               nxt=(mod_lat, g0))
    xt, h = comb_l(yk_a, gates_a, n_tok=lat_a, x_off=0, yk_off=b * n_ctx, fin_off=b * n_ctx,
                   out_off=n_ctx * b, prev=(xt, h))
    xt, h = comb_l(yk_b, gates_b, n_tok=b * l - lat_a, x_off=lat_a, yk_off=0, fin_off=n_a,
                   out_off=n_ctx * b + lat_a, prev=(xt, h))
    bm, cm, lam = _s5_params(s5_lam_re[0], s5_lam_im[0], s5_log_step[0], s5_b_re[0], s5_b_im[0],
                             s5_c_re[0], s5_c_im[0])
    ng = d // LANES
    y = _s5_scan(h.reshape(ng, n_all, b, LANES), n_ctx, bm, cm, lam, 64).reshape(2, ng, l * b, LANES)
    g1, g2, g3 = row(norm_g[1, 1]), row(norm_g[1, 2]), row(norm_g[1, 3])
    lat0 = n_ctx * b // tm
    x1, fin, lg = _post_mixer(
        _post_glu_kernel,
        [(h, pl.BlockSpec((ng, tm, LANES), lambda i: (0, i + lat0, 0))),
         (y, pl.BlockSpec((None, ng, tm, LANES), lambda i: (0, 0, i, 0))),
         (y, pl.BlockSpec((None, ng, tm, LANES), lambda i: (1, 0, i, 0)))],
        [row(s5_d[0]), s5_w_glu[0].astype(BF16), row(s5_b_glu[0])],
        x=xt, n_tok=l * b, x_off=n_ctx * b, mods=mod_lat[None], g1=g1, g2=g2, wr_t=_router_halves(moe_w_router[1]), tm=tm,
        rows_per_mod=l * b, name="post_s5")
    shgu, shd = shared_weights(1)
    n_h = (l // 2) * b
    moe = functools.partial(_moe, fin, lg, moe_bias[1], moe_w_gate, moe_w_up, moe_w_down, 1, tb)
    yk_a, gates_a = moe(0, n_h)
    yk_b, gates_b = moe(n_h, l * b - n_h)
    comb = functools.partial(_combine, fin=fin, shgu=shgu, shd=shd, x1=x1, mods=mod_lat[None], g3=g3, tm=tm,
                             rows_per_mod=l * b, yk_off=0, batch_out=b)
    out = comb(yk_a, gates_a, n_tok=n_h, x_off=0, fin_off=0)
    return comb(yk_b, gates_b, n_tok=l * b - n_h, x_off=n_h, fin_off=n_h, prev=out)
```

```python
import functools

import jax
import jax.numpy as jnp
from jax import lax
from jax.experimental import pallas as pl
from jax.experimental.pallas import tpu as pltpu
from jax.experimental.pallas import tpu_sc as plsc

F32 = jnp.float32
BF16 = jnp.bfloat16
U32 = jnp.uint32

N_MOD = 6
NORM_EPS = 1e-6
LOG2_E = 1.4426950408889634
GRID_W = 64
MLA_HEADS = 8
Q_LORA = 384
KV_LORA = 256
NOPE_DIM = 128
ROPE_DIM = 64
V_DIM = 128
V_PAD = 256
ROPE_BASE = 10000.0
QK_PAD = 256
S5_GROUP = 16
S5_STATE = 64
S5_GROUPS_PER_BLOCK = 8
S5_PIECES = 4
N_EXPERTS = 64
TOP_K = 8
N_EXPERT_GROUPS = 8
TOPK_GROUPS = 4
D_EXPERT = 256
ROUTED_SCALE = 2.5

VMEM_LIMIT = 56 * 1024 * 1024


def _cparams(sem):
    return pltpu.CompilerParams(dimension_semantics=sem, vmem_limit_bytes=VMEM_LIMIT)


def _rms(x, g):
    return x * lax.rsqrt(jnp.mean(x * x, axis=-1, keepdims=True) + NORM_EPS) * g


def _rows(v, like):
    r = v.shape[0]
    if r == 1:
        return v
    tm, d = like.shape
    return jnp.broadcast_to(v[None], (tm // r, r, d)).reshape(tm, d)


def _mod_chunk(mod_ref, j, d):
    return mod_ref[:, j * d:(j + 1) * d]


def _dot(a, b):
    return jnp.dot(a, b, preferred_element_type=F32)


PACK_ROWS = 4
LANES = 128


def _pack_store(ref, val, lead=(), row0=0):
    n = val.shape[0]
    bits = lax.bitcast_convert_type(val.astype(BF16).astype(F32), U32)
    for s in range(PACK_ROWS):
        lo = bits[:, s * LANES:(s + 1) * LANES] >> 16
        hi = bits[:, (s + PACK_ROWS) * LANES:(s + PACK_ROWS + 1) * LANES] & jnp.uint32(0xFFFF0000)
        ref[lead + (pl.ds(row0 * PACK_ROWS + s, n, stride=PACK_ROWS), slice(None))] = lo | hi


def _unpack_load(ref, n, lead=(), row0=0):
    los, his = [], []
    for s in range(PACK_ROWS):
        w = ref[lead + (pl.ds(row0 * PACK_ROWS + s, n, stride=PACK_ROWS), slice(None))]
        los.append(lax.bitcast_convert_type(w << 16, F32))
        his.append(lax.bitcast_convert_type(w & jnp.uint32(0xFFFF0000), F32))
    return los + his


def _ada_kernel(c_ref, w_ref, b_ref, o_ref):
    c = c_ref[...]
    s = c * jax.nn.sigmoid(c)
    o_ref[...] = jnp.dot(s, w_ref[...], preferred_element_type=F32,
                         precision=lax.Precision.HIGHEST) + b_ref[...]


def _ada_mods(cvec, ada_w, ada_b):
    depth, d, n = ada_w.shape
    rows = cvec.shape[0]
    tn = 1536
    return pl.pallas_call(
        _ada_kernel,
        out_shape=jax.ShapeDtypeStruct((depth, rows, n), F32),
        grid=(depth, n // tn),
        in_specs=[pl.BlockSpec((rows, d), lambda l, j: (0, 0)),
                  pl.BlockSpec((None, d, tn), lambda l, j: (l, 0, j)),
                  pl.BlockSpec((None, 1, tn), lambda l, j: (l, 0, j))],
        out_specs=pl.BlockSpec((None, rows, tn), lambda l, j: (l, 0, j)),
        compiler_params=_cparams(("arbitrary", "arbitrary")),
        name="ada_mods",
    )(cvec, ada_w, ada_b.reshape(depth, 1, n))


def _pre_mla_kernel(x_ref, mod_ref, g0_ref, wd_ref, gq_ref, gkv_ref, wq_ref, wkv_ref, cos_ref, sin_ref,
                    q_ref, k_ref, v_ref):
    d = x_ref.shape[-1]
    x = x_ref[...]
    h = _rms(x, g0_ref[...]) * (1.0 + _mod_chunk(mod_ref, 1, d)) + _mod_chunk(mod_ref, 0, d)
    a = _dot(h.astype(BF16), wd_ref[...])
    cq = _rms(a[:, :Q_LORA], gq_ref[...])
    ckv = _rms(a[:, Q_LORA:Q_LORA + KV_LORA], gkv_ref[...])
    rd = ROPE_DIM
    cos = cos_ref[:, 0:rd]
    sin = sin_ref[:, 0:rd]
    o = Q_LORA + KV_LORA
    k_rot = (a[:, o:o + rd] * cos + a[:, o + rd:o + 2 * rd] * sin).astype(BF16)
    qa = _dot(cq.astype(BF16), wq_ref[...])
    kva = _dot(ckv.astype(BF16), wkv_ref[...])
    hw = MLA_HEADS * 128
    hr = MLA_HEADS * rd
    zpad = jnp.zeros((x.shape[0], QK_PAD - NOPE_DIM - rd), BF16)
    scale = (NOPE_DIM + ROPE_DIM) ** -0.5 * LOG2_E
    for hd in range(MLA_HEADS):
        lo = hd * 128
        q_rot = qa[:, hw + hd * rd:hw + (hd + 1) * rd] * cos + qa[:, hw + hr + hd * rd:hw + hr + (hd + 1) * rd] * sin
        q_ref[:, hd * QK_PAD:hd * QK_PAD + 128] = (qa[:, lo:lo + 128] * scale).astype(BF16)
        q_ref[:, hd * QK_PAD + 128:hd * QK_PAD + 128 + rd] = (q_rot * scale).astype(BF16)
        q_ref[:, hd * QK_PAD + 128 + rd:(hd + 1) * QK_PAD] = zpad
        k_ref[:, hd * QK_PAD:hd * QK_PAD + 128] = kva[:, lo:lo + 128].astype(BF16)
        k_ref[:, hd * QK_PAD + 128:hd * QK_PAD + 128 + rd] = k_rot
        k_ref[:, hd * QK_PAD + 128 + rd:(hd + 1) * QK_PAD] = zpad
        v_ref[:, hd * V_PAD:hd * V_PAD + V_DIM] = kva[:, hw + lo:hw + lo + 128].astype(BF16)
        v_ref[:, hd * V_PAD + V_DIM:(hd + 1) * V_PAD] = jnp.ones((x.shape[0], V_PAD - V_DIM), BF16)


def _pre_mla(x, mods, g0, wd, gq, gkv, wq, wkv, cos_t, sin_t, tm):
    b, n, d = x.shape
    nb_mod = mods.shape[0]
    full = lambda a: pl.BlockSpec(a.shape, lambda i, j: (0,) * a.ndim)
    mod_map = (lambda i, j: (i, 0, 0)) if nb_mod > 1 else (lambda i, j: (0, 0, 0))
    qk_w = MLA_HEADS * QK_PAD
    v_w = MLA_HEADS * V_PAD
    return pl.pallas_call(
        _pre_mla_kernel,
        out_shape=(jax.ShapeDtypeStruct((b, n, qk_w), BF16),
                   jax.ShapeDtypeStruct((b, n, qk_w), BF16),
                   jax.ShapeDtypeStruct((b, n, v_w), BF16)),
        grid=(b, n // tm),
        in_specs=[pl.BlockSpec((None, tm, d), lambda i, j: (i, j, 0)),
                  pl.BlockSpec((None, 1, mods.shape[-1]), mod_map),
                  full(g0), full(wd), full(gq), full(gkv), full(wq), full(wkv),
                  pl.BlockSpec((tm, 128), lambda i, j: (j, 0)),
                  pl.BlockSpec((tm, 128), lambda i, j: (j, 0))],
        out_specs=(pl.BlockSpec((None, tm, qk_w), lambda i, j: (i, j, 0)),
                   pl.BlockSpec((None, tm, qk_w), lambda i, j: (i, j, 0)),
                   pl.BlockSpec((None, tm, v_w), lambda i, j: (i, j, 0))),
        compiler_params=_cparams(("arbitrary", "arbitrary")),
        name="pre_mla",
    )(x, mods, g0, wd, gq, gkv, wq, wkv, cos_t, sin_t)


def _attn_kernel(*refs, n_seg):
    q_ref = refs[0]
    k_refs = refs[1:1 + n_seg]
    v_refs = refs[1 + n_seg:1 + 2 * n_seg]
    o_ref = refs[1 + 2 * n_seg]
    nt = (((1,), (1,)), ((), ()))

    def scores(hd):
        q = q_ref[:, hd * QK_PAD:(hd + 1) * QK_PAD]
        return [lax.dot_general(q, k[:, hd * QK_PAD:(hd + 1) * QK_PAD], nt, preferred_element_type=F32)
                for k in k_refs]

    nxt = scores(0)
    for hd in range(MLA_HEADS):
        ss = nxt
        if hd + 1 < MLA_HEADS:
            nxt = scores(hd + 1)
        m = ss[0].max(axis=-1, keepdims=True)
        for s in ss[1:]:
            m = jnp.maximum(m, s.max(axis=-1, keepdims=True))
        acc = None
        for s, v in zip(ss, v_refs):
            pv = _dot(jnp.exp2((s - m).astype(BF16)), v[:, hd * V_PAD:(hd + 1) * V_PAD])
            acc = pv if acc is None else acc + pv
        o_ref[:, hd * V_DIM:(hd + 1) * V_DIM] = (acc[:, :V_DIM] / acc[:, V_DIM:V_DIM + 1]).astype(BF16)


def _attention(q, ks, vs, tq):
    b, nq, qk_w = q.shape
    v_w = MLA_HEADS * V_DIM
    kv_spec = lambda a: pl.BlockSpec((None,) + a.shape[1:], lambda i, j: (i, 0, 0))
    return pl.pallas_call(
        functools.partial(_attn_kernel, n_seg=len(ks)),
        out_shape=jax.ShapeDtypeStruct((b, nq, v_w), BF16),
        grid=(b, nq // tq),
        in_specs=[pl.BlockSpec((None, tq, qk_w), lambda i, j: (i, j, 0))]
                 + [kv_spec(a) for a in ks] + [kv_spec(a) for a in vs],
        out_specs=pl.BlockSpec((None, tq, v_w), lambda i, j: (i, j, 0)),
        compiler_params=_cparams(("arbitrary", "arbitrary")),
        name="mla_attention",
    )(q, *ks, *vs)


SUB_ROWS = 256


def _sub_tiles(n):
    return [slice(r, r + SUB_ROWS) for r in range(0, n, SUB_ROWS)]


def _post_core(o, x, rows, mod_ref, g1_ref, g2_ref, wr_ref, x1_ref, fin_ref, lg_ref):
    d = x.shape[-1]
    ne = lg_ref.shape[0]
    gate = _rows(_mod_chunk(mod_ref, 2, d), x)
    shift = _rows(_mod_chunk(mod_ref, 3, d), x)
    scale = _rows(_mod_chunk(mod_ref, 4, d), x)
    x1 = x + gate * _rms(o, g1_ref[...])
    fin = _rms(x1, g2_ref[...]) * (1.0 + scale) + shift
    x1_ref[rows, :] = x1
    _pack_store(fin_ref, fin, row0=rows.start)
    nt = (((1,), (1,)), ((), ()))
    f_hi = fin.astype(BF16)
    f_lo = (fin - f_hi.astype(F32)).astype(BF16)
    r_hi = lax.dot_general(wr_ref[...], f_hi, nt, preferred_element_type=F32)
    r_lo = lax.dot_general(wr_ref[0:ne, :], f_lo, nt, preferred_element_type=F32)
    lg_ref[:, rows] = r_hi[:ne] + r_hi[ne:] + r_lo


def _post_proj_kernel(o_ref, wo_ref, x_ref, mod_ref, g1_ref, g2_ref, wr_ref, *rest):
    x1_ref, fin_ref, lg_ref = rest[-3:]
    for rows in _sub_tiles(x_ref.shape[0]):
        o = _dot(o_ref[rows, :], wo_ref[...])
        _post_core(o, x_ref[rows, :], rows, mod_ref, g1_ref, g2_ref, wr_ref, x1_ref, fin_ref, lg_ref)


def _post_proj_tm_kernel(o_ref, wo_ref, x_ref, mod_ref, g1_ref, g2_ref, wr_ref, *rest):
    x1_ref, fin_ref, lg_ref = rest[-3:]
    nb, tt, d = x_ref.shape
    ts = SUB_ROWS // nb
    for t0 in range(0, tt, ts):
        o = _dot(o_ref[:, t0:t0 + ts, :].reshape(nb * ts, o_ref.shape[-1]), wo_ref[...])
        o = jnp.swapaxes(o.reshape(nb, ts, d), 0, 1).reshape(ts * nb, d)
        x = jnp.swapaxes(x_ref[:, t0:t0 + ts, :], 0, 1).reshape(ts * nb, d)
        _post_core(o, x, slice(t0 * nb, (t0 + ts) * nb), mod_ref, g1_ref, g2_ref, wr_ref, x1_ref, fin_ref, lg_ref)


def _chunk_to_rows(yc_ref, c, nb):
    q, grp = S5_Q, S5_GROUP
    per_tile = LANES // grp
    lane_grp = lax.broadcasted_iota(jnp.int32, (nb, LANES), 1) // grp
    n_pairs = yc_ref.shape[0]
    pieces = [[yc_ref[p, c * nb:(c + 1) * nb, lt * LANES:(lt + 1) * LANES] for lt in range(2 * q // per_tile)]
              for p in range(n_pairs)]
    out_rows = []
    for t in range(q):
        tiles = []
        for lb in range(2 * n_pairs // per_tile):
            dest = None
            for g8 in range(per_tile):
                g = lb * per_tile + g8
                piece = pieces[g // 2][(g % 2) * (q // per_tile) + t // per_tile]
                r = pltpu.roll(piece, ((g8 - t % per_tile) * grp) % LANES, 1)
                dest = r if dest is None else jnp.where(lane_grp == g8, r, dest)
            tiles.append(dest)
        out_rows.append(jnp.concatenate(tiles, axis=-1))
    return jnp.concatenate(out_rows, axis=0)


def _post_glu_kernel(h_ref, yc_ref, dsk_ref, wg_ref, bg_ref, x_ref, mod_ref, g1_ref, g2_ref, wr_ref,
                     *rest):
    x1_ref, fin_ref, lg_ref = rest[-3:]
    d = x_ref.shape[-1]
    nb = mod_ref.shape[0]
    assert SUB_ROWS == S5_Q * nb
    for ci, rows in enumerate(_sub_tiles(x_ref.shape[0])):
        h = jnp.concatenate([h_ref[g, rows, :] for g in range(h_ref.shape[0])], axis=-1).astype(F32)
        y = h * dsk_ref[...] + _chunk_to_rows(yc_ref, ci, nb)
        z = _dot(jax.nn.gelu(y, approximate=True).astype(BF16), wg_ref[...]) + bg_ref[...]
        o = z[:, :d] * jax.nn.sigmoid(z[:, d:])
        _post_core(o, x_ref[rows, :], rows, mod_ref, g1_ref, g2_ref, wr_ref, x1_ref, fin_ref, lg_ref)


def _post_mixer(kernel, tok_inputs, consts, x, n_tok, x_off, mods, g1, g2, wr_t, tm, rows_per_mod, name,
                moe_total=None, moe_off=0, prev=None, x_spec=None):
    d = x.shape[-1]
    ne = wr_t.shape[0] // 2
    moe_total = n_tok if moe_total is None else moe_total
    tiles_per_mod = rows_per_mod // tm
    xo, mo = x_off // tm, moe_off // tm
    full = lambda a: pl.BlockSpec(a.shape, lambda i: (0,) * a.ndim)
    tile = pl.BlockSpec((tm, d), lambda i: (i, 0))
    mod_spec = pl.BlockSpec((None,) + mods.shape[1:], lambda i: (i // tiles_per_mod, 0, 0))
    x_spec = pl.BlockSpec((tm, d), lambda i: (i + xo, 0)) if x_spec is None else x_spec
    in_specs = ([spec for _, spec in tok_inputs] + [full(a) for a in consts]
                + [x_spec, mod_spec, full(g1), full(g2), full(wr_t)])
    args = [a for a, _ in tok_inputs] + list(consts) + [x, mods, g1, g2, wr_t]
    aliases = {}
    if prev is not None:
        aliases = {len(args): 1, len(args) + 1: 2}
        in_specs += [pl.BlockSpec(memory_space=pl.ANY)] * 2
        args += list(prev)
    return pl.pallas_call(
        kernel,
        out_shape=(jax.ShapeDtypeStruct((n_tok, d), F32),
                   jax.ShapeDtypeStruct((moe_total * PACK_ROWS, LANES), U32),
                   jax.ShapeDtypeStruct((ne, moe_total), F32)),
        grid=(n_tok // tm,),
        in_specs=in_specs,
        out_specs=(tile, pl.BlockSpec((tm * PACK_ROWS, LANES), lambda i: (i + mo, 0)),
                   pl.BlockSpec((ne, tm), lambda i: (0, i + mo))),
        input_output_aliases=aliases,
        compiler_params=_cparams(("arbitrary",)),
        name=name,
    )(*args)


def _route_kernel(lg_ref, bias_ref, eidx_ref, gate_ref, rank_ref, cnt_ref, tri_ref, base_ref):
    i = pl.program_id(0)
    ne, tt = lg_ref.shape
    gsz = ne // N_EXPERT_GROUPS
    shp = (N_EXPERT_GROUPS, gsz, tt)
    neg = -jnp.inf

    @pl.when(i == 0)
    def _():
        base_ref[...] = jnp.zeros_like(base_ref)
        r = lax.broadcasted_iota(jnp.int32, (tt, tt), 0)
        c = lax.broadcasted_iota(jnp.int32, (tt, tt), 1)
        tri_ref[...] = (r < c).astype(BF16)

    scores = jax.nn.sigmoid(lg_ref[...])
    s3 = scores.reshape(shp)
    b3 = (scores + bias_ref[...]).reshape(shp)
    io_e = lax.broadcasted_iota(jnp.int32, shp, 1)
    io_g = lax.broadcasted_iota(jnp.int32, shp, 0)
    io_flat = io_g * gsz + io_e
    m1 = b3.max(axis=1, keepdims=True)
    i1 = jnp.where(b3 == m1, io_e, gsz).min(axis=1, keepdims=True)
    m2 = jnp.where(io_e == i1, neg, b3).max(axis=1, keepdims=True)
    cur = jnp.broadcast_to(m1 + m2, shp)
    gsel = jnp.zeros(shp, jnp.bool_)
    for _ in range(TOPK_GROUPS):
        m = cur.max(axis=0, keepdims=True)
        gi = jnp.where(cur == m, io_g, N_EXPERT_GROUPS).min(axis=0, keepdims=True)
        hit = io_g == gi
        gsel = jnp.logical_or(gsel, hit)
        cur = jnp.where(hit, neg, cur)
    cand = jnp.where(gsel, b3, neg)
    sel = jnp.zeros(shp, jnp.bool_)
    eids, gts = [], []
    for _ in range(TOP_K):
        m = cand.max(axis=0, keepdims=True).max(axis=1, keepdims=True)
        ei = jnp.where(cand == m, io_flat, ne).min(axis=0, keepdims=True).min(axis=1, keepdims=True)
        hit = io_flat == ei
        gts.append(jnp.where(hit, s3, 0.0).sum(axis=0, keepdims=True).sum(axis=1, keepdims=True))
        eids.append(ei)
        sel = jnp.logical_or(sel, hit)
        cand = jnp.where(hit, neg, cand)
    gsum = gts[0]
    for g in gts[1:]:
        gsum = gsum + g
    self32 = sel.astype(F32).reshape(ne, tt)
    cnt = _dot(self32.astype(BF16), tri_ref[...]) + base_ref[...]
    cnt3 = cnt.reshape(shp)
    for k in range(TOP_K):
        hit = io_flat == eids[k]
        rk = jnp.where(hit, cnt3, 0.0).sum(axis=0, keepdims=True).sum(axis=1, keepdims=True)
        rank_ref[k:k + 1, :] = rk.reshape(1, tt).astype(jnp.int32)
        eidx_ref[k:k + 1, :] = eids[k].reshape(1, tt)
        gate_ref[k:k + 1, :] = (gts[k] / gsum * ROUTED_SCALE).reshape(1, tt)
    base_new = base_ref[...] + self32.sum(axis=1, keepdims=True)
    base_ref[...] = base_new
    cnt_ref[...] = jnp.broadcast_to(base_new, cnt_ref.shape)


def _route(logits_t, bias, tt, tok0, t):
    ne = logits_t.shape[0]
    off = tok0 // tt
    out_i = jax.ShapeDtypeStruct((TOP_K, t), jnp.int32)
    row = pl.BlockSpec((TOP_K, tt), lambda i: (0, i))
    return pl.pallas_call(
        _route_kernel,
        out_shape=(out_i, jax.ShapeDtypeStruct((TOP_K, t), F32), out_i,
                   jax.ShapeDtypeStruct((ne, 128), F32)),
        grid=(t // tt,),
        in_specs=[pl.BlockSpec((ne, tt), lambda i: (0, i + off)),
                  pl.BlockSpec((ne, 1), lambda i: (0, 0))],
        out_specs=(row, row, row, pl.BlockSpec((ne, 128), lambda i: (0, 0))),
        scratch_shapes=[pltpu.VMEM((tt, tt), BF16), pltpu.VMEM((ne, 1), F32)],
        compiler_params=_cparams(("arbitrary",)),
        name="moe_route",
    )(logits_t, bias.reshape(ne, 1))


def _dest_kernel(eidx_ref, rank_ref, start_ref, dest_ref):
    kk, tt = eidx_ref.shape
    ne = start_ref.shape[0]
    n_chunk, _, r = dest_ref.shape
    io_e = lax.broadcasted_iota(jnp.int32, (ne, tt), 0)
    start = start_ref[...]
    for k in range(kk):
        hit = io_e == eidx_ref[k:k + 1, :]
        dk = jnp.where(hit, start, 0).sum(axis=0, keepdims=True) + rank_ref[k:k + 1, :]
        for c in range(n_chunk):
            dest_ref[c, k:k + 1, :] = dk[:, c * r:(c + 1) * r]


def _dest_rows(eidx_t, rank_t, start, tt, r):
    kk, t = eidx_t.shape
    ne = start.shape[0]
    return pl.pallas_call(
        _dest_kernel,
        out_shape=jax.ShapeDtypeStruct((t // r, kk, r), jnp.int32),
        grid=(t // tt,),
        in_specs=[pl.BlockSpec((kk, tt), lambda i: (0, i)),
                  pl.BlockSpec((kk, tt), lambda i: (0, i)),
                  pl.BlockSpec((ne, 1), lambda i: (0, 0))],
        out_specs=pl.BlockSpec((tt // r, kk, r), lambda i: (i, 0, 0)),
        compiler_params=_cparams(("arbitrary",)),
        name="moe_dest",
    )(eidx_t, rank_t, start.reshape(ne, 1))


SC_CHUNK = 64


def _sc_mesh():
    return plsc.VectorSubcoreMesh(core_axis_name="c", subcore_axis_name="s")


def _sc_workers():
    info = plsc.get_sparse_core_info()
    return info.num_cores, info.num_cores * info.num_subcores


def _sc_scatter_rows(rows, dest, n_out, row0=0):
    n_chunk, kk, r = dest.shape
    nc, nw = _sc_workers()
    cpw = n_chunk // nw
    assert cpw * nw == n_chunk and cpw % 2 == 0 and row0 % r == 0 and row0 + n_chunk * r <= rows.shape[0]

    @functools.partial(
        pl.kernel, mesh=_sc_mesh(),
        out_type=jax.ShapeDtypeStruct((n_out,) + rows.shape[1:], rows.dtype),
        scratch_types=[pltpu.VMEM((2, kk, r), jnp.int32), pltpu.VMEM((2, r) + rows.shape[1:], rows.dtype),
                       pltpu.SemaphoreType.DMA((2,)), pltpu.SemaphoreType.DMA((2,))])
    def scatter(rows_hbm, dest_hbm, out_hbm, idx_v, rows_v, load_sem, scat_sem):
        c0 = (lax.axis_index("s") * nc + lax.axis_index("c")) * cpw

        def loads(c, b):
            return (pltpu.make_async_copy(dest_hbm.at[c], idx_v.at[b], load_sem.at[b]),
                    pltpu.make_async_copy(rows_hbm.at[pl.ds(row0 + c * r, r)], rows_v.at[b], load_sem.at[b]))

        def scat(b, k):
            return pltpu.make_async_copy(rows_v.at[b], out_hbm.at[idx_v.at[b, k]], scat_sem.at[b])

        for cp in loads(c0, 0):
            cp.start()

        @pl.loop(0, cpw, step=2)
        def _(ci):
            for b in range(2):
                c = c0 + ci + b
                for cp in loads(c, b):
                    cp.wait()
                for k in range(kk):
                    scat(b, k).start()

                @pl.when(ci + b >= 1)
                def _():
                    for k in range(kk):
                        scat(1 - b, k).wait()

                @pl.when(ci + b + 1 < cpw)
                def _():
                    for cp in loads(c + 1, 1 - b):
                        cp.start()

        for k in range(kk):
            scat((cpw - 1) % 2, k).wait()

    return scatter(rows, dest)


def _sc_gather_rows(src, dest):
    n_chunk, kk, r = dest.shape
    t = n_chunk * r
    nc, nw = _sc_workers()
    cpw = n_chunk // nw
    nbuf = 3
    assert cpw * nw == n_chunk and kk > nbuf

    @functools.partial(
        pl.kernel, mesh=_sc_mesh(),
        out_type=jax.ShapeDtypeStruct((kk, t) + src.shape[1:], src.dtype),
        scratch_types=[pltpu.VMEM((kk, r), jnp.int32), pltpu.VMEM((nbuf, r) + src.shape[1:], src.dtype),
                       pltpu.SemaphoreType.DMA((nbuf,)), pltpu.SemaphoreType.DMA((nbuf,))])
    def gather(src_hbm, dest_hbm, out_hbm, idx_v, rows_v, get_sem, put_sem):
        c0 = (lax.axis_index("s") * nc + lax.axis_index("c")) * cpw

        @pl.loop(0, cpw)
        def _(ci):
            c = c0 + ci
            pltpu.sync_copy(dest_hbm.at[c], idx_v)

            def get(k):
                return pltpu.make_async_copy(src_hbm.at[idx_v.at[k]], rows_v.at[k % nbuf], get_sem.at[k % nbuf])

            def put(k):
                return pltpu.make_async_copy(rows_v.at[k % nbuf], out_hbm.at[k, pl.ds(c * r, r)],
                                             put_sem.at[k % nbuf])

            for k in range(nbuf - 1):
                get(k).start()
            for k in range(kk):
                get(k).wait()
                put(k).start()
                if k + nbuf - 1 < kk:
                    if k >= 1:
                        put(k - 1).wait()
                    get(k + nbuf - 1).start()
            for k in range(kk - nbuf, kk):
                put(k).wait()

    return gather(src, dest)


def _expert_kernel(be_ref, nu_ref, x_ref, wg_ref, wu_ref, wd_ref, o_ref, wgu_s, wd_s):
    i = pl.program_id(0)
    tb = o_ref.shape[0] // PACK_ROWS

    @pl.when(i < nu_ref[0])
    def _():
        @pl.when(jnp.logical_or(i == 0, be_ref[i] != be_ref[jnp.maximum(i - 1, 0)]))
        def _():
            wgu_s[:, :D_EXPERT] = wg_ref[...].astype(BF16)
            wgu_s[:, D_EXPERT:] = wu_ref[...].astype(BF16)
            wd_s[...] = wd_ref[...].astype(BF16)

        x = jnp.concatenate([v.astype(BF16) for v in _unpack_load(x_ref, tb)], axis=-1)
        gu = _dot(x, wgu_s[...])
        g = gu[:, :D_EXPERT]
        h = g * jax.nn.sigmoid(g) * gu[:, D_EXPERT:]
        _pack_store(o_ref, _dot(h.astype(BF16), wd_s[...]))


def _experts(xs, blk_e, n_used, w_gate, w_up, w_down, layer, tb):
    rows = xs.shape[0] // PACK_ROWS
    _, ne, d, de = w_gate.shape
    nb = rows // tb
    row_map = lambda i, be, nu: (jnp.minimum(i, nu[0] - 1), 0)
    w_map = lambda i, be, nu: (layer, be[i], 0, 0)
    grid_spec = pltpu.PrefetchScalarGridSpec(
        num_scalar_prefetch=2,
        grid=(nb,),
        in_specs=[pl.BlockSpec((tb * PACK_ROWS, LANES), row_map),
                  pl.BlockSpec((None, None, d, de), w_map),
                  pl.BlockSpec((None, None, d, de), w_map),
                  pl.BlockSpec((None, None, de, d), w_map)],
        out_specs=pl.BlockSpec((tb * PACK_ROWS, LANES), row_map),
        scratch_shapes=[pltpu.VMEM((d, 2 * de), BF16), pltpu.VMEM((de, d), BF16)],
    )
    return pl.pallas_call(
        _expert_kernel,
        out_shape=jax.ShapeDtypeStruct(xs.shape, U32),
        grid_spec=grid_spec,
        compiler_params=_cparams(("arbitrary",)),
        name="moe_experts",
    )(blk_e, n_used, xs, w_gate, w_up, w_down)


def _combine_kernel(yk_ref, gate_ref, fin_ref, shgu_ref, shd_ref, x1_ref, mod_ref, g3_ref, *rest, fuse_next):
    if fuse_next:
        nmod_ref, ng0_ref = rest[0], rest[1]
        o_ref, h_ref = rest[-2], rest[-1]
    else:
        o_ref = rest[-1]
    tm, d = x1_ref.shape
    for rows in _sub_tiles(tm):
        n, r0 = SUB_ROWS, rows.start
        gates = gate_ref[rows, :]
        blocks = None
        for k in range(TOP_K):
            gk = gates[:, k:k + 1]
            terms = [gk * v for v in _unpack_load(yk_ref, n, lead=(k,), row0=r0)]
            blocks = terms if blocks is None else [a + b for a, b in zip(blocks, terms)]
        fin = jnp.concatenate([v.astype(BF16) for v in _unpack_load(fin_ref, n, row0=r0)], axis=-1)
        gu = _dot(fin, shgu_ref[...])
        g = gu[:, :D_EXPERT]
        hsh = g * jax.nn.sigmoid(g) * gu[:, D_EXPERT:]
        f = jnp.concatenate(blocks, axis=-1) + _dot(hsh.astype(BF16), shd_ref[...])
        x1 = x1_ref[rows, :]
        x2 = x1 + _rows(_mod_chunk(mod_ref, 5, d), x1) * _rms(f, g3_ref[...])
        if len(o_ref.shape) == 2:
            o_ref[rows, :] = x2
        else:
            nb = o_ref.shape[0]
            ts = SUB_ROWS // nb
            o_ref[:, r0 // nb:r0 // nb + ts, :] = jnp.swapaxes(x2.reshape(ts, nb, d), 0, 1)
        if fuse_next:
            hn = (_rms(x2, ng0_ref[...]) * (1.0 + _rows(_mod_chunk(nmod_ref, 1, d), x2))
                  + _rows(_mod_chunk(nmod_ref, 0, d), x2))
            for gi in range(h_ref.shape[0]):
                h_ref[gi, rows, :] = hn[:, gi * LANES:(gi + 1) * LANES].astype(BF16)


def _combine(yk, gates, fin, shgu, shd, x1, mods, g3, tm, rows_per_mod, n_tok, x_off, yk_off, fin_off,
             batch_out=0, prev=None, out_rows=None, out_off=None, nxt=None):
    t, d = x1.shape
    out_rows = t if out_rows is None else out_rows
    out_off = x_off if out_off is None else out_off
    xo, yo, fo, oo = x_off // tm, yk_off // tm, fin_off // tm, out_off // tm
    tiles_per_mod = rows_per_mod // tm
    full = lambda a: pl.BlockSpec(a.shape, lambda i: (0,) * a.ndim)
    if batch_out:
        out_shape = [jax.ShapeDtypeStruct((batch_out, out_rows // batch_out, d), F32)]
        out_specs = [pl.BlockSpec((batch_out, tm // batch_out, d), lambda i: (0, i + oo, 0))]
    else:
        out_shape = [jax.ShapeDtypeStruct((out_rows, d), F32)]
        out_specs = [pl.BlockSpec((tm, d), lambda i: (i + oo, 0))]
    in_specs = [pl.BlockSpec((TOP_K, tm * PACK_ROWS, LANES), lambda i: (0, i + yo, 0)),
                pl.BlockSpec((tm, TOP_K), lambda i: (i + yo, 0)),
                pl.BlockSpec((tm * PACK_ROWS, LANES), lambda i: (i + fo, 0)),
                full(shgu), full(shd),
                pl.BlockSpec((tm, d), lambda i: (i + xo, 0)),
                pl.BlockSpec((None,) + mods.shape[1:], lambda i: ((i + xo) // tiles_per_mod, 0, 0)),
                full(g3)]
    args = [yk, gates, fin, shgu, shd, x1, mods, g3]
    if nxt is not None:
        in_specs += [full(nxt[0]), full(nxt[1])]
        args += list(nxt)
        out_shape.append(jax.ShapeDtypeStruct((d // LANES, out_rows, LANES), BF16))
        out_specs.append(pl.BlockSpec((d // LANES, tm, LANES), lambda i: (0, i + oo, 0)))
    aliases = {}
    if prev is not None:
        for j, p in enumerate(prev if isinstance(prev, (tuple, list)) else [prev]):
            in_specs.append(pl.BlockSpec(memory_space=pl.ANY))
            aliases[len(args)] = j
            args.append(p)
    out = pl.pallas_call(
        functools.partial(_combine_kernel, fuse_next=nxt is not None),
        out_shape=tuple(out_shape),
        grid=(n_tok // tm,),
        in_specs=in_specs,
        out_specs=tuple(out_specs),
        input_output_aliases=aliases,
        compiler_params=_cparams(("arbitrary",)),
        name="moe_combine",
    )(*args)
    return out if nxt is not None else out[0]


def _moe(fin, logits_t, bias, w_gate, w_up, w_down, layer, tb, tok0, t):
    t_all = fin.shape[0] // PACK_ROWS
    ne = w_gate.shape[1]
    tt = 512
    eidx_t, gates_t, rank_t, cnt = _route(logits_t, bias, tt, tok0, t)
    counts = cnt[:, 0].astype(jnp.int32)
    padded = (counts + tb - 1) // tb * tb
    pad_end = jnp.cumsum(padded)
    pad_start = pad_end - padded
    nb = (t * TOP_K) // tb + ne
    n_used = pad_end[-1] // tb
    blk_start = jnp.arange(nb, dtype=jnp.int32) * tb
    blk = jnp.sum(pad_end[None, :] <= jnp.minimum(blk_start, pad_end[-1] - 1)[:, None], axis=1)
    blk_e = jnp.minimum(blk, ne - 1).astype(jnp.int32)
    dest = _dest_rows(eidx_t, rank_t, pad_start, tt, SC_CHUNK)
    xs = _sc_scatter_rows(fin.reshape(t_all, PACK_ROWS, LANES), dest, nb * tb, row0=tok0)
    ys = _experts(xs.reshape(nb * tb * PACK_ROWS, LANES), blk_e, n_used.reshape(1).astype(jnp.int32),
                  w_gate, w_up, w_down, layer, tb)
    yk = _sc_gather_rows(ys.reshape(nb * tb, PACK_ROWS, LANES), dest)
    return yk.reshape(TOP_K, t * PACK_ROWS, LANES), gates_t.T


def _pre_s5_kernel(x_ref, mod_ref, g0_ref, *refs):
    h_ref, xt_ref = refs[-2:]
    nb, tt, d = x_ref.shape
    x = jnp.swapaxes(x_ref[...], 0, 1).reshape(tt * nb, d)
    h = (_rms(x, g0_ref[...]) * (1.0 + _rows(_mod_chunk(mod_ref, 1, d), x))
         + _rows(_mod_chunk(mod_ref, 0, d), x))
    for g in range(h_ref.shape[0]):
        h_ref[g] = h[:, g * LANES:(g + 1) * LANES].astype(BF16)
    xt_ref[...] = x


def _pre_s5(x, mods, g0, n_total, t_off, prev, tt):
    nb, n, d = x.shape
    off = t_off // tt
    out_shape = (jax.ShapeDtypeStruct((d // LANES, n_total * nb, LANES), BF16),
                 jax.ShapeDtypeStruct((n_total * nb, d), F32))
    out_specs = (pl.BlockSpec((d // LANES, tt * nb, LANES), lambda i: (0, i + off, 0)),
                 pl.BlockSpec((tt * nb, d), lambda i: (i + off, 0)))
    in_specs = [pl.BlockSpec((nb, tt, d), lambda i: (0, i, 0)),
                pl.BlockSpec(mods.shape, lambda i: (0, 0)),
                pl.BlockSpec(g0.shape, lambda i: (0, 0))]
    args = (x, mods, g0)
    aliases = {}
    if prev is not None:
        in_specs += [pl.BlockSpec(memory_space=pl.ANY)] * 2
        args += tuple(prev)
        aliases = {3: 0, 4: 1}
    return pl.pallas_call(
        _pre_s5_kernel,
        out_shape=out_shape,
        grid=(n // tt,),
        in_specs=in_specs,
        out_specs=out_specs,
        input_output_aliases=aliases,
        compiler_params=_cparams(("arbitrary",)),
        name="pre_s5",
    )(*args)


def _s5_scan_kernel(h_ref, bm_ref, cm_ref, lam_ref, y_ref, bu0, bu1, xb0, xb1, st_ref):
    first = jnp.logical_and(jnp.logical_and(pl.program_id(0) == 0, pl.program_id(1) == 0), pl.program_id(2) == 0)
    dr = pl.program_id(1)
    s = pl.program_id(2)
    tc, nb, cw = h_ref.shape
    half = st_ref.shape[1] // 2

    @pl.when(first)
    def _():
        for r in (bu0, bu1, xb0, xb1, st_ref):
            r[...] = jnp.zeros_like(r)

    def stages(bu_w, bu_r, xb_w, xb_r):
        lr = jnp.broadcast_to(lam_ref[0:1, :], (nb, half))
        li = jnp.broadcast_to(lam_ref[1:2, :], (nb, half))
        fresh = s == 1
        xr = jnp.where(fresh, 0.0, st_ref[:, 0:half])
        xi = jnp.where(fresh, 0.0, st_ref[:, half:2 * half])
        tp = tc // S5_PIECES
        for p in range(S5_PIECES):
            for i in range(p * tp, (p + 1) * tp):
                t = i + dr * (tc - 1 - 2 * i)
                rows = pl.ds(pl.multiple_of(t * nb, nb), nb)
                nr = lr * xr - li * xi + bu_r[rows, 0:half]
                ni = lr * xi + li * xr + bu_r[rows, half:2 * half]
                xb_w[rows, 0:half] = nr.astype(BF16)
                xb_w[rows, half:2 * half] = ni.astype(BF16)
                xr, xi = nr, ni
            ts = slice(p * tp, (p + 1) * tp)
            mr = slice(p * tp * nb, (p + 1) * tp * nb)
            y_ref[ts] = _dot(xb_r[mr, :], cm_ref[...]).reshape(tp, nb, cw)
            bu_w[mr, :] = _dot(h_ref[ts].reshape(tp * nb, cw).astype(BF16), bm_ref[...])
        st_ref[:, 0:half] = xr
        st_ref[:, half:2 * half] = xi

    @pl.when(s % 2 == 0)
    def _():
        stages(bu0, bu1, xb1, xb0)

    @pl.when(s % 2 == 1)
    def _():
        stages(bu1, bu0, xb0, xb1)


def _s5_scan(h_all, n_ctx, bm, cm, lam, tc):
    ng, nt, nb, cw = h_all.shape
    nl = nt - n_ctx
    assert cw == S5_GROUPS_PER_BLOCK * S5_GROUP
    sw = 2 * S5_GROUPS_PER_BLOCK * S5_STATE
    ncc, n = n_ctx // tc, nt // tc

    def chunk(dr, j):
        j = jnp.clip(j, 0, n - 1)
        rev = jnp.where(j < ncc, ncc - 1 - j, n - 1 - (j - ncc))
        return jnp.where(dr == 0, j, rev)

    def out_map(g, dr, s):
        return (dr, g, chunk(dr, jnp.clip(s - 2, ncc, n - 1)) - ncc, 0, 0)

    return pl.pallas_call(
        _s5_scan_kernel,
        out_shape=jax.ShapeDtypeStruct((2, ng, nl, nb, cw), F32),
        grid=(ng, 2, n + 2),
        in_specs=[pl.BlockSpec((None, tc, nb, cw), lambda g, dr, s: (g, chunk(dr, s), 0, 0)),
                  pl.BlockSpec((None, None, cw, sw), lambda g, dr, s: (dr, g, 0, 0)),
                  pl.BlockSpec((None, None, sw, cw), lambda g, dr, s: (dr, g, 0, 0)),
                  pl.BlockSpec((None, None, 2, sw // 2), lambda g, dr, s: (dr, g, 0, 0))],
        out_specs=pl.BlockSpec((None, None, tc, nb, cw), out_map),
        scratch_shapes=[pltpu.VMEM((tc * nb, sw), F32), pltpu.VMEM((tc * nb, sw), F32),
                        pltpu.VMEM((tc * nb, sw), BF16), pltpu.VMEM((tc * nb, sw), BF16),
                        pltpu.VMEM((nb, sw), F32)],
        compiler_params=_cparams(("arbitrary", "arbitrary", "arbitrary")),
        name="s5_scan",
    )(h_all, bm, cm, lam)


def _s5_params(lam_re, lam_im, log_step, b_re, b_im, c_re, c_im):
    g, p = lam_re.shape[1:]
    gb = S5_GROUPS_PER_BLOCK
    nblk = g // gb
    step = jnp.exp(log_step)[..., None]
    mag = jnp.exp(lam_re * step)
    lb_re = mag * jnp.cos(lam_im * step)
    lb_im = mag * jnp.sin(lam_im * step)
    den = lam_re * lam_re + lam_im * lam_im
    f_re = ((lb_re - 1.0) * lam_re + lb_im * lam_im) / den
    f_im = (lb_im * lam_re - (lb_re - 1.0) * lam_im) / den
    bb_re = f_re[..., None] * b_re - f_im[..., None] * b_im
    bb_im = f_re[..., None] * b_im + f_im[..., None] * b_re
    eye = jnp.eye(gb, dtype=F32)

    def in_map(w):
        w = w.reshape(2, nblk, gb, p, S5_GROUP)
        return jnp.einsum("dnapi,ab->dnaibp", w, eye).reshape(2, nblk, gb * S5_GROUP, gb * p)

    def out_map(w):
        w = w.reshape(2, nblk, gb, S5_GROUP, p)
        return jnp.einsum("dnaip,ab->dnapbi", w, eye).reshape(2, nblk, gb * p, gb * S5_GROUP)

    bm = jnp.concatenate([in_map(bb_re), in_map(bb_im)], axis=-1).astype(BF16)
    cm = jnp.concatenate([out_map(c_re), out_map(-c_im)], axis=-2).astype(BF16)
    lam = jnp.stack([lb_re.reshape(2, nblk, gb * p), lb_im.reshape(2, nblk, gb * p)], axis=2)
    return bm, cm, lam


S5_Q = 16
S5_STEP_GROUPS = 4


def _s5c_kernel(h_ref, toep_ref, win_ref, wout_ref, lamq_ref, y_ref, u_ref, sre_ref, sim_ref, xin_ref, *,
                nb, n_ctx_chunks):
    npair, n_rows, kw = u_ref.shape
    half = kw // 2
    sw = 2 * S5_STATE
    n = n_rows // nb
    ncc = n_ctx_chunks
    per_tile = LANES // S5_GROUP
    gl0 = (pl.program_id(0) % (per_tile // S5_STEP_GROUPS)) * S5_STEP_GROUPS
    lane_grp = lax.broadcasted_iota(jnp.int32, (nb, LANES), 1) // S5_GROUP

    def relayout(c, carry):
        r0 = c * (S5_Q * nb)
        pieces = [h_ref[pl.ds(pl.multiple_of(r0 + s * nb, nb), nb), :].astype(F32) for s in range(S5_Q)]
        rows = pl.ds(pl.multiple_of(c * nb, nb), nb)
        for gq in range(S5_STEP_GROUPS):
            for j in range(S5_Q // per_tile):
                dest = None
                for s8 in range(per_tile):
                    shift = (s8 * S5_GROUP + LANES - (gl0 + gq) * S5_GROUP) % LANES
                    r = pltpu.roll(pieces[j * per_tile + s8], shift, 1)
                    dest = r if dest is None else jnp.where(lane_grp == s8, r, dest)
                lo = (gq % 2) * half + j * LANES
                u_ref[gq // 2, rows, lo:lo + LANES] = dest.astype(BF16)
        return carry

    lax.fori_loop(0, n, relayout, 0)
    for dr in range(2):
        for p in range(npair):
            s = _dot(u_ref[p], win_ref[dr, p])
            sre_ref[dr, :, p * sw:(p + 1) * sw] = s[:, :sw]
            sim_ref[dr, :, p * sw:(p + 1) * sw] = s[:, sw:]
    w = npair * sw
    lr = [jnp.broadcast_to(lamq_ref[dr, 0:1, :], (nb, w)) for dr in range(2)]
    li = [jnp.broadcast_to(lamq_ref[dr, 1:2, :], (nb, w)) for dr in range(2)]

    def step(j, carry):
        rev = jnp.where(j < ncc, ncc - 1 - j, n - 1 - (j - ncc))
        out = []
        for dr in range(2):
            xr, xi = carry[dr]
            c = j if dr == 0 else rev
            rows = pl.ds(pl.multiple_of(c * nb, nb), nb)
            for p in range(npair):
                xin_ref[dr, rows, 2 * p * sw:(2 * p + 1) * sw] = xr[:, p * sw:(p + 1) * sw].astype(BF16)
                xin_ref[dr, rows, (2 * p + 1) * sw:(2 * p + 2) * sw] = xi[:, p * sw:(p + 1) * sw].astype(BF16)
            nr = lr[dr] * xr - li[dr] * xi + sre_ref[dr, rows, :]
            ni = lr[dr] * xi + li[dr] * xr + sim_ref[dr, rows, :]
            out.append((nr, ni))
        return tuple(out)

    zero = jnp.zeros((nb, w), F32)
    lax.fori_loop(0, n, step, ((zero, zero), (zero, zero)), unroll=2)
    lat = slice(ncc * nb, n_rows)
    for p in range(npair):
        acc = None
        for dr in range(2):
            intra = jnp.concatenate([_dot(u_ref[p, lat, 0:half], toep_ref[dr, 2 * p]),
                                     _dot(u_ref[p, lat, half:kw], toep_ref[dr, 2 * p + 1])], axis=-1)
            term = intra + _dot(xin_ref[dr, lat, 2 * p * sw:(2 * p + 2) * sw], wout_ref[dr, p])
            acc = term if acc is None else acc + term
        y_ref[p] = acc


def _s5_chunked(h, toep, win, wout, lamq, nb, n_ctx):
    ng, n_tok, _ = h.shape
    n_groups = toep.shape[1]
    pp = S5_STEP_GROUPS // 2
    steps_per_block = (LANES // S5_GROUP) // S5_STEP_GROUPS
    n_rows = n_tok // S5_Q
    ncc = n_ctx // S5_Q
    lat_rows = n_rows - ncc * nb
    kw = 2 * S5_Q * S5_GROUP
    sw = 2 * S5_STATE
    return pl.pallas_call(
        functools.partial(_s5c_kernel, nb=nb, n_ctx_chunks=ncc),
        out_shape=jax.ShapeDtypeStruct((n_groups // 2, lat_rows, kw), F32),
        grid=(n_groups // S5_STEP_GROUPS,),
        in_specs=[pl.BlockSpec((None, n_tok, LANES), lambda i: (i // steps_per_block, 0, 0),
                               pipeline_mode=pl.Buffered(1)),
                  pl.BlockSpec((2, 2 * pp, kw // 2, kw // 2), lambda i: (0, i, 0, 0)),
                  pl.BlockSpec((2, pp, kw, 2 * sw), lambda i: (0, i, 0, 0)),
                  pl.BlockSpec((2, pp, 2 * sw, kw), lambda i: (0, i, 0, 0)),
                  pl.BlockSpec((2, None, 2, pp * sw), lambda i: (0, i, 0, 0))],
        out_specs=pl.BlockSpec((pp, lat_rows, kw), lambda i: (i, 0, 0)),
        scratch_shapes=[pltpu.VMEM((pp, n_rows, kw), BF16),
                        pltpu.VMEM((2, n_rows, pp * sw), F32), pltpu.VMEM((2, n_rows, pp * sw), F32),
                        pltpu.VMEM((2, n_rows, 2 * pp * sw), BF16)],
        compiler_params=_cparams(("arbitrary",)),
        name="s5_chunked",
    )(h, toep, win, wout, lamq)


def _s5c_params(lam_re, lam_im, log_step, b_re, b_im, c_re, c_im):
    hp = lax.Precision.HIGHEST
    q = S5_Q
    _, g, p = lam_re.shape
    ni = b_re.shape[-1]
    step = jnp.exp(log_step)[..., None]
    ar, ai = lam_re * step, lam_im * step
    tau = jnp.arange(q + 1, dtype=F32)[:, None, None, None]
    mag = jnp.exp(tau * ar)
    pr, pi = mag * jnp.cos(tau * ai), mag * jnp.sin(tau * ai)
    den = lam_re * lam_re + lam_im * lam_im
    f_re = ((pr[1] - 1.0) * lam_re + pi[1] * lam_im) / den
    f_im = (pi[1] * lam_re - (pr[1] - 1.0) * lam_im) / den
    bb_re = f_re[..., None] * b_re - f_im[..., None] * b_im
    bb_im = f_re[..., None] * b_im + f_im[..., None] * b_re
    cp_re = c_re[None] * pr[:, :, :, None, :] - c_im[None] * pi[:, :, :, None, :]
    cp_im = c_re[None] * pi[:, :, :, None, :] + c_im[None] * pr[:, :, :, None, :]
    taps = (jnp.einsum("tdgop,dgpi->tdgoi", cp_re[:q], bb_re, precision=hp)
            - jnp.einsum("tdgop,dgpi->tdgoi", cp_im[:q], bb_im, precision=hp))
    s_idx = jnp.arange(q)[:, None]
    t_idx = jnp.arange(q)[None, :]

    def toeplitz(dr):
        lag = (t_idx - s_idx) if dr == 0 else (s_idx - t_idx)
        k = taps[:, dr][jnp.clip(lag, 0, q - 1)]
        k = jnp.where((lag >= 0)[:, :, None, None, None], k, 0.0)
        return k.transpose(2, 0, 4, 1, 3).reshape(g, q * ni, q * ni)

    toep = jnp.stack([toeplitz(0), toeplitz(1)]).astype(BF16)

    def state_in(dr):
        e = (q - 1 - jnp.arange(q)) if dr == 0 else jnp.arange(q)
        er, ei = pr[e, dr], pi[e, dr]
        br, bi = bb_re[dr].transpose(0, 2, 1), bb_im[dr].transpose(0, 2, 1)
        w_re = er[:, :, None, :] * br[None] - ei[:, :, None, :] * bi[None]
        w_im = er[:, :, None, :] * bi[None] + ei[:, :, None, :] * br[None]
        fl = lambda a: a.transpose(1, 0, 2, 3).reshape(g, q * ni, p)
        return fl(w_re), fl(w_im)

    def state_out(dr):
        f = (jnp.arange(q) + 1) if dr == 0 else (q - jnp.arange(q))
        fl = lambda a: a.transpose(1, 3, 0, 2).reshape(g, p, q * ni)
        return fl(cp_re[f, dr]), fl(-cp_im[f, dr])

    z = lambda *shape: jnp.zeros(shape, F32)

    def pair_in(dr):
        w_re, w_im = state_in(dr)
        a_re, b_re_, a_im, b_im_ = w_re[0::2], w_re[1::2], w_im[0::2], w_im[1::2]
        zz = z(g // 2, q * ni, p)
        top = jnp.concatenate([a_re, zz, a_im, zz], axis=-1)
        bot = jnp.concatenate([zz, b_re_, zz, b_im_], axis=-1)
        return jnp.concatenate([top, bot], axis=1)

    def pair_out(dr):
        w_re, w_im = state_out(dr)
        zz = z(g // 2, p, q * ni)
        rows = [jnp.concatenate([w_re[0::2], zz], axis=-1), jnp.concatenate([zz, w_re[1::2]], axis=-1),
                jnp.concatenate([w_im[0::2], zz], axis=-1), jnp.concatenate([zz, w_im[1::2]], axis=-1)]
        return jnp.concatenate(rows, axis=1)

    win = jnp.stack([pair_in(0), pair_in(1)]).astype(BF16)
    wout = jnp.stack([pair_out(0), pair_out(1)]).astype(BF16)
    ng = S5_STEP_GROUPS
    lamq = jnp.stack([pr[q].reshape(2, g // ng, ng * p), pi[q].reshape(2, g // ng, ng * p)], axis=2)
    return toep, win, wout, lamq


def _rope_tables(n_tokens):
    rows = n_tokens // GRID_W
    row = jnp.repeat(jnp.arange(rows), GRID_W).astype(F32)
    col = jnp.tile(jnp.arange(GRID_W), rows).astype(F32)
    n_freq = ROPE_DIM // 4
    inv_freq = ROPE_BASE ** (-jnp.arange(n_freq, dtype=F32) / n_freq)
    ang = jnp.concatenate([row[:, None] * inv_freq, col[:, None] * inv_freq], axis=-1)
    cos, sin = jnp.cos(ang), jnp.sin(ang)
    z = jnp.zeros((n_tokens, 128 - ROPE_DIM), F32)
    return (jnp.concatenate([cos, cos, z], axis=-1), jnp.concatenate([-sin, sin, z], axis=-1))


def _router_halves(w_router):
    wt = w_router.T
    hi = wt.astype(BF16)
    lo = (wt - hi.astype(F32)).astype(BF16)
    return jnp.concatenate([hi, lo], axis=0)


def _split_pairs(w):
    ev, od = w[..., 0::2], w[..., 1::2]
    return jnp.concatenate([ev, od], axis=-1), jnp.concatenate([od, ev], axis=-1)


def _mla_weights(w_dqkv, w_uq, w_ukv):
    kp, kps = _split_pairs(w_dqkv[:, Q_LORA + KV_LORA:])
    wd = jnp.concatenate([w_dqkv[:, :Q_LORA + KV_LORA], kp, kps], axis=-1).astype(BF16)
    wq3 = w_uq.reshape(Q_LORA, MLA_HEADS, NOPE_DIM + ROPE_DIM)
    qp, qps = _split_pairs(wq3[:, :, NOPE_DIM:])
    wq = jnp.concatenate([wq3[:, :, :NOPE_DIM].reshape(Q_LORA, -1), qp.reshape(Q_LORA, -1),
                          qps.reshape(Q_LORA, -1)], axis=-1).astype(BF16)
    wkv3 = w_ukv.reshape(KV_LORA, MLA_HEADS, NOPE_DIM + V_DIM)
    wkv = jnp.concatenate([wkv3[:, :, :NOPE_DIM].reshape(KV_LORA, -1),
                           wkv3[:, :, NOPE_DIM:].reshape(KV_LORA, -1)], axis=-1).astype(BF16)
    return wd, wq, wkv


@jax.jit
def kernel(x, c, ctx, c_ctx, ada_w, ada_b, norm_g, mla_w_dqkv, mla_g_q, mla_g_kv, mla_w_uq, mla_w_ukv, mla_w_o, s5_lam_re, s5_lam_im, s5_log_step, s5_b_re, s5_b_im, s5_c_re, s5_c_im, s5_d, s5_w_glu, s5_b_glu, moe_w_router, moe_bias, moe_w_gate, moe_w_up, moe_w_down, sh_w_gate, sh_w_up, sh_w_down):
    b, l, d = x.shape
    n_ctx = ctx.shape[1]
    assert ada_w.shape[0] == 2 and b % 8 == 0
    ta = 256
    tm = 512
    tb = 512
    row = lambda v: v.reshape(1, -1)

    n_rows = (b + 1 + 7) // 8 * 8
    cvec = jnp.zeros((n_rows, d), F32).at[:b].set(c).at[b].set(c_ctx)
    mods = _ada_mods(cvec, ada_w, ada_b)

    def shared_weights(i):
        shgu = jnp.concatenate([sh_w_gate[i], sh_w_up[i]], axis=-1).astype(BF16)
        return shgu, sh_w_down[i].astype(BF16)

    mod_lat = mods[0, :b].reshape(b, 1, N_MOD * d)
    mod_ctx = mods[0, b].reshape(1, 1, N_MOD * d)
    wd, wq, wkv = _mla_weights(mla_w_dqkv[0], mla_w_uq[0], mla_w_ukv[0])
    cos_l, sin_l = _rope_tables(l)
    cos_c = jnp.concatenate([jnp.ones((n_ctx, ROPE_DIM), F32), jnp.zeros((n_ctx, 128 - ROPE_DIM), F32)], -1)
    sin_c = jnp.zeros((n_ctx, 128), F32)
    pre = functools.partial(_pre_mla, g0=row(norm_g[0, 0]), wd=wd, gq=row(mla_g_q[0]), gkv=row(mla_g_kv[0]),
                            wq=wq, wkv=wkv, tm=ta)
    q_c, k_c, v_c = pre(ctx, mod_ctx, cos_t=cos_c, sin_t=sin_c)
    q_l, k_l, v_l = pre(x, mod_lat, cos_t=cos_l, sin_t=sin_l)
    o_l = _attention(q_l, [k_c, k_l], [v_c, v_l], 2 * ta)
    o_c = _attention(q_c, [k_c], [v_c], n_ctx)

    wo = mla_w_o[0].astype(BF16)
    wr_t = _router_halves(moe_w_router[0])
    g1, g2, g3 = row(norm_g[0, 1]), row(norm_g[0, 2]), row(norm_g[0, 3])
    post = functools.partial(_post_mixer, _post_proj_kernel, consts=[wo], g1=g1, g2=g2, wr_t=wr_t, tm=tm,
                             name="post_mla")
    o_spec = pl.BlockSpec((tm, o_l.shape[-1]), lambda i: (i, 0))
    n_moe = b * (n_ctx + l)
    x1_c, fin, lg = post([(o_c.reshape(b * n_ctx, -1), o_spec)], x=ctx.reshape(b * n_ctx, d),
                         n_tok=b * n_ctx, x_off=0, mods=mod_ctx, rows_per_mod=b * n_ctx, moe_total=n_moe)
    tt = tm // b
    mod_lat_tm = mods[0, :b][None]
    x1_l, fin, lg = _post_mixer(
        _post_proj_tm_kernel, [(o_l, pl.BlockSpec((b, tt, o_l.shape[-1]), lambda i: (0, i, 0)))], [wo],
        x=x, n_tok=b * l, x_off=0, mods=mod_lat_tm, g1=g1, g2=g2, wr_t=wr_t, tm=tm, rows_per_mod=b * l,
        name="post_mla", moe_total=n_moe, moe_off=b * n_ctx, prev=(fin, lg),
        x_spec=pl.BlockSpec((b, tt, d), lambda i: (0, i, 0)))
    shgu, shd = shared_weights(0)
    lat_a = (l // 2) * b
    n_a = b * n_ctx + lat_a
    moe = functools.partial(_moe, fin, lg, moe_bias[0], moe_w_gate, moe_w_up, moe_w_down, 0, tb)
    yk_a, gates_a = moe(0, n_a)
    yk_b, gates_b = moe(n_a, n_moe - n_a)
    comb = functools.partial(_combine, fin=fin, shgu=shgu, shd=shd, g3=g3, tm=tm)
    x2_c = comb(yk_a, gates_a, x1=x1_c, mods=mod_ctx, rows_per_mod=b * n_ctx, n_tok=b * n_ctx, x_off=0, yk_off=0,
                fin_off=0)

    n_all = n_ctx + l
    mod_lat = mods[1, :b]
    mod_ctx = jnp.broadcast_to(mods[1, b][None], (b, N_MOD * d))
    g0 = row(norm_g[1, 0])
    h, xt = _pre_s5(x2_c.reshape(b, n_ctx, d), mod_ctx, g0, n_all, 0, None, tt)
    comb_l = functools.partial(comb, x1=x1_l, mods=mod_lat_tm, rows_per_mod=b * l, out_rows=n_all * b,
                               nxt=(mod_lat, g0))
    xt, h = comb_l(yk_a, gates_a, n_tok=lat_a, x_off=0, yk_off=b * n_ctx, fin_off=b * n_ctx,
                   out_off=n_ctx * b, prev=(xt, h))
    xt, h = comb_l(yk_b, gates_b, n_tok=b * l - lat_a, x_off=lat_a, yk_off=0, fin_off=n_a,
                   out_off=n_ctx * b + lat_a, prev=(xt, h))
    toep, win, wout, lamq = _s5c_params(s5_lam_re[0], s5_lam_im[0], s5_log_step[0], s5_b_re[0], s5_b_im[0],
                                        s5_c_re[0], s5_c_im[0])
    ng = d // LANES
    yc = _s5_chunked(h, toep, win, wout, lamq, b, n_ctx)
    g1, g2, g3 = row(norm_g[1, 1]), row(norm_g[1, 2]), row(norm_g[1, 3])
    lat0 = n_ctx * b // tm
    x1, fin, lg = _post_mixer(
        _post_glu_kernel,
        [(h, pl.BlockSpec((ng, tm, LANES), lambda i: (0, i + lat0, 0))),
         (yc, pl.BlockSpec((yc.shape[0], tm // S5_Q, yc.shape[-1]), lambda i: (0, i, 0)))],
        [row(s5_d[0]), s5_w_glu[0].astype(BF16), row(s5_b_glu[0])],
        x=xt, n_tok=l * b, x_off=n_ctx * b, mods=mod_lat[None], g1=g1, g2=g2, wr_t=_router_halves(moe_w_router[1]), tm=tm,
        rows_per_mod=l * b, name="post_s5")
    shgu, shd = shared_weights(1)
    n_h = (l // 2) * b
    moe = functools.partial(_moe, fin, lg, moe_bias[1], moe_w_gate, moe_w_up, moe_w_down, 1, tb)
    yk_a, gates_a = moe(0, n_h)
    yk_b, gates_b = moe(n_h, l * b - n_h)
    comb = functools.partial(_combine, fin=fin, shgu=shgu, shd=shd, x1=x1, mods=mod_lat[None], g3=g3, tm=tm,
                             rows_per_mod=l * b, yk_off=0, batch_out=b)
    out = comb(yk_a, gates_a, n_tok=n_h, x_off=0, fin_off=0)
    return comb(yk_b, gates_b, n_tok=l * b - n_h, x_off=n_h, fin_off=n_h, prev=out)
```

```python
import functools

import jax
import jax.numpy as jnp
from jax import lax
from jax.experimental import pallas as pl
from jax.experimental.pallas import tpu as pltpu
from jax.experimental.pallas import tpu_sc as plsc

F32 = jnp.float32
BF16 = jnp.bfloat16
U32 = jnp.uint32

N_MOD = 6
NORM_EPS = 1e-6
LOG2_E = 1.4426950408889634
GRID_W = 64
MLA_HEADS = 8
Q_LORA = 384
KV_LORA = 256
NOPE_DIM = 128
ROPE_DIM = 64
V_DIM = 128
V_PAD = 256
ROPE_BASE = 10000.0
QK_PAD = 256
S5_GROUP = 16
S5_STATE = 64
S5_GROUPS_PER_BLOCK = 8
S5_PIECES = 4
N_EXPERTS = 64
TOP_K = 8
N_EXPERT_GROUPS = 8
TOPK_GROUPS = 4
D_EXPERT = 256
ROUTED_SCALE = 2.5

VMEM_LIMIT = 56 * 1024 * 1024


def _cparams(sem):
    return pltpu.CompilerParams(dimension_semantics=sem, vmem_limit_bytes=VMEM_LIMIT)


def _rms(x, g):
    return x * lax.rsqrt(jnp.mean(x * x, axis=-1, keepdims=True) + NORM_EPS) * g


def _rows(v, like):
    r = v.shape[0]
    if r == 1:
        return v
    tm, d = like.shape
    return jnp.broadcast_to(v[None], (tm // r, r, d)).reshape(tm, d)


def _mod_chunk(mod_ref, j, d):
    return mod_ref[:, j * d:(j + 1) * d]


def _dot(a, b):
    return jnp.dot(a, b, preferred_element_type=F32)


PACK_ROWS = 4
LANES = 128


def _pack_store(ref, val, lead=(), row0=0):
    n = val.shape[0]
    bits = lax.bitcast_convert_type(val.astype(BF16).astype(F32), U32)
    for s in range(PACK_ROWS):
        lo = bits[:, s * LANES:(s + 1) * LANES] >> 16
        hi = bits[:, (s + PACK_ROWS) * LANES:(s + PACK_ROWS + 1) * LANES] & jnp.uint32(0xFFFF0000)
        ref[lead + (pl.ds(row0 * PACK_ROWS + s, n, stride=PACK_ROWS), slice(None))] = lo | hi


def _unpack_load(ref, n, lead=(), row0=0):
    los, his = [], []
    for s in range(PACK_ROWS):
        w = ref[lead + (pl.ds(row0 * PACK_ROWS + s, n, stride=PACK_ROWS), slice(None))]
        los.append(lax.bitcast_convert_type(w << 16, F32))
        his.append(lax.bitcast_convert_type(w & jnp.uint32(0xFFFF0000), F32))
    return los + his


def _ada_kernel(c_ref, w_ref, b_ref, o_ref):
    c = c_ref[...]
    s = c * jax.nn.sigmoid(c)
    o_ref[...] = jnp.dot(s, w_ref[...], preferred_element_type=F32,
                         precision=lax.Precision.HIGHEST) + b_ref[...]


def _ada_mods(cvec, ada_w, ada_b):
    depth, d, n = ada_w.shape
    rows = cvec.shape[0]
    tn = 1536
    return pl.pallas_call(
        _ada_kernel,
        out_shape=jax.ShapeDtypeStruct((depth, rows, n), F32),
        grid=(depth, n // tn),
        in_specs=[pl.BlockSpec((rows, d), lambda l, j: (0, 0)),
                  pl.BlockSpec((None, d, tn), lambda l, j: (l, 0, j)),
                  pl.BlockSpec((None, 1, tn), lambda l, j: (l, 0, j))],
        out_specs=pl.BlockSpec((None, rows, tn), lambda l, j: (l, 0, j)),
        compiler_params=_cparams(("arbitrary", "arbitrary")),
        name="ada_mods",
    )(cvec, ada_w, ada_b.reshape(depth, 1, n))


def _pre_mla_kernel(x_ref, mod_ref, g0_ref, wd_ref, gq_ref, gkv_ref, wq_ref, wkv_ref, cos_ref, sin_ref,
                    q_ref, k_ref, v_ref):
    d = x_ref.shape[-1]
    x = x_ref[...]
    h = _rms(x, g0_ref[...]) * (1.0 + _mod_chunk(mod_ref, 1, d)) + _mod_chunk(mod_ref, 0, d)
    a = _dot(h.astype(BF16), wd_ref[...])
    cq = _rms(a[:, :Q_LORA], gq_ref[...])
    ckv = _rms(a[:, Q_LORA:Q_LORA + KV_LORA], gkv_ref[...])
    rd = ROPE_DIM
    cos = cos_ref[:, 0:rd]
    sin = sin_ref[:, 0:rd]
    o = Q_LORA + KV_LORA
    k_rot = (a[:, o:o + rd] * cos + a[:, o + rd:o + 2 * rd] * sin).astype(BF16)
    qa = _dot(cq.astype(BF16), wq_ref[...])
    kva = _dot(ckv.astype(BF16), wkv_ref[...])
    hw = MLA_HEADS * 128
    hr = MLA_HEADS * rd
    zpad = jnp.zeros((x.shape[0], QK_PAD - NOPE_DIM - rd), BF16)
    scale = (NOPE_DIM + ROPE_DIM) ** -0.5 * LOG2_E
    for hd in range(MLA_HEADS):
        lo = hd * 128
        q_rot = qa[:, hw + hd * rd:hw + (hd + 1) * rd] * cos + qa[:, hw + hr + hd * rd:hw + hr + (hd + 1) * rd] * sin
        q_ref[:, hd * QK_PAD:hd * QK_PAD + 128] = (qa[:, lo:lo + 128] * scale).astype(BF16)
        q_ref[:, hd * QK_PAD + 128:hd * QK_PAD + 128 + rd] = (q_rot * scale).astype(BF16)
        q_ref[:, hd * QK_PAD + 128 + rd:(hd + 1) * QK_PAD] = zpad
        k_ref[:, hd * QK_PAD:hd * QK_PAD + 128] = kva[:, lo:lo + 128].astype(BF16)
        k_ref[:, hd * QK_PAD + 128:hd * QK_PAD + 128 + rd] = k_rot
        k_ref[:, hd * QK_PAD + 128 + rd:(hd + 1) * QK_PAD] = zpad
        v_ref[:, hd * V_PAD:hd * V_PAD + V_DIM] = kva[:, hw + lo:hw + lo + 128].astype(BF16)
        v_ref[:, hd * V_PAD + V_DIM:(hd + 1) * V_PAD] = jnp.ones((x.shape[0], V_PAD - V_DIM), BF16)


def _pre_mla(x, mods, g0, wd, gq, gkv, wq, wkv, cos_t, sin_t, tm):
    b, n, d = x.shape
    nb_mod = mods.shape[0]
    full = lambda a: pl.BlockSpec(a.shape, lambda i, j: (0,) * a.ndim)
    mod_map = (lambda i, j: (i, 0, 0)) if nb_mod > 1 else (lambda i, j: (0, 0, 0))
    qk_w = MLA_HEADS * QK_PAD
    v_w = MLA_HEADS * V_PAD
    return pl.pallas_call(
        _pre_mla_kernel,
        out_shape=(jax.ShapeDtypeStruct((b, n, qk_w), BF16),
                   jax.ShapeDtypeStruct((b, n, qk_w), BF16),
                   jax.ShapeDtypeStruct((b, n, v_w), BF16)),
        grid=(b, n // tm),
        in_specs=[pl.BlockSpec((None, tm, d), lambda i, j: (i, j, 0)),
                  pl.BlockSpec((None, 1, mods.shape[-1]), mod_map),
                  full(g0), full(wd), full(gq), full(gkv), full(wq), full(wkv),
                  pl.BlockSpec((tm, 128), lambda i, j: (j, 0)),
                  pl.BlockSpec((tm, 128), lambda i, j: (j, 0))],
        out_specs=(pl.BlockSpec((None, tm, qk_w), lambda i, j: (i, j, 0)),
                   pl.BlockSpec((None, tm, qk_w), lambda i, j: (i, j, 0)),
                   pl.BlockSpec((None, tm, v_w), lambda i, j: (i, j, 0))),
        compiler_params=_cparams(("arbitrary", "arbitrary")),
        name="pre_mla",
    )(x, mods, g0, wd, gq, gkv, wq, wkv, cos_t, sin_t)


def _attn_kernel(*refs, n_seg):
    q_ref = refs[0]
    k_refs = refs[1:1 + n_seg]
    v_refs = refs[1 + n_seg:1 + 2 * n_seg]
    o_ref = refs[1 + 2 * n_seg]
    nt = (((1,), (1,)), ((), ()))

    def scores(hd):
        q = q_ref[:, hd * QK_PAD:(hd + 1) * QK_PAD]
        return [lax.dot_general(q, k[:, hd * QK_PAD:(hd + 1) * QK_PAD], nt, preferred_element_type=F32)
                for k in k_refs]

    nxt = scores(0)
    for hd in range(MLA_HEADS):
        ss = nxt
        if hd + 1 < MLA_HEADS:
            nxt = scores(hd + 1)
        m = ss[0].max(axis=-1, keepdims=True)
        for s in ss[1:]:
            m = jnp.maximum(m, s.max(axis=-1, keepdims=True))
        acc = None
        for s, v in zip(ss, v_refs):
            pv = _dot(jnp.exp2((s - m).astype(BF16)), v[:, hd * V_PAD:(hd + 1) * V_PAD])
            acc = pv if acc is None else acc + pv
        o_ref[:, hd * V_DIM:(hd + 1) * V_DIM] = (acc[:, :V_DIM] / acc[:, V_DIM:V_DIM + 1]).astype(BF16)


def _attention(q, ks, vs, tq):
    b, nq, qk_w = q.shape
    v_w = MLA_HEADS * V_DIM
    kv_spec = lambda a: pl.BlockSpec((None,) + a.shape[1:], lambda i, j: (i, 0, 0))
    return pl.pallas_call(
        functools.partial(_attn_kernel, n_seg=len(ks)),
        out_shape=jax.ShapeDtypeStruct((b, nq, v_w), BF16),
        grid=(b, nq // tq),
        in_specs=[pl.BlockSpec((None, tq, qk_w), lambda i, j: (i, j, 0))]
                 + [kv_spec(a) for a in ks] + [kv_spec(a) for a in vs],
        out_specs=pl.BlockSpec((None, tq, v_w), lambda i, j: (i, j, 0)),
        compiler_params=_cparams(("arbitrary", "arbitrary")),
        name="mla_attention",
    )(q, *ks, *vs)


SUB_ROWS = 256


def _sub_tiles(n):
    return [slice(r, r + SUB_ROWS) for r in range(0, n, SUB_ROWS)]


def _post_core(o, x, rows, mod_ref, g1_ref, g2_ref, wr_ref, x1_ref, fin_ref, lg_ref):
    d = x.shape[-1]
    ne = lg_ref.shape[0]
    gate = _rows(_mod_chunk(mod_ref, 2, d), x)
    shift = _rows(_mod_chunk(mod_ref, 3, d), x)
    scale = _rows(_mod_chunk(mod_ref, 4, d), x)
    x1 = x + gate * _rms(o, g1_ref[...])
    fin = _rms(x1, g2_ref[...]) * (1.0 + scale) + shift
    x1_ref[rows, :] = x1
    _pack_store(fin_ref, fin, row0=rows.start)
    nt = (((1,), (1,)), ((), ()))
    f_hi = fin.astype(BF16)
    f_lo = (fin - f_hi.astype(F32)).astype(BF16)
    r_hi = lax.dot_general(wr_ref[...], f_hi, nt, preferred_element_type=F32)
    r_lo = lax.dot_general(wr_ref[0:ne, :], f_lo, nt, preferred_element_type=F32)
    lg_ref[:, rows] = r_hi[:ne] + r_hi[ne:] + r_lo


def _post_proj_kernel(o_ref, wo_ref, x_ref, mod_ref, g1_ref, g2_ref, wr_ref, *rest):
    x1_ref, fin_ref, lg_ref = rest[-3:]
    for rows in _sub_tiles(x_ref.shape[0]):
        o = _dot(o_ref[rows, :], wo_ref[...])
        _post_core(o, x_ref[rows, :], rows, mod_ref, g1_ref, g2_ref, wr_ref, x1_ref, fin_ref, lg_ref)


def _post_proj_tm_kernel(o_ref, wo_ref, x_ref, mod_ref, g1_ref, g2_ref, wr_ref, *rest):
    x1_ref, fin_ref, lg_ref = rest[-3:]
    nb, tt, d = x_ref.shape
    ts = SUB_ROWS // nb
    for t0 in range(0, tt, ts):
        o = _dot(o_ref[:, t0:t0 + ts, :].reshape(nb * ts, o_ref.shape[-1]), wo_ref[...])
        o = jnp.swapaxes(o.reshape(nb, ts, d), 0, 1).reshape(ts * nb, d)
        x = jnp.swapaxes(x_ref[:, t0:t0 + ts, :], 0, 1).reshape(ts * nb, d)
        _post_core(o, x, slice(t0 * nb, (t0 + ts) * nb), mod_ref, g1_ref, g2_ref, wr_ref, x1_ref, fin_ref, lg_ref)


def _chunk_to_rows(yc_ref, c, nb):
    q, grp = S5_Q, S5_GROUP
    per_tile = LANES // grp
    lane_grp = lax.broadcasted_iota(jnp.int32, (nb, LANES), 1) // grp
    n_pairs = yc_ref.shape[0]
    pieces = [[yc_ref[p, c * nb:(c + 1) * nb, lt * LANES:(lt + 1) * LANES] for lt in range(2 * q // per_tile)]
              for p in range(n_pairs)]
    out_rows = []
    for t in range(q):
        tiles = []
        for lb in range(2 * n_pairs // per_tile):
            dest = None
            for g8 in range(per_tile):
                g = lb * per_tile + g8
                piece = pieces[g // 2][(g % 2) * (q // per_tile) + t // per_tile]
                r = pltpu.roll(piece, ((g8 - t % per_tile) * grp) % LANES, 1)
                dest = r if dest is None else jnp.where(lane_grp == g8, r, dest)
            tiles.append(dest)
        out_rows.append(jnp.concatenate(tiles, axis=-1))
    return jnp.concatenate(out_rows, axis=0)


def _post_glu_kernel(h_ref, yc_ref, dsk_ref, wg_ref, bg_ref, x_ref, mod_ref, g1_ref, g2_ref, wr_ref,
                     *rest):
    x1_ref, fin_ref, lg_ref = rest[-3:]
    d = x_ref.shape[-1]
    nb = mod_ref.shape[0]
    assert SUB_ROWS == S5_Q * nb
    for ci, rows in enumerate(_sub_tiles(x_ref.shape[0])):
        h = jnp.concatenate([h_ref[g, rows, :] for g in range(h_ref.shape[0])], axis=-1).astype(F32)
        y = h * dsk_ref[...] + _chunk_to_rows(yc_ref, ci, nb)
        z = _dot(jax.nn.gelu(y, approximate=True).astype(BF16), wg_ref[...]) + bg_ref[...]
        o = z[:, :d] * jax.nn.sigmoid(z[:, d:])
        _post_core(o, x_ref[rows, :], rows, mod_ref, g1_ref, g2_ref, wr_ref, x1_ref, fin_ref, lg_ref)


def _post_mixer(kernel, tok_inputs, consts, x, n_tok, x_off, mods, g1, g2, wr_t, tm, rows_per_mod, name,
                moe_total=None, moe_off=0, prev=None, x_spec=None):
    d = x.shape[-1]
    ne = wr_t.shape[0] // 2
    moe_total = n_tok if moe_total is None else moe_total
    tiles_per_mod = rows_per_mod // tm
    xo, mo = x_off // tm, moe_off // tm
    full = lambda a: pl.BlockSpec(a.shape, lambda i: (0,) * a.ndim)
    tile = pl.BlockSpec((tm, d), lambda i: (i, 0))
    mod_spec = pl.BlockSpec((None,) + mods.shape[1:], lambda i: (i // tiles_per_mod, 0, 0))
    x_spec = pl.BlockSpec((tm, d), lambda i: (i + xo, 0)) if x_spec is None else x_spec
    in_specs = ([spec for _, spec in tok_inputs] + [full(a) for a in consts]
                + [x_spec, mod_spec, full(g1), full(g2), full(wr_t)])
    args = [a for a, _ in tok_inputs] + list(consts) + [x, mods, g1, g2, wr_t]
    aliases = {}
    if prev is not None:
        aliases = {len(args): 1, len(args) + 1: 2}
        in_specs += [pl.BlockSpec(memory_space=pl.ANY)] * 2
        args += list(prev)
    return pl.pallas_call(
        kernel,
        out_shape=(jax.ShapeDtypeStruct((n_tok, d), F32),
                   jax.ShapeDtypeStruct((moe_total * PACK_ROWS, LANES), U32),
                   jax.ShapeDtypeStruct((ne, moe_total), F32)),
        grid=(n_tok // tm,),
        in_specs=in_specs,
        out_specs=(tile, pl.BlockSpec((tm * PACK_ROWS, LANES), lambda i: (i + mo, 0)),
                   pl.BlockSpec((ne, tm), lambda i: (0, i + mo))),
        input_output_aliases=aliases,
        compiler_params=_cparams(("arbitrary",)),
        name=name,
    )(*args)


def _route_kernel(lg_ref, bias_ref, eidx_ref, gate_ref, rank_ref, cnt_ref, tri_ref, base_ref):
    i = pl.program_id(0)
    ne, tt = lg_ref.shape
    gsz = ne // N_EXPERT_GROUPS
    shp = (N_EXPERT_GROUPS, gsz, tt)
    neg = -jnp.inf

    @pl.when(i == 0)
    def _():
        base_ref[...] = jnp.zeros_like(base_ref)
        r = lax.broadcasted_iota(jnp.int32, (tt, tt), 0)
        c = lax.broadcasted_iota(jnp.int32, (tt, tt), 1)
        tri_ref[...] = (r < c).astype(BF16)

    scores = jax.nn.sigmoid(lg_ref[...])
    s3 = scores.reshape(shp)
    b3 = (scores + bias_ref[...]).reshape(shp)
    io_e = lax.broadcasted_iota(jnp.int32, shp, 1)
    io_g = lax.broadcasted_iota(jnp.int32, shp, 0)
    io_flat = io_g * gsz + io_e
    m1 = b3.max(axis=1, keepdims=True)
    i1 = jnp.where(b3 == m1, io_e, gsz).min(axis=1, keepdims=True)
    m2 = jnp.where(io_e == i1, neg, b3).max(axis=1, keepdims=True)
    cur = jnp.broadcast_to(m1 + m2, shp)
    gsel = jnp.zeros(shp, jnp.bool_)
    for _ in range(TOPK_GROUPS):
        m = cur.max(axis=0, keepdims=True)
        gi = jnp.where(cur == m, io_g, N_EXPERT_GROUPS).min(axis=0, keepdims=True)
        hit = io_g == gi
        gsel = jnp.logical_or(gsel, hit)
        cur = jnp.where(hit, neg, cur)
    cand = jnp.where(gsel, b3, neg)
    sel = jnp.zeros(shp, jnp.bool_)
    eids, gts = [], []
    for _ in range(TOP_K):
        m = cand.max(axis=0, keepdims=True).max(axis=1, keepdims=True)
        ei = jnp.where(cand == m, io_flat, ne).min(axis=0, keepdims=True).min(axis=1, keepdims=True)
        hit = io_flat == ei
        gts.append(jnp.where(hit, s3, 0.0).sum(axis=0, keepdims=True).sum(axis=1, keepdims=True))
        eids.append(ei)
        sel = jnp.logical_or(sel, hit)
        cand = jnp.where(hit, neg, cand)
    gsum = gts[0]
    for g in gts[1:]:
        gsum = gsum + g
    self32 = sel.astype(F32).reshape(ne, tt)
    cnt = _dot(self32.astype(BF16), tri_ref[...]) + base_ref[...]
    cnt3 = cnt.reshape(shp)
    for k in range(TOP_K):
        hit = io_flat == eids[k]
        rk = jnp.where(hit, cnt3, 0.0).sum(axis=0, keepdims=True).sum(axis=1, keepdims=True)
        rank_ref[k:k + 1, :] = rk.reshape(1, tt).astype(jnp.int32)
        eidx_ref[k:k + 1, :] = eids[k].reshape(1, tt)
        gate_ref[k:k + 1, :] = (gts[k] / gsum * ROUTED_SCALE).reshape(1, tt)
    base_new = base_ref[...] + self32.sum(axis=1, keepdims=True)
    base_ref[...] = base_new
    cnt_ref[...] = jnp.broadcast_to(base_new, cnt_ref.shape)


def _route(logits_t, bias, tt, tok0, t):
    ne = logits_t.shape[0]
    off = tok0 // tt
    out_i = jax.ShapeDtypeStruct((TOP_K, t), jnp.int32)
    row = pl.BlockSpec((TOP_K, tt), lambda i: (0, i))
    return pl.pallas_call(
        _route_kernel,
        out_shape=(out_i, jax.ShapeDtypeStruct((TOP_K, t), F32), out_i,
                   jax.ShapeDtypeStruct((ne, 128), F32)),
        grid=(t // tt,),
        in_specs=[pl.BlockSpec((ne, tt), lambda i: (0, i + off)),
                  pl.BlockSpec((ne, 1), lambda i: (0, 0))],
        out_specs=(row, row, row, pl.BlockSpec((ne, 128), lambda i: (0, 0))),
        scratch_shapes=[pltpu.VMEM((tt, tt), BF16), pltpu.VMEM((ne, 1), F32)],
        compiler_params=_cparams(("arbitrary",)),
        name="moe_route",
    )(logits_t, bias.reshape(ne, 1))


def _dest_kernel(eidx_ref, rank_ref, start_ref, dest_ref):
    kk, tt = eidx_ref.shape
    ne = start_ref.shape[0]
    n_chunk, _, r = dest_ref.shape
    io_e = lax.broadcasted_iota(jnp.int32, (ne, tt), 0)
    start = start_ref[...]
    for k in range(kk):
        hit = io_e == eidx_ref[k:k + 1, :]
        dk = jnp.where(hit, start, 0).sum(axis=0, keepdims=True) + rank_ref[k:k + 1, :]
        for c in range(n_chunk):
            dest_ref[c, k:k + 1, :] = dk[:, c * r:(c + 1) * r]


def _dest_rows(eidx_t, rank_t, start, tt, r):
    kk, t = eidx_t.shape
    ne = start.shape[0]
    return pl.pallas_call(
        _dest_kernel,
        out_shape=jax.ShapeDtypeStruct((t // r, kk, r), jnp.int32),
        grid=(t // tt,),
        in_specs=[pl.BlockSpec((kk, tt), lambda i: (0, i)),
                  pl.BlockSpec((kk, tt), lambda i: (0, i)),
                  pl.BlockSpec((ne, 1), lambda i: (0, 0))],
        out_specs=pl.BlockSpec((tt // r, kk, r), lambda i: (i, 0, 0)),
        compiler_params=_cparams(("arbitrary",)),
        name="moe_dest",
    )(eidx_t, rank_t, start.reshape(ne, 1))


SC_CHUNK = 64


def _sc_mesh():
    return plsc.VectorSubcoreMesh(core_axis_name="c", subcore_axis_name="s")


def _sc_workers():
    info = plsc.get_sparse_core_info()
    return info.num_cores, info.num_cores * info.num_subcores


def _sc_scatter_rows(rows, dest, n_out, row0=0):
    n_chunk, kk, r = dest.shape
    nc, nw = _sc_workers()
    cpw = n_chunk // nw
    assert cpw * nw == n_chunk and cpw % 2 == 0 and row0 % r == 0 and row0 + n_chunk * r <= rows.shape[0]

    @functools.partial(
        pl.kernel, mesh=_sc_mesh(),
        out_type=jax.ShapeDtypeStruct((n_out,) + rows.shape[1:], rows.dtype),
        scratch_types=[pltpu.VMEM((2, kk, r), jnp.int32), pltpu.VMEM((2, r) + rows.shape[1:], rows.dtype),
                       pltpu.SemaphoreType.DMA((2,)), pltpu.SemaphoreType.DMA((2,))])
    def scatter(rows_hbm, dest_hbm, out_hbm, idx_v, rows_v, load_sem, scat_sem):
        c0 = (lax.axis_index("s") * nc + lax.axis_index("c")) * cpw

        def loads(c, b):
            return (pltpu.make_async_copy(dest_hbm.at[c], idx_v.at[b], load_sem.at[b]),
                    pltpu.make_async_copy(rows_hbm.at[pl.ds(row0 + c * r, r)], rows_v.at[b], load_sem.at[b]))

        def scat(b, k):
            return pltpu.make_async_copy(rows_v.at[b], out_hbm.at[idx_v.at[b, k]], scat_sem.at[b])

        for cp in loads(c0, 0):
            cp.start()

        @pl.loop(0, cpw, step=2)
        def _(ci):
            for b in range(2):
                c = c0 + ci + b
                for cp in loads(c, b):
                    cp.wait()
                for k in range(kk):
                    scat(b, k).start()

                @pl.when(ci + b >= 1)
                def _():
                    for k in range(kk):
                        scat(1 - b, k).wait()

                @pl.when(ci + b + 1 < cpw)
                def _():
                    for cp in loads(c + 1, 1 - b):
                        cp.start()

        for k in range(kk):
            scat((cpw - 1) % 2, k).wait()

    return scatter(rows, dest)


def _sc_gather_rows(src, dest):
    n_chunk, kk, r = dest.shape
    t = n_chunk * r
    nc, nw = _sc_workers()
    cpw = n_chunk // nw
    nbuf = 3
    assert cpw * nw == n_chunk and kk > nbuf

    @functools.partial(
        pl.kernel, mesh=_sc_mesh(),
        out_type=jax.ShapeDtypeStruct((kk, t) + src.shape[1:], src.dtype),
        scratch_types=[pltpu.VMEM((kk, r), jnp.int32), pltpu.VMEM((nbuf, r) + src.shape[1:], src.dtype),
                       pltpu.SemaphoreType.DMA((nbuf,)), pltpu.SemaphoreType.DMA((nbuf,))])
    def gather(src_hbm, dest_hbm, out_hbm, idx_v, rows_v, get_sem, put_sem):
        c0 = (lax.axis_index("s") * nc + lax.axis_index("c")) * cpw

        @pl.loop(0, cpw)
        def _(ci):
            c = c0 + ci
            pltpu.sync_copy(dest_hbm.at[c], idx_v)

            def get(k):
                return pltpu.make_async_copy(src_hbm.at[idx_v.at[k]], rows_v.at[k % nbuf], get_sem.at[k % nbuf])

            def put(k):
                return pltpu.make_async_copy(rows_v.at[k % nbuf], out_hbm.at[k, pl.ds(c * r, r)],
                                             put_sem.at[k % nbuf])

            for k in range(nbuf - 1):
                get(k).start()
            for k in range(kk):
                get(k).wait()
                put(k).start()
                if k + nbuf - 1 < kk:
                    if k >= 1:
                        put(k - 1).wait()
                    get(k + nbuf - 1).start()
            for k in range(kk - nbuf, kk):
                put(k).wait()

    return gather(src, dest)


def _expert_kernel(be_ref, nu_ref, x_ref, wg_ref, wu_ref, wd_ref, o_ref, wgu_s, wd_s):
    i = pl.program_id(0)
    tb = o_ref.shape[0] // PACK_ROWS

    @pl.when(i < nu_ref[0])
    def _():
        @pl.when(jnp.logical_or(i == 0, be_ref[i] != be_ref[jnp.maximum(i - 1, 0)]))
        def _():
            wgu_s[:, :D_EXPERT] = wg_ref[...].astype(BF16)
            wgu_s[:, D_EXPERT:] = wu_ref[...].astype(BF16)
            wd_s[...] = wd_ref[...].astype(BF16)

        x = jnp.concatenate([v.astype(BF16) for v in _unpack_load(x_ref, tb)], axis=-1)
        gu = _dot(x, wgu_s[...])
        g = gu[:, :D_EXPERT]
        h = g * jax.nn.sigmoid(g) * gu[:, D_EXPERT:]
        _pack_store(o_ref, _dot(h.astype(BF16), wd_s[...]))


def _experts(xs, blk_e, n_used, w_gate, w_up, w_down, layer, tb):
    rows = xs.shape[0] // PACK_ROWS
    _, ne, d, de = w_gate.shape
    nb = rows // tb
    row_map = lambda i, be, nu: (jnp.minimum(i, nu[0] - 1), 0)
    w_map = lambda i, be, nu: (layer, be[i], 0, 0)
    grid_spec = pltpu.PrefetchScalarGridSpec(
        num_scalar_prefetch=2,
        grid=(nb,),
        in_specs=[pl.BlockSpec((tb * PACK_ROWS, LANES), row_map),
                  pl.BlockSpec((None, None, d, de), w_map),
                  pl.BlockSpec((None, None, d, de), w_map),
                  pl.BlockSpec((None, None, de, d), w_map)],
        out_specs=pl.BlockSpec((tb * PACK_ROWS, LANES), row_map),
        scratch_shapes=[pltpu.VMEM((d, 2 * de), BF16), pltpu.VMEM((de, d), BF16)],
    )
    return pl.pallas_call(
        _expert_kernel,
        out_shape=jax.ShapeDtypeStruct(xs.shape, U32),
        grid_spec=grid_spec,
        compiler_params=_cparams(("arbitrary",)),
        name="moe_experts",
    )(blk_e, n_used, xs, w_gate, w_up, w_down)


def _combine_kernel(yk_ref, gate_ref, fin_ref, shgu_ref, shd_ref, x1_ref, mod_ref, g3_ref, *rest, fuse_next):
    if fuse_next:
        nmod_ref, ng0_ref = rest[0], rest[1]
        o_ref, h_ref = rest[-2], rest[-1]
    else:
        o_ref = rest[-1]
    tm, d = x1_ref.shape
    for rows in _sub_tiles(tm):
        n, r0 = SUB_ROWS, rows.start
        gates = gate_ref[rows, :]
        blocks = None
        for k in range(TOP_K):
            gk = gates[:, k:k + 1]
            terms = [gk * v for v in _unpack_load(yk_ref, n, lead=(k,), row0=r0)]
            blocks = terms if blocks is None else [a + b for a, b in zip(blocks, terms)]
        fin = jnp.concatenate([v.astype(BF16) for v in _unpack_load(fin_ref, n, row0=r0)], axis=-1)
        gu = _dot(fin, shgu_ref[...])
        g = gu[:, :D_EXPERT]
        hsh = g * jax.nn.sigmoid(g) * gu[:, D_EXPERT:]
        f = jnp.concatenate(blocks, axis=-1) + _dot(hsh.astype(BF16), shd_ref[...])
        x1 = x1_ref[rows, :]
        x2 = x1 + _rows(_mod_chunk(mod_ref, 5, d), x1) * _rms(f, g3_ref[...])
        if len(o_ref.shape) == 2:
            o_ref[rows, :] = x2
        else:
            nb = o_ref.shape[0]
            ts = SUB_ROWS // nb
            o_ref[:, r0 // nb:r0 // nb + ts, :] = jnp.swapaxes(x2.reshape(ts, nb, d), 0, 1)
        if fuse_next:
            hn = (_rms(x2, ng0_ref[...]) * (1.0 + _rows(_mod_chunk(nmod_ref, 1, d), x2))
                  + _rows(_mod_chunk(nmod_ref, 0, d), x2))
            for gi in range(h_ref.shape[0]):
                h_ref[gi, rows, :] = hn[:, gi * LANES:(gi + 1) * LANES].astype(BF16)


def _combine(yk, gates, fin, shgu, shd, x1, mods, g3, tm, rows_per_mod, n_tok, x_off, yk_off, fin_off,
             batch_out=0, prev=None, out_rows=None, out_off=None, nxt=None):
    t, d = x1.shape
    out_rows = t if out_rows is None else out_rows
    out_off = x_off if out_off is None else out_off
    xo, yo, fo, oo = x_off // tm, yk_off // tm, fin_off // tm, out_off // tm
    tiles_per_mod = rows_per_mod // tm
    full = lambda a: pl.BlockSpec(a.shape, lambda i: (0,) * a.ndim)
    if batch_out:
        out_shape = [jax.ShapeDtypeStruct((batch_out, out_rows // batch_out, d), F32)]
        out_specs = [pl.BlockSpec((batch_out, tm // batch_out, d), lambda i: (0, i + oo, 0))]
    else:
        out_shape = [jax.ShapeDtypeStruct((out_rows, d), F32)]
        out_specs = [pl.BlockSpec((tm, d), lambda i: (i + oo, 0))]
    in_specs = [pl.BlockSpec((TOP_K, tm * PACK_ROWS, LANES), lambda i: (0, i + yo, 0)),
                pl.BlockSpec((tm, TOP_K), lambda i: (i + yo, 0)),
                pl.BlockSpec((tm * PACK_ROWS, LANES), lambda i: (i + fo, 0)),
                full(shgu), full(shd),
                pl.BlockSpec((tm, d), lambda i: (i + xo, 0)),
                pl.BlockSpec((None,) + mods.shape[1:], lambda i: ((i + xo) // tiles_per_mod, 0, 0)),
                full(g3)]
    args = [yk, gates, fin, shgu, shd, x1, mods, g3]
    if nxt is not None:
        in_specs += [full(nxt[0]), full(nxt[1])]
        args += list(nxt)
        out_shape.append(jax.ShapeDtypeStruct((d // LANES, out_rows, LANES), BF16))
        out_specs.append(pl.BlockSpec((d // LANES, tm, LANES), lambda i: (0, i + oo, 0)))
    aliases = {}
    if prev is not None:
        for j, p in enumerate(prev if isinstance(prev, (tuple, list)) else [prev]):
            in_specs.append(pl.BlockSpec(memory_space=pl.ANY))
            aliases[len(args)] = j
            args.append(p)
    out = pl.pallas_call(
        functools.partial(_combine_kernel, fuse_next=nxt is not None),
        out_shape=tuple(out_shape),
        grid=(n_tok // tm,),
        in_specs=in_specs,
        out_specs=tuple(out_specs),
        input_output_aliases=aliases,
        compiler_params=_cparams(("arbitrary",)),
        name="moe_combine",
    )(*args)
    return out if nxt is not None else out[0]


def _moe(fin, logits_t, bias, w_gate, w_up, w_down, layer, tb, tok0, t):
    t_all = fin.shape[0] // PACK_ROWS
    ne = w_gate.shape[1]
    tt = 512
    eidx_t, gates_t, rank_t, cnt = _route(logits_t, bias, tt, tok0, t)
    counts = cnt[:, 0].astype(jnp.int32)
    padded = (counts + tb - 1) // tb * tb
    pad_end = jnp.cumsum(padded)
    pad_start = pad_end - padded
    nb = (t * TOP_K) // tb + ne
    n_used = pad_end[-1] // tb
    blk_start = jnp.arange(nb, dtype=jnp.int32) * tb
    blk = jnp.sum(pad_end[None, :] <= jnp.minimum(blk_start, pad_end[-1] - 1)[:, None], axis=1)
    blk_e = jnp.minimum(blk, ne - 1).astype(jnp.int32)
    dest = _dest_rows(eidx_t, rank_t, pad_start, tt, SC_CHUNK)
    xs = _sc_scatter_rows(fin.reshape(t_all, PACK_ROWS, LANES), dest, nb * tb, row0=tok0)
    ys = _experts(xs.reshape(nb * tb * PACK_ROWS, LANES), blk_e, n_used.reshape(1).astype(jnp.int32),
                  w_gate, w_up, w_down, layer, tb)
    yk = _sc_gather_rows(ys.reshape(nb * tb, PACK_ROWS, LANES), dest)
    return yk.reshape(TOP_K, t * PACK_ROWS, LANES), gates_t.T


def _pre_s5_kernel(x_ref, mod_ref, g0_ref, *refs):
    h_ref, xt_ref = refs[-2:]
    nb, tt, d = x_ref.shape
    x = jnp.swapaxes(x_ref[...], 0, 1).reshape(tt * nb, d)
    h = (_rms(x, g0_ref[...]) * (1.0 + _rows(_mod_chunk(mod_ref, 1, d), x))
         + _rows(_mod_chunk(mod_ref, 0, d), x))
    for g in range(h_ref.shape[0]):
        h_ref[g] = h[:, g * LANES:(g + 1) * LANES].astype(BF16)
    xt_ref[...] = x


def _pre_s5(x, mods, g0, n_total, t_off, prev, tt):
    nb, n, d = x.shape
    off = t_off // tt
    out_shape = (jax.ShapeDtypeStruct((d // LANES, n_total * nb, LANES), BF16),
                 jax.ShapeDtypeStruct((n_total * nb, d), F32))
    out_specs = (pl.BlockSpec((d // LANES, tt * nb, LANES), lambda i: (0, i + off, 0)),
                 pl.BlockSpec((tt * nb, d), lambda i: (i + off, 0)))
    in_specs = [pl.BlockSpec((nb, tt, d), lambda i: (0, i, 0)),
                pl.BlockSpec(mods.shape, lambda i: (0, 0)),
                pl.BlockSpec(g0.shape, lambda i: (0, 0))]
    args = (x, mods, g0)
    aliases = {}
    if prev is not None:
        in_specs += [pl.BlockSpec(memory_space=pl.ANY)] * 2
        args += tuple(prev)
        aliases = {3: 0, 4: 1}
    return pl.pallas_call(
        _pre_s5_kernel,
        out_shape=out_shape,
        grid=(n // tt,),
        in_specs=in_specs,
        out_specs=out_specs,
        input_output_aliases=aliases,
        compiler_params=_cparams(("arbitrary",)),
        name="pre_s5",
    )(*args)


def _s5_scan_kernel(h_ref, bm_ref, cm_ref, lam_ref, y_ref, bu0, bu1, xb0, xb1, st_ref):
    first = jnp.logical_and(jnp.logical_and(pl.program_id(0) == 0, pl.program_id(1) == 0), pl.program_id(2) == 0)
    dr = pl.program_id(1)
    s = pl.program_id(2)
    tc, nb, cw = h_ref.shape
    half = st_ref.shape[1] // 2

    @pl.when(first)
    def _():
        for r in (bu0, bu1, xb0, xb1, st_ref):
            r[...] = jnp.zeros_like(r)

    def stages(bu_w, bu_r, xb_w, xb_r):
        lr = jnp.broadcast_to(lam_ref[0:1, :], (nb, half))
        li = jnp.broadcast_to(lam_ref[1:2, :], (nb, half))
        fresh = s == 1
        xr = jnp.where(fresh, 0.0, st_ref[:, 0:half])
        xi = jnp.where(fresh, 0.0, st_ref[:, half:2 * half])
        tp = tc // S5_PIECES
        for p in range(S5_PIECES):
            for i in range(p * tp, (p + 1) * tp):
                t = i + dr * (tc - 1 - 2 * i)
                rows = pl.ds(pl.multiple_of(t * nb, nb), nb)
                nr = lr * xr - li * xi + bu_r[rows, 0:half]
                ni = lr * xi + li * xr + bu_r[rows, half:2 * half]
                xb_w[rows, 0:half] = nr.astype(BF16)
                xb_w[rows, half:2 * half] = ni.astype(BF16)
                xr, xi = nr, ni
            ts = slice(p * tp, (p + 1) * tp)
            mr = slice(p * tp * nb, (p + 1) * tp * nb)
            y_ref[ts] = _dot(xb_r[mr, :], cm_ref[...]).reshape(tp, nb, cw)
            bu_w[mr, :] = _dot(h_ref[ts].reshape(tp * nb, cw).astype(BF16), bm_ref[...])
        st_ref[:, 0:half] = xr
        st_ref[:, half:2 * half] = xi

    @pl.when(s % 2 == 0)
    def _():
        stages(bu0, bu1, xb1, xb0)

    @pl.when(s % 2 == 1)
    def _():
        stages(bu1, bu0, xb0, xb1)


def _s5_scan(h_all, n_ctx, bm, cm, lam, tc):
    ng, nt, nb, cw = h_all.shape
    nl = nt - n_ctx
    assert cw == S5_GROUPS_PER_BLOCK * S5_GROUP
    sw = 2 * S5_GROUPS_PER_BLOCK * S5_STATE
    ncc, n = n_ctx // tc, nt // tc

    def chunk(dr, j):
        j = jnp.clip(j, 0, n - 1)
        rev = jnp.where(j < ncc, ncc - 1 - j, n - 1 - (j - ncc))
        return jnp.where(dr == 0, j, rev)

    def out_map(g, dr, s):
        return (dr, g, chunk(dr, jnp.clip(s - 2, ncc, n - 1)) - ncc, 0, 0)

    return pl.pallas_call(
        _s5_scan_kernel,
        out_shape=jax.ShapeDtypeStruct((2, ng, nl, nb, cw), F32),
        grid=(ng, 2, n + 2),
        in_specs=[pl.BlockSpec((None, tc, nb, cw), lambda g, dr, s: (g, chunk(dr, s), 0, 0)),
                  pl.BlockSpec((None, None, cw, sw), lambda g, dr, s: (dr, g, 0, 0)),
                  pl.BlockSpec((None, None, sw, cw), lambda g, dr, s: (dr, g, 0, 0)),
                  pl.BlockSpec((None, None, 2, sw // 2), lambda g, dr, s: (dr, g, 0, 0))],
        out_specs=pl.BlockSpec((None, None, tc, nb, cw), out_map),
        scratch_shapes=[pltpu.VMEM((tc * nb, sw), F32), pltpu.VMEM((tc * nb, sw), F32),
                        pltpu.VMEM((tc * nb, sw), BF16), pltpu.VMEM((tc * nb, sw), BF16),
                        pltpu.VMEM((nb, sw), F32)],
        compiler_params=_cparams(("arbitrary", "arbitrary", "arbitrary")),
        name="s5_scan",
    )(h_all, bm, cm, lam)


def _s5_params(lam_re, lam_im, log_step, b_re, b_im, c_re, c_im):
    g, p = lam_re.shape[1:]
    gb = S5_GROUPS_PER_BLOCK
    nblk = g // gb
    step = jnp.exp(log_step)[..., None]
    mag = jnp.exp(lam_re * step)
    lb_re = mag * jnp.cos(lam_im * step)
    lb_im = mag * jnp.sin(lam_im * step)
    den = lam_re * lam_re + lam_im * lam_im
    f_re = ((lb_re - 1.0) * lam_re + lb_im * lam_im) / den
    f_im = (lb_im * lam_re - (lb_re - 1.0) * lam_im) / den
    bb_re = f_re[..., None] * b_re - f_im[..., None] * b_im
    bb_im = f_re[..., None] * b_im + f_im[..., None] * b_re
    eye = jnp.eye(gb, dtype=F32)

    def in_map(w):
        w = w.reshape(2, nblk, gb, p, S5_GROUP)
        return jnp.einsum("dnapi,ab->dnaibp", w, eye).reshape(2, nblk, gb * S5_GROUP, gb * p)

    def out_map(w):
        w = w.reshape(2, nblk, gb, S5_GROUP, p)
        return jnp.einsum("dnaip,ab->dnapbi", w, eye).reshape(2, nblk, gb * p, gb * S5_GROUP)

    bm = jnp.concatenate([in_map(bb_re), in_map(bb_im)], axis=-1).astype(BF16)
    cm = jnp.concatenate([out_map(c_re), out_map(-c_im)], axis=-2).astype(BF16)
    lam = jnp.stack([lb_re.reshape(2, nblk, gb * p), lb_im.reshape(2, nblk, gb * p)], axis=2)
    return bm, cm, lam


S5_Q = 16
S5_STEP_GROUPS = 4


def _s5c_kernel(h_ref, toep_ref, win_ref, wout_ref, lamq_ref, y_ref, u_ref, sre_ref, sim_ref, xin_ref, *,
                nb, n_ctx_chunks):
    npair, n_rows, kw = u_ref.shape
    half = kw // 2
    sw = 2 * S5_STATE
    n = n_rows // nb
    ncc = n_ctx_chunks
    per_tile = LANES // S5_GROUP
    lane_grp = lax.broadcasted_iota(jnp.int32, (nb, LANES), 1) // S5_GROUP
    steps_per_block = per_tile // S5_STEP_GROUPS

    def relayout_from(gl0):
        def relayout(c, carry):
            r0 = c * (S5_Q * nb)
            pieces = [h_ref[pl.ds(pl.multiple_of(r0 + s * nb, nb), nb), :].astype(F32) for s in range(S5_Q)]
            rows = pl.ds(pl.multiple_of(c * nb, nb), nb)
            for gq in range(S5_STEP_GROUPS):
                for j in range(S5_Q // per_tile):
                    dest = None
                    for s8 in range(per_tile):
                        r = pltpu.roll(pieces[j * per_tile + s8], ((s8 - gl0 - gq) * S5_GROUP) % LANES, 1)
                        dest = r if dest is None else jnp.where(lane_grp == s8, r, dest)
                    lo = (gq % 2) * half + j * LANES
                    u_ref[gq // 2, rows, lo:lo + LANES] = dest.astype(BF16)
            return carry

        lax.fori_loop(0, n, relayout, 0)

    for k in range(steps_per_block):
        pl.when(pl.program_id(0) % steps_per_block == k)(functools.partial(relayout_from, k * S5_STEP_GROUPS))
    for dr in range(2):
        for p in range(npair):
            s = _dot(u_ref[p], win_ref[dr, p])
            sre_ref[dr, :, p * sw:(p + 1) * sw] = s[:, :sw]
            sim_ref[dr, :, p * sw:(p + 1) * sw] = s[:, sw:]
    w = npair * sw
    lr = [jnp.broadcast_to(lamq_ref[dr, 0:1, :], (nb, w)) for dr in range(2)]
    li = [jnp.broadcast_to(lamq_ref[dr, 1:2, :], (nb, w)) for dr in range(2)]

    def step(j, carry):
        rev = jnp.where(j < ncc, ncc - 1 - j, n - 1 - (j - ncc))
        out = []
        for dr in range(2):
            xr, xi = carry[dr]
            c = j if dr == 0 else rev
            rows = pl.ds(pl.multiple_of(c * nb, nb), nb)
            for p in range(npair):
                xin_ref[dr, rows, 2 * p * sw:(2 * p + 1) * sw] = xr[:, p * sw:(p + 1) * sw].astype(BF16)
                xin_ref[dr, rows, (2 * p + 1) * sw:(2 * p + 2) * sw] = xi[:, p * sw:(p + 1) * sw].astype(BF16)
            nr = lr[dr] * xr - li[dr] * xi + sre_ref[dr, rows, :]
            ni = lr[dr] * xi + li[dr] * xr + sim_ref[dr, rows, :]
            out.append((nr, ni))
        return tuple(out)

    zero = jnp.zeros((nb, w), F32)
    lax.fori_loop(0, n, step, ((zero, zero), (zero, zero)), unroll=2)
    lat = slice(ncc * nb, n_rows)
    for p in range(npair):
        acc = None
        for dr in range(2):
            intra = jnp.concatenate([_dot(u_ref[p, lat, 0:half], toep_ref[dr, 2 * p]),
                                     _dot(u_ref[p, lat, half:kw], toep_ref[dr, 2 * p + 1])], axis=-1)
            term = intra + _dot(xin_ref[dr, lat, 2 * p * sw:(2 * p + 2) * sw], wout_ref[dr, p])
            acc = term if acc is None else acc + term
        y_ref[p] = acc


def _s5_chunked(h, toep, win, wout, lamq, nb, n_ctx):
    ng, n_tok, _ = h.shape
    n_groups = toep.shape[1]
    pp = S5_STEP_GROUPS // 2
    steps_per_block = (LANES // S5_GROUP) // S5_STEP_GROUPS
    n_rows = n_tok // S5_Q
    ncc = n_ctx // S5_Q
    lat_rows = n_rows - ncc * nb
    kw = 2 * S5_Q * S5_GROUP
    sw = 2 * S5_STATE
    return pl.pallas_call(
        functools.partial(_s5c_kernel, nb=nb, n_ctx_chunks=ncc),
        out_shape=jax.ShapeDtypeStruct((n_groups // 2, lat_rows, kw), F32),
        grid=(n_groups // S5_STEP_GROUPS,),
        in_specs=[pl.BlockSpec((None, n_tok, LANES), lambda i: (i // steps_per_block, 0, 0),
                               pipeline_mode=pl.Buffered(1)),
                  pl.BlockSpec((2, 2 * pp, kw // 2, kw // 2), lambda i: (0, i, 0, 0)),
                  pl.BlockSpec((2, pp, kw, 2 * sw), lambda i: (0, i, 0, 0)),
                  pl.BlockSpec((2, pp, 2 * sw, kw), lambda i: (0, i, 0, 0)),
                  pl.BlockSpec((2, None, 2, pp * sw), lambda i: (0, i, 0, 0))],
        out_specs=pl.BlockSpec((pp, lat_rows, kw), lambda i: (i, 0, 0)),
        scratch_shapes=[pltpu.VMEM((pp, n_rows, kw), BF16),
                        pltpu.VMEM((2, n_rows, pp * sw), F32), pltpu.VMEM((2, n_rows, pp * sw), F32),
                        pltpu.VMEM((2, n_rows, 2 * pp * sw), BF16)],
        compiler_params=_cparams(("arbitrary",)),
        name="s5_chunked",
    )(h, toep, win, wout, lamq)


def _s5c_params(lam_re, lam_im, log_step, b_re, b_im, c_re, c_im):
    hp = lax.Precision.HIGHEST
    q = S5_Q
    _, g, p = lam_re.shape
    ni = b_re.shape[-1]
    step = jnp.exp(log_step)[..., None]
    ar, ai = lam_re * step, lam_im * step
    tau = jnp.arange(q + 1, dtype=F32)[:, None, None, None]
    mag = jnp.exp(tau * ar)
    pr, pi = mag * jnp.cos(tau * ai), mag * jnp.sin(tau * ai)
    den = lam_re * lam_re + lam_im * lam_im
    f_re = ((pr[1] - 1.0) * lam_re + pi[1] * lam_im) / den
    f_im = (pi[1] * lam_re - (pr[1] - 1.0) * lam_im) / den
    bb_re = f_re[..., None] * b_re - f_im[..., None] * b_im
    bb_im = f_re[..., None] * b_im + f_im[..., None] * b_re
    cp_re = c_re[None] * pr[:, :, :, None, :] - c_im[None] * pi[:, :, :, None, :]
    cp_im = c_re[None] * pi[:, :, :, None, :] + c_im[None] * pr[:, :, :, None, :]
    taps = (jnp.einsum("tdgop,dgpi->tdgoi", cp_re[:q], bb_re, precision=hp)
            - jnp.einsum("tdgop,dgpi->tdgoi", cp_im[:q], bb_im, precision=hp))
    s_idx = jnp.arange(q)[:, None]
    t_idx = jnp.arange(q)[None, :]

    def toeplitz(dr):
        lag = (t_idx - s_idx) if dr == 0 else (s_idx - t_idx)
        k = taps[:, dr][jnp.clip(lag, 0, q - 1)]
        k = jnp.where((lag >= 0)[:, :, None, None, None], k, 0.0)
        return k.transpose(2, 0, 4, 1, 3).reshape(g, q * ni, q * ni)

    toep = jnp.stack([toeplitz(0), toeplitz(1)]).astype(BF16)

    def state_in(dr):
        e = (q - 1 - jnp.arange(q)) if dr == 0 else jnp.arange(q)
        er, ei = pr[e, dr], pi[e, dr]
        br, bi = bb_re[dr].transpose(0, 2, 1), bb_im[dr].transpose(0, 2, 1)
        w_re = er[:, :, None, :] * br[None] - ei[:, :, None, :] * bi[None]
        w_im = er[:, :, None, :] * bi[None] + ei[:, :, None, :] * br[None]
        fl = lambda a: a.transpose(1, 0, 2, 3).reshape(g, q * ni, p)
        return fl(w_re), fl(w_im)

    def state_out(dr):
        f = (jnp.arange(q) + 1) if dr == 0 else (q - jnp.arange(q))
        fl = lambda a: a.transpose(1, 3, 0, 2).reshape(g, p, q * ni)
        return fl(cp_re[f, dr]), fl(-cp_im[f, dr])

    z = lambda *shape: jnp.zeros(shape, F32)

    def pair_in(dr):
        w_re, w_im = state_in(dr)
        a_re, b_re_, a_im, b_im_ = w_re[0::2], w_re[1::2], w_im[0::2], w_im[1::2]
        zz = z(g // 2, q * ni, p)
        top = jnp.concatenate([a_re, zz, a_im, zz], axis=-1)
        bot = jnp.concatenate([zz, b_re_, zz, b_im_], axis=-1)
        return jnp.concatenate([top, bot], axis=1)

    def pair_out(dr):
        w_re, w_im = state_out(dr)
        zz = z(g // 2, p, q * ni)
        rows = [jnp.concatenate([w_re[0::2], zz], axis=-1), jnp.concatenate([zz, w_re[1::2]], axis=-1),
                jnp.concatenate([w_im[0::2], zz], axis=-1), jnp.concatenate([zz, w_im[1::2]], axis=-1)]
        return jnp.concatenate(rows, axis=1)

    win = jnp.stack([pair_in(0), pair_in(1)]).astype(BF16)
    wout = jnp.stack([pair_out(0), pair_out(1)]).astype(BF16)
    ng = S5_STEP_GROUPS
    lamq = jnp.stack([pr[q].reshape(2, g // ng, ng * p), pi[q].reshape(2, g // ng, ng * p)], axis=2)
    return toep, win, wout, lamq


def _s5c_params_fast(lam_re, lam_im, log_step, b_re, b_im, c_re, c_im):
    q = S5_Q
    _, g, p = lam_re.shape
    ni = b_re.shape[-1]
    step = jnp.exp(log_step)[..., None]
    ar, ai = lam_re * step, lam_im * step
    tau = jnp.arange(q + 1, dtype=F32)[None, None, :, None]
    mag = jnp.exp(tau * ar[:, :, None, :])
    pr = mag * jnp.cos(tau * ai[:, :, None, :])
    pi = mag * jnp.sin(tau * ai[:, :, None, :])
    den = lam_re * lam_re + lam_im * lam_im
    f_re = ((pr[:, :, 1] - 1.0) * lam_re + pi[:, :, 1] * lam_im) / den
    f_im = (pi[:, :, 1] * lam_re - (pr[:, :, 1] - 1.0) * lam_im) / den
    bb_re = f_re[..., None] * b_re - f_im[..., None] * b_im
    bb_im = f_re[..., None] * b_im + f_im[..., None] * b_re
    bt_re, bt_im = bb_re.transpose(0, 1, 3, 2), bb_im.transpose(0, 1, 3, 2)
    m_re = pr[:, :, None, :q, :] * bt_re[:, :, :, None, :] - pi[:, :, None, :q, :] * bt_im[:, :, :, None, :]
    m_im = pr[:, :, None, :q, :] * bt_im[:, :, :, None, :] + pi[:, :, None, :q, :] * bt_re[:, :, :, None, :]
    taps = (jnp.einsum("dgitp,dgop->dgito", m_re, c_re, precision=lax.Precision.HIGHEST)
            - jnp.einsum("dgitp,dgop->dgito", m_im, c_im, precision=lax.Precision.HIGHEST))
    zq = jnp.zeros_like(taps[0])

    def toeplitz(dr):
        k = taps[dr]
        if dr == 0:
            ext = jnp.concatenate([zq, k], axis=2)
            rows = [ext[:, :, q - s:2 * q - s, :] for s in range(q)]
        else:
            ext = jnp.concatenate([k[:, :, ::-1, :], zq], axis=2)
            rows = [ext[:, :, q - 1 - s:2 * q - 1 - s, :] for s in range(q)]
        return jnp.stack(rows, axis=1).reshape(g, q * ni, q * ni)

    toep = jnp.stack([toeplitz(0), toeplitz(1)]).astype(BF16)

    def state_in(dr):
        if dr == 0:
            er, ei = m_re[dr][:, :, ::-1, :], m_im[dr][:, :, ::-1, :]
        else:
            er, ei = m_re[dr], m_im[dr]
        fl = lambda a: a.transpose(0, 2, 1, 3).reshape(g, q * ni, p)
        return fl(er), fl(ei)

    def state_out(dr):
        if dr == 0:
            fr, fi = pr[dr][:, 1:q + 1], pi[dr][:, 1:q + 1]
        else:
            fr, fi = pr[dr][:, q:0:-1], pi[dr][:, q:0:-1]
        ct_re, ct_im = c_re[dr].transpose(0, 2, 1), c_im[dr].transpose(0, 2, 1)
        frp, fip = fr.transpose(0, 2, 1)[..., None], fi.transpose(0, 2, 1)[..., None]
        w_re = ct_re[:, :, None, :] * frp - ct_im[:, :, None, :] * fip
        w_im = -(ct_re[:, :, None, :] * fip + ct_im[:, :, None, :] * frp)
        return w_re.reshape(g, p, q * ni), w_im.reshape(g, p, q * ni)

    def pair_in(dr):
        w_re, w_im = state_in(dr)
        zz = jnp.zeros((g // 2, q * ni, p), F32)
        top = jnp.concatenate([w_re[0::2], zz, w_im[0::2], zz], axis=-1)
        bot = jnp.concatenate([zz, w_re[1::2], zz, w_im[1::2]], axis=-1)
        return jnp.concatenate([top, bot], axis=1)

    def pair_out(dr):
        w_re, w_im = state_out(dr)
        zz = jnp.zeros((g // 2, p, q * ni), F32)
        rows = [jnp.concatenate([w_re[0::2], zz], axis=-1), jnp.concatenate([zz, w_re[1::2]], axis=-1),
                jnp.concatenate([w_im[0::2], zz], axis=-1), jnp.concatenate([zz, w_im[1::2]], axis=-1)]
        return jnp.concatenate(rows, axis=1)

    win = jnp.stack([pair_in(0), pair_in(1)]).astype(BF16)
    wout = jnp.stack([pair_out(0), pair_out(1)]).astype(BF16)
    ng = S5_STEP_GROUPS
    lamq = jnp.stack([pr[:, :, q].reshape(2, g // ng, ng * p), pi[:, :, q].reshape(2, g // ng, ng * p)], axis=2)
    return toep, win, wout, lamq


def _rope_tables(n_tokens):
    rows = n_tokens // GRID_W
    row = jnp.repeat(jnp.arange(rows), GRID_W).astype(F32)
    col = jnp.tile(jnp.arange(GRID_W), rows).astype(F32)
    n_freq = ROPE_DIM // 4
    inv_freq = ROPE_BASE ** (-jnp.arange(n_freq, dtype=F32) / n_freq)
    ang = jnp.concatenate([row[:, None] * inv_freq, col[:, None] * inv_freq], axis=-1)
    cos, sin = jnp.cos(ang), jnp.sin(ang)
    z = jnp.zeros((n_tokens, 128 - ROPE_DIM), F32)
    return (jnp.concatenate([cos, cos, z], axis=-1), jnp.concatenate([-sin, sin, z], axis=-1))


def _router_halves(w_router):
    wt = w_router.T
    hi = wt.astype(BF16)
    lo = (wt - hi.astype(F32)).astype(BF16)
    return jnp.concatenate([hi, lo], axis=0)


def _split_pairs(w):
    ev, od = w[..., 0::2], w[..., 1::2]
    return jnp.concatenate([ev, od], axis=-1), jnp.concatenate([od, ev], axis=-1)


def _mla_weights(w_dqkv, w_uq, w_ukv):
    kp, kps = _split_pairs(w_dqkv[:, Q_LORA + KV_LORA:])
    wd = jnp.concatenate([w_dqkv[:, :Q_LORA + KV_LORA], kp, kps], axis=-1).astype(BF16)
    wq3 = w_uq.reshape(Q_LORA, MLA_HEADS, NOPE_DIM + ROPE_DIM)
    qp, qps = _split_pairs(wq3[:, :, NOPE_DIM:])
    wq = jnp.concatenate([wq3[:, :, :NOPE_DIM].reshape(Q_LORA, -1), qp.reshape(Q_LORA, -1),
                          qps.reshape(Q_LORA, -1)], axis=-1).astype(BF16)
    wkv3 = w_ukv.reshape(KV_LORA, MLA_HEADS, NOPE_DIM + V_DIM)
    wkv = jnp.concatenate([wkv3[:, :, :NOPE_DIM].reshape(KV_LORA, -1),
                           wkv3[:, :, NOPE_DIM:].reshape(KV_LORA, -1)], axis=-1).astype(BF16)
    return wd, wq, wkv


@jax.jit
def kernel(x, c, ctx, c_ctx, ada_w, ada_b, norm_g, mla_w_dqkv, mla_g_q, mla_g_kv, mla_w_uq, mla_w_ukv, mla_w_o, s5_lam_re, s5_lam_im, s5_log_step, s5_b_re, s5_b_im, s5_c_re, s5_c_im, s5_d, s5_w_glu, s5_b_glu, moe_w_router, moe_bias, moe_w_gate, moe_w_up, moe_w_down, sh_w_gate, sh_w_up, sh_w_down):
    b, l, d = x.shape
    n_ctx = ctx.shape[1]
    assert ada_w.shape[0] == 2 and b % 8 == 0
    ta = 256
    tm = 512
    tb = 512
    row = lambda v: v.reshape(1, -1)

    n_rows = (b + 1 + 7) // 8 * 8
    cvec = jnp.zeros((n_rows, d), F32).at[:b].set(c).at[b].set(c_ctx)
    mods = _ada_mods(cvec, ada_w, ada_b)

    def shared_weights(i):
        shgu = jnp.concatenate([sh_w_gate[i], sh_w_up[i]], axis=-1).astype(BF16)
        return shgu, sh_w_down[i].astype(BF16)

    mod_lat = mods[0, :b].reshape(b, 1, N_MOD * d)
    mod_ctx = mods[0, b].reshape(1, 1, N_MOD * d)
    wd, wq, wkv = _mla_weights(mla_w_dqkv[0], mla_w_uq[0], mla_w_ukv[0])
    cos_l, sin_l = _rope_tables(l)
    cos_c = jnp.concatenate([jnp.ones((n_ctx, ROPE_DIM), F32), jnp.zeros((n_ctx, 128 - ROPE_DIM), F32)], -1)
    sin_c = jnp.zeros((n_ctx, 128), F32)
    pre = functools.partial(_pre_mla, g0=row(norm_g[0, 0]), wd=wd, gq=row(mla_g_q[0]), gkv=row(mla_g_kv[0]),
                            wq=wq, wkv=wkv, tm=ta)
    q_c, k_c, v_c = pre(ctx, mod_ctx, cos_t=cos_c, sin_t=sin_c)
    q_l, k_l, v_l = pre(x, mod_lat, cos_t=cos_l, sin_t=sin_l)
    o_l = _attention(q_l, [k_c, k_l], [v_c, v_l], 2 * ta)
    o_c = _attention(q_c, [k_c], [v_c], n_ctx)

    wo = mla_w_o[0].astype(BF16)
    wr_t = _router_halves(moe_w_router[0])
    g1, g2, g3 = row(norm_g[0, 1]), row(norm_g[0, 2]), row(norm_g[0, 3])
    post = functools.partial(_post_mixer, _post_proj_kernel, consts=[wo], g1=g1, g2=g2, wr_t=wr_t, tm=tm,
                             name="post_mla")
    o_spec = pl.BlockSpec((tm, o_l.shape[-1]), lambda i: (i, 0))
    n_moe = b * (n_ctx + l)
    x1_c, fin, lg = post([(o_c.reshape(b * n_ctx, -1), o_spec)], x=ctx.reshape(b * n_ctx, d),
                         n_tok=b * n_ctx, x_off=0, mods=mod_ctx, rows_per_mod=b * n_ctx, moe_total=n_moe)
    tt = tm // b
    mod_lat_tm = mods[0, :b][None]
    x1_l, fin, lg = _post_mixer(
        _post_proj_tm_kernel, [(o_l, pl.BlockSpec((b, tt, o_l.shape[-1]), lambda i: (0, i, 0)))], [wo],
        x=x, n_tok=b * l, x_off=0, mods=mod_lat_tm, g1=g1, g2=g2, wr_t=wr_t, tm=tm, rows_per_mod=b * l,
        name="post_mla", moe_total=n_moe, moe_off=b * n_ctx, prev=(fin, lg),
        x_spec=pl.BlockSpec((b, tt, d), lambda i: (0, i, 0)))
    shgu, shd = shared_weights(0)
    lat_a = (l // 2) * b
    n_a = b * n_ctx + lat_a
    moe = functools.partial(_moe, fin, lg, moe_bias[0], moe_w_gate, moe_w_up, moe_w_down, 0, tb)
    yk_a, gates_a = moe(0, n_a)
    yk_b, gates_b = moe(n_a, n_moe - n_a)
    comb = functools.partial(_combine, fin=fin, shgu=shgu, shd=shd, g3=g3, tm=tm)
    x2_c = comb(yk_a, gates_a, x1=x1_c, mods=mod_ctx, rows_per_mod=b * n_ctx, n_tok=b * n_ctx, x_off=0, yk_off=0,
                fin_off=0)

    n_all = n_ctx + l
    mod_lat = mods[1, :b]
    mod_ctx = jnp.broadcast_to(mods[1, b][None], (b, N_MOD * d))
    g0 = row(norm_g[1, 0])
    h, xt = _pre_s5(x2_c.reshape(b, n_ctx, d), mod_ctx, g0, n_all, 0, None, tt)
    comb_l = functools.partial(comb, x1=x1_l, mods=mod_lat_tm, rows_per_mod=b * l, out_rows=n_all * b,
                               nxt=(mod_lat, g0))
    xt, h = comb_l(yk_a, gates_a, n_tok=lat_a, x_off=0, yk_off=b * n_ctx, fin_off=b * n_ctx,
                   out_off=n_ctx * b, prev=(xt, h))
    xt, h = comb_l(yk_b, gates_b, n_tok=b * l - lat_a, x_off=lat_a, yk_off=0, fin_off=n_a,
                   out_off=n_ctx * b + lat_a, prev=(xt, h))
    toep, win, wout, lamq = _s5c_params_fast(s5_lam_re[0], s5_lam_im[0], s5_log_step[0], s5_b_re[0], s5_b_im[0],
                                        s5_c_re[0], s5_c_im[0])
    ng = d // LANES
    yc = _s5_chunked(h, toep, win, wout, lamq, b, n_ctx)
    g1, g2, g3 = row(norm_g[1, 1]), row(norm_g[1, 2]), row(norm_g[1, 3])
    lat0 = n_ctx * b // tm
    x1, fin, lg = _post_mixer(
        _post_glu_kernel,
        [(h, pl.BlockSpec((ng, tm, LANES), lambda i: (0, i + lat0, 0))),
         (yc, pl.BlockSpec((yc.shape[0], tm // S5_Q, yc.shape[-1]), lambda i: (0, i, 0)))],
        [row(s5_d[0]), s5_w_glu[0].astype(BF16), row(s5_b_glu[0])],
        x=xt, n_tok=l * b, x_off=n_ctx * b, mods=mod_lat[None], g1=g1, g2=g2, wr_t=_router_halves(moe_w_router[1]), tm=tm,
        rows_per_mod=l * b, name="post_s5")
    shgu, shd = shared_weights(1)
    n_h = (l // 2) * b
    moe = functools.partial(_moe, fin, lg, moe_bias[1], moe_w_gate, moe_w_up, moe_w_down, 1, tb)
    yk_a, gates_a = moe(0, n_h)
    yk_b, gates_b = moe(n_h, l * b - n_h)
    comb = functools.partial(_combine, fin=fin, shgu=shgu, shd=shd, x1=x1, mods=mod_lat[None], g3=g3, tm=tm,
                             rows_per_mod=l * b, yk_off=0, batch_out=b)
    out = comb(yk_a, gates_a, n_tok=n_h, x_off=0, fin_off=0)
    return comb(yk_b, gates_b, n_tok=l * b - n_h, x_off=n_h, fin_off=n_h, prev=out)
```

```python
import functools

import jax
import jax.numpy as jnp
from jax import lax
from jax.experimental import pallas as pl
from jax.experimental.pallas import tpu as pltpu
from jax.experimental.pallas import tpu_sc as plsc

F32 = jnp.float32
BF16 = jnp.bfloat16
U32 = jnp.uint32

N_MOD = 6
NORM_EPS = 1e-6
LOG2_E = 1.4426950408889634
GRID_W = 64
MLA_HEADS = 8
Q_LORA = 384
KV_LORA = 256
NOPE_DIM = 128
ROPE_DIM = 64
V_DIM = 128
V_PAD = 256
ROPE_BASE = 10000.0
QK_PAD = 256
S5_GROUP = 16
S5_STATE = 64
S5_GROUPS_PER_BLOCK = 8
S5_PIECES = 4
N_EXPERTS = 64
TOP_K = 8
N_EXPERT_GROUPS = 8
TOPK_GROUPS = 4
D_EXPERT = 256
ROUTED_SCALE = 2.5

VMEM_LIMIT = 56 * 1024 * 1024


def _cparams(sem):
    return pltpu.CompilerParams(dimension_semantics=sem, vmem_limit_bytes=VMEM_LIMIT)


def _rms(x, g):
    return x * lax.rsqrt(jnp.mean(x * x, axis=-1, keepdims=True) + NORM_EPS) * g


def _rows(v, like):
    r = v.shape[0]
    if r == 1:
        return v
    tm, d = like.shape
    return jnp.broadcast_to(v[None], (tm // r, r, d)).reshape(tm, d)


def _mod_chunk(mod_ref, j, d):
    return mod_ref[:, j * d:(j + 1) * d]


def _dot(a, b):
    return jnp.dot(a, b, preferred_element_type=F32)


PACK_ROWS = 4
LANES = 128


def _pack_store(ref, val, lead=(), row0=0):
    n = val.shape[0]
    bits = lax.bitcast_convert_type(val.astype(BF16).astype(F32), U32)
    for s in range(PACK_ROWS):
        lo = bits[:, s * LANES:(s + 1) * LANES] >> 16
        hi = bits[:, (s + PACK_ROWS) * LANES:(s + PACK_ROWS + 1) * LANES] & jnp.uint32(0xFFFF0000)
        ref[lead + (pl.ds(row0 * PACK_ROWS + s, n, stride=PACK_ROWS), slice(None))] = lo | hi


def _unpack_load(ref, n, lead=(), row0=0):
    los, his = [], []
    for s in range(PACK_ROWS):
        w = ref[lead + (pl.ds(row0 * PACK_ROWS + s, n, stride=PACK_ROWS), slice(None))]
        los.append(lax.bitcast_convert_type(w << 16, F32))
        his.append(lax.bitcast_convert_type(w & jnp.uint32(0xFFFF0000), F32))
    return los + his


def _ada_kernel(c_ref, w_ref, b_ref, o_ref):
    c = c_ref[...]
    s = c * jax.nn.sigmoid(c)
    o_ref[...] = jnp.dot(s, w_ref[...], preferred_element_type=F32,
                         precision=lax.Precision.HIGHEST) + b_ref[...]


def _ada_mods(cvec, ada_w, ada_b):
    depth, d, n = ada_w.shape
    rows = cvec.shape[0]
    tn = 1536
    return pl.pallas_call(
        _ada_kernel,
        out_shape=jax.ShapeDtypeStruct((depth, rows, n), F32),
        grid=(depth, n // tn),
        in_specs=[pl.BlockSpec((rows, d), lambda l, j: (0, 0)),
                  pl.BlockSpec((None, d, tn), lambda l, j: (l, 0, j)),
                  pl.BlockSpec((None, 1, tn), lambda l, j: (l, 0, j))],
        out_specs=pl.BlockSpec((None, rows, tn), lambda l, j: (l, 0, j)),
        compiler_params=_cparams(("arbitrary", "arbitrary")),
        name="ada_mods",
    )(cvec, ada_w, ada_b.reshape(depth, 1, n))


def _pre_mla_kernel(x_ref, mod_ref, g0_ref, wd_ref, gq_ref, gkv_ref, wq_ref, wkv_ref, cos_ref, sin_ref,
                    q_ref, k_ref, v_ref):
    d = x_ref.shape[-1]
    x = x_ref[...]
    h = _rms(x, g0_ref[...]) * (1.0 + _mod_chunk(mod_ref, 1, d)) + _mod_chunk(mod_ref, 0, d)
    a = _dot(h.astype(BF16), wd_ref[...])
    cq = _rms(a[:, :Q_LORA], gq_ref[...])
    ckv = _rms(a[:, Q_LORA:Q_LORA + KV_LORA], gkv_ref[...])
    rd = ROPE_DIM
    cos = cos_ref[:, 0:rd]
    sin = sin_ref[:, 0:rd]
    o = Q_LORA + KV_LORA
    k_rot = (a[:, o:o + rd] * cos + a[:, o + rd:o + 2 * rd] * sin).astype(BF16)
    qa = _dot(cq.astype(BF16), wq_ref[...])
    kva = _dot(ckv.astype(BF16), wkv_ref[...])
    hw = MLA_HEADS * 128
    hr = MLA_HEADS * rd
    zpad = jnp.zeros((x.shape[0], QK_PAD - NOPE_DIM - rd), BF16)
    scale = (NOPE_DIM + ROPE_DIM) ** -0.5 * LOG2_E
    for hd in range(MLA_HEADS):
        lo = hd * 128
        q_rot = qa[:, hw + hd * rd:hw + (hd + 1) * rd] * cos + qa[:, hw + hr + hd * rd:hw + hr + (hd + 1) * rd] * sin
        q_ref[:, hd * QK_PAD:hd * QK_PAD + 128] = (qa[:, lo:lo + 128] * scale).astype(BF16)
        q_ref[:, hd * QK_PAD + 128:hd * QK_PAD + 128 + rd] = (q_rot * scale).astype(BF16)
        q_ref[:, hd * QK_PAD + 128 + rd:(hd + 1) * QK_PAD] = zpad
        k_ref[:, hd * QK_PAD:hd * QK_PAD + 128] = kva[:, lo:lo + 128].astype(BF16)
        k_ref[:, hd * QK_PAD + 128:hd * QK_PAD + 128 + rd] = k_rot
        k_ref[:, hd * QK_PAD + 128 + rd:(hd + 1) * QK_PAD] = zpad
        v_ref[:, hd * V_PAD:hd * V_PAD + V_DIM] = kva[:, hw + lo:hw + lo + 128].astype(BF16)
        v_ref[:, hd * V_PAD + V_DIM:(hd + 1) * V_PAD] = jnp.ones((x.shape[0], V_PAD - V_DIM), BF16)


def _pre_mla(x, mods, g0, wd, gq, gkv, wq, wkv, cos_t, sin_t, tm):
    b, n, d = x.shape
    nb_mod = mods.shape[0]
    full = lambda a: pl.BlockSpec(a.shape, lambda i, j: (0,) * a.ndim)
    mod_map = (lambda i, j: (i, 0, 0)) if nb_mod > 1 else (lambda i, j: (0, 0, 0))
    qk_w = MLA_HEADS * QK_PAD
    v_w = MLA_HEADS * V_PAD
    return pl.pallas_call(
        _pre_mla_kernel,
        out_shape=(jax.ShapeDtypeStruct((b, n, qk_w), BF16),
                   jax.ShapeDtypeStruct((b, n, qk_w), BF16),
                   jax.ShapeDtypeStruct((b, n, v_w), BF16)),
        grid=(b, n // tm),
        in_specs=[pl.BlockSpec((None, tm, d), lambda i, j: (i, j, 0)),
                  pl.BlockSpec((None, 1, mods.shape[-1]), mod_map),
                  full(g0), full(wd), full(gq), full(gkv), full(wq), full(wkv),
                  pl.BlockSpec((tm, 128), lambda i, j: (j, 0)),
                  pl.BlockSpec((tm, 128), lambda i, j: (j, 0))],
        out_specs=(pl.BlockSpec((None, tm, qk_w), lambda i, j: (i, j, 0)),
                   pl.BlockSpec((None, tm, qk_w), lambda i, j: (i, j, 0)),
                   pl.BlockSpec((None, tm, v_w), lambda i, j: (i, j, 0))),
        compiler_params=_cparams(("arbitrary", "arbitrary")),
        name="pre_mla",
    )(x, mods, g0, wd, gq, gkv, wq, wkv, cos_t, sin_t)


def _attn_kernel(*refs, n_seg):
    q_ref = refs[0]
    k_refs = refs[1:1 + n_seg]
    v_refs = refs[1 + n_seg:1 + 2 * n_seg]
    o_ref = refs[1 + 2 * n_seg]
    nt = (((1,), (1,)), ((), ()))

    def scores(hd):
        q = q_ref[:, hd * QK_PAD:(hd + 1) * QK_PAD]
        return [lax.dot_general(q, k[:, hd * QK_PAD:(hd + 1) * QK_PAD], nt, preferred_element_type=F32)
                for k in k_refs]

    nxt = scores(0)
    for hd in range(MLA_HEADS):
        ss = nxt
        if hd + 1 < MLA_HEADS:
            nxt = scores(hd + 1)
        m = ss[0].max(axis=-1, keepdims=True)
        for s in ss[1:]:
            m = jnp.maximum(m, s.max(axis=-1, keepdims=True))
        acc = None
        for s, v in zip(ss, v_refs):
            pv = _dot(jnp.exp2((s - m).astype(BF16)), v[:, hd * V_PAD:(hd + 1) * V_PAD])
            acc = pv if acc is None else acc + pv
        o_ref[:, hd * V_DIM:(hd + 1) * V_DIM] = (acc[:, :V_DIM] / acc[:, V_DIM:V_DIM + 1]).astype(BF16)


def _attention(q, ks, vs, tq):
    b, nq, qk_w = q.shape
    v_w = MLA_HEADS * V_DIM
    kv_spec = lambda a: pl.BlockSpec((None,) + a.shape[1:], lambda i, j: (i, 0, 0))
    return pl.pallas_call(
        functools.partial(_attn_kernel, n_seg=len(ks)),
        out_shape=jax.ShapeDtypeStruct((b, nq, v_w), BF16),
        grid=(b, nq // tq),
        in_specs=[pl.BlockSpec((None, tq, qk_w), lambda i, j: (i, j, 0))]
                 + [kv_spec(a) for a in ks] + [kv_spec(a) for a in vs],
        out_specs=pl.BlockSpec((None, tq, v_w), lambda i, j: (i, j, 0)),
        compiler_params=_cparams(("arbitrary", "arbitrary")),
        name="mla_attention",
    )(q, *ks, *vs)


SUB_ROWS = 256


def _sub_tiles(n):
    return [slice(r, r + SUB_ROWS) for r in range(0, n, SUB_ROWS)]


def _post_core(o, x, rows, mod_ref, g1_ref, g2_ref, wr_ref, x1_ref, fin_ref, lg_ref):
    d = x.shape[-1]
    ne = lg_ref.shape[0]
    gate = _rows(_mod_chunk(mod_ref, 2, d), x)
    shift = _rows(_mod_chunk(mod_ref, 3, d), x)
    scale = _rows(_mod_chunk(mod_ref, 4, d), x)
    x1 = x + gate * _rms(o, g1_ref[...])
    fin = _rms(x1, g2_ref[...]) * (1.0 + scale) + shift
    x1_ref[rows, :] = x1
    _pack_store(fin_ref, fin, row0=rows.start)
    nt = (((1,), (1,)), ((), ()))
    f_hi = fin.astype(BF16)
    f_lo = (fin - f_hi.astype(F32)).astype(BF16)
    r_hi = lax.dot_general(wr_ref[...], f_hi, nt, preferred_element_type=F32)
    r_lo = lax.dot_general(wr_ref[0:ne, :], f_lo, nt, preferred_element_type=F32)
    lg_ref[:, rows] = r_hi[:ne] + r_hi[ne:] + r_lo


def _post_proj_kernel(o_ref, wo_ref, x_ref, mod_ref, g1_ref, g2_ref, wr_ref, *rest):
    x1_ref, fin_ref, lg_ref = rest[-3:]
    for rows in _sub_tiles(x_ref.shape[0]):
        o = _dot(o_ref[rows, :], wo_ref[...])
        _post_core(o, x_ref[rows, :], rows, mod_ref, g1_ref, g2_ref, wr_ref, x1_ref, fin_ref, lg_ref)


def _post_proj_tm_kernel(o_ref, wo_ref, x_ref, mod_ref, g1_ref, g2_ref, wr_ref, *rest):
    x1_ref, fin_ref, lg_ref = rest[-3:]
    nb, tt, d = x_ref.shape
    ts = SUB_ROWS // nb
    for t0 in range(0, tt, ts):
        o = _dot(o_ref[:, t0:t0 + ts, :].reshape(nb * ts, o_ref.shape[-1]), wo_ref[...])
        o = jnp.swapaxes(o.reshape(nb, ts, d), 0, 1).reshape(ts * nb, d)
        x = jnp.swapaxes(x_ref[:, t0:t0 + ts, :], 0, 1).reshape(ts * nb, d)
        _post_core(o, x, slice(t0 * nb, (t0 + ts) * nb), mod_ref, g1_ref, g2_ref, wr_ref, x1_ref, fin_ref, lg_ref)


def _chunk_to_rows(yc_ref, c, nb):
    q, grp = S5_Q, S5_GROUP
    per_tile = LANES // grp
    lane_grp = lax.broadcasted_iota(jnp.int32, (nb, LANES), 1) // grp
    n_pairs = yc_ref.shape[0]
    pieces = [[yc_ref[p, c * nb:(c + 1) * nb, lt * LANES:(lt + 1) * LANES] for lt in range(2 * q // per_tile)]
              for p in range(n_pairs)]
    out_rows = []
    for t in range(q):
        tiles = []
        for lb in range(2 * n_pairs // per_tile):
            dest = None
            for g8 in range(per_tile):
                g = lb * per_tile + g8
                piece = pieces[g // 2][(g % 2) * (q // per_tile) + t // per_tile]
                r = pltpu.roll(piece, ((g8 - t % per_tile) * grp) % LANES, 1)
                dest = r if dest is None else jnp.where(lane_grp == g8, r, dest)
            tiles.append(dest)
        out_rows.append(jnp.concatenate(tiles, axis=-1))
    return jnp.concatenate(out_rows, axis=0)


def _post_glu_kernel(h_ref, yc_ref, dsk_ref, wg_ref, bg_ref, x_ref, mod_ref, g1_ref, g2_ref, wr_ref,
                     *rest):
    x1_ref, fin_ref, lg_ref = rest[-3:]
    d = x_ref.shape[-1]
    nb = mod_ref.shape[0]
    assert SUB_ROWS == S5_Q * nb
    for ci, rows in enumerate(_sub_tiles(x_ref.shape[0])):
        h = jnp.concatenate([h_ref[g, rows, :] for g in range(h_ref.shape[0])], axis=-1).astype(F32)
        y = h * dsk_ref[...] + _chunk_to_rows(yc_ref, ci, nb)
        z = _dot(jax.nn.gelu(y, approximate=True).astype(BF16), wg_ref[...]) + bg_ref[...]
        o = z[:, :d] * jax.nn.sigmoid(z[:, d:])
        _post_core(o, x_ref[rows, :], rows, mod_ref, g1_ref, g2_ref, wr_ref, x1_ref, fin_ref, lg_ref)


def _post_mixer(kernel, tok_inputs, consts, x, n_tok, x_off, mods, g1, g2, wr_t, tm, rows_per_mod, name,
                moe_total=None, moe_off=0, prev=None, x_spec=None):
    d = x.shape[-1]
    ne = wr_t.shape[0] // 2
    moe_total = n_tok if moe_total is None else moe_total
    tiles_per_mod = rows_per_mod // tm
    xo, mo = x_off // tm, moe_off // tm
    full = lambda a: pl.BlockSpec(a.shape, lambda i: (0,) * a.ndim)
    tile = pl.BlockSpec((tm, d), lambda i: (i, 0))
    mod_spec = pl.BlockSpec((None,) + mods.shape[1:], lambda i: (i // tiles_per_mod, 0, 0))
    x_spec = pl.BlockSpec((tm, d), lambda i: (i + xo, 0)) if x_spec is None else x_spec
    in_specs = ([spec for _, spec in tok_inputs] + [full(a) for a in consts]
                + [x_spec, mod_spec, full(g1), full(g2), full(wr_t)])
    args = [a for a, _ in tok_inputs] + list(consts) + [x, mods, g1, g2, wr_t]
    aliases = {}
    if prev is not None:
        aliases = {len(args): 1, len(args) + 1: 2}
        in_specs += [pl.BlockSpec(memory_space=pl.ANY)] * 2
        args += list(prev)
    return pl.pallas_call(
        kernel,
        out_shape=(jax.ShapeDtypeStruct((n_tok, d), F32),
                   jax.ShapeDtypeStruct((moe_total * PACK_ROWS, LANES), U32),
                   jax.ShapeDtypeStruct((ne, moe_total), F32)),
        grid=(n_tok // tm,),
        in_specs=in_specs,
        out_specs=(tile, pl.BlockSpec((tm * PACK_ROWS, LANES), lambda i: (i + mo, 0)),
                   pl.BlockSpec((ne, tm), lambda i: (0, i + mo))),
        input_output_aliases=aliases,
        compiler_params=_cparams(("arbitrary",)),
        name=name,
    )(*args)


def _route_kernel(lg_ref, bias_ref, eidx_ref, gate_ref, rank_ref, cnt_ref, tri_ref, base_ref):
    i = pl.program_id(0)
    ne, tt = lg_ref.shape
    gsz = ne // N_EXPERT_GROUPS
    shp = (N_EXPERT_GROUPS, gsz, tt)
    neg = -jnp.inf

    @pl.when(i == 0)
    def _():
        base_ref[...] = jnp.zeros_like(base_ref)
        r = lax.broadcasted_iota(jnp.int32, (tt, tt), 0)
        c = lax.broadcasted_iota(jnp.int32, (tt, tt), 1)
        tri_ref[...] = (r < c).astype(BF16)

    scores = jax.nn.sigmoid(lg_ref[...])
    s3 = scores.reshape(shp)
    b3 = (scores + bias_ref[...]).reshape(shp)
    io_e = lax.broadcasted_iota(jnp.int32, shp, 1)
    io_g = lax.broadcasted_iota(jnp.int32, shp, 0)
    io_flat = io_g * gsz + io_e
    m1 = b3.max(axis=1, keepdims=True)
    i1 = jnp.where(b3 == m1, io_e, gsz).min(axis=1, keepdims=True)
    m2 = jnp.where(io_e == i1, neg, b3).max(axis=1, keepdims=True)
    cur = jnp.broadcast_to(m1 + m2, shp)
    gsel = jnp.zeros(shp, jnp.bool_)
    for _ in range(TOPK_GROUPS):
        m = cur.max(axis=0, keepdims=True)
        gi = jnp.where(cur == m, io_g, N_EXPERT_GROUPS).min(axis=0, keepdims=True)
        hit = io_g == gi
        gsel = jnp.logical_or(gsel, hit)
        cur = jnp.where(hit, neg, cur)
    cand = jnp.where(gsel, b3, neg)
    sel = jnp.zeros(shp, jnp.bool_)
    eids, gts = [], []
    for _ in range(TOP_K):
        m = cand.max(axis=0, keepdims=True).max(axis=1, keepdims=True)
        ei = jnp.where(cand == m, io_flat, ne).min(axis=0, keepdims=True).min(axis=1, keepdims=True)
        hit = io_flat == ei
        gts.append(jnp.where(hit, s3, 0.0).sum(axis=0, keepdims=True).sum(axis=1, keepdims=True))
        eids.append(ei)
        sel = jnp.logical_or(sel, hit)
        cand = jnp.where(hit, neg, cand)
    gsum = gts[0]
    for g in gts[1:]:
        gsum = gsum + g
    self32 = sel.astype(F32).reshape(ne, tt)
    cnt = _dot(self32.astype(BF16), tri_ref[...]) + base_ref[...]
    cnt3 = cnt.reshape(shp)
    for k in range(TOP_K):
        hit = io_flat == eids[k]
        rk = jnp.where(hit, cnt3, 0.0).sum(axis=0, keepdims=True).sum(axis=1, keepdims=True)
        rank_ref[k:k + 1, :] = rk.reshape(1, tt).astype(jnp.int32)
        eidx_ref[k:k + 1, :] = eids[k].reshape(1, tt)
        gate_ref[k:k + 1, :] = (gts[k] / gsum * ROUTED_SCALE).reshape(1, tt)
    base_new = base_ref[...] + self32.sum(axis=1, keepdims=True)
    base_ref[...] = base_new
    cnt_ref[...] = jnp.broadcast_to(base_new, cnt_ref.shape)


def _route(logits_t, bias, tt, tok0, t):
    ne = logits_t.shape[0]
    off = tok0 // tt
    out_i = jax.ShapeDtypeStruct((TOP_K, t), jnp.int32)
    row = pl.BlockSpec((TOP_K, tt), lambda i: (0, i))
    return pl.pallas_call(
        _route_kernel,
        out_shape=(out_i, jax.ShapeDtypeStruct((TOP_K, t), F32), out_i,
                   jax.ShapeDtypeStruct((ne, 128), F32)),
        grid=(t // tt,),
        in_specs=[pl.BlockSpec((ne, tt), lambda i: (0, i + off)),
                  pl.BlockSpec((ne, 1), lambda i: (0, 0))],
        out_specs=(row, row, row, pl.BlockSpec((ne, 128), lambda i: (0, 0))),
        scratch_shapes=[pltpu.VMEM((tt, tt), BF16), pltpu.VMEM((ne, 1), F32)],
        compiler_params=_cparams(("arbitrary",)),
        name="moe_route",
    )(logits_t, bias.reshape(ne, 1))


def _dest_kernel(eidx_ref, rank_ref, start_ref, dest_ref):
    kk, tt = eidx_ref.shape
    ne = start_ref.shape[0]
    n_chunk, _, r = dest_ref.shape
    io_e = lax.broadcasted_iota(jnp.int32, (ne, tt), 0)
    start = start_ref[...]
    for k in range(kk):
        hit = io_e == eidx_ref[k:k + 1, :]
        dk = jnp.where(hit, start, 0).sum(axis=0, keepdims=True) + rank_ref[k:k + 1, :]
        for c in range(n_chunk):
            dest_ref[c, k:k + 1, :] = dk[:, c * r:(c + 1) * r]


def _dest_rows(eidx_t, rank_t, start, tt, r):
    kk, t = eidx_t.shape
    ne = start.shape[0]
    return pl.pallas_call(
        _dest_kernel,
        out_shape=jax.ShapeDtypeStruct((t // r, kk, r), jnp.int32),
        grid=(t // tt,),
        in_specs=[pl.BlockSpec((kk, tt), lambda i: (0, i)),
                  pl.BlockSpec((kk, tt), lambda i: (0, i)),
                  pl.BlockSpec((ne, 1), lambda i: (0, 0))],
        out_specs=pl.BlockSpec((tt // r, kk, r), lambda i: (i, 0, 0)),
        compiler_params=_cparams(("arbitrary",)),
        name="moe_dest",
    )(eidx_t, rank_t, start.reshape(ne, 1))


SC_CHUNK = 64


def _sc_mesh():
    return plsc.VectorSubcoreMesh(core_axis_name="c", subcore_axis_name="s")


def _sc_workers():
    info = plsc.get_sparse_core_info()
    return info.num_cores, info.num_cores * info.num_subcores


def _sc_scatter_rows(rows, dest, n_out, row0=0):
    n_chunk, kk, r = dest.shape
    nc, nw = _sc_workers()
    cpw = n_chunk // nw
    assert cpw * nw == n_chunk and cpw % 2 == 0 and row0 % r == 0 and row0 + n_chunk * r <= rows.shape[0]

    @functools.partial(
        pl.kernel, mesh=_sc_mesh(),
        out_type=jax.ShapeDtypeStruct((n_out,) + rows.shape[1:], rows.dtype),
        scratch_types=[pltpu.VMEM((2, kk, r), jnp.int32), pltpu.VMEM((2, r) + rows.shape[1:], rows.dtype),
                       pltpu.SemaphoreType.DMA((2,)), pltpu.SemaphoreType.DMA((2,))])
    def scatter(rows_hbm, dest_hbm, out_hbm, idx_v, rows_v, load_sem, scat_sem):
        c0 = (lax.axis_index("s") * nc + lax.axis_index("c")) * cpw

        def loads(c, b):
            return (pltpu.make_async_copy(dest_hbm.at[c], idx_v.at[b], load_sem.at[b]),
                    pltpu.make_async_copy(rows_hbm.at[pl.ds(row0 + c * r, r)], rows_v.at[b], load_sem.at[b]))

        def scat(b, k):
            return pltpu.make_async_copy(rows_v.at[b], out_hbm.at[idx_v.at[b, k]], scat_sem.at[b])

        for cp in loads(c0, 0):
            cp.start()

        @pl.loop(0, cpw, step=2)
        def _(ci):
            for b in range(2):
                c = c0 + ci + b
                for cp in loads(c, b):
                    cp.wait()
                for k in range(kk):
                    scat(b, k).start()

                @pl.when(ci + b >= 1)
                def _():
                    for k in range(kk):
                        scat(1 - b, k).wait()

                @pl.when(ci + b + 1 < cpw)
                def _():
                    for cp in loads(c + 1, 1 - b):
                        cp.start()

        for k in range(kk):
            scat((cpw - 1) % 2, k).wait()

    return scatter(rows, dest)


def _sc_gather_rows(src, dest):
    n_chunk, kk, r = dest.shape
    t = n_chunk * r
    nc, nw = _sc_workers()
    cpw = n_chunk // nw
    nbuf = 3
    assert cpw * nw == n_chunk and kk > nbuf

    @functools.partial(
        pl.kernel, mesh=_sc_mesh(),
        out_type=jax.ShapeDtypeStruct((kk, t) + src.shape[1:], src.dtype),
        scratch_types=[pltpu.VMEM((kk, r), jnp.int32), pltpu.VMEM((nbuf, r) + src.shape[1:], src.dtype),
                       pltpu.SemaphoreType.DMA((nbuf,)), pltpu.SemaphoreType.DMA((nbuf,))])
    def gather(src_hbm, dest_hbm, out_hbm, idx_v, rows_v, get_sem, put_sem):
        c0 = (lax.axis_index("s") * nc + lax.axis_index("c")) * cpw

        @pl.loop(0, cpw)
        def _(ci):
            c = c0 + ci
            pltpu.sync_copy(dest_hbm.at[c], idx_v)

            def get(k):
                return pltpu.make_async_copy(src_hbm.at[idx_v.at[k]], rows_v.at[k % nbuf], get_sem.at[k % nbuf])

            def put(k):
                return pltpu.make_async_copy(rows_v.at[k % nbuf], out_hbm.at[k, pl.ds(c * r, r)],
                                             put_sem.at[k % nbuf])

            for k in range(nbuf - 1):
                get(k).start()
            for k in range(kk):
                get(k).wait()
                put(k).start()
                if k + nbuf - 1 < kk:
                    if k >= 1:
                        put(k - 1).wait()
                    get(k + nbuf - 1).start()
            for k in range(kk - nbuf, kk):
                put(k).wait()

    return gather(src, dest)


def _expert_kernel(be_ref, nu_ref, x_ref, wg_ref, wu_ref, wd_ref, o_ref, wgu_s, wd_s):
    i = pl.program_id(0)
    tb = o_ref.shape[0] // PACK_ROWS

    @pl.when(i < nu_ref[0])
    def _():
        @pl.when(jnp.logical_or(i == 0, be_ref[i] != be_ref[jnp.maximum(i - 1, 0)]))
        def _():
            wgu_s[:, :D_EXPERT] = wg_ref[...].astype(BF16)
            wgu_s[:, D_EXPERT:] = wu_ref[...].astype(BF16)
            wd_s[...] = wd_ref[...].astype(BF16)

        x = jnp.concatenate([v.astype(BF16) for v in _unpack_load(x_ref, tb)], axis=-1)
        gu = _dot(x, wgu_s[...])
        g = gu[:, :D_EXPERT]
        h = g * jax.nn.sigmoid(g) * gu[:, D_EXPERT:]
        _pack_store(o_ref, _dot(h.astype(BF16), wd_s[...]))


def _experts(xs, blk_e, n_used, w_gate, w_up, w_down, layer, tb):
    rows = xs.shape[0] // PACK_ROWS
    _, ne, d, de = w_gate.shape
    nb = rows // tb
    row_map = lambda i, be, nu: (jnp.minimum(i, nu[0] - 1), 0)
    w_map = lambda i, be, nu: (layer, be[i], 0, 0)
    grid_spec = pltpu.PrefetchScalarGridSpec(
        num_scalar_prefetch=2,
        grid=(nb,),
        in_specs=[pl.BlockSpec((tb * PACK_ROWS, LANES), row_map),
                  pl.BlockSpec((None, None, d, de), w_map),
                  pl.BlockSpec((None, None, d, de), w_map),
                  pl.BlockSpec((None, None, de, d), w_map)],
        out_specs=pl.BlockSpec((tb * PACK_ROWS, LANES), row_map),
        scratch_shapes=[pltpu.VMEM((d, 2 * de), BF16), pltpu.VMEM((de, d), BF16)],
    )
    return pl.pallas_call(
        _expert_kernel,
        out_shape=jax.ShapeDtypeStruct(xs.shape, U32),
        grid_spec=grid_spec,
        compiler_params=_cparams(("arbitrary",)),
        name="moe_experts",
    )(blk_e, n_used, xs, w_gate, w_up, w_down)


def _combine_kernel(yk_ref, gate_ref, fin_ref, shgu_ref, shd_ref, x1_ref, mod_ref, g3_ref, *rest, fuse_next):
    if fuse_next:
        nmod_ref, ng0_ref = rest[0], rest[1]
        o_ref, h_ref = rest[-2], rest[-1]
    else:
        o_ref = rest[-1]
    tm, d = x1_ref.shape
    for rows in _sub_tiles(tm):
        n, r0 = SUB_ROWS, rows.start
        gates = gate_ref[rows, :]
        blocks = None
        for k in range(TOP_K):
            gk = gates[:, k:k + 1]
            terms = [gk * v for v in _unpack_load(yk_ref, n, lead=(k,), row0=r0)]
            blocks = terms if blocks is None else [a + b for a, b in zip(blocks, terms)]
        fin = jnp.concatenate([v.astype(BF16) for v in _unpack_load(fin_ref, n, row0=r0)], axis=-1)
        gu = _dot(fin, shgu_ref[...])
        g = gu[:, :D_EXPERT]
        hsh = g * jax.nn.sigmoid(g) * gu[:, D_EXPERT:]
        f = jnp.concatenate(blocks, axis=-1) + _dot(hsh.astype(BF16), shd_ref[...])
        x1 = x1_ref[rows, :]
        x2 = x1 + _rows(_mod_chunk(mod_ref, 5, d), x1) * _rms(f, g3_ref[...])
        if len(o_ref.shape) == 2:
            o_ref[rows, :] = x2
        else:
            nb = o_ref.shape[0]
            ts = SUB_ROWS // nb
            o_ref[:, r0 // nb:r0 // nb + ts, :] = jnp.swapaxes(x2.reshape(ts, nb, d), 0, 1)
        if fuse_next:
            hn = (_rms(x2, ng0_ref[...]) * (1.0 + _rows(_mod_chunk(nmod_ref, 1, d), x2))
                  + _rows(_mod_chunk(nmod_ref, 0, d), x2))
            for gi in range(h_ref.shape[0]):
                h_ref[gi, rows, :] = hn[:, gi * LANES:(gi + 1) * LANES].astype(BF16)


def _combine(yk, gates, fin, shgu, shd, x1, mods, g3, tm, rows_per_mod, n_tok, x_off, yk_off, fin_off,
             batch_out=0, prev=None, out_rows=None, out_off=None, nxt=None):
    t, d = x1.shape
    out_rows = t if out_rows is None else out_rows
    out_off = x_off if out_off is None else out_off
    xo, yo, fo, oo = x_off // tm, yk_off // tm, fin_off // tm, out_off // tm
    tiles_per_mod = rows_per_mod // tm
    full = lambda a: pl.BlockSpec(a.shape, lambda i: (0,) * a.ndim)
    if batch_out:
        out_shape = [jax.ShapeDtypeStruct((batch_out, out_rows // batch_out, d), F32)]
        out_specs = [pl.BlockSpec((batch_out, tm // batch_out, d), lambda i: (0, i + oo, 0))]
    else:
        out_shape = [jax.ShapeDtypeStruct((out_rows, d), F32)]
        out_specs = [pl.BlockSpec((tm, d), lambda i: (i + oo, 0))]
    in_specs = [pl.BlockSpec((TOP_K, tm * PACK_ROWS, LANES), lambda i: (0, i + yo, 0)),
                pl.BlockSpec((tm, TOP_K), lambda i: (i + yo, 0)),
                pl.BlockSpec((tm * PACK_ROWS, LANES), lambda i: (i + fo, 0)),
                full(shgu), full(shd),
                pl.BlockSpec((tm, d), lambda i: (i + xo, 0)),
                pl.BlockSpec((None,) + mods.shape[1:], lambda i: ((i + xo) // tiles_per_mod, 0, 0)),
                full(g3)]
    args = [yk, gates, fin, shgu, shd, x1, mods, g3]
    if nxt is not None:
        in_specs += [full(nxt[0]), full(nxt[1])]
        args += list(nxt)
        out_shape.append(jax.ShapeDtypeStruct((d // LANES, out_rows, LANES), BF16))
        out_specs.append(pl.BlockSpec((d // LANES, tm, LANES), lambda i: (0, i + oo, 0)))
    aliases = {}
    if prev is not None:
        for j, p in enumerate(prev if isinstance(prev, (tuple, list)) else [prev]):
            in_specs.append(pl.BlockSpec(memory_space=pl.ANY))
            aliases[len(args)] = j
            args.append(p)
    out = pl.pallas_call(
        functools.partial(_combine_kernel, fuse_next=nxt is not None),
        out_shape=tuple(out_shape),
        grid=(n_tok // tm,),
        in_specs=in_specs,
        out_specs=tuple(out_specs),
        input_output_aliases=aliases,
        compiler_params=_cparams(("arbitrary",)),
        name="moe_combine",
    )(*args)
    return out if nxt is not None else out[0]


def _moe(fin, logits_t, bias, w_gate, w_up, w_down, layer, tb, tok0, t):
    t_all = fin.shape[0] // PACK_ROWS
    ne = w_gate.shape[1]
    tt = 512
    eidx_t, gates_t, rank_t, cnt = _route(logits_t, bias, tt, tok0, t)
    counts = cnt[:, 0].astype(jnp.int32)
    padded = (counts + tb - 1) // tb * tb
    pad_end = jnp.cumsum(padded)
    pad_start = pad_end - padded
    nb = (t * TOP_K) // tb + ne
    n_used = pad_end[-1] // tb
    blk_start = jnp.arange(nb, dtype=jnp.int32) * tb
    blk = jnp.sum(pad_end[None, :] <= jnp.minimum(blk_start, pad_end[-1] - 1)[:, None], axis=1)
    blk_e = jnp.minimum(blk, ne - 1).astype(jnp.int32)
    dest = _dest_rows(eidx_t, rank_t, pad_start, tt, SC_CHUNK)
    xs = _sc_scatter_rows(fin.reshape(t_all, PACK_ROWS, LANES), dest, nb * tb, row0=tok0)
    ys = _experts(xs.reshape(nb * tb * PACK_ROWS, LANES), blk_e, n_used.reshape(1).astype(jnp.int32),
                  w_gate, w_up, w_down, layer, tb)
    yk = _sc_gather_rows(ys.reshape(nb * tb, PACK_ROWS, LANES), dest)
    return yk.reshape(TOP_K, t * PACK_ROWS, LANES), gates_t.T


def _pre_s5_kernel(x_ref, mod_ref, g0_ref, *refs):
    h_ref, xt_ref = refs[-2:]
    nb, tt, d = x_ref.shape
    x = jnp.swapaxes(x_ref[...], 0, 1).reshape(tt * nb, d)
    h = (_rms(x, g0_ref[...]) * (1.0 + _rows(_mod_chunk(mod_ref, 1, d), x))
         + _rows(_mod_chunk(mod_ref, 0, d), x))
    for g in range(h_ref.shape[0]):
        h_ref[g] = h[:, g * LANES:(g + 1) * LANES].astype(BF16)
    xt_ref[...] = x


def _pre_s5(x, mods, g0, n_total, t_off, prev, tt):
    nb, n, d = x.shape
    off = t_off // tt
    out_shape = (jax.ShapeDtypeStruct((d // LANES, n_total * nb, LANES), BF16),
                 jax.ShapeDtypeStruct((n_total * nb, d), F32))
    out_specs = (pl.BlockSpec((d // LANES, tt * nb, LANES), lambda i: (0, i + off, 0)),
                 pl.BlockSpec((tt * nb, d), lambda i: (i + off, 0)))
    in_specs = [pl.BlockSpec((nb, tt, d), lambda i: (0, i, 0)),
                pl.BlockSpec(mods.shape, lambda i: (0, 0)),
                pl.BlockSpec(g0.shape, lambda i: (0, 0))]
    args = (x, mods, g0)
    aliases = {}
    if prev is not None:
        in_specs += [pl.BlockSpec(memory_space=pl.ANY)] * 2
        args += tuple(prev)
        aliases = {3: 0, 4: 1}
    return pl.pallas_call(
        _pre_s5_kernel,
        out_shape=out_shape,
        grid=(n // tt,),
        in_specs=in_specs,
        out_specs=out_specs,
        input_output_aliases=aliases,
        compiler_params=_cparams(("arbitrary",)),
        name="pre_s5",
    )(*args)


def _s5_scan_kernel(h_ref, bm_ref, cm_ref, lam_ref, y_ref, bu0, bu1, xb0, xb1, st_ref):
    first = jnp.logical_and(jnp.logical_and(pl.program_id(0) == 0, pl.program_id(1) == 0), pl.program_id(2) == 0)
    dr = pl.program_id(1)
    s = pl.program_id(2)
    tc, nb, cw = h_ref.shape
    half = st_ref.shape[1] // 2

    @pl.when(first)
    def _():
        for r in (bu0, bu1, xb0, xb1, st_ref):
            r[...] = jnp.zeros_like(r)

    def stages(bu_w, bu_r, xb_w, xb_r):
        lr = jnp.broadcast_to(lam_ref[0:1, :], (nb, half))
        li = jnp.broadcast_to(lam_ref[1:2, :], (nb, half))
        fresh = s == 1
        xr = jnp.where(fresh, 0.0, st_ref[:, 0:half])
        xi = jnp.where(fresh, 0.0, st_ref[:, half:2 * half])
        tp = tc // S5_PIECES
        for p in range(S5_PIECES):
            for i in range(p * tp, (p + 1) * tp):
                t = i + dr * (tc - 1 - 2 * i)
                rows = pl.ds(pl.multiple_of(t * nb, nb), nb)
                nr = lr * xr - li * xi + bu_r[rows, 0:half]
                ni = lr * xi + li * xr + bu_r[rows, half:2 * half]
                xb_w[rows, 0:half] = nr.astype(BF16)
                xb_w[rows, half:2 * half] = ni.astype(BF16)
                xr, xi = nr, ni
            ts = slice(p * tp, (p + 1) * tp)
            mr = slice(p * tp * nb, (p + 1) * tp * nb)
            y_ref[ts] = _dot(xb_r[mr, :], cm_ref[...]).reshape(tp, nb, cw)
            bu_w[mr, :] = _dot(h_ref[ts].reshape(tp * nb, cw).astype(BF16), bm_ref[...])
        st_ref[:, 0:half] = xr
        st_ref[:, half:2 * half] = xi

    @pl.when(s % 2 == 0)
    def _():
        stages(bu0, bu1, xb1, xb0)

    @pl.when(s % 2 == 1)
    def _():
        stages(bu1, bu0, xb0, xb1)


def _s5_scan(h_all, n_ctx, bm, cm, lam, tc):
    ng, nt, nb, cw = h_all.shape
    nl = nt - n_ctx
    assert cw == S5_GROUPS_PER_BLOCK * S5_GROUP
    sw = 2 * S5_GROUPS_PER_BLOCK * S5_STATE
    ncc, n = n_ctx // tc, nt // tc

    def chunk(dr, j):
        j = jnp.clip(j, 0, n - 1)
        rev = jnp.where(j < ncc, ncc - 1 - j, n - 1 - (j - ncc))
        return jnp.where(dr == 0, j, rev)

    def out_map(g, dr, s):
        return (dr, g, chunk(dr, jnp.clip(s - 2, ncc, n - 1)) - ncc, 0, 0)

    return pl.pallas_call(
        _s5_scan_kernel,
        out_shape=jax.ShapeDtypeStruct((2, ng, nl, nb, cw), F32),
        grid=(ng, 2, n + 2),
        in_specs=[pl.BlockSpec((None, tc, nb, cw), lambda g, dr, s: (g, chunk(dr, s), 0, 0)),
                  pl.BlockSpec((None, None, cw, sw), lambda g, dr, s: (dr, g, 0, 0)),
                  pl.BlockSpec((None, None, sw, cw), lambda g, dr, s: (dr, g, 0, 0)),
                  pl.BlockSpec((None, None, 2, sw // 2), lambda g, dr, s: (dr, g, 0, 0))],
        out_specs=pl.BlockSpec((None, None, tc, nb, cw), out_map),
        scratch_shapes=[pltpu.VMEM((tc * nb, sw), F32), pltpu.VMEM((tc * nb, sw), F32),
                        pltpu.VMEM((tc * nb, sw), BF16), pltpu.VMEM((tc * nb, sw), BF16),
                        pltpu.VMEM((nb, sw), F32)],
        compiler_params=_cparams(("arbitrary", "arbitrary", "arbitrary")),
        name="s5_scan",
    )(h_all, bm, cm, lam)


def _s5_params(lam_re, lam_im, log_step, b_re, b_im, c_re, c_im):
    g, p = lam_re.shape[1:]
    gb = S5_GROUPS_PER_BLOCK
    nblk = g // gb
    step = jnp.exp(log_step)[..., None]
    mag = jnp.exp(lam_re * step)
    lb_re = mag * jnp.cos(lam_im * step)
    lb_im = mag * jnp.sin(lam_im * step)
    den = lam_re * lam_re + lam_im * lam_im
    f_re = ((lb_re - 1.0) * lam_re + lb_im * lam_im) / den
    f_im = (lb_im * lam_re - (lb_re - 1.0) * lam_im) / den
    bb_re = f_re[..., None] * b_re - f_im[..., None] * b_im
    bb_im = f_re[..., None] * b_im + f_im[..., None] * b_re
    eye = jnp.eye(gb, dtype=F32)

    def in_map(w):
        w = w.reshape(2, nblk, gb, p, S5_GROUP)
        return jnp.einsum("dnapi,ab->dnaibp", w, eye).reshape(2, nblk, gb * S5_GROUP, gb * p)

    def out_map(w):
        w = w.reshape(2, nblk, gb, S5_GROUP, p)
        return jnp.einsum("dnaip,ab->dnapbi", w, eye).reshape(2, nblk, gb * p, gb * S5_GROUP)

    bm = jnp.concatenate([in_map(bb_re), in_map(bb_im)], axis=-1).astype(BF16)
    cm = jnp.concatenate([out_map(c_re), out_map(-c_im)], axis=-2).astype(BF16)
    lam = jnp.stack([lb_re.reshape(2, nblk, gb * p), lb_im.reshape(2, nblk, gb * p)], axis=2)
    return bm, cm, lam


S5_Q = 16
S5_STEP_GROUPS = 4


def _s5c_kernel(h_ref, toep_ref, win_ref, wout_ref, lamq_ref, y_ref, u_ref, sre_ref, sim_ref, xin_ref, *,
                nb, n_ctx_chunks):
    npair, n_rows, kw = u_ref.shape
    half = kw // 2
    sw = 2 * S5_STATE
    n = n_rows // nb
    ncc = n_ctx_chunks
    per_tile = LANES // S5_GROUP
    lane_grp = lax.broadcasted_iota(jnp.int32, (nb, LANES), 1) // S5_GROUP
    steps_per_block = per_tile // S5_STEP_GROUPS

    def relayout_from(gl0):
        def relayout(c, carry):
            r0 = c * (S5_Q * nb)
            pieces = [h_ref[pl.ds(pl.multiple_of(r0 + s * nb, nb), nb), :].astype(F32) for s in range(S5_Q)]
            rows = pl.ds(pl.multiple_of(c * nb, nb), nb)
            for gq in range(S5_STEP_GROUPS):
                for j in range(S5_Q // per_tile):
                    dest = None
                    for s8 in range(per_tile):
                        r = pltpu.roll(pieces[j * per_tile + s8], ((s8 - gl0 - gq) * S5_GROUP) % LANES, 1)
                        dest = r if dest is None else jnp.where(lane_grp == s8, r, dest)
                    lo = (gq % 2) * half + j * LANES
                    u_ref[gq // 2, rows, lo:lo + LANES] = dest.astype(BF16)
            return carry

        lax.fori_loop(0, n, relayout, 0)

    for k in range(steps_per_block):
        pl.when(pl.program_id(0) % steps_per_block == k)(functools.partial(relayout_from, k * S5_STEP_GROUPS))
    for dr in range(2):
        for p in range(npair):
            s = _dot(u_ref[p], win_ref[dr, p])
            sre_ref[dr, :, p * sw:(p + 1) * sw] = s[:, :sw]
            sim_ref[dr, :, p * sw:(p + 1) * sw] = s[:, sw:]
    w = npair * sw
    lr = [jnp.broadcast_to(lamq_ref[dr, 0:1, :], (nb, w)) for dr in range(2)]
    li = [jnp.broadcast_to(lamq_ref[dr, 1:2, :], (nb, w)) for dr in range(2)]

    def step(j, carry):
        rev = jnp.where(j < ncc, ncc - 1 - j, n - 1 - (j - ncc))
        out = []
        for dr in range(2):
            xr, xi = carry[dr]
            c = j if dr == 0 else rev
            rows = pl.ds(pl.multiple_of(c * nb, nb), nb)
            for p in range(npair):
                xin_ref[dr, rows, 2 * p * sw:(2 * p + 1) * sw] = xr[:, p * sw:(p + 1) * sw].astype(BF16)
                xin_ref[dr, rows, (2 * p + 1) * sw:(2 * p + 2) * sw] = xi[:, p * sw:(p + 1) * sw].astype(BF16)
            nr = lr[dr] * xr - li[dr] * xi + sre_ref[dr, rows, :]
            ni = lr[dr] * xi + li[dr] * xr + sim_ref[dr, rows, :]
            out.append((nr, ni))
        return tuple(out)

    zero = jnp.zeros((nb, w), F32)
    lax.fori_loop(0, n, step, ((zero, zero), (zero, zero)), unroll=2)
    lat = slice(ncc * nb, n_rows)
    for p in range(npair):
        acc = None
        for dr in range(2):
            intra = jnp.concatenate([_dot(u_ref[p, lat, 0:half], toep_ref[dr, 2 * p]),
                                     _dot(u_ref[p, lat, half:kw], toep_ref[dr, 2 * p + 1])], axis=-1)
            term = intra + _dot(xin_ref[dr, lat, 2 * p * sw:(2 * p + 2) * sw], wout_ref[dr, p])
            acc = term if acc is None else acc + term
        y_ref[p] = acc


def _s5_chunked(h, toep, win, wout, lamq, nb, n_ctx):
    ng, n_tok, _ = h.shape
    n_groups = toep.shape[1]
    pp = S5_STEP_GROUPS // 2
    steps_per_block = (LANES // S5_GROUP) // S5_STEP_GROUPS
    n_rows = n_tok // S5_Q
    ncc = n_ctx // S5_Q
    lat_rows = n_rows - ncc * nb
    kw = 2 * S5_Q * S5_GROUP
    sw = 2 * S5_STATE
    return pl.pallas_call(
        functools.partial(_s5c_kernel, nb=nb, n_ctx_chunks=ncc),
        out_shape=jax.ShapeDtypeStruct((n_groups // 2, lat_rows, kw), F32),
        grid=(n_groups // S5_STEP_GROUPS,),
        in_specs=[pl.BlockSpec((None, n_tok, LANES), lambda i: (i // steps_per_block, 0, 0),
                               pipeline_mode=pl.Buffered(1)),
                  pl.BlockSpec((2, 2 * pp, kw // 2, kw // 2), lambda i: (0, i, 0, 0)),
                  pl.BlockSpec((2, pp, kw, 2 * sw), lambda i: (0, i, 0, 0)),
                  pl.BlockSpec((2, pp, 2 * sw, kw), lambda i: (0, i, 0, 0)),
                  pl.BlockSpec((2, None, 2, pp * sw), lambda i: (0, i, 0, 0))],
        out_specs=pl.BlockSpec((pp, lat_rows, kw), lambda i: (i, 0, 0)),
        scratch_shapes=[pltpu.VMEM((pp, n_rows, kw), BF16),
                        pltpu.VMEM((2, n_rows, pp * sw), F32), pltpu.VMEM((2, n_rows, pp * sw), F32),
                        pltpu.VMEM((2, n_rows, 2 * pp * sw), BF16)],
        compiler_params=_cparams(("arbitrary",)),
        name="s5_chunked",
    )(h, toep, win, wout, lamq)


def _s5c_params(lam_re, lam_im, log_step, b_re, b_im, c_re, c_im):
    hp = lax.Precision.HIGHEST
    q = S5_Q
    _, g, p = lam_re.shape
    ni = b_re.shape[-1]
    step = jnp.exp(log_step)[..., None]
    ar, ai = lam_re * step, lam_im * step
    tau = jnp.arange(q + 1, dtype=F32)[:, None, None, None]
    mag = jnp.exp(tau * ar)
    pr, pi = mag * jnp.cos(tau * ai), mag * jnp.sin(tau * ai)
    den = lam_re * lam_re + lam_im * lam_im
    f_re = ((pr[1] - 1.0) * lam_re + pi[1] * lam_im) / den
    f_im = (pi[1] * lam_re - (pr[1] - 1.0) * lam_im) / den
    bb_re = f_re[..., None] * b_re - f_im[..., None] * b_im
    bb_im = f_re[..., None] * b_im + f_im[..., None] * b_re
    cp_re = c_re[None] * pr[:, :, :, None, :] - c_im[None] * pi[:, :, :, None, :]
    cp_im = c_re[None] * pi[:, :, :, None, :] + c_im[None] * pr[:, :, :, None, :]
    taps = (jnp.einsum("tdgop,dgpi->tdgoi", cp_re[:q], bb_re, precision=hp)
            - jnp.einsum("tdgop,dgpi->tdgoi", cp_im[:q], bb_im, precision=hp))
    s_idx = jnp.arange(q)[:, None]
    t_idx = jnp.arange(q)[None, :]

    def toeplitz(dr):
        lag = (t_idx - s_idx) if dr == 0 else (s_idx - t_idx)
        k = taps[:, dr][jnp.clip(lag, 0, q - 1)]
        k = jnp.where((lag >= 0)[:, :, None, None, None], k, 0.0)
        return k.transpose(2, 0, 4, 1, 3).reshape(g, q * ni, q * ni)

    toep = jnp.stack([toeplitz(0), toeplitz(1)]).astype(BF16)

    def state_in(dr):
        e = (q - 1 - jnp.arange(q)) if dr == 0 else jnp.arange(q)
        er, ei = pr[e, dr], pi[e, dr]
        br, bi = bb_re[dr].transpose(0, 2, 1), bb_im[dr].transpose(0, 2, 1)
        w_re = er[:, :, None, :] * br[None] - ei[:, :, None, :] * bi[None]
        w_im = er[:, :, None, :] * bi[None] + ei[:, :, None, :] * br[None]
        fl = lambda a: a.transpose(1, 0, 2, 3).reshape(g, q * ni, p)
        return fl(w_re), fl(w_im)

    def state_out(dr):
        f = (jnp.arange(q) + 1) if dr == 0 else (q - jnp.arange(q))
        fl = lambda a: a.transpose(1, 3, 0, 2).reshape(g, p, q * ni)
        return fl(cp_re[f, dr]), fl(-cp_im[f, dr])

    z = lambda *shape: jnp.zeros(shape, F32)

    def pair_in(dr):
        w_re, w_im = state_in(dr)
        a_re, b_re_, a_im, b_im_ = w_re[0::2], w_re[1::2], w_im[0::2], w_im[1::2]
        zz = z(g // 2, q * ni, p)
        top = jnp.concatenate([a_re, zz, a_im, zz], axis=-1)
        bot = jnp.concatenate([zz, b_re_, zz, b_im_], axis=-1)
        return jnp.concatenate([top, bot], axis=1)

    def pair_out(dr):
        w_re, w_im = state_out(dr)
        zz = z(g // 2, p, q * ni)
        rows = [jnp.concatenate([w_re[0::2], zz], axis=-1), jnp.concatenate([zz, w_re[1::2]], axis=-1),
                jnp.concatenate([w_im[0::2], zz], axis=-1), jnp.concatenate([zz, w_im[1::2]], axis=-1)]
        return jnp.concatenate(rows, axis=1)

    win = jnp.stack([pair_in(0), pair_in(1)]).astype(BF16)
    wout = jnp.stack([pair_out(0), pair_out(1)]).astype(BF16)
    ng = S5_STEP_GROUPS
    lamq = jnp.stack([pr[q].reshape(2, g // ng, ng * p), pi[q].reshape(2, g // ng, ng * p)], axis=2)
    return toep, win, wout, lamq


def _s5c_params_fast(lam_re, lam_im, log_step, b_re, b_im, c_re, c_im):
    q = S5_Q
    _, g, p = lam_re.shape
    ni = b_re.shape[-1]
    step = jnp.exp(log_step)[..., None]
    ar, ai = lam_re * step, lam_im * step
    tau = jnp.arange(q + 1, dtype=F32)[None, None, :, None]
    mag = jnp.exp(tau * ar[:, :, None, :])
    pr = mag * jnp.cos(tau * ai[:, :, None, :])
    pi = mag * jnp.sin(tau * ai[:, :, None, :])
    den = lam_re * lam_re + lam_im * lam_im
    f_re = ((pr[:, :, 1] - 1.0) * lam_re + pi[:, :, 1] * lam_im) / den
    f_im = (pi[:, :, 1] * lam_re - (pr[:, :, 1] - 1.0) * lam_im) / den
    bb_re = f_re[..., None] * b_re - f_im[..., None] * b_im
    bb_im = f_re[..., None] * b_im + f_im[..., None] * b_re
    bt_re, bt_im = bb_re.transpose(0, 1, 3, 2), bb_im.transpose(0, 1, 3, 2)
    m_re = pr[:, :, None, :q, :] * bt_re[:, :, :, None, :] - pi[:, :, None, :q, :] * bt_im[:, :, :, None, :]
    m_im = pr[:, :, None, :q, :] * bt_im[:, :, :, None, :] + pi[:, :, None, :q, :] * bt_re[:, :, :, None, :]
    taps = (jnp.einsum("dgitp,dgop->dgito", m_re, c_re, precision=lax.Precision.HIGHEST)
            - jnp.einsum("dgitp,dgop->dgito", m_im, c_im, precision=lax.Precision.HIGHEST))
    zq = jnp.zeros_like(taps[0])

    def toeplitz(dr):
        k = taps[dr]
        if dr == 0:
            ext = jnp.concatenate([zq, k], axis=2)
            rows = [ext[:, :, q - s:2 * q - s, :] for s in range(q)]
        else:
            ext = jnp.concatenate([k[:, :, ::-1, :], zq], axis=2)
            rows = [ext[:, :, q - 1 - s:2 * q - 1 - s, :] for s in range(q)]
        return jnp.stack(rows, axis=1).reshape(g, q * ni, q * ni)

    toep = jnp.stack([toeplitz(0), toeplitz(1)]).astype(BF16)

    def state_in(dr):
        if dr == 0:
            er, ei = m_re[dr][:, :, ::-1, :], m_im[dr][:, :, ::-1, :]
        else:
            er, ei = m_re[dr], m_im[dr]
        fl = lambda a: a.transpose(0, 2, 1, 3).reshape(g, q * ni, p)
        return fl(er), fl(ei)

    def state_out(dr):
        if dr == 0:
            fr, fi = pr[dr][:, 1:q + 1], pi[dr][:, 1:q + 1]
        else:
            fr, fi = pr[dr][:, q:0:-1], pi[dr][:, q:0:-1]
        ct_re, ct_im = c_re[dr].transpose(0, 2, 1), c_im[dr].transpose(0, 2, 1)
        frp, fip = fr.transpose(0, 2, 1)[..., None], fi.transpose(0, 2, 1)[..., None]
        w_re = ct_re[:, :, None, :] * frp - ct_im[:, :, None, :] * fip
        w_im = -(ct_re[:, :, None, :] * fip + ct_im[:, :, None, :] * frp)
        return w_re.reshape(g, p, q * ni), w_im.reshape(g, p, q * ni)

    def pair_in(dr):
        w_re, w_im = state_in(dr)
        zz = jnp.zeros((g // 2, q * ni, p), F32)
        top = jnp.concatenate([w_re[0::2], zz, w_im[0::2], zz], axis=-1)
        bot = jnp.concatenate([zz, w_re[1::2], zz, w_im[1::2]], axis=-1)
        return jnp.concatenate([top, bot], axis=1)

    def pair_out(dr):
        w_re, w_im = state_out(dr)
        zz = jnp.zeros((g // 2, p, q * ni), F32)
        rows = [jnp.concatenate([w_re[0::2], zz], axis=-1), jnp.concatenate([zz, w_re[1::2]], axis=-1),
                jnp.concatenate([w_im[0::2], zz], axis=-1), jnp.concatenate([zz, w_im[1::2]], axis=-1)]
        return jnp.concatenate(rows, axis=1)

    win = jnp.stack([pair_in(0), pair_in(1)]).astype(BF16)
    wout = jnp.stack([pair_out(0), pair_out(1)]).astype(BF16)
    ng = S5_STEP_GROUPS
    lamq = jnp.stack([pr[:, :, q].reshape(2, g // ng, ng * p), pi[:, :, q].reshape(2, g // ng, ng * p)], axis=2)
    return toep, win, wout, lamq


def _rope_tables(n_tokens):
    rows = n_tokens // GRID_W
    row = jnp.repeat(jnp.arange(rows), GRID_W).astype(F32)
    col = jnp.tile(jnp.arange(GRID_W), rows).astype(F32)
    n_freq = ROPE_DIM // 4
    inv_freq = ROPE_BASE ** (-jnp.arange(n_freq, dtype=F32) / n_freq)
    ang = jnp.concatenate([row[:, None] * inv_freq, col[:, None] * inv_freq], axis=-1)
    cos, sin = jnp.cos(ang), jnp.sin(ang)
    z = jnp.zeros((n_tokens, 128 - ROPE_DIM), F32)
    return (jnp.concatenate([cos, cos, z], axis=-1), jnp.concatenate([-sin, sin, z], axis=-1))


def _router_halves(w_router):
    wt = w_router.T
    hi = wt.astype(BF16)
    lo = (wt - hi.astype(F32)).astype(BF16)
    return jnp.concatenate([hi, lo], axis=0)


def _split_pairs(w):
    ev, od = w[..., 0::2], w[..., 1::2]
    return jnp.concatenate([ev, od], axis=-1), jnp.concatenate([od, ev], axis=-1)


def _mla_weights(w_dqkv, w_uq, w_ukv):
    kp, kps = _split_pairs(w_dqkv[:, Q_LORA + KV_LORA:])
    wd = jnp.concatenate([w_dqkv[:, :Q_LORA + KV_LORA], kp, kps], axis=-1).astype(BF16)
    wq3 = w_uq.reshape(Q_LORA, MLA_HEADS, NOPE_DIM + ROPE_DIM)
    qp, qps = _split_pairs(wq3[:, :, NOPE_DIM:])
    wq = jnp.concatenate([wq3[:, :, :NOPE_DIM].reshape(Q_LORA, -1), qp.reshape(Q_LORA, -1),
                          qps.reshape(Q_LORA, -1)], axis=-1).astype(BF16)
    wkv3 = w_ukv.reshape(KV_LORA, MLA_HEADS, NOPE_DIM + V_DIM)
    wkv = jnp.concatenate([wkv3[:, :, :NOPE_DIM].reshape(KV_LORA, -1),
                           wkv3[:, :, NOPE_DIM:].reshape(KV_LORA, -1)], axis=-1).astype(BF16)
    return wd, wq, wkv


@jax.jit
def kernel(x, c, ctx, c_ctx, ada_w, ada_b, norm_g, mla_w_dqkv, mla_g_q, mla_g_kv, mla_w_uq, mla_w_ukv, mla_w_o, s5_lam_re, s5_lam_im, s5_log_step, s5_b_re, s5_b_im, s5_c_re, s5_c_im, s5_d, s5_w_glu, s5_b_glu, moe_w_router, moe_bias, moe_w_gate, moe_w_up, moe_w_down, sh_w_gate, sh_w_up, sh_w_down):
    b, l, d = x.shape
    n_ctx = ctx.shape[1]
    assert ada_w.shape[0] == 2 and b % 8 == 0
    ta = 256
    tm = 512
    tb = 512
    row = lambda v: v.reshape(1, -1)

    n_rows = (b + 1 + 7) // 8 * 8
    cvec = jnp.zeros((n_rows, d), F32).at[:b].set(c).at[b].set(c_ctx)
    mods = _ada_mods(cvec, ada_w, ada_b)

    def shared_weights(i):
        shgu = jnp.concatenate([sh_w_gate[i], sh_w_up[i]], axis=-1).astype(BF16)
        return shgu, sh_w_down[i].astype(BF16)

    mod_lat = mods[0, :b].reshape(b, 1, N_MOD * d)
    mod_ctx = mods[0, b].reshape(1, 1, N_MOD * d)
    wd, wq, wkv = _mla_weights(mla_w_dqkv[0], mla_w_uq[0], mla_w_ukv[0])
    cos_l, sin_l = _rope_tables(l)
    cos_c = jnp.concatenate([jnp.ones((n_ctx, ROPE_DIM), F32), jnp.zeros((n_ctx, 128 - ROPE_DIM), F32)], -1)
    sin_c = jnp.zeros((n_ctx, 128), F32)
    pre = functools.partial(_pre_mla, g0=row(norm_g[0, 0]), wd=wd, gq=row(mla_g_q[0]), gkv=row(mla_g_kv[0]),
                            wq=wq, wkv=wkv, tm=ta)
    q_c, k_c, v_c = pre(ctx, mod_ctx, cos_t=cos_c, sin_t=sin_c)
    q_l, k_l, v_l = pre(x, mod_lat, cos_t=cos_l, sin_t=sin_l)
    o_l = _attention(q_l, [k_c, k_l], [v_c, v_l], 2 * ta)
    o_c = _attention(q_c, [k_c], [v_c], n_ctx)

    wo = mla_w_o[0].astype(BF16)
    wr_t = _router_halves(moe_w_router[0])
    g1, g2, g3 = row(norm_g[0, 1]), row(norm_g[0, 2]), row(norm_g[0, 3])
    post = functools.partial(_post_mixer, _post_proj_kernel, consts=[wo], g1=g1, g2=g2, wr_t=wr_t, tm=tm,
                             name="post_mla")
    o_spec = pl.BlockSpec((tm, o_l.shape[-1]), lambda i: (i, 0))
    n_moe = b * (n_ctx + l)
    x1_c, fin, lg = post([(o_c.reshape(b * n_ctx, -1), o_spec)], x=ctx.reshape(b * n_ctx, d),
                         n_tok=b * n_ctx, x_off=0, mods=mod_ctx, rows_per_mod=b * n_ctx, moe_total=n_moe)
    tt = tm // b
    mod_lat_tm = mods[0, :b][None]
    x1_l, fin, lg = _post_mixer(
        _post_proj_tm_kernel, [(o_l, pl.BlockSpec((b, tt, o_l.shape[-1]), lambda i: (0, i, 0)))], [wo],
        x=x, n_tok=b * l, x_off=0, mods=mod_lat_tm, g1=g1, g2=g2, wr_t=wr_t, tm=tm, rows_per_mod=b * l,
        name="post_mla", moe_total=n_moe, moe_off=b * n_ctx, prev=(fin, lg),
        x_spec=pl.BlockSpec((b, tt, d), lambda i: (0, i, 0)))
    shgu, shd = shared_weights(0)
    lat_a = (l // 2) * b
    n_a = b * n_ctx + lat_a
    moe = functools.partial(_moe, fin, lg, moe_bias[0], moe_w_gate, moe_w_up, moe_w_down, 0, tb)
    yk_a, gates_a = moe(0, n_a)
    yk_b, gates_b = moe(n_a, n_moe - n_a)
    comb = functools.partial(_combine, fin=fin, shgu=shgu, shd=shd, g3=g3, tm=tm)
    x2_c = comb(yk_a, gates_a, x1=x1_c, mods=mod_ctx, rows_per_mod=b * n_ctx, n_tok=b * n_ctx, x_off=0, yk_off=0,
                fin_off=0)

    n_all = n_ctx + l
    mod_lat = mods[1, :b]
    mod_ctx = jnp.broadcast_to(mods[1, b][None], (b, N_MOD * d))
    g0 = row(norm_g[1, 0])
    h, xt = _pre_s5(x2_c.reshape(b, n_ctx, d), mod_ctx, g0, n_all, 0, None, tt)
    comb_l = functools.partial(comb, x1=x1_l, mods=mod_lat_tm, rows_per_mod=b * l, out_rows=n_all * b,
                               nxt=(mod_lat, g0))
    xt, h = comb_l(yk_a, gates_a, n_tok=lat_a, x_off=0, yk_off=b * n_ctx, fin_off=b * n_ctx,
                   out_off=n_ctx * b, prev=(xt, h))
    xt, h = comb_l(yk_b, gates_b, n_tok=b * l - lat_a, x_off=lat_a, yk_off=0, fin_off=n_a,
                   out_off=n_ctx * b + lat_a, prev=(xt, h))
    toep, win, wout, lamq = _s5c_params(s5_lam_re[0], s5_lam_im[0], s5_log_step[0], s5_b_re[0], s5_b_im[0],
                                        s5_c_re[0], s5_c_im[0])
    ng = d // LANES
    yc = _s5_chunked(h, toep, win, wout, lamq, b, n_ctx)
    g1, g2, g3 = row(norm_g[1, 1]), row(norm_g[1, 2]), row(norm_g[1, 3])
    lat0 = n_ctx * b // tm
    x1, fin, lg = _post_mixer(
        _post_glu_kernel,
        [(h, pl.BlockSpec((ng, tm, LANES), lambda i: (0, i + lat0, 0))),
         (yc, pl.BlockSpec((yc.shape[0], tm // S5_Q, yc.shape[-1]), lambda i: (0, i, 0)))],
        [row(s5_d[0]), s5_w_glu[0].astype(BF16), row(s5_b_glu[0])],
        x=xt, n_tok=l * b, x_off=n_ctx * b, mods=mod_lat[None], g1=g1, g2=g2, wr_t=_router_halves(moe_w_router[1]), tm=tm,
        rows_per_mod=l * b, name="post_s5")
    shgu, shd = shared_weights(1)
    n_h = (l // 2) * b
    moe = functools.partial(_moe, fin, lg, moe_bias[1], moe_w_gate, moe_w_up, moe_w_down, 1, tb)
    yk_a, gates_a = moe(0, n_h)
    yk_b, gates_b = moe(n_h, l * b - n_h)
    comb = functools.partial(_combine, fin=fin, shgu=shgu, shd=shd, x1=x1, mods=mod_lat[None], g3=g3, tm=tm,
                             rows_per_mod=l * b, yk_off=0, batch_out=b)
    out = comb(yk_a, gates_a, n_tok=n_h, x_off=0, fin_off=0)
    return comb(yk_b, gates_b, n_tok=l * b - n_h, x_off=n_h, fin_off=n_h, prev=out)
```

```python
import functools

import jax
import jax.numpy as jnp
from jax import lax
from jax.experimental import pallas as pl
from jax.experimental.pallas import tpu as pltpu
from jax.experimental.pallas import tpu_sc as plsc

F32 = jnp.float32
BF16 = jnp.bfloat16
U32 = jnp.uint32

N_MOD = 6
NORM_EPS = 1e-6
LOG2_E = 1.4426950408889634
GRID_W = 64
MLA_HEADS = 8
Q_LORA = 384
KV_LORA = 256
NOPE_DIM = 128
ROPE_DIM = 64
V_DIM = 128
V_PAD = 256
ROPE_BASE = 10000.0
QK_PAD = 256
S5_GROUP = 16
S5_STATE = 64
N_EXPERTS = 64
TOP_K = 8
N_EXPERT_GROUPS = 8
TOPK_GROUPS = 4
D_EXPERT = 256
ROUTED_SCALE = 2.5

VMEM_LIMIT = 56 * 1024 * 1024


def _cparams(sem):
    return pltpu.CompilerParams(dimension_semantics=sem, vmem_limit_bytes=VMEM_LIMIT)


def _rms(x, g):
    return x * lax.rsqrt(jnp.mean(x * x, axis=-1, keepdims=True) + NORM_EPS) * g


def _rows(v, like):
    r = v.shape[0]
    if r == 1:
        return v
    tm, d = like.shape
    return jnp.broadcast_to(v[None], (tm // r, r, d)).reshape(tm, d)


def _mod_chunk(mod_ref, j, d):
    return mod_ref[:, j * d:(j + 1) * d]


def _dot(a, b):
    return jnp.dot(a, b, preferred_element_type=F32)


PACK_ROWS = 4
LANES = 128


def _pack_store(ref, val, lead=(), row0=0):
    n = val.shape[0]
    bits = lax.bitcast_convert_type(val.astype(BF16).astype(F32), U32)
    for s in range(PACK_ROWS):
        lo = bits[:, s * LANES:(s + 1) * LANES] >> 16
        hi = bits[:, (s + PACK_ROWS) * LANES:(s + PACK_ROWS + 1) * LANES] & jnp.uint32(0xFFFF0000)
        ref[lead + (pl.ds(row0 * PACK_ROWS + s, n, stride=PACK_ROWS), slice(None))] = lo | hi


def _unpack_load(ref, n, lead=(), row0=0):
    los, his = [], []
    for s in range(PACK_ROWS):
        w = ref[lead + (pl.ds(row0 * PACK_ROWS + s, n, stride=PACK_ROWS), slice(None))]
        los.append(lax.bitcast_convert_type(w << 16, F32))
        his.append(lax.bitcast_convert_type(w & jnp.uint32(0xFFFF0000), F32))
    return los + his


def _ada_kernel(c_ref, w_ref, b_ref, o_ref):
    c = c_ref[...]
    s = c * jax.nn.sigmoid(c)
    o_ref[...] = jnp.dot(s, w_ref[...], preferred_element_type=F32,
                         precision=lax.Precision.HIGHEST) + b_ref[...]


def _ada_mods(cvec, ada_w, ada_b):
    depth, d, n = ada_w.shape
    rows = cvec.shape[0]
    tn = 1536
    return pl.pallas_call(
        _ada_kernel,
        out_shape=jax.ShapeDtypeStruct((depth, rows, n), F32),
        grid=(depth, n // tn),
        in_specs=[pl.BlockSpec((rows, d), lambda l, j: (0, 0)),
                  pl.BlockSpec((None, d, tn), lambda l, j: (l, 0, j)),
                  pl.BlockSpec((None, 1, tn), lambda l, j: (l, 0, j))],
        out_specs=pl.BlockSpec((None, rows, tn), lambda l, j: (l, 0, j)),
        compiler_params=_cparams(("arbitrary", "arbitrary")),
        name="ada_mods",
    )(cvec, ada_w, ada_b.reshape(depth, 1, n))


def _pre_mla_kernel(x_ref, mod_ref, g0_ref, wd_ref, gq_ref, gkv_ref, wq_ref, wkv_ref, cos_ref, sin_ref,
                    q_ref, k_ref, v_ref):
    d = x_ref.shape[-1]
    x = x_ref[...]
    h = _rms(x, g0_ref[...]) * (1.0 + _mod_chunk(mod_ref, 1, d)) + _mod_chunk(mod_ref, 0, d)
    a = _dot(h.astype(BF16), wd_ref[...])
    cq = _rms(a[:, :Q_LORA], gq_ref[...])
    ckv = _rms(a[:, Q_LORA:Q_LORA + KV_LORA], gkv_ref[...])
    rd = ROPE_DIM
    cos = cos_ref[:, 0:rd]
    sin = sin_ref[:, 0:rd]
    o = Q_LORA + KV_LORA
    k_rot = (a[:, o:o + rd] * cos + a[:, o + rd:o + 2 * rd] * sin).astype(BF16)
    qa = _dot(cq.astype(BF16), wq_ref[...])
    kva = _dot(ckv.astype(BF16), wkv_ref[...])
    hw = MLA_HEADS * 128
    hr = MLA_HEADS * rd
    zpad = jnp.zeros((x.shape[0], QK_PAD - NOPE_DIM - rd), BF16)
    scale = (NOPE_DIM + ROPE_DIM) ** -0.5 * LOG2_E
    for hd in range(MLA_HEADS):
        lo = hd * 128
        q_rot = qa[:, hw + hd * rd:hw + (hd + 1) * rd] * cos + qa[:, hw + hr + hd * rd:hw + hr + (hd + 1) * rd] * sin
        q_ref[:, hd * QK_PAD:hd * QK_PAD + 128] = (qa[:, lo:lo + 128] * scale).astype(BF16)
        q_ref[:, hd * QK_PAD + 128:hd * QK_PAD + 128 + rd] = (q_rot * scale).astype(BF16)
        q_ref[:, hd * QK_PAD + 128 + rd:(hd + 1) * QK_PAD] = zpad
        k_ref[:, hd * QK_PAD:hd * QK_PAD + 128] = kva[:, lo:lo + 128].astype(BF16)
        k_ref[:, hd * QK_PAD + 128:hd * QK_PAD + 128 + rd] = k_rot
        k_ref[:, hd * QK_PAD + 128 + rd:(hd + 1) * QK_PAD] = zpad
        v_ref[:, hd * V_PAD:hd * V_PAD + V_DIM] = kva[:, hw + lo:hw + lo + 128].astype(BF16)
        v_ref[:, hd * V_PAD + V_DIM:(hd + 1) * V_PAD] = jnp.ones((x.shape[0], V_PAD - V_DIM), BF16)


def _pre_mla(x, mods, g0, wd, gq, gkv, wq, wkv, cos_t, sin_t, tm):
    b, n, d = x.shape
    nb_mod = mods.shape[0]
    full = lambda a: pl.BlockSpec(a.shape, lambda i, j: (0,) * a.ndim)
    mod_map = (lambda i, j: (i, 0, 0)) if nb_mod > 1 else (lambda i, j: (0, 0, 0))
    qk_w = MLA_HEADS * QK_PAD
    v_w = MLA_HEADS * V_PAD
    return pl.pallas_call(
        _pre_mla_kernel,
        out_shape=(jax.ShapeDtypeStruct((b, n, qk_w), BF16),
                   jax.ShapeDtypeStruct((b, n, qk_w), BF16),
                   jax.ShapeDtypeStruct((b, n, v_w), BF16)),
        grid=(b, n // tm),
        in_specs=[pl.BlockSpec((None, tm, d), lambda i, j: (i, j, 0)),
                  pl.BlockSpec((None, 1, mods.shape[-1]), mod_map),
                  full(g0), full(wd), full(gq), full(gkv), full(wq), full(wkv),
                  pl.BlockSpec((tm, 128), lambda i, j: (j, 0)),
                  pl.BlockSpec((tm, 128), lambda i, j: (j, 0))],
        out_specs=(pl.BlockSpec((None, tm, qk_w), lambda i, j: (i, j, 0)),
                   pl.BlockSpec((None, tm, qk_w), lambda i, j: (i, j, 0)),
                   pl.BlockSpec((None, tm, v_w), lambda i, j: (i, j, 0))),
        compiler_params=_cparams(("arbitrary", "arbitrary")),
        name="pre_mla",
    )(x, mods, g0, wd, gq, gkv, wq, wkv, cos_t, sin_t)


def _attn_kernel(*refs, n_seg):
    q_ref = refs[0]
    k_refs = refs[1:1 + n_seg]
    v_refs = refs[1 + n_seg:1 + 2 * n_seg]
    o_ref = refs[1 + 2 * n_seg]
    nt = (((1,), (1,)), ((), ()))

    def scores(hd):
        q = q_ref[:, hd * QK_PAD:(hd + 1) * QK_PAD]
        return [lax.dot_general(q, k[:, hd * QK_PAD:(hd + 1) * QK_PAD], nt, preferred_element_type=F32)
                for k in k_refs]

    nxt = scores(0)
    for hd in range(MLA_HEADS):
        ss = nxt
        if hd + 1 < MLA_HEADS:
            nxt = scores(hd + 1)
        m = ss[0].max(axis=-1, keepdims=True)
        for s in ss[1:]:
            m = jnp.maximum(m, s.max(axis=-1, keepdims=True))
        acc = None
        for s, v in zip(ss, v_refs):
            pv = _dot(jnp.exp2((s - m).astype(BF16)), v[:, hd * V_PAD:(hd + 1) * V_PAD])
            acc = pv if acc is None else acc + pv
        o_ref[:, hd * V_DIM:(hd + 1) * V_DIM] = (acc[:, :V_DIM] / acc[:, V_DIM:V_DIM + 1]).astype(BF16)


def _attention(q, ks, vs, tq):
    b, nq, qk_w = q.shape
    v_w = MLA_HEADS * V_DIM
    kv_spec = lambda a: pl.BlockSpec((None,) + a.shape[1:], lambda i, j: (i, 0, 0))
    return pl.pallas_call(
        functools.partial(_attn_kernel, n_seg=len(ks)),
        out_shape=jax.ShapeDtypeStruct((b, nq, v_w), BF16),
        grid=(b, nq // tq),
        in_specs=[pl.BlockSpec((None, tq, qk_w), lambda i, j: (i, j, 0))]
                 + [kv_spec(a) for a in ks] + [kv_spec(a) for a in vs],
        out_specs=pl.BlockSpec((None, tq, v_w), lambda i, j: (i, j, 0)),
        compiler_params=_cparams(("arbitrary", "arbitrary")),
        name="mla_attention",
    )(q, *ks, *vs)


SUB_ROWS = 256


def _sub_tiles(n):
    return [slice(r, r + SUB_ROWS) for r in range(0, n, SUB_ROWS)]


def _post_core(o, x, rows, mod_ref, g1_ref, g2_ref, wr_ref, x1_ref, fin_ref, lg_ref):
    d = x.shape[-1]
    ne = lg_ref.shape[0]
    gate = _rows(_mod_chunk(mod_ref, 2, d), x)
    shift = _rows(_mod_chunk(mod_ref, 3, d), x)
    scale = _rows(_mod_chunk(mod_ref, 4, d), x)
    x1 = x + gate * _rms(o, g1_ref[...])
    fin = _rms(x1, g2_ref[...]) * (1.0 + scale) + shift
    x1_ref[rows, :] = x1
    _pack_store(fin_ref, fin, row0=rows.start)
    nt = (((1,), (1,)), ((), ()))
    f_hi = fin.astype(BF16)
    f_lo = (fin - f_hi.astype(F32)).astype(BF16)
    r_hi = lax.dot_general(wr_ref[...], f_hi, nt, preferred_element_type=F32)
    r_lo = lax.dot_general(wr_ref[0:ne, :], f_lo, nt, preferred_element_type=F32)
    lg_ref[:, rows] = r_hi[:ne] + r_hi[ne:] + r_lo


def _post_proj_kernel(o_ref, wo_ref, x_ref, mod_ref, g1_ref, g2_ref, wr_ref, *rest):
    x1_ref, fin_ref, lg_ref = rest[-3:]
    for rows in _sub_tiles(x_ref.shape[0]):
        o = _dot(o_ref[rows, :], wo_ref[...])
        _post_core(o, x_ref[rows, :], rows, mod_ref, g1_ref, g2_ref, wr_ref, x1_ref, fin_ref, lg_ref)


def _post_proj_tm_kernel(o_ref, wo_ref, x_ref, mod_ref, g1_ref, g2_ref, wr_ref, *rest):
    x1_ref, fin_ref, lg_ref = rest[-3:]
    nb, tt, d = x_ref.shape
    ts = SUB_ROWS // nb
    for t0 in range(0, tt, ts):
        o = _dot(o_ref[:, t0:t0 + ts, :].reshape(nb * ts, o_ref.shape[-1]), wo_ref[...])
        o = jnp.swapaxes(o.reshape(nb, ts, d), 0, 1).reshape(ts * nb, d)
        x = jnp.swapaxes(x_ref[:, t0:t0 + ts, :], 0, 1).reshape(ts * nb, d)
        _post_core(o, x, slice(t0 * nb, (t0 + ts) * nb), mod_ref, g1_ref, g2_ref, wr_ref, x1_ref, fin_ref, lg_ref)


def _chunk_to_rows(yc_ref, c, nb):
    q, grp = S5_Q, S5_GROUP
    per_tile = LANES // grp
    lane_grp = lax.broadcasted_iota(jnp.int32, (nb, LANES), 1) // grp
    n_pairs = yc_ref.shape[0]
    pieces = [[yc_ref[p, c * nb:(c + 1) * nb, lt * LANES:(lt + 1) * LANES] for lt in range(2 * q // per_tile)]
              for p in range(n_pairs)]
    out_rows = []
    for t in range(q):
        tiles = []
        for lb in range(2 * n_pairs // per_tile):
            dest = None
            for g8 in range(per_tile):
                g = lb * per_tile + g8
                piece = pieces[g // 2][(g % 2) * (q // per_tile) + t // per_tile]
                r = pltpu.roll(piece, ((g8 - t % per_tile) * grp) % LANES, 1)
                dest = r if dest is None else jnp.where(lane_grp == g8, r, dest)
            tiles.append(dest)
        out_rows.append(jnp.concatenate(tiles, axis=-1))
    return jnp.concatenate(out_rows, axis=0)


def _post_glu_kernel(h_ref, yc_ref, dsk_ref, wg_ref, bg_ref, x_ref, mod_ref, g1_ref, g2_ref, wr_ref,
                     *rest):
    x1_ref, fin_ref, lg_ref = rest[-3:]
    d = x_ref.shape[-1]
    nb = mod_ref.shape[0]
    assert SUB_ROWS == S5_Q * nb
    for ci, rows in enumerate(_sub_tiles(x_ref.shape[0])):
        h = jnp.concatenate([h_ref[g, rows, :] for g in range(h_ref.shape[0])], axis=-1).astype(F32)
        y = h * dsk_ref[...] + _chunk_to_rows(yc_ref, ci, nb)
        z = _dot(jax.nn.gelu(y, approximate=True).astype(BF16), wg_ref[...]) + bg_ref[...]
        o = z[:, :d] * jax.nn.sigmoid(z[:, d:])
        _post_core(o, x_ref[rows, :], rows, mod_ref, g1_ref, g2_ref, wr_ref, x1_ref, fin_ref, lg_ref)


def _post_mixer(kernel, tok_inputs, consts, x, n_tok, x_off, mods, g1, g2, wr_t, tm, rows_per_mod, name,
                moe_total=None, moe_off=0, prev=None, x_spec=None):
    d = x.shape[-1]
    ne = wr_t.shape[0] // 2
    moe_total = n_tok if moe_total is None else moe_total
    tiles_per_mod = rows_per_mod // tm
    xo, mo = x_off // tm, moe_off // tm
    full = lambda a: pl.BlockSpec(a.shape, lambda i: (0,) * a.ndim)
    tile = pl.BlockSpec((tm, d), lambda i: (i, 0))
    mod_spec = pl.BlockSpec((None,) + mods.shape[1:], lambda i: (i // tiles_per_mod, 0, 0))
    x_spec = pl.BlockSpec((tm, d), lambda i: (i + xo, 0)) if x_spec is None else x_spec
    in_specs = ([spec for _, spec in tok_inputs] + [full(a) for a in consts]
                + [x_spec, mod_spec, full(g1), full(g2), full(wr_t)])
    args = [a for a, _ in tok_inputs] + list(consts) + [x, mods, g1, g2, wr_t]
    aliases = {}
    if prev is not None:
        aliases = {len(args): 1, len(args) + 1: 2}
        in_specs += [pl.BlockSpec(memory_space=pl.ANY)] * 2
        args += list(prev)
    return pl.pallas_call(
        kernel,
        out_shape=(jax.ShapeDtypeStruct((n_tok, d), F32),
                   jax.ShapeDtypeStruct((moe_total * PACK_ROWS, LANES), U32),
                   jax.ShapeDtypeStruct((ne, moe_total), F32)),
        grid=(n_tok // tm,),
        in_specs=in_specs,
        out_specs=(tile, pl.BlockSpec((tm * PACK_ROWS, LANES), lambda i: (i + mo, 0)),
                   pl.BlockSpec((ne, tm), lambda i: (0, i + mo))),
        input_output_aliases=aliases,
        compiler_params=_cparams(("arbitrary",)),
        name=name,
    )(*args)


def _route_kernel(lg_ref, bias_ref, eidx_ref, gate_ref, rank_ref, cnt_ref, tri_ref, base_ref):
    i = pl.program_id(0)
    ne, tt = lg_ref.shape
    gsz = ne // N_EXPERT_GROUPS
    shp = (N_EXPERT_GROUPS, gsz, tt)
    neg = -jnp.inf

    @pl.when(i == 0)
    def _():
        base_ref[...] = jnp.zeros_like(base_ref)
        r = lax.broadcasted_iota(jnp.int32, (tt, tt), 0)
        c = lax.broadcasted_iota(jnp.int32, (tt, tt), 1)
        tri_ref[...] = (r < c).astype(BF16)

    scores = jax.nn.sigmoid(lg_ref[...])
    s3 = scores.reshape(shp)
    b3 = (scores + bias_ref[...]).reshape(shp)
    io_e = lax.broadcasted_iota(jnp.int32, shp, 1)
    io_g = lax.broadcasted_iota(jnp.int32, shp, 0)
    io_flat = io_g * gsz + io_e
    m1 = b3.max(axis=1, keepdims=True)
    i1 = jnp.where(b3 == m1, io_e, gsz).min(axis=1, keepdims=True)
    m2 = jnp.where(io_e == i1, neg, b3).max(axis=1, keepdims=True)
    cur = jnp.broadcast_to(m1 + m2, shp)
    gsel = jnp.zeros(shp, jnp.bool_)
    for _ in range(TOPK_GROUPS):
        m = cur.max(axis=0, keepdims=True)
        gi = jnp.where(cur == m, io_g, N_EXPERT_GROUPS).min(axis=0, keepdims=True)
        hit = io_g == gi
        gsel = jnp.logical_or(gsel, hit)
        cur = jnp.where(hit, neg, cur)
    cand = jnp.where(gsel, b3, neg)
    sel = jnp.zeros(shp, jnp.bool_)
    eids, gts = [], []
    for _ in range(TOP_K):
        m = cand.max(axis=0, keepdims=True).max(axis=1, keepdims=True)
        ei = jnp.where(cand == m, io_flat, ne).min(axis=0, keepdims=True).min(axis=1, keepdims=True)
        hit = io_flat == ei
        gts.append(jnp.where(hit, s3, 0.0).sum(axis=0, keepdims=True).sum(axis=1, keepdims=True))
        eids.append(ei)
        sel = jnp.logical_or(sel, hit)
        cand = jnp.where(hit, neg, cand)
    gsum = gts[0]
    for g in gts[1:]:
        gsum = gsum + g
    self32 = sel.astype(F32).reshape(ne, tt)
    cnt = _dot(self32.astype(BF16), tri_ref[...]) + base_ref[...]
    cnt3 = cnt.reshape(shp)
    for k in range(TOP_K):
        hit = io_flat == eids[k]
        rk = jnp.where(hit, cnt3, 0.0).sum(axis=0, keepdims=True).sum(axis=1, keepdims=True)
        rank_ref[k:k + 1, :] = rk.reshape(1, tt).astype(jnp.int32)
        eidx_ref[k:k + 1, :] = eids[k].reshape(1, tt)
        gate_ref[k:k + 1, :] = (gts[k] / gsum * ROUTED_SCALE).reshape(1, tt)
    base_new = base_ref[...] + self32.sum(axis=1, keepdims=True)
    base_ref[...] = base_new
    cnt_ref[...] = jnp.broadcast_to(base_new, cnt_ref.shape)


def _route(logits_t, bias, tt, tok0, t):
    ne = logits_t.shape[0]
    off = tok0 // tt
    out_i = jax.ShapeDtypeStruct((TOP_K, t), jnp.int32)
    row = pl.BlockSpec((TOP_K, tt), lambda i: (0, i))
    return pl.pallas_call(
        _route_kernel,
        out_shape=(out_i, jax.ShapeDtypeStruct((TOP_K, t), F32), out_i,
                   jax.ShapeDtypeStruct((ne, 128), F32)),
        grid=(t // tt,),
        in_specs=[pl.BlockSpec((ne, tt), lambda i: (0, i + off)),
                  pl.BlockSpec((ne, 1), lambda i: (0, 0))],
        out_specs=(row, row, row, pl.BlockSpec((ne, 128), lambda i: (0, 0))),
        scratch_shapes=[pltpu.VMEM((tt, tt), BF16), pltpu.VMEM((ne, 1), F32)],
        compiler_params=_cparams(("arbitrary",)),
        name="moe_route",
    )(logits_t, bias.reshape(ne, 1))


def _dest_kernel(eidx_ref, rank_ref, start_ref, dest_ref):
    kk, tt = eidx_ref.shape
    ne = start_ref.shape[0]
    n_chunk, _, r = dest_ref.shape
    io_e = lax.broadcasted_iota(jnp.int32, (ne, tt), 0)
    start = start_ref[...]
    for k in range(kk):
        hit = io_e == eidx_ref[k:k + 1, :]
        dk = jnp.where(hit, start, 0).sum(axis=0, keepdims=True) + rank_ref[k:k + 1, :]
        for c in range(n_chunk):
            dest_ref[c, k:k + 1, :] = dk[:, c * r:(c + 1) * r]


def _dest_rows(eidx_t, rank_t, start, tt, r):
    kk, t = eidx_t.shape
    ne = start.shape[0]
    return pl.pallas_call(
        _dest_kernel,
        out_shape=jax.ShapeDtypeStruct((t // r, kk, r), jnp.int32),
        grid=(t // tt,),
        in_specs=[pl.BlockSpec((kk, tt), lambda i: (0, i)),
                  pl.BlockSpec((kk, tt), lambda i: (0, i)),
                  pl.BlockSpec((ne, 1), lambda i: (0, 0))],
        out_specs=pl.BlockSpec((tt // r, kk, r), lambda i: (i, 0, 0)),
        compiler_params=_cparams(("arbitrary",)),
        name="moe_dest",
    )(eidx_t, rank_t, start.reshape(ne, 1))


SC_CHUNK = 64


def _sc_mesh():
    return plsc.VectorSubcoreMesh(core_axis_name="c", subcore_axis_name="s")


def _sc_workers():
    info = plsc.get_sparse_core_info()
    return info.num_cores, info.num_cores * info.num_subcores


def _sc_scatter_rows(rows, dest, n_out, row0=0):
    n_chunk, kk, r = dest.shape
    nc, nw = _sc_workers()
    cpw = n_chunk // nw
    assert cpw * nw == n_chunk and cpw % 2 == 0 and row0 % r == 0 and row0 + n_chunk * r <= rows.shape[0]

    @functools.partial(
        pl.kernel, mesh=_sc_mesh(),
        out_type=jax.ShapeDtypeStruct((n_out,) + rows.shape[1:], rows.dtype),
        scratch_types=[pltpu.VMEM((2, kk, r), jnp.int32), pltpu.VMEM((2, r) + rows.shape[1:], rows.dtype),
                       pltpu.SemaphoreType.DMA((2,)), pltpu.SemaphoreType.DMA((2,))])
    def scatter(rows_hbm, dest_hbm, out_hbm, idx_v, rows_v, load_sem, scat_sem):
        c0 = (lax.axis_index("s") * nc + lax.axis_index("c")) * cpw

        def loads(c, b):
            return (pltpu.make_async_copy(dest_hbm.at[c], idx_v.at[b], load_sem.at[b]),
                    pltpu.make_async_copy(rows_hbm.at[pl.ds(row0 + c * r, r)], rows_v.at[b], load_sem.at[b]))

        def scat(b, k):
            return pltpu.make_async_copy(rows_v.at[b], out_hbm.at[idx_v.at[b, k]], scat_sem.at[b])

        for cp in loads(c0, 0):
            cp.start()

        @pl.loop(0, cpw, step=2)
        def _(ci):
            for b in range(2):
                c = c0 + ci + b
                for cp in loads(c, b):
                    cp.wait()
                for k in range(kk):
                    scat(b, k).start()

                @pl.when(ci + b >= 1)
                def _():
                    for k in range(kk):
                        scat(1 - b, k).wait()

                @pl.when(ci + b + 1 < cpw)
                def _():
                    for cp in loads(c + 1, 1 - b):
                        cp.start()

        for k in range(kk):
            scat((cpw - 1) % 2, k).wait()

    return scatter(rows, dest)


def _sc_gather_rows(src, dest):
    n_chunk, kk, r = dest.shape
    t = n_chunk * r
    nc, nw = _sc_workers()
    cpw = n_chunk // nw
    nbuf = 3
    assert cpw * nw == n_chunk and kk > nbuf

    @functools.partial(
        pl.kernel, mesh=_sc_mesh(),
        out_type=jax.ShapeDtypeStruct((kk, t) + src.shape[1:], src.dtype),
        scratch_types=[pltpu.VMEM((kk, r), jnp.int32), pltpu.VMEM((nbuf, r) + src.shape[1:], src.dtype),
                       pltpu.SemaphoreType.DMA((nbuf,)), pltpu.SemaphoreType.DMA((nbuf,))])
    def gather(src_hbm, dest_hbm, out_hbm, idx_v, rows_v, get_sem, put_sem):
        c0 = (lax.axis_index("s") * nc + lax.axis_index("c")) * cpw

        @pl.loop(0, cpw)
        def _(ci):
            c = c0 + ci
            pltpu.sync_copy(dest_hbm.at[c], idx_v)

            def get(k):
                return pltpu.make_async_copy(src_hbm.at[idx_v.at[k]], rows_v.at[k % nbuf], get_sem.at[k % nbuf])

            def put(k):
                return pltpu.make_async_copy(rows_v.at[k % nbuf], out_hbm.at[k, pl.ds(c * r, r)],
                                             put_sem.at[k % nbuf])

            for k in range(nbuf - 1):
                get(k).start()
            for k in range(kk):
                get(k).wait()
                put(k).start()
                if k + nbuf - 1 < kk:
                    if k >= 1:
                        put(k - 1).wait()
                    get(k + nbuf - 1).start()
            for k in range(kk - nbuf, kk):
                put(k).wait()

    return gather(src, dest)


def _expert_kernel(be_ref, nu_ref, x_ref, wg_ref, wu_ref, wd_ref, o_ref, wgu_s, wd_s):
    i = pl.program_id(0)
    tb = o_ref.shape[0] // PACK_ROWS

    @pl.when(i < nu_ref[0])
    def _():
        @pl.when(jnp.logical_or(i == 0, be_ref[i] != be_ref[jnp.maximum(i - 1, 0)]))
        def _():
            wgu_s[:, :D_EXPERT] = wg_ref[...].astype(BF16)
            wgu_s[:, D_EXPERT:] = wu_ref[...].astype(BF16)
            wd_s[...] = wd_ref[...].astype(BF16)

        x = jnp.concatenate([v.astype(BF16) for v in _unpack_load(x_ref, tb)], axis=-1)
        gu = _dot(x, wgu_s[...])
        g = gu[:, :D_EXPERT]
        h = g * jax.nn.sigmoid(g) * gu[:, D_EXPERT:]
        _pack_store(o_ref, _dot(h.astype(BF16), wd_s[...]))


def _experts(xs, blk_e, n_used, w_gate, w_up, w_down, layer, tb):
    rows = xs.shape[0] // PACK_ROWS
    _, ne, d, de = w_gate.shape
    nb = rows // tb
    row_map = lambda i, be, nu: (jnp.minimum(i, nu[0] - 1), 0)
    w_map = lambda i, be, nu: (layer, be[i], 0, 0)
    grid_spec = pltpu.PrefetchScalarGridSpec(
        num_scalar_prefetch=2,
        grid=(nb,),
        in_specs=[pl.BlockSpec((tb * PACK_ROWS, LANES), row_map),
                  pl.BlockSpec((None, None, d, de), w_map),
                  pl.BlockSpec((None, None, d, de), w_map),
                  pl.BlockSpec((None, None, de, d), w_map)],
        out_specs=pl.BlockSpec((tb * PACK_ROWS, LANES), row_map),
        scratch_shapes=[pltpu.VMEM((d, 2 * de), BF16), pltpu.VMEM((de, d), BF16)],
    )
    return pl.pallas_call(
        _expert_kernel,
        out_shape=jax.ShapeDtypeStruct(xs.shape, U32),
        grid_spec=grid_spec,
        compiler_params=_cparams(("arbitrary",)),
        name="moe_experts",
    )(blk_e, n_used, xs, w_gate, w_up, w_down)


def _combine_kernel(yk_ref, gate_ref, fin_ref, shgu_ref, shd_ref, x1_ref, mod_ref, g3_ref, *rest, fuse_next):
    if fuse_next:
        nmod_ref, ng0_ref = rest[0], rest[1]
        o_ref, h_ref = rest[-2], rest[-1]
    else:
        o_ref = rest[-1]
    tm, d = x1_ref.shape
    for rows in _sub_tiles(tm):
        n, r0 = SUB_ROWS, rows.start
        gates = gate_ref[rows, :]
        blocks = None
        for k in range(TOP_K):
            gk = gates[:, k:k + 1]
            terms = [gk * v for v in _unpack_load(yk_ref, n, lead=(k,), row0=r0)]
            blocks = terms if blocks is None else [a + b for a, b in zip(blocks, terms)]
        fin = jnp.concatenate([v.astype(BF16) for v in _unpack_load(fin_ref, n, row0=r0)], axis=-1)
        gu = _dot(fin, shgu_ref[...])
        g = gu[:, :D_EXPERT]
        hsh = g * jax.nn.sigmoid(g) * gu[:, D_EXPERT:]
        f = jnp.concatenate(blocks, axis=-1) + _dot(hsh.astype(BF16), shd_ref[...])
        x1 = x1_ref[rows, :]
        x2 = x1 + _rows(_mod_chunk(mod_ref, 5, d), x1) * _rms(f, g3_ref[...])
        if len(o_ref.shape) == 2:
            o_ref[rows, :] = x2
        else:
            nb = o_ref.shape[0]
            ts = SUB_ROWS // nb
            o_ref[:, r0 // nb:r0 // nb + ts, :] = jnp.swapaxes(x2.reshape(ts, nb, d), 0, 1)
        if fuse_next:
            hn = (_rms(x2, ng0_ref[...]) * (1.0 + _rows(_mod_chunk(nmod_ref, 1, d), x2))
                  + _rows(_mod_chunk(nmod_ref, 0, d), x2))
            for gi in range(h_ref.shape[0]):
                h_ref[gi, rows, :] = hn[:, gi * LANES:(gi + 1) * LANES].astype(BF16)


def _combine(yk, gates, fin, shgu, shd, x1, mods, g3, tm, rows_per_mod, n_tok, x_off, yk_off, fin_off,
             batch_out=0, prev=None, out_rows=None, out_off=None, nxt=None):
    t, d = x1.shape
    out_rows = t if out_rows is None else out_rows
    out_off = x_off if out_off is None else out_off
    xo, yo, fo, oo = x_off // tm, yk_off // tm, fin_off // tm, out_off // tm
    tiles_per_mod = rows_per_mod // tm
    full = lambda a: pl.BlockSpec(a.shape, lambda i: (0,) * a.ndim)
    if batch_out:
        out_shape = [jax.ShapeDtypeStruct((batch_out, out_rows // batch_out, d), F32)]
        out_specs = [pl.BlockSpec((batch_out, tm // batch_out, d), lambda i: (0, i + oo, 0))]
    else:
        out_shape = [jax.ShapeDtypeStruct((out_rows, d), F32)]
        out_specs = [pl.BlockSpec((tm, d), lambda i: (i + oo, 0))]
    in_specs = [pl.BlockSpec((TOP_K, tm * PACK_ROWS, LANES), lambda i: (0, i + yo, 0)),
                pl.BlockSpec((tm, TOP_K), lambda i: (i + yo, 0)),
                pl.BlockSpec((tm * PACK_ROWS, LANES), lambda i: (i + fo, 0)),
                full(shgu), full(shd),
                pl.BlockSpec((tm, d), lambda i: (i + xo, 0)),
                pl.BlockSpec((None,) + mods.shape[1:], lambda i: ((i + xo) // tiles_per_mod, 0, 0)),
                full(g3)]
    args = [yk, gates, fin, shgu, shd, x1, mods, g3]
    if nxt is not None:
        in_specs += [full(nxt[0]), full(nxt[1])]
        args += list(nxt)
        out_shape.append(jax.ShapeDtypeStruct((d // LANES, out_rows, LANES), BF16))
        out_specs.append(pl.BlockSpec((d // LANES, tm, LANES), lambda i: (0, i + oo, 0)))
    aliases = {}
    if prev is not None:
        for j, p in enumerate(prev if isinstance(prev, (tuple, list)) else [prev]):
            in_specs.append(pl.BlockSpec(memory_space=pl.ANY))
            aliases[len(args)] = j
            args.append(p)
    out = pl.pallas_call(
        functools.partial(_combine_kernel, fuse_next=nxt is not None),
        out_shape=tuple(out_shape),
        grid=(n_tok // tm,),
        in_specs=in_specs,
        out_specs=tuple(out_specs),
        input_output_aliases=aliases,
        compiler_params=_cparams(("arbitrary",)),
        name="moe_combine",
    )(*args)
    return out if nxt is not None else out[0]


def _moe(fin, logits_t, bias, w_gate, w_up, w_down, layer, tb, tok0, t):
    t_all = fin.shape[0] // PACK_ROWS
    ne = w_gate.shape[1]
    tt = 512
    eidx_t, gates_t, rank_t, cnt = _route(logits_t, bias, tt, tok0, t)
    counts = cnt[:, 0].astype(jnp.int32)
    padded = (counts + tb - 1) // tb * tb
    pad_end = jnp.cumsum(padded)
    pad_start = pad_end - padded
    nb = (t * TOP_K) // tb + ne
    n_used = pad_end[-1] // tb
    blk_start = jnp.arange(nb, dtype=jnp.int32) * tb
    blk = jnp.sum(pad_end[None, :] <= jnp.minimum(blk_start, pad_end[-1] - 1)[:, None], axis=1)
    blk_e = jnp.minimum(blk, ne - 1).astype(jnp.int32)
    dest = _dest_rows(eidx_t, rank_t, pad_start, tt, SC_CHUNK)
    xs = _sc_scatter_rows(fin.reshape(t_all, PACK_ROWS, LANES), dest, nb * tb, row0=tok0)
    ys = _experts(xs.reshape(nb * tb * PACK_ROWS, LANES), blk_e, n_used.reshape(1).astype(jnp.int32),
                  w_gate, w_up, w_down, layer, tb)
    yk = _sc_gather_rows(ys.reshape(nb * tb, PACK_ROWS, LANES), dest)
    return yk.reshape(TOP_K, t * PACK_ROWS, LANES), gates_t.T


def _pre_s5_kernel(x_ref, mod_ref, g0_ref, *refs):
    h_ref, xt_ref = refs[-2:]
    nb, tt, d = x_ref.shape
    x = jnp.swapaxes(x_ref[...], 0, 1).reshape(tt * nb, d)
    h = (_rms(x, g0_ref[...]) * (1.0 + _rows(_mod_chunk(mod_ref, 1, d), x))
         + _rows(_mod_chunk(mod_ref, 0, d), x))
    for g in range(h_ref.shape[0]):
        h_ref[g] = h[:, g * LANES:(g + 1) * LANES].astype(BF16)
    xt_ref[...] = x


def _pre_s5(x, mods, g0, n_total, t_off, prev, tt):
    nb, n, d = x.shape
    off = t_off // tt
    out_shape = (jax.ShapeDtypeStruct((d // LANES, n_total * nb, LANES), BF16),
                 jax.ShapeDtypeStruct((n_total * nb, d), F32))
    out_specs = (pl.BlockSpec((d // LANES, tt * nb, LANES), lambda i: (0, i + off, 0)),
                 pl.BlockSpec((tt * nb, d), lambda i: (i + off, 0)))
    in_specs = [pl.BlockSpec((nb, tt, d), lambda i: (0, i, 0)),
                pl.BlockSpec(mods.shape, lambda i: (0, 0)),
                pl.BlockSpec(g0.shape, lambda i: (0, 0))]
    args = (x, mods, g0)
    aliases = {}
    if prev is not None:
        in_specs += [pl.BlockSpec(memory_space=pl.ANY)] * 2
        args += tuple(prev)
        aliases = {3: 0, 4: 1}
    return pl.pallas_call(
        _pre_s5_kernel,
        out_shape=out_shape,
        grid=(n // tt,),
        in_specs=in_specs,
        out_specs=out_specs,
        input_output_aliases=aliases,
        compiler_params=_cparams(("arbitrary",)),
        name="pre_s5",
    )(*args)


S5_Q = 16
S5_STEP_GROUPS = 4


def _s5c_kernel(h_ref, toep_ref, win_ref, wout_ref, lamq_ref, y_ref, u_ref, sre_ref, sim_ref, xin_ref, *,
                nb, n_ctx_chunks):
    npair, n_rows, kw = u_ref.shape
    half = kw // 2
    sw = 2 * S5_STATE
    n = n_rows // nb
    ncc = n_ctx_chunks
    per_tile = LANES // S5_GROUP
    lane_grp = lax.broadcasted_iota(jnp.int32, (nb, LANES), 1) // S5_GROUP
    steps_per_block = per_tile // S5_STEP_GROUPS

    def relayout_from(gl0):
        def relayout(c, carry):
            r0 = c * (S5_Q * nb)
            pieces = [h_ref[pl.ds(pl.multiple_of(r0 + s * nb, nb), nb), :].astype(F32) for s in range(S5_Q)]
            rows = pl.ds(pl.multiple_of(c * nb, nb), nb)
            for gq in range(S5_STEP_GROUPS):
                for j in range(S5_Q // per_tile):
                    dest = None
                    for s8 in range(per_tile):
                        r = pltpu.roll(pieces[j * per_tile + s8], ((s8 - gl0 - gq) * S5_GROUP) % LANES, 1)
                        dest = r if dest is None else jnp.where(lane_grp == s8, r, dest)
                    lo = (gq % 2) * half + j * LANES
                    u_ref[gq // 2, rows, lo:lo + LANES] = dest.astype(BF16)
            return carry

        lax.fori_loop(0, n, relayout, 0)

    for k in range(steps_per_block):
        pl.when(pl.program_id(0) % steps_per_block == k)(functools.partial(relayout_from, k * S5_STEP_GROUPS))
    for dr in range(2):
        for p in range(npair):
            s = _dot(u_ref[p], win_ref[dr, p])
            sre_ref[dr, :, p * sw:(p + 1) * sw] = s[:, :sw]
            sim_ref[dr, :, p * sw:(p + 1) * sw] = s[:, sw:]
    w = npair * sw
    lr = [jnp.broadcast_to(lamq_ref[dr, 0:1, :], (nb, w)) for dr in range(2)]
    li = [jnp.broadcast_to(lamq_ref[dr, 1:2, :], (nb, w)) for dr in range(2)]

    def step(j, carry):
        rev = jnp.where(j < ncc, ncc - 1 - j, n - 1 - (j - ncc))
        out = []
        for dr in range(2):
            xr, xi = carry[dr]
            c = j if dr == 0 else rev
            rows = pl.ds(pl.multiple_of(c * nb, nb), nb)
            for p in range(npair):
                xin_ref[dr, rows, 2 * p * sw:(2 * p + 1) * sw] = xr[:, p * sw:(p + 1) * sw].astype(BF16)
                xin_ref[dr, rows, (2 * p + 1) * sw:(2 * p + 2) * sw] = xi[:, p * sw:(p + 1) * sw].astype(BF16)
            nr = lr[dr] * xr - li[dr] * xi + sre_ref[dr, rows, :]
            ni = lr[dr] * xi + li[dr] * xr + sim_ref[dr, rows, :]
            out.append((nr, ni))
        return tuple(out)

    zero = jnp.zeros((nb, w), F32)
    lax.fori_loop(0, n, step, ((zero, zero), (zero, zero)), unroll=2)
    lat = slice(ncc * nb, n_rows)
    for p in range(npair):
        acc = None
        for dr in range(2):
            intra = jnp.concatenate([_dot(u_ref[p, lat, 0:half], toep_ref[dr, 2 * p]),
                                     _dot(u_ref[p, lat, half:kw], toep_ref[dr, 2 * p + 1])], axis=-1)
            term = intra + _dot(xin_ref[dr, lat, 2 * p * sw:(2 * p + 2) * sw], wout_ref[dr, p])
            acc = term if acc is None else acc + term
        y_ref[p] = acc


def _s5_chunked(h, toep, win, wout, lamq, nb, n_ctx):
    ng, n_tok, _ = h.shape
    n_groups = toep.shape[1]
    pp = S5_STEP_GROUPS // 2
    steps_per_block = (LANES // S5_GROUP) // S5_STEP_GROUPS
    n_rows = n_tok // S5_Q
    ncc = n_ctx // S5_Q
    lat_rows = n_rows - ncc * nb
    kw = 2 * S5_Q * S5_GROUP
    sw = 2 * S5_STATE
    return pl.pallas_call(
        functools.partial(_s5c_kernel, nb=nb, n_ctx_chunks=ncc),
        out_shape=jax.ShapeDtypeStruct((n_groups // 2, lat_rows, kw), F32),
        grid=(n_groups // S5_STEP_GROUPS,),
        in_specs=[pl.BlockSpec((None, n_tok, LANES), lambda i: (i // steps_per_block, 0, 0),
                               pipeline_mode=pl.Buffered(1)),
                  pl.BlockSpec((2, 2 * pp, kw // 2, kw // 2), lambda i: (0, i, 0, 0)),
                  pl.BlockSpec((2, pp, kw, 2 * sw), lambda i: (0, i, 0, 0)),
                  pl.BlockSpec((2, pp, 2 * sw, kw), lambda i: (0, i, 0, 0)),
                  pl.BlockSpec((2, None, 2, pp * sw), lambda i: (0, i, 0, 0))],
        out_specs=pl.BlockSpec((pp, lat_rows, kw), lambda i: (i, 0, 0)),
        scratch_shapes=[pltpu.VMEM((pp, n_rows, kw), BF16),
                        pltpu.VMEM((2, n_rows, pp * sw), F32), pltpu.VMEM((2, n_rows, pp * sw), F32),
                        pltpu.VMEM((2, n_rows, 2 * pp * sw), BF16)],
        compiler_params=_cparams(("arbitrary",)),
        name="s5_chunked",
    )(h, toep, win, wout, lamq)


def _s5c_params(lam_re, lam_im, log_step, b_re, b_im, c_re, c_im):
    hp = lax.Precision.HIGHEST
    q = S5_Q
    _, g, p = lam_re.shape
    ni = b_re.shape[-1]
    step = jnp.exp(log_step)[..., None]
    ar, ai = lam_re * step, lam_im * step
    tau = jnp.arange(q + 1, dtype=F32)[:, None, None, None]
    mag = jnp.exp(tau * ar)
    pr, pi = mag * jnp.cos(tau * ai), mag * jnp.sin(tau * ai)
    den = lam_re * lam_re + lam_im * lam_im
    f_re = ((pr[1] - 1.0) * lam_re + pi[1] * lam_im) / den
    f_im = (pi[1] * lam_re - (pr[1] - 1.0) * lam_im) / den
    bb_re = f_re[..., None] * b_re - f_im[..., None] * b_im
    bb_im = f_re[..., None] * b_im + f_im[..., None] * b_re
    cp_re = c_re[None] * pr[:, :, :, None, :] - c_im[None] * pi[:, :, :, None, :]
    cp_im = c_re[None] * pi[:, :, :, None, :] + c_im[None] * pr[:, :, :, None, :]
    taps = (jnp.einsum("tdgop,dgpi->tdgoi", cp_re[:q], bb_re, precision=hp)
            - jnp.einsum("tdgop,dgpi->tdgoi", cp_im[:q], bb_im, precision=hp))
    s_idx = jnp.arange(q)[:, None]
    t_idx = jnp.arange(q)[None, :]

    def toeplitz(dr):
        lag = (t_idx - s_idx) if dr == 0 else (s_idx - t_idx)
        k = taps[:, dr][jnp.clip(lag, 0, q - 1)]
        k = jnp.where((lag >= 0)[:, :, None, None, None], k, 0.0)
        return k.transpose(2, 0, 4, 1, 3).reshape(g, q * ni, q * ni)

    toep = jnp.stack([toeplitz(0), toeplitz(1)]).astype(BF16)

    def state_in(dr):
        e = (q - 1 - jnp.arange(q)) if dr == 0 else jnp.arange(q)
        er, ei = pr[e, dr], pi[e, dr]
        br, bi = bb_re[dr].transpose(0, 2, 1), bb_im[dr].transpose(0, 2, 1)
        w_re = er[:, :, None, :] * br[None] - ei[:, :, None, :] * bi[None]
        w_im = er[:, :, None, :] * bi[None] + ei[:, :, None, :] * br[None]
        fl = lambda a: a.transpose(1, 0, 2, 3).reshape(g, q * ni, p)
        return fl(w_re), fl(w_im)

    def state_out(dr):
        f = (jnp.arange(q) + 1) if dr == 0 else (q - jnp.arange(q))
        fl = lambda a: a.transpose(1, 3, 0, 2).reshape(g, p, q * ni)
        return fl(cp_re[f, dr]), fl(-cp_im[f, dr])

    z = lambda *shape: jnp.zeros(shape, F32)

    def pair_in(dr):
        w_re, w_im = state_in(dr)
        a_re, b_re_, a_im, b_im_ = w_re[0::2], w_re[1::2], w_im[0::2], w_im[1::2]
        zz = z(g // 2, q * ni, p)
        top = jnp.concatenate([a_re, zz, a_im, zz], axis=-1)
        bot = jnp.concatenate([zz, b_re_, zz, b_im_], axis=-1)
        return jnp.concatenate([top, bot], axis=1)

    def pair_out(dr):
        w_re, w_im = state_out(dr)
        zz = z(g // 2, p, q * ni)
        rows = [jnp.concatenate([w_re[0::2], zz], axis=-1), jnp.concatenate([zz, w_re[1::2]], axis=-1),
                jnp.concatenate([w_im[0::2], zz], axis=-1), jnp.concatenate([zz, w_im[1::2]], axis=-1)]
        return jnp.concatenate(rows, axis=1)

    win = jnp.stack([pair_in(0), pair_in(1)]).astype(BF16)
    wout = jnp.stack([pair_out(0), pair_out(1)]).astype(BF16)
    ng = S5_STEP_GROUPS
    lamq = jnp.stack([pr[q].reshape(2, g // ng, ng * p), pi[q].reshape(2, g // ng, ng * p)], axis=2)
    return toep, win, wout, lamq


def _rope_tables(n_tokens):
    rows = n_tokens // GRID_W
    row = jnp.repeat(jnp.arange(rows), GRID_W).astype(F32)
    col = jnp.tile(jnp.arange(GRID_W), rows).astype(F32)
    n_freq = ROPE_DIM // 4
    inv_freq = ROPE_BASE ** (-jnp.arange(n_freq, dtype=F32) / n_freq)
    ang = jnp.concatenate([row[:, None] * inv_freq, col[:, None] * inv_freq], axis=-1)
    cos, sin = jnp.cos(ang), jnp.sin(ang)
    z = jnp.zeros((n_tokens, 128 - ROPE_DIM), F32)
    return (jnp.concatenate([cos, cos, z], axis=-1), jnp.concatenate([-sin, sin, z], axis=-1))


def _router_halves(w_router):
    wt = w_router.T
    hi = wt.astype(BF16)
    lo = (wt - hi.astype(F32)).astype(BF16)
    return jnp.concatenate([hi, lo], axis=0)


def _split_pairs(w):
    ev, od = w[..., 0::2], w[..., 1::2]
    return jnp.concatenate([ev, od], axis=-1), jnp.concatenate([od, ev], axis=-1)


def _mla_weights(w_dqkv, w_uq, w_ukv):
    kp, kps = _split_pairs(w_dqkv[:, Q_LORA + KV_LORA:])
    wd = jnp.concatenate([w_dqkv[:, :Q_LORA + KV_LORA], kp, kps], axis=-1).astype(BF16)
    wq3 = w_uq.reshape(Q_LORA, MLA_HEADS, NOPE_DIM + ROPE_DIM)
    qp, qps = _split_pairs(wq3[:, :, NOPE_DIM:])
    wq = jnp.concatenate([wq3[:, :, :NOPE_DIM].reshape(Q_LORA, -1), qp.reshape(Q_LORA, -1),
                          qps.reshape(Q_LORA, -1)], axis=-1).astype(BF16)
    wkv3 = w_ukv.reshape(KV_LORA, MLA_HEADS, NOPE_DIM + V_DIM)
    wkv = jnp.concatenate([wkv3[:, :, :NOPE_DIM].reshape(KV_LORA, -1),
                           wkv3[:, :, NOPE_DIM:].reshape(KV_LORA, -1)], axis=-1).astype(BF16)
    return wd, wq, wkv


@jax.jit
def kernel(x, c, ctx, c_ctx, ada_w, ada_b, norm_g, mla_w_dqkv, mla_g_q, mla_g_kv, mla_w_uq, mla_w_ukv, mla_w_o, s5_lam_re, s5_lam_im, s5_log_step, s5_b_re, s5_b_im, s5_c_re, s5_c_im, s5_d, s5_w_glu, s5_b_glu, moe_w_router, moe_bias, moe_w_gate, moe_w_up, moe_w_down, sh_w_gate, sh_w_up, sh_w_down):
    b, l, d = x.shape
    n_ctx = ctx.shape[1]
    assert ada_w.shape[0] == 2 and b % 8 == 0
    ta = 256
    tm = 512
    tb = 512
    row = lambda v: v.reshape(1, -1)

    n_rows = (b + 1 + 7) // 8 * 8
    cvec = jnp.zeros((n_rows, d), F32).at[:b].set(c).at[b].set(c_ctx)
    mods = _ada_mods(cvec, ada_w, ada_b)

    def shared_weights(i):
        shgu = jnp.concatenate([sh_w_gate[i], sh_w_up[i]], axis=-1).astype(BF16)
        return shgu, sh_w_down[i].astype(BF16)

    mod_lat = mods[0, :b].reshape(b, 1, N_MOD * d)
    mod_ctx = mods[0, b].reshape(1, 1, N_MOD * d)
    wd, wq, wkv = _mla_weights(mla_w_dqkv[0], mla_w_uq[0], mla_w_ukv[0])
    cos_l, sin_l = _rope_tables(l)
    cos_c = jnp.concatenate([jnp.ones((n_ctx, ROPE_DIM), F32), jnp.zeros((n_ctx, 128 - ROPE_DIM), F32)], -1)
    sin_c = jnp.zeros((n_ctx, 128), F32)
    pre = functools.partial(_pre_mla, g0=row(norm_g[0, 0]), wd=wd, gq=row(mla_g_q[0]), gkv=row(mla_g_kv[0]),
                            wq=wq, wkv=wkv, tm=ta)
    q_c, k_c, v_c = pre(ctx, mod_ctx, cos_t=cos_c, sin_t=sin_c)
    q_l, k_l, v_l = pre(x, mod_lat, cos_t=cos_l, sin_t=sin_l)
    o_l = _attention(q_l, [k_c, k_l], [v_c, v_l], 2 * ta)
    o_c = _attention(q_c, [k_c], [v_c], n_ctx)

    wo = mla_w_o[0].astype(BF16)
    wr_t = _router_halves(moe_w_router[0])
    g1, g2, g3 = row(norm_g[0, 1]), row(norm_g[0, 2]), row(norm_g[0, 3])
    post = functools.partial(_post_mixer, _post_proj_kernel, consts=[wo], g1=g1, g2=g2, wr_t=wr_t, tm=tm,
                             name="post_mla")
    o_spec = pl.BlockSpec((tm, o_l.shape[-1]), lambda i: (i, 0))
    n_moe = b * (n_ctx + l)
    x1_c, fin, lg = post([(o_c.reshape(b * n_ctx, -1), o_spec)], x=ctx.reshape(b * n_ctx, d),
                         n_tok=b * n_ctx, x_off=0, mods=mod_ctx, rows_per_mod=b * n_ctx, moe_total=n_moe)
    tt = tm // b
    mod_lat_tm = mods[0, :b][None]
    x1_l, fin, lg = _post_mixer(
        _post_proj_tm_kernel, [(o_l, pl.BlockSpec((b, tt, o_l.shape[-1]), lambda i: (0, i, 0)))], [wo],
        x=x, n_tok=b * l, x_off=0, mods=mod_lat_tm, g1=g1, g2=g2, wr_t=wr_t, tm=tm, rows_per_mod=b * l,
        name="post_mla", moe_total=n_moe, moe_off=b * n_ctx, prev=(fin, lg),
        x_spec=pl.BlockSpec((b, tt, d), lambda i: (0, i, 0)))
    shgu, shd = shared_weights(0)
    lat_a = (l // 2) * b
    n_a = b * n_ctx + lat_a
    moe = functools.partial(_moe, fin, lg, moe_bias[0], moe_w_gate, moe_w_up, moe_w_down, 0, tb)
    yk_a, gates_a = moe(0, n_a)
    yk_b, gates_b = moe(n_a, n_moe - n_a)
    comb = functools.partial(_combine, fin=fin, shgu=shgu, shd=shd, g3=g3, tm=tm)
    x2_c = comb(yk_a, gates_a, x1=x1_c, mods=mod_ctx, rows_per_mod=b * n_ctx, n_tok=b * n_ctx, x_off=0, yk_off=0,
                fin_off=0)

    n_all = n_ctx + l
    mod_lat = mods[1, :b]
    mod_ctx = jnp.broadcast_to(mods[1, b][None], (b, N_MOD * d))
    g0 = row(norm_g[1, 0])
    h, xt = _pre_s5(x2_c.reshape(b, n_ctx, d), mod_ctx, g0, n_all, 0, None, tt)
    comb_l = functools.partial(comb, x1=x1_l, mods=mod_lat_tm, rows_per_mod=b * l, out_rows=n_all * b,
                               nxt=(mod_lat, g0))
    xt, h = comb_l(yk_a, gates_a, n_tok=lat_a, x_off=0, yk_off=b * n_ctx, fin_off=b * n_ctx,
                   out_off=n_ctx * b, prev=(xt, h))
    xt, h = comb_l(yk_b, gates_b, n_tok=b * l - lat_a, x_off=lat_a, yk_off=0, fin_off=n_a,
                   out_off=n_ctx * b + lat_a, prev=(xt, h))
    toep, win, wout, lamq = _s5c_params(s5_lam_re[0], s5_lam_im[0], s5_log_step[0], s5_b_re[0], s5_b_im[0],
                                        s5_c_re[0], s5_c_im[0])
    ng = d // LANES
    yc = _s5_chunked(h, toep, win, wout, lamq, b, n_ctx)
    g1, g2, g3 = row(norm_g[1, 1]), row(norm_g[1, 2]), row(norm_g[1, 3])
    lat0 = n_ctx * b // tm
    x1, fin, lg = _post_mixer(
        _post_glu_kernel,
        [(h, pl.BlockSpec((ng, tm, LANES), lambda i: (0, i + lat0, 0))),
         (yc, pl.BlockSpec((yc.shape[0], tm // S5_Q, yc.shape[-1]), lambda i: (0, i, 0)))],
        [row(s5_d[0]), s5_w_glu[0].astype(BF16), row(s5_b_glu[0])],
        x=xt, n_tok=l * b, x_off=n_ctx * b, mods=mod_lat[None], g1=g1, g2=g2, wr_t=_router_halves(moe_w_router[1]), tm=tm,
        rows_per_mod=l * b, name="post_s5")
    shgu, shd = shared_weights(1)
    n_h = (l // 2) * b
    moe = functools.partial(_moe, fin, lg, moe_bias[1], moe_w_gate, moe_w_up, moe_w_down, 1, tb)
    yk_a, gates_a = moe(0, n_h)
    yk_b, gates_b = moe(n_h, l * b - n_h)
    comb = functools.partial(_combine, fin=fin, shgu=shgu, shd=shd, x1=x1, mods=mod_lat[None], g3=g3, tm=tm,
                             rows_per_mod=l * b, yk_off=0, batch_out=b)
    out = comb(yk_a, gates_a, n_tok=n_h, x_off=0, fin_off=0)
    return comb(yk_b, gates_b, n_tok=l * b - n_h, x_off=n_h, fin_off=n_h, prev=out)
```

```python
import functools

import jax
import jax.numpy as jnp
from jax import lax
from jax.experimental import pallas as pl
from jax.experimental.pallas import tpu as pltpu
from jax.experimental.pallas import tpu_sc as plsc

F32 = jnp.float32
BF16 = jnp.bfloat16
U32 = jnp.uint32

N_MOD = 6
NORM_EPS = 1e-6
LOG2_E = 1.4426950408889634
GRID_W = 64
MLA_HEADS = 8
Q_LORA = 384
KV_LORA = 256
NOPE_DIM = 128
ROPE_DIM = 64
V_DIM = 128
V_PAD = 256
ROPE_BASE = 10000.0
QK_PAD = 256
S5_GROUP = 16
S5_STATE = 64
N_EXPERTS = 64
TOP_K = 8
N_EXPERT_GROUPS = 8
TOPK_GROUPS = 4
D_EXPERT = 256
ROUTED_SCALE = 2.5

VMEM_LIMIT = 56 * 1024 * 1024


def _cparams(sem):
    return pltpu.CompilerParams(dimension_semantics=sem, vmem_limit_bytes=VMEM_LIMIT)


def _rms(x, g):
    return x * lax.rsqrt(jnp.mean(x * x, axis=-1, keepdims=True) + NORM_EPS) * g


def _rows(v, like):
    r = v.shape[0]
    if r == 1:
        return v
    tm, d = like.shape
    return jnp.broadcast_to(v[None], (tm // r, r, d)).reshape(tm, d)


def _mod_chunk(mod_ref, j, d):
    return mod_ref[:, j * d:(j + 1) * d]


def _dot(a, b):
    return jnp.dot(a, b, preferred_element_type=F32)


PACK_ROWS = 4
LANES = 128


def _pack_store(ref, val, lead=(), row0=0):
    n = val.shape[0]
    bits = lax.bitcast_convert_type(val.astype(BF16).astype(F32), U32)
    for s in range(PACK_ROWS):
        lo = bits[:, s * LANES:(s + 1) * LANES] >> 16
        hi = bits[:, (s + PACK_ROWS) * LANES:(s + PACK_ROWS + 1) * LANES] & jnp.uint32(0xFFFF0000)
        ref[lead + (pl.ds(row0 * PACK_ROWS + s, n, stride=PACK_ROWS), slice(None))] = lo | hi


def _unpack_load(ref, n, lead=(), row0=0):
    los, his = [], []
    for s in range(PACK_ROWS):
        w = ref[lead + (pl.ds(row0 * PACK_ROWS + s, n, stride=PACK_ROWS), slice(None))]
        los.append(lax.bitcast_convert_type(w << 16, F32))
        his.append(lax.bitcast_convert_type(w & jnp.uint32(0xFFFF0000), F32))
    return los + his


def _ada_kernel(c_ref, w_ref, b_ref, o_ref):
    c = c_ref[...]
    s = c * jax.nn.sigmoid(c)
    o_ref[...] = jnp.dot(s, w_ref[...], preferred_element_type=F32,
                         precision=lax.Precision.HIGHEST) + b_ref[...]


def _ada_mods(cvec, ada_w, ada_b):
    depth, d, n = ada_w.shape
    rows = cvec.shape[0]
    tn = 1536
    return pl.pallas_call(
        _ada_kernel,
        out_shape=jax.ShapeDtypeStruct((depth, rows, n), F32),
        grid=(depth, n // tn),
        in_specs=[pl.BlockSpec((rows, d), lambda l, j: (0, 0)),
                  pl.BlockSpec((None, d, tn), lambda l, j: (l, 0, j)),
                  pl.BlockSpec((None, 1, tn), lambda l, j: (l, 0, j))],
        out_specs=pl.BlockSpec((None, rows, tn), lambda l, j: (l, 0, j)),
        compiler_params=_cparams(("arbitrary", "arbitrary")),
        name="ada_mods",
    )(cvec, ada_w, ada_b.reshape(depth, 1, n))


def _pre_mla_kernel(x_ref, mod_ref, g0_ref, wd_ref, gq_ref, gkv_ref, wq_ref, wkv_ref, cos_ref, sin_ref,
                    q_ref, k_ref, v_ref):
    d = x_ref.shape[-1]
    x = x_ref[...]
    h = _rms(x, g0_ref[...]) * (1.0 + _mod_chunk(mod_ref, 1, d)) + _mod_chunk(mod_ref, 0, d)
    a = _dot(h.astype(BF16), wd_ref[...])
    cq = _rms(a[:, :Q_LORA], gq_ref[...])
    ckv = _rms(a[:, Q_LORA:Q_LORA + KV_LORA], gkv_ref[...])
    rd = ROPE_DIM
    cos = cos_ref[:, 0:rd]
    sin = sin_ref[:, 0:rd]
    o = Q_LORA + KV_LORA
    k_rot = (a[:, o:o + rd] * cos + a[:, o + rd:o + 2 * rd] * sin).astype(BF16)
    qa = _dot(cq.astype(BF16), wq_ref[...])
    kva = _dot(ckv.astype(BF16), wkv_ref[...])
    hw = MLA_HEADS * 128
    hr = MLA_HEADS * rd
    zpad = jnp.zeros((x.shape[0], QK_PAD - NOPE_DIM - rd), BF16)
    scale = (NOPE_DIM + ROPE_DIM) ** -0.5 * LOG2_E
    for hd in range(MLA_HEADS):
        lo = hd * 128
        q_rot = qa[:, hw + hd * rd:hw + (hd + 1) * rd] * cos + qa[:, hw + hr + hd * rd:hw + hr + (hd + 1) * rd] * sin
        q_ref[:, hd * QK_PAD:hd * QK_PAD + 128] = (qa[:, lo:lo + 128] * scale).astype(BF16)
        q_ref[:, hd * QK_PAD + 128:hd * QK_PAD + 128 + rd] = (q_rot * scale).astype(BF16)
        q_ref[:, hd * QK_PAD + 128 + rd:(hd + 1) * QK_PAD] = zpad
        k_ref[:, hd * QK_PAD:hd * QK_PAD + 128] = kva[:, lo:lo + 128].astype(BF16)
        k_ref[:, hd * QK_PAD + 128:hd * QK_PAD + 128 + rd] = k_rot
        k_ref[:, hd * QK_PAD + 128 + rd:(hd + 1) * QK_PAD] = zpad
        v_ref[:, hd * V_PAD:hd * V_PAD + V_DIM] = kva[:, hw + lo:hw + lo + 128].astype(BF16)
        v_ref[:, hd * V_PAD + V_DIM:(hd + 1) * V_PAD] = jnp.ones((x.shape[0], V_PAD - V_DIM), BF16)


def _pre_mla(x, mods, g0, wd, gq, gkv, wq, wkv, cos_t, sin_t, tm):
    b, n, d = x.shape
    nb_mod = mods.shape[0]
    full = lambda a: pl.BlockSpec(a.shape, lambda i, j: (0,) * a.ndim)
    mod_map = (lambda i, j: (i, 0, 0)) if nb_mod > 1 else (lambda i, j: (0, 0, 0))
    qk_w = MLA_HEADS * QK_PAD
    v_w = MLA_HEADS * V_PAD
    return pl.pallas_call(
        _pre_mla_kernel,
        out_shape=(jax.ShapeDtypeStruct((b, n, qk_w), BF16),
                   jax.ShapeDtypeStruct((b, n, qk_w), BF16),
                   jax.ShapeDtypeStruct((b, n, v_w), BF16)),
        grid=(b, n // tm),
        in_specs=[pl.BlockSpec((None, tm, d), lambda i, j: (i, j, 0)),
                  pl.BlockSpec((None, 1, mods.shape[-1]), mod_map),
                  full(g0), full(wd), full(gq), full(gkv), full(wq), full(wkv),
                  pl.BlockSpec((tm, 128), lambda i, j: (j, 0)),
                  pl.BlockSpec((tm, 128), lambda i, j: (j, 0))],
        out_specs=(pl.BlockSpec((None, tm, qk_w), lambda i, j: (i, j, 0)),
                   pl.BlockSpec((None, tm, qk_w), lambda i, j: (i, j, 0)),
                   pl.BlockSpec((None, tm, v_w), lambda i, j: (i, j, 0))),
        compiler_params=_cparams(("arbitrary", "arbitrary")),
        name="pre_mla",
    )(x, mods, g0, wd, gq, gkv, wq, wkv, cos_t, sin_t)


def _attn_kernel(*refs, n_seg):
    q_ref = refs[0]
    k_refs = refs[1:1 + n_seg]
    v_refs = refs[1 + n_seg:1 + 2 * n_seg]
    o_ref = refs[1 + 2 * n_seg]
    nt = (((1,), (1,)), ((), ()))

    def scores(hd):
        q = q_ref[:, hd * QK_PAD:(hd + 1) * QK_PAD]
        return [lax.dot_general(q, k[:, hd * QK_PAD:(hd + 1) * QK_PAD], nt, preferred_element_type=F32)
                for k in k_refs]

    nxt = scores(0)
    for hd in range(MLA_HEADS):
        ss = nxt
        if hd + 1 < MLA_HEADS:
            nxt = scores(hd + 1)
        m = ss[0].max(axis=-1, keepdims=True)
        for s in ss[1:]:
            m = jnp.maximum(m, s.max(axis=-1, keepdims=True))
        acc = None
        for s, v in zip(ss, v_refs):
            pv = _dot(jnp.exp2((s - m).astype(BF16)), v[:, hd * V_PAD:(hd + 1) * V_PAD])
            acc = pv if acc is None else acc + pv
        o_ref[:, hd * V_DIM:(hd + 1) * V_DIM] = (acc[:, :V_DIM] / acc[:, V_DIM:V_DIM + 1]).astype(BF16)


def _attention(q, ks, vs, tq):
    b, nq, qk_w = q.shape
    v_w = MLA_HEADS * V_DIM
    kv_spec = lambda a: pl.BlockSpec((None,) + a.shape[1:], lambda i, j: (i, 0, 0))
    return pl.pallas_call(
        functools.partial(_attn_kernel, n_seg=len(ks)),
        out_shape=jax.ShapeDtypeStruct((b, nq, v_w), BF16),
        grid=(b, nq // tq),
        in_specs=[pl.BlockSpec((None, tq, qk_w), lambda i, j: (i, j, 0))]
                 + [kv_spec(a) for a in ks] + [kv_spec(a) for a in vs],
        out_specs=pl.BlockSpec((None, tq, v_w), lambda i, j: (i, j, 0)),
        compiler_params=_cparams(("arbitrary", "arbitrary")),
        name="mla_attention",
    )(q, *ks, *vs)


SUB_ROWS = 256


def _sub_tiles(n):
    return [slice(r, r + SUB_ROWS) for r in range(0, n, SUB_ROWS)]


def _post_core(o, x, rows, mod_ref, g1_ref, g2_ref, wr_ref, x1_ref, fin_ref, lg_ref):
    d = x.shape[-1]
    ne = lg_ref.shape[0]
    gate = _rows(_mod_chunk(mod_ref, 2, d), x)
    shift = _rows(_mod_chunk(mod_ref, 3, d), x)
    scale = _rows(_mod_chunk(mod_ref, 4, d), x)
    x1 = x + gate * _rms(o, g1_ref[...])
    fin = _rms(x1, g2_ref[...]) * (1.0 + scale) + shift
    x1_ref[rows, :] = x1
    _pack_store(fin_ref, fin, row0=rows.start)
    nt = (((1,), (1,)), ((), ()))
    f_hi = fin.astype(BF16)
    f_lo = (fin - f_hi.astype(F32)).astype(BF16)
    r_hi = lax.dot_general(wr_ref[...], f_hi, nt, preferred_element_type=F32)
    r_lo = lax.dot_general(wr_ref[0:ne, :], f_lo, nt, preferred_element_type=F32)
    lg_ref[:, rows] = r_hi[:ne] + r_hi[ne:] + r_lo


def _post_proj_kernel(o_ref, wo_ref, x_ref, mod_ref, g1_ref, g2_ref, wr_ref, *rest):
    x1_ref, fin_ref, lg_ref = rest[-3:]
    for rows in _sub_tiles(x_ref.shape[0]):
        o = _dot(o_ref[rows, :], wo_ref[...])
        _post_core(o, x_ref[rows, :], rows, mod_ref, g1_ref, g2_ref, wr_ref, x1_ref, fin_ref, lg_ref)


def _post_proj_tm_kernel(o_ref, wo_ref, x_ref, mod_ref, g1_ref, g2_ref, wr_ref, *rest):
    x1_ref, fin_ref, lg_ref = rest[-3:]
    nb, tt, d = x_ref.shape
    ts = SUB_ROWS // nb
    for t0 in range(0, tt, ts):
        o = _dot(o_ref[:, t0:t0 + ts, :].reshape(nb * ts, o_ref.shape[-1]), wo_ref[...])
        o = jnp.swapaxes(o.reshape(nb, ts, d), 0, 1).reshape(ts * nb, d)
        x = jnp.swapaxes(x_ref[:, t0:t0 + ts, :], 0, 1).reshape(ts * nb, d)
        _post_core(o, x, slice(t0 * nb, (t0 + ts) * nb), mod_ref, g1_ref, g2_ref, wr_ref, x1_ref, fin_ref, lg_ref)


def _chunk_to_rows(yc_ref, c, nb):
    q, grp = S5_Q, S5_GROUP
    per_tile = LANES // grp
    lane_grp = lax.broadcasted_iota(jnp.int32, (nb, LANES), 1) // grp
    n_pairs = yc_ref.shape[0]
    pieces = [[yc_ref[p, c * nb:(c + 1) * nb, lt * LANES:(lt + 1) * LANES] for lt in range(2 * q // per_tile)]
              for p in range(n_pairs)]
    out_rows = []
    for t in range(q):
        tiles = []
        for lb in range(2 * n_pairs // per_tile):
            dest = None
            for g8 in range(per_tile):
                g = lb * per_tile + g8
                piece = pieces[g // 2][(g % 2) * (q // per_tile) + t // per_tile]
                r = pltpu.roll(piece, ((g8 - t % per_tile) * grp) % LANES, 1)
                dest = r if dest is None else jnp.where(lane_grp == g8, r, dest)
            tiles.append(dest)
        out_rows.append(jnp.concatenate(tiles, axis=-1))
    return jnp.concatenate(out_rows, axis=0)


def _post_glu_kernel(h_ref, yc_ref, dsk_ref, wg_ref, bg_ref, x_ref, mod_ref, g1_ref, g2_ref, wr_ref,
                     *rest):
    x1_ref, fin_ref, lg_ref = rest[-3:]
    d = x_ref.shape[-1]
    nb = mod_ref.shape[0]
    assert SUB_ROWS == S5_Q * nb
    for ci, rows in enumerate(_sub_tiles(x_ref.shape[0])):
        h = jnp.concatenate([h_ref[g, rows, :] for g in range(h_ref.shape[0])], axis=-1).astype(F32)
        y = h * dsk_ref[...] + _chunk_to_rows(yc_ref, ci, nb)
        z = _dot(jax.nn.gelu(y, approximate=True).astype(BF16), wg_ref[...]) + bg_ref[...]
        o = z[:, :d] * jax.nn.sigmoid(z[:, d:])
        _post_core(o, x_ref[rows, :], rows, mod_ref, g1_ref, g2_ref, wr_ref, x1_ref, fin_ref, lg_ref)


def _post_mixer(kernel, tok_inputs, consts, x, n_tok, x_off, mods, g1, g2, wr_t, tm, rows_per_mod, name,
                moe_total=None, moe_off=0, prev=None, x_spec=None):
    d = x.shape[-1]
    ne = wr_t.shape[0] // 2
    moe_total = n_tok if moe_total is None else moe_total
    tiles_per_mod = rows_per_mod // tm
    xo, mo = x_off // tm, moe_off // tm
    full = lambda a: pl.BlockSpec(a.shape, lambda i: (0,) * a.ndim)
    tile = pl.BlockSpec((tm, d), lambda i: (i, 0))
    mod_spec = pl.BlockSpec((None,) + mods.shape[1:], lambda i: (i // tiles_per_mod, 0, 0))
    x_spec = pl.BlockSpec((tm, d), lambda i: (i + xo, 0)) if x_spec is None else x_spec
    in_specs = ([spec for _, spec in tok_inputs] + [full(a) for a in consts]
                + [x_spec, mod_spec, full(g1), full(g2), full(wr_t)])
    args = [a for a, _ in tok_inputs] + list(consts) + [x, mods, g1, g2, wr_t]
    aliases = {}
    if prev is not None:
        aliases = {len(args): 1, len(args) + 1: 2}
        in_specs += [pl.BlockSpec(memory_space=pl.ANY)] * 2
        args += list(prev)
    return pl.pallas_call(
        kernel,
        out_shape=(jax.ShapeDtypeStruct((n_tok, d), F32),
                   jax.ShapeDtypeStruct((moe_total * PACK_ROWS, LANES), U32),
                   jax.ShapeDtypeStruct((ne, moe_total), F32)),
        grid=(n_tok // tm,),
        in_specs=in_specs,
        out_specs=(tile, pl.BlockSpec((tm * PACK_ROWS, LANES), lambda i: (i + mo, 0)),
                   pl.BlockSpec((ne, tm), lambda i: (0, i + mo))),
        input_output_aliases=aliases,
        compiler_params=_cparams(("arbitrary",)),
        name=name,
    )(*args)


def _route_kernel(lg_ref, bias_ref, eidx_ref, gate_ref, rank_ref, cnt_ref, tri_ref, base_ref):
    i = pl.program_id(0)
    ne, tt = lg_ref.shape
    gsz = ne // N_EXPERT_GROUPS
    shp = (N_EXPERT_GROUPS, gsz, tt)
    neg = -jnp.inf

    @pl.when(i == 0)
    def _():
        base_ref[...] = jnp.zeros_like(base_ref)
        r = lax.broadcasted_iota(jnp.int32, (tt, tt), 0)
        c = lax.broadcasted_iota(jnp.int32, (tt, tt), 1)
        tri_ref[...] = (r < c).astype(BF16)

    scores = jax.nn.sigmoid(lg_ref[...])
    s3 = scores.reshape(shp)
    b3 = (scores + bias_ref[...]).reshape(shp)
    io_e = lax.broadcasted_iota(jnp.int32, shp, 1)
    io_g = lax.broadcasted_iota(jnp.int32, shp, 0)
    io_flat = io_g * gsz + io_e
    m1 = b3.max(axis=1, keepdims=True)
    i1 = jnp.where(b3 == m1, io_e, gsz).min(axis=1, keepdims=True)
    m2 = jnp.where(io_e == i1, neg, b3).max(axis=1, keepdims=True)
    cur = jnp.broadcast_to(m1 + m2, shp)
    gsel = jnp.zeros(shp, jnp.bool_)
    for _ in range(TOPK_GROUPS):
        m = cur.max(axis=0, keepdims=True)
        gi = jnp.where(cur == m, io_g, N_EXPERT_GROUPS).min(axis=0, keepdims=True)
        hit = io_g == gi
        gsel = jnp.logical_or(gsel, hit)
        cur = jnp.where(hit, neg, cur)
    cand = jnp.where(gsel, b3, neg)
    sel = jnp.zeros(shp, jnp.bool_)
    eids, gts = [], []
    for _ in range(TOP_K):
        m = cand.max(axis=0, keepdims=True).max(axis=1, keepdims=True)
        ei = jnp.where(cand == m, io_flat, ne).min(axis=0, keepdims=True).min(axis=1, keepdims=True)
        hit = io_flat == ei
        gts.append(jnp.where(hit, s3, 0.0).sum(axis=0, keepdims=True).sum(axis=1, keepdims=True))
        eids.append(ei)
        sel = jnp.logical_or(sel, hit)
        cand = jnp.where(hit, neg, cand)
    gsum = gts[0]
    for g in gts[1:]:
        gsum = gsum + g
    self32 = sel.astype(F32).reshape(ne, tt)
    cnt = _dot(self32.astype(BF16), tri_ref[...]) + base_ref[...]
    cnt3 = cnt.reshape(shp)
    for k in range(TOP_K):
        hit = io_flat == eids[k]
        rk = jnp.where(hit, cnt3, 0.0).sum(axis=0, keepdims=True).sum(axis=1, keepdims=True)
        rank_ref[k:k + 1, :] = rk.reshape(1, tt).astype(jnp.int32)
        eidx_ref[k:k + 1, :] = eids[k].reshape(1, tt)
        gate_ref[k:k + 1, :] = (gts[k] / gsum * ROUTED_SCALE).reshape(1, tt)
    base_new = base_ref[...] + self32.sum(axis=1, keepdims=True)
    base_ref[...] = base_new
    cnt_ref[...] = jnp.broadcast_to(base_new, cnt_ref.shape)


def _route(logits_t, bias, tt, tok0, t):
    ne = logits_t.shape[0]
    off = tok0 // tt
    out_i = jax.ShapeDtypeStruct((TOP_K, t), jnp.int32)
    row = pl.BlockSpec((TOP_K, tt), lambda i: (0, i))
    return pl.pallas_call(
        _route_kernel,
        out_shape=(out_i, jax.ShapeDtypeStruct((TOP_K, t), F32), out_i,
                   jax.ShapeDtypeStruct((ne, 128), F32)),
        grid=(t // tt,),
        in_specs=[pl.BlockSpec((ne, tt), lambda i: (0, i + off)),
                  pl.BlockSpec((ne, 1), lambda i: (0, 0))],
        out_specs=(row, row, row, pl.BlockSpec((ne, 128), lambda i: (0, 0))),
        scratch_shapes=[pltpu.VMEM((tt, tt), BF16), pltpu.VMEM((ne, 1), F32)],
        compiler_params=_cparams(("arbitrary",)),
        name="moe_route",
    )(logits_t, bias.reshape(ne, 1))


def _dest_kernel(eidx_ref, rank_ref, start_ref, dest_ref):
    kk, tt = eidx_ref.shape
    ne = start_ref.shape[0]
    n_chunk, _, r = dest_ref.shape
    io_e = lax.broadcasted_iota(jnp.int32, (ne, tt), 0)
    start = start_ref[...]
    for k in range(kk):
        hit = io_e == eidx_ref[k:k + 1, :]
        dk = jnp.where(hit, start, 0).sum(axis=0, keepdims=True) + rank_ref[k:k + 1, :]
        for c in range(n_chunk):
            dest_ref[c, k:k + 1, :] = dk[:, c * r:(c + 1) * r]


def _dest_rows(eidx_t, rank_t, start, tt, r):
    kk, t = eidx_t.shape
    ne = start.shape[0]
    return pl.pallas_call(
        _dest_kernel,
        out_shape=jax.ShapeDtypeStruct((t // r, kk, r), jnp.int32),
        grid=(t // tt,),
        in_specs=[pl.BlockSpec((kk, tt), lambda i: (0, i)),
                  pl.BlockSpec((kk, tt), lambda i: (0, i)),
                  pl.BlockSpec((ne, 1), lambda i: (0, 0))],
        out_specs=pl.BlockSpec((tt // r, kk, r), lambda i: (i, 0, 0)),
        compiler_params=_cparams(("arbitrary",)),
        name="moe_dest",
    )(eidx_t, rank_t, start.reshape(ne, 1))


SC_CHUNK = 64


def _sc_mesh():
    return plsc.VectorSubcoreMesh(core_axis_name="c", subcore_axis_name="s")


def _sc_workers():
    info = plsc.get_sparse_core_info()
    return info.num_cores, info.num_cores * info.num_subcores


def _sc_scatter_rows(rows, dest, n_out, row0=0):
    n_chunk, kk, r = dest.shape
    nc, nw = _sc_workers()
    cpw = n_chunk // nw
    assert cpw * nw == n_chunk and cpw % 2 == 0 and row0 % r == 0 and row0 + n_chunk * r <= rows.shape[0]

    @functools.partial(
        pl.kernel, mesh=_sc_mesh(),
        out_type=jax.ShapeDtypeStruct((n_out,) + rows.shape[1:], rows.dtype),
        scratch_types=[pltpu.VMEM((2, kk, r), jnp.int32), pltpu.VMEM((2, r) + rows.shape[1:], rows.dtype),
                       pltpu.SemaphoreType.DMA((2,)), pltpu.SemaphoreType.DMA((2,))])
    def scatter(rows_hbm, dest_hbm, out_hbm, idx_v, rows_v, load_sem, scat_sem):
        c0 = (lax.axis_index("s") * nc + lax.axis_index("c")) * cpw

        def loads(c, b):
            return (pltpu.make_async_copy(dest_hbm.at[c], idx_v.at[b], load_sem.at[b]),
                    pltpu.make_async_copy(rows_hbm.at[pl.ds(row0 + c * r, r)], rows_v.at[b], load_sem.at[b]))

        def scat(b, k):
            return pltpu.make_async_copy(rows_v.at[b], out_hbm.at[idx_v.at[b, k]], scat_sem.at[b])

        for cp in loads(c0, 0):
            cp.start()

        @pl.loop(0, cpw, step=2)
        def _(ci):
            for b in range(2):
                c = c0 + ci + b
                for cp in loads(c, b):
                    cp.wait()
                for k in range(kk):
                    scat(b, k).start()

                @pl.when(ci + b >= 1)
                def _():
                    for k in range(kk):
                        scat(1 - b, k).wait()

                @pl.when(ci + b + 1 < cpw)
                def _():
                    for cp in loads(c + 1, 1 - b):
                        cp.start()

        for k in range(kk):
            scat((cpw - 1) % 2, k).wait()

    return scatter(rows, dest)


def _sc_gather_rows(src, dest):
    n_chunk, kk, r = dest.shape
    t = n_chunk * r
    nc, nw = _sc_workers()
    cpw = n_chunk // nw
    nbuf = 3
    assert cpw * nw == n_chunk and kk > nbuf

    @functools.partial(
        pl.kernel, mesh=_sc_mesh(),
        out_type=jax.ShapeDtypeStruct((kk, t) + src.shape[1:], src.dtype),
        scratch_types=[pltpu.VMEM((kk, r), jnp.int32), pltpu.VMEM((nbuf, r) + src.shape[1:], src.dtype),
                       pltpu.SemaphoreType.DMA((nbuf,)), pltpu.SemaphoreType.DMA((nbuf,))])
    def gather(src_hbm, dest_hbm, out_hbm, idx_v, rows_v, get_sem, put_sem):
        c0 = (lax.axis_index("s") * nc + lax.axis_index("c")) * cpw

        @pl.loop(0, cpw)
        def _(ci):
            c = c0 + ci
            pltpu.sync_copy(dest_hbm.at[c], idx_v)

            def get(k):
                return pltpu.make_async_copy(src_hbm.at[idx_v.at[k]], rows_v.at[k % nbuf], get_sem.at[k % nbuf])

            def put(k):
                return pltpu.make_async_copy(rows_v.at[k % nbuf], out_hbm.at[k, pl.ds(c * r, r)],
                                             put_sem.at[k % nbuf])

            for k in range(nbuf - 1):
                get(k).start()
            for k in range(kk):
                get(k).wait()
                put(k).start()
                if k + nbuf - 1 < kk:
                    if k >= 1:
                        put(k - 1).wait()
                    get(k + nbuf - 1).start()
            for k in range(kk - nbuf, kk):
                put(k).wait()

    return gather(src, dest)


def _expert_kernel(be_ref, nu_ref, x_ref, wg_ref, wu_ref, wd_ref, o_ref, wgu_s, wd_s):
    i = pl.program_id(0)
    tb = o_ref.shape[0] // PACK_ROWS

    @pl.when(i < nu_ref[0])
    def _():
        @pl.when(jnp.logical_or(i == 0, be_ref[i] != be_ref[jnp.maximum(i - 1, 0)]))
        def _():
            wgu_s[:, :D_EXPERT] = wg_ref[...].astype(BF16)
            wgu_s[:, D_EXPERT:] = wu_ref[...].astype(BF16)
            wd_s[...] = wd_ref[...].astype(BF16)

        x = jnp.concatenate([v.astype(BF16) for v in _unpack_load(x_ref, tb)], axis=-1)
        gu = _dot(x, wgu_s[...])
        g = gu[:, :D_EXPERT]
        h = g * jax.nn.sigmoid(g) * gu[:, D_EXPERT:]
        _pack_store(o_ref, _dot(h.astype(BF16), wd_s[...]))


def _experts(xs, blk_e, n_used, w_gate, w_up, w_down, layer, tb):
    rows = xs.shape[0] // PACK_ROWS
    _, ne, d, de = w_gate.shape
    nb = rows // tb
    row_map = lambda i, be, nu: (jnp.minimum(i, nu[0] - 1), 0)
    w_map = lambda i, be, nu: (layer, be[i], 0, 0)
    grid_spec = pltpu.PrefetchScalarGridSpec(
        num_scalar_prefetch=2,
        grid=(nb,),
        in_specs=[pl.BlockSpec((tb * PACK_ROWS, LANES), row_map),
                  pl.BlockSpec((None, None, d, de), w_map),
                  pl.BlockSpec((None, None, d, de), w_map),
                  pl.BlockSpec((None, None, de, d), w_map)],
        out_specs=pl.BlockSpec((tb * PACK_ROWS, LANES), row_map),
        scratch_shapes=[pltpu.VMEM((d, 2 * de), BF16), pltpu.VMEM((de, d), BF16)],
    )
    return pl.pallas_call(
        _expert_kernel,
        out_shape=jax.ShapeDtypeStruct(xs.shape, U32),
        grid_spec=grid_spec,
        compiler_params=_cparams(("arbitrary",)),
        name="moe_experts",
    )(blk_e, n_used, xs, w_gate, w_up, w_down)


def _combine_kernel(yk_ref, gate_ref, fin_ref, shgu_ref, shd_ref, x1_ref, mod_ref, g3_ref, *rest, fuse_next):
    if fuse_next:
        nmod_ref, ng0_ref = rest[0], rest[1]
        o_ref, h_ref = rest[-2], rest[-1]
    else:
        o_ref = rest[-1]
    tm, d = x1_ref.shape
    for rows in _sub_tiles(tm):
        n, r0 = SUB_ROWS, rows.start
        gates = gate_ref[rows, :]
        blocks = None
        for k in range(TOP_K):
            gk = gates[:, k:k + 1]
            terms = [gk * v for v in _unpack_load(yk_ref, n, lead=(k,), row0=r0)]
            blocks = terms if blocks is None else [a + b for a, b in zip(blocks, terms)]
        fin = jnp.concatenate([v.astype(BF16) for v in _unpack_load(fin_ref, n, row0=r0)], axis=-1)
        gu = _dot(fin, shgu_ref[...])
        g = gu[:, :D_EXPERT]
        hsh = g * jax.nn.sigmoid(g) * gu[:, D_EXPERT:]
        f = jnp.concatenate(blocks, axis=-1) + _dot(hsh.astype(BF16), shd_ref[...])
        x1 = x1_ref[rows, :]
        x2 = x1 + _rows(_mod_chunk(mod_ref, 5, d), x1) * _rms(f, g3_ref[...])
        if len(o_ref.shape) == 2:
            o_ref[rows, :] = x2
        else:
            nb = o_ref.shape[0]
            ts = SUB_ROWS // nb
            o_ref[:, r0 // nb:r0 // nb + ts, :] = jnp.swapaxes(x2.reshape(ts, nb, d), 0, 1)
        if fuse_next:
            hn = (_rms(x2, ng0_ref[...]) * (1.0 + _rows(_mod_chunk(nmod_ref, 1, d), x2))
                  + _rows(_mod_chunk(nmod_ref, 0, d), x2))
            for gi in range(h_ref.shape[0]):
                h_ref[gi, rows, :] = hn[:, gi * LANES:(gi + 1) * LANES].astype(BF16)


def _combine(yk, gates, fin, shgu, shd, x1, mods, g3, tm, rows_per_mod, n_tok, x_off, yk_off, fin_off,
             batch_out=0, prev=None, out_rows=None, out_off=None, nxt=None):
    t, d = x1.shape
    out_rows = t if out_rows is None else out_rows
    out_off = x_off if out_off is None else out_off
    xo, yo, fo, oo = x_off // tm, yk_off // tm, fin_off // tm, out_off // tm
    tiles_per_mod = rows_per_mod // tm
    full = lambda a: pl.BlockSpec(a.shape, lambda i: (0,) * a.ndim)
    if batch_out:
        out_shape = [jax.ShapeDtypeStruct((batch_out, out_rows // batch_out, d), F32)]
        out_specs = [pl.BlockSpec((batch_out, tm // batch_out, d), lambda i: (0, i + oo, 0))]
    else:
        out_shape = [jax.ShapeDtypeStruct((out_rows, d), F32)]
        out_specs = [pl.BlockSpec((tm, d), lambda i: (i + oo, 0))]
    in_specs = [pl.BlockSpec((TOP_K, tm * PACK_ROWS, LANES), lambda i: (0, i + yo, 0)),
                pl.BlockSpec((tm, TOP_K), lambda i: (i + yo, 0)),
                pl.BlockSpec((tm * PACK_ROWS, LANES), lambda i: (i + fo, 0)),
                full(shgu), full(shd),
                pl.BlockSpec((tm, d), lambda i: (i + xo, 0)),
                pl.BlockSpec((None,) + mods.shape[1:], lambda i: ((i + xo) // tiles_per_mod, 0, 0)),
                full(g3)]
    args = [yk, gates, fin, shgu, shd, x1, mods, g3]
    if nxt is not None:
        in_specs += [full(nxt[0]), full(nxt[1])]
        args += list(nxt)
        out_shape.append(jax.ShapeDtypeStruct((d // LANES, out_rows, LANES), BF16))
        out_specs.append(pl.BlockSpec((d // LANES, tm, LANES), lambda i: (0, i + oo, 0)))
    aliases = {}
    if prev is not None:
        for j, p in enumerate(prev if isinstance(prev, (tuple, list)) else [prev]):
            in_specs.append(pl.BlockSpec(memory_space=pl.ANY))
            aliases[len(args)] = j
            args.append(p)
    out = pl.pallas_call(
        functools.partial(_combine_kernel, fuse_next=nxt is not None),
        out_shape=tuple(out_shape),
        grid=(n_tok // tm,),
        in_specs=in_specs,
        out_specs=tuple(out_specs),
        input_output_aliases=aliases,
        compiler_params=_cparams(("arbitrary",)),
        name="moe_combine",
    )(*args)
    return out if nxt is not None else out[0]


def _moe(fin, logits_t, bias, w_gate, w_up, w_down, layer, tb, tok0, t):
    t_all = fin.shape[0] // PACK_ROWS
    ne = w_gate.shape[1]
    tt = 512
    eidx_t, gates_t, rank_t, cnt = _route(logits_t, bias, tt, tok0, t)
    counts = cnt[:, 0].astype(jnp.int32)
    padded = (counts + tb - 1) // tb * tb
    pad_end = jnp.cumsum(padded)
    pad_start = pad_end - padded
    nb = (t * TOP_K) // tb + ne
    n_used = pad_end[-1] // tb
    blk_start = jnp.arange(nb, dtype=jnp.int32) * tb
    blk = jnp.sum(pad_end[None, :] <= jnp.minimum(blk_start, pad_end[-1] - 1)[:, None], axis=1)
    blk_e = jnp.minimum(blk, ne - 1).astype(jnp.int32)
    dest = _dest_rows(eidx_t, rank_t, pad_start, tt, SC_CHUNK)
    xs = _sc_scatter_rows(fin.reshape(t_all, PACK_ROWS, LANES), dest, nb * tb, row0=tok0)
    ys = _experts(xs.reshape(nb * tb * PACK_ROWS, LANES), blk_e, n_used.reshape(1).astype(jnp.int32),
                  w_gate, w_up, w_down, layer, tb)
    yk = _sc_gather_rows(ys.reshape(nb * tb, PACK_ROWS, LANES), dest)
    return yk.reshape(TOP_K, t * PACK_ROWS, LANES), gates_t.T


def _pre_s5_kernel(x_ref, mod_ref, g0_ref, *refs):
    h_ref, xt_ref = refs[-2:]
    nb, tt, d = x_ref.shape
    x = jnp.swapaxes(x_ref[...], 0, 1).reshape(tt * nb, d)
    h = (_rms(x, g0_ref[...]) * (1.0 + _rows(_mod_chunk(mod_ref, 1, d), x))
         + _rows(_mod_chunk(mod_ref, 0, d), x))
    for g in range(h_ref.shape[0]):
        h_ref[g] = h[:, g * LANES:(g + 1) * LANES].astype(BF16)
    xt_ref[...] = x


def _pre_s5(x, mods, g0, n_total, t_off, prev, tt):
    nb, n, d = x.shape
    off = t_off // tt
    out_shape = (jax.ShapeDtypeStruct((d // LANES, n_total * nb, LANES), BF16),
                 jax.ShapeDtypeStruct((n_total * nb, d), F32))
    out_specs = (pl.BlockSpec((d // LANES, tt * nb, LANES), lambda i: (0, i + off, 0)),
                 pl.BlockSpec((tt * nb, d), lambda i: (i + off, 0)))
    in_specs = [pl.BlockSpec((nb, tt, d), lambda i: (0, i, 0)),
                pl.BlockSpec(mods.shape, lambda i: (0, 0)),
                pl.BlockSpec(g0.shape, lambda i: (0, 0))]
    args = (x, mods, g0)
    aliases = {}
    if prev is not None:
        in_specs += [pl.BlockSpec(memory_space=pl.ANY)] * 2
        args += tuple(prev)
        aliases = {3: 0, 4: 1}
    return pl.pallas_call(
        _pre_s5_kernel,
        out_shape=out_shape,
        grid=(n // tt,),
        in_specs=in_specs,
        out_specs=out_specs,
        input_output_aliases=aliases,
        compiler_params=_cparams(("arbitrary",)),
        name="pre_s5",
    )(*args)


S5_Q = 16
S5_STEP_GROUPS = 4


def _s5c_kernel(h_ref, toep_ref, win_ref, wout_ref, lamq_ref, y_ref, u_ref, sre_ref, sim_ref, xin_ref, *,
                nb, n_ctx_chunks):
    npair, n_rows, kw = u_ref.shape
    half = kw // 2
    sw = 2 * S5_STATE
    n = n_rows // nb
    ncc = n_ctx_chunks
    per_tile = LANES // S5_GROUP
    lane_grp = lax.broadcasted_iota(jnp.int32, (nb, LANES), 1) // S5_GROUP
    steps_per_block = per_tile // S5_STEP_GROUPS

    def relayout_from(gl0):
        def relayout(c, carry):
            r0 = c * (S5_Q * nb)
            pieces = [h_ref[pl.ds(pl.multiple_of(r0 + s * nb, nb), nb), :].astype(F32) for s in range(S5_Q)]
            rows = pl.ds(pl.multiple_of(c * nb, nb), nb)
            for gq in range(S5_STEP_GROUPS):
                for j in range(S5_Q // per_tile):
                    dest = None
                    for s8 in range(per_tile):
                        r = pltpu.roll(pieces[j * per_tile + s8], ((s8 - gl0 - gq) * S5_GROUP) % LANES, 1)
                        dest = r if dest is None else jnp.where(lane_grp == s8, r, dest)
                    lo = (gq % 2) * half + j * LANES
                    u_ref[gq // 2, rows, lo:lo + LANES] = dest.astype(BF16)
            return carry

        lax.fori_loop(0, n, relayout, 0)

    for k in range(steps_per_block):
        pl.when(pl.program_id(0) % steps_per_block == k)(functools.partial(relayout_from, k * S5_STEP_GROUPS))
    for dr in range(2):
        for p in range(npair):
            s = _dot(u_ref[p], win_ref[dr, p])
            sre_ref[dr, :, p * sw:(p + 1) * sw] = s[:, :sw]
            sim_ref[dr, :, p * sw:(p + 1) * sw] = s[:, sw:]
    w = npair * sw
    lr = [jnp.broadcast_to(lamq_ref[dr, 0:1, :], (nb, w)) for dr in range(2)]
    li = [jnp.broadcast_to(lamq_ref[dr, 1:2, :], (nb, w)) for dr in range(2)]

    def step(j, carry):
        rev = jnp.where(j < ncc, ncc - 1 - j, n - 1 - (j - ncc))
        out = []
        for dr in range(2):
            xr, xi = carry[dr]
            c = j if dr == 0 else rev
            rows = pl.ds(pl.multiple_of(c * nb, nb), nb)
            for p in range(npair):
                xin_ref[dr, rows, 2 * p * sw:(2 * p + 1) * sw] = xr[:, p * sw:(p + 1) * sw].astype(BF16)
                xin_ref[dr, rows, (2 * p + 1) * sw:(2 * p + 2) * sw] = xi[:, p * sw:(p + 1) * sw].astype(BF16)
            nr = lr[dr] * xr - li[dr] * xi + sre_ref[dr, rows, :]
            ni = lr[dr] * xi + li[dr] * xr + sim_ref[dr, rows, :]
            out.append((nr, ni))
        return tuple(out)

    zero = jnp.zeros((nb, w), F32)
    lax.fori_loop(0, n, step, ((zero, zero), (zero, zero)), unroll=2)
    lat = slice(ncc * nb, n_rows)
    for p in range(npair):
        acc = None
        for dr in range(2):
            intra = jnp.concatenate([_dot(u_ref[p, lat, 0:half], toep_ref[dr, 2 * p]),
                                     _dot(u_ref[p, lat, half:kw], toep_ref[dr, 2 * p + 1])], axis=-1)
            term = intra + _dot(xin_ref[dr, lat, 2 * p * sw:(2 * p + 2) * sw], wout_ref[dr, p])
            acc = term if acc is None else acc + term
        y_ref[p] = acc


def _s5_chunked(h, toep, win, wout, lamq, nb, n_ctx):
    ng, n_tok, _ = h.shape
    n_groups = toep.shape[1]
    pp = S5_STEP_GROUPS // 2
    steps_per_block = (LANES // S5_GROUP) // S5_STEP_GROUPS
    n_rows = n_tok // S5_Q
    ncc = n_ctx // S5_Q
    lat_rows = n_rows - ncc * nb
    kw = 2 * S5_Q * S5_GROUP
    sw = 2 * S5_STATE
    return pl.pallas_call(
        functools.partial(_s5c_kernel, nb=nb, n_ctx_chunks=ncc),
        out_shape=jax.ShapeDtypeStruct((n_groups // 2, lat_rows, kw), F32),
        grid=(n_groups // S5_STEP_GROUPS,),
        in_specs=[pl.BlockSpec((None, n_tok, LANES), lambda i: (i // steps_per_block, 0, 0),
                               pipeline_mode=pl.Buffered(1)),
                  pl.BlockSpec((2, 2 * pp, kw // 2, kw // 2), lambda i: (0, i, 0, 0)),
                  pl.BlockSpec((2, pp, kw, 2 * sw), lambda i: (0, i, 0, 0)),
                  pl.BlockSpec((2, pp, 2 * sw, kw), lambda i: (0, i, 0, 0)),
                  pl.BlockSpec((2, None, 2, pp * sw), lambda i: (0, i, 0, 0))],
        out_specs=pl.BlockSpec((pp, lat_rows, kw), lambda i: (i, 0, 0)),
        scratch_shapes=[pltpu.VMEM((pp, n_rows, kw), BF16),
                        pltpu.VMEM((2, n_rows, pp * sw), F32), pltpu.VMEM((2, n_rows, pp * sw), F32),
                        pltpu.VMEM((2, n_rows, 2 * pp * sw), BF16)],
        compiler_params=_cparams(("arbitrary",)),
        name="s5_chunked",
    )(h, toep, win, wout, lamq)


def _s5c_params(lam_re, lam_im, log_step, b_re, b_im, c_re, c_im):
    hp = lax.Precision.HIGHEST
    q = S5_Q
    _, g, p = lam_re.shape
    ni = b_re.shape[-1]
    step = jnp.exp(log_step)[..., None]
    ar, ai = lam_re * step, lam_im * step
    tau = jnp.arange(q + 1, dtype=F32)[:, None, None, None]
    mag = jnp.exp(tau * ar)
    pr, pi = mag * jnp.cos(tau * ai), mag * jnp.sin(tau * ai)
    den = lam_re * lam_re + lam_im * lam_im
    f_re = ((pr[1] - 1.0) * lam_re + pi[1] * lam_im) / den
    f_im = (pi[1] * lam_re - (pr[1] - 1.0) * lam_im) / den
    bb_re = f_re[..., None] * b_re - f_im[..., None] * b_im
    bb_im = f_re[..., None] * b_im + f_im[..., None] * b_re
    cp_re = c_re[None] * pr[:, :, :, None, :] - c_im[None] * pi[:, :, :, None, :]
    cp_im = c_re[None] * pi[:, :, :, None, :] + c_im[None] * pr[:, :, :, None, :]
    taps = (jnp.einsum("tdgop,dgpi->tdgoi", cp_re[:q], bb_re, precision=hp)
            - jnp.einsum("tdgop,dgpi->tdgoi", cp_im[:q], bb_im, precision=hp))

    def toeplitz(dr):
        kd = taps[:, dr]
        zq = jnp.zeros_like(kd)
        if dr == 0:
            ext = jnp.concatenate([zq, kd], axis=0)
            k = jnp.stack([ext[q - s:2 * q - s] for s in range(q)])
        else:
            ext = jnp.concatenate([kd[::-1], zq], axis=0)
            k = jnp.stack([ext[q - 1 - s:2 * q - 1 - s] for s in range(q)])
        return k.transpose(2, 0, 4, 1, 3).reshape(g, q * ni, q * ni)

    toep = jnp.stack([toeplitz(0), toeplitz(1)]).astype(BF16)

    def state_in(dr):
        er, ei = (pr[:q, dr][::-1], pi[:q, dr][::-1]) if dr == 0 else (pr[:q, dr], pi[:q, dr])
        br, bi = bb_re[dr].transpose(0, 2, 1), bb_im[dr].transpose(0, 2, 1)
        w_re = er[:, :, None, :] * br[None] - ei[:, :, None, :] * bi[None]
        w_im = er[:, :, None, :] * bi[None] + ei[:, :, None, :] * br[None]
        fl = lambda a: a.transpose(1, 0, 2, 3).reshape(g, q * ni, p)
        return fl(w_re), fl(w_im)

    def state_out(dr):
        sel = (lambda a: a[1:q + 1, dr]) if dr == 0 else (lambda a: a[q:0:-1, dr])
        fl = lambda a: a.transpose(1, 3, 0, 2).reshape(g, p, q * ni)
        return fl(sel(cp_re)), fl(-sel(cp_im))

    z = lambda *shape: jnp.zeros(shape, F32)

    def pair_in(dr):
        w_re, w_im = state_in(dr)
        a_re, b_re_, a_im, b_im_ = w_re[0::2], w_re[1::2], w_im[0::2], w_im[1::2]
        zz = z(g // 2, q * ni, p)
        top = jnp.concatenate([a_re, zz, a_im, zz], axis=-1)
        bot = jnp.concatenate([zz, b_re_, zz, b_im_], axis=-1)
        return jnp.concatenate([top, bot], axis=1)

    def pair_out(dr):
        w_re, w_im = state_out(dr)
        zz = z(g // 2, p, q * ni)
        rows = [jnp.concatenate([w_re[0::2], zz], axis=-1), jnp.concatenate([zz, w_re[1::2]], axis=-1),
                jnp.concatenate([w_im[0::2], zz], axis=-1), jnp.concatenate([zz, w_im[1::2]], axis=-1)]
        return jnp.concatenate(rows, axis=1)

    win = jnp.stack([pair_in(0), pair_in(1)]).astype(BF16)
    wout = jnp.stack([pair_out(0), pair_out(1)]).astype(BF16)
    ng = S5_STEP_GROUPS
    lamq = jnp.stack([pr[q].reshape(2, g // ng, ng * p), pi[q].reshape(2, g // ng, ng * p)], axis=2)
    return toep, win, wout, lamq


def _rope_tables(n_tokens):
    rows = n_tokens // GRID_W
    row = jnp.repeat(jnp.arange(rows), GRID_W).astype(F32)
    col = jnp.tile(jnp.arange(GRID_W), rows).astype(F32)
    n_freq = ROPE_DIM // 4
    inv_freq = ROPE_BASE ** (-jnp.arange(n_freq, dtype=F32) / n_freq)
    ang = jnp.concatenate([row[:, None] * inv_freq, col[:, None] * inv_freq], axis=-1)
    cos, sin = jnp.cos(ang), jnp.sin(ang)
    z = jnp.zeros((n_tokens, 128 - ROPE_DIM), F32)
    return (jnp.concatenate([cos, cos, z], axis=-1), jnp.concatenate([-sin, sin, z], axis=-1))


def _router_halves(w_router):
    wt = w_router.T
    hi = wt.astype(BF16)
    lo = (wt - hi.astype(F32)).astype(BF16)
    return jnp.concatenate([hi, lo], axis=0)


def _split_pairs(w):
    ev, od = w[..., 0::2], w[..., 1::2]
    return jnp.concatenate([ev, od], axis=-1), jnp.concatenate([od, ev], axis=-1)


def _mla_weights(w_dqkv, w_uq, w_ukv):
    kp, kps = _split_pairs(w_dqkv[:, Q_LORA + KV_LORA:])
    wd = jnp.concatenate([w_dqkv[:, :Q_LORA + KV_LORA], kp, kps], axis=-1).astype(BF16)
    wq3 = w_uq.reshape(Q_LORA, MLA_HEADS, NOPE_DIM + ROPE_DIM)
    qp, qps = _split_pairs(wq3[:, :, NOPE_DIM:])
    wq = jnp.concatenate([wq3[:, :, :NOPE_DIM].reshape(Q_LORA, -1), qp.reshape(Q_LORA, -1),
                          qps.reshape(Q_LORA, -1)], axis=-1).astype(BF16)
    wkv3 = w_ukv.reshape(KV_LORA, MLA_HEADS, NOPE_DIM + V_DIM)
    wkv = jnp.concatenate([wkv3[:, :, :NOPE_DIM].reshape(KV_LORA, -1),
                           wkv3[:, :, NOPE_DIM:].reshape(KV_LORA, -1)], axis=-1).astype(BF16)
    return wd, wq, wkv


@jax.jit
def kernel(x, c, ctx, c_ctx, ada_w, ada_b, norm_g, mla_w_dqkv, mla_g_q, mla_g_kv, mla_w_uq, mla_w_ukv, mla_w_o, s5_lam_re, s5_lam_im, s5_log_step, s5_b_re, s5_b_im, s5_c_re, s5_c_im, s5_d, s5_w_glu, s5_b_glu, moe_w_router, moe_bias, moe_w_gate, moe_w_up, moe_w_down, sh_w_gate, sh_w_up, sh_w_down):
    b, l, d = x.shape
    n_ctx = ctx.shape[1]
    assert ada_w.shape[0] == 2 and b % 8 == 0
    ta = 256
    tm = 512
    tb = 512
    row = lambda v: v.reshape(1, -1)

    n_rows = (b + 1 + 7) // 8 * 8
    cvec = jnp.zeros((n_rows, d), F32).at[:b].set(c).at[b].set(c_ctx)
    mods = _ada_mods(cvec, ada_w, ada_b)

    def shared_weights(i):
        shgu = jnp.concatenate([sh_w_gate[i], sh_w_up[i]], axis=-1).astype(BF16)
        return shgu, sh_w_down[i].astype(BF16)

    mod_lat = mods[0, :b].reshape(b, 1, N_MOD * d)
    mod_ctx = mods[0, b].reshape(1, 1, N_MOD * d)
    wd, wq, wkv = _mla_weights(mla_w_dqkv[0], mla_w_uq[0], mla_w_ukv[0])
    cos_l, sin_l = _rope_tables(l)
    cos_c = jnp.concatenate([jnp.ones((n_ctx, ROPE_DIM), F32), jnp.zeros((n_ctx, 128 - ROPE_DIM), F32)], -1)
    sin_c = jnp.zeros((n_ctx, 128), F32)
    pre = functools.partial(_pre_mla, g0=row(norm_g[0, 0]), wd=wd, gq=row(mla_g_q[0]), gkv=row(mla_g_kv[0]),
                            wq=wq, wkv=wkv, tm=ta)
    q_c, k_c, v_c = pre(ctx, mod_ctx, cos_t=cos_c, sin_t=sin_c)
    q_l, k_l, v_l = pre(x, mod_lat, cos_t=cos_l, sin_t=sin_l)
    o_l = _attention(q_l, [k_c, k_l], [v_c, v_l], 2 * ta)
    o_c = _attention(q_c, [k_c], [v_c], n_ctx)

    wo = mla_w_o[0].astype(BF16)
    wr_t = _router_halves(moe_w_router[0])
    g1, g2, g3 = row(norm_g[0, 1]), row(norm_g[0, 2]), row(norm_g[0, 3])
    post = functools.partial(_post_mixer, _post_proj_kernel, consts=[wo], g1=g1, g2=g2, wr_t=wr_t, tm=tm,
                             name="post_mla")
    o_spec = pl.BlockSpec((tm, o_l.shape[-1]), lambda i: (i, 0))
    n_moe = b * (n_ctx + l)
    x1_c, fin, lg = post([(o_c.reshape(b * n_ctx, -1), o_spec)], x=ctx.reshape(b * n_ctx, d),
                         n_tok=b * n_ctx, x_off=0, mods=mod_ctx, rows_per_mod=b * n_ctx, moe_total=n_moe)
    tt = tm // b
    mod_lat_tm = mods[0, :b][None]
    x1_l, fin, lg = _post_mixer(
        _post_proj_tm_kernel, [(o_l, pl.BlockSpec((b, tt, o_l.shape[-1]), lambda i: (0, i, 0)))], [wo],
        x=x, n_tok=b * l, x_off=0, mods=mod_lat_tm, g1=g1, g2=g2, wr_t=wr_t, tm=tm, rows_per_mod=b * l,
        name="post_mla", moe_total=n_moe, moe_off=b * n_ctx, prev=(fin, lg),
        x_spec=pl.BlockSpec((b, tt, d), lambda i: (0, i, 0)))
    shgu, shd = shared_weights(0)
    lat_a = (l // 2) * b
    n_a = b * n_ctx + lat_a
    moe = functools.partial(_moe, fin, lg, moe_bias[0], moe_w_gate, moe_w_up, moe_w_down, 0, tb)
    yk_a, gates_a = moe(0, n_a)
    yk_b, gates_b = moe(n_a, n_moe - n_a)
    comb = functools.partial(_combine, fin=fin, shgu=shgu, shd=shd, g3=g3, tm=tm)
    x2_c = comb(yk_a, gates_a, x1=x1_c, mods=mod_ctx, rows_per_mod=b * n_ctx, n_tok=b * n_ctx, x_off=0, yk_off=0,
                fin_off=0)

    n_all = n_ctx + l
    mod_lat = mods[1, :b]
    mod_ctx = jnp.broadcast_to(mods[1, b][None], (b, N_MOD * d))
    g0 = row(norm_g[1, 0])
    h, xt = _pre_s5(x2_c.reshape(b, n_ctx, d), mod_ctx, g0, n_all, 0, None, tt)
    comb_l = functools.partial(comb, x1=x1_l, mods=mod_lat_tm, rows_per_mod=b * l, out_rows=n_all * b,
                               nxt=(mod_lat, g0))
    xt, h = comb_l(yk_a, gates_a, n_tok=lat_a, x_off=0, yk_off=b * n_ctx, fin_off=b * n_ctx,
                   out_off=n_ctx * b, prev=(xt, h))
    xt, h = comb_l(yk_b, gates_b, n_tok=b * l - lat_a, x_off=lat_a, yk_off=0, fin_off=n_a,
                   out_off=n_ctx * b + lat_a, prev=(xt, h))
    toep, win, wout, lamq = _s5c_params(s5_lam_re[0], s5_lam_im[0], s5_log_step[0], s5_b_re[0], s5_b_im[0],
                                        s5_c_re[0], s5_c_im[0])
    ng = d // LANES
    yc = _s5_chunked(h, toep, win, wout, lamq, b, n_ctx)
    g1, g2, g3 = row(norm_g[1, 1]), row(norm_g[1, 2]), row(norm_g[1, 3])
    lat0 = n_ctx * b // tm
    x1, fin, lg = _post_mixer(
        _post_glu_kernel,
        [(h, pl.BlockSpec((ng, tm, LANES), lambda i: (0, i + lat0, 0))),
         (yc, pl.BlockSpec((yc.shape[0], tm // S5_Q, yc.shape[-1]), lambda i: (0, i, 0)))],
        [row(s5_d[0]), s5_w_glu[0].astype(BF16), row(s5_b_glu[0])],
        x=xt, n_tok=l * b, x_off=n_ctx * b, mods=mod_lat[None], g1=g1, g2=g2, wr_t=_router_halves(moe_w_router[1]), tm=tm,
        rows_per_mod=l * b, name="post_s5")
    shgu, shd = shared_weights(1)
    n_h = (l // 2) * b
    moe = functools.partial(_moe, fin, lg, moe_bias[1], moe_w_gate, moe_w_up, moe_w_down, 1, tb)
    yk_a, gates_a = moe(0, n_h)
    yk_b, gates_b = moe(n_h, l * b - n_h)
    comb = functools.partial(_combine, fin=fin, shgu=shgu, shd=shd, x1=x1, mods=mod_lat[None], g3=g3, tm=tm,
                             rows_per_mod=l * b, yk_off=0, batch_out=b)
    out = comb(yk_a, gates_a, n_tok=n_h, x_off=0, fin_off=0)
    return comb(yk_b, gates_b, n_tok=l * b - n_h, x_off=n_h, fin_off=n_h, prev=out)
```

```python
import functools

import jax
import jax.numpy as jnp
from jax import lax
from jax.experimental import pallas as pl
from jax.experimental.pallas import tpu as pltpu
from jax.experimental.pallas import tpu_sc as plsc

F32 = jnp.float32
BF16 = jnp.bfloat16
U32 = jnp.uint32

N_MOD = 6
NORM_EPS = 1e-6
LOG2_E = 1.4426950408889634
GRID_W = 64
MLA_HEADS = 8
Q_LORA = 384
KV_LORA = 256
NOPE_DIM = 128
ROPE_DIM = 64
V_DIM = 128
V_PAD = 256
ROPE_BASE = 10000.0
QK_PAD = 256
S5_GROUP = 16
S5_STATE = 64
N_EXPERTS = 64
TOP_K = 8
N_EXPERT_GROUPS = 8
TOPK_GROUPS = 4
D_EXPERT = 256
ROUTED_SCALE = 2.5

VMEM_LIMIT = 56 * 1024 * 1024


def _cparams(sem):
    return pltpu.CompilerParams(dimension_semantics=sem, vmem_limit_bytes=VMEM_LIMIT)


def _rms(x, g):
    return x * lax.rsqrt(jnp.mean(x * x, axis=-1, keepdims=True) + NORM_EPS) * g


def _rows(v, like):
    r = v.shape[0]
    if r == 1:
        return v
    tm, d = like.shape
    return jnp.broadcast_to(v[None], (tm // r, r, d)).reshape(tm, d)


def _mod_chunk(mod_ref, j, d):
    return mod_ref[:, j * d:(j + 1) * d]


def _dot(a, b):
    return jnp.dot(a, b, preferred_element_type=F32)


PACK_ROWS = 4
LANES = 128


def _pack_store(ref, val, lead=(), row0=0):
    n = val.shape[0]
    bits = lax.bitcast_convert_type(val.astype(BF16).astype(F32), U32)
    for s in range(PACK_ROWS):
        lo = bits[:, s * LANES:(s + 1) * LANES] >> 16
        hi = bits[:, (s + PACK_ROWS) * LANES:(s + PACK_ROWS + 1) * LANES] & jnp.uint32(0xFFFF0000)
        ref[lead + (pl.ds(row0 * PACK_ROWS + s, n, stride=PACK_ROWS), slice(None))] = lo | hi


def _unpack_load(ref, n, lead=(), row0=0):
    los, his = [], []
    for s in range(PACK_ROWS):
        w = ref[lead + (pl.ds(row0 * PACK_ROWS + s, n, stride=PACK_ROWS), slice(None))]
        los.append(lax.bitcast_convert_type(w << 16, F32))
        his.append(lax.bitcast_convert_type(w & jnp.uint32(0xFFFF0000), F32))
    return los + his


def _ada_kernel(c_ref, w_ref, b_ref, o_ref):
    c = c_ref[...]
    s = c * jax.nn.sigmoid(c)
    o_ref[...] = jnp.dot(s, w_ref[...], preferred_element_type=F32,
                         precision=lax.Precision.HIGHEST) + b_ref[...]


def _ada_mods(cvec, ada_w, ada_b):
    depth, d, n = ada_w.shape
    rows = cvec.shape[0]
    tn = 1536
    return pl.pallas_call(
        _ada_kernel,
        out_shape=jax.ShapeDtypeStruct((depth, rows, n), F32),
        grid=(depth, n // tn),
        in_specs=[pl.BlockSpec((rows, d), lambda l, j: (0, 0)),
                  pl.BlockSpec((None, d, tn), lambda l, j: (l, 0, j)),
                  pl.BlockSpec((None, 1, tn), lambda l, j: (l, 0, j))],
        out_specs=pl.BlockSpec((None, rows, tn), lambda l, j: (l, 0, j)),
        compiler_params=_cparams(("arbitrary", "arbitrary")),
        name="ada_mods",
    )(cvec, ada_w, ada_b.reshape(depth, 1, n))


def _pre_mla_kernel(x_ref, mod_ref, g0_ref, wd_ref, gq_ref, gkv_ref, wq_ref, wkv_ref, cos_ref, sin_ref,
                    q_ref, k_ref, v_ref):
    d = x_ref.shape[-1]
    x = x_ref[...]
    h = _rms(x, g0_ref[...]) * (1.0 + _mod_chunk(mod_ref, 1, d)) + _mod_chunk(mod_ref, 0, d)
    a = _dot(h.astype(BF16), wd_ref[...])
    cq = _rms(a[:, :Q_LORA], gq_ref[...])
    ckv = _rms(a[:, Q_LORA:Q_LORA + KV_LORA], gkv_ref[...])
    rd = ROPE_DIM
    cos = cos_ref[:, 0:rd]
    sin = sin_ref[:, 0:rd]
    o = Q_LORA + KV_LORA
    k_rot = (a[:, o:o + rd] * cos + a[:, o + rd:o + 2 * rd] * sin).astype(BF16)
    qa = _dot(cq.astype(BF16), wq_ref[...])
    kva = _dot(ckv.astype(BF16), wkv_ref[...])
    hw = MLA_HEADS * 128
    hr = MLA_HEADS * rd
    zpad = jnp.zeros((x.shape[0], QK_PAD - NOPE_DIM - rd), BF16)
    scale = (NOPE_DIM + ROPE_DIM) ** -0.5 * LOG2_E
    for hd in range(MLA_HEADS):
        lo = hd * 128
        q_rot = qa[:, hw + hd * rd:hw + (hd + 1) * rd] * cos + qa[:, hw + hr + hd * rd:hw + hr + (hd + 1) * rd] * sin
        q_ref[:, hd * QK_PAD:hd * QK_PAD + 128] = (qa[:, lo:lo + 128] * scale).astype(BF16)
        q_ref[:, hd * QK_PAD + 128:hd * QK_PAD + 128 + rd] = (q_rot * scale).astype(BF16)
        q_ref[:, hd * QK_PAD + 128 + rd:(hd + 1) * QK_PAD] = zpad
        k_ref[:, hd * QK_PAD:hd * QK_PAD + 128] = kva[:, lo:lo + 128].astype(BF16)
        k_ref[:, hd * QK_PAD + 128:hd * QK_PAD + 128 + rd] = k_rot
        k_ref[:, hd * QK_PAD + 128 + rd:(hd + 1) * QK_PAD] = zpad
        v_ref[:, hd * V_PAD:hd * V_PAD + V_DIM] = kva[:, hw + lo:hw + lo + 128].astype(BF16)
        v_ref[:, hd * V_PAD + V_DIM:(hd + 1) * V_PAD] = jnp.ones((x.shape[0], V_PAD - V_DIM), BF16)


def _pre_mla(x, mods, g0, wd, gq, gkv, wq, wkv, cos_t, sin_t, tm):
    b, n, d = x.shape
    nb_mod = mods.shape[0]
    full = lambda a: pl.BlockSpec(a.shape, lambda i, j: (0,) * a.ndim)
    mod_map = (lambda i, j: (i, 0, 0)) if nb_mod > 1 else (lambda i, j: (0, 0, 0))
    qk_w = MLA_HEADS * QK_PAD
    v_w = MLA_HEADS * V_PAD
    return pl.pallas_call(
        _pre_mla_kernel,
        out_shape=(jax.ShapeDtypeStruct((b, n, qk_w), BF16),
                   jax.ShapeDtypeStruct((b, n, qk_w), BF16),
                   jax.ShapeDtypeStruct((b, n, v_w), BF16)),
        grid=(b, n // tm),
        in_specs=[pl.BlockSpec((None, tm, d), lambda i, j: (i, j, 0)),
                  pl.BlockSpec((None, 1, mods.shape[-1]), mod_map),
                  full(g0), full(wd), full(gq), full(gkv), full(wq), full(wkv),
                  pl.BlockSpec((tm, 128), lambda i, j: (j, 0)),
                  pl.BlockSpec((tm, 128), lambda i, j: (j, 0))],
        out_specs=(pl.BlockSpec((None, tm, qk_w), lambda i, j: (i, j, 0)),
                   pl.BlockSpec((None, tm, qk_w), lambda i, j: (i, j, 0)),
                   pl.BlockSpec((None, tm, v_w), lambda i, j: (i, j, 0))),
        compiler_params=_cparams(("arbitrary", "arbitrary")),
        name="pre_mla",
    )(x, mods, g0, wd, gq, gkv, wq, wkv, cos_t, sin_t)


def _attn_kernel(*refs, n_seg):
    q_ref = refs[0]
    k_refs = refs[1:1 + n_seg]
    v_refs = refs[1 + n_seg:1 + 2 * n_seg]
    o_ref = refs[1 + 2 * n_seg]
    nt = (((1,), (1,)), ((), ()))

    def scores(hd):
        q = q_ref[:, hd * QK_PAD:(hd + 1) * QK_PAD]
        return [lax.dot_general(q, k[:, hd * QK_PAD:(hd + 1) * QK_PAD], nt, preferred_element_type=F32)
                for k in k_refs]

    nxt = scores(0)
    for hd in range(MLA_HEADS):
        ss = nxt
        if hd + 1 < MLA_HEADS:
            nxt = scores(hd + 1)
        m = ss[0].max(axis=-1, keepdims=True)
        for s in ss[1:]:
            m = jnp.maximum(m, s.max(axis=-1, keepdims=True))
        acc = None
        for s, v in zip(ss, v_refs):
            pv = _dot(jnp.exp2((s - m).astype(BF16)), v[:, hd * V_PAD:(hd + 1) * V_PAD])
            acc = pv if acc is None else acc + pv
        o_ref[:, hd * V_DIM:(hd + 1) * V_DIM] = (acc[:, :V_DIM] / acc[:, V_DIM:V_DIM + 1]).astype(BF16)


def _attention(q, ks, vs, tq):
    b, nq, qk_w = q.shape
    v_w = MLA_HEADS * V_DIM
    kv_spec = lambda a: pl.BlockSpec((None,) + a.shape[1:], lambda i, j: (i, 0, 0))
    return pl.pallas_call(
        functools.partial(_attn_kernel, n_seg=len(ks)),
        out_shape=jax.ShapeDtypeStruct((b, nq, v_w), BF16),
        grid=(b, nq // tq),
        in_specs=[pl.BlockSpec((None, tq, qk_w), lambda i, j: (i, j, 0))]
                 + [kv_spec(a) for a in ks] + [kv_spec(a) for a in vs],
        out_specs=pl.BlockSpec((None, tq, v_w), lambda i, j: (i, j, 0)),
        compiler_params=_cparams(("arbitrary", "arbitrary")),
        name="mla_attention",
    )(q, *ks, *vs)


SUB_ROWS = 256


def _sub_tiles(n):
    return [slice(r, r + SUB_ROWS) for r in range(0, n, SUB_ROWS)]


def _post_core(o, x, rows, mod_ref, g1_ref, g2_ref, wr_ref, x1_ref, fin_ref, lg_ref):
    d = x.shape[-1]
    ne = lg_ref.shape[0]
    gate = _rows(_mod_chunk(mod_ref, 2, d), x)
    shift = _rows(_mod_chunk(mod_ref, 3, d), x)
    scale = _rows(_mod_chunk(mod_ref, 4, d), x)
    x1 = x + gate * _rms(o, g1_ref[...])
    fin = _rms(x1, g2_ref[...]) * (1.0 + scale) + shift
    x1_ref[rows, :] = x1
    _pack_store(fin_ref, fin, row0=rows.start)
    nt = (((1,), (1,)), ((), ()))
    f_hi = fin.astype(BF16)
    f_lo = (fin - f_hi.astype(F32)).astype(BF16)
    r_hi = lax.dot_general(wr_ref[...], f_hi, nt, preferred_element_type=F32)
    r_lo = lax.dot_general(wr_ref[0:ne, :], f_lo, nt, preferred_element_type=F32)
    lg_ref[:, rows] = r_hi[:ne] + r_hi[ne:] + r_lo


def _post_proj_kernel(o_ref, wo_ref, x_ref, mod_ref, g1_ref, g2_ref, wr_ref, *rest):
    x1_ref, fin_ref, lg_ref = rest[-3:]
    for rows in _sub_tiles(x_ref.shape[0]):
        o = _dot(o_ref[rows, :], wo_ref[...])
        _post_core(o, x_ref[rows, :], rows, mod_ref, g1_ref, g2_ref, wr_ref, x1_ref, fin_ref, lg_ref)


def _post_proj_tm_kernel(o_ref, wo_ref, x_ref, mod_ref, g1_ref, g2_ref, wr_ref, *rest):
    x1_ref, fin_ref, lg_ref = rest[-3:]
    nb, tt, d = x_ref.shape
    ts = SUB_ROWS // nb
    for t0 in range(0, tt, ts):
        o = _dot(o_ref[:, t0:t0 + ts, :].reshape(nb * ts, o_ref.shape[-1]), wo_ref[...])
        o = jnp.swapaxes(o.reshape(nb, ts, d), 0, 1).reshape(ts * nb, d)
        x = jnp.swapaxes(x_ref[:, t0:t0 + ts, :], 0, 1).reshape(ts * nb, d)
        _post_core(o, x, slice(t0 * nb, (t0 + ts) * nb), mod_ref, g1_ref, g2_ref, wr_ref, x1_ref, fin_ref, lg_ref)


def _chunk_to_rows(yc_ref, c, nb):
    q, grp = S5_Q, S5_GROUP
    per_tile = LANES // grp
    lane_grp = lax.broadcasted_iota(jnp.int32, (nb, LANES), 1) // grp
    n_pairs = yc_ref.shape[0]
    pieces = [[yc_ref[p, c * nb:(c + 1) * nb, lt * LANES:(lt + 1) * LANES] for lt in range(2 * q // per_tile)]
              for p in range(n_pairs)]
    out_rows = []
    for t in range(q):
        tiles = []
        for lb in range(2 * n_pairs // per_tile):
            dest = None
            for g8 in range(per_tile):
                g = lb * per_tile + g8
                piece = pieces[g // 2][(g % 2) * (q // per_tile) + t // per_tile]
                r = pltpu.roll(piece, ((g8 - t % per_tile) * grp) % LANES, 1)
                dest = r if dest is None else jnp.where(lane_grp == g8, r, dest)
            tiles.append(dest)
        out_rows.append(jnp.concatenate(tiles, axis=-1))
    return jnp.concatenate(out_rows, axis=0)


def _post_glu_kernel(h_ref, yc_ref, dsk_ref, wg_ref, bg_ref, x_ref, mod_ref, g1_ref, g2_ref, wr_ref,
                     *rest):
    x1_ref, fin_ref, lg_ref = rest[-3:]
    d = x_ref.shape[-1]
    nb = mod_ref.shape[0]
    assert SUB_ROWS == S5_Q * nb
    for ci, rows in enumerate(_sub_tiles(x_ref.shape[0])):
        h = jnp.concatenate([h_ref[g, rows, :] for g in range(h_ref.shape[0])], axis=-1).astype(F32)
        y = h * dsk_ref[...] + _chunk_to_rows(yc_ref, ci, nb)
        z = _dot(jax.nn.gelu(y, approximate=True).astype(BF16), wg_ref[...]) + bg_ref[...]
        o = z[:, :d] * jax.nn.sigmoid(z[:, d:])
        _post_core(o, x_ref[rows, :], rows, mod_ref, g1_ref, g2_ref, wr_ref, x1_ref, fin_ref, lg_ref)


def _post_mixer(kernel, tok_inputs, consts, x, n_tok, x_off, mods, g1, g2, wr_t, tm, rows_per_mod, name,
                moe_total=None, moe_off=0, prev=None, x_spec=None):
    d = x.shape[-1]
    ne = wr_t.shape[0] // 2
    moe_total = n_tok if moe_total is None else moe_total
    tiles_per_mod = rows_per_mod // tm
    xo, mo = x_off // tm, moe_off // tm
    full = lambda a: pl.BlockSpec(a.shape, lambda i: (0,) * a.ndim)
    tile = pl.BlockSpec((tm, d), lambda i: (i, 0))
    mod_spec = pl.BlockSpec((None,) + mods.shape[1:], lambda i: (i // tiles_per_mod, 0, 0))
    x_spec = pl.BlockSpec((tm, d), lambda i: (i + xo, 0)) if x_spec is None else x_spec
    in_specs = ([spec for _, spec in tok_inputs] + [full(a) for a in consts]
                + [x_spec, mod_spec, full(g1), full(g2), full(wr_t)])
    args = [a for a, _ in tok_inputs] + list(consts) + [x, mods, g1, g2, wr_t]
    aliases = {}
    if prev is not None:
        aliases = {len(args): 1, len(args) + 1: 2}
        in_specs += [pl.BlockSpec(memory_space=pl.ANY)] * 2
        args += list(prev)
    return pl.pallas_call(
        kernel,
        out_shape=(jax.ShapeDtypeStruct((n_tok, d), F32),
                   jax.ShapeDtypeStruct((moe_total * PACK_ROWS, LANES), U32),
                   jax.ShapeDtypeStruct((ne, moe_total), F32)),
        grid=(n_tok // tm,),
        in_specs=in_specs,
        out_specs=(tile, pl.BlockSpec((tm * PACK_ROWS, LANES), lambda i: (i + mo, 0)),
                   pl.BlockSpec((ne, tm), lambda i: (0, i + mo))),
        input_output_aliases=aliases,
        compiler_params=_cparams(("arbitrary",)),
        name=name,
    )(*args)


def _route_kernel(lg_ref, bias_ref, eidx_ref, gate_ref, rank_ref, cnt_ref, tri_ref, base_ref):
    i = pl.program_id(0)
    ne, tt = lg_ref.shape
    gsz = ne // N_EXPERT_GROUPS
    shp = (N_EXPERT_GROUPS, gsz, tt)
    neg = -jnp.inf

    @pl.when(i == 0)
    def _():
        base_ref[...] = jnp.zeros_like(base_ref)
        r = lax.broadcasted_iota(jnp.int32, (tt, tt), 0)
        c = lax.broadcasted_iota(jnp.int32, (tt, tt), 1)
        tri_ref[...] = (r < c).astype(BF16)

    scores = jax.nn.sigmoid(lg_ref[...])
    s3 = scores.reshape(shp)
    b3 = (scores + bias_ref[...]).reshape(shp)
    io_e = lax.broadcasted_iota(jnp.int32, shp, 1)
    io_g = lax.broadcasted_iota(jnp.int32, shp, 0)
    io_flat = io_g * gsz + io_e
    m1 = b3.max(axis=1, keepdims=True)
    i1 = jnp.where(b3 == m1, io_e, gsz).min(axis=1, keepdims=True)
    m2 = jnp.where(io_e == i1, neg, b3).max(axis=1, keepdims=True)
    cur = jnp.broadcast_to(m1 + m2, shp)
    gsel = jnp.zeros(shp, jnp.bool_)
    for _ in range(TOPK_GROUPS):
        m = cur.max(axis=0, keepdims=True)
        gi = jnp.where(cur == m, io_g, N_EXPERT_GROUPS).min(axis=0, keepdims=True)
        hit = io_g == gi
        gsel = jnp.logical_or(gsel, hit)
        cur = jnp.where(hit, neg, cur)
    cand = jnp.where(gsel, b3, neg)
    sel = jnp.zeros(shp, jnp.bool_)
    eids, gts = [], []
    for _ in range(TOP_K):
        m = cand.max(axis=0, keepdims=True).max(axis=1, keepdims=True)
        ei = jnp.where(cand == m, io_flat, ne).min(axis=0, keepdims=True).min(axis=1, keepdims=True)
        hit = io_flat == ei
        gts.append(jnp.where(hit, s3, 0.0).sum(axis=0, keepdims=True).sum(axis=1, keepdims=True))
        eids.append(ei)
        sel = jnp.logical_or(sel, hit)
        cand = jnp.where(hit, neg, cand)
    gsum = gts[0]
    for g in gts[1:]:
        gsum = gsum + g
    self32 = sel.astype(F32).reshape(ne, tt)
    cnt = _dot(self32.astype(BF16), tri_ref[...]) + base_ref[...]
    cnt3 = cnt.reshape(shp)
    for k in range(TOP_K):
        hit = io_flat == eids[k]
        rk = jnp.where(hit, cnt3, 0.0).sum(axis=0, keepdims=True).sum(axis=1, keepdims=True)
        rank_ref[k:k + 1, :] = rk.reshape(1, tt).astype(jnp.int32)
        eidx_ref[k:k + 1, :] = eids[k].reshape(1, tt)
        gate_ref[k:k + 1, :] = (gts[k] / gsum * ROUTED_SCALE).reshape(1, tt)
    base_new = base_ref[...] + self32.sum(axis=1, keepdims=True)
    base_ref[...] = base_new
    cnt_ref[...] = jnp.broadcast_to(base_new, cnt_ref.shape)


def _route(logits_t, bias, tt, tok0, t):
    ne = logits_t.shape[0]
    off = tok0 // tt
    out_i = jax.ShapeDtypeStruct((TOP_K, t), jnp.int32)
    row = pl.BlockSpec((TOP_K, tt), lambda i: (0, i))
    return pl.pallas_call(
        _route_kernel,
        out_shape=(out_i, jax.ShapeDtypeStruct((TOP_K, t), F32), out_i,
                   jax.ShapeDtypeStruct((ne, 128), F32)),
        grid=(t // tt,),
        in_specs=[pl.BlockSpec((ne, tt), lambda i: (0, i + off)),
                  pl.BlockSpec((ne, 1), lambda i: (0, 0))],
        out_specs=(row, row, row, pl.BlockSpec((ne, 128), lambda i: (0, 0))),
        scratch_shapes=[pltpu.VMEM((tt, tt), BF16), pltpu.VMEM((ne, 1), F32)],
        compiler_params=_cparams(("arbitrary",)),
        name="moe_route",
    )(logits_t, bias.reshape(ne, 1))


def _dest_kernel(eidx_ref, rank_ref, start_ref, dest_ref):
    kk, tt = eidx_ref.shape
    ne = start_ref.shape[0]
    n_chunk, _, r = dest_ref.shape
    io_e = lax.broadcasted_iota(jnp.int32, (ne, tt), 0)
    start = start_ref[...]
    for k in range(kk):
        hit = io_e == eidx_ref[k:k + 1, :]
        dk = jnp.where(hit, start, 0).sum(axis=0, keepdims=True) + rank_ref[k:k + 1, :]
        for c in range(n_chunk):
            dest_ref[c, k:k + 1, :] = dk[:, c * r:(c + 1) * r]


def _dest_rows(eidx_t, rank_t, start, tt, r):
    kk, t = eidx_t.shape
    ne = start.shape[0]
    return pl.pallas_call(
        _dest_kernel,
        out_shape=jax.ShapeDtypeStruct((t // r, kk, r), jnp.int32),
        grid=(t // tt,),
        in_specs=[pl.BlockSpec((kk, tt), lambda i: (0, i)),
                  pl.BlockSpec((kk, tt), lambda i: (0, i)),
                  pl.BlockSpec((ne, 1), lambda i: (0, 0))],
        out_specs=pl.BlockSpec((tt // r, kk, r), lambda i: (i, 0, 0)),
        compiler_params=_cparams(("arbitrary",)),
        name="moe_dest",
    )(eidx_t, rank_t, start.reshape(ne, 1))


SC_CHUNK = 64


def _sc_mesh():
    return plsc.VectorSubcoreMesh(core_axis_name="c", subcore_axis_name="s")


def _sc_workers():
    info = plsc.get_sparse_core_info()
    return info.num_cores, info.num_cores * info.num_subcores


def _sc_scatter_rows(rows, dest, n_out, row0=0):
    n_chunk, kk, r = dest.shape
    nc, nw = _sc_workers()
    cpw = n_chunk // nw
    assert cpw * nw == n_chunk and cpw % 2 == 0 and row0 % r == 0 and row0 + n_chunk * r <= rows.shape[0]

    @functools.partial(
        pl.kernel, mesh=_sc_mesh(),
        out_type=jax.ShapeDtypeStruct((n_out,) + rows.shape[1:], rows.dtype),
        scratch_types=[pltpu.VMEM((2, kk, r), jnp.int32), pltpu.VMEM((2, r) + rows.shape[1:], rows.dtype),
                       pltpu.SemaphoreType.DMA((2,)), pltpu.SemaphoreType.DMA((2,))])
    def scatter(rows_hbm, dest_hbm, out_hbm, idx_v, rows_v, load_sem, scat_sem):
        c0 = (lax.axis_index("s") * nc + lax.axis_index("c")) * cpw

        def loads(c, b):
            return (pltpu.make_async_copy(dest_hbm.at[c], idx_v.at[b], load_sem.at[b]),
                    pltpu.make_async_copy(rows_hbm.at[pl.ds(row0 + c * r, r)], rows_v.at[b], load_sem.at[b]))

        def scat(b, k):
            return pltpu.make_async_copy(rows_v.at[b], out_hbm.at[idx_v.at[b, k]], scat_sem.at[b])

        for cp in loads(c0, 0):
            cp.start()

        @pl.loop(0, cpw, step=2)
        def _(ci):
            for b in range(2):
                c = c0 + ci + b
                for cp in loads(c, b):
                    cp.wait()
                for k in range(kk):
                    scat(b, k).start()

                @pl.when(ci + b >= 1)
                def _():
                    for k in range(kk):
                        scat(1 - b, k).wait()

                @pl.when(ci + b + 1 < cpw)
                def _():
                    for cp in loads(c + 1, 1 - b):
                        cp.start()

        for k in range(kk):
            scat((cpw - 1) % 2, k).wait()

    return scatter(rows, dest)


def _sc_gather_rows(src, dest):
    n_chunk, kk, r = dest.shape
    t = n_chunk * r
    nc, nw = _sc_workers()
    cpw = n_chunk // nw
    nbuf = 3
    assert cpw * nw == n_chunk and kk > nbuf

    @functools.partial(
        pl.kernel, mesh=_sc_mesh(),
        out_type=jax.ShapeDtypeStruct((kk, t) + src.shape[1:], src.dtype),
        scratch_types=[pltpu.VMEM((kk, r), jnp.int32), pltpu.VMEM((nbuf, r) + src.shape[1:], src.dtype),
                       pltpu.SemaphoreType.DMA((nbuf,)), pltpu.SemaphoreType.DMA((nbuf,))])
    def gather(src_hbm, dest_hbm, out_hbm, idx_v, rows_v, get_sem, put_sem):
        c0 = (lax.axis_index("s") * nc + lax.axis_index("c")) * cpw

        @pl.loop(0, cpw)
        def _(ci):
            c = c0 + ci
            pltpu.sync_copy(dest_hbm.at[c], idx_v)

            def get(k):
                return pltpu.make_async_copy(src_hbm.at[idx_v.at[k]], rows_v.at[k % nbuf], get_sem.at[k % nbuf])

            def put(k):
                return pltpu.make_async_copy(rows_v.at[k % nbuf], out_hbm.at[k, pl.ds(c * r, r)],
                                             put_sem.at[k % nbuf])

            for k in range(nbuf - 1):
                get(k).start()
            for k in range(kk):
                get(k).wait()
                put(k).start()
                if k + nbuf - 1 < kk:
                    if k >= 1:
                        put(k - 1).wait()
                    get(k + nbuf - 1).start()
            for k in range(kk - nbuf, kk):
                put(k).wait()

    return gather(src, dest)


def _expert_kernel(be_ref, nu_ref, x_ref, wg_ref, wu_ref, wd_ref, o_ref, wgu_s, wd_s):
    i = pl.program_id(0)
    tb = o_ref.shape[0] // PACK_ROWS

    @pl.when(i < nu_ref[0])
    def _():
        @pl.when(jnp.logical_or(i == 0, be_ref[i] != be_ref[jnp.maximum(i - 1, 0)]))
        def _():
            wgu_s[:, :D_EXPERT] = wg_ref[...].astype(BF16)
            wgu_s[:, D_EXPERT:] = wu_ref[...].astype(BF16)
            wd_s[...] = wd_ref[...].astype(BF16)

        x = jnp.concatenate([v.astype(BF16) for v in _unpack_load(x_ref, tb)], axis=-1)
        gu = _dot(x, wgu_s[...])
        g = gu[:, :D_EXPERT]
        h = g * jax.nn.sigmoid(g) * gu[:, D_EXPERT:]
        _pack_store(o_ref, _dot(h.astype(BF16), wd_s[...]))


def _experts(xs, blk_e, n_used, w_gate, w_up, w_down, layer, tb):
    rows = xs.shape[0] // PACK_ROWS
    _, ne, d, de = w_gate.shape
    nb = rows // tb
    row_map = lambda i, be, nu: (jnp.minimum(i, nu[0] - 1), 0)
    w_map = lambda i, be, nu: (layer, be[i], 0, 0)
    grid_spec = pltpu.PrefetchScalarGridSpec(
        num_scalar_prefetch=2,
        grid=(nb,),
        in_specs=[pl.BlockSpec((tb * PACK_ROWS, LANES), row_map),
                  pl.BlockSpec((None, None, d, de), w_map),
                  pl.BlockSpec((None, None, d, de), w_map),
                  pl.BlockSpec((None, None, de, d), w_map)],
        out_specs=pl.BlockSpec((tb * PACK_ROWS, LANES), row_map),
        scratch_shapes=[pltpu.VMEM((d, 2 * de), BF16), pltpu.VMEM((de, d), BF16)],
    )
    return pl.pallas_call(
        _expert_kernel,
        out_shape=jax.ShapeDtypeStruct(xs.shape, U32),
        grid_spec=grid_spec,
        compiler_params=_cparams(("arbitrary",)),
        name="moe_experts",
    )(blk_e, n_used, xs, w_gate, w_up, w_down)


def _combine_kernel(yk_ref, gate_ref, fin_ref, shgu_ref, shd_ref, x1_ref, mod_ref, g3_ref, *rest, fuse_next):
    if fuse_next:
        nmod_ref, ng0_ref = rest[0], rest[1]
        o_ref, h_ref = rest[-2], rest[-1]
    else:
        o_ref = rest[-1]
    tm, d = x1_ref.shape
    for rows in _sub_tiles(tm):
        n, r0 = SUB_ROWS, rows.start
        gates = gate_ref[rows, :]
        blocks = None
        for k in range(TOP_K):
            gk = gates[:, k:k + 1]
            terms = [gk * v for v in _unpack_load(yk_ref, n, lead=(k,), row0=r0)]
            blocks = terms if blocks is None else [a + b for a, b in zip(blocks, terms)]
        fin = jnp.concatenate([v.astype(BF16) for v in _unpack_load(fin_ref, n, row0=r0)], axis=-1)
        gu = _dot(fin, shgu_ref[...])
        g = gu[:, :D_EXPERT]
        hsh = g * jax.nn.sigmoid(g) * gu[:, D_EXPERT:]
        f = jnp.concatenate(blocks, axis=-1) + _dot(hsh.astype(BF16), shd_ref[...])
        x1 = x1_ref[rows, :]
        x2 = x1 + _rows(_mod_chunk(mod_ref, 5, d), x1) * _rms(f, g3_ref[...])
        if len(o_ref.shape) == 2:
            o_ref[rows, :] = x2
        else:
            nb = o_ref.shape[0]
            ts = SUB_ROWS // nb
            o_ref[:, r0 // nb:r0 // nb + ts, :] = jnp.swapaxes(x2.reshape(ts, nb, d), 0, 1)
        if fuse_next:
            hn = (_rms(x2, ng0_ref[...]) * (1.0 + _rows(_mod_chunk(nmod_ref, 1, d), x2))
                  + _rows(_mod_chunk(nmod_ref, 0, d), x2))
            for gi in range(h_ref.shape[0]):
                h_ref[gi, rows, :] = hn[:, gi * LANES:(gi + 1) * LANES].astype(BF16)


def _combine(yk, gates, fin, shgu, shd, x1, mods, g3, tm, rows_per_mod, n_tok, x_off, yk_off, fin_off,
             batch_out=0, prev=None, out_rows=None, out_off=None, nxt=None):
    t, d = x1.shape
    out_rows = t if out_rows is None else out_rows
    out_off = x_off if out_off is None else out_off
    xo, yo, fo, oo = x_off // tm, yk_off // tm, fin_off // tm, out_off // tm
    tiles_per_mod = rows_per_mod // tm
    full = lambda a: pl.BlockSpec(a.shape, lambda i: (0,) * a.ndim)
    if batch_out:
        out_shape = [jax.ShapeDtypeStruct((batch_out, out_rows // batch_out, d), F32)]
        out_specs = [pl.BlockSpec((batch_out, tm // batch_out, d), lambda i: (0, i + oo, 0))]
    else:
        out_shape = [jax.ShapeDtypeStruct((out_rows, d), F32)]
        out_specs = [pl.BlockSpec((tm, d), lambda i: (i + oo, 0))]
    in_specs = [pl.BlockSpec((TOP_K, tm * PACK_ROWS, LANES), lambda i: (0, i + yo, 0)),
                pl.BlockSpec((tm, TOP_K), lambda i: (i + yo, 0)),
                pl.BlockSpec((tm * PACK_ROWS, LANES), lambda i: (i + fo, 0)),
                full(shgu), full(shd),
                pl.BlockSpec((tm, d), lambda i: (i + xo, 0)),
                pl.BlockSpec((None,) + mods.shape[1:], lambda i: ((i + xo) // tiles_per_mod, 0, 0)),
                full(g3)]
    args = [yk, gates, fin, shgu, shd, x1, mods, g3]
    if nxt is not None:
        in_specs += [full(nxt[0]), full(nxt[1])]
        args += list(nxt)
        out_shape.append(jax.ShapeDtypeStruct((d // LANES, out_rows, LANES), BF16))
        out_specs.append(pl.BlockSpec((d // LANES, tm, LANES), lambda i: (0, i + oo, 0)))
    aliases = {}
    if prev is not None:
        for j, p in enumerate(prev if isinstance(prev, (tuple, list)) else [prev]):
            in_specs.append(pl.BlockSpec(memory_space=pl.ANY))
            aliases[len(args)] = j
            args.append(p)
    out = pl.pallas_call(
        functools.partial(_combine_kernel, fuse_next=nxt is not None),
        out_shape=tuple(out_shape),
        grid=(n_tok // tm,),
        in_specs=in_specs,
        out_specs=tuple(out_specs),
        input_output_aliases=aliases,
        compiler_params=_cparams(("arbitrary",)),
        name="moe_combine",
    )(*args)
    return out if nxt is not None else out[0]


def _moe(fin, logits_t, bias, w_gate, w_up, w_down, layer, tb, tok0, t):
    t_all = fin.shape[0] // PACK_ROWS
    ne = w_gate.shape[1]
    tt = 512
    eidx_t, gates_t, rank_t, cnt = _route(logits_t, bias, tt, tok0, t)
    counts = cnt[:, 0].astype(jnp.int32)
    padded = (counts + tb - 1) // tb * tb
    pad_end = jnp.cumsum(padded)
    pad_start = pad_end - padded
    nb = (t * TOP_K) // tb + ne
    n_used = pad_end[-1] // tb
    blk_start = jnp.arange(nb, dtype=jnp.int32) * tb
    blk = jnp.sum(pad_end[None, :] <= jnp.minimum(blk_start, pad_end[-1] - 1)[:, None], axis=1)
    blk_e = jnp.minimum(blk, ne - 1).astype(jnp.int32)
    dest = _dest_rows(eidx_t, rank_t, pad_start, tt, SC_CHUNK)
    xs = _sc_scatter_rows(fin.reshape(t_all, PACK_ROWS, LANES), dest, nb * tb, row0=tok0)
    ys = _experts(xs.reshape(nb * tb * PACK_ROWS, LANES), blk_e, n_used.reshape(1).astype(jnp.int32),
                  w_gate, w_up, w_down, layer, tb)
    yk = _sc_gather_rows(ys.reshape(nb * tb, PACK_ROWS, LANES), dest)
    return yk.reshape(TOP_K, t * PACK_ROWS, LANES), gates_t.T


def _pre_s5_kernel(x_ref, mod_ref, g0_ref, *refs):
    h_ref, xt_ref = refs[-2:]
    nb, tt, d = x_ref.shape
    x = jnp.swapaxes(x_ref[...], 0, 1).reshape(tt * nb, d)
    h = (_rms(x, g0_ref[...]) * (1.0 + _rows(_mod_chunk(mod_ref, 1, d), x))
         + _rows(_mod_chunk(mod_ref, 0, d), x))
    for g in range(h_ref.shape[0]):
        h_ref[g] = h[:, g * LANES:(g + 1) * LANES].astype(BF16)
    xt_ref[...] = x


def _pre_s5(x, mods, g0, n_total, t_off, prev, tt):
    nb, n, d = x.shape
    off = t_off // tt
    out_shape = (jax.ShapeDtypeStruct((d // LANES, n_total * nb, LANES), BF16),
                 jax.ShapeDtypeStruct((n_total * nb, d), F32))
    out_specs = (pl.BlockSpec((d // LANES, tt * nb, LANES), lambda i: (0, i + off, 0)),
                 pl.BlockSpec((tt * nb, d), lambda i: (i + off, 0)))
    in_specs = [pl.BlockSpec((nb, tt, d), lambda i: (0, i, 0)),
                pl.BlockSpec(mods.shape, lambda i: (0, 0)),
                pl.BlockSpec(g0.shape, lambda i: (0, 0))]
    args = (x, mods, g0)
    aliases = {}
    if prev is not None:
        in_specs += [pl.BlockSpec(memory_space=pl.ANY)] * 2
        args += tuple(prev)
        aliases = {3: 0, 4: 1}
    return pl.pallas_call(
        _pre_s5_kernel,
        out_shape=out_shape,
        grid=(n // tt,),
        in_specs=in_specs,
        out_specs=out_specs,
        input_output_aliases=aliases,
        compiler_params=_cparams(("arbitrary",)),
        name="pre_s5",
    )(*args)


S5_Q = 16
S5_STEP_GROUPS = 4


def _s5c_kernel(h_ref, toep_ref, win_ref, wout_ref, lamq_ref, y_ref, u_ref, sre_ref, sim_ref, xin_ref, *,
                nb, n_ctx_chunks):
    npair, n_rows, kw = u_ref.shape
    half = kw // 2
    sw = 2 * S5_STATE
    n = n_rows // nb
    ncc = n_ctx_chunks
    per_tile = LANES // S5_GROUP
    lane_grp = lax.broadcasted_iota(jnp.int32, (nb, LANES), 1) // S5_GROUP
    steps_per_block = per_tile // S5_STEP_GROUPS

    def relayout_from(gl0):
        def relayout(c, carry):
            r0 = c * (S5_Q * nb)
            pieces = [h_ref[pl.ds(pl.multiple_of(r0 + s * nb, nb), nb), :].astype(F32) for s in range(S5_Q)]
            rows = pl.ds(pl.multiple_of(c * nb, nb), nb)
            for gq in range(S5_STEP_GROUPS):
                for j in range(S5_Q // per_tile):
                    dest = None
                    for s8 in range(per_tile):
                        r = pltpu.roll(pieces[j * per_tile + s8], ((s8 - gl0 - gq) * S5_GROUP) % LANES, 1)
                        dest = r if dest is None else jnp.where(lane_grp == s8, r, dest)
                    lo = (gq % 2) * half + j * LANES
                    u_ref[gq // 2, rows, lo:lo + LANES] = dest.astype(BF16)
            return carry

        lax.fori_loop(0, n, relayout, 0, unroll=2)

    for k in range(steps_per_block):
        pl.when(pl.program_id(0) % steps_per_block == k)(functools.partial(relayout_from, k * S5_STEP_GROUPS))
    for dr in range(2):
        for p in range(npair):
            s = _dot(u_ref[p], win_ref[dr, p])
            sre_ref[dr, :, p * sw:(p + 1) * sw] = s[:, :sw]
            sim_ref[dr, :, p * sw:(p + 1) * sw] = s[:, sw:]
    w = npair * sw
    lr = [jnp.broadcast_to(lamq_ref[dr, 0:1, :], (nb, w)) for dr in range(2)]
    li = [jnp.broadcast_to(lamq_ref[dr, 1:2, :], (nb, w)) for dr in range(2)]

    def step(j, carry):
        rev = jnp.where(j < ncc, ncc - 1 - j, n - 1 - (j - ncc))
        out = []
        for dr in range(2):
            xr, xi = carry[dr]
            c = j if dr == 0 else rev
            rows = pl.ds(pl.multiple_of(c * nb, nb), nb)
            for p in range(npair):
                xin_ref[dr, rows, 2 * p * sw:(2 * p + 1) * sw] = xr[:, p * sw:(p + 1) * sw].astype(BF16)
                xin_ref[dr, rows, (2 * p + 1) * sw:(2 * p + 2) * sw] = xi[:, p * sw:(p + 1) * sw].astype(BF16)
            nr = lr[dr] * xr - li[dr] * xi + sre_ref[dr, rows, :]
            ni = lr[dr] * xi + li[dr] * xr + sim_ref[dr, rows, :]
            out.append((nr, ni))
        return tuple(out)

    zero = jnp.zeros((nb, w), F32)
    lax.fori_loop(0, n, step, ((zero, zero), (zero, zero)), unroll=2)
    lat = slice(ncc * nb, n_rows)
    for p in range(npair):
        acc = None
        for dr in range(2):
            intra = jnp.concatenate([_dot(u_ref[p, lat, 0:half], toep_ref[dr, 2 * p]),
                                     _dot(u_ref[p, lat, half:kw], toep_ref[dr, 2 * p + 1])], axis=-1)
            term = intra + _dot(xin_ref[dr, lat, 2 * p * sw:(2 * p + 2) * sw], wout_ref[dr, p])
            acc = term if acc is None else acc + term
        y_ref[p] = acc


def _s5_chunked(h, toep, win, wout, lamq, nb, n_ctx):
    ng, n_tok, _ = h.shape
    n_groups = toep.shape[1]
    pp = S5_STEP_GROUPS // 2
    steps_per_block = (LANES // S5_GROUP) // S5_STEP_GROUPS
    n_rows = n_tok // S5_Q
    ncc = n_ctx // S5_Q
    lat_rows = n_rows - ncc * nb
    kw = 2 * S5_Q * S5_GROUP
    sw = 2 * S5_STATE
    return pl.pallas_call(
        functools.partial(_s5c_kernel, nb=nb, n_ctx_chunks=ncc),
        out_shape=jax.ShapeDtypeStruct((n_groups // 2, lat_rows, kw), F32),
        grid=(n_groups // S5_STEP_GROUPS,),
        in_specs=[pl.BlockSpec((None, n_tok, LANES), lambda i: (i // steps_per_block, 0, 0),
                               pipeline_mode=pl.Buffered(1)),
                  pl.BlockSpec((2, 2 * pp, kw // 2, kw // 2), lambda i: (0, i, 0, 0)),
                  pl.BlockSpec((2, pp, kw, 2 * sw), lambda i: (0, i, 0, 0)),
                  pl.BlockSpec((2, pp, 2 * sw, kw), lambda i: (0, i, 0, 0)),
                  pl.BlockSpec((2, None, 2, pp * sw), lambda i: (0, i, 0, 0))],
        out_specs=pl.BlockSpec((pp, lat_rows, kw), lambda i: (i, 0, 0)),
        scratch_shapes=[pltpu.VMEM((pp, n_rows, kw), BF16),
                        pltpu.VMEM((2, n_rows, pp * sw), F32), pltpu.VMEM((2, n_rows, pp * sw), F32),
                        pltpu.VMEM((2, n_rows, 2 * pp * sw), BF16)],
        compiler_params=_cparams(("arbitrary",)),
        name="s5_chunked",
    )(h, toep, win, wout, lamq)


def _s5c_params(lam_re, lam_im, log_step, b_re, b_im, c_re, c_im):
    hp = lax.Precision.HIGHEST
    q = S5_Q
    _, g, p = lam_re.shape
    ni = b_re.shape[-1]
    step = jnp.exp(log_step)[..., None]
    ar, ai = lam_re * step, lam_im * step
    tau = jnp.arange(q + 1, dtype=F32)[:, None, None, None]
    mag = jnp.exp(tau * ar)
    pr, pi = mag * jnp.cos(tau * ai), mag * jnp.sin(tau * ai)
    den = lam_re * lam_re + lam_im * lam_im
    f_re = ((pr[1] - 1.0) * lam_re + pi[1] * lam_im) / den
    f_im = (pi[1] * lam_re - (pr[1] - 1.0) * lam_im) / den
    bb_re = f_re[..., None] * b_re - f_im[..., None] * b_im
    bb_im = f_re[..., None] * b_im + f_im[..., None] * b_re
    cp_re = c_re[None] * pr[:, :, :, None, :] - c_im[None] * pi[:, :, :, None, :]
    cp_im = c_re[None] * pi[:, :, :, None, :] + c_im[None] * pr[:, :, :, None, :]
    taps = (jnp.einsum("tdgop,dgpi->tdgoi", cp_re[:q], bb_re, precision=hp)
            - jnp.einsum("tdgop,dgpi->tdgoi", cp_im[:q], bb_im, precision=hp))
    s_idx = jnp.arange(q)[:, None]
    t_idx = jnp.arange(q)[None, :]

    def toeplitz(dr):
        lag = (t_idx - s_idx) if dr == 0 else (s_idx - t_idx)
        k = taps[:, dr][jnp.clip(lag, 0, q - 1)]
        k = jnp.where((lag >= 0)[:, :, None, None, None], k, 0.0)
        return k.transpose(2, 0, 4, 1, 3).reshape(g, q * ni, q * ni)

    toep = jnp.stack([toeplitz(0), toeplitz(1)]).astype(BF16)

    def state_in(dr):
        e = (q - 1 - jnp.arange(q)) if dr == 0 else jnp.arange(q)
        er, ei = pr[e, dr], pi[e, dr]
        br, bi = bb_re[dr].transpose(0, 2, 1), bb_im[dr].transpose(0, 2, 1)
        w_re = er[:, :, None, :] * br[None] - ei[:, :, None, :] * bi[None]
        w_im = er[:, :, None, :] * bi[None] + ei[:, :, None, :] * br[None]
        fl = lambda a: a.transpose(1, 0, 2, 3).reshape(g, q * ni, p)
        return fl(w_re), fl(w_im)

    def state_out(dr):
        f = (jnp.arange(q) + 1) if dr == 0 else (q - jnp.arange(q))
        fl = lambda a: a.transpose(1, 3, 0, 2).reshape(g, p, q * ni)
        return fl(cp_re[f, dr]), fl(-cp_im[f, dr])

    z = lambda *shape: jnp.zeros(shape, F32)

    def pair_in(dr):
        w_re, w_im = state_in(dr)
        a_re, b_re_, a_im, b_im_ = w_re[0::2], w_re[1::2], w_im[0::2], w_im[1::2]
        zz = z(g // 2, q * ni, p)
        top = jnp.concatenate([a_re, zz, a_im, zz], axis=-1)
        bot = jnp.concatenate([zz, b_re_, zz, b_im_], axis=-1)
        return jnp.concatenate([top, bot], axis=1)

    def pair_out(dr):
        w_re, w_im = state_out(dr)
        zz = z(g // 2, p, q * ni)
        rows = [jnp.concatenate([w_re[0::2], zz], axis=-1), jnp.concatenate([zz, w_re[1::2]], axis=-1),
                jnp.concatenate([w_im[0::2], zz], axis=-1), jnp.concatenate([zz, w_im[1::2]], axis=-1)]
        return jnp.concatenate(rows, axis=1)

    win = jnp.stack([pair_in(0), pair_in(1)]).astype(BF16)
    wout = jnp.stack([pair_out(0), pair_out(1)]).astype(BF16)
    ng = S5_STEP_GROUPS
    lamq = jnp.stack([pr[q].reshape(2, g // ng, ng * p), pi[q].reshape(2, g // ng, ng * p)], axis=2)
    return toep, win, wout, lamq


def _rope_tables(n_tokens):
    rows = n_tokens // GRID_W
    row = jnp.repeat(jnp.arange(rows), GRID_W).astype(F32)
    col = jnp.tile(jnp.arange(GRID_W), rows).astype(F32)
    n_freq = ROPE_DIM // 4
    inv_freq = ROPE_BASE ** (-jnp.arange(n_freq, dtype=F32) / n_freq)
    ang = jnp.concatenate([row[:, None] * inv_freq, col[:, None] * inv_freq], axis=-1)
    cos, sin = jnp.cos(ang), jnp.sin(ang)
    z = jnp.zeros((n_tokens, 128 - ROPE_DIM), F32)
    return (jnp.concatenate([cos, cos, z], axis=-1), jnp.concatenate([-sin, sin, z], axis=-1))


def _router_halves(w_router):
    wt = w_router.T
    hi = wt.astype(BF16)
    lo = (wt - hi.astype(F32)).astype(BF16)
    return jnp.concatenate([hi, lo], axis=0)


def _split_pairs(w):
    ev, od = w[..., 0::2], w[..., 1::2]
    return jnp.concatenate([ev, od], axis=-1), jnp.concatenate([od, ev], axis=-1)


def _mla_weights(w_dqkv, w_uq, w_ukv):
    kp, kps = _split_pairs(w_dqkv[:, Q_LORA + KV_LORA:])
    wd = jnp.concatenate([w_dqkv[:, :Q_LORA + KV_LORA], kp, kps], axis=-1).astype(BF16)
    wq3 = w_uq.reshape(Q_LORA, MLA_HEADS, NOPE_DIM + ROPE_DIM)
    qp, qps = _split_pairs(wq3[:, :, NOPE_DIM:])
    wq = jnp.concatenate([wq3[:, :, :NOPE_DIM].reshape(Q_LORA, -1), qp.reshape(Q_LORA, -1),
                          qps.reshape(Q_LORA, -1)], axis=-1).astype(BF16)
    wkv3 = w_ukv.reshape(KV_LORA, MLA_HEADS, NOPE_DIM + V_DIM)
    wkv = jnp.concatenate([wkv3[:, :, :NOPE_DIM].reshape(KV_LORA, -1),
                           wkv3[:, :, NOPE_DIM:].reshape(KV_LORA, -1)], axis=-1).astype(BF16)
    return wd, wq, wkv


@jax.jit
def kernel(x, c, ctx, c_ctx, ada_w, ada_b, norm_g, mla_w_dqkv, mla_g_q, mla_g_kv, mla_w_uq, mla_w_ukv, mla_w_o, s5_lam_re, s5_lam_im, s5_log_step, s5_b_re, s5_b_im, s5_c_re, s5_c_im, s5_d, s5_w_glu, s5_b_glu, moe_w_router, moe_bias, moe_w_gate, moe_w_up, moe_w_down, sh_w_gate, sh_w_up, sh_w_down):
    b, l, d = x.shape
    n_ctx = ctx.shape[1]
    assert ada_w.shape[0] == 2 and b % 8 == 0
    ta = 256
    tm = 512
    tb = 512
    row = lambda v: v.reshape(1, -1)

    n_rows = (b + 1 + 7) // 8 * 8
    cvec = jnp.zeros((n_rows, d), F32).at[:b].set(c).at[b].set(c_ctx)
    mods = _ada_mods(cvec, ada_w, ada_b)

    def shared_weights(i):
        shgu = jnp.concatenate([sh_w_gate[i], sh_w_up[i]], axis=-1).astype(BF16)
        return shgu, sh_w_down[i].astype(BF16)

    mod_lat = mods[0, :b].reshape(b, 1, N_MOD * d)
    mod_ctx = mods[0, b].reshape(1, 1, N_MOD * d)
    wd, wq, wkv = _mla_weights(mla_w_dqkv[0], mla_w_uq[0], mla_w_ukv[0])
    cos_l, sin_l = _rope_tables(l)
    cos_c = jnp.concatenate([jnp.ones((n_ctx, ROPE_DIM), F32), jnp.zeros((n_ctx, 128 - ROPE_DIM), F32)], -1)
    sin_c = jnp.zeros((n_ctx, 128), F32)
    pre = functools.partial(_pre_mla, g0=row(norm_g[0, 0]), wd=wd, gq=row(mla_g_q[0]), gkv=row(mla_g_kv[0]),
                            wq=wq, wkv=wkv, tm=ta)
    q_c, k_c, v_c = pre(ctx, mod_ctx, cos_t=cos_c, sin_t=sin_c)
    q_l, k_l, v_l = pre(x, mod_lat, cos_t=cos_l, sin_t=sin_l)
    o_l = _attention(q_l, [k_c, k_l], [v_c, v_l], 2 * ta)
    o_c = _attention(q_c, [k_c], [v_c], n_ctx)

    wo = mla_w_o[0].astype(BF16)
    wr_t = _router_halves(moe_w_router[0])
    g1, g2, g3 = row(norm_g[0, 1]), row(norm_g[0, 2]), row(norm_g[0, 3])
    post = functools.partial(_post_mixer, _post_proj_kernel, consts=[wo], g1=g1, g2=g2, wr_t=wr_t, tm=tm,
                             name="post_mla")
    o_spec = pl.BlockSpec((tm, o_l.shape[-1]), lambda i: (i, 0))
    n_moe = b * (n_ctx + l)
    x1_c, fin, lg = post([(o_c.reshape(b * n_ctx, -1), o_spec)], x=ctx.reshape(b * n_ctx, d),
                         n_tok=b * n_ctx, x_off=0, mods=mod_ctx, rows_per_mod=b * n_ctx, moe_total=n_moe)
    tt = tm // b
    mod_lat_tm = mods[0, :b][None]
    x1_l, fin, lg = _post_mixer(
        _post_proj_tm_kernel, [(o_l, pl.BlockSpec((b, tt, o_l.shape[-1]), lambda i: (0, i, 0)))], [wo],
        x=x, n_tok=b * l, x_off=0, mods=mod_lat_tm, g1=g1, g2=g2, wr_t=wr_t, tm=tm, rows_per_mod=b * l,
        name="post_mla", moe_total=n_moe, moe_off=b * n_ctx, prev=(fin, lg),
        x_spec=pl.BlockSpec((b, tt, d), lambda i: (0, i, 0)))
    shgu, shd = shared_weights(0)
    lat_a = (l // 2) * b
    n_a = b * n_ctx + lat_a
    moe = functools.partial(_moe, fin, lg, moe_bias[0], moe_w_gate, moe_w_up, moe_w_down, 0, tb)
    yk_a, gates_a = moe(0, n_a)
    yk_b, gates_b = moe(n_a, n_moe - n_a)
    comb = functools.partial(_combine, fin=fin, shgu=shgu, shd=shd, g3=g3, tm=tm)
    x2_c = comb(yk_a, gates_a, x1=x1_c, mods=mod_ctx, rows_per_mod=b * n_ctx, n_tok=b * n_ctx, x_off=0, yk_off=0,
                fin_off=0)

    n_all = n_ctx + l
    mod_lat = mods[1, :b]
    mod_ctx = jnp.broadcast_to(mods[1, b][None], (b, N_MOD * d))
    g0 = row(norm_g[1, 0])
    h, xt = _pre_s5(x2_c.reshape(b, n_ctx, d), mod_ctx, g0, n_all, 0, None, tt)
    comb_l = functools.partial(comb, x1=x1_l, mods=mod_lat_tm, rows_per_mod=b * l, out_rows=n_all * b,
                               nxt=(mod_lat, g0))
    xt, h = comb_l(yk_a, gates_a, n_tok=lat_a, x_off=0, yk_off=b * n_ctx, fin_off=b * n_ctx,
                   out_off=n_ctx * b, prev=(xt, h))
    xt, h = comb_l(yk_b, gates_b, n_tok=b * l - lat_a, x_off=lat_a, yk_off=0, fin_off=n_a,
                   out_off=n_ctx * b + lat_a, prev=(xt, h))
    toep, win, wout, lamq = _s5c_params(s5_lam_re[0], s5_lam_im[0], s5_log_step[0], s5_b_re[0], s5_b_im[0],
                                        s5_c_re[0], s5_c_im[0])
    ng = d // LANES
    yc = _s5_chunked(h, toep, win, wout, lamq, b, n_ctx)
    g1, g2, g3 = row(norm_g[1, 1]), row(norm_g[1, 2]), row(norm_g[1, 3])
    lat0 = n_ctx * b // tm
    x1, fin, lg = _post_mixer(
        _post_glu_kernel,
        [(h, pl.BlockSpec((ng, tm, LANES), lambda i: (0, i + lat0, 0))),
         (yc, pl.BlockSpec((yc.shape[0], tm // S5_Q, yc.shape[-1]), lambda i: (0, i, 0)))],
        [row(s5_d[0]), s5_w_glu[0].astype(BF16), row(s5_b_glu[0])],
        x=xt, n_tok=l * b, x_off=n_ctx * b, mods=mod_lat[None], g1=g1, g2=g2, wr_t=_router_halves(moe_w_router[1]), tm=tm,
        rows_per_mod=l * b, name="post_s5")
    shgu, shd = shared_weights(1)
    n_h = (l // 2) * b
    moe = functools.partial(_moe, fin, lg, moe_bias[1], moe_w_gate, moe_w_up, moe_w_down, 1, tb)
    yk_a, gates_a = moe(0, n_h)
    yk_b, gates_b = moe(n_h, l * b - n_h)
    comb = functools.partial(_combine, fin=fin, shgu=shgu, shd=shd, x1=x1, mods=mod_lat[None], g3=g3, tm=tm,
                             rows_per_mod=l * b, yk_off=0, batch_out=b)
    out = comb(yk_a, gates_a, n_tok=n_h, x_off=0, fin_off=0)
    return comb(yk_b, gates_b, n_tok=l * b - n_h, x_off=n_h, fin_off=n_h, prev=out)
```

```python
import functools

import jax
import jax.numpy as jnp
from jax import lax
from jax.experimental import pallas as pl
from jax.experimental.pallas import tpu as pltpu
from jax.experimental.pallas import tpu_sc as plsc

F32 = jnp.float32
BF16 = jnp.bfloat16
U32 = jnp.uint32

N_MOD = 6
NORM_EPS = 1e-6
LOG2_E = 1.4426950408889634
GRID_W = 64
MLA_HEADS = 8
Q_LORA = 384
KV_LORA = 256
NOPE_DIM = 128
ROPE_DIM = 64
V_DIM = 128
V_PAD = 256
ROPE_BASE = 10000.0
QK_PAD = 256
S5_GROUP = 16
S5_STATE = 64
N_EXPERTS = 64
TOP_K = 8
N_EXPERT_GROUPS = 8
TOPK_GROUPS = 4
D_EXPERT = 256
ROUTED_SCALE = 2.5

VMEM_LIMIT = 56 * 1024 * 1024


def _cparams(sem):
    return pltpu.CompilerParams(dimension_semantics=sem, vmem_limit_bytes=VMEM_LIMIT)


def _rms(x, g):
    return x * lax.rsqrt(jnp.mean(x * x, axis=-1, keepdims=True) + NORM_EPS) * g


def _rows(v, like):
    r = v.shape[0]
    if r == 1:
        return v
    tm, d = like.shape
    return jnp.broadcast_to(v[None], (tm // r, r, d)).reshape(tm, d)


def _mod_chunk(mod_ref, j, d):
    return mod_ref[:, j * d:(j + 1) * d]


def _dot(a, b):
    return jnp.dot(a, b, preferred_element_type=F32)


PACK_ROWS = 4
LANES = 128


def _pack_store(ref, val, lead=(), row0=0):
    n = val.shape[0]
    bits = lax.bitcast_convert_type(val.astype(BF16).astype(F32), U32)
    for s in range(PACK_ROWS):
        lo = bits[:, s * LANES:(s + 1) * LANES] >> 16
        hi = bits[:, (s + PACK_ROWS) * LANES:(s + PACK_ROWS + 1) * LANES] & jnp.uint32(0xFFFF0000)
        ref[lead + (pl.ds(row0 * PACK_ROWS + s, n, stride=PACK_ROWS), slice(None))] = lo | hi


def _unpack_load(ref, n, lead=(), row0=0):
    los, his = [], []
    for s in range(PACK_ROWS):
        w = ref[lead + (pl.ds(row0 * PACK_ROWS + s, n, stride=PACK_ROWS), slice(None))]
        los.append(lax.bitcast_convert_type(w << 16, F32))
        his.append(lax.bitcast_convert_type(w & jnp.uint32(0xFFFF0000), F32))
    return los + his


def _ada_kernel(c_ref, w_ref, b_ref, o_ref):
    c = c_ref[...]
    s = c * jax.nn.sigmoid(c)
    o_ref[...] = jnp.dot(s, w_ref[...], preferred_element_type=F32,
                         precision=lax.Precision.HIGHEST) + b_ref[...]


def _ada_mods(cvec, ada_w, ada_b):
    depth, d, n = ada_w.shape
    rows = cvec.shape[0]
    tn = 1536
    return pl.pallas_call(
        _ada_kernel,
        out_shape=jax.ShapeDtypeStruct((depth, rows, n), F32),
        grid=(depth, n // tn),
        in_specs=[pl.BlockSpec((rows, d), lambda l, j: (0, 0)),
                  pl.BlockSpec((None, d, tn), lambda l, j: (l, 0, j)),
                  pl.BlockSpec((None, 1, tn), lambda l, j: (l, 0, j))],
        out_specs=pl.BlockSpec((None, rows, tn), lambda l, j: (l, 0, j)),
        compiler_params=_cparams(("arbitrary", "arbitrary")),
        name="ada_mods",
    )(cvec, ada_w, ada_b.reshape(depth, 1, n))


def _pre_mla_kernel(x_ref, mod_ref, g0_ref, wd_ref, gq_ref, gkv_ref, wq_ref, wkv_ref, cos_ref, sin_ref,
                    q_ref, k_ref, v_ref):
    d = x_ref.shape[-1]
    x = x_ref[...]
    h = _rms(x, g0_ref[...]) * (1.0 + _mod_chunk(mod_ref, 1, d)) + _mod_chunk(mod_ref, 0, d)
    a = _dot(h.astype(BF16), wd_ref[...])
    cq = _rms(a[:, :Q_LORA], gq_ref[...])
    ckv = _rms(a[:, Q_LORA:Q_LORA + KV_LORA], gkv_ref[...])
    rd = ROPE_DIM
    cos = cos_ref[:, 0:rd]
    sin = sin_ref[:, 0:rd]
    o = Q_LORA + KV_LORA
    k_rot = (a[:, o:o + rd] * cos + a[:, o + rd:o + 2 * rd] * sin).astype(BF16)
    qa = _dot(cq.astype(BF16), wq_ref[...])
    kva = _dot(ckv.astype(BF16), wkv_ref[...])
    hw = MLA_HEADS * 128
    hr = MLA_HEADS * rd
    zpad = jnp.zeros((x.shape[0], QK_PAD - NOPE_DIM - rd), BF16)
    scale = (NOPE_DIM + ROPE_DIM) ** -0.5 * LOG2_E
    for hd in range(MLA_HEADS):
        lo = hd * 128
        q_rot = qa[:, hw + hd * rd:hw + (hd + 1) * rd] * cos + qa[:, hw + hr + hd * rd:hw + hr + (hd + 1) * rd] * sin
        q_ref[:, hd * QK_PAD:hd * QK_PAD + 128] = (qa[:, lo:lo + 128] * scale).astype(BF16)
        q_ref[:, hd * QK_PAD + 128:hd * QK_PAD + 128 + rd] = (q_rot * scale).astype(BF16)
        q_ref[:, hd * QK_PAD + 128 + rd:(hd + 1) * QK_PAD] = zpad
        k_ref[:, hd * QK_PAD:hd * QK_PAD + 128] = kva[:, lo:lo + 128].astype(BF16)
        k_ref[:, hd * QK_PAD + 128:hd * QK_PAD + 128 + rd] = k_rot
        k_ref[:, hd * QK_PAD + 128 + rd:(hd + 1) * QK_PAD] = zpad
        v_ref[:, hd * V_PAD:hd * V_PAD + V_DIM] = kva[:, hw + lo:hw + lo + 128].astype(BF16)
        v_ref[:, hd * V_PAD + V_DIM:(hd + 1) * V_PAD] = jnp.ones((x.shape[0], V_PAD - V_DIM), BF16)


def _pre_mla(x, mods, g0, wd, gq, gkv, wq, wkv, cos_t, sin_t, tm):
    b, n, d = x.shape
    nb_mod = mods.shape[0]
    full = lambda a: pl.BlockSpec(a.shape, lambda i, j: (0,) * a.ndim)
    mod_map = (lambda i, j: (i, 0, 0)) if nb_mod > 1 else (lambda i, j: (0, 0, 0))
    qk_w = MLA_HEADS * QK_PAD
    v_w = MLA_HEADS * V_PAD
    return pl.pallas_call(
        _pre_mla_kernel,
        out_shape=(jax.ShapeDtypeStruct((b, n, qk_w), BF16),
                   jax.ShapeDtypeStruct((b, n, qk_w), BF16),
                   jax.ShapeDtypeStruct((b, n, v_w), BF16)),
        grid=(b, n // tm),
        in_specs=[pl.BlockSpec((None, tm, d), lambda i, j: (i, j, 0)),
                  pl.BlockSpec((None, 1, mods.shape[-1]), mod_map),
                  full(g0), full(wd), full(gq), full(gkv), full(wq), full(wkv),
                  pl.BlockSpec((tm, 128), lambda i, j: (j, 0)),
                  pl.BlockSpec((tm, 128), lambda i, j: (j, 0))],
        out_specs=(pl.BlockSpec((None, tm, qk_w), lambda i, j: (i, j, 0)),
                   pl.BlockSpec((None, tm, qk_w), lambda i, j: (i, j, 0)),
                   pl.BlockSpec((None, tm, v_w), lambda i, j: (i, j, 0))),
        compiler_params=_cparams(("arbitrary", "arbitrary")),
        name="pre_mla",
    )(x, mods, g0, wd, gq, gkv, wq, wkv, cos_t, sin_t)


def _attn_kernel(*refs, n_seg):
    q_ref = refs[0]
    k_refs = refs[1:1 + n_seg]
    v_refs = refs[1 + n_seg:1 + 2 * n_seg]
    o_ref = refs[1 + 2 * n_seg]
    nt = (((1,), (1,)), ((), ()))

    def scores(hd):
        q = q_ref[:, hd * QK_PAD:(hd + 1) * QK_PAD]
        return [lax.dot_general(q, k[:, hd * QK_PAD:(hd + 1) * QK_PAD], nt, preferred_element_type=F32)
                for k in k_refs]

    nxt = scores(0)
    for hd in range(MLA_HEADS):
        ss = nxt
        if hd + 1 < MLA_HEADS:
            nxt = scores(hd + 1)
        m = ss[0].max(axis=-1, keepdims=True)
        for s in ss[1:]:
            m = jnp.maximum(m, s.max(axis=-1, keepdims=True))
        acc = None
        for s, v in zip(ss, v_refs):
            pv = _dot(jnp.exp2((s - m).astype(BF16)), v[:, hd * V_PAD:(hd + 1) * V_PAD])
            acc = pv if acc is None else acc + pv
        o_ref[:, hd * V_DIM:(hd + 1) * V_DIM] = (acc[:, :V_DIM] / acc[:, V_DIM:V_DIM + 1]).astype(BF16)


def _attention(q, ks, vs, tq):
    b, nq, qk_w = q.shape
    v_w = MLA_HEADS * V_DIM
    kv_spec = lambda a: pl.BlockSpec((None,) + a.shape[1:], lambda i, j: (i, 0, 0))
    return pl.pallas_call(
        functools.partial(_attn_kernel, n_seg=len(ks)),
        out_shape=jax.ShapeDtypeStruct((b, nq, v_w), BF16),
        grid=(b, nq // tq),
        in_specs=[pl.BlockSpec((None, tq, qk_w), lambda i, j: (i, j, 0))]
                 + [kv_spec(a) for a in ks] + [kv_spec(a) for a in vs],
        out_specs=pl.BlockSpec((None, tq, v_w), lambda i, j: (i, j, 0)),
        compiler_params=_cparams(("arbitrary", "arbitrary")),
        name="mla_attention",
    )(q, *ks, *vs)


SUB_ROWS = 256


def _sub_tiles(n):
    return [slice(r, r + SUB_ROWS) for r in range(0, n, SUB_ROWS)]


def _post_core(o, x, rows, mod_ref, g1_ref, g2_ref, wr_ref, x1_ref, fin_ref, lg_ref):
    d = x.shape[-1]
    ne = lg_ref.shape[0]
    gate = _rows(_mod_chunk(mod_ref, 2, d), x)
    shift = _rows(_mod_chunk(mod_ref, 3, d), x)
    scale = _rows(_mod_chunk(mod_ref, 4, d), x)
    x1 = x + gate * _rms(o, g1_ref[...])
    fin = _rms(x1, g2_ref[...]) * (1.0 + scale) + shift
    x1_ref[rows, :] = x1
    _pack_store(fin_ref, fin, row0=rows.start)
    nt = (((1,), (1,)), ((), ()))
    f_hi = fin.astype(BF16)
    f_lo = (fin - f_hi.astype(F32)).astype(BF16)
    r_hi = lax.dot_general(wr_ref[...], f_hi, nt, preferred_element_type=F32)
    r_lo = lax.dot_general(wr_ref[0:ne, :], f_lo, nt, preferred_element_type=F32)
    lg_ref[:, rows] = r_hi[:ne] + r_hi[ne:] + r_lo


def _post_proj_kernel(o_ref, wo_ref, x_ref, mod_ref, g1_ref, g2_ref, wr_ref, *rest):
    x1_ref, fin_ref, lg_ref = rest[-3:]
    for rows in _sub_tiles(x_ref.shape[0]):
        o = _dot(o_ref[rows, :], wo_ref[...])
        _post_core(o, x_ref[rows, :], rows, mod_ref, g1_ref, g2_ref, wr_ref, x1_ref, fin_ref, lg_ref)


def _post_proj_tm_kernel(o_ref, wo_ref, x_ref, mod_ref, g1_ref, g2_ref, wr_ref, *rest):
    x1_ref, fin_ref, lg_ref = rest[-3:]
    nb, tt, d = x_ref.shape
    ts = SUB_ROWS // nb
    for t0 in range(0, tt, ts):
        o = _dot(o_ref[:, t0:t0 + ts, :].reshape(nb * ts, o_ref.shape[-1]), wo_ref[...])
        o = jnp.swapaxes(o.reshape(nb, ts, d), 0, 1).reshape(ts * nb, d)
        x = jnp.swapaxes(x_ref[:, t0:t0 + ts, :], 0, 1).reshape(ts * nb, d)
        _post_core(o, x, slice(t0 * nb, (t0 + ts) * nb), mod_ref, g1_ref, g2_ref, wr_ref, x1_ref, fin_ref, lg_ref)


def _chunk_to_rows(yc_ref, c, nb):
    q, grp = S5_Q, S5_GROUP
    per_tile = LANES // grp
    lane_grp = lax.broadcasted_iota(jnp.int32, (nb, LANES), 1) // grp
    n_pairs = yc_ref.shape[0]
    pieces = [[yc_ref[p, c * nb:(c + 1) * nb, lt * LANES:(lt + 1) * LANES] for lt in range(2 * q // per_tile)]
              for p in range(n_pairs)]
    out_rows = []
    for t in range(q):
        tiles = []
        for lb in range(2 * n_pairs // per_tile):
            dest = None
            for g8 in range(per_tile):
                g = lb * per_tile + g8
                piece = pieces[g // 2][(g % 2) * (q // per_tile) + t // per_tile]
                r = pltpu.roll(piece, ((g8 - t % per_tile) * grp) % LANES, 1)
                dest = r if dest is None else jnp.where(lane_grp == g8, r, dest)
            tiles.append(dest)
        out_rows.append(jnp.concatenate(tiles, axis=-1))
    return jnp.concatenate(out_rows, axis=0)


def _post_glu_kernel(h_ref, yc_ref, dsk_ref, wg_ref, bg_ref, x_ref, mod_ref, g1_ref, g2_ref, wr_ref,
                     *rest):
    x1_ref, fin_ref, lg_ref = rest[-3:]
    d = x_ref.shape[-1]
    nb = mod_ref.shape[0]
    assert SUB_ROWS == S5_Q * nb
    for ci, rows in enumerate(_sub_tiles(x_ref.shape[0])):
        h = jnp.concatenate([h_ref[g, rows, :] for g in range(h_ref.shape[0])], axis=-1).astype(F32)
        y = h * dsk_ref[...] + _chunk_to_rows(yc_ref, ci, nb)
        z = _dot(jax.nn.gelu(y, approximate=True).astype(BF16), wg_ref[...]) + bg_ref[...]
        o = z[:, :d] * jax.nn.sigmoid(z[:, d:])
        _post_core(o, x_ref[rows, :], rows, mod_ref, g1_ref, g2_ref, wr_ref, x1_ref, fin_ref, lg_ref)


def _post_mixer(kernel, tok_inputs, consts, x, n_tok, x_off, mods, g1, g2, wr_t, tm, rows_per_mod, name,
                moe_total=None, moe_off=0, prev=None, x_spec=None):
    d = x.shape[-1]
    ne = wr_t.shape[0] // 2
    moe_total = n_tok if moe_total is None else moe_total
    tiles_per_mod = rows_per_mod // tm
    xo, mo = x_off // tm, moe_off // tm
    full = lambda a: pl.BlockSpec(a.shape, lambda i: (0,) * a.ndim)
    tile = pl.BlockSpec((tm, d), lambda i: (i, 0))
    mod_spec = pl.BlockSpec((None,) + mods.shape[1:], lambda i: (i // tiles_per_mod, 0, 0))
    x_spec = pl.BlockSpec((tm, d), lambda i: (i + xo, 0)) if x_spec is None else x_spec
    in_specs = ([spec for _, spec in tok_inputs] + [full(a) for a in consts]
                + [x_spec, mod_spec, full(g1), full(g2), full(wr_t)])
    args = [a for a, _ in tok_inputs] + list(consts) + [x, mods, g1, g2, wr_t]
    aliases = {}
    if prev is not None:
        aliases = {len(args): 1, len(args) + 1: 2}
        in_specs += [pl.BlockSpec(memory_space=pl.ANY)] * 2
        args += list(prev)
    return pl.pallas_call(
        kernel,
        out_shape=(jax.ShapeDtypeStruct((n_tok, d), F32),
                   jax.ShapeDtypeStruct((moe_total * PACK_ROWS, LANES), U32),
                   jax.ShapeDtypeStruct((ne, moe_total), F32)),
        grid=(n_tok // tm,),
        in_specs=in_specs,
        out_specs=(tile, pl.BlockSpec((tm * PACK_ROWS, LANES), lambda i: (i + mo, 0)),
                   pl.BlockSpec((ne, tm), lambda i: (0, i + mo))),
        input_output_aliases=aliases,
        compiler_params=_cparams(("arbitrary",)),
        name=name,
    )(*args)


def _route_kernel(lg_ref, bias_ref, eidx_ref, gate_ref, rank_ref, cnt_ref, tri_ref, base_ref):
    i = pl.program_id(0)
    ne, tt = lg_ref.shape
    gsz = ne // N_EXPERT_GROUPS
    shp = (N_EXPERT_GROUPS, gsz, tt)
    neg = -jnp.inf

    @pl.when(i == 0)
    def _():
        base_ref[...] = jnp.zeros_like(base_ref)
        r = lax.broadcasted_iota(jnp.int32, (tt, tt), 0)
        c = lax.broadcasted_iota(jnp.int32, (tt, tt), 1)
        tri_ref[...] = (r < c).astype(BF16)

    scores = jax.nn.sigmoid(lg_ref[...])
    s3 = scores.reshape(shp)
    b3 = (scores + bias_ref[...]).reshape(shp)
    io_e = lax.broadcasted_iota(jnp.int32, shp, 1)
    io_g = lax.broadcasted_iota(jnp.int32, shp, 0)
    io_flat = io_g * gsz + io_e
    m1 = b3.max(axis=1, keepdims=True)
    i1 = jnp.where(b3 == m1, io_e, gsz).min(axis=1, keepdims=True)
    m2 = jnp.where(io_e == i1, neg, b3).max(axis=1, keepdims=True)
    cur = jnp.broadcast_to(m1 + m2, shp)
    gsel = jnp.zeros(shp, jnp.bool_)
    for _ in range(TOPK_GROUPS):
        m = cur.max(axis=0, keepdims=True)
        gi = jnp.where(cur == m, io_g, N_EXPERT_GROUPS).min(axis=0, keepdims=True)
        hit = io_g == gi
        gsel = jnp.logical_or(gsel, hit)
        cur = jnp.where(hit, neg, cur)
    cand = jnp.where(gsel, b3, neg)
    sel = jnp.zeros(shp, jnp.bool_)
    eids, gts = [], []
    for _ in range(TOP_K):
        m = cand.max(axis=0, keepdims=True).max(axis=1, keepdims=True)
        ei = jnp.where(cand == m, io_flat, ne).min(axis=0, keepdims=True).min(axis=1, keepdims=True)
        hit = io_flat == ei
        gts.append(jnp.where(hit, s3, 0.0).sum(axis=0, keepdims=True).sum(axis=1, keepdims=True))
        eids.append(ei)
        sel = jnp.logical_or(sel, hit)
        cand = jnp.where(hit, neg, cand)
    gsum = gts[0]
    for g in gts[1:]:
        gsum = gsum + g
    self32 = sel.astype(F32).reshape(ne, tt)
    cnt = _dot(self32.astype(BF16), tri_ref[...]) + base_ref[...]
    cnt3 = cnt.reshape(shp)
    for k in range(TOP_K):
        hit = io_flat == eids[k]
        rk = jnp.where(hit, cnt3, 0.0).sum(axis=0, keepdims=True).sum(axis=1, keepdims=True)
        rank_ref[k:k + 1, :] = rk.reshape(1, tt).astype(jnp.int32)
        eidx_ref[k:k + 1, :] = eids[k].reshape(1, tt)
        gate_ref[k:k + 1, :] = (gts[k] / gsum * ROUTED_SCALE).reshape(1, tt)
    base_new = base_ref[...] + self32.sum(axis=1, keepdims=True)
    base_ref[...] = base_new
    cnt_ref[...] = jnp.broadcast_to(base_new, cnt_ref.shape)


def _route(logits_t, bias, tt, tok0, t):
    ne = logits_t.shape[0]
    off = tok0 // tt
    out_i = jax.ShapeDtypeStruct((TOP_K, t), jnp.int32)
    row = pl.BlockSpec((TOP_K, tt), lambda i: (0, i))
    return pl.pallas_call(
        _route_kernel,
        out_shape=(out_i, jax.ShapeDtypeStruct((TOP_K, t), F32), out_i,
                   jax.ShapeDtypeStruct((ne, 128), F32)),
        grid=(t // tt,),
        in_specs=[pl.BlockSpec((ne, tt), lambda i: (0, i + off)),
                  pl.BlockSpec((ne, 1), lambda i: (0, 0))],
        out_specs=(row, row, row, pl.BlockSpec((ne, 128), lambda i: (0, 0))),
        scratch_shapes=[pltpu.VMEM((tt, tt), BF16), pltpu.VMEM((ne, 1), F32)],
        compiler_params=_cparams(("arbitrary",)),
        name="moe_route",
    )(logits_t, bias.reshape(ne, 1))


def _dest_kernel(eidx_ref, rank_ref, start_ref, dest_ref):
    kk, tt = eidx_ref.shape
    ne = start_ref.shape[0]
    n_chunk, _, r = dest_ref.shape
    io_e = lax.broadcasted_iota(jnp.int32, (ne, tt), 0)
    start = start_ref[...]
    for k in range(kk):
        hit = io_e == eidx_ref[k:k + 1, :]
        dk = jnp.where(hit, start, 0).sum(axis=0, keepdims=True) + rank_ref[k:k + 1, :]
        for c in range(n_chunk):
            dest_ref[c, k:k + 1, :] = dk[:, c * r:(c + 1) * r]


def _dest_rows(eidx_t, rank_t, start, tt, r):
    kk, t = eidx_t.shape
    ne = start.shape[0]
    return pl.pallas_call(
        _dest_kernel,
        out_shape=jax.ShapeDtypeStruct((t // r, kk, r), jnp.int32),
        grid=(t // tt,),
        in_specs=[pl.BlockSpec((kk, tt), lambda i: (0, i)),
                  pl.BlockSpec((kk, tt), lambda i: (0, i)),
                  pl.BlockSpec((ne, 1), lambda i: (0, 0))],
        out_specs=pl.BlockSpec((tt // r, kk, r), lambda i: (i, 0, 0)),
        compiler_params=_cparams(("arbitrary",)),
        name="moe_dest",
    )(eidx_t, rank_t, start.reshape(ne, 1))


SC_CHUNK = 64


def _sc_mesh():
    return plsc.VectorSubcoreMesh(core_axis_name="c", subcore_axis_name="s")


def _sc_workers():
    info = plsc.get_sparse_core_info()
    return info.num_cores, info.num_cores * info.num_subcores


def _sc_scatter_rows(rows, dest, n_out, row0=0):
    n_chunk, kk, r = dest.shape
    nc, nw = _sc_workers()
    cpw = n_chunk // nw
    assert cpw * nw == n_chunk and cpw % 2 == 0 and row0 % r == 0 and row0 + n_chunk * r <= rows.shape[0]

    @functools.partial(
        pl.kernel, mesh=_sc_mesh(),
        out_type=jax.ShapeDtypeStruct((n_out,) + rows.shape[1:], rows.dtype),
        scratch_types=[pltpu.VMEM((2, kk, r), jnp.int32), pltpu.VMEM((2, r) + rows.shape[1:], rows.dtype),
                       pltpu.SemaphoreType.DMA((2,)), pltpu.SemaphoreType.DMA((2,))])
    def scatter(rows_hbm, dest_hbm, out_hbm, idx_v, rows_v, load_sem, scat_sem):
        c0 = (lax.axis_index("s") * nc + lax.axis_index("c")) * cpw

        def loads(c, b):
            return (pltpu.make_async_copy(dest_hbm.at[c], idx_v.at[b], load_sem.at[b]),
                    pltpu.make_async_copy(rows_hbm.at[pl.ds(row0 + c * r, r)], rows_v.at[b], load_sem.at[b]))

        def scat(b, k):
            return pltpu.make_async_copy(rows_v.at[b], out_hbm.at[idx_v.at[b, k]], scat_sem.at[b])

        for cp in loads(c0, 0):
            cp.start()

        @pl.loop(0, cpw, step=2)
        def _(ci):
            for b in range(2):
                c = c0 + ci + b
                for cp in loads(c, b):
                    cp.wait()
                for k in range(kk):
                    scat(b, k).start()

                @pl.when(ci + b >= 1)
                def _():
                    for k in range(kk):
                        scat(1 - b, k).wait()

                @pl.when(ci + b + 1 < cpw)
                def _():
                    for cp in loads(c + 1, 1 - b):
                        cp.start()

        for k in range(kk):
            scat((cpw - 1) % 2, k).wait()

    return scatter(rows, dest)


def _sc_gather_rows(src, dest):
    n_chunk, kk, r = dest.shape
    t = n_chunk * r
    nc, nw = _sc_workers()
    cpw = n_chunk // nw
    nbuf = 3
    assert cpw * nw == n_chunk and kk > nbuf

    @functools.partial(
        pl.kernel, mesh=_sc_mesh(),
        out_type=jax.ShapeDtypeStruct((kk, t) + src.shape[1:], src.dtype),
        scratch_types=[pltpu.VMEM((kk, r), jnp.int32), pltpu.VMEM((nbuf, r) + src.shape[1:], src.dtype),
                       pltpu.SemaphoreType.DMA((nbuf,)), pltpu.SemaphoreType.DMA((nbuf,))])
    def gather(src_hbm, dest_hbm, out_hbm, idx_v, rows_v, get_sem, put_sem):
        c0 = (lax.axis_index("s") * nc + lax.axis_index("c")) * cpw

        @pl.loop(0, cpw)
        def _(ci):
            c = c0 + ci
            pltpu.sync_copy(dest_hbm.at[c], idx_v)

            def get(k):
                return pltpu.make_async_copy(src_hbm.at[idx_v.at[k]], rows_v.at[k % nbuf], get_sem.at[k % nbuf])

            def put(k):
                return pltpu.make_async_copy(rows_v.at[k % nbuf], out_hbm.at[k, pl.ds(c * r, r)],
                                             put_sem.at[k % nbuf])

            for k in range(nbuf - 1):
                get(k).start()
            for k in range(kk):
                get(k).wait()
                put(k).start()
                if k + nbuf - 1 < kk:
                    if k >= 1:
                        put(k - 1).wait()
                    get(k + nbuf - 1).start()
            for k in range(kk - nbuf, kk):
                put(k).wait()

    return gather(src, dest)


def _expert_kernel(be_ref, nu_ref, x_ref, wg_ref, wu_ref, wd_ref, o_ref, wgu_s, wd_s):
    i = pl.program_id(0)
    tb = o_ref.shape[0] // PACK_ROWS

    @pl.when(i < nu_ref[0])
    def _():
        @pl.when(jnp.logical_or(i == 0, be_ref[i] != be_ref[jnp.maximum(i - 1, 0)]))
        def _():
            wgu_s[:, :D_EXPERT] = wg_ref[...].astype(BF16)
            wgu_s[:, D_EXPERT:] = wu_ref[...].astype(BF16)
            wd_s[...] = wd_ref[...].astype(BF16)

        x = jnp.concatenate([v.astype(BF16) for v in _unpack_load(x_ref, tb)], axis=-1)
        gu = _dot(x, wgu_s[...])
        g = gu[:, :D_EXPERT]
        h = g * jax.nn.sigmoid(g) * gu[:, D_EXPERT:]
        _pack_store(o_ref, _dot(h.astype(BF16), wd_s[...]))


def _experts(xs, blk_e, n_used, w_gate, w_up, w_down, layer, tb):
    rows = xs.shape[0] // PACK_ROWS
    _, ne, d, de = w_gate.shape
    nb = rows // tb
    row_map = lambda i, be, nu: (jnp.minimum(i, nu[0] - 1), 0)
    w_map = lambda i, be, nu: (layer, be[i], 0, 0)
    grid_spec = pltpu.PrefetchScalarGridSpec(
        num_scalar_prefetch=2,
        grid=(nb,),
        in_specs=[pl.BlockSpec((tb * PACK_ROWS, LANES), row_map),
                  pl.BlockSpec((None, None, d, de), w_map),
                  pl.BlockSpec((None, None, d, de), w_map),
                  pl.BlockSpec((None, None, de, d), w_map)],
        out_specs=pl.BlockSpec((tb * PACK_ROWS, LANES), row_map),
        scratch_shapes=[pltpu.VMEM((d, 2 * de), BF16), pltpu.VMEM((de, d), BF16)],
    )
    return pl.pallas_call(
        _expert_kernel,
        out_shape=jax.ShapeDtypeStruct(xs.shape, U32),
        grid_spec=grid_spec,
        compiler_params=_cparams(("arbitrary",)),
        name="moe_experts",
    )(blk_e, n_used, xs, w_gate, w_up, w_down)


def _combine_kernel(yk_ref, gate_ref, fin_ref, shgu_ref, shd_ref, x1_ref, mod_ref, g3_ref, *rest, fuse_next):
    if fuse_next:
        nmod_ref, ng0_ref = rest[0], rest[1]
        o_ref, h_ref = rest[-2], rest[-1]
    else:
        o_ref = rest[-1]
    tm, d = x1_ref.shape
    for rows in _sub_tiles(tm):
        n, r0 = SUB_ROWS, rows.start
        gates = gate_ref[rows, :]
        blocks = None
        for k in range(TOP_K):
            gk = gates[:, k:k + 1]
            terms = [gk * v for v in _unpack_load(yk_ref, n, lead=(k,), row0=r0)]
            blocks = terms if blocks is None else [a + b for a, b in zip(blocks, terms)]
        fin = jnp.concatenate([v.astype(BF16) for v in _unpack_load(fin_ref, n, row0=r0)], axis=-1)
        gu = _dot(fin, shgu_ref[...])
        g = gu[:, :D_EXPERT]
        hsh = g * jax.nn.sigmoid(g) * gu[:, D_EXPERT:]
        f = jnp.concatenate(blocks, axis=-1) + _dot(hsh.astype(BF16), shd_ref[...])
        x1 = x1_ref[rows, :]
        x2 = x1 + _rows(_mod_chunk(mod_ref, 5, d), x1) * _rms(f, g3_ref[...])
        if len(o_ref.shape) == 2:
            o_ref[rows, :] = x2
        else:
            nb = o_ref.shape[0]
            ts = SUB_ROWS // nb
            o_ref[:, r0 // nb:r0 // nb + ts, :] = jnp.swapaxes(x2.reshape(ts, nb, d), 0, 1)
        if fuse_next:
            hn = (_rms(x2, ng0_ref[...]) * (1.0 + _rows(_mod_chunk(nmod_ref, 1, d), x2))
                  + _rows(_mod_chunk(nmod_ref, 0, d), x2))
            for gi in range(h_ref.shape[0]):
                h_ref[gi, rows, :] = hn[:, gi * LANES:(gi + 1) * LANES].astype(BF16)


def _combine(yk, gates, fin, shgu, shd, x1, mods, g3, tm, rows_per_mod, n_tok, x_off, yk_off, fin_off,
             batch_out=0, prev=None, out_rows=None, out_off=None, nxt=None):
    t, d = x1.shape
    out_rows = t if out_rows is None else out_rows
    out_off = x_off if out_off is None else out_off
    xo, yo, fo, oo = x_off // tm, yk_off // tm, fin_off // tm, out_off // tm
    tiles_per_mod = rows_per_mod // tm
    full = lambda a: pl.BlockSpec(a.shape, lambda i: (0,) * a.ndim)
    if batch_out:
        out_shape = [jax.ShapeDtypeStruct((batch_out, out_rows // batch_out, d), F32)]
        out_specs = [pl.BlockSpec((batch_out, tm // batch_out, d), lambda i: (0, i + oo, 0))]
    else:
        out_shape = [jax.ShapeDtypeStruct((out_rows, d), F32)]
        out_specs = [pl.BlockSpec((tm, d), lambda i: (i + oo, 0))]
    in_specs = [pl.BlockSpec((TOP_K, tm * PACK_ROWS, LANES), lambda i: (0, i + yo, 0)),
                pl.BlockSpec((tm, TOP_K), lambda i: (i + yo, 0)),
                pl.BlockSpec((tm * PACK_ROWS, LANES), lambda i: (i + fo, 0)),
                full(shgu), full(shd),
                pl.BlockSpec((tm, d), lambda i: (i + xo, 0)),
                pl.BlockSpec((None,) + mods.shape[1:], lambda i: ((i + xo) // tiles_per_mod, 0, 0)),
                full(g3)]
    args = [yk, gates, fin, shgu, shd, x1, mods, g3]
    if nxt is not None:
        in_specs += [full(nxt[0]), full(nxt[1])]
        args += list(nxt)
        out_shape.append(jax.ShapeDtypeStruct((d // LANES, out_rows, LANES), BF16))
        out_specs.append(pl.BlockSpec((d // LANES, tm, LANES), lambda i: (0, i + oo, 0)))
    aliases = {}
    if prev is not None:
        for j, p in enumerate(prev if isinstance(prev, (tuple, list)) else [prev]):
            in_specs.append(pl.BlockSpec(memory_space=pl.ANY))
            aliases[len(args)] = j
            args.append(p)
    out = pl.pallas_call(
        functools.partial(_combine_kernel, fuse_next=nxt is not None),
        out_shape=tuple(out_shape),
        grid=(n_tok // tm,),
        in_specs=in_specs,
        out_specs=tuple(out_specs),
        input_output_aliases=aliases,
        compiler_params=_cparams(("arbitrary",)),
        name="moe_combine",
    )(*args)
    return out if nxt is not None else out[0]


def _moe(fin, logits_t, bias, w_gate, w_up, w_down, layer, tb, tok0, t):
    t_all = fin.shape[0] // PACK_ROWS
    ne = w_gate.shape[1]
    tt = 512
    eidx_t, gates_t, rank_t, cnt = _route(logits_t, bias, tt, tok0, t)
    counts = cnt[:, 0].astype(jnp.int32)
    padded = (counts + tb - 1) // tb * tb
    pad_end = jnp.cumsum(padded)
    pad_start = pad_end - padded
    nb = (t * TOP_K) // tb + ne
    n_used = pad_end[-1] // tb
    blk_start = jnp.arange(nb, dtype=jnp.int32) * tb
    blk = jnp.sum(pad_end[None, :] <= jnp.minimum(blk_start, pad_end[-1] - 1)[:, None], axis=1)
    blk_e = jnp.minimum(blk, ne - 1).astype(jnp.int32)
    dest = _dest_rows(eidx_t, rank_t, pad_start, tt, SC_CHUNK)
    xs = _sc_scatter_rows(fin.reshape(t_all, PACK_ROWS, LANES), dest, nb * tb, row0=tok0)
    ys = _experts(xs.reshape(nb * tb * PACK_ROWS, LANES), blk_e, n_used.reshape(1).astype(jnp.int32),
                  w_gate, w_up, w_down, layer, tb)
    yk = _sc_gather_rows(ys.reshape(nb * tb, PACK_ROWS, LANES), dest)
    return yk.reshape(TOP_K, t * PACK_ROWS, LANES), gates_t.T


def _pre_s5_kernel(x_ref, mod_ref, g0_ref, *refs):
    h_ref, xt_ref = refs[-2:]
    nb, tt, d = x_ref.shape
    x = jnp.swapaxes(x_ref[...], 0, 1).reshape(tt * nb, d)
    h = (_rms(x, g0_ref[...]) * (1.0 + _rows(_mod_chunk(mod_ref, 1, d), x))
         + _rows(_mod_chunk(mod_ref, 0, d), x))
    for g in range(h_ref.shape[0]):
        h_ref[g] = h[:, g * LANES:(g + 1) * LANES].astype(BF16)
    xt_ref[...] = x


def _pre_s5(x, mods, g0, n_total, t_off, prev, tt):
    nb, n, d = x.shape
    off = t_off // tt
    out_shape = (jax.ShapeDtypeStruct((d // LANES, n_total * nb, LANES), BF16),
                 jax.ShapeDtypeStruct((n_total * nb, d), F32))
    out_specs = (pl.BlockSpec((d // LANES, tt * nb, LANES), lambda i: (0, i + off, 0)),
                 pl.BlockSpec((tt * nb, d), lambda i: (i + off, 0)))
    in_specs = [pl.BlockSpec((nb, tt, d), lambda i: (0, i, 0)),
                pl.BlockSpec(mods.shape, lambda i: (0, 0)),
                pl.BlockSpec(g0.shape, lambda i: (0, 0))]
    args = (x, mods, g0)
    aliases = {}
    if prev is not None:
        in_specs += [pl.BlockSpec(memory_space=pl.ANY)] * 2
        args += tuple(prev)
        aliases = {3: 0, 4: 1}
    return pl.pallas_call(
        _pre_s5_kernel,
        out_shape=out_shape,
        grid=(n // tt,),
        in_specs=in_specs,
        out_specs=out_specs,
        input_output_aliases=aliases,
        compiler_params=_cparams(("arbitrary",)),
        name="pre_s5",
    )(*args)


S5_Q = 16
S5_STEP_GROUPS = 4


def _s5c_kernel(h_ref, toep_ref, win_ref, wout_ref, lamq_ref, y_ref, u_ref, sre_ref, sim_ref, xin_ref, *,
                nb, n_ctx_chunks):
    npair, n_rows, kw = u_ref.shape
    half = kw // 2
    sw = 2 * S5_STATE
    n = n_rows // nb
    ncc = n_ctx_chunks
    per_tile = LANES // S5_GROUP
    lane_grp = lax.broadcasted_iota(jnp.int32, (nb, LANES), 1) // S5_GROUP
    steps_per_block = per_tile // S5_STEP_GROUPS

    def relayout_from(gl0):
        def relayout(c, carry):
            r0 = c * (S5_Q * nb)
            pieces = [h_ref[pl.ds(pl.multiple_of(r0 + s * nb, nb), nb), :].astype(F32) for s in range(S5_Q)]
            rows = pl.ds(pl.multiple_of(c * nb, nb), nb)
            for gq in range(S5_STEP_GROUPS):
                for j in range(S5_Q // per_tile):
                    dest = None
                    for s8 in range(per_tile):
                        r = pltpu.roll(pieces[j * per_tile + s8], ((s8 - gl0 - gq) * S5_GROUP) % LANES, 1)
                        dest = r if dest is None else jnp.where(lane_grp == s8, r, dest)
                    lo = (gq % 2) * half + j * LANES
                    u_ref[gq // 2, rows, lo:lo + LANES] = dest.astype(BF16)
            return carry

        lax.fori_loop(0, n, relayout, 0, unroll=4)

    for k in range(steps_per_block):
        pl.when(pl.program_id(0) % steps_per_block == k)(functools.partial(relayout_from, k * S5_STEP_GROUPS))
    for dr in range(2):
        for p in range(npair):
            s = _dot(u_ref[p], win_ref[dr, p])
            sre_ref[dr, :, p * sw:(p + 1) * sw] = s[:, :sw]
            sim_ref[dr, :, p * sw:(p + 1) * sw] = s[:, sw:]
    w = npair * sw
    lr = [jnp.broadcast_to(lamq_ref[dr, 0:1, :], (nb, w)) for dr in range(2)]
    li = [jnp.broadcast_to(lamq_ref[dr, 1:2, :], (nb, w)) for dr in range(2)]

    def step(j, carry):
        rev = jnp.where(j < ncc, ncc - 1 - j, n - 1 - (j - ncc))
        out = []
        for dr in range(2):
            xr, xi = carry[dr]
            c = j if dr == 0 else rev
            rows = pl.ds(pl.multiple_of(c * nb, nb), nb)
            for p in range(npair):
                xin_ref[dr, rows, 2 * p * sw:(2 * p + 1) * sw] = xr[:, p * sw:(p + 1) * sw].astype(BF16)
                xin_ref[dr, rows, (2 * p + 1) * sw:(2 * p + 2) * sw] = xi[:, p * sw:(p + 1) * sw].astype(BF16)
            nr = lr[dr] * xr - li[dr] * xi + sre_ref[dr, rows, :]
            ni = lr[dr] * xi + li[dr] * xr + sim_ref[dr, rows, :]
            out.append((nr, ni))
        return tuple(out)

    zero = jnp.zeros((nb, w), F32)
    lax.fori_loop(0, n, step, ((zero, zero), (zero, zero)), unroll=2)
    lat = slice(ncc * nb, n_rows)
    for p in range(npair):
        acc = None
        for dr in range(2):
            intra = jnp.concatenate([_dot(u_ref[p, lat, 0:half], toep_ref[dr, 2 * p]),
                                     _dot(u_ref[p, lat, half:kw], toep_ref[dr, 2 * p + 1])], axis=-1)
            term = intra + _dot(xin_ref[dr, lat, 2 * p * sw:(2 * p + 2) * sw], wout_ref[dr, p])
            acc = term if acc is None else acc + term
        y_ref[p] = acc


def _s5_chunked(h, toep, win, wout, lamq, nb, n_ctx):
    ng, n_tok, _ = h.shape
    n_groups = toep.shape[1]
    pp = S5_STEP_GROUPS // 2
    steps_per_block = (LANES // S5_GROUP) // S5_STEP_GROUPS
    n_rows = n_tok // S5_Q
    ncc = n_ctx // S5_Q
    lat_rows = n_rows - ncc * nb
    kw = 2 * S5_Q * S5_GROUP
    sw = 2 * S5_STATE
    return pl.pallas_call(
        functools.partial(_s5c_kernel, nb=nb, n_ctx_chunks=ncc),
        out_shape=jax.ShapeDtypeStruct((n_groups // 2, lat_rows, kw), F32),
        grid=(n_groups // S5_STEP_GROUPS,),
        in_specs=[pl.BlockSpec((None, n_tok, LANES), lambda i: (i // steps_per_block, 0, 0),
                               pipeline_mode=pl.Buffered(1)),
                  pl.BlockSpec((2, 2 * pp, kw // 2, kw // 2), lambda i: (0, i, 0, 0)),
                  pl.BlockSpec((2, pp, kw, 2 * sw), lambda i: (0, i, 0, 0)),
                  pl.BlockSpec((2, pp, 2 * sw, kw), lambda i: (0, i, 0, 0)),
                  pl.BlockSpec((2, None, 2, pp * sw), lambda i: (0, i, 0, 0))],
        out_specs=pl.BlockSpec((pp, lat_rows, kw), lambda i: (i, 0, 0)),
        scratch_shapes=[pltpu.VMEM((pp, n_rows, kw), BF16),
                        pltpu.VMEM((2, n_rows, pp * sw), F32), pltpu.VMEM((2, n_rows, pp * sw), F32),
                        pltpu.VMEM((2, n_rows, 2 * pp * sw), BF16)],
        compiler_params=_cparams(("arbitrary",)),
        name="s5_chunked",
    )(h, toep, win, wout, lamq)


def _s5c_params(lam_re, lam_im, log_step, b_re, b_im, c_re, c_im):
    hp = lax.Precision.HIGHEST
    q = S5_Q
    _, g, p = lam_re.shape
    ni = b_re.shape[-1]
    step = jnp.exp(log_step)[..., None]
    ar, ai = lam_re * step, lam_im * step
    tau = jnp.arange(q + 1, dtype=F32)[:, None, None, None]
    mag = jnp.exp(tau * ar)
    pr, pi = mag * jnp.cos(tau * ai), mag * jnp.sin(tau * ai)
    den = lam_re * lam_re + lam_im * lam_im
    f_re = ((pr[1] - 1.0) * lam_re + pi[1] * lam_im) / den
    f_im = (pi[1] * lam_re - (pr[1] - 1.0) * lam_im) / den
    bb_re = f_re[..., None] * b_re - f_im[..., None] * b_im
    bb_im = f_re[..., None] * b_im + f_im[..., None] * b_re
    cp_re = c_re[None] * pr[:, :, :, None, :] - c_im[None] * pi[:, :, :, None, :]
    cp_im = c_re[None] * pi[:, :, :, None, :] + c_im[None] * pr[:, :, :, None, :]
    taps = (jnp.einsum("tdgop,dgpi->tdgoi", cp_re[:q], bb_re, precision=hp)
            - jnp.einsum("tdgop,dgpi->tdgoi", cp_im[:q], bb_im, precision=hp))
    s_idx = jnp.arange(q)[:, None]
    t_idx = jnp.arange(q)[None, :]

    def toeplitz(dr):
        lag = (t_idx - s_idx) if dr == 0 else (s_idx - t_idx)
        k = taps[:, dr][jnp.clip(lag, 0, q - 1)]
        k = jnp.where((lag >= 0)[:, :, None, None, None], k, 0.0)
        return k.transpose(2, 0, 4, 1, 3).reshape(g, q * ni, q * ni)

    toep = jnp.stack([toeplitz(0), toeplitz(1)]).astype(BF16)

    def state_in(dr):
        e = (q - 1 - jnp.arange(q)) if dr == 0 else jnp.arange(q)
        er, ei = pr[e, dr], pi[e, dr]
        br, bi = bb_re[dr].transpose(0, 2, 1), bb_im[dr].transpose(0, 2, 1)
        w_re = er[:, :, None, :] * br[None] - ei[:, :, None, :] * bi[None]
        w_im = er[:, :, None, :] * bi[None] + ei[:, :, None, :] * br[None]
        fl = lambda a: a.transpose(1, 0, 2, 3).reshape(g, q * ni, p)
        return fl(w_re), fl(w_im)

    def state_out(dr):
        f = (jnp.arange(q) + 1) if dr == 0 else (q - jnp.arange(q))
        fl = lambda a: a.transpose(1, 3, 0, 2).reshape(g, p, q * ni)
        return fl(cp_re[f, dr]), fl(-cp_im[f, dr])

    z = lambda *shape: jnp.zeros(shape, F32)

    def pair_in(dr):
        w_re, w_im = state_in(dr)
        a_re, b_re_, a_im, b_im_ = w_re[0::2], w_re[1::2], w_im[0::2], w_im[1::2]
        zz = z(g // 2, q * ni, p)
        top = jnp.concatenate([a_re, zz, a_im, zz], axis=-1)
        bot = jnp.concatenate([zz, b_re_, zz, b_im_], axis=-1)
        return jnp.concatenate([top, bot], axis=1)

    def pair_out(dr):
        w_re, w_im = state_out(dr)
        zz = z(g // 2, p, q * ni)
        rows = [jnp.concatenate([w_re[0::2], zz], axis=-1), jnp.concatenate([zz, w_re[1::2]], axis=-1),
                jnp.concatenate([w_im[0::2], zz], axis=-1), jnp.concatenate([zz, w_im[1::2]], axis=-1)]
        return jnp.concatenate(rows, axis=1)

    win = jnp.stack([pair_in(0), pair_in(1)]).astype(BF16)
    wout = jnp.stack([pair_out(0), pair_out(1)]).astype(BF16)
    ng = S5_STEP_GROUPS
    lamq = jnp.stack([pr[q].reshape(2, g // ng, ng * p), pi[q].reshape(2, g // ng, ng * p)], axis=2)
    return toep, win, wout, lamq


def _rope_tables(n_tokens):
    rows = n_tokens // GRID_W
    row = jnp.repeat(jnp.arange(rows), GRID_W).astype(F32)
    col = jnp.tile(jnp.arange(GRID_W), rows).astype(F32)
    n_freq = ROPE_DIM // 4
    inv_freq = ROPE_BASE ** (-jnp.arange(n_freq, dtype=F32) / n_freq)
    ang = jnp.concatenate([row[:, None] * inv_freq, col[:, None] * inv_freq], axis=-1)
    cos, sin = jnp.cos(ang), jnp.sin(ang)
    z = jnp.zeros((n_tokens, 128 - ROPE_DIM), F32)
    return (jnp.concatenate([cos, cos, z], axis=-1), jnp.concatenate([-sin, sin, z], axis=-1))


def _router_halves(w_router):
    wt = w_router.T
    hi = wt.astype(BF16)
    lo = (wt - hi.astype(F32)).astype(BF16)
    return jnp.concatenate([hi, lo], axis=0)


def _split_pairs(w):
    ev, od = w[..., 0::2], w[..., 1::2]
    return jnp.concatenate([ev, od], axis=-1), jnp.concatenate([od, ev], axis=-1)


def _mla_weights(w_dqkv, w_uq, w_ukv):
    kp, kps = _split_pairs(w_dqkv[:, Q_LORA + KV_LORA:])
    wd = jnp.concatenate([w_dqkv[:, :Q_LORA + KV_LORA], kp, kps], axis=-1).astype(BF16)
    wq3 = w_uq.reshape(Q_LORA, MLA_HEADS, NOPE_DIM + ROPE_DIM)
    qp, qps = _split_pairs(wq3[:, :, NOPE_DIM:])
    wq = jnp.concatenate([wq3[:, :, :NOPE_DIM].reshape(Q_LORA, -1), qp.reshape(Q_LORA, -1),
                          qps.reshape(Q_LORA, -1)], axis=-1).astype(BF16)
    wkv3 = w_ukv.reshape(KV_LORA, MLA_HEADS, NOPE_DIM + V_DIM)
    wkv = jnp.concatenate([wkv3[:, :, :NOPE_DIM].reshape(KV_LORA, -1),
                           wkv3[:, :, NOPE_DIM:].reshape(KV_LORA, -1)], axis=-1).astype(BF16)
    return wd, wq, wkv


@jax.jit
def kernel(x, c, ctx, c_ctx, ada_w, ada_b, norm_g, mla_w_dqkv, mla_g_q, mla_g_kv, mla_w_uq, mla_w_ukv, mla_w_o, s5_lam_re, s5_lam_im, s5_log_step, s5_b_re, s5_b_im, s5_c_re, s5_c_im, s5_d, s5_w_glu, s5_b_glu, moe_w_router, moe_bias, moe_w_gate, moe_w_up, moe_w_down, sh_w_gate, sh_w_up, sh_w_down):
    b, l, d = x.shape
    n_ctx = ctx.shape[1]
    assert ada_w.shape[0] == 2 and b % 8 == 0
    ta = 256
    tm = 512
    tb = 512
    row = lambda v: v.reshape(1, -1)

    n_rows = (b + 1 + 7) // 8 * 8
    cvec = jnp.zeros((n_rows, d), F32).at[:b].set(c).at[b].set(c_ctx)
    mods = _ada_mods(cvec, ada_w, ada_b)

    def shared_weights(i):
        shgu = jnp.concatenate([sh_w_gate[i], sh_w_up[i]], axis=-1).astype(BF16)
        return shgu, sh_w_down[i].astype(BF16)

    mod_lat = mods[0, :b].reshape(b, 1, N_MOD * d)
    mod_ctx = mods[0, b].reshape(1, 1, N_MOD * d)
    wd, wq, wkv = _mla_weights(mla_w_dqkv[0], mla_w_uq[0], mla_w_ukv[0])
    cos_l, sin_l = _rope_tables(l)
    cos_c = jnp.concatenate([jnp.ones((n_ctx, ROPE_DIM), F32), jnp.zeros((n_ctx, 128 - ROPE_DIM), F32)], -1)
    sin_c = jnp.zeros((n_ctx, 128), F32)
    pre = functools.partial(_pre_mla, g0=row(norm_g[0, 0]), wd=wd, gq=row(mla_g_q[0]), gkv=row(mla_g_kv[0]),
                            wq=wq, wkv=wkv, tm=ta)
    q_c, k_c, v_c = pre(ctx, mod_ctx, cos_t=cos_c, sin_t=sin_c)
    q_l, k_l, v_l = pre(x, mod_lat, cos_t=cos_l, sin_t=sin_l)
    o_l = _attention(q_l, [k_c, k_l], [v_c, v_l], 2 * ta)
    o_c = _attention(q_c, [k_c], [v_c], n_ctx)

    wo = mla_w_o[0].astype(BF16)
    wr_t = _router_halves(moe_w_router[0])
    g1, g2, g3 = row(norm_g[0, 1]), row(norm_g[0, 2]), row(norm_g[0, 3])
    post = functools.partial(_post_mixer, _post_proj_kernel, consts=[wo], g1=g1, g2=g2, wr_t=wr_t, tm=tm,
                             name="post_mla")
    o_spec = pl.BlockSpec((tm, o_l.shape[-1]), lambda i: (i, 0))
    n_moe = b * (n_ctx + l)
    x1_c, fin, lg = post([(o_c.reshape(b * n_ctx, -1), o_spec)], x=ctx.reshape(b * n_ctx, d),
                         n_tok=b * n_ctx, x_off=0, mods=mod_ctx, rows_per_mod=b * n_ctx, moe_total=n_moe)
    tt = tm // b
    mod_lat_tm = mods[0, :b][None]
    x1_l, fin, lg = _post_mixer(
        _post_proj_tm_kernel, [(o_l, pl.BlockSpec((b, tt, o_l.shape[-1]), lambda i: (0, i, 0)))], [wo],
        x=x, n_tok=b * l, x_off=0, mods=mod_lat_tm, g1=g1, g2=g2, wr_t=wr_t, tm=tm, rows_per_mod=b * l,
        name="post_mla", moe_total=n_moe, moe_off=b * n_ctx, prev=(fin, lg),
        x_spec=pl.BlockSpec((b, tt, d), lambda i: (0, i, 0)))
    shgu, shd = shared_weights(0)
    lat_a = (l // 2) * b
    n_a = b * n_ctx + lat_a
    moe = functools.partial(_moe, fin, lg, moe_bias[0], moe_w_gate, moe_w_up, moe_w_down, 0, tb)
    yk_a, gates_a = moe(0, n_a)
    yk_b, gates_b = moe(n_a, n_moe - n_a)
    comb = functools.partial(_combine, fin=fin, shgu=shgu, shd=shd, g3=g3, tm=tm)
    x2_c = comb(yk_a, gates_a, x1=x1_c, mods=mod_ctx, rows_per_mod=b * n_ctx, n_tok=b * n_ctx, x_off=0, yk_off=0,
                fin_off=0)

    n_all = n_ctx + l
    mod_lat = mods[1, :b]
    mod_ctx = jnp.broadcast_to(mods[1, b][None], (b, N_MOD * d))
    g0 = row(norm_g[1, 0])
    h, xt = _pre_s5(x2_c.reshape(b, n_ctx, d), mod_ctx, g0, n_all, 0, None, tt)
    comb_l = functools.partial(comb, x1=x1_l, mods=mod_lat_tm, rows_per_mod=b * l, out_rows=n_all * b,
                               nxt=(mod_lat, g0))
    xt, h = comb_l(yk_a, gates_a, n_tok=lat_a, x_off=0, yk_off=b * n_ctx, fin_off=b * n_ctx,
                   out_off=n_ctx * b, prev=(xt, h))
    xt, h = comb_l(yk_b, gates_b, n_tok=b * l - lat_a, x_off=lat_a, yk_off=0, fin_off=n_a,
                   out_off=n_ctx * b + lat_a, prev=(xt, h))
    toep, win, wout, lamq = _s5c_params(s5_lam_re[0], s5_lam_im[0], s5_log_step[0], s5_b_re[0], s5_b_im[0],
                                        s5_c_re[0], s5_c_im[0])
    ng = d // LANES
    yc = _s5_chunked(h, toep, win, wout, lamq, b, n_ctx)
    g1, g2, g3 = row(norm_g[1, 1]), row(norm_g[1, 2]), row(norm_g[1, 3])
    lat0 = n_ctx * b // tm
    x1, fin, lg = _post_mixer(
        _post_glu_kernel,
        [(h, pl.BlockSpec((ng, tm, LANES), lambda i: (0, i + lat0, 0))),
         (yc, pl.BlockSpec((yc.shape[0], tm // S5_Q, yc.shape[-1]), lambda i: (0, i, 0)))],
        [row(s5_d[0]), s5_w_glu[0].astype(BF16), row(s5_b_glu[0])],
        x=xt, n_tok=l * b, x_off=n_ctx * b, mods=mod_lat[None], g1=g1, g2=g2, wr_t=_router_halves(moe_w_router[1]), tm=tm,
        rows_per_mod=l * b, name="post_s5")
    shgu, shd = shared_weights(1)
    n_h = (l // 2) * b
    moe = functools.partial(_moe, fin, lg, moe_bias[1], moe_w_gate, moe_w_up, moe_w_down, 1, tb)
    yk_a, gates_a = moe(0, n_h)
    yk_b, gates_b = moe(n_h, l * b - n_h)
    comb = functools.partial(_combine, fin=fin, shgu=shgu, shd=shd, x1=x1, mods=mod_lat[None], g3=g3, tm=tm,
                             rows_per_mod=l * b, yk_off=0, batch_out=b)
    out = comb(yk_a, gates_a, n_tok=n_h, x_off=0, fin_off=0)
    return comb(yk_b, gates_b, n_tok=l * b - n_h, x_off=n_h, fin_off=n_h, prev=out)
```

```python
import functools

import jax
import jax.numpy as jnp
from jax import lax
from jax.experimental import pallas as pl
from jax.experimental.pallas import tpu as pltpu
from jax.experimental.pallas import tpu_sc as plsc

F32 = jnp.float32
BF16 = jnp.bfloat16
U32 = jnp.uint32

N_MOD = 6
NORM_EPS = 1e-6
LOG2_E = 1.4426950408889634
GRID_W = 64
MLA_HEADS = 8
Q_LORA = 384
KV_LORA = 256
NOPE_DIM = 128
ROPE_DIM = 64
V_DIM = 128
V_PAD = 256
ROPE_BASE = 10000.0
QK_PAD = 256
S5_GROUP = 16
S5_STATE = 64
N_EXPERTS = 64
TOP_K = 8
N_EXPERT_GROUPS = 8
TOPK_GROUPS = 4
D_EXPERT = 256
ROUTED_SCALE = 2.5

VMEM_LIMIT = 56 * 1024 * 1024


def _cparams(sem):
    return pltpu.CompilerParams(dimension_semantics=sem, vmem_limit_bytes=VMEM_LIMIT)


def _rms(x, g):
    return x * lax.rsqrt(jnp.mean(x * x, axis=-1, keepdims=True) + NORM_EPS) * g


def _rows(v, like):
    r = v.shape[0]
    if r == 1:
        return v
    tm, d = like.shape
    return jnp.broadcast_to(v[None], (tm // r, r, d)).reshape(tm, d)


def _mod_chunk(mod_ref, j, d):
    return mod_ref[:, j * d:(j + 1) * d]


def _dot(a, b):
    return jnp.dot(a, b, preferred_element_type=F32)


PACK_ROWS = 4
LANES = 128


def _pack_store(ref, val, lead=(), row0=0):
    n = val.shape[0]
    bits = lax.bitcast_convert_type(val.astype(BF16).astype(F32), U32)
    for s in range(PACK_ROWS):
        lo = bits[:, s * LANES:(s + 1) * LANES] >> 16
        hi = bits[:, (s + PACK_ROWS) * LANES:(s + PACK_ROWS + 1) * LANES] & jnp.uint32(0xFFFF0000)
        ref[lead + (pl.ds(row0 * PACK_ROWS + s, n, stride=PACK_ROWS), slice(None))] = lo | hi


def _unpack_load(ref, n, lead=(), row0=0):
    los, his = [], []
    for s in range(PACK_ROWS):
        w = ref[lead + (pl.ds(row0 * PACK_ROWS + s, n, stride=PACK_ROWS), slice(None))]
        los.append(lax.bitcast_convert_type(w << 16, F32))
        his.append(lax.bitcast_convert_type(w & jnp.uint32(0xFFFF0000), F32))
    return los + his


def _ada_kernel(c_ref, w_ref, b_ref, o_ref):
    c = c_ref[...]
    s = c * jax.nn.sigmoid(c)
    o_ref[...] = jnp.dot(s, w_ref[...], preferred_element_type=F32,
                         precision=lax.Precision.HIGHEST) + b_ref[...]


def _ada_mods(cvec, ada_w, ada_b):
    depth, d, n = ada_w.shape
    rows = cvec.shape[0]
    tn = 1536
    return pl.pallas_call(
        _ada_kernel,
        out_shape=jax.ShapeDtypeStruct((depth, rows, n), F32),
        grid=(depth, n // tn),
        in_specs=[pl.BlockSpec((rows, d), lambda l, j: (0, 0)),
                  pl.BlockSpec((None, d, tn), lambda l, j: (l, 0, j)),
                  pl.BlockSpec((None, 1, tn), lambda l, j: (l, 0, j))],
        out_specs=pl.BlockSpec((None, rows, tn), lambda l, j: (l, 0, j)),
        compiler_params=_cparams(("arbitrary", "arbitrary")),
        name="ada_mods",
    )(cvec, ada_w, ada_b.reshape(depth, 1, n))


def _pre_mla_kernel(x_ref, mod_ref, g0_ref, wd_ref, gq_ref, gkv_ref, wq_ref, wkv_ref, cos_ref, sin_ref,
                    q_ref, k_ref, v_ref):
    d = x_ref.shape[-1]
    x = x_ref[...]
    h = _rms(x, g0_ref[...]) * (1.0 + _mod_chunk(mod_ref, 1, d)) + _mod_chunk(mod_ref, 0, d)
    a = _dot(h.astype(BF16), wd_ref[...])
    cq = _rms(a[:, :Q_LORA], gq_ref[...])
    ckv = _rms(a[:, Q_LORA:Q_LORA + KV_LORA], gkv_ref[...])
    rd = ROPE_DIM
    cos = cos_ref[:, 0:rd]
    sin = sin_ref[:, 0:rd]
    o = Q_LORA + KV_LORA
    k_rot = (a[:, o:o + rd] * cos + a[:, o + rd:o + 2 * rd] * sin).astype(BF16)
    qa = _dot(cq.astype(BF16), wq_ref[...])
    kva = _dot(ckv.astype(BF16), wkv_ref[...])
    hw = MLA_HEADS * 128
    hr = MLA_HEADS * rd
    zpad = jnp.zeros((x.shape[0], QK_PAD - NOPE_DIM - rd), BF16)
    scale = (NOPE_DIM + ROPE_DIM) ** -0.5 * LOG2_E
    for hd in range(MLA_HEADS):
        lo = hd * 128
        q_rot = qa[:, hw + hd * rd:hw + (hd + 1) * rd] * cos + qa[:, hw + hr + hd * rd:hw + hr + (hd + 1) * rd] * sin
        q_ref[:, hd * QK_PAD:hd * QK_PAD + 128] = (qa[:, lo:lo + 128] * scale).astype(BF16)
        q_ref[:, hd * QK_PAD + 128:hd * QK_PAD + 128 + rd] = (q_rot * scale).astype(BF16)
        q_ref[:, hd * QK_PAD + 128 + rd:(hd + 1) * QK_PAD] = zpad
        k_ref[:, hd * QK_PAD:hd * QK_PAD + 128] = kva[:, lo:lo + 128].astype(BF16)
        k_ref[:, hd * QK_PAD + 128:hd * QK_PAD + 128 + rd] = k_rot
        k_ref[:, hd * QK_PAD + 128 + rd:(hd + 1) * QK_PAD] = zpad
        v_ref[:, hd * V_PAD:hd * V_PAD + V_DIM] = kva[:, hw + lo:hw + lo + 128].astype(BF16)
        v_ref[:, hd * V_PAD + V_DIM:(hd + 1) * V_PAD] = jnp.ones((x.shape[0], V_PAD - V_DIM), BF16)


def _pre_mla(x, mods, g0, wd, gq, gkv, wq, wkv, cos_t, sin_t, tm):
    b, n, d = x.shape
    nb_mod = mods.shape[0]
    full = lambda a: pl.BlockSpec(a.shape, lambda i, j: (0,) * a.ndim)
    mod_map = (lambda i, j: (i, 0, 0)) if nb_mod > 1 else (lambda i, j: (0, 0, 0))
    qk_w = MLA_HEADS * QK_PAD
    v_w = MLA_HEADS * V_PAD
    return pl.pallas_call(
        _pre_mla_kernel,
        out_shape=(jax.ShapeDtypeStruct((b, n, qk_w), BF16),
                   jax.ShapeDtypeStruct((b, n, qk_w), BF16),
                   jax.ShapeDtypeStruct((b, n, v_w), BF16)),
        grid=(b, n // tm),
        in_specs=[pl.BlockSpec((None, tm, d), lambda i, j: (i, j, 0)),
                  pl.BlockSpec((None, 1, mods.shape[-1]), mod_map),
                  full(g0), full(wd), full(gq), full(gkv), full(wq), full(wkv),
                  pl.BlockSpec((tm, 128), lambda i, j: (j, 0)),
                  pl.BlockSpec((tm, 128), lambda i, j: (j, 0))],
        out_specs=(pl.BlockSpec((None, tm, qk_w), lambda i, j: (i, j, 0)),
                   pl.BlockSpec((None, tm, qk_w), lambda i, j: (i, j, 0)),
                   pl.BlockSpec((None, tm, v_w), lambda i, j: (i, j, 0))),
        compiler_params=_cparams(("arbitrary", "arbitrary")),
        name="pre_mla",
    )(x, mods, g0, wd, gq, gkv, wq, wkv, cos_t, sin_t)


def _attn_kernel(*refs, n_seg):
    q_ref = refs[0]
    k_refs = refs[1:1 + n_seg]
    v_refs = refs[1 + n_seg:1 + 2 * n_seg]
    o_ref = refs[1 + 2 * n_seg]
    nt = (((1,), (1,)), ((), ()))

    def scores(hd):
        q = q_ref[:, hd * QK_PAD:(hd + 1) * QK_PAD]
        return [lax.dot_general(q, k[:, hd * QK_PAD:(hd + 1) * QK_PAD], nt, preferred_element_type=F32)
                for k in k_refs]

    nxt = scores(0)
    for hd in range(MLA_HEADS):
        ss = nxt
        if hd + 1 < MLA_HEADS:
            nxt = scores(hd + 1)
        m = ss[0].max(axis=-1, keepdims=True)
        for s in ss[1:]:
            m = jnp.maximum(m, s.max(axis=-1, keepdims=True))
        acc = None
        for s, v in zip(ss, v_refs):
            pv = _dot(jnp.exp2((s - m).astype(BF16)), v[:, hd * V_PAD:(hd + 1) * V_PAD])
            acc = pv if acc is None else acc + pv
        o_ref[:, hd * V_DIM:(hd + 1) * V_DIM] = (acc[:, :V_DIM] / acc[:, V_DIM:V_DIM + 1]).astype(BF16)


def _attention(q, ks, vs, tq):
    b, nq, qk_w = q.shape
    v_w = MLA_HEADS * V_DIM
    kv_spec = lambda a: pl.BlockSpec((None,) + a.shape[1:], lambda i, j: (i, 0, 0))
    return pl.pallas_call(
        functools.partial(_attn_kernel, n_seg=len(ks)),
        out_shape=jax.ShapeDtypeStruct((b, nq, v_w), BF16),
        grid=(b, nq // tq),
        in_specs=[pl.BlockSpec((None, tq, qk_w), lambda i, j: (i, j, 0))]
                 + [kv_spec(a) for a in ks] + [kv_spec(a) for a in vs],
        out_specs=pl.BlockSpec((None, tq, v_w), lambda i, j: (i, j, 0)),
        compiler_params=_cparams(("arbitrary", "arbitrary")),
        name="mla_attention",
    )(q, *ks, *vs)


SUB_ROWS = 256


def _sub_tiles(n):
    return [slice(r, r + SUB_ROWS) for r in range(0, n, SUB_ROWS)]


def _post_core(o, x, rows, mod_ref, g1_ref, g2_ref, wr_ref, x1_ref, fin_ref, lg_ref):
    d = x.shape[-1]
    ne = lg_ref.shape[0]
    gate = _rows(_mod_chunk(mod_ref, 2, d), x)
    shift = _rows(_mod_chunk(mod_ref, 3, d), x)
    scale = _rows(_mod_chunk(mod_ref, 4, d), x)
    x1 = x + gate * _rms(o, g1_ref[...])
    fin = _rms(x1, g2_ref[...]) * (1.0 + scale) + shift
    x1_ref[rows, :] = x1
    _pack_store(fin_ref, fin, row0=rows.start)
    nt = (((1,), (1,)), ((), ()))
    f_hi = fin.astype(BF16)
    f_lo = (fin - f_hi.astype(F32)).astype(BF16)
    r_hi = lax.dot_general(wr_ref[...], f_hi, nt, preferred_element_type=F32)
    r_lo = lax.dot_general(wr_ref[0:ne, :], f_lo, nt, preferred_element_type=F32)
    lg_ref[:, rows] = r_hi[:ne] + r_hi[ne:] + r_lo


def _post_proj_kernel(o_ref, wo_ref, x_ref, mod_ref, g1_ref, g2_ref, wr_ref, *rest):
    x1_ref, fin_ref, lg_ref = rest[-3:]
    for rows in _sub_tiles(x_ref.shape[0]):
        o = _dot(o_ref[rows, :], wo_ref[...])
        _post_core(o, x_ref[rows, :], rows, mod_ref, g1_ref, g2_ref, wr_ref, x1_ref, fin_ref, lg_ref)


def _post_proj_tm_kernel(o_ref, wo_ref, x_ref, mod_ref, g1_ref, g2_ref, wr_ref, *rest):
    x1_ref, fin_ref, lg_ref = rest[-3:]
    nb, tt, d = x_ref.shape
    ts = SUB_ROWS // nb
    for t0 in range(0, tt, ts):
        o = _dot(o_ref[:, t0:t0 + ts, :].reshape(nb * ts, o_ref.shape[-1]), wo_ref[...])
        o = jnp.swapaxes(o.reshape(nb, ts, d), 0, 1).reshape(ts * nb, d)
        x = jnp.swapaxes(x_ref[:, t0:t0 + ts, :], 0, 1).reshape(ts * nb, d)
        _post_core(o, x, slice(t0 * nb, (t0 + ts) * nb), mod_ref, g1_ref, g2_ref, wr_ref, x1_ref, fin_ref, lg_ref)


def _chunk_to_rows(yc_ref, c, nb):
    q, grp = S5_Q, S5_GROUP
    per_tile = LANES // grp
    lane_grp = lax.broadcasted_iota(jnp.int32, (nb, LANES), 1) // grp
    n_pairs = yc_ref.shape[0]
    pieces = [[yc_ref[p, c * nb:(c + 1) * nb, lt * LANES:(lt + 1) * LANES] for lt in range(2 * q // per_tile)]
              for p in range(n_pairs)]
    out_rows = []
    for t in range(q):
        tiles = []
        for lb in range(2 * n_pairs // per_tile):
            dest = None
            for g8 in range(per_tile):
                g = lb * per_tile + g8
                piece = pieces[g // 2][(g % 2) * (q // per_tile) + t // per_tile]
                r = pltpu.roll(piece, ((g8 - t % per_tile) * grp) % LANES, 1)
                dest = r if dest is None else jnp.where(lane_grp == g8, r, dest)
            tiles.append(dest)
        out_rows.append(jnp.concatenate(tiles, axis=-1))
    return jnp.concatenate(out_rows, axis=0)


def _post_glu_kernel(h_ref, yc_ref, dsk_ref, wg_ref, bg_ref, x_ref, mod_ref, g1_ref, g2_ref, wr_ref,
                     *rest):
    x1_ref, fin_ref, lg_ref = rest[-3:]
    d = x_ref.shape[-1]
    nb = mod_ref.shape[0]
    assert SUB_ROWS == S5_Q * nb
    for ci, rows in enumerate(_sub_tiles(x_ref.shape[0])):
        h = jnp.concatenate([h_ref[g, rows, :] for g in range(h_ref.shape[0])], axis=-1).astype(F32)
        y = h * dsk_ref[...] + _chunk_to_rows(yc_ref, ci, nb)
        z = _dot(jax.nn.gelu(y, approximate=True).astype(BF16), wg_ref[...]) + bg_ref[...]
        o = z[:, :d] * jax.nn.sigmoid(z[:, d:])
        _post_core(o, x_ref[rows, :], rows, mod_ref, g1_ref, g2_ref, wr_ref, x1_ref, fin_ref, lg_ref)


def _post_mixer(kernel, tok_inputs, consts, x, n_tok, x_off, mods, g1, g2, wr_t, tm, rows_per_mod, name,
                moe_total=None, moe_off=0, prev=None, x_spec=None):
    d = x.shape[-1]
    ne = wr_t.shape[0] // 2
    moe_total = n_tok if moe_total is None else moe_total
    tiles_per_mod = rows_per_mod // tm
    xo, mo = x_off // tm, moe_off // tm
    full = lambda a: pl.BlockSpec(a.shape, lambda i: (0,) * a.ndim)
    tile = pl.BlockSpec((tm, d), lambda i: (i, 0))
    mod_spec = pl.BlockSpec((None,) + mods.shape[1:], lambda i: (i // tiles_per_mod, 0, 0))
    x_spec = pl.BlockSpec((tm, d), lambda i: (i + xo, 0)) if x_spec is None else x_spec
    in_specs = ([spec for _, spec in tok_inputs] + [full(a) for a in consts]
                + [x_spec, mod_spec, full(g1), full(g2), full(wr_t)])
    args = [a for a, _ in tok_inputs] + list(consts) + [x, mods, g1, g2, wr_t]
    aliases = {}
    if prev is not None:
        aliases = {len(args): 1, len(args) + 1: 2}
        in_specs += [pl.BlockSpec(memory_space=pl.ANY)] * 2
        args += list(prev)
    return pl.pallas_call(
        kernel,
        out_shape=(jax.ShapeDtypeStruct((n_tok, d), F32),
                   jax.ShapeDtypeStruct((moe_total * PACK_ROWS, LANES), U32),
                   jax.ShapeDtypeStruct((ne, moe_total), F32)),
        grid=(n_tok // tm,),
        in_specs=in_specs,
        out_specs=(tile, pl.BlockSpec((tm * PACK_ROWS, LANES), lambda i: (i + mo, 0)),
                   pl.BlockSpec((ne, tm), lambda i: (0, i + mo))),
        input_output_aliases=aliases,
        compiler_params=_cparams(("arbitrary",)),
        name=name,
    )(*args)


def _route_kernel(lg_ref, bias_ref, eidx_ref, gate_ref, rank_ref, cnt_ref, tri_ref, base_ref):
    i = pl.program_id(0)
    ne, tt = lg_ref.shape
    gsz = ne // N_EXPERT_GROUPS
    shp = (N_EXPERT_GROUPS, gsz, tt)
    neg = -jnp.inf

    @pl.when(i == 0)
    def _():
        base_ref[...] = jnp.zeros_like(base_ref)
        r = lax.broadcasted_iota(jnp.int32, (tt, tt), 0)
        c = lax.broadcasted_iota(jnp.int32, (tt, tt), 1)
        tri_ref[...] = (r < c).astype(BF16)

    scores = jax.nn.sigmoid(lg_ref[...])
    s3 = scores.reshape(shp)
    b3 = (scores + bias_ref[...]).reshape(shp)
    io_e = lax.broadcasted_iota(jnp.int32, shp, 1)
    io_g = lax.broadcasted_iota(jnp.int32, shp, 0)
    io_flat = io_g * gsz + io_e
    m1 = b3.max(axis=1, keepdims=True)
    i1 = jnp.where(b3 == m1, io_e, gsz).min(axis=1, keepdims=True)
    m2 = jnp.where(io_e == i1, neg, b3).max(axis=1, keepdims=True)
    cur = jnp.broadcast_to(m1 + m2, shp)
    gsel = jnp.zeros(shp, jnp.bool_)
    for _ in range(TOPK_GROUPS):
        m = cur.max(axis=0, keepdims=True)
        gi = jnp.where(cur == m, io_g, N_EXPERT_GROUPS).min(axis=0, keepdims=True)
        hit = io_g == gi
        gsel = jnp.logical_or(gsel, hit)
        cur = jnp.where(hit, neg, cur)
    cand = jnp.where(gsel, b3, neg)
    sel = jnp.zeros(shp, jnp.bool_)
    eids, gts = [], []
    for _ in range(TOP_K):
        m = cand.max(axis=0, keepdims=True).max(axis=1, keepdims=True)
        ei = jnp.where(cand == m, io_flat, ne).min(axis=0, keepdims=True).min(axis=1, keepdims=True)
        hit = io_flat == ei
        gts.append(jnp.where(hit, s3, 0.0).sum(axis=0, keepdims=True).sum(axis=1, keepdims=True))
        eids.append(ei)
        sel = jnp.logical_or(sel, hit)
        cand = jnp.where(hit, neg, cand)
    gsum = gts[0]
    for g in gts[1:]:
        gsum = gsum + g
    self32 = sel.astype(F32).reshape(ne, tt)
    cnt = _dot(self32.astype(BF16), tri_ref[...]) + base_ref[...]
    cnt3 = cnt.reshape(shp)
    for k in range(TOP_K):
        hit = io_flat == eids[k]
        rk = jnp.where(hit, cnt3, 0.0).sum(axis=0, keepdims=True).sum(axis=1, keepdims=True)
        rank_ref[k:k + 1, :] = rk.reshape(1, tt).astype(jnp.int32)
        eidx_ref[k:k + 1, :] = eids[k].reshape(1, tt)
        gate_ref[k:k + 1, :] = (gts[k] / gsum * ROUTED_SCALE).reshape(1, tt)
    base_new = base_ref[...] + self32.sum(axis=1, keepdims=True)
    base_ref[...] = base_new
    cnt_ref[...] = jnp.broadcast_to(base_new, cnt_ref.shape)


def _route(logits_t, bias, tt, tok0, t):
    ne = logits_t.shape[0]
    off = tok0 // tt
    out_i = jax.ShapeDtypeStruct((TOP_K, t), jnp.int32)
    row = pl.BlockSpec((TOP_K, tt), lambda i: (0, i))
    return pl.pallas_call(
        _route_kernel,
        out_shape=(out_i, jax.ShapeDtypeStruct((TOP_K, t), F32), out_i,
                   jax.ShapeDtypeStruct((ne, 128), F32)),
        grid=(t // tt,),
        in_specs=[pl.BlockSpec((ne, tt), lambda i: (0, i + off)),
                  pl.BlockSpec((ne, 1), lambda i: (0, 0))],
        out_specs=(row, row, row, pl.BlockSpec((ne, 128), lambda i: (0, 0))),
        scratch_shapes=[pltpu.VMEM((tt, tt), BF16), pltpu.VMEM((ne, 1), F32)],
        compiler_params=_cparams(("arbitrary",)),
        name="moe_route",
    )(logits_t, bias.reshape(ne, 1))


def _dest_kernel(eidx_ref, rank_ref, start_ref, dest_ref):
    kk, tt = eidx_ref.shape
    ne = start_ref.shape[0]
    n_chunk, _, r = dest_ref.shape
    io_e = lax.broadcasted_iota(jnp.int32, (ne, tt), 0)
    start = start_ref[...]
    for k in range(kk):
        hit = io_e == eidx_ref[k:k + 1, :]
        dk = jnp.where(hit, start, 0).sum(axis=0, keepdims=True) + rank_ref[k:k + 1, :]
        for c in range(n_chunk):
            dest_ref[c, k:k + 1, :] = dk[:, c * r:(c + 1) * r]


def _dest_rows(eidx_t, rank_t, start, tt, r):
    kk, t = eidx_t.shape
    ne = start.shape[0]
    return pl.pallas_call(
        _dest_kernel,
        out_shape=jax.ShapeDtypeStruct((t // r, kk, r), jnp.int32),
        grid=(t // tt,),
        in_specs=[pl.BlockSpec((kk, tt), lambda i: (0, i)),
                  pl.BlockSpec((kk, tt), lambda i: (0, i)),
                  pl.BlockSpec((ne, 1), lambda i: (0, 0))],
        out_specs=pl.BlockSpec((tt // r, kk, r), lambda i: (i, 0, 0)),
        compiler_params=_cparams(("arbitrary",)),
        name="moe_dest",
    )(eidx_t, rank_t, start.reshape(ne, 1))


SC_CHUNK = 64


def _sc_mesh():
    return plsc.VectorSubcoreMesh(core_axis_name="c", subcore_axis_name="s")


def _sc_workers():
    info = plsc.get_sparse_core_info()
    return info.num_cores, info.num_cores * info.num_subcores


def _sc_scatter_rows(rows, dest, n_out, row0=0):
    n_chunk, kk, r = dest.shape
    nc, nw = _sc_workers()
    cpw = n_chunk // nw
    assert cpw * nw == n_chunk and cpw % 2 == 0 and row0 % r == 0 and row0 + n_chunk * r <= rows.shape[0]

    @functools.partial(
        pl.kernel, mesh=_sc_mesh(),
        out_type=jax.ShapeDtypeStruct((n_out,) + rows.shape[1:], rows.dtype),
        scratch_types=[pltpu.VMEM((2, kk, r), jnp.int32), pltpu.VMEM((2, r) + rows.shape[1:], rows.dtype),
                       pltpu.SemaphoreType.DMA((2,)), pltpu.SemaphoreType.DMA((2,))])
    def scatter(rows_hbm, dest_hbm, out_hbm, idx_v, rows_v, load_sem, scat_sem):
        c0 = (lax.axis_index("s") * nc + lax.axis_index("c")) * cpw

        def loads(c, b):
            return (pltpu.make_async_copy(dest_hbm.at[c], idx_v.at[b], load_sem.at[b]),
                    pltpu.make_async_copy(rows_hbm.at[pl.ds(row0 + c * r, r)], rows_v.at[b], load_sem.at[b]))

        def scat(b, k):
            return pltpu.make_async_copy(rows_v.at[b], out_hbm.at[idx_v.at[b, k]], scat_sem.at[b])

        for cp in loads(c0, 0):
            cp.start()

        @pl.loop(0, cpw, step=2)
        def _(ci):
            for b in range(2):
                c = c0 + ci + b
                for cp in loads(c, b):
                    cp.wait()
                for k in range(kk):
                    scat(b, k).start()

                @pl.when(ci + b >= 1)
                def _():
                    for k in range(kk):
                        scat(1 - b, k).wait()

                @pl.when(ci + b + 1 < cpw)
                def _():
                    for cp in loads(c + 1, 1 - b):
                        cp.start()

        for k in range(kk):
            scat((cpw - 1) % 2, k).wait()

    return scatter(rows, dest)


def _sc_gather_rows(src, dest):
    n_chunk, kk, r = dest.shape
    t = n_chunk * r
    nc, nw = _sc_workers()
    cpw = n_chunk // nw
    nbuf = 3
    assert cpw * nw == n_chunk and kk > nbuf

    @functools.partial(
        pl.kernel, mesh=_sc_mesh(),
        out_type=jax.ShapeDtypeStruct((kk, t) + src.shape[1:], src.dtype),
        scratch_types=[pltpu.VMEM((kk, r), jnp.int32), pltpu.VMEM((nbuf, r) + src.shape[1:], src.dtype),
                       pltpu.SemaphoreType.DMA((nbuf,)), pltpu.SemaphoreType.DMA((nbuf,))])
    def gather(src_hbm, dest_hbm, out_hbm, idx_v, rows_v, get_sem, put_sem):
        c0 = (lax.axis_index("s") * nc + lax.axis_index("c")) * cpw

        @pl.loop(0, cpw)
        def _(ci):
            c = c0 + ci
            pltpu.sync_copy(dest_hbm.at[c], idx_v)

            def get(k):
                return pltpu.make_async_copy(src_hbm.at[idx_v.at[k]], rows_v.at[k % nbuf], get_sem.at[k % nbuf])

            def put(k):
                return pltpu.make_async_copy(rows_v.at[k % nbuf], out_hbm.at[k, pl.ds(c * r, r)],
                                             put_sem.at[k % nbuf])

            for k in range(nbuf - 1):
                get(k).start()
            for k in range(kk):
                get(k).wait()
                put(k).start()
                if k + nbuf - 1 < kk:
                    if k >= 1:
                        put(k - 1).wait()
                    get(k + nbuf - 1).start()
            for k in range(kk - nbuf, kk):
                put(k).wait()

    return gather(src, dest)


def _expert_kernel(be_ref, nu_ref, x_ref, wg_ref, wu_ref, wd_ref, o_ref, wgu_s, wd_s):
    i = pl.program_id(0)
    tb = o_ref.shape[0] // PACK_ROWS

    @pl.when(i < nu_ref[0])
    def _():
        @pl.when(jnp.logical_or(i == 0, be_ref[i] != be_ref[jnp.maximum(i - 1, 0)]))
        def _():
            wgu_s[:, :D_EXPERT] = wg_ref[...].astype(BF16)
            wgu_s[:, D_EXPERT:] = wu_ref[...].astype(BF16)
            wd_s[...] = wd_ref[...].astype(BF16)

        x = jnp.concatenate([v.astype(BF16) for v in _unpack_load(x_ref, tb)], axis=-1)
        gu = _dot(x, wgu_s[...])
        g = gu[:, :D_EXPERT]
        h = g * jax.nn.sigmoid(g) * gu[:, D_EXPERT:]
        _pack_store(o_ref, _dot(h.astype(BF16), wd_s[...]))


def _experts(xs, blk_e, n_used, w_gate, w_up, w_down, layer, tb):
    rows = xs.shape[0] // PACK_ROWS
    _, ne, d, de = w_gate.shape
    nb = rows // tb
    row_map = lambda i, be, nu: (jnp.minimum(i, nu[0] - 1), 0)
    w_map = lambda i, be, nu: (layer, be[i], 0, 0)
    grid_spec = pltpu.PrefetchScalarGridSpec(
        num_scalar_prefetch=2,
        grid=(nb,),
        in_specs=[pl.BlockSpec((tb * PACK_ROWS, LANES), row_map),
                  pl.BlockSpec((None, None, d, de), w_map),
                  pl.BlockSpec((None, None, d, de), w_map),
                  pl.BlockSpec((None, None, de, d), w_map)],
        out_specs=pl.BlockSpec((tb * PACK_ROWS, LANES), row_map),
        scratch_shapes=[pltpu.VMEM((d, 2 * de), BF16), pltpu.VMEM((de, d), BF16)],
    )
    return pl.pallas_call(
        _expert_kernel,
        out_shape=jax.ShapeDtypeStruct(xs.shape, U32),
        grid_spec=grid_spec,
        compiler_params=_cparams(("arbitrary",)),
        name="moe_experts",
    )(blk_e, n_used, xs, w_gate, w_up, w_down)


def _combine_kernel(yk_ref, gate_ref, fin_ref, shgu_ref, shd_ref, x1_ref, mod_ref, g3_ref, *rest, fuse_next):
    if fuse_next:
        nmod_ref, ng0_ref = rest[0], rest[1]
        o_ref, h_ref = rest[-2], rest[-1]
    else:
        o_ref = rest[-1]
    tm, d = x1_ref.shape
    for rows in _sub_tiles(tm):
        n, r0 = SUB_ROWS, rows.start
        gates = gate_ref[rows, :]
        blocks = None
        for k in range(TOP_K):
            gk = gates[:, k:k + 1]
            terms = [gk * v for v in _unpack_load(yk_ref, n, lead=(k,), row0=r0)]
            blocks = terms if blocks is None else [a + b for a, b in zip(blocks, terms)]
        fin = jnp.concatenate([v.astype(BF16) for v in _unpack_load(fin_ref, n, row0=r0)], axis=-1)
        gu = _dot(fin, shgu_ref[...])
        g = gu[:, :D_EXPERT]
        hsh = g * jax.nn.sigmoid(g) * gu[:, D_EXPERT:]
        f = jnp.concatenate(blocks, axis=-1) + _dot(hsh.astype(BF16), shd_ref[...])
        x1 = x1_ref[rows, :]
        x2 = x1 + _rows(_mod_chunk(mod_ref, 5, d), x1) * _rms(f, g3_ref[...])
        if len(o_ref.shape) == 2:
            o_ref[rows, :] = x2
        else:
            nb = o_ref.shape[0]
            ts = SUB_ROWS // nb
            o_ref[:, r0 // nb:r0 // nb + ts, :] = jnp.swapaxes(x2.reshape(ts, nb, d), 0, 1)
        if fuse_next:
            hn = (_rms(x2, ng0_ref[...]) * (1.0 + _rows(_mod_chunk(nmod_ref, 1, d), x2))
                  + _rows(_mod_chunk(nmod_ref, 0, d), x2))
            for gi in range(h_ref.shape[0]):
                h_ref[gi, rows, :] = hn[:, gi * LANES:(gi + 1) * LANES].astype(BF16)


def _combine(yk, gates, fin, shgu, shd, x1, mods, g3, tm, rows_per_mod, n_tok, x_off, yk_off, fin_off,
             batch_out=0, prev=None, out_rows=None, out_off=None, nxt=None):
    t, d = x1.shape
    out_rows = t if out_rows is None else out_rows
    out_off = x_off if out_off is None else out_off
    xo, yo, fo, oo = x_off // tm, yk_off // tm, fin_off // tm, out_off // tm
    tiles_per_mod = rows_per_mod // tm
    full = lambda a: pl.BlockSpec(a.shape, lambda i: (0,) * a.ndim)
    if batch_out:
        out_shape = [jax.ShapeDtypeStruct((batch_out, out_rows // batch_out, d), F32)]
        out_specs = [pl.BlockSpec((batch_out, tm // batch_out, d), lambda i: (0, i + oo, 0))]
    else:
        out_shape = [jax.ShapeDtypeStruct((out_rows, d), F32)]
        out_specs = [pl.BlockSpec((tm, d), lambda i: (i + oo, 0))]
    in_specs = [pl.BlockSpec((TOP_K, tm * PACK_ROWS, LANES), lambda i: (0, i + yo, 0)),
                pl.BlockSpec((tm, TOP_K), lambda i: (i + yo, 0)),
                pl.BlockSpec((tm * PACK_ROWS, LANES), lambda i: (i + fo, 0)),
                full(shgu), full(shd),
                pl.BlockSpec((tm, d), lambda i: (i + xo, 0)),
                pl.BlockSpec((None,) + mods.shape[1:], lambda i: ((i + xo) // tiles_per_mod, 0, 0)),
                full(g3)]
    args = [yk, gates, fin, shgu, shd, x1, mods, g3]
    if nxt is not None:
        in_specs += [full(nxt[0]), full(nxt[1])]
        args += list(nxt)
        out_shape.append(jax.ShapeDtypeStruct((d // LANES, out_rows, LANES), BF16))
        out_specs.append(pl.BlockSpec((d // LANES, tm, LANES), lambda i: (0, i + oo, 0)))
    aliases = {}
    if prev is not None:
        for j, p in enumerate(prev if isinstance(prev, (tuple, list)) else [prev]):
            in_specs.append(pl.BlockSpec(memory_space=pl.ANY))
            aliases[len(args)] = j
            args.append(p)
    out = pl.pallas_call(
        functools.partial(_combine_kernel, fuse_next=nxt is not None),
        out_shape=tuple(out_shape),
        grid=(n_tok // tm,),
        in_specs=in_specs,
        out_specs=tuple(out_specs),
        input_output_aliases=aliases,
        compiler_params=_cparams(("arbitrary",)),
        name="moe_combine",
    )(*args)
    return out if nxt is not None else out[0]


def _moe(fin, logits_t, bias, w_gate, w_up, w_down, layer, tb, tok0, t):
    t_all = fin.shape[0] // PACK_ROWS
    ne = w_gate.shape[1]
    tt = 512
    eidx_t, gates_t, rank_t, cnt = _route(logits_t, bias, tt, tok0, t)
    counts = cnt[:, 0].astype(jnp.int32)
    padded = (counts + tb - 1) // tb * tb
    pad_end = jnp.cumsum(padded)
    pad_start = pad_end - padded
    nb = (t * TOP_K) // tb + ne
    n_used = pad_end[-1] // tb
    blk_start = jnp.arange(nb, dtype=jnp.int32) * tb
    blk = jnp.sum(pad_end[None, :] <= jnp.minimum(blk_start, pad_end[-1] - 1)[:, None], axis=1)
    blk_e = jnp.minimum(blk, ne - 1).astype(jnp.int32)
    dest = _dest_rows(eidx_t, rank_t, pad_start, tt, SC_CHUNK)
    xs = _sc_scatter_rows(fin.reshape(t_all, PACK_ROWS, LANES), dest, nb * tb, row0=tok0)
    ys = _experts(xs.reshape(nb * tb * PACK_ROWS, LANES), blk_e, n_used.reshape(1).astype(jnp.int32),
                  w_gate, w_up, w_down, layer, tb)
    yk = _sc_gather_rows(ys.reshape(nb * tb, PACK_ROWS, LANES), dest)
    return yk.reshape(TOP_K, t * PACK_ROWS, LANES), gates_t.T


def _pre_s5_kernel(x_ref, mod_ref, g0_ref, *refs):
    h_ref, xt_ref = refs[-2:]
    nb, tt, d = x_ref.shape
    x = jnp.swapaxes(x_ref[...], 0, 1).reshape(tt * nb, d)
    h = (_rms(x, g0_ref[...]) * (1.0 + _rows(_mod_chunk(mod_ref, 1, d), x))
         + _rows(_mod_chunk(mod_ref, 0, d), x))
    for g in range(h_ref.shape[0]):
        h_ref[g] = h[:, g * LANES:(g + 1) * LANES].astype(BF16)
    xt_ref[...] = x


def _pre_s5(x, mods, g0, n_total, t_off, prev, tt):
    nb, n, d = x.shape
    off = t_off // tt
    out_shape = (jax.ShapeDtypeStruct((d // LANES, n_total * nb, LANES), BF16),
                 jax.ShapeDtypeStruct((n_total * nb, d), F32))
    out_specs = (pl.BlockSpec((d // LANES, tt * nb, LANES), lambda i: (0, i + off, 0)),
                 pl.BlockSpec((tt * nb, d), lambda i: (i + off, 0)))
    in_specs = [pl.BlockSpec((nb, tt, d), lambda i: (0, i, 0)),
                pl.BlockSpec(mods.shape, lambda i: (0, 0)),
                pl.BlockSpec(g0.shape, lambda i: (0, 0))]
    args = (x, mods, g0)
    aliases = {}
    if prev is not None:
        in_specs += [pl.BlockSpec(memory_space=pl.ANY)] * 2
        args += tuple(prev)
        aliases = {3: 0, 4: 1}
    return pl.pallas_call(
        _pre_s5_kernel,
        out_shape=out_shape,
        grid=(n // tt,),
        in_specs=in_specs,
        out_specs=out_specs,
        input_output_aliases=aliases,
        compiler_params=_cparams(("arbitrary",)),
        name="pre_s5",
    )(*args)


S5_Q = 16
S5_STEP_GROUPS = 4


def _s5c_kernel(h_ref, toep_ref, win_ref, wout_ref, lamq_ref, y_ref, u_ref, sre_ref, sim_ref, xin_ref, *,
                nb, n_ctx_chunks):
    npair, n_rows, kw = u_ref.shape
    half = kw // 2
    sw = 2 * S5_STATE
    n = n_rows // nb
    ncc = n_ctx_chunks
    per_tile = LANES // S5_GROUP
    lane_grp = lax.broadcasted_iota(jnp.int32, (nb, LANES), 1) // S5_GROUP
    steps_per_block = per_tile // S5_STEP_GROUPS

    def relayout_from(gl0):
        def relayout(c, carry):
            r0 = c * (S5_Q * nb)
            pieces = [h_ref[pl.ds(pl.multiple_of(r0 + s * nb, nb), nb), :].astype(F32) for s in range(S5_Q)]
            rows = pl.ds(pl.multiple_of(c * nb, nb), nb)
            for gq in range(S5_STEP_GROUPS):
                for j in range(S5_Q // per_tile):
                    dest = None
                    for s8 in range(per_tile):
                        r = pltpu.roll(pieces[j * per_tile + s8], ((s8 - gl0 - gq) * S5_GROUP) % LANES, 1)
                        dest = r if dest is None else jnp.where(lane_grp == s8, r, dest)
                    lo = (gq % 2) * half + j * LANES
                    u_ref[gq // 2, rows, lo:lo + LANES] = dest.astype(BF16)
            return carry

        lax.fori_loop(0, n, relayout, 0, unroll=8)

    for k in range(steps_per_block):
        pl.when(pl.program_id(0) % steps_per_block == k)(functools.partial(relayout_from, k * S5_STEP_GROUPS))
    for dr in range(2):
        for p in range(npair):
            s = _dot(u_ref[p], win_ref[dr, p])
            sre_ref[dr, :, p * sw:(p + 1) * sw] = s[:, :sw]
            sim_ref[dr, :, p * sw:(p + 1) * sw] = s[:, sw:]
    w = npair * sw
    lr = [jnp.broadcast_to(lamq_ref[dr, 0:1, :], (nb, w)) for dr in range(2)]
    li = [jnp.broadcast_to(lamq_ref[dr, 1:2, :], (nb, w)) for dr in range(2)]

    def step(j, carry):
        rev = jnp.where(j < ncc, ncc - 1 - j, n - 1 - (j - ncc))
        out = []
        for dr in range(2):
            xr, xi = carry[dr]
            c = j if dr == 0 else rev
            rows = pl.ds(pl.multiple_of(c * nb, nb), nb)
            for p in range(npair):
                xin_ref[dr, rows, 2 * p * sw:(2 * p + 1) * sw] = xr[:, p * sw:(p + 1) * sw].astype(BF16)
                xin_ref[dr, rows, (2 * p + 1) * sw:(2 * p + 2) * sw] = xi[:, p * sw:(p + 1) * sw].astype(BF16)
            nr = lr[dr] * xr - li[dr] * xi + sre_ref[dr, rows, :]
            ni = lr[dr] * xi + li[dr] * xr + sim_ref[dr, rows, :]
            out.append((nr, ni))
        return tuple(out)

    zero = jnp.zeros((nb, w), F32)
    lax.fori_loop(0, n, step, ((zero, zero), (zero, zero)), unroll=2)
    lat = slice(ncc * nb, n_rows)
    for p in range(npair):
        acc = None
        for dr in range(2):
            intra = jnp.concatenate([_dot(u_ref[p, lat, 0:half], toep_ref[dr, 2 * p]),
                                     _dot(u_ref[p, lat, half:kw], toep_ref[dr, 2 * p + 1])], axis=-1)
            term = intra + _dot(xin_ref[dr, lat, 2 * p * sw:(2 * p + 2) * sw], wout_ref[dr, p])
            acc = term if acc is None else acc + term
        y_ref[p] = acc


def _s5_chunked(h, toep, win, wout, lamq, nb, n_ctx):
    ng, n_tok, _ = h.shape
    n_groups = toep.shape[1]
    pp = S5_STEP_GROUPS // 2
    steps_per_block = (LANES // S5_GROUP) // S5_STEP_GROUPS
    n_rows = n_tok // S5_Q
    ncc = n_ctx // S5_Q
    lat_rows = n_rows - ncc * nb
    kw = 2 * S5_Q * S5_GROUP
    sw = 2 * S5_STATE
    return pl.pallas_call(
        functools.partial(_s5c_kernel, nb=nb, n_ctx_chunks=ncc),
        out_shape=jax.ShapeDtypeStruct((n_groups // 2, lat_rows, kw), F32),
        grid=(n_groups // S5_STEP_GROUPS,),
        in_specs=[pl.BlockSpec((None, n_tok, LANES), lambda i: (i // steps_per_block, 0, 0),
                               pipeline_mode=pl.Buffered(1)),
                  pl.BlockSpec((2, 2 * pp, kw // 2, kw // 2), lambda i: (0, i, 0, 0)),
                  pl.BlockSpec((2, pp, kw, 2 * sw), lambda i: (0, i, 0, 0)),
                  pl.BlockSpec((2, pp, 2 * sw, kw), lambda i: (0, i, 0, 0)),
                  pl.BlockSpec((2, None, 2, pp * sw), lambda i: (0, i, 0, 0))],
        out_specs=pl.BlockSpec((pp, lat_rows, kw), lambda i: (i, 0, 0)),
        scratch_shapes=[pltpu.VMEM((pp, n_rows, kw), BF16),
                        pltpu.VMEM((2, n_rows, pp * sw), F32), pltpu.VMEM((2, n_rows, pp * sw), F32),
                        pltpu.VMEM((2, n_rows, 2 * pp * sw), BF16)],
        compiler_params=_cparams(("arbitrary",)),
        name="s5_chunked",
    )(h, toep, win, wout, lamq)


def _s5c_params(lam_re, lam_im, log_step, b_re, b_im, c_re, c_im):
    hp = lax.Precision.HIGHEST
    q = S5_Q
    _, g, p = lam_re.shape
    ni = b_re.shape[-1]
    step = jnp.exp(log_step)[..., None]
    ar, ai = lam_re * step, lam_im * step
    tau = jnp.arange(q + 1, dtype=F32)[:, None, None, None]
    mag = jnp.exp(tau * ar)
    pr, pi = mag * jnp.cos(tau * ai), mag * jnp.sin(tau * ai)
    den = lam_re * lam_re + lam_im * lam_im
    f_re = ((pr[1] - 1.0) * lam_re + pi[1] * lam_im) / den
    f_im = (pi[1] * lam_re - (pr[1] - 1.0) * lam_im) / den
    bb_re = f_re[..., None] * b_re - f_im[..., None] * b_im
    bb_im = f_re[..., None] * b_im + f_im[..., None] * b_re
    cp_re = c_re[None] * pr[:, :, :, None, :] - c_im[None] * pi[:, :, :, None, :]
    cp_im = c_re[None] * pi[:, :, :, None, :] + c_im[None] * pr[:, :, :, None, :]
    taps = (jnp.einsum("tdgop,dgpi->tdgoi", cp_re[:q], bb_re, precision=hp)
            - jnp.einsum("tdgop,dgpi->tdgoi", cp_im[:q], bb_im, precision=hp))
    s_idx = jnp.arange(q)[:, None]
    t_idx = jnp.arange(q)[None, :]

    def toeplitz(dr):
        lag = (t_idx - s_idx) if dr == 0 else (s_idx - t_idx)
        k = taps[:, dr][jnp.clip(lag, 0, q - 1)]
        k = jnp.where((lag >= 0)[:, :, None, None, None], k, 0.0)
        return k.transpose(2, 0, 4, 1, 3).reshape(g, q * ni, q * ni)

    toep = jnp.stack([toeplitz(0), toeplitz(1)]).astype(BF16)

    def state_in(dr):
        e = (q - 1 - jnp.arange(q)) if dr == 0 else jnp.arange(q)
        er, ei = pr[e, dr], pi[e, dr]
        br, bi = bb_re[dr].transpose(0, 2, 1), bb_im[dr].transpose(0, 2, 1)
        w_re = er[:, :, None, :] * br[None] - ei[:, :, None, :] * bi[None]
        w_im = er[:, :, None, :] * bi[None] + ei[:, :, None, :] * br[None]
        fl = lambda a: a.transpose(1, 0, 2, 3).reshape(g, q * ni, p)
        return fl(w_re), fl(w_im)

    def state_out(dr):
        f = (jnp.arange(q) + 1) if dr == 0 else (q - jnp.arange(q))
        fl = lambda a: a.transpose(1, 3, 0, 2).reshape(g, p, q * ni)
        return fl(cp_re[f, dr]), fl(-cp_im[f, dr])

    z = lambda *shape: jnp.zeros(shape, F32)

    def pair_in(dr):
        w_re, w_im = state_in(dr)
        a_re, b_re_, a_im, b_im_ = w_re[0::2], w_re[1::2], w_im[0::2], w_im[1::2]
        zz = z(g // 2, q * ni, p)
        top = jnp.concatenate([a_re, zz, a_im, zz], axis=-1)
        bot = jnp.concatenate([zz, b_re_, zz, b_im_], axis=-1)
        return jnp.concatenate([top, bot], axis=1)

    def pair_out(dr):
        w_re, w_im = state_out(dr)
        zz = z(g // 2, p, q * ni)
        rows = [jnp.concatenate([w_re[0::2], zz], axis=-1), jnp.concatenate([zz, w_re[1::2]], axis=-1),
                jnp.concatenate([w_im[0::2], zz], axis=-1), jnp.concatenate([zz, w_im[1::2]], axis=-1)]
        return jnp.concatenate(rows, axis=1)

    win = jnp.stack([pair_in(0), pair_in(1)]).astype(BF16)
    wout = jnp.stack([pair_out(0), pair_out(1)]).astype(BF16)
    ng = S5_STEP_GROUPS
    lamq = jnp.stack([pr[q].reshape(2, g // ng, ng * p), pi[q].reshape(2, g // ng, ng * p)], axis=2)
    return toep, win, wout, lamq


def _rope_tables(n_tokens):
    rows = n_tokens // GRID_W
    row = jnp.repeat(jnp.arange(rows), GRID_W).astype(F32)
    col = jnp.tile(jnp.arange(GRID_W), rows).astype(F32)
    n_freq = ROPE_DIM // 4
    inv_freq = ROPE_BASE ** (-jnp.arange(n_freq, dtype=F32) / n_freq)
    ang = jnp.concatenate([row[:, None] * inv_freq, col[:, None] * inv_freq], axis=-1)
    cos, sin = jnp.cos(ang), jnp.sin(ang)
    z = jnp.zeros((n_tokens, 128 - ROPE_DIM), F32)
    return (jnp.concatenate([cos, cos, z], axis=-1), jnp.concatenate([-sin, sin, z], axis=-1))


def _router_halves(w_router):
    wt = w_router.T
    hi = wt.astype(BF16)
    lo = (wt - hi.astype(F32)).astype(BF16)
    return jnp.concatenate([hi, lo], axis=0)


def _split_pairs(w):
    ev, od = w[..., 0::2], w[..., 1::2]
    return jnp.concatenate([ev, od], axis=-1), jnp.concatenate([od, ev], axis=-1)


def _mla_weights(w_dqkv, w_uq, w_ukv):
    kp, kps = _split_pairs(w_dqkv[:, Q_LORA + KV_LORA:])
    wd = jnp.concatenate([w_dqkv[:, :Q_LORA + KV_LORA], kp, kps], axis=-1).astype(BF16)
    wq3 = w_uq.reshape(Q_LORA, MLA_HEADS, NOPE_DIM + ROPE_DIM)
    qp, qps = _split_pairs(wq3[:, :, NOPE_DIM:])
    wq = jnp.concatenate([wq3[:, :, :NOPE_DIM].reshape(Q_LORA, -1), qp.reshape(Q_LORA, -1),
                          qps.reshape(Q_LORA, -1)], axis=-1).astype(BF16)
    wkv3 = w_ukv.reshape(KV_LORA, MLA_HEADS, NOPE_DIM + V_DIM)
    wkv = jnp.concatenate([wkv3[:, :, :NOPE_DIM].reshape(KV_LORA, -1),
                           wkv3[:, :, NOPE_DIM:].reshape(KV_LORA, -1)], axis=-1).astype(BF16)
    return wd, wq, wkv


@jax.jit
def kernel(x, c, ctx, c_ctx, ada_w, ada_b, norm_g, mla_w_dqkv, mla_g_q, mla_g_kv, mla_w_uq, mla_w_ukv, mla_w_o, s5_lam_re, s5_lam_im, s5_log_step, s5_b_re, s5_b_im, s5_c_re, s5_c_im, s5_d, s5_w_glu, s5_b_glu, moe_w_router, moe_bias, moe_w_gate, moe_w_up, moe_w_down, sh_w_gate, sh_w_up, sh_w_down):
    b, l, d = x.shape
    n_ctx = ctx.shape[1]
    assert ada_w.shape[0] == 2 and b % 8 == 0
    ta = 256
    tm = 512
    tb = 512
    row = lambda v: v.reshape(1, -1)

    n_rows = (b + 1 + 7) // 8 * 8
    cvec = jnp.zeros((n_rows, d), F32).at[:b].set(c).at[b].set(c_ctx)
    mods = _ada_mods(cvec, ada_w, ada_b)

    def shared_weights(i):
        shgu = jnp.concatenate([sh_w_gate[i], sh_w_up[i]], axis=-1).astype(BF16)
        return shgu, sh_w_down[i].astype(BF16)

    mod_lat = mods[0, :b].reshape(b, 1, N_MOD * d)
    mod_ctx = mods[0, b].reshape(1, 1, N_MOD * d)
    wd, wq, wkv = _mla_weights(mla_w_dqkv[0], mla_w_uq[0], mla_w_ukv[0])
    cos_l, sin_l = _rope_tables(l)
    cos_c = jnp.concatenate([jnp.ones((n_ctx, ROPE_DIM), F32), jnp.zeros((n_ctx, 128 - ROPE_DIM), F32)], -1)
    sin_c = jnp.zeros((n_ctx, 128), F32)
    pre = functools.partial(_pre_mla, g0=row(norm_g[0, 0]), wd=wd, gq=row(mla_g_q[0]), gkv=row(mla_g_kv[0]),
                            wq=wq, wkv=wkv, tm=ta)
    q_c, k_c, v_c = pre(ctx, mod_ctx, cos_t=cos_c, sin_t=sin_c)
    q_l, k_l, v_l = pre(x, mod_lat, cos_t=cos_l, sin_t=sin_l)
    o_l = _attention(q_l, [k_c, k_l], [v_c, v_l], 2 * ta)
    o_c = _attention(q_c, [k_c], [v_c], n_ctx)

    wo = mla_w_o[0].astype(BF16)
    wr_t = _router_halves(moe_w_router[0])
    g1, g2, g3 = row(norm_g[0, 1]), row(norm_g[0, 2]), row(norm_g[0, 3])
    post = functools.partial(_post_mixer, _post_proj_kernel, consts=[wo], g1=g1, g2=g2, wr_t=wr_t, tm=tm,
                             name="post_mla")
    o_spec = pl.BlockSpec((tm, o_l.shape[-1]), lambda i: (i, 0))
    n_moe = b * (n_ctx + l)
    x1_c, fin, lg = post([(o_c.reshape(b * n_ctx, -1), o_spec)], x=ctx.reshape(b * n_ctx, d),
                         n_tok=b * n_ctx, x_off=0, mods=mod_ctx, rows_per_mod=b * n_ctx, moe_total=n_moe)
    tt = tm // b
    mod_lat_tm = mods[0, :b][None]
    x1_l, fin, lg = _post_mixer(
        _post_proj_tm_kernel, [(o_l, pl.BlockSpec((b, tt, o_l.shape[-1]), lambda i: (0, i, 0)))], [wo],
        x=x, n_tok=b * l, x_off=0, mods=mod_lat_tm, g1=g1, g2=g2, wr_t=wr_t, tm=tm, rows_per_mod=b * l,
        name="post_mla", moe_total=n_moe, moe_off=b * n_ctx, prev=(fin, lg),
        x_spec=pl.BlockSpec((b, tt, d), lambda i: (0, i, 0)))
    shgu, shd = shared_weights(0)
    lat_a = (l // 2) * b
    n_a = b * n_ctx + lat_a
    moe = functools.partial(_moe, fin, lg, moe_bias[0], moe_w_gate, moe_w_up, moe_w_down, 0, tb)
    yk_a, gates_a = moe(0, n_a)
    yk_b, gates_b = moe(n_a, n_moe - n_a)
    comb = functools.partial(_combine, fin=fin, shgu=shgu, shd=shd, g3=g3, tm=tm)
    x2_c = comb(yk_a, gates_a, x1=x1_c, mods=mod_ctx, rows_per_mod=b * n_ctx, n_tok=b * n_ctx, x_off=0, yk_off=0,
                fin_off=0)

    n_all = n_ctx + l
    mod_lat = mods[1, :b]
    mod_ctx = jnp.broadcast_to(mods[1, b][None], (b, N_MOD * d))
    g0 = row(norm_g[1, 0])
    h, xt = _pre_s5(x2_c.reshape(b, n_ctx, d), mod_ctx, g0, n_all, 0, None, tt)
    comb_l = functools.partial(comb, x1=x1_l, mods=mod_lat_tm, rows_per_mod=b * l, out_rows=n_all * b,
                               nxt=(mod_lat, g0))
    xt, h = comb_l(yk_a, gates_a, n_tok=lat_a, x_off=0, yk_off=b * n_ctx, fin_off=b * n_ctx,
                   out_off=n_ctx * b, prev=(xt, h))
    xt, h = comb_l(yk_b, gates_b, n_tok=b * l - lat_a, x_off=lat_a, yk_off=0, fin_off=n_a,
                   out_off=n_ctx * b + lat_a, prev=(xt, h))
    toep, win, wout, lamq = _s5c_params(s5_lam_re[0], s5_lam_im[0], s5_log_step[0], s5_b_re[0], s5_b_im[0],
                                        s5_c_re[0], s5_c_im[0])
    ng = d // LANES
    yc = _s5_chunked(h, toep, win, wout, lamq, b, n_ctx)
    g1, g2, g3 = row(norm_g[1, 1]), row(norm_g[1, 2]), row(norm_g[1, 3])
    lat0 = n_ctx * b // tm
    x1, fin, lg = _post_mixer(
        _post_glu_kernel,
        [(h, pl.BlockSpec((ng, tm, LANES), lambda i: (0, i + lat0, 0))),
         (yc, pl.BlockSpec((yc.shape[0], tm // S5_Q, yc.shape[-1]), lambda i: (0, i, 0)))],
        [row(s5_d[0]), s5_w_glu[0].astype(BF16), row(s5_b_glu[0])],
        x=xt, n_tok=l * b, x_off=n_ctx * b, mods=mod_lat[None], g1=g1, g2=g2, wr_t=_router_halves(moe_w_router[1]), tm=tm,
        rows_per_mod=l * b, name="post_s5")
    shgu, shd = shared_weights(1)
    n_h = (l // 2) * b
    moe = functools.partial(_moe, fin, lg, moe_bias[1], moe_w_gate, moe_w_up, moe_w_down, 1, tb)
    yk_a, gates_a = moe(0, n_h)
    yk_b, gates_b = moe(n_h, l * b - n_h)
    comb = functools.partial(_combine, fin=fin, shgu=shgu, shd=shd, x1=x1, mods=mod_lat[None], g3=g3, tm=tm,
                             rows_per_mod=l * b, yk_off=0, batch_out=b)
    out = comb(yk_a, gates_a, n_tok=n_h, x_off=0, fin_off=0)
    return comb(yk_b, gates_b, n_tok=l * b - n_h, x_off=n_h, fin_off=n_h, prev=out)
```
